```python
import math
import jax, jax.numpy as jnp
from jax import lax
import numpy as np

D_MODEL = 1024
BATCH = 8
SEQ = 4096
DEPTH = 4

ATTN_GROUPS = ((128, 1), (512, 4), (2048, 16))
N_ATTN_GROUPS = len(ATTN_GROUPS)
HEADS_PER_GROUP = 6
ATTN_HEADS = N_ATTN_GROUPS * HEADS_PER_GROUP
HEAD_DIM = 64
ATTN_WIDTH = ATTN_HEADS * HEAD_DIM
ATTN_OUT_WIDTH = HEADS_PER_GROUP * HEAD_DIM
REL_BUCKETS = 32
REL_MAX_DISTANCE = 2048
POOL_WINDOWS = (2, 4, 8, 16)
POOL_WIDTH = D_MODEL
POOL_GROUP = POOL_WIDTH // len(POOL_WINDOWS)
SSD_INNER = D_MODEL
SSD_HEAD_DIM = 64
SSD_HEADS = SSD_INNER // SSD_HEAD_DIM
SSD_GROUPS = 2
SSD_STATE = 128
SSD_CONV = 4
SSD_CHUNK = 128
SSD_XBC = SSD_INNER + 2 * SSD_GROUPS * SSD_STATE
D_FF = 2816
FFN_CONV = 3
N_BRANCH = 3
NORM_EPS = 1e-6
IN_SPLIT_SIZES = (ATTN_WIDTH, ATTN_WIDTH, ATTN_WIDTH, POOL_WIDTH, SSD_INNER, SSD_XBC, SSD_HEADS, N_BRANCH * D_MODEL)
IN_WIDTH = sum(IN_SPLIT_SIZES)

kernel_name = "hybrid_gated_dilattn_pool_ssd_trunk"


def rmsnorm(x, g):
    xf = x.astype(jnp.float32)
    y = xf * lax.rsqrt(jnp.mean(xf * xf, axis=-1, keepdims=True) + NORM_EPS)
    return (y * g.astype(jnp.float32)).astype(x.dtype)


def causal_dwconv(x, w, b):
    K = w.shape[0]
    s = x.shape[1]
    xp = jnp.pad(x, ((0, 0), (K - 1, 0), (0, 0)))
    y = xp[:, 0:s] * w[0]
    for k in range(1, K):
        y = y + xp[:, k:k + s] * w[k]
    return y + b


def t5_bucket(dist):
    max_exact = REL_BUCKETS // 2
    is_small = dist < max_exact
    nf = jnp.maximum(dist, 1).astype(jnp.float32)
    large = max_exact + (jnp.log(nf / max_exact) / math.log(REL_MAX_DISTANCE / max_exact)
                         * (REL_BUCKETS - max_exact)).astype(jnp.int32)
    large = jnp.minimum(large, REL_BUCKETS - 1)
    return jnp.where(is_small, dist, large)


def dilated_group_attention(q, k, v, bias_table, dilation, steps):
    b, s, h, dh = q.shape
    L = s // dilation
    W = steps
    nb = -(-L // W)
    Lp = nb * W
    bd = b * dilation

    def to_res(t):
        return t.reshape(b, L, dilation, h, dh).transpose(0, 2, 3, 1, 4).reshape(bd, h, L, dh)

    qb = jnp.pad(to_res(q), ((0, 0), (0, 0), (0, Lp - L), (0, 0))).reshape(bd, h, nb, W, dh)
    kr = jnp.pad(to_res(k), ((0, 0), (0, 0), (W, Lp - L), (0, 0))).reshape(bd, h, nb + 1, W, dh)
    vr = jnp.pad(to_res(v), ((0, 0), (0, 0), (W, Lp - L), (0, 0))).reshape(bd, h, nb + 1, W, dh)
    kb = jnp.concatenate([kr[:, :, :-1], kr[:, :, 1:]], axis=3)
    vb = jnp.concatenate([vr[:, :, :-1], vr[:, :, 1:]], axis=3)

    qi = jnp.arange(W)[:, None]
    kk = jnp.arange(2 * W)[None, :]
    rel = qi + W - kk
    band = (rel >= 0) & (rel <= W)
    kabs = jnp.arange(nb)[:, None, None] * W + kk[None] - W
    valid = band[None] & (kabs >= 0)
    bucket = t5_bucket(jnp.clip(rel, 0, None) * dilation)
    bias = bias_table[bucket].transpose(2, 0, 1).astype(jnp.float32)

    scale = 1.0 / math.sqrt(dh)
    scores = jnp.einsum("bhnqd,bhnkd->bhnqk", qb, kb).astype(jnp.float32) * scale + bias[:, None]
    scores = jnp.where(valid[None, None], scores, -jnp.inf)
    m = jnp.max(scores, axis=-1, keepdims=True)
    p = jnp.exp(scores - m)
    l = jnp.sum(p, axis=-1)
    o = jnp.einsum("bhnqk,bhnkd->bhnqd", p, vb.astype(jnp.float32)) / l[..., None]
    lse = m[..., 0] + jnp.log(l)

    o = o.reshape(bd, h, Lp, dh)[:, :, :L].reshape(b, dilation, h, L, dh)
    o = o.transpose(0, 3, 1, 2, 4).reshape(b, s, h, dh)
    lse = lse.reshape(bd, h, Lp)[:, :, :L].reshape(b, dilation, h, L)
    lse = lse.transpose(0, 3, 1, 2).reshape(b, s, h)
    return o, lse


def dilated_attention_mixer(q, k, v, rel_bias):
    b, s = q.shape[:2]
    outs, lses = [], []
    for gi, (window, dilation) in enumerate(ATTN_GROUPS):
        hs = slice(gi * HEADS_PER_GROUP, (gi + 1) * HEADS_PER_GROUP)
        o, lse = dilated_group_attention(q[:, :, hs], k[:, :, hs], v[:, :, hs],
                                         rel_bias[:, hs], dilation, window // dilation)
        outs.append(o)
        lses.append(lse)
    o = jnp.stack(outs, axis=0)
    alpha = jax.nn.softmax(jnp.stack(lses, axis=0), axis=0)
    o = jnp.sum(alpha[..., None] * o, axis=0)
    return o.reshape(b, s, ATTN_OUT_WIDTH).astype(q.dtype)


def pool_mixer(u, w_grp, scale):
    b, s, _ = u.shape
    uf = u.astype(jnp.float32)
    cs = jnp.cumsum(uf, axis=1)
    pos = (jnp.arange(s) + 1)[None, :, None]
    outs = []
    for gi, w in enumerate(POOL_WINDOWS):
        sl = slice(gi * POOL_GROUP, (gi + 1) * POOL_GROUP)
        c = cs[..., sl]
        shifted = jnp.pad(c, ((0, 0), (w, 0), (0, 0)))[:, :s]
        cnt = jnp.minimum(pos, w).astype(jnp.float32)
        outs.append((c - shifted) / cnt - uf[..., sl])
    d = jnp.stack(outs, axis=2).astype(u.dtype)
    y = jnp.einsum("bsgc,gcd->bsgd", d, w_grp).reshape(b, s, POOL_WIDTH)
    return y * scale


def ssd_scan(x, dt, A, Bm, Cm):
    b, s, h, p = x.shape
    g, n = Bm.shape[2:]
    e = h // g
    l = SSD_CHUNK
    c = s // l
    xc = (x * dt[..., None]).reshape(b, c, l, g, e, p)
    a = (dt * A).reshape(b, c, l, h).transpose(0, 3, 1, 2)
    a_cs = jnp.cumsum(a, axis=-1)
    Bc = Bm.reshape(b, c, l, g, n)
    Cc = Cm.reshape(b, c, l, g, n)
    causal = jnp.tril(jnp.ones((l, l), dtype=bool))
    seg = a_cs[..., :, None] - a_cs[..., None, :]
    Lmat = jnp.exp(jnp.where(causal, seg, -jnp.inf)).reshape(b, g, e, c, l, l)
    cb = jnp.einsum("bclgn,bcsgn->bcgls", Cc, Bc)
    y_diag = jnp.einsum("bcgls,bgecls,bcsgep->bclgep", cb, Lmat, xc)
    decay = jnp.exp(a_cs[..., -1:] - a_cs).reshape(b, g, e, c, l)
    states = jnp.einsum("bclgn,bgecl,bclgep->bcgepn", Bc, decay, xc)
    chunk_decay = jnp.exp(a_cs[..., -1]).reshape(b, g, e, c)

    def step(carry, inp):
        st, dec = inp
        return carry * dec[..., None, None] + st, carry

    init = jnp.zeros((b, g, e, p, n), dtype=x.dtype)
    _, prev = lax.scan(step, init, (states.transpose(1, 0, 2, 3, 4, 5), chunk_decay.transpose(3, 0, 1, 2)))
    prev = prev.transpose(1, 0, 2, 3, 4, 5)
    out_decay = jnp.exp(a_cs).reshape(b, g, e, c, l)
    y_off = jnp.einsum("bclgn,bcgepn,bgecl->bclgep", Cc, prev, out_decay)
    return (y_diag + y_off).reshape(b, s, h, p)


def ssd_mixer(z, xbc, dt_raw, conv_w, conv_b, dt_bias, a_log, d_skip, norm_w):
    b, s, _ = z.shape
    xbc = jax.nn.silu(causal_dwconv(xbc, conv_w, conv_b))
    xs = xbc[..., :SSD_INNER].reshape(b, s, SSD_HEADS, SSD_HEAD_DIM).astype(jnp.float32)
    Bm = xbc[..., SSD_INNER:SSD_INNER + SSD_GROUPS * SSD_STATE].reshape(b, s, SSD_GROUPS, SSD_STATE).astype(jnp.float32)
    Cm = xbc[..., SSD_INNER + SSD_GROUPS * SSD_STATE:].reshape(b, s, SSD_GROUPS, SSD_STATE).astype(jnp.float32)
    dt = jax.nn.softplus(dt_raw.astype(jnp.float32) + dt_bias.astype(jnp.float32))
    A = -jnp.exp(a_log.astype(jnp.float32))
    y = ssd_scan(xs, dt, A, Bm, Cm) + d_skip.astype(jnp.float32)[:, None] * xs
    y = y.reshape(b, s, SSD_INNER) * jax.nn.silu(z.astype(jnp.float32))
    yg = y.reshape(b, s, SSD_GROUPS, SSD_INNER // SSD_GROUPS)
    yg = yg * lax.rsqrt(jnp.mean(yg * yg, axis=-1, keepdims=True) + NORM_EPS)
    y = yg.reshape(b, s, SSD_INNER) * norm_w.astype(jnp.float32)
    return y.astype(z.dtype)


def conv_ffn(u, w_up, conv_w, conv_b, w_down):
    h = causal_dwconv(u @ w_up, conv_w, conv_b)
    a, v = h[..., :D_FF], h[..., D_FF:]
    return (jax.nn.silu(a) * v) @ w_down


def _fwd_setup_inputs(seed: int = 0) -> dict:
    key = jax.random.key(seed)
    ks = jax.random.split(key, 24)
    f32 = jnp.float32

    def nrm(k, shape, scale):
        return jax.random.normal(k, shape, dtype=f32) * scale

    dt0 = jnp.exp(jax.random.uniform(ks[10], (DEPTH, SSD_HEADS), dtype=f32)
                  * (math.log(0.1) - math.log(0.001)) + math.log(0.001))
    return {
        "x": nrm(ks[0], (BATCH, SEQ, D_MODEL), 1.0),
        "rel_bias": nrm(ks[1], (REL_BUCKETS, ATTN_HEADS), 0.1),
        "ln1_g": 1.0 + nrm(ks[2], (DEPTH, D_MODEL), 0.02),
        "w_in": nrm(ks[3], (DEPTH, D_MODEL, IN_WIDTH), D_MODEL ** -0.5),
        "b_gate": nrm(ks[4], (DEPTH, N_BRANCH * D_MODEL), 0.02),
        "w_a": nrm(ks[5], (DEPTH, ATTN_OUT_WIDTH, D_MODEL), ATTN_OUT_WIDTH ** -0.5),
        "pool_w": nrm(ks[6], (DEPTH, len(POOL_WINDOWS), POOL_GROUP, POOL_GROUP), POOL_GROUP ** -0.5),
        "pool_scale": 1.0 + nrm(ks[7], (DEPTH, POOL_WIDTH), 0.1),
        "w_b": nrm(ks[8], (DEPTH, POOL_WIDTH, D_MODEL), POOL_WIDTH ** -0.5),
        "ssd_conv_w": nrm(ks[9], (DEPTH, SSD_CONV, SSD_XBC), SSD_CONV ** -0.5),
        "ssd_conv_b": nrm(ks[11], (DEPTH, SSD_XBC), 0.02),
        "ssd_dt_bias": dt0 + jnp.log(-jnp.expm1(-dt0)),
        "ssd_a_log": jnp.log(jax.random.uniform(ks[12], (DEPTH, SSD_HEADS), dtype=f32, minval=1.0, maxval=16.0)),
        "ssd_d": 1.0 + nrm(ks[13], (DEPTH, SSD_HEADS), 0.1),
        "ssd_norm_w": 1.0 + nrm(ks[14], (DEPTH, SSD_INNER), 0.02),
        "w_c": nrm(ks[15], (DEPTH, SSD_INNER, D_MODEL), SSD_INNER ** -0.5),
        "w_o": nrm(ks[16], (DEPTH, D_MODEL, D_MODEL), D_MODEL ** -0.5),
        "ln2_g": 1.0 + nrm(ks[17], (DEPTH, D_MODEL), 0.02),
        "ffn_w_up": nrm(ks[18], (DEPTH, D_MODEL, 2 * D_FF), D_MODEL ** -0.5),
        "ffn_conv_w": nrm(ks[19], (DEPTH, FFN_CONV, 2 * D_FF), FFN_CONV ** -0.5),
        "ffn_conv_b": nrm(ks[20], (DEPTH, 2 * D_FF), 0.02),
        "ffn_w_down": nrm(ks[21], (DEPTH, D_FF, D_MODEL), D_FF ** -0.5),
        "final_g": 1.0 + nrm(ks[22], (D_MODEL,), 0.02),
    }


def _fwd_reference(x, rel_bias, ln1_g, w_in, b_gate, w_a, pool_w, pool_scale, w_b,
              ssd_conv_w, ssd_conv_b, ssd_dt_bias, ssd_a_log, ssd_d, ssd_norm_w, w_c,
              w_o, ln2_g, ffn_w_up, ffn_conv_w, ffn_conv_b, ffn_w_down, final_g):
    b, s, _ = x.shape
    split_idx = []
    acc = 0
    for sz in IN_SPLIT_SIZES[:-1]:
        acc += sz
        split_idx.append(acc)
    for i in range(DEPTH):
        u = rmsnorm(x, ln1_g[i])
        proj = u @ w_in[i]
        q, k, v, pool_in, z, xbc, dt_raw, gate_pre = jnp.split(proj, split_idx, axis=-1)
        q = q.reshape(b, s, ATTN_HEADS, HEAD_DIM)
        k = k.reshape(b, s, ATTN_HEADS, HEAD_DIM)
        v = v.reshape(b, s, ATTN_HEADS, HEAD_DIM)
        y_a = dilated_attention_mixer(q, k, v, rel_bias) @ w_a[i]
        y_b = pool_mixer(pool_in, pool_w[i], pool_scale[i]) @ w_b[i]
        y_c = ssd_mixer(z, xbc, dt_raw, ssd_conv_w[i], ssd_conv_b[i], ssd_dt_bias[i],
                        ssd_a_log[i], ssd_d[i], ssd_norm_w[i]) @ w_c[i]
        gates = jax.nn.sigmoid(gate_pre + b_gate[i]).reshape(b, s, N_BRANCH, D_MODEL)
        merged = gates[:, :, 0] * y_a + gates[:, :, 1] * y_b + gates[:, :, 2] * y_c
        x = x + merged @ w_o[i]
        x = x + conv_ffn(rmsnorm(x, ln2_g[i]), ffn_w_up[i], ffn_conv_w[i], ffn_conv_b[i], ffn_w_down[i])
    return rmsnorm(x, final_g)


import jax as _jax
import jax.numpy as _jnp

TWIN_FORMAT = 'train_step'
FWD_PARAMS = ['x', 'rel_bias', 'ln1_g', 'w_in', 'b_gate', 'w_a', 'pool_w', 'pool_scale', 'w_b', 'ssd_conv_w', 'ssd_conv_b', 'ssd_dt_bias', 'ssd_a_log', 'ssd_d', 'ssd_norm_w', 'w_c', 'w_o', 'ln2_g', 'ffn_w_up', 'ffn_conv_w', 'ffn_conv_b', 'ffn_w_down', 'final_g']
TWIN_WEIGHTS = ['rel_bias', 'ln1_g', 'w_in', 'b_gate', 'w_a', 'pool_w', 'pool_scale', 'w_b', 'ssd_conv_w', 'ssd_conv_b', 'ssd_dt_bias', 'ssd_a_log', 'ssd_d', 'ssd_norm_w', 'w_c', 'w_o', 'ln2_g', 'ffn_w_up', 'ffn_conv_w', 'ffn_conv_b', 'ffn_w_down', 'final_g']
TWIN_DIFF_INPUT = 'x'
TWIN_INPUTS = ['x', 'rel_bias', 'ln1_g', 'w_in', 'b_gate', 'w_a', 'pool_w', 'pool_scale', 'w_b', 'ssd_conv_w', 'ssd_conv_b', 'ssd_dt_bias', 'ssd_a_log', 'ssd_d', 'ssd_norm_w', 'w_c', 'w_o', 'ln2_g', 'ffn_w_up', 'ffn_conv_w', 'ffn_conv_b', 'ffn_w_down', 'final_g', 'loss_target', 'm_rel_bias', 'm_ln1_g', 'm_w_in', 'm_b_gate', 'm_w_a', 'm_pool_w', 'm_pool_scale', 'm_w_b', 'm_ssd_conv_w', 'm_ssd_conv_b', 'm_ssd_dt_bias', 'm_ssd_a_log', 'm_ssd_d', 'm_ssd_norm_w', 'm_w_c', 'm_w_o', 'm_ln2_g', 'm_ffn_w_up', 'm_ffn_conv_w', 'm_ffn_conv_b', 'm_ffn_w_down', 'm_final_g', 'v_rel_bias', 'v_ln1_g', 'v_w_in', 'v_b_gate', 'v_w_a', 'v_pool_w', 'v_pool_scale', 'v_w_b', 'v_ssd_conv_w', 'v_ssd_conv_b', 'v_ssd_dt_bias', 'v_ssd_a_log', 'v_ssd_d', 'v_ssd_norm_w', 'v_w_c', 'v_w_o', 'v_ln2_g', 'v_ffn_w_up', 'v_ffn_conv_w', 'v_ffn_conv_b', 'v_ffn_w_down', 'v_final_g']
TWIN_OUTPUTS = ['loss', 'grad_x', 'grad_rel_bias', 'grad_ln1_g', 'grad_w_in', 'grad_b_gate', 'grad_w_a', 'grad_pool_w', 'grad_pool_scale', 'grad_w_b', 'grad_ssd_conv_w', 'grad_ssd_conv_b', 'grad_ssd_dt_bias', 'grad_ssd_a_log', 'grad_ssd_d', 'grad_ssd_norm_w', 'grad_w_c', 'grad_w_o', 'grad_ln2_g', 'grad_ffn_w_up', 'grad_ffn_conv_w', 'grad_ffn_conv_b', 'grad_ffn_w_down', 'grad_final_g', 'delta_rel_bias', 'delta_ln1_g', 'delta_w_in', 'delta_b_gate', 'delta_w_a', 'delta_pool_w', 'delta_pool_scale', 'delta_w_b', 'delta_ssd_conv_w', 'delta_ssd_conv_b', 'delta_ssd_dt_bias', 'delta_ssd_a_log', 'delta_ssd_d', 'delta_ssd_norm_w', 'delta_w_c', 'delta_w_o', 'delta_ln2_g', 'delta_ffn_w_up', 'delta_ffn_conv_w', 'delta_ffn_conv_b', 'delta_ffn_w_down', 'delta_final_g', 'new_m_rel_bias', 'new_m_ln1_g', 'new_m_w_in', 'new_m_b_gate', 'new_m_w_a', 'new_m_pool_w', 'new_m_pool_scale', 'new_m_w_b', 'new_m_ssd_conv_w', 'new_m_ssd_conv_b', 'new_m_ssd_dt_bias', 'new_m_ssd_a_log', 'new_m_ssd_d', 'new_m_ssd_norm_w', 'new_m_w_c', 'new_m_w_o', 'new_m_ln2_g', 'new_m_ffn_w_up', 'new_m_ffn_conv_w', 'new_m_ffn_conv_b', 'new_m_ffn_w_down', 'new_m_final_g', 'new_v_rel_bias', 'new_v_ln1_g', 'new_v_w_in', 'new_v_b_gate', 'new_v_w_a', 'new_v_pool_w', 'new_v_pool_scale', 'new_v_w_b', 'new_v_ssd_conv_w', 'new_v_ssd_conv_b', 'new_v_ssd_dt_bias', 'new_v_ssd_a_log', 'new_v_ssd_d', 'new_v_ssd_norm_w', 'new_v_w_c', 'new_v_w_o', 'new_v_ln2_g', 'new_v_ffn_w_up', 'new_v_ffn_conv_w', 'new_v_ffn_conv_b', 'new_v_ffn_w_down', 'new_v_final_g']
TWIN_LEAF_KINDS = {'loss': 'loss', 'grad_x': 'grad_x', 'grad_rel_bias': 'grad_w', 'grad_ln1_g': 'grad_w', 'grad_w_in': 'grad_w', 'grad_b_gate': 'grad_w', 'grad_w_a': 'grad_w', 'grad_pool_w': 'grad_w', 'grad_pool_scale': 'grad_w', 'grad_w_b': 'grad_w', 'grad_ssd_conv_w': 'grad_w', 'grad_ssd_conv_b': 'grad_w', 'grad_ssd_dt_bias': 'grad_w', 'grad_ssd_a_log': 'grad_w', 'grad_ssd_d': 'grad_w', 'grad_ssd_norm_w': 'grad_w', 'grad_w_c': 'grad_w', 'grad_w_o': 'grad_w', 'grad_ln2_g': 'grad_w', 'grad_ffn_w_up': 'grad_w', 'grad_ffn_conv_w': 'grad_w', 'grad_ffn_conv_b': 'grad_w', 'grad_ffn_w_down': 'grad_w', 'grad_final_g': 'grad_w', 'delta_rel_bias': 'delta_w', 'delta_ln1_g': 'delta_w', 'delta_w_in': 'delta_w', 'delta_b_gate': 'delta_w', 'delta_w_a': 'delta_w', 'delta_pool_w': 'delta_w', 'delta_pool_scale': 'delta_w', 'delta_w_b': 'delta_w', 'delta_ssd_conv_w': 'delta_w', 'delta_ssd_conv_b': 'delta_w', 'delta_ssd_dt_bias': 'delta_w', 'delta_ssd_a_log': 'delta_w', 'delta_ssd_d': 'delta_w', 'delta_ssd_norm_w': 'delta_w', 'delta_w_c': 'delta_w', 'delta_w_o': 'delta_w', 'delta_ln2_g': 'delta_w', 'delta_ffn_w_up': 'delta_w', 'delta_ffn_conv_w': 'delta_w', 'delta_ffn_conv_b': 'delta_w', 'delta_ffn_w_down': 'delta_w', 'delta_final_g': 'delta_w', 'new_m_rel_bias': 'new_m', 'new_m_ln1_g': 'new_m', 'new_m_w_in': 'new_m', 'new_m_b_gate': 'new_m', 'new_m_w_a': 'new_m', 'new_m_pool_w': 'new_m', 'new_m_pool_scale': 'new_m', 'new_m_w_b': 'new_m', 'new_m_ssd_conv_w': 'new_m', 'new_m_ssd_conv_b': 'new_m', 'new_m_ssd_dt_bias': 'new_m', 'new_m_ssd_a_log': 'new_m', 'new_m_ssd_d': 'new_m', 'new_m_ssd_norm_w': 'new_m', 'new_m_w_c': 'new_m', 'new_m_w_o': 'new_m', 'new_m_ln2_g': 'new_m', 'new_m_ffn_w_up': 'new_m', 'new_m_ffn_conv_w': 'new_m', 'new_m_ffn_conv_b': 'new_m', 'new_m_ffn_w_down': 'new_m', 'new_m_final_g': 'new_m', 'new_v_rel_bias': 'new_v', 'new_v_ln1_g': 'new_v', 'new_v_w_in': 'new_v', 'new_v_b_gate': 'new_v', 'new_v_w_a': 'new_v', 'new_v_pool_w': 'new_v', 'new_v_pool_scale': 'new_v', 'new_v_w_b': 'new_v', 'new_v_ssd_conv_w': 'new_v', 'new_v_ssd_conv_b': 'new_v', 'new_v_ssd_dt_bias': 'new_v', 'new_v_ssd_a_log': 'new_v', 'new_v_ssd_d': 'new_v', 'new_v_ssd_norm_w': 'new_v', 'new_v_w_c': 'new_v', 'new_v_w_o': 'new_v', 'new_v_ln2_g': 'new_v', 'new_v_ffn_w_up': 'new_v', 'new_v_ffn_conv_w': 'new_v', 'new_v_ffn_conv_b': 'new_v', 'new_v_ffn_w_down': 'new_v', 'new_v_final_g': 'new_v'}


def _forward(args):
    return _fwd_reference(*[args[k] for k in FWD_PARAMS])


def _output_shape():
    out = _jax.eval_shape(lambda: _forward(_fwd_setup_inputs(0)))
    return out.shape, out.dtype

N_MICROBATCH = 1
ADAM_LR = 0.001
ADAM_B1 = 0.9
ADAM_B2 = 0.999
ADAM_EPS = 1e-08
ADAM_WD = 0.01
ADAM_STEP = 10
PER_EXAMPLE_BATCH_AXIS = {'x': 0, 'loss_target': 0}
SHARED_INPUTS = []
_WEIGHT_DTYPES = {'rel_bias': _jnp.float32, 'ln1_g': _jnp.float32, 'w_in': _jnp.float32, 'b_gate': _jnp.float32, 'w_a': _jnp.float32, 'pool_w': _jnp.float32, 'pool_scale': _jnp.float32, 'w_b': _jnp.float32, 'ssd_conv_w': _jnp.float32, 'ssd_conv_b': _jnp.float32, 'ssd_dt_bias': _jnp.float32, 'ssd_a_log': _jnp.float32, 'ssd_d': _jnp.float32, 'ssd_norm_w': _jnp.float32, 'w_c': _jnp.float32, 'w_o': _jnp.float32, 'ln2_g': _jnp.float32, 'ffn_w_up': _jnp.float32, 'ffn_conv_w': _jnp.float32, 'ffn_conv_b': _jnp.float32, 'ffn_w_down': _jnp.float32, 'final_g': _jnp.float32}
MOMENT_SCALE = {'rel_bias': 3.611257e-02, 'ln1_g': 1.548339e-01, 'w_in': 4.919013e-02, 'b_gate': 2.535523e-02, 'w_a': 1.677669e-02, 'pool_w': 7.375222e-02, 'pool_scale': 7.223993e-02, 'w_b': 7.365205e-02, 'ssd_conv_w': 7.392070e-02, 'ssd_conv_b': 9.476564e-02, 'ssd_dt_bias': 1.670681e-01, 'ssd_a_log': 2.072830e-01, 'ssd_d': 4.890996e-01, 'ssd_norm_w': 8.676132e-02, 'w_c': 8.363199e-02, 'w_o': 1.121843e-01, 'ln2_g': 1.217992e-01, 'ffn_w_up': 4.913624e-02, 'ffn_conv_w': 4.923282e-02, 'ffn_conv_b': 4.846094e-02, 'ffn_w_down': 8.077995e-02, 'final_g': 3.199267e+01}


def _to_microbatches(a, axis):
    t = _jnp.moveaxis(a, axis, 0)
    t = t.reshape((N_MICROBATCH, t.shape[0] // N_MICROBATCH) + t.shape[1:])
    return _jnp.moveaxis(t, 1, axis + 1)


def setup_inputs(seed: int = 0) -> dict:
    inp = _fwd_setup_inputs(seed)
    key = _jax.random.fold_in(_jax.random.key(seed), 7919)
    shape, _ = _output_shape()
    out = dict(inp)
    out["loss_target"] = _jax.random.normal(_jax.random.fold_in(key, 0), shape, _jnp.float32)
    for i, name in enumerate(TWIN_WEIGHTS):
        w = inp[name].astype(_jnp.float32)
        if MOMENT_SCALE is None:
            s = _jnp.sqrt(_jnp.mean(_jnp.square(w)) + 1e-30)
        else:
            s = MOMENT_SCALE[name]
        km, kv = _jax.random.split(_jax.random.fold_in(key, i + 1))
        out[name] = w
        out["m_" + name] = s * _jax.random.normal(km, w.shape, _jnp.float32)
        out["v_" + name] = (s * s) * _jax.random.uniform(kv, w.shape, _jnp.float32, 0.5, 1.5)
    if N_MICROBATCH > 1:
        for name, axis in PER_EXAMPLE_BATCH_AXIS.items():
            out[name] = _to_microbatches(out[name], axis)
    return {'x': out['x'], 'rel_bias': out['rel_bias'], 'ln1_g': out['ln1_g'], 'w_in': out['w_in'], 'b_gate': out['b_gate'], 'w_a': out['w_a'], 'pool_w': out['pool_w'], 'pool_scale': out['pool_scale'], 'w_b': out['w_b'], 'ssd_conv_w': out['ssd_conv_w'], 'ssd_conv_b': out['ssd_conv_b'], 'ssd_dt_bias': out['ssd_dt_bias'], 'ssd_a_log': out['ssd_a_log'], 'ssd_d': out['ssd_d'], 'ssd_norm_w': out['ssd_norm_w'], 'w_c': out['w_c'], 'w_o': out['w_o'], 'ln2_g': out['ln2_g'], 'ffn_w_up': out['ffn_w_up'], 'ffn_conv_w': out['ffn_conv_w'], 'ffn_conv_b': out['ffn_conv_b'], 'ffn_w_down': out['ffn_w_down'], 'final_g': out['final_g'], 'loss_target': out['loss_target'], 'm_rel_bias': out['m_rel_bias'], 'm_ln1_g': out['m_ln1_g'], 'm_w_in': out['m_w_in'], 'm_b_gate': out['m_b_gate'], 'm_w_a': out['m_w_a'], 'm_pool_w': out['m_pool_w'], 'm_pool_scale': out['m_pool_scale'], 'm_w_b': out['m_w_b'], 'm_ssd_conv_w': out['m_ssd_conv_w'], 'm_ssd_conv_b': out['m_ssd_conv_b'], 'm_ssd_dt_bias': out['m_ssd_dt_bias'], 'm_ssd_a_log': out['m_ssd_a_log'], 'm_ssd_d': out['m_ssd_d'], 'm_ssd_norm_w': out['m_ssd_norm_w'], 'm_w_c': out['m_w_c'], 'm_w_o': out['m_w_o'], 'm_ln2_g': out['m_ln2_g'], 'm_ffn_w_up': out['m_ffn_w_up'], 'm_ffn_conv_w': out['m_ffn_conv_w'], 'm_ffn_conv_b': out['m_ffn_conv_b'], 'm_ffn_w_down': out['m_ffn_w_down'], 'm_final_g': out['m_final_g'], 'v_rel_bias': out['v_rel_bias'], 'v_ln1_g': out['v_ln1_g'], 'v_w_in': out['v_w_in'], 'v_b_gate': out['v_b_gate'], 'v_w_a': out['v_w_a'], 'v_pool_w': out['v_pool_w'], 'v_pool_scale': out['v_pool_scale'], 'v_w_b': out['v_w_b'], 'v_ssd_conv_w': out['v_ssd_conv_w'], 'v_ssd_conv_b': out['v_ssd_conv_b'], 'v_ssd_dt_bias': out['v_ssd_dt_bias'], 'v_ssd_a_log': out['v_ssd_a_log'], 'v_ssd_d': out['v_ssd_d'], 'v_ssd_norm_w': out['v_ssd_norm_w'], 'v_w_c': out['v_w_c'], 'v_w_o': out['v_w_o'], 'v_ln2_g': out['v_ln2_g'], 'v_ffn_w_up': out['v_ffn_w_up'], 'v_ffn_conv_w': out['v_ffn_conv_w'], 'v_ffn_conv_b': out['v_ffn_conv_b'], 'v_ffn_w_down': out['v_ffn_w_down'], 'v_final_g': out['v_final_g']}


def _loss(weights, diff, rest, loss_target):
    with _jax.named_scope("forward"):
        args = {**rest, TWIN_DIFF_INPUT: diff, **{k: w.astype(_WEIGHT_DTYPES[k]) for k, w in weights.items()}}
        y = _forward(args)
    with _jax.named_scope("loss_head"):
        err = _jnp.square(y.astype(_jnp.float32) - loss_target)
        return 0.5 * _jnp.sum(_jnp.mean(err, axis=-1)) if err.ndim else 0.5 * err


def _adamw(w, g, m, v):
    m = ADAM_B1 * m + (1.0 - ADAM_B1) * g
    v = ADAM_B2 * v + (1.0 - ADAM_B2) * _jnp.square(g)
    m_hat = m / (1.0 - ADAM_B1 ** ADAM_STEP)
    v_hat = v / (1.0 - ADAM_B2 ** ADAM_STEP)
    delta = -ADAM_LR * (m_hat / (_jnp.sqrt(v_hat) + ADAM_EPS) + ADAM_WD * w)
    return delta, m, v


def reference(x, rel_bias, ln1_g, w_in, b_gate, w_a, pool_w, pool_scale, w_b, ssd_conv_w, ssd_conv_b, ssd_dt_bias, ssd_a_log, ssd_d, ssd_norm_w, w_c, w_o, ln2_g, ffn_w_up, ffn_conv_w, ffn_conv_b, ffn_w_down, final_g, loss_target, m_rel_bias, m_ln1_g, m_w_in, m_b_gate, m_w_a, m_pool_w, m_pool_scale, m_w_b, m_ssd_conv_w, m_ssd_conv_b, m_ssd_dt_bias, m_ssd_a_log, m_ssd_d, m_ssd_norm_w, m_w_c, m_w_o, m_ln2_g, m_ffn_w_up, m_ffn_conv_w, m_ffn_conv_b, m_ffn_w_down, m_final_g, v_rel_bias, v_ln1_g, v_w_in, v_b_gate, v_w_a, v_pool_w, v_pool_scale, v_w_b, v_ssd_conv_w, v_ssd_conv_b, v_ssd_dt_bias, v_ssd_a_log, v_ssd_d, v_ssd_norm_w, v_w_c, v_w_o, v_ln2_g, v_ffn_w_up, v_ffn_conv_w, v_ffn_conv_b, v_ffn_w_down, v_final_g):
    given = dict(x=x, rel_bias=rel_bias, ln1_g=ln1_g, w_in=w_in, b_gate=b_gate, w_a=w_a, pool_w=pool_w, pool_scale=pool_scale, w_b=w_b, ssd_conv_w=ssd_conv_w, ssd_conv_b=ssd_conv_b, ssd_dt_bias=ssd_dt_bias, ssd_a_log=ssd_a_log, ssd_d=ssd_d, ssd_norm_w=ssd_norm_w, w_c=w_c, w_o=w_o, ln2_g=ln2_g, ffn_w_up=ffn_w_up, ffn_conv_w=ffn_conv_w, ffn_conv_b=ffn_conv_b, ffn_w_down=ffn_w_down, final_g=final_g, loss_target=loss_target, m_rel_bias=m_rel_bias, m_ln1_g=m_ln1_g, m_w_in=m_w_in, m_b_gate=m_b_gate, m_w_a=m_w_a, m_pool_w=m_pool_w, m_pool_scale=m_pool_scale, m_w_b=m_w_b, m_ssd_conv_w=m_ssd_conv_w, m_ssd_conv_b=m_ssd_conv_b, m_ssd_dt_bias=m_ssd_dt_bias, m_ssd_a_log=m_ssd_a_log, m_ssd_d=m_ssd_d, m_ssd_norm_w=m_ssd_norm_w, m_w_c=m_w_c, m_w_o=m_w_o, m_ln2_g=m_ln2_g, m_ffn_w_up=m_ffn_w_up, m_ffn_conv_w=m_ffn_conv_w, m_ffn_conv_b=m_ffn_conv_b, m_ffn_w_down=m_ffn_w_down, m_final_g=m_final_g, v_rel_bias=v_rel_bias, v_ln1_g=v_ln1_g, v_w_in=v_w_in, v_b_gate=v_b_gate, v_w_a=v_w_a, v_pool_w=v_pool_w, v_pool_scale=v_pool_scale, v_w_b=v_w_b, v_ssd_conv_w=v_ssd_conv_w, v_ssd_conv_b=v_ssd_conv_b, v_ssd_dt_bias=v_ssd_dt_bias, v_ssd_a_log=v_ssd_a_log, v_ssd_d=v_ssd_d, v_ssd_norm_w=v_ssd_norm_w, v_w_c=v_w_c, v_w_o=v_w_o, v_ln2_g=v_ln2_g, v_ffn_w_up=v_ffn_w_up, v_ffn_conv_w=v_ffn_conv_w, v_ffn_conv_b=v_ffn_conv_b, v_ffn_w_down=v_ffn_w_down, v_final_g=v_final_g)
    weights = {n: given[n] for n in TWIN_WEIGHTS}
    shared = {n: given[n] for n in SHARED_INPUTS}
    per_example = {n: given[n] for n in ['x']}
    grad_fn = _jax.value_and_grad(_loss, argnums=(0, 1))

    def one_microbatch(ex, loss_target):
        ex = dict(ex)
        diff = ex.pop(TWIN_DIFF_INPUT)
        return grad_fn(weights, diff, {**shared, **ex}, loss_target)

    if N_MICROBATCH == 1:
        loss, (grad_w, grad_x) = one_microbatch(per_example, given["loss_target"])
    else:
        def body(carry, xs):
            loss_sum, grad_sum = carry
            l_k, (gw_k, gx_k) = one_microbatch(xs[0], xs[1])
            with _jax.named_scope("update"):
                return (loss_sum + l_k, _jax.tree.map(_jnp.add, grad_sum, gw_k)), gx_k

        init = (_jnp.zeros((), _jnp.float32), _jax.tree.map(_jnp.zeros_like, weights))
        (loss, grad_w), grad_x = _jax.lax.scan(body, init, (per_example, given["loss_target"]))
    with _jax.named_scope("update"):
        delta_w, new_m, new_v = {}, {}, {}
        for n in TWIN_WEIGHTS:
            delta_w[n], new_m[n], new_v[n] = _adamw(weights[n], grad_w[n], given["m_" + n], given["v_" + n])
    return (loss, grad_x, *[grad_w[n] for n in TWIN_WEIGHTS], *[delta_w[n] for n in TWIN_WEIGHTS],
            *[new_m[n] for n in TWIN_WEIGHTS], *[new_v[n] for n in TWIN_WEIGHTS])
```

```python
import functools
import math

import jax
import jax.numpy as jnp
from jax import lax
from jax.experimental import pallas as pl
from jax.experimental.pallas import tpu as pltpu

F32 = jnp.float32
BF16 = jnp.bfloat16
MESH = pl.DeviceIdType.MESH

D = 1024
HD = 64
GW = 384
AW = 3 * GW
WIN = 128
DILATIONS = (1, 4, 16)
REL_BUCKETS = 32
REL_MAX_DISTANCE = 2048
POOL_WINDOWS = (2, 4, 8, 16)
PG = 256
SSD_HEADS = 16
SSD_N = 128
SSD_CHUNK = 128
XBC = 1536
D_FF = 2816
EPS = 1e-6
NEG = -1e30
HALO = 16
LANES = 128

SEC_A = 3 * AW
SEC_B = D
SEC_C = D + XBC
SEC_D = 3200
IN_WIDTH = SEC_A + SEC_B + SEC_C + 16 + 3 * D

ADAM_LR = 0.001
ADAM_B1 = 0.9
ADAM_B2 = 0.999
ADAM_EPS = 1e-08
ADAM_WD = 0.01
ADAM_STEP = 10
ADAM_TILE = 256 * 1024


def _pick(d, cands):
    for t in cands:
        if d % t == 0:
            return t
    return d


def _iota(shape, dim):
    return lax.broadcasted_iota(jnp.int32, shape, dim)


def _dg(a, b, ca, cb):
    return lax.dot_general(a.astype(BF16), b.astype(BF16), (((ca,), (cb,)), ((), ())),
                           preferred_element_type=F32)


@jax.custom_vjp
def _bdot_nn(a, b):
    return _dg(a, b, 1, 0)


def _nn_fwd(a, b):
    return _dg(a, b, 1, 0), (a, b)


def _nn_bwd(res, g):
    a, b = res
    return _dg(g, b, 1, 1), _dg(a, g, 0, 0)


_bdot_nn.defvjp(_nn_fwd, _nn_bwd)


@jax.custom_vjp
def _bdot_nt(a, b):
    return _dg(a, b, 1, 1)


def _nt_fwd(a, b):
    return _dg(a, b, 1, 1), (a, b)


def _nt_bwd(res, g):
    a, b = res
    return _dg(g, b, 1, 0), _dg(g, a, 0, 0)


_bdot_nt.defvjp(_nt_fwd, _nt_bwd)


@jax.custom_vjp
def _bdot_tn(a, b):
    return _dg(a, b, 0, 0)


def _tn_fwd(a, b):
    return _dg(a, b, 0, 0), (a, b)


def _tn_bwd(res, g):
    a, b = res
    return _dg(b, g, 1, 1), _dg(a, g, 1, 0)


_bdot_tn.defvjp(_tn_fwd, _tn_bwd)


def _fdot(a, b):
    return jnp.dot(a, b, preferred_element_type=F32, precision=lax.Precision.HIGHEST)


def _sigmoid(x):
    return 1.0 / (1.0 + jnp.exp(-x))


def _silu(x):
    return x * _sigmoid(x)


def _softplus(x):
    return jnp.maximum(x, 0.0) + jnp.log(1.0 + jnp.exp(-jnp.abs(x)))


def _lane_pick(m, h):
    return jnp.sum(jnp.where(_iota(m.shape, 1) == h, m, 0.0), axis=1, keepdims=True)


def _row_pick(m, h):
    return jnp.sum(jnp.where(_iota(m.shape, 0) == h, m, 0.0), axis=0, keepdims=True)


def _stack_rows(rows, n):
    c = rows[0].shape[1]
    r = _iota((n, c), 0)
    out = jnp.zeros((n, c), F32)
    for k, v in enumerate(rows):
        out = out + jnp.where(r == k, v, 0.0)
    return out


def _mm(a, b, *, ta=False, tb=False, add=None, out_dtype=F32, name):
    if ta:
        K, M = a.shape
    else:
        M, K = a.shape
    if tb:
        N, Kb = b.shape
    else:
        Kb, N = b.shape
    assert K == Kb, (a.shape, b.shape, ta, tb)
    tm = _pick(M, (512, 640, 384, 256, 128))
    tn = _pick(N, (512, 640, 384, 256, 128))
    tk = _pick(K, (1024, 512, 640, 384, 256, 128))
    nk = K // tk
    ca = 0 if ta else 1
    cb = 1 if tb else 0

    def body(*refs):
        if add is None:
            a_ref, b_ref, o_ref, acc_ref = refs
            add_ref = None
        else:
            a_ref, b_ref, add_ref, o_ref, acc_ref = refs
        k = pl.program_id(2)
        part = _dg(a_ref[...], b_ref[...], ca, cb)

        @pl.when(k == 0)
        def _():
            acc_ref[...] = part

        @pl.when(k > 0)
        def _():
            acc_ref[...] += part

        @pl.when(k == nk - 1)
        def _():
            r = acc_ref[...]
            if add_ref is not None:
                r = r + add_ref[...].astype(F32)
            o_ref[...] = r.astype(o_ref.dtype)

    a_spec = pl.BlockSpec((tk, tm), lambda i, j, k: (k, i)) if ta else pl.BlockSpec((tm, tk), lambda i, j, k: (i, k))
    b_spec = pl.BlockSpec((tn, tk), lambda i, j, k: (j, k)) if tb else pl.BlockSpec((tk, tn), lambda i, j, k: (k, j))
    in_specs = [a_spec, b_spec]
    args = [a, b]
    if add is not None:
        in_specs.append(pl.BlockSpec((tm, tn), lambda i, j, k: (i, j)))
        args.append(add)
    return pl.pallas_call(
        body, name=name, grid=(M // tm, N // tn, nk), in_specs=in_specs,
        out_specs=pl.BlockSpec((tm, tn), lambda i, j, k: (i, j)),
        out_shape=jax.ShapeDtypeStruct((M, N), out_dtype),
        scratch_shapes=[pltpu.VMEM((tm, tn), F32)],
        compiler_params=pltpu.CompilerParams(dimension_semantics=("parallel", "parallel", "arbitrary")),
    )(*args)


def _rows(name, fn, ins, outs, accs=(), *, tm, nrows, ncol=1):
    nt = nrows // tm
    hb = tm // HALO
    nh = nrows // HALO
    in_specs, args = [], []
    for kind, arr, cw, base in ins:
        if kind == "row":
            cw = arr.shape[1] if cw is None else cw
            in_specs.append(pl.BlockSpec((tm, cw), lambda j, i, base=base: (i, base + j)))
        elif kind == "prev":
            in_specs.append(pl.BlockSpec((HALO, cw), lambda j, i, base=base: (jnp.maximum(i * hb - 1, 0), base + j)))
        elif kind == "next":
            in_specs.append(pl.BlockSpec((HALO, cw), lambda j, i, base=base: (jnp.minimum((i + 1) * hb, nh - 1), base + j)))
        elif kind == "const":
            in_specs.append(pl.BlockSpec(arr.shape, lambda j, i, nd=arr.ndim: (0,) * nd))
        elif kind == "ccol":
            in_specs.append(pl.BlockSpec((arr.shape[0], cw), lambda j, i, base=base: (0, base + j)))
        else:
            raise ValueError(kind)
        args.append(arr)
    out_specs, out_shape = [], []
    for ctot, cw, base, dt in outs:
        out_specs.append(pl.BlockSpec((tm, cw), lambda j, i, base=base: (i, base + j)))
        out_shape.append(jax.ShapeDtypeStruct((nrows, ctot), dt))
    for r, ctot, cw in accs:
        out_specs.append(pl.BlockSpec((r, cw), lambda j, i: (0, j)))
        out_shape.append(jax.ShapeDtypeStruct((r, ctot), F32))
    n_in, n_out = len(ins), len(outs)

    def body(*refs):
        j = pl.program_id(0)
        i = pl.program_id(1)
        res = fn(i, j, *[r[...] for r in refs[:n_in]])
        for r, v in zip(refs[n_in:n_in + n_out], res[:n_out]):
            r[...] = v.astype(r.dtype)
        for r, v in zip(refs[n_in + n_out:], res[n_out:]):
            @pl.when(i == 0)
            def _(r=r, v=v):
                r[...] = v

            @pl.when(i > 0)
            def _(r=r, v=v):
                r[...] += v

    res = pl.pallas_call(
        body, name=name, grid=(ncol, nt), in_specs=in_specs, out_specs=out_specs, out_shape=out_shape,
        compiler_params=pltpu.CompilerParams(dimension_semantics=("arbitrary", "arbitrary")),
    )(*args)
    return res


def _shift_down(xcat, k):
    return xcat if k == 0 else pltpu.roll(xcat, k, 0)


def _shift_up(xcat, k):
    return xcat if k == 0 else pltpu.roll(xcat, xcat.shape[0] - k, 0)


def _with_prev(i, halo, x):
    return jnp.concatenate([jnp.where(i == 0, 0.0, halo), x], axis=0)


def _with_next(i, nt, x, halo):
    return jnp.concatenate([x, jnp.where(i == nt - 1, 0.0, halo)], axis=0)


def _rms_core(x, g):
    r = lax.rsqrt(jnp.mean(x * x, axis=-1, keepdims=True) + EPS)
    return x * r * g


def _rms_fwd(x, g, name):
    S = x.shape[0]
    return _rows(name, lambda i, j, xv, gv: [_rms_core(xv, gv)],
                 [("row", x, None, 0), ("const", g, None, 0)], [(D, D, 0, BF16)], tm=256, nrows=S)[0]


def _rms_bwd(x, g, du, dres, name):
    S = x.shape[0]

    def fn(i, j, xv, gv, duv, drv):
        _, vjp = jax.vjp(_rms_core, xv, gv)
        dx, dg = vjp(duv)
        return [drv + dx, dg]

    return _rows(name, fn, [("row", x, None, 0), ("const", g, None, 0), ("row", du, None, 0), ("row", dres, None, 0)],
                 [(D, D, 0, F32)], [(1, D, D)], tm=256, nrows=S)


def _final_loss(x, target, g):
    S = x.shape[0]

    def fn(i, j, xv, tv, gv):
        def f(xx, gg):
            err = _rms_core(xx, gg) - tv
            return 0.5 * jnp.sum(err * err) / D

        loss, vjp = jax.vjp(f, xv, gv)
        dx, dg = vjp(jnp.ones((), F32))
        return [dx, dg, jnp.zeros((1, LANES), F32) + loss]

    return _rows("final_loss", fn, [("row", x, None, 0), ("row", target, None, 0), ("const", g, None, 0)],
                 [(D, D, 0, F32)], [(1, D, D), (1, LANES, LANES)], tm=256, nrows=S)


def _attn_valid(n):
    qi = _iota((WIN, 2 * WIN), 0)
    kk = _iota((WIN, 2 * WIN), 1)
    rel = qi + WIN - kk
    return (rel >= 0) & (rel <= WIN) & ((kk >= WIN) | (n > 0))


def _attn_block(q, kp, kc, vp, vc, b0, b1, valid):
    k = jnp.concatenate([kp, kc], axis=0)
    v = jnp.concatenate([vp, vc], axis=0)
    lo = _iota((WIN, LANES), 1) < HD
    scale = 1.0 / math.sqrt(HD)
    os_, ls_ = [], []
    for hh, b in ((0, b0), (1, b1)):
        qm = jnp.where(lo if hh == 0 else ~lo, q, 0.0)
        s = _bdot_nt(qm, k) * scale + b
        s = jnp.where(valid, s, NEG)
        m = lax.stop_gradient(jnp.max(s, axis=1, keepdims=True))
        p = jnp.exp(s - m)
        l = jnp.sum(p, axis=1, keepdims=True)
        os_.append(_bdot_nn(p, v) / l)
        ls_.append(m + jnp.log(l))
    return jnp.where(lo, os_[0], os_[1]), jnp.where(lo, ls_[0], ls_[1])


def _attn_fwd(pa, bias, gi, name):
    S = pa.shape[0]
    d = DILATIONS[gi]
    L = S // d
    nb = L // WIN
    cpr = SEC_A // LANES
    pav = pa.reshape(L, d * SEC_A)
    qb = 3 * gi

    def body(q_ref, kp_ref, kc_ref, vp_ref, vc_ref, b_ref, o_ref, l_ref):
        valid = _attn_valid(pl.program_id(2))
        o, lse = _attn_block(q_ref[...], kp_ref[...], kc_ref[...], vp_ref[...], vc_ref[...],
                             b_ref[0], b_ref[1], valid)
        o_ref[...] = o
        l_ref[...] = lse

    def spec(off, prev):
        if prev:
            return pl.BlockSpec((WIN, LANES), lambda r, hp, n: (jnp.maximum(n - 1, 0), r * cpr + off + qb + hp))
        return pl.BlockSpec((WIN, LANES), lambda r, hp, n: (n, r * cpr + off + qb + hp))

    ospec = pl.BlockSpec((WIN, LANES), lambda r, hp, n: (n, r * 3 + hp))
    o, lse = pl.pallas_call(
        body, name=name, grid=(d, 3, nb),
        in_specs=[spec(0, False), spec(9, True), spec(9, False), spec(18, True), spec(18, False),
                  pl.BlockSpec((2, WIN, 2 * WIN), lambda r, hp, n: (hp, 0, 0))],
        out_specs=[ospec, ospec],
        out_shape=[jax.ShapeDtypeStruct((L, d * GW), F32)] * 2,
        compiler_params=pltpu.CompilerParams(dimension_semantics=("parallel", "parallel", "arbitrary")),
    )(pav, pav, pav, pav, pav, bias)
    return o.reshape(S, GW), lse.reshape(S, GW)


def _attn_bwd(pa, bias, do, dlse, gi, name):
    S = pa.shape[0]
    d = DILATIONS[gi]
    L = S // d
    nb = L // WIN
    cpr = SEC_A // LANES
    pav = pa.reshape(L, d * SEC_A)
    dov = do.reshape(L, d * GW)
    dlv = dlse.reshape(L, d * GW)
    qb = 3 * gi

    def body(q_ref, kp_ref, kc_ref, vp_ref, vc_ref, b_ref, do_ref, dl_ref,
             dq_ref, dk_ref, dv_ref, db_ref, ck, cv):
        r = pl.program_id(1)
        n = pl.program_id(2)

        @pl.when((r == 0) & (n == 0))
        def _():
            db_ref[...] = jnp.zeros_like(db_ref)

        @pl.when(n == 0)
        def _():
            ck[...] = jnp.zeros_like(ck)
            cv[...] = jnp.zeros_like(cv)

        @pl.when(n < nb)
        def _():
            valid = _attn_valid(n)
            f = functools.partial(_attn_block, valid=valid)
            _, vjp = jax.vjp(f, q_ref[...], kp_ref[...], kc_ref[...], vp_ref[...], vc_ref[...], b_ref[0], b_ref[1])
            dq, dkp, dkc, dvp, dvc, db0, db1 = vjp((do_ref[...].astype(F32), dl_ref[...]))
            dq_ref[...] = dq.astype(dq_ref.dtype)
            dk_ref[...] = (ck[...] + dkp).astype(dk_ref.dtype)
            dv_ref[...] = (cv[...] + dvp).astype(dv_ref.dtype)
            ck[...] = dkc
            cv[...] = dvc
            db_ref[0] += db0
            db_ref[1] += db1

        @pl.when(n == nb)
        def _():
            dk_ref[...] = ck[...].astype(dk_ref.dtype)
            dv_ref[...] = cv[...].astype(dv_ref.dtype)

    def cur(n):
        return jnp.minimum(n, nb - 1)

    def spec(off, prev):
        if prev:
            return pl.BlockSpec((WIN, LANES), lambda hp, r, n: (jnp.maximum(cur(n) - 1, 0), r * cpr + off + qb + hp))
        return pl.BlockSpec((WIN, LANES), lambda hp, r, n: (cur(n), r * cpr + off + qb + hp))

    gspec = pl.BlockSpec((WIN, LANES), lambda hp, r, n: (cur(n), r * 3 + hp))
    kspec = pl.BlockSpec((WIN, LANES), lambda hp, r, n: (jnp.maximum(n - 1, 0), r * 3 + hp))
    dq, dk, dv, db = pl.pallas_call(
        body, name=name, grid=(3, d, nb + 1),
        in_specs=[spec(0, False), spec(9, True), spec(9, False), spec(18, True), spec(18, False),
                  pl.BlockSpec((2, WIN, 2 * WIN), lambda hp, r, n: (hp, 0, 0)), gspec, gspec],
        out_specs=[gspec, kspec, kspec, pl.BlockSpec((2, WIN, 2 * WIN), lambda hp, r, n: (hp, 0, 0))],
        out_shape=[jax.ShapeDtypeStruct((L, d * GW), BF16)] * 3 + [jax.ShapeDtypeStruct((6, WIN, 2 * WIN), F32)],
        scratch_shapes=[pltpu.VMEM((WIN, LANES), F32), pltpu.VMEM((WIN, LANES), F32)],
        compiler_params=pltpu.CompilerParams(dimension_semantics=("arbitrary", "arbitrary", "arbitrary")),
    )(pav, pav, pav, pav, pav, bias, dov, dlv)
    return dq.reshape(S, GW), dk.reshape(S, GW), dv.reshape(S, GW), db


def _mix_core(o0, o1, o2, l0, l1, l2):
    m = lax.stop_gradient(jnp.maximum(jnp.maximum(l0, l1), l2))
    e0, e1, e2 = jnp.exp(l0 - m), jnp.exp(l1 - m), jnp.exp(l2 - m)
    return (e0 * o0 + e1 * o1 + e2 * o2) / (e0 + e1 + e2)


def _mix_fwd(os_, ls_, name):
    S = os_[0].shape[0]
    ins = [("row", a, None, 0) for a in (*os_, *ls_)]
    return _rows(name, lambda i, j, *v: [_mix_core(*v)], ins, [(GW, GW, 0, BF16)], tm=256, nrows=S)[0]


def _mix_bwd(os_, ls_, datt, name):
    S = datt.shape[0]

    def fn(i, j, *v):
        _, vjp = jax.vjp(_mix_core, *v[:6])
        return list(vjp(v[6]))

    ins = [("row", a, None, 0) for a in (*os_, *ls_, datt)]
    outs = [(GW, GW, 0, BF16)] * 3 + [(GW, GW, 0, F32)] * 3
    r = _rows(name, fn, ins, outs, tm=256, nrows=S)
    return r[:3], r[3:]


def _t5_bucket(dist):
    max_exact = REL_BUCKETS // 2
    is_small = dist < max_exact
    nf = jnp.maximum(dist, 1).astype(F32)
    large = max_exact + (jnp.log(nf / max_exact) / math.log(REL_MAX_DISTANCE / max_exact)
                         * (REL_BUCKETS - max_exact)).astype(jnp.int32)
    large = jnp.minimum(large, REL_BUCKETS - 1)
    return jnp.where(is_small, dist, large)


def _buckets(d):
    qi = jnp.arange(WIN)[:, None]
    kk = jnp.arange(2 * WIN)[None, :]
    rel = qi + WIN - kk
    return _t5_bucket(jnp.clip(rel, 0, None) * d)


def _pool_cnt(i, tm, w):
    pos = i * tm + _iota((tm, PG), 0) + 1
    return jnp.minimum(pos, w).astype(F32)


def _pool_d(i, tm, halo, u):
    ds = []
    for g, w in enumerate(POOL_WINDOWS):
        ug = u[:, g * PG:(g + 1) * PG]
        s = _with_prev(i, halo[:, g * PG:(g + 1) * PG], ug)
        step = 1
        while step < w:
            s = s + _shift_down(s, step)
            step *= 2
        ds.append(s[HALO:] / _pool_cnt(i, tm, w) - ug)
    return ds


def _pool_lin(d0, d1, d2, d3, w0, w1, w2, w3, scale):
    y = jnp.concatenate([_bdot_nn(d0, w0), _bdot_nn(d1, w1), _bdot_nn(d2, w2), _bdot_nn(d3, w3)], axis=1)
    return y * scale


def _pool_fwd(pb, pw, scale, name):
    S = pb.shape[0]
    tm = 256

    def fn(i, j, halo, u, w, sc):
        ds = _pool_d(i, tm, halo, u)
        return [_pool_lin(*ds, *[w[k].astype(F32) for k in range(4)], sc)]

    return _rows(name, fn, [("prev", pb, D, 0), ("row", pb, None, 0), ("const", pw, None, 0), ("const", scale, None, 0)],
                 [(D, D, 0, BF16)], tm=tm, nrows=S)[0]


def _pool_bwd(pb, pw, scale, dpo, name):
    S = pb.shape[0]
    tm = 256
    nt = S // tm

    def fn1(i, j, halo, u, w, sc, dy):
        ds = _pool_d(i, tm, halo, u)
        _, vjp = jax.vjp(_pool_lin, *ds, *[w[k].astype(F32) for k in range(4)], sc)
        g = vjp(dy)
        e = jnp.concatenate([g[k] / _pool_cnt(i, tm, wd) for k, wd in enumerate(POOL_WINDOWS)], axis=1)
        return [e, jnp.concatenate(g[4:8], axis=0), g[8]]

    e, dpw, dsc = _rows(name + "_a", fn1,
                        [("prev", pb, D, 0), ("row", pb, None, 0), ("const", pw, None, 0), ("const", scale, None, 0),
                         ("row", dpo, None, 0)],
                        [(D, D, 0, F32)], [(4 * PG, PG, PG), (1, D, D)], tm=tm, nrows=S)

    def fn2(i, j, ev, halo):
        outs = []
        for g, w in enumerate(POOL_WINDOWS):
            eg = ev[:, g * PG:(g + 1) * PG]
            s = _with_next(i, nt, eg, halo[:, g * PG:(g + 1) * PG])
            step = 1
            while step < w:
                s = s + _shift_up(s, step)
                step *= 2
            outs.append(s[:tm] - eg * _pool_cnt(i, tm, w))
        return [jnp.concatenate(outs, axis=1)]

    du = _rows(name + "_b", fn2, [("row", e, None, 0), ("next", e, D, 0)], [(D, D, 0, BF16)], tm=tm, nrows=S)[0]
    return du, dpw, dsc


def _conv_taps(i, halo, x, K):
    cat = _with_prev(i, halo, x)
    return [_shift_down(cat, K - 1 - k)[HALO:] for k in range(K)]


def _conv_pre(taps, w, b):
    acc = b
    for k, t in enumerate(taps):
        acc = acc + t * _row_pick(w, k)
    return acc


def _conv_t(name, dpre, w, K, ncol, cw, out_dtype):
    S, C = dpre.shape
    tm = 256
    nt = S // tm

    def fn(i, j, dp, halo, wv):
        cat = _with_next(i, nt, dp, halo)
        acc = jnp.zeros((tm, cw), F32)
        for k in range(K):
            acc = acc + _shift_up(cat, K - 1 - k)[:tm] * _row_pick(wv, k)
        return [acc]

    return _rows(name, fn, [("row", dpre, cw, 0), ("next", dpre, cw, 0), ("ccol", w, cw, 0)],
                 [(C, cw, 0, out_dtype)], tm=tm, nrows=S, ncol=ncol)[0]


CW = 256


def _ssd_conv_fwd(pc, w, b, name):
    S = pc.shape[0]
    base = D // CW

    def fn(i, j, halo, x, wv, bv):
        return [_silu(_conv_pre(_conv_taps(i, halo, x, 4), wv, bv))]

    return _rows(name, fn, [("prev", pc, CW, base), ("row", pc, CW, base), ("ccol", w, CW, 0), ("ccol", b, CW, 0)],
                 [(XBC, CW, 0, F32)], tm=256, nrows=S, ncol=XBC // CW)[0]


def _ssd_conv_bwd(pc, w, b, dy, name):
    S = pc.shape[0]
    base = D // CW

    def fn(i, j, halo, x, wv, bv, dyv):
        taps = _conv_taps(i, halo, x, 4)
        pre = _conv_pre(taps, wv, bv)
        sg = _sigmoid(pre)
        dpre = dyv * sg * (1.0 + pre * (1.0 - sg))
        dw = _stack_rows([jnp.sum(dpre * t, axis=0, keepdims=True) for t in taps], 4)
        return [dpre, dw, jnp.sum(dpre, axis=0, keepdims=True)]

    dpre, dw, db = _rows(name + "_a", fn,
                         [("prev", pc, CW, base), ("row", pc, CW, base), ("ccol", w, CW, 0), ("ccol", b, CW, 0),
                          ("row", dy, CW, 0)],
                         [(XBC, CW, 0, F32)], [(4, XBC, CW), (1, XBC, CW)], tm=256, nrows=S, ncol=XBC // CW)
    dx = _conv_t(name + "_b", dpre, w, 4, XBC // CW, CW, BF16)
    return dx, dw, db


NFC = D_FF // CW


def _ffn_act_fwd(h, w, b, name):
    S = h.shape[0]

    def fn(i, j, ha, a, hv, v, wa, wv, ba, bv):
        pa = _conv_pre(_conv_taps(i, ha, a, 3), wa, ba)
        pv = _conv_pre(_conv_taps(i, hv, v, 3), wv, bv)
        return [_silu(pa) * pv]

    return _rows(name, fn,
                 [("prev", h, CW, 0), ("row", h, CW, 0), ("prev", h, CW, NFC), ("row", h, CW, NFC),
                  ("ccol", w, CW, 0), ("ccol", w, CW, NFC), ("ccol", b, CW, 0), ("ccol", b, CW, NFC)],
                 [(D_FF, CW, 0, BF16)], tm=256, nrows=S, ncol=NFC)[0]


def _ffn_act_bwd(h, w, b, df, name):
    S = h.shape[0]

    tm = 256
    nt = S // tm
    hb = tm // HALO

    def body(ha_ref, a_ref, hv_ref, v_ref, wa_ref, wv_ref, ba_ref, bv_ref, df_ref, dp_ref, dw_ref, db_ref):
        j = pl.program_id(0)
        i = pl.program_id(1)
        ta = _conv_taps(i, ha_ref[...], a_ref[...], 3)
        tv = _conv_taps(i, hv_ref[...], v_ref[...], 3)
        pa = _conv_pre(ta, wa_ref[...], ba_ref[...])
        pv = _conv_pre(tv, wv_ref[...], bv_ref[...])
        sg = _sigmoid(pa)
        dfv = df_ref[...].astype(F32)
        is_a = j < NFC
        dpre = jnp.where(is_a, dfv * pv * sg * (1.0 + pa * (1.0 - sg)), dfv * pa * sg)
        dp_ref[...] = dpre
        dw = _stack_rows([jnp.sum(dpre * jnp.where(is_a, x, y), axis=0, keepdims=True) for x, y in zip(ta, tv)], 3)
        db = jnp.sum(dpre, axis=0, keepdims=True)

        @pl.when(i == 0)
        def _():
            dw_ref[...] = dw
            db_ref[...] = db

        @pl.when(i > 0)
        def _():
            dw_ref[...] += dw
            db_ref[...] += db

    def rowspec(base):
        return pl.BlockSpec((tm, CW), lambda j, i: (i, base + j % NFC))

    def prevspec(base):
        return pl.BlockSpec((HALO, CW), lambda j, i: (jnp.maximum(i * hb - 1, 0), base + j % NFC))

    def ccol(r, base):
        return pl.BlockSpec((r, CW), lambda j, i: (0, base + j % NFC))

    dpre, dw, db = pl.pallas_call(
        body, name=name + "_a", grid=(2 * NFC, nt),
        in_specs=[prevspec(0), rowspec(0), prevspec(NFC), rowspec(NFC), ccol(3, 0), ccol(3, NFC), ccol(1, 0), ccol(1, NFC),
                  rowspec(0)],
        out_specs=[pl.BlockSpec((tm, CW), lambda j, i: (i, j)), pl.BlockSpec((3, CW), lambda j, i: (0, j)),
                   pl.BlockSpec((1, CW), lambda j, i: (0, j))],
        out_shape=[jax.ShapeDtypeStruct((S, 2 * D_FF), F32), jax.ShapeDtypeStruct((3, 2 * D_FF), F32),
                   jax.ShapeDtypeStruct((1, 2 * D_FF), F32)],
        compiler_params=pltpu.CompilerParams(dimension_semantics=("arbitrary", "arbitrary")),
    )(h, h, h, h, w, w, b, b, df)
    dh = _conv_t(name + "_b", dpre, w, 3, 2 * NFC, CW, BF16)
    return dh, dw, db


NSLAB = D // LANES


def _ssd_chunk(xs, Bs, Cs, dtraw, dtb, alog, prev):
    lsz = SSD_CHUNK
    lane = _iota((lsz, LANES), 1)
    row = _iota((lsz, LANES), 0)
    dt = jnp.where(lane < SSD_HEADS, _softplus(dtraw + dtb), 0.0)
    a = dt * (-jnp.exp(alog))
    tril = row >= lane
    a_cs = _fdot(tril.astype(F32), a)
    a_cst = a_cs.T
    a_last = jnp.sum(a, axis=0, keepdims=True)
    lo = lane < HD
    top = row < HD
    cbs = [_bdot_nt(Cs[g], Bs[g]) for g in range(2)]
    ys, news = [], []
    for s in range(NSLAB):
        g = s // (NSLAB // 2)
        cols, lms, dts, als = [], [], [], []
        for hh in range(2):
            h = 2 * s + hh
            col = _lane_pick(a_cs, h)
            seg = col - _row_pick(a_cst, h)
            lms.append(jnp.exp(jnp.where(tril, seg, NEG)))
            cols.append(col)
            dts.append(_lane_pick(dt, h))
            als.append(_lane_pick(a_last, h))
        col_x = jnp.where(lo, cols[0], cols[1])
        al_x = jnp.where(lo, als[0], als[1])
        xc = xs[s] * jnp.where(lo, dts[0], dts[1])
        yd = jnp.where(lo, _bdot_nn(cbs[g] * lms[0], xc), _bdot_nn(cbs[g] * lms[1], xc))
        yoff = _bdot_nt(Cs[g], prev[s]) * jnp.exp(col_x)
        ys.append(yd + yoff)
        st = _bdot_tn(xc * jnp.exp(al_x - col_x), Bs[g])
        news.append(prev[s] * jnp.exp(jnp.where(top, als[0], als[1])) + st)
    return ys, news


def _ssd_scan_fwd(xbc_c, pd, dtb, alog, name):
    S = xbc_c.shape[0]
    nc = S // SSD_CHUNK

    def body(x_ref, b_ref, c_ref, dt_ref, dtb_ref, al_ref, y_ref, st_ref, state):
        c = pl.program_id(0)

        @pl.when(c == 0)
        def _():
            state[...] = jnp.zeros_like(state)

        xs = [x_ref[:, s * LANES:(s + 1) * LANES] for s in range(NSLAB)]
        Bs = [b_ref[:, g * SSD_N:(g + 1) * SSD_N] for g in range(2)]
        Cs = [c_ref[:, g * SSD_N:(g + 1) * SSD_N] for g in range(2)]
        prev = [state[s * LANES:(s + 1) * LANES, :] for s in range(NSLAB)]
        ys, news = _ssd_chunk(xs, Bs, Cs, dt_ref[...], dtb_ref[...], al_ref[...], prev)
        st_ref[0] = state[...]
        for s in range(NSLAB):
            y_ref[:, s * LANES:(s + 1) * LANES] = ys[s]
            state[s * LANES:(s + 1) * LANES, :] = news[s]

    return pl.pallas_call(
        body, name=name, grid=(nc,),
        in_specs=[pl.BlockSpec((SSD_CHUNK, D), lambda c: (c, 0)),
                  pl.BlockSpec((SSD_CHUNK, 2 * SSD_N), lambda c: (c, D // (2 * SSD_N))),
                  pl.BlockSpec((SSD_CHUNK, 2 * SSD_N), lambda c: (c, D // (2 * SSD_N) + 1)),
                  pl.BlockSpec((SSD_CHUNK, LANES), lambda c: (c, 0)),
                  pl.BlockSpec((1, LANES), lambda c: (0, 0)), pl.BlockSpec((1, LANES), lambda c: (0, 0))],
        out_specs=[pl.BlockSpec((SSD_CHUNK, D), lambda c: (c, 0)), pl.BlockSpec((1, D, SSD_N), lambda c: (c, 0, 0))],
        out_shape=[jax.ShapeDtypeStruct((S, D), F32), jax.ShapeDtypeStruct((nc, D, SSD_N), F32)],
        scratch_shapes=[pltpu.VMEM((D, SSD_N), F32)],
        compiler_params=pltpu.CompilerParams(dimension_semantics=("arbitrary",)),
    )(xbc_c, xbc_c, xbc_c, pd, dtb, alog)


def _ssd_scan_bwd(xbc_c, pd, dtb, alog, states, dy, dxs_skip, name):
    S = xbc_c.shape[0]
    nc = S // SSD_CHUNK

    def body(x_ref, b_ref, c_ref, dt_ref, dtb_ref, al_ref, st_ref, dy_ref, sk_ref,
             dx_ref, ddt_ref, ddtb_ref, dal_ref, dstate):
        c = pl.program_id(0)

        @pl.when(c == 0)
        def _():
            dstate[...] = jnp.zeros_like(dstate)
            ddtb_ref[...] = jnp.zeros_like(ddtb_ref)
            dal_ref[...] = jnp.zeros_like(dal_ref)

        xs = [x_ref[:, s * LANES:(s + 1) * LANES] for s in range(NSLAB)]
        Bs = [b_ref[:, g * SSD_N:(g + 1) * SSD_N] for g in range(2)]
        Cs = [c_ref[:, g * SSD_N:(g + 1) * SSD_N] for g in range(2)]
        prev = [st_ref[0, s * LANES:(s + 1) * LANES, :] for s in range(NSLAB)]
        _, vjp = jax.vjp(_ssd_chunk, xs, Bs, Cs, dt_ref[...], dtb_ref[...], al_ref[...], prev)
        dys = [dy_ref[:, s * LANES:(s + 1) * LANES] for s in range(NSLAB)]
        dnew = [dstate[s * LANES:(s + 1) * LANES, :] for s in range(NSLAB)]
        dxs, dBs, dCs, ddt, ddtb, dal, dprev = vjp((dys, dnew))
        for s in range(NSLAB):
            dx_ref[:, s * LANES:(s + 1) * LANES] = dxs[s] + sk_ref[:, s * LANES:(s + 1) * LANES]
            dstate[s * LANES:(s + 1) * LANES, :] = dprev[s]
        for g in range(2):
            dx_ref[:, D + g * SSD_N:D + (g + 1) * SSD_N] = dBs[g]
            dx_ref[:, D + 2 * SSD_N + g * SSD_N:D + 2 * SSD_N + (g + 1) * SSD_N] = dCs[g]
        ddt_ref[...] = ddt
        ddtb_ref[...] += ddtb
        dal_ref[...] += dal

    def rv(c):
        return nc - 1 - c

    return pl.pallas_call(
        body, name=name, grid=(nc,),
        in_specs=[pl.BlockSpec((SSD_CHUNK, D), lambda c: (rv(c), 0)),
                  pl.BlockSpec((SSD_CHUNK, 2 * SSD_N), lambda c: (rv(c), D // (2 * SSD_N))),
                  pl.BlockSpec((SSD_CHUNK, 2 * SSD_N), lambda c: (rv(c), D // (2 * SSD_N) + 1)),
                  pl.BlockSpec((SSD_CHUNK, LANES), lambda c: (rv(c), 0)),
                  pl.BlockSpec((1, LANES), lambda c: (0, 0)), pl.BlockSpec((1, LANES), lambda c: (0, 0)),
                  pl.BlockSpec((1, D, SSD_N), lambda c: (rv(c), 0, 0)),
                  pl.BlockSpec((SSD_CHUNK, D), lambda c: (rv(c), 0)),
                  pl.BlockSpec((SSD_CHUNK, D), lambda c: (rv(c), 0))],
        out_specs=[pl.BlockSpec((SSD_CHUNK, XBC), lambda c: (rv(c), 0)),
                   pl.BlockSpec((SSD_CHUNK, LANES), lambda c: (rv(c), 0)),
                   pl.BlockSpec((1, LANES), lambda c: (0, 0)), pl.BlockSpec((1, LANES), lambda c: (0, 0))],
        out_shape=[jax.ShapeDtypeStruct((S, XBC), F32), jax.ShapeDtypeStruct((S, LANES), F32),
                   jax.ShapeDtypeStruct((1, LANES), F32), jax.ShapeDtypeStruct((1, LANES), F32)],
        scratch_shapes=[pltpu.VMEM((D, SSD_N), F32)],
        compiler_params=pltpu.CompilerParams(dimension_semantics=("arbitrary",)),
    )(xbc_c, xbc_c, xbc_c, pd, dtb, alog, states, dy, dxs_skip)


def _ssd_post_core(y, xs, z, d128, nw):
    tm = y.shape[0]
    ex = (_iota((LANES, D), 1) // HD == _iota((LANES, D), 0)).astype(F32)
    d_x = jnp.sum(_fdot(jnp.broadcast_to(d128, (8, LANES)), ex), axis=0, keepdims=True) * 0.125
    y2 = (y + d_x * xs) * _silu(z)
    lo = _iota((tm, D), 1) < D // 2
    sq = y2 * y2
    ms0 = jnp.sum(jnp.where(lo, sq, 0.0), axis=-1, keepdims=True) / (D // 2)
    ms1 = jnp.sum(jnp.where(lo, 0.0, sq), axis=-1, keepdims=True) / (D // 2)
    r = jnp.where(lo, lax.rsqrt(ms0 + EPS), lax.rsqrt(ms1 + EPS))
    return y2 * r * nw


def _ssd_post_ins(y, xbc_c, pc, d128, nw):
    return [("row", y, None, 0), ("row", xbc_c, D, 0), ("row", pc, D, 0), ("const", d128, None, 0), ("const", nw, None, 0)]


def _ssd_post_fwd(y, xbc_c, pc, d128, nw, name):
    S = y.shape[0]
    return _rows(name, lambda i, j, *v: [_ssd_post_core(*v)], _ssd_post_ins(y, xbc_c, pc, d128, nw),
                 [(D, D, 0, BF16)], tm=128, nrows=S)[0]


def _ssd_post_bwd(y, xbc_c, pc, d128, nw, dout, name):
    S = y.shape[0]

    def fn(i, j, *v):
        _, vjp = jax.vjp(_ssd_post_core, *v[:5])
        return list(vjp(v[5]))

    return _rows(name, fn, _ssd_post_ins(y, xbc_c, pc, d128, nw) + [("row", dout, None, 0)],
                 [(D, D, 0, F32), (D, D, 0, F32), (D, D, 0, BF16)], [(1, LANES, LANES), (1, D, D)], tm=128, nrows=S)


def _gates_core(g0, g1, g2, b0, b1, b2, ya, yb, yc):
    return _sigmoid(g0 + b0) * ya + _sigmoid(g1 + b1) * yb + _sigmoid(g2 + b2) * yc


def _gate_parts(pdv, bv):
    gp = pltpu.roll(pdv, SEC_D - 16, 1)
    return [gp[:, k * D:(k + 1) * D] for k in range(3)] + [bv[:, k * D:(k + 1) * D] for k in range(3)]


def _gates_fwd(pd, bg, ya, yb, yc, name):
    S = pd.shape[0]

    def fn(i, j, pdv, bv, a, b, c):
        return [_gates_core(*_gate_parts(pdv, bv), a, b, c)]

    return _rows(name, fn, [("row", pd, None, 0), ("const", bg, None, 0), ("row", ya, None, 0), ("row", yb, None, 0),
                            ("row", yc, None, 0)], [(D, D, 0, BF16)], tm=128, nrows=S)[0]


def _gates_bwd(pd, bg, ya, yb, yc, dm, name):
    S = pd.shape[0]
    tm = 128

    def fn(i, j, pdv, bv, a, b, c, dmv):
        _, vjp = jax.vjp(_gates_core, *_gate_parts(pdv, bv), a, b, c)
        g = vjp(dmv)
        dgp = jnp.concatenate([g[0], g[1], g[2], jnp.zeros((tm, LANES), F32)], axis=1)
        return [g[6], g[7], g[8], pltpu.roll(dgp, 16, 1), jnp.concatenate([g[3], g[4], g[5]], axis=1)]

    return _rows(name, fn, [("row", pd, None, 0), ("const", bg, None, 0), ("row", ya, None, 0), ("row", yb, None, 0),
                            ("row", yc, None, 0), ("row", dm, None, 0)],
                 [(D, D, 0, BF16)] * 3 + [(SEC_D, SEC_D, 0, BF16)], [(1, 3 * D, 3 * D)], tm=tm, nrows=S)


def _adamw(w, g, m, v, name):
    rows, C = w.shape
    tm = _pick(rows, [t for t in (512, 256, 128, 64, 32, 16, 8) if t * C <= ADAM_TILE])

    def fn(i, j, wv, gv, mv, vv):
        m2 = ADAM_B1 * mv + (1.0 - ADAM_B1) * gv
        v2 = ADAM_B2 * vv + (1.0 - ADAM_B2) * jnp.square(gv)
        m_hat = m2 / (1.0 - ADAM_B1 ** ADAM_STEP)
        v_hat = v2 / (1.0 - ADAM_B2 ** ADAM_STEP)
        delta = -ADAM_LR * (m_hat / (jnp.sqrt(v_hat) + ADAM_EPS) + ADAM_WD * wv)
        return [delta, m2, v2]

    return _rows(name, fn, [("row", a, None, 0) for a in (w, g, m, v)], [(C, C, 0, F32)] * 3, tm=tm, nrows=rows)


def _position():
    return lax.axis_index("x"), lax.axis_index("y"), lax.axis_index("c")


def _other_chips(x, y):
    return [(1 - x, y), (x, 1 - y), (1 - x, 1 - y)]


_HBM = pl.BlockSpec(memory_space=pltpu.HBM)


def _gather_weights(pack, name):
    R, C = pack.shape
    half = R // 2

    def body(p_ref, out_ref, send_sems, recv_sems, local_sem):
        x, y, c = _position()
        sibling = (x, y, 1 - c)
        chips = _other_chips(x, y)

        def slab(chip, h):
            return out_ref.at[2 * chip[0] + chip[1], pl.ds(h * half, half), :]

        def copy(k, src, dst, to):
            return pltpu.make_async_remote_copy(src_ref=src, dst_ref=dst, send_sem=send_sems.at[k],
                                                recv_sem=recv_sems.at[k], device_id=to, device_id_type=MESH)

        mine = pltpu.make_async_copy(p_ref, out_ref.at[2 * x + y], local_sem)
        mine.start()
        first = [copy(j, p_ref.at[pl.ds(c * half, half), :], slab((x, y), c), (*chip, c)) for j, chip in enumerate(chips)]
        for cp in first:
            cp.start()
        passed = [copy(3 + j, slab(chip, c), slab(chip, c), sibling) for j, chip in enumerate(chips)]
        for j, chip in enumerate(chips):
            copy(j, slab(chip, c), slab(chip, c), (x, y, c)).wait_recv()
            passed[j].start()
        for j, chip in enumerate(chips):
            copy(3 + j, slab(chip, 1 - c), slab(chip, 1 - c), (x, y, c)).wait_recv()
        for cp in first + passed:
            cp.wait_send()
        mine.wait()

    return pl.pallas_call(
        body, name=name, in_specs=[_HBM], out_specs=_HBM,
        out_shape=jax.ShapeDtypeStruct((4, R, C), pack.dtype),
        scratch_shapes=[pltpu.SemaphoreType.DMA((6,)), pltpu.SemaphoreType.DMA((6,)), pltpu.SemaphoreType.DMA],
    )(pack)


def _rs_pair_exchange(g, name):
    _, _, Rh, C = g.shape

    def body(g_ref, out_ref, send_sem, recv_sem):
        x, y, c = _position()
        cp = pltpu.make_async_remote_copy(src_ref=g_ref.at[1 - c], dst_ref=out_ref, send_sem=send_sem,
                                          recv_sem=recv_sem, device_id=(x, y, 1 - c), device_id_type=MESH)
        cp.start()
        cp.wait()

    return pl.pallas_call(
        body, name=name, in_specs=[_HBM], out_specs=_HBM,
        out_shape=jax.ShapeDtypeStruct((4, Rh, C), g.dtype),
        scratch_shapes=[pltpu.SemaphoreType.DMA, pltpu.SemaphoreType.DMA],
    )(g)


def _rs_chip_exchange(h, name):
    _, Rh, C = h.shape

    def body(h_ref, out_ref, send_sems, recv_sems):
        x, y, c = _position()
        chips = _other_chips(x, y)
        cps = [pltpu.make_async_remote_copy(src_ref=h_ref.at[2 * chip[0] + chip[1]], dst_ref=out_ref.at[j],
                                            send_sem=send_sems.at[j], recv_sem=recv_sems.at[j],
                                            device_id=(*chip, c), device_id_type=MESH)
               for j, chip in enumerate(chips)]
        for cp in cps:
            cp.start()
        for cp in cps:
            cp.wait()

    return pl.pallas_call(
        body, name=name, in_specs=[_HBM], out_specs=_HBM,
        out_shape=jax.ShapeDtypeStruct((3, Rh, C), h.dtype),
        scratch_shapes=[pltpu.SemaphoreType.DMA((3,)), pltpu.SemaphoreType.DMA((3,))],
    )(h)


def _rs_share(r, name):
    Rh, C = r.shape

    def body(r_ref, out_ref, send_sem, recv_sem, local_sem):
        x, y, c = _position()
        mine = pltpu.make_async_copy(r_ref, out_ref.at[c], local_sem)
        mine.start()
        cp = pltpu.make_async_remote_copy(src_ref=r_ref, dst_ref=out_ref.at[c], send_sem=send_sem,
                                          recv_sem=recv_sem, device_id=(x, y, 1 - c), device_id_type=MESH)
        cp.start()
        cp.wait_send()
        pltpu.make_async_remote_copy(src_ref=r_ref, dst_ref=out_ref.at[1 - c], send_sem=send_sem,
                                     recv_sem=recv_sem, device_id=(x, y, 1 - c), device_id_type=MESH).wait_recv()
        mine.wait()

    return pl.pallas_call(
        body, name=name, in_specs=[_HBM], out_specs=_HBM,
        out_shape=jax.ShapeDtypeStruct((2, Rh, C), r.dtype),
        scratch_shapes=[pltpu.SemaphoreType.DMA, pltpu.SemaphoreType.DMA, pltpu.SemaphoreType.DMA],
    )(r)


def _rs_add_pair(g, recv, cidx, name):
    _, _, Rh, C = g.shape
    rows = 4 * Rh
    tm = _pick(rows, (400, 280, 200, 160, 80, 40, 16, 8))
    gv = g.reshape(2, rows, C)
    rv = recv.reshape(rows, C)

    def body(c_ref, g_ref, r_ref, o_ref):
        o_ref[...] = (g_ref[0].astype(F32) + r_ref[...].astype(F32)).astype(o_ref.dtype)

    out = pl.pallas_call(
        body, name=name,
        grid_spec=pltpu.PrefetchScalarGridSpec(
            num_scalar_prefetch=1, grid=(rows // tm,),
            in_specs=[pl.BlockSpec((1, tm, C), lambda i, cr: (cr[0], i, 0)), pl.BlockSpec((tm, C), lambda i, cr: (i, 0))],
            out_specs=pl.BlockSpec((tm, C), lambda i, cr: (i, 0))),
        out_shape=jax.ShapeDtypeStruct((rows, C), BF16),
    )(cidx, gv, rv)
    return out.reshape(4, Rh, C)


def _rs_add_chips(h, recv, chip_idx, name):
    _, Rh, C = h.shape
    tm = _pick(Rh, (400, 280, 200, 160, 80, 40, 16, 8))

    def body(c_ref, h_ref, r_ref, o_ref):
        acc = h_ref[0].astype(F32)
        for j in range(3):
            acc = acc + r_ref[j].astype(F32)
        o_ref[...] = acc

    return pl.pallas_call(
        body, name=name,
        grid_spec=pltpu.PrefetchScalarGridSpec(
            num_scalar_prefetch=1, grid=(Rh // tm,),
            in_specs=[pl.BlockSpec((1, tm, C), lambda i, cr: (cr[0], i, 0)), pl.BlockSpec((3, tm, C), lambda i, cr: (0, i, 0))],
            out_specs=pl.BlockSpec((tm, C), lambda i, cr: (i, 0))),
        out_shape=jax.ShapeDtypeStruct((Rh, C), F32),
    )(chip_idx, h, recv)


def _all_reduce_small(vec, name):
    n, C = vec.shape

    def body(v_ref, out_ref, buf, send_sems, recv_sems):
        x, y, c = _position()

        def flip(k):
            return ((1 - x) if k & 4 else x, (1 - y) if k & 2 else y, (1 - c) if k & 1 else c)

        def idx(p):
            return 4 * p[0] + 2 * p[1] + p[2]

        me = idx((x, y, c))
        buf[me] = v_ref[...]
        cps = [pltpu.make_async_remote_copy(src_ref=v_ref, dst_ref=buf.at[me], send_sem=send_sems.at[k - 1],
                                            recv_sem=recv_sems.at[k - 1], device_id=flip(k), device_id_type=MESH)
               for k in range(1, 8)]
        for cp in cps:
            cp.start()
        for k in range(1, 8):
            pltpu.make_async_remote_copy(src_ref=v_ref, dst_ref=buf.at[idx(flip(k))], send_sem=send_sems.at[k - 1],
                                         recv_sem=recv_sems.at[k - 1], device_id=flip(k), device_id_type=MESH).wait_recv()
        for cp in cps:
            cp.wait_send()
        acc = buf[0]
        for s in range(1, 8):
            acc = acc + buf[s]
        out_ref[...] = acc

    return pl.pallas_call(
        body, name=name,
        in_specs=[pl.BlockSpec(memory_space=pltpu.VMEM)], out_specs=pl.BlockSpec(memory_space=pltpu.VMEM),
        out_shape=jax.ShapeDtypeStruct((n, C), F32),
        scratch_shapes=[pltpu.VMEM((8, n, C), F32), pltpu.SemaphoreType.DMA((7,)), pltpu.SemaphoreType.DMA((7,))],
    )(vec)


BIG = (("w_in", (D, IN_WIDTH // 4), "cols"), ("w_a", (GW, D // 4), "cols"), ("pool_w", (4, PG // 4, PG), "pool"),
       ("w_b", (D // 4, D), "rows"), ("w_c", (D // 4, D), "rows"), ("w_o", (D // 4, D), "rows"),
       ("ffn_w_up", (D, 2 * D_FF // 4), "cols"), ("ffn_w_down", (D_FF // 4, D), "rows"))
PACK_ROWS = sum(math.prod(s) for _, s, _ in BIG) // D
PACK_PAD = -(-PACK_ROWS // 32) * 32


def _assemble(parts, how):
    if how == "cols":
        r, cs = parts.shape[1:]
        return parts.transpose(1, 0, 2).reshape(r, 4 * cs)
    if how == "rows":
        return parts.reshape(4 * parts.shape[1], parts.shape[2])
    return parts.transpose(1, 0, 2, 3).reshape(4, PG, PG)


def _split(full, how, block):
    if how == "cols":
        return full.reshape(block[0], 4, block[1]).transpose(1, 0, 2)
    if how == "rows":
        return full.reshape(4, *block)
    return full.reshape(4, 4, *block[1:]).transpose(1, 0, 2, 3)


def _pack_layer(blocks, dtype):
    lead = blocks["w_in"].shape[:-2]
    flat = [blocks[n].astype(dtype).reshape(*lead, -1, D) for n, _, _ in BIG]
    flat.append(jnp.zeros((*lead, PACK_PAD - PACK_ROWS, D), dtype))
    return jnp.concatenate(flat, axis=-2)


def _unpack_layer(pack):
    lead = pack.shape[:-2]
    out, r = {}, 0
    for n, s, _ in BIG:
        k = math.prod(s) // D
        out[n] = pack[..., r:r + k, :].reshape(*lead, *s)
        r += k
    return out


def _layer_fwd(x, w, sm, bias):
    u = _rms_fwd(x, sm["ln1_g"], "rms1")
    pa = _mm(u, w["in_a"], name="in_a")
    pb = _mm(u, w["in_b"], name="in_b")
    pc = _mm(u, w["in_c"], name="in_c")
    pd = _mm(u, w["in_d"], name="in_d")
    os_, ls_ = [], []
    for gi in range(3):
        o, l = _attn_fwd(pa, bias[gi], gi, "attn_fwd%d" % gi)
        os_.append(o)
        ls_.append(l)
    att = _mix_fwd(os_, ls_, "mix_fwd")
    ya = _mm(att, w["w_a"], name="mm_wa")
    pool_o = _pool_fwd(pb, w["pool_w"], sm["pool_scale"], "pool_fwd")
    yb = _mm(pool_o, w["w_b"], name="mm_wb")
    xbc_c = _ssd_conv_fwd(pc, sm["ssd_conv_w"], sm["ssd_conv_b"], "ssd_conv_fwd")
    y_scan, states = _ssd_scan_fwd(xbc_c, pd, sm["ssd_dt_bias"], sm["ssd_a_log"], "ssd_scan_fwd")
    ssd_o = _ssd_post_fwd(y_scan, xbc_c, pc, sm["ssd_d"], sm["ssd_norm_w"], "ssd_post_fwd")
    yc = _mm(ssd_o, w["w_c"], name="mm_wc")
    merged = _gates_fwd(pd, sm["b_gate"], ya, yb, yc, "gates_fwd")
    x1 = _mm(merged, w["w_o"], add=x, name="mm_wo")
    u2 = _rms_fwd(x1, sm["ln2_g"], "rms2")
    h = _mm(u2, w["ffn_w_up"], name="mm_up")
    f = _ffn_act_fwd(h, sm["ffn_conv_w"], sm["ffn_conv_b"], "ffn_act_fwd")
    x2 = _mm(f, w["ffn_w_down"], add=x1, name="mm_down")
    saved = dict(x=x, u=u, pa=pa, pb=pb, pc=pc, pd=pd, os=os_, ls=ls_, att=att, ya=ya, yb=yb, yc=yc, pool_o=pool_o,
                 xbc_c=xbc_c, y_scan=y_scan, states=states, ssd_o=ssd_o, merged=merged, x1=x1, u2=u2, h=h, f=f)
    return x2, saved


def _layer_bwd(dx2, w, sm, bias, onehots, sv):
    gw, gs = {}, {}
    df = _mm(dx2, w["ffn_w_down"], tb=True, name="d_f")
    gw["ffn_w_down"] = _mm(sv["f"], dx2, ta=True, name="g_down")
    dh, gs["ffn_conv_w"], gs["ffn_conv_b"] = _ffn_act_bwd(sv["h"], sm["ffn_conv_w"], sm["ffn_conv_b"], df, "ffn_act_bwd")
    du2 = _mm(dh, w["ffn_w_up"], tb=True, name="d_u2")
    gw["ffn_w_up"] = _mm(sv["u2"], dh, ta=True, name="g_up")
    dx1, gs["ln2_g"] = _rms_bwd(sv["x1"], sm["ln2_g"], du2, dx2, "rms2_bwd")
    dmerged = _mm(dx1, w["w_o"], tb=True, name="d_merged")
    gw["w_o"] = _mm(sv["merged"], dx1, ta=True, name="g_wo")
    dya, dyb, dyc, dgate_pd, gs["b_gate"] = _gates_bwd(
        sv["pd"], sm["b_gate"], sv["ya"], sv["yb"], sv["yc"], dmerged, "gates_bwd")
    dssd_o = _mm(dyc, w["w_c"], tb=True, name="d_ssd_o")
    gw["w_c"] = _mm(sv["ssd_o"], dyc, ta=True, name="g_wc")
    dy_scan, dxs_skip, dz, gs["ssd_d"], gs["ssd_norm_w"] = _ssd_post_bwd(
        sv["y_scan"], sv["xbc_c"], sv["pc"], sm["ssd_d"], sm["ssd_norm_w"], dssd_o, "ssd_post_bwd")
    dxbc_c, ddt, gs["ssd_dt_bias"], gs["ssd_a_log"] = _ssd_scan_bwd(
        sv["xbc_c"], sv["pd"], sm["ssd_dt_bias"], sm["ssd_a_log"], sv["states"], dy_scan, dxs_skip, "ssd_scan_bwd")
    dxbc, gs["ssd_conv_w"], gs["ssd_conv_b"] = _ssd_conv_bwd(sv["pc"], sm["ssd_conv_w"], sm["ssd_conv_b"], dxbc_c, "ssd_conv_bwd")
    dpc = jnp.concatenate([dz, dxbc], axis=1)
    dpool_o = _mm(dyb, w["w_b"], tb=True, name="d_pool_o")
    gw["w_b"] = _mm(sv["pool_o"], dyb, ta=True, name="g_wb")
    dpb, dpw, gs["pool_scale"] = _pool_bwd(sv["pb"], w["pool_w"], sm["pool_scale"], dpool_o, "pool_bwd")
    gw["pool_w"] = dpw.reshape(4, PG, PG)
    datt = _mm(dya, w["w_a"], tb=True, name="d_att")
    gw["w_a"] = _mm(sv["att"], dya, ta=True, name="g_wa")
    dos, dls = _mix_bwd(sv["os"], sv["ls"], datt, "mix_bwd")
    dqs, dks, dvs, dbs = [], [], [], []
    for gi in range(3):
        dq, dk, dv, db = _attn_bwd(sv["pa"], bias[gi], dos[gi], dls[gi], gi, "attn_bwd%d" % gi)
        dqs.append(dq)
        dks.append(dk)
        dvs.append(dv)
        dbs.append(_mm(db.reshape(6, WIN * 2 * WIN), onehots[gi], name="g_relb"))
    gs["rel_bias"] = jnp.concatenate(dbs, axis=0)
    dpa = jnp.concatenate(dqs + dks + dvs, axis=1)
    dpd = _add_dt(dgate_pd, ddt, "dpd_dt")
    u = sv["u"]
    du = _mm(dpa, w["in_a"], tb=True, name="d_u_a")
    du = _mm(dpb, w["in_b"], tb=True, add=du, name="d_u_b")
    du = _mm(dpc, w["in_c"], tb=True, add=du, name="d_u_c")
    du = _mm(dpd, w["in_d"], tb=True, add=du, name="d_u_d")
    g_in = jnp.concatenate([_mm(u, dpa, ta=True, name="g_in_a"), _mm(u, dpb, ta=True, name="g_in_b"),
                            _mm(u, dpc, ta=True, name="g_in_c"), _mm(u, dpd, ta=True, name="g_in_d")[:, :IN_WIDTH - SEC_A - SEC_B - SEC_C]],
                           axis=1)
    gw["w_in"] = g_in
    dx, gs["ln1_g"] = _rms_bwd(sv["x"], sm["ln1_g"], du, dx1, "rms1_bwd")
    return dx, gw, gs


def _add_dt(dpd, ddt, name):
    S = dpd.shape[0]

    def body(p_ref, d_ref, o_ref):
        o_ref[...] = p_ref[...]
        o_ref[:, :LANES] = (p_ref[:, :LANES].astype(F32) + d_ref[...]).astype(o_ref.dtype)

    tm = 256
    return pl.pallas_call(
        body, name=name, grid=(S // tm,),
        in_specs=[pl.BlockSpec((tm, SEC_D), lambda i: (i, 0)), pl.BlockSpec((tm, LANES), lambda i: (i, 0))],
        out_specs=pl.BlockSpec((tm, SEC_D), lambda i: (i, 0)),
        out_shape=jax.ShapeDtypeStruct((S, SEC_D), BF16),
    )(dpd, ddt)


SMALL_LAYER = ("ln1_g", "b_gate", "pool_scale", "ssd_conv_w", "ssd_conv_b", "ssd_dt_bias", "ssd_a_log", "ssd_d",
               "ssd_norm_w", "ln2_g", "ffn_conv_w", "ffn_conv_b")


def _pad_lanes(v):
    return jnp.pad(v, (0, LANES - v.shape[0])).reshape(1, LANES)


def _layer_weights(full):
    w_in = full["w_in"]
    o1, o2, o3 = SEC_A, SEC_A + SEC_B, SEC_A + SEC_B + SEC_C
    w = dict(full)
    w["in_a"] = w_in[:, :o1]
    w["in_b"] = w_in[:, o1:o2]
    w["in_c"] = w_in[:, o2:o3]
    w["in_d"] = jnp.pad(w_in[:, o3:], ((0, 0), (0, SEC_D - (IN_WIDTH - o3))))
    return w


def _layer_small(p, i):
    sm = {n: p[n][i] for n in SMALL_LAYER}
    out = {}
    for n, v in sm.items():
        if n in ("ssd_dt_bias", "ssd_a_log", "ssd_d"):
            out[n] = _pad_lanes(v)
        elif v.ndim == 1:
            out[n] = v.reshape(1, -1)
        else:
            out[n] = v
    return out


def _local_step(x, target, rel_bias, final_g, layer_full, small):
    nl = small["ln1_g"].shape[0]
    buckets = [_buckets(d) for d in DILATIONS]
    bias = [rel_bias[buckets[gi]][:, :, 6 * gi:6 * gi + 6].transpose(2, 0, 1).astype(F32) for gi in range(3)]
    onehots = [jnp.pad(jax.nn.one_hot(buckets[gi].reshape(-1), REL_BUCKETS, dtype=BF16), ((0, 0), (0, LANES - REL_BUCKETS)))
               for gi in range(3)]
    saved, ws, sms = [], [], []
    h = x
    for i in range(nl):
        w = _layer_weights(layer_full(i))
        sm = _layer_small(small, i)
        h, sv = _layer_fwd(h, w, sm, bias)
        saved.append(sv)
        ws.append(w)
        sms.append(sm)
    dh, dfinal, loss = _final_loss(h, target, final_g.reshape(1, D))
    gws, gss = [None] * nl, [None] * nl
    drel = jnp.zeros((18, LANES), F32)
    for i in reversed(range(nl)):
        dh, gws[i], gss[i] = _layer_bwd(dh, ws[i], sms[i], bias, onehots, saved[i])
        drel = drel + gss[i].pop("rel_bias")
    return loss, dh, gws, gss, dfinal, drel


WEIGHTS = ("rel_bias", "ln1_g", "w_in", "b_gate", "w_a", "pool_w", "pool_scale", "w_b", "ssd_conv_w", "ssd_conv_b",
           "ssd_dt_bias", "ssd_a_log", "ssd_d", "ssd_norm_w", "w_c", "w_o", "ln2_g", "ffn_w_up", "ffn_conv_w",
           "ffn_conv_b", "ffn_w_down", "final_g")
BIG_NAMES = tuple(n for n, _, _ in BIG)
SHARDED_SMALL = {"ssd_conv_w": XBC // 4, "ffn_conv_w": 2 * D_FF // 4}


def _to_rows(flat):
    n = flat.shape[0]
    rows = -(-n // LANES)
    rows = -(-rows // 8) * 8
    return jnp.pad(flat, (0, rows * LANES - n)).reshape(rows, LANES)


def _flatten(tree, names):
    return jnp.concatenate([tree[n].reshape(-1) for n in names])


def _unflatten(flat, shapes, names):
    out, o = {}, 0
    for n in names:
        k = math.prod(shapes[n])
        out[n] = flat[o:o + k].reshape(shapes[n])
        o += k
    return out


def kernel(x, rel_bias, ln1_g, w_in, b_gate, w_a, pool_w, pool_scale, w_b, ssd_conv_w, ssd_conv_b, ssd_dt_bias, ssd_a_log, ssd_d, ssd_norm_w, w_c, w_o, ln2_g, ffn_w_up, ffn_conv_w, ffn_conv_b, ffn_w_down, final_g, loss_target, m_rel_bias, m_ln1_g, m_w_in, m_b_gate, m_w_a, m_pool_w, m_pool_scale, m_w_b, m_ssd_conv_w, m_ssd_conv_b, m_ssd_dt_bias, m_ssd_a_log, m_ssd_d, m_ssd_norm_w, m_w_c, m_w_o, m_ln2_g, m_ffn_w_up, m_ffn_conv_w, m_ffn_conv_b, m_ffn_w_down, m_final_g, v_rel_bias, v_ln1_g, v_w_in, v_b_gate, v_w_a, v_pool_w, v_pool_scale, v_w_b, v_ssd_conv_w, v_ssd_conv_b, v_ssd_dt_bias, v_ssd_a_log, v_ssd_d, v_ssd_norm_w, v_w_c, v_w_o, v_ln2_g, v_ffn_w_up, v_ffn_conv_w, v_ffn_conv_b, v_ffn_w_down, v_final_g):
    W = dict(rel_bias=rel_bias, ln1_g=ln1_g, w_in=w_in, b_gate=b_gate, w_a=w_a, pool_w=pool_w, pool_scale=pool_scale,
             w_b=w_b, ssd_conv_w=ssd_conv_w, ssd_conv_b=ssd_conv_b, ssd_dt_bias=ssd_dt_bias, ssd_a_log=ssd_a_log,
             ssd_d=ssd_d, ssd_norm_w=ssd_norm_w, w_c=w_c, w_o=w_o, ln2_g=ln2_g, ffn_w_up=ffn_w_up,
             ffn_conv_w=ffn_conv_w, ffn_conv_b=ffn_conv_b, ffn_w_down=ffn_w_down, final_g=final_g)
    M = dict(rel_bias=m_rel_bias, ln1_g=m_ln1_g, w_in=m_w_in, b_gate=m_b_gate, w_a=m_w_a, pool_w=m_pool_w,
             pool_scale=m_pool_scale, w_b=m_w_b, ssd_conv_w=m_ssd_conv_w, ssd_conv_b=m_ssd_conv_b,
             ssd_dt_bias=m_ssd_dt_bias, ssd_a_log=m_ssd_a_log, ssd_d=m_ssd_d, ssd_norm_w=m_ssd_norm_w, w_c=m_w_c,
             w_o=m_w_o, ln2_g=m_ln2_g, ffn_w_up=m_ffn_w_up, ffn_conv_w=m_ffn_conv_w, ffn_conv_b=m_ffn_conv_b,
             ffn_w_down=m_ffn_w_down, final_g=m_final_g)
    V = dict(rel_bias=v_rel_bias, ln1_g=v_ln1_g, w_in=v_w_in, b_gate=v_b_gate, w_a=v_w_a, pool_w=v_pool_w,
             pool_scale=v_pool_scale, w_b=v_w_b, ssd_conv_w=v_ssd_conv_w, ssd_conv_b=v_ssd_conv_b,
             ssd_dt_bias=v_ssd_dt_bias, ssd_a_log=v_ssd_a_log, ssd_d=v_ssd_d, ssd_norm_w=v_ssd_norm_w, w_c=v_w_c,
             w_o=v_w_o, ln2_g=v_ln2_g, ffn_w_up=v_ffn_w_up, ffn_conv_w=v_ffn_conv_w, ffn_conv_b=v_ffn_conv_b,
             ffn_w_down=v_ffn_w_down, final_g=v_final_g)
    nl = ln1_g.shape[0]
    px, py, pc_ = _position()
    chip = 2 * px + py
    cidx = jnp.reshape(pc_, (1,)).astype(jnp.int32)
    chip_idx = jnp.reshape(chip, (1,)).astype(jnp.int32)

    placed = {}
    for n, cs in SHARDED_SMALL.items():
        full = jnp.zeros(W[n].shape[:-1] + (4 * cs,), F32)
        full = lax.dynamic_update_slice(full, W[n], (0, 0, chip * cs))
        placed[n] = jnp.where(pc_ == 0, full, 0.0)
    names_sh = tuple(SHARDED_SMALL)
    shapes_sh = {n: placed[n].shape for n in names_sh}
    got = _all_reduce_small(_to_rows(_flatten(placed, names_sh)), "gather_small")
    small = {n: W[n] for n in SMALL_LAYER}
    small.update(_unflatten(got.reshape(-1), shapes_sh, names_sh))

    packs = _pack_layer({n: W[n] for n in BIG_NAMES}, BF16)

    def layer_full(i):
        allp = _gather_weights(packs[i], "gather_w")
        parts = _unpack_layer(allp)
        return {n: _assemble(parts[n], how) for n, _, how in BIG}

    loss, dx, gws, gss, dfinal, drel = _local_step(x[0], loss_target[0], rel_bias, final_g, layer_full, small)

    grads = {}
    red = []
    for i in range(nl):
        blocks = {n: _split(gws[i][n], how, s) for n, s, how in BIG}
        g = _pack_layer(blocks, BF16)
        g = g.reshape(4, 2, PACK_PAD // 2, D).transpose(1, 0, 2, 3)
        recv = _rs_pair_exchange(g, "rs_pair")
        hsum = _rs_add_pair(g, recv, cidx, "rs_add_pair")
        recv3 = _rs_chip_exchange(hsum, "rs_chips")
        r = _rs_add_chips(hsum, recv3, chip_idx, "rs_add_chips")
        both = _rs_share(r, "rs_share")
        red.append(_unpack_layer(both.reshape(PACK_PAD, D)))
    for n in BIG_NAMES:
        grads[n] = jnp.stack([red[i][n] for i in range(nl)], axis=0)

    sg = {}
    for n in SMALL_LAYER:
        sg[n] = jnp.stack([gss[i][n] for i in range(nl)], axis=0)
    for n in ("ssd_dt_bias", "ssd_a_log", "ssd_d"):
        sg[n] = sg[n][:, 0, :SSD_HEADS]
    sg["rel_bias"] = drel[:, :REL_BUCKETS].T
    sg["final_g"] = dfinal.reshape(D)
    sg["loss"] = loss[0, :1]
    names_sg = tuple(sg)
    shapes_sg = {n: ((nl,) + W[n].shape[1:] if n in SMALL_LAYER and n not in SHARDED_SMALL else
                     (placed[n].shape if n in SHARDED_SMALL else sg[n].shape)) for n in names_sg}
    for n in names_sg:
        sg[n] = sg[n].reshape(shapes_sg[n])
    tot = _all_reduce_small(_to_rows(_flatten(sg, names_sg)), "allreduce_small")
    tot = _unflatten(tot.reshape(-1), shapes_sg, names_sg)
    loss_out = tot.pop("loss").reshape(())
    for n, cs in SHARDED_SMALL.items():
        tot[n] = lax.dynamic_slice(tot[n], (0, 0, chip * cs), tot[n].shape[:-1] + (cs,))
    grads.update(tot)

    delta, new_m, new_v = {}, {}, {}
    for n in BIG_NAMES:
        shp = W[n].shape
        r2 = lambda a: a.reshape(-1, shp[-1])
        dl, m2, v2 = _adamw(r2(W[n]), r2(grads[n]), r2(M[n]), r2(V[n]), "adamw_" + n)
        delta[n], new_m[n], new_v[n] = dl.reshape(shp), m2.reshape(shp), v2.reshape(shp)
    names_s = tuple(n for n in WEIGHTS if n not in BIG_NAMES)
    shapes_s = {n: W[n].shape for n in names_s}
    pk = lambda t: _to_rows(_flatten(t, names_s))
    dl, m2, v2 = _adamw(pk(W), pk(grads), pk(M), pk(V), "adamw_small")
    delta.update(_unflatten(dl.reshape(-1), shapes_s, names_s))
    new_m.update(_unflatten(m2.reshape(-1), shapes_s, names_s))
    new_v.update(_unflatten(v2.reshape(-1), shapes_s, names_s))

    return (loss_out, dx[None], *[grads[n] for n in WEIGHTS], *[delta[n] for n in WEIGHTS],
            *[new_m[n] for n in WEIGHTS], *[new_v[n] for n in WEIGHTS])
```

```python
import functools
import math

import jax
import jax.numpy as jnp
from jax import lax
from jax.experimental import pallas as pl
from jax.experimental.pallas import tpu as pltpu

F32 = jnp.float32
BF16 = jnp.bfloat16
MESH = pl.DeviceIdType.MESH

D = 1024
HD = 64
GW = 384
AW = 3 * GW
WIN = 128
DILATIONS = (1, 4, 16)
REL_BUCKETS = 32
REL_MAX_DISTANCE = 2048
POOL_WINDOWS = (2, 4, 8, 16)
PG = 256
SSD_HEADS = 16
SSD_N = 128
SSD_CHUNK = 128
XBC = 1536
D_FF = 2816
EPS = 1e-6
NEG = -1e30
HALO = 16
LANES = 128

SEC_A = 3 * AW
SEC_B = D
SEC_C = D + XBC
SEC_D = 3200
IN_WIDTH = SEC_A + SEC_B + SEC_C + 16 + 3 * D

ADAM_LR = 0.001
ADAM_B1 = 0.9
ADAM_B2 = 0.999
ADAM_EPS = 1e-08
ADAM_WD = 0.01
ADAM_STEP = 10
ADAM_TILE = 256 * 1024
MM_TILE_BYTES = 12 * 1024 * 1024


def _pick(d, cands):
    for t in cands:
        if d % t == 0:
            return t
    return d


def _iota(shape, dim):
    return lax.broadcasted_iota(jnp.int32, shape, dim)


def _dg(a, b, ca, cb):
    return lax.dot_general(a.astype(BF16), b.astype(BF16), (((ca,), (cb,)), ((), ())),
                           preferred_element_type=F32)


@jax.custom_vjp
def _bdot_nn(a, b):
    return _dg(a, b, 1, 0)


def _nn_fwd(a, b):
    return _dg(a, b, 1, 0), (a, b)


def _nn_bwd(res, g):
    a, b = res
    return _dg(g, b, 1, 1), _dg(a, g, 0, 0)


_bdot_nn.defvjp(_nn_fwd, _nn_bwd)


@jax.custom_vjp
def _bdot_nt(a, b):
    return _dg(a, b, 1, 1)


def _nt_fwd(a, b):
    return _dg(a, b, 1, 1), (a, b)


def _nt_bwd(res, g):
    a, b = res
    return _dg(g, b, 1, 0), _dg(g, a, 0, 0)


_bdot_nt.defvjp(_nt_fwd, _nt_bwd)


@jax.custom_vjp
def _bdot_tn(a, b):
    return _dg(a, b, 0, 0)


def _tn_fwd(a, b):
    return _dg(a, b, 0, 0), (a, b)


def _tn_bwd(res, g):
    a, b = res
    return _dg(b, g, 1, 1), _dg(a, g, 1, 0)


_bdot_tn.defvjp(_tn_fwd, _tn_bwd)


def _fdot(a, b):
    return jnp.dot(a, b, preferred_element_type=F32, precision=lax.Precision.HIGHEST)


def _sigmoid(x):
    return 1.0 / (1.0 + jnp.exp(-x))


def _silu(x):
    return x * _sigmoid(x)


def _softplus(x):
    return jnp.maximum(x, 0.0) + jnp.log(1.0 + jnp.exp(-jnp.abs(x)))


def _lane_pick(m, h):
    return jnp.sum(jnp.where(_iota(m.shape, 1) == h, m, 0.0), axis=1, keepdims=True)


def _row_pick(m, h):
    return jnp.sum(jnp.where(_iota(m.shape, 0) == h, m, 0.0), axis=0, keepdims=True)


def _stack_rows(rows, n):
    c = rows[0].shape[1]
    r = _iota((n, c), 0)
    out = jnp.zeros((n, c), F32)
    for k, v in enumerate(rows):
        out = out + jnp.where(r == k, v, 0.0)
    return out


def _mm(a, b, *, ta=False, tb=False, add=None, out_dtype=F32, name):
    if ta:
        K, M = a.shape
    else:
        M, K = a.shape
    if tb:
        N, Kb = b.shape
    else:
        Kb, N = b.shape
    assert K == Kb, (a.shape, b.shape, ta, tb)
    tm = _pick(M, (512, 640, 384, 256, 128))
    tn = _pick(N, (512, 640, 384, 256, 128))
    per_k = tm * a.dtype.itemsize + tn * b.dtype.itemsize
    tk = _pick(K, [t for t in (K, 2048, 1024, 512, 640, 384, 256, 128) if t * per_k <= MM_TILE_BYTES])
    nk = K // tk
    ca = 0 if ta else 1
    cb = 1 if tb else 0

    def body(*refs):
        if add is None:
            a_ref, b_ref, o_ref = refs[:3]
            add_ref = None
        else:
            a_ref, b_ref, add_ref, o_ref = refs[:4]
        part = _dg(a_ref[...], b_ref[...], ca, cb)

        def finish(r):
            if add_ref is not None:
                r = r + add_ref[...].astype(F32)
            o_ref[...] = r.astype(o_ref.dtype)

        if nk == 1:
            finish(part)
            return
        acc_ref = refs[-1]
        k = pl.program_id(2)

        @pl.when(k == 0)
        def _():
            acc_ref[...] = part

        @pl.when((k > 0) & (k < nk - 1))
        def _():
            acc_ref[...] += part

        @pl.when(k == nk - 1)
        def _():
            finish(acc_ref[...] + part)

    a_spec = pl.BlockSpec((tk, tm), lambda i, j, k: (k, i)) if ta else pl.BlockSpec((tm, tk), lambda i, j, k: (i, k))
    b_spec = pl.BlockSpec((tn, tk), lambda i, j, k: (j, k)) if tb else pl.BlockSpec((tk, tn), lambda i, j, k: (k, j))
    in_specs = [a_spec, b_spec]
    args = [a, b]
    if add is not None:
        in_specs.append(pl.BlockSpec((tm, tn), lambda i, j, k: (i, j)))
        args.append(add)
    return pl.pallas_call(
        body, name=name, grid=(M // tm, N // tn, nk), in_specs=in_specs,
        out_specs=pl.BlockSpec((tm, tn), lambda i, j, k: (i, j)),
        out_shape=jax.ShapeDtypeStruct((M, N), out_dtype),
        scratch_shapes=[pltpu.VMEM((tm, tn), F32)] if nk > 1 else [],
        compiler_params=pltpu.CompilerParams(dimension_semantics=("parallel", "parallel", "arbitrary")),
    )(*args)


def _rows(name, fn, ins, outs, accs=(), *, tm, nrows, ncol=1):
    nt = nrows // tm
    hb = tm // HALO
    nh = nrows // HALO
    in_specs, args = [], []
    for kind, arr, cw, base in ins:
        if kind == "row":
            cw = arr.shape[1] if cw is None else cw
            in_specs.append(pl.BlockSpec((tm, cw), lambda j, i, base=base: (i, base + j)))
        elif kind == "prev":
            in_specs.append(pl.BlockSpec((HALO, cw), lambda j, i, base=base: (jnp.maximum(i * hb - 1, 0), base + j)))
        elif kind == "next":
            in_specs.append(pl.BlockSpec((HALO, cw), lambda j, i, base=base: (jnp.minimum((i + 1) * hb, nh - 1), base + j)))
        elif kind == "const":
            in_specs.append(pl.BlockSpec(arr.shape, lambda j, i, nd=arr.ndim: (0,) * nd))
        elif kind == "ccol":
            in_specs.append(pl.BlockSpec((arr.shape[0], cw), lambda j, i, base=base: (0, base + j)))
        else:
            raise ValueError(kind)
        args.append(arr)
    out_specs, out_shape = [], []
    for ctot, cw, base, dt in outs:
        out_specs.append(pl.BlockSpec((tm, cw), lambda j, i, base=base: (i, base + j)))
        out_shape.append(jax.ShapeDtypeStruct((nrows, ctot), dt))
    for r, ctot, cw in accs:
        out_specs.append(pl.BlockSpec((r, cw), lambda j, i: (0, j)))
        out_shape.append(jax.ShapeDtypeStruct((r, ctot), F32))
    n_in, n_out = len(ins), len(outs)

    def body(*refs):
        j = pl.program_id(0)
        i = pl.program_id(1)
        res = fn(i, j, *[r[...] for r in refs[:n_in]])
        for r, v in zip(refs[n_in:n_in + n_out], res[:n_out]):
            r[...] = v.astype(r.dtype)
        for r, v in zip(refs[n_in + n_out:], res[n_out:]):
            @pl.when(i == 0)
            def _(r=r, v=v):
                r[...] = v

            @pl.when(i > 0)
            def _(r=r, v=v):
                r[...] += v

    res = pl.pallas_call(
        body, name=name, grid=(ncol, nt), in_specs=in_specs, out_specs=out_specs, out_shape=out_shape,
        compiler_params=pltpu.CompilerParams(dimension_semantics=("arbitrary", "arbitrary")),
    )(*args)
    return res


def _shift_down(xcat, k):
    return xcat if k == 0 else pltpu.roll(xcat, k, 0)


def _shift_up(xcat, k):
    return xcat if k == 0 else pltpu.roll(xcat, xcat.shape[0] - k, 0)


def _with_prev(i, halo, x):
    return jnp.concatenate([jnp.where(i == 0, 0.0, halo), x], axis=0)


def _with_next(i, nt, x, halo):
    return jnp.concatenate([x, jnp.where(i == nt - 1, 0.0, halo)], axis=0)


def _rms_core(x, g):
    r = lax.rsqrt(jnp.mean(x * x, axis=-1, keepdims=True) + EPS)
    return x * r * g


def _rms_fwd(x, g, name):
    S = x.shape[0]
    return _rows(name, lambda i, j, xv, gv: [_rms_core(xv, gv)],
                 [("row", x, None, 0), ("const", g, None, 0)], [(D, D, 0, BF16)], tm=256, nrows=S)[0]


def _rms_bwd(x, g, du, dres, name):
    S = x.shape[0]

    def fn(i, j, xv, gv, duv, drv):
        _, vjp = jax.vjp(_rms_core, xv, gv)
        dx, dg = vjp(duv)
        return [drv + dx, dg]

    return _rows(name, fn, [("row", x, None, 0), ("const", g, None, 0), ("row", du, None, 0), ("row", dres, None, 0)],
                 [(D, D, 0, F32)], [(1, D, D)], tm=256, nrows=S)


def _final_loss(x, target, g):
    S = x.shape[0]

    def fn(i, j, xv, tv, gv):
        def f(xx, gg):
            err = _rms_core(xx, gg) - tv
            return 0.5 * jnp.sum(err * err) / D

        loss, vjp = jax.vjp(f, xv, gv)
        dx, dg = vjp(jnp.ones((), F32))
        return [dx, dg, jnp.zeros((1, LANES), F32) + loss]

    return _rows("final_loss", fn, [("row", x, None, 0), ("row", target, None, 0), ("const", g, None, 0)],
                 [(D, D, 0, F32)], [(1, D, D), (1, LANES, LANES)], tm=256, nrows=S)


def _attn_valid(n):
    qi = _iota((WIN, 2 * WIN), 0)
    kk = _iota((WIN, 2 * WIN), 1)
    rel = qi + WIN - kk
    return (rel >= 0) & (rel <= WIN) & ((kk >= WIN) | (n > 0))


def _attn_block(q, kp, kc, vp, vc, b0, b1, valid):
    k = jnp.concatenate([kp, kc], axis=0)
    v = jnp.concatenate([vp, vc], axis=0)
    lo = _iota((WIN, LANES), 1) < HD
    scale = 1.0 / math.sqrt(HD)
    os_, ls_ = [], []
    for hh, b in ((0, b0), (1, b1)):
        qm = jnp.where(lo if hh == 0 else ~lo, q, 0.0)
        s = _bdot_nt(qm, k) * scale + b
        s = jnp.where(valid, s, NEG)
        m = lax.stop_gradient(jnp.max(s, axis=1, keepdims=True))
        p = jnp.exp(s - m)
        l = jnp.sum(p, axis=1, keepdims=True)
        os_.append(_bdot_nn(p, v) / l)
        ls_.append(m + jnp.log(l))
    return jnp.where(lo, os_[0], os_[1]), jnp.where(lo, ls_[0], ls_[1])


def _residue_rows(r, d):
    return pl.ds(0, WIN) if d == 1 else pl.ds(r, WIN, stride=d)


def _for_residues(d, fn):
    if d == 1:
        fn(0, 0)
    else:
        lax.fori_loop(0, d, fn, 0)


def _bias_table(rel_bias, bucket, gi, name):
    def body(t_ref, b_ref, o_ref):
        h = 6 * gi + pl.program_id(0)
        b = b_ref[...]
        acc = jnp.zeros(b.shape, F32)
        for k in range(REL_BUCKETS):
            acc = jnp.where(b == k, t_ref[k, h], acc)
        o_ref[0] = acc

    return pl.pallas_call(
        body, name=name, grid=(6,),
        in_specs=[pl.BlockSpec(memory_space=pltpu.SMEM), pl.BlockSpec((WIN, 2 * WIN), lambda h: (0, 0))],
        out_specs=pl.BlockSpec((1, WIN, 2 * WIN), lambda h: (h, 0, 0)),
        out_shape=jax.ShapeDtypeStruct((6, WIN, 2 * WIN), F32),
    )(rel_bias, bucket)


def _attn_fwd(pa, bias, gi, name):
    S = pa.shape[0]
    d = DILATIONS[gi]
    bt = WIN * d
    nb = S // bt
    qb = 3 * gi

    def body(q_ref, kp_ref, kc_ref, vp_ref, vc_ref, b_ref, o_ref, l_ref):
        valid = _attn_valid(pl.program_id(1))
        b0, b1 = b_ref[0], b_ref[1]

        def residue(r, carry):
            sl = _residue_rows(r, d)
            o, lse = _attn_block(q_ref[sl, :], kp_ref[sl, :], kc_ref[sl, :], vp_ref[sl, :], vc_ref[sl, :], b0, b1, valid)
            o_ref[sl, :] = o
            l_ref[sl, :] = lse
            return carry

        _for_residues(d, residue)

    def spec(off, prev):
        if prev:
            return pl.BlockSpec((bt, LANES), lambda hp, n: (jnp.maximum(n - 1, 0), off + qb + hp))
        return pl.BlockSpec((bt, LANES), lambda hp, n: (n, off + qb + hp))

    ospec = pl.BlockSpec((bt, LANES), lambda hp, n: (n, hp))
    return pl.pallas_call(
        body, name=name, grid=(3, nb),
        in_specs=[spec(0, False), spec(9, True), spec(9, False), spec(18, True), spec(18, False),
                  pl.BlockSpec((2, WIN, 2 * WIN), lambda hp, n: (hp, 0, 0))],
        out_specs=[ospec, ospec],
        out_shape=[jax.ShapeDtypeStruct((S, GW), F32)] * 2,
        compiler_params=pltpu.CompilerParams(dimension_semantics=("parallel", "arbitrary")),
    )(pa, pa, pa, pa, pa, bias)


def _attn_bwd(pa, bias, do, dlse, db_in, dqkv, gi, name):
    S = pa.shape[0]
    d = DILATIONS[gi]
    bt = WIN * d
    nb = S // bt
    qb = 3 * gi

    def body(q_ref, kp_ref, kc_ref, vp_ref, vc_ref, b_ref, do_ref, dl_ref, dbi_ref, dqi_ref, dki_ref, dvi_ref,
             dq_ref, dk_ref, dv_ref, db_ref, ck, cv):
        n = pl.program_id(1)

        @pl.when(n == 0)
        def _():
            db_ref[...] = dbi_ref[...]
            ck[...] = jnp.zeros_like(ck)
            cv[...] = jnp.zeros_like(cv)

        @pl.when(n < nb)
        def _():
            f = functools.partial(_attn_block, valid=_attn_valid(n))
            b0, b1 = b_ref[0], b_ref[1]

            def residue(r, carry):
                sl = _residue_rows(r, d)
                cs = pl.ds(pl.multiple_of(r * WIN, WIN), WIN)
                _, vjp = jax.vjp(f, q_ref[sl, :], kp_ref[sl, :], kc_ref[sl, :], vp_ref[sl, :], vc_ref[sl, :], b0, b1)
                dq, dkp, dkc, dvp, dvc, db0, db1 = vjp((do_ref[sl, :].astype(F32), dl_ref[sl, :]))
                dq_ref[sl, :] = dq
                dk_ref[sl, :] = ck[cs, :] + dkp
                dv_ref[sl, :] = cv[cs, :] + dvp
                ck[cs, :] = dkc
                cv[cs, :] = dvc
                db_ref[0] += db0
                db_ref[1] += db1
                return carry

            _for_residues(d, residue)

        @pl.when(n == nb)
        def _():
            def residue(r, carry):
                sl = _residue_rows(r, d)
                cs = pl.ds(pl.multiple_of(r * WIN, WIN), WIN)
                dk_ref[sl, :] = ck[cs, :]
                dv_ref[sl, :] = cv[cs, :]
                return carry

            _for_residues(d, residue)

    def cur(n):
        return jnp.minimum(n, nb - 1)

    def spec(off, prev):
        if prev:
            return pl.BlockSpec((bt, LANES), lambda hp, n: (jnp.maximum(cur(n) - 1, 0), off + qb + hp))
        return pl.BlockSpec((bt, LANES), lambda hp, n: (cur(n), off + qb + hp))

    gspec = pl.BlockSpec((bt, LANES), lambda hp, n: (cur(n), hp))
    bspec = pl.BlockSpec((2, WIN, 2 * WIN), lambda hp, n: (hp, 0, 0))
    anyspec = pl.BlockSpec(memory_space=pl.ANY)
    qspec = pl.BlockSpec((bt, LANES), lambda hp, n: (cur(n), qb + hp))
    kspec = pl.BlockSpec((bt, LANES), lambda hp, n: (jnp.maximum(n - 1, 0), qb + hp))
    dq, dk, dv, db = pl.pallas_call(
        body, name=name, grid=(3, nb + 1),
        in_specs=[spec(0, False), spec(9, True), spec(9, False), spec(18, True), spec(18, False),
                  bspec, gspec, gspec, bspec, anyspec, anyspec, anyspec],
        out_specs=[qspec, kspec, kspec, bspec],
        out_shape=[jax.ShapeDtypeStruct((S, AW), F32)] * 3 + [jax.ShapeDtypeStruct((6, WIN, 2 * WIN), F32)],
        scratch_shapes=[pltpu.VMEM((bt, LANES), F32), pltpu.VMEM((bt, LANES), F32)],
        input_output_aliases={9: 0, 10: 1, 11: 2},
        compiler_params=pltpu.CompilerParams(dimension_semantics=("arbitrary", "arbitrary")),
    )(pa, pa, pa, pa, pa, bias, do, dlse, db_in, *dqkv)
    return (dq, dk, dv), db


def _mix_core(o0, o1, o2, l0, l1, l2):
    m = lax.stop_gradient(jnp.maximum(jnp.maximum(l0, l1), l2))
    e0, e1, e2 = jnp.exp(l0 - m), jnp.exp(l1 - m), jnp.exp(l2 - m)
    return (e0 * o0 + e1 * o1 + e2 * o2) / (e0 + e1 + e2)


def _mix_fwd(os_, ls_, name):
    S = os_[0].shape[0]
    ins = [("row", a, None, 0) for a in (*os_, *ls_)]
    return _rows(name, lambda i, j, *v: [_mix_core(*v)], ins, [(GW, GW, 0, BF16)], tm=256, nrows=S)[0]


def _mix_bwd(os_, ls_, datt, name):
    S = datt.shape[0]

    def fn(i, j, *v):
        _, vjp = jax.vjp(_mix_core, *v[:6])
        return list(vjp(v[6]))

    ins = [("row", a, None, 0) for a in (*os_, *ls_, datt)]
    outs = [(GW, GW, 0, F32)] * 6
    r = _rows(name, fn, ins, outs, tm=256, nrows=S)
    return r[:3], r[3:]


def _t5_bucket(dist):
    max_exact = REL_BUCKETS // 2
    is_small = dist < max_exact
    nf = jnp.maximum(dist, 1).astype(F32)
    large = max_exact + (jnp.log(nf / max_exact) / math.log(REL_MAX_DISTANCE / max_exact)
                         * (REL_BUCKETS - max_exact)).astype(jnp.int32)
    large = jnp.minimum(large, REL_BUCKETS - 1)
    return jnp.where(is_small, dist, large)


def _buckets(d):
    qi = jnp.arange(WIN)[:, None]
    kk = jnp.arange(2 * WIN)[None, :]
    rel = qi + WIN - kk
    return _t5_bucket(jnp.clip(rel, 0, None) * d)


def _pool_cnt(i, tm, w):
    pos = i * tm + _iota((tm, PG), 0) + 1
    return jnp.minimum(pos, w).astype(F32)


def _pool_d(i, tm, halo, u):
    ds = []
    for g, w in enumerate(POOL_WINDOWS):
        ug = u[:, g * PG:(g + 1) * PG]
        s = _with_prev(i, halo[:, g * PG:(g + 1) * PG], ug)
        step = 1
        while step < w:
            s = s + _shift_down(s, step)
            step *= 2
        ds.append(s[HALO:] / _pool_cnt(i, tm, w) - ug)
    return ds


def _pool_lin(d0, d1, d2, d3, w0, w1, w2, w3, scale):
    y = jnp.concatenate([_bdot_nn(d0, w0), _bdot_nn(d1, w1), _bdot_nn(d2, w2), _bdot_nn(d3, w3)], axis=1)
    return y * scale


def _pool_fwd(pb, pw, scale, name):
    S = pb.shape[0]
    tm = 256

    def fn(i, j, halo, u, w, sc):
        ds = _pool_d(i, tm, halo, u)
        return [_pool_lin(*ds, *[w[k].astype(F32) for k in range(4)], sc)]

    return _rows(name, fn, [("prev", pb, D, 0), ("row", pb, None, 0), ("const", pw, None, 0), ("const", scale, None, 0)],
                 [(D, D, 0, BF16)], tm=tm, nrows=S)[0]


def _pool_bwd(pb, pw, scale, dpo, name):
    S = pb.shape[0]
    tm = 256
    nt = S // tm

    def fn1(i, j, halo, u, w, sc, dy):
        ds = _pool_d(i, tm, halo, u)
        _, vjp = jax.vjp(_pool_lin, *ds, *[w[k].astype(F32) for k in range(4)], sc)
        g = vjp(dy)
        e = jnp.concatenate([g[k] / _pool_cnt(i, tm, wd) for k, wd in enumerate(POOL_WINDOWS)], axis=1)
        return [e, jnp.concatenate(g[4:8], axis=0), g[8]]

    e, dpw, dsc = _rows(name + "_a", fn1,
                        [("prev", pb, D, 0), ("row", pb, None, 0), ("const", pw, None, 0), ("const", scale, None, 0),
                         ("row", dpo, None, 0)],
                        [(D, D, 0, F32)], [(4 * PG, PG, PG), (1, D, D)], tm=tm, nrows=S)

    def fn2(i, j, ev, halo):
        outs = []
        for g, w in enumerate(POOL_WINDOWS):
            eg = ev[:, g * PG:(g + 1) * PG]
            s = _with_next(i, nt, eg, halo[:, g * PG:(g + 1) * PG])
            step = 1
            while step < w:
                s = s + _shift_up(s, step)
                step *= 2
            outs.append(s[:tm] - eg * _pool_cnt(i, tm, w))
        return [jnp.concatenate(outs, axis=1)]

    du = _rows(name + "_b", fn2, [("row", e, None, 0), ("next", e, D, 0)], [(D, D, 0, BF16)], tm=tm, nrows=S)[0]
    return du, dpw, dsc


def _conv_taps(i, halo, x, K):
    cat = _with_prev(i, halo, x)
    return [_shift_down(cat, K - 1 - k)[HALO:] for k in range(K)]


def _conv_pre(taps, w, b):
    acc = b
    for k, t in enumerate(taps):
        acc = acc + t * _row_pick(w, k)
    return acc


def _conv_t(name, dpre, w, K, ncol, cw, out_dtype):
    S, C = dpre.shape
    tm = 256
    nt = S // tm

    def fn(i, j, dp, halo, wv):
        cat = _with_next(i, nt, dp, halo)
        acc = jnp.zeros((tm, cw), F32)
        for k in range(K):
            acc = acc + _shift_up(cat, K - 1 - k)[:tm] * _row_pick(wv, k)
        return [acc]

    return _rows(name, fn, [("row", dpre, cw, 0), ("next", dpre, cw, 0), ("ccol", w, cw, 0)],
                 [(C, cw, 0, out_dtype)], tm=tm, nrows=S, ncol=ncol)[0]


CW = 256


def _ssd_conv_fwd(pc, w, b, name):
    S = pc.shape[0]
    base = D // CW

    def fn(i, j, halo, x, wv, bv):
        return [_silu(_conv_pre(_conv_taps(i, halo, x, 4), wv, bv))]

    return _rows(name, fn, [("prev", pc, CW, base), ("row", pc, CW, base), ("ccol", w, CW, 0), ("ccol", b, CW, 0)],
                 [(XBC, CW, 0, F32)], tm=256, nrows=S, ncol=XBC // CW)[0]


def _ssd_conv_bwd(pc, w, b, dy, name):
    S = pc.shape[0]
    base = D // CW

    def fn(i, j, halo, x, wv, bv, dyv):
        taps = _conv_taps(i, halo, x, 4)
        pre = _conv_pre(taps, wv, bv)
        sg = _sigmoid(pre)
        dpre = dyv * sg * (1.0 + pre * (1.0 - sg))
        dw = _stack_rows([jnp.sum(dpre * t, axis=0, keepdims=True) for t in taps], 4)
        return [dpre, dw, jnp.sum(dpre, axis=0, keepdims=True)]

    dpre, dw, db = _rows(name + "_a", fn,
                         [("prev", pc, CW, base), ("row", pc, CW, base), ("ccol", w, CW, 0), ("ccol", b, CW, 0),
                          ("row", dy, CW, 0)],
                         [(XBC, CW, 0, F32)], [(4, XBC, CW), (1, XBC, CW)], tm=256, nrows=S, ncol=XBC // CW)
    dx = _conv_t(name + "_b", dpre, w, 4, XBC // CW, CW, BF16)
    return dx, dw, db


NFC = D_FF // CW


def _ffn_act_fwd(h, w, b, name):
    S = h.shape[0]

    def fn(i, j, ha, a, hv, v, wa, wv, ba, bv):
        pa = _conv_pre(_conv_taps(i, ha, a, 3), wa, ba)
        pv = _conv_pre(_conv_taps(i, hv, v, 3), wv, bv)
        return [_silu(pa) * pv]

    return _rows(name, fn,
                 [("prev", h, CW, 0), ("row", h, CW, 0), ("prev", h, CW, NFC), ("row", h, CW, NFC),
                  ("ccol", w, CW, 0), ("ccol", w, CW, NFC), ("ccol", b, CW, 0), ("ccol", b, CW, NFC)],
                 [(D_FF, CW, 0, BF16)], tm=256, nrows=S, ncol=NFC)[0]


def _ffn_act_bwd(h, w, b, df, name):
    S = h.shape[0]
    tm = 256
    nt = S // tm

    def ext_taps(i, prev, x, nxt):
        cat = jnp.concatenate([jnp.where(i == 0, 0.0, prev), x, jnp.where(i == nt - 1, 0.0, nxt)], axis=0)
        return [_shift_down(cat, 2 - k)[HALO:] for k in range(3)]

    def fn(i, j, pa_, a, na, pv_, v, nv, wa, wv, ba, bv, dfv, dfn):
        ta = ext_taps(i, pa_, a, na)
        tv = ext_taps(i, pv_, v, nv)
        pa = _conv_pre(ta, wa, ba)
        pv = _conv_pre(tv, wv, bv)
        sg = _sigmoid(pa)
        dfe = jnp.concatenate([dfv.astype(F32), jnp.where(i == nt - 1, 0.0, dfn.astype(F32))], axis=0)
        dpa = dfe * pv * sg * (1.0 + pa * (1.0 - sg))
        dpv = dfe * pa * sg
        res = []
        for dp, wv_ in ((dpa, wa), (dpv, wv)):
            acc = jnp.zeros((tm, CW), F32)
            for k in range(3):
                acc = acc + _shift_up(dp, 2 - k)[:tm] * _row_pick(wv_, k)
            res.append(acc)
        for dp, taps in ((dpa, ta), (dpv, tv)):
            res.append(_stack_rows([jnp.sum(dp[:tm] * t[:tm], axis=0, keepdims=True) for t in taps], 3))
        for dp in (dpa, dpv):
            res.append(jnp.sum(dp[:tm], axis=0, keepdims=True))
        return res

    ins = []
    for base in (0, NFC):
        ins += [("prev", h, CW, base), ("row", h, CW, base), ("next", h, CW, base)]
    ins += [("ccol", w, CW, 0), ("ccol", w, CW, NFC), ("ccol", b, CW, 0), ("ccol", b, CW, NFC),
            ("row", df, CW, 0), ("next", df, CW, 0)]
    dha, dhv, dwa, dwv, dba, dbv = _rows(
        name, fn, ins, [(D_FF, CW, 0, BF16)] * 2, [(3, D_FF, CW)] * 2 + [(1, D_FF, CW)] * 2, tm=tm, nrows=S, ncol=NFC)
    return dha, dhv, jnp.concatenate([dwa, dwv], axis=1), jnp.concatenate([dba, dbv], axis=1)


NSLAB = D // LANES


def _ssd_chunk(xs, Bs, Cs, dtraw, dtb, alog, prev):
    lsz = SSD_CHUNK
    lane = _iota((lsz, LANES), 1)
    row = _iota((lsz, LANES), 0)
    dt = jnp.where(lane < SSD_HEADS, _softplus(dtraw + dtb), 0.0)
    a = dt * (-jnp.exp(alog))
    tril = row >= lane
    a_cs = _fdot(tril.astype(F32), a)
    a_cst = a_cs.T
    a_last = jnp.sum(a, axis=0, keepdims=True)
    lo = lane < HD
    top = row < HD
    cbs = [_bdot_nt(Cs[g], Bs[g]) for g in range(2)]
    ys, news = [], []
    for s in range(NSLAB):
        g = s // (NSLAB // 2)
        cols, lms, dts, als = [], [], [], []
        for hh in range(2):
            h = 2 * s + hh
            col = _lane_pick(a_cs, h)
            seg = col - _row_pick(a_cst, h)
            lms.append(jnp.exp(jnp.where(tril, seg, NEG)))
            cols.append(col)
            dts.append(_lane_pick(dt, h))
            als.append(_lane_pick(a_last, h))
        col_x = jnp.where(lo, cols[0], cols[1])
        al_x = jnp.where(lo, als[0], als[1])
        xc = xs[s] * jnp.where(lo, dts[0], dts[1])
        yd = jnp.where(lo, _bdot_nn(cbs[g] * lms[0], xc), _bdot_nn(cbs[g] * lms[1], xc))
        yoff = _bdot_nt(Cs[g], prev[s]) * jnp.exp(col_x)
        ys.append(yd + yoff)
        st = _bdot_tn(xc * jnp.exp(al_x - col_x), Bs[g])
        news.append(prev[s] * jnp.exp(jnp.where(top, als[0], als[1])) + st)
    return ys, news


def _ssd_scan_fwd(xbc_c, pd, dtb, alog, name):
    S = xbc_c.shape[0]
    nc = S // SSD_CHUNK

    def body(x_ref, b_ref, c_ref, dt_ref, dtb_ref, al_ref, y_ref, st_ref, state):
        c = pl.program_id(0)

        @pl.when(c == 0)
        def _():
            state[...] = jnp.zeros_like(state)

        xs = [x_ref[:, s * LANES:(s + 1) * LANES] for s in range(NSLAB)]
        Bs = [b_ref[:, g * SSD_N:(g + 1) * SSD_N] for g in range(2)]
        Cs = [c_ref[:, g * SSD_N:(g + 1) * SSD_N] for g in range(2)]
        prev = [state[s * LANES:(s + 1) * LANES, :] for s in range(NSLAB)]
        ys, news = _ssd_chunk(xs, Bs, Cs, dt_ref[...], dtb_ref[...], al_ref[...], prev)
        st_ref[0] = state[...]
        for s in range(NSLAB):
            y_ref[:, s * LANES:(s + 1) * LANES] = ys[s]
            state[s * LANES:(s + 1) * LANES, :] = news[s]

    return pl.pallas_call(
        body, name=name, grid=(nc,),
        in_specs=[pl.BlockSpec((SSD_CHUNK, D), lambda c: (c, 0)),
                  pl.BlockSpec((SSD_CHUNK, 2 * SSD_N), lambda c: (c, D // (2 * SSD_N))),
                  pl.BlockSpec((SSD_CHUNK, 2 * SSD_N), lambda c: (c, D // (2 * SSD_N) + 1)),
                  pl.BlockSpec((SSD_CHUNK, LANES), lambda c: (c, 0)),
                  pl.BlockSpec((1, LANES), lambda c: (0, 0)), pl.BlockSpec((1, LANES), lambda c: (0, 0))],
        out_specs=[pl.BlockSpec((SSD_CHUNK, D), lambda c: (c, 0)), pl.BlockSpec((1, D, SSD_N), lambda c: (c, 0, 0))],
        out_shape=[jax.ShapeDtypeStruct((S, D), F32), jax.ShapeDtypeStruct((nc, D, SSD_N), F32)],
        scratch_shapes=[pltpu.VMEM((D, SSD_N), F32)],
        compiler_params=pltpu.CompilerParams(dimension_semantics=("arbitrary",)),
    )(xbc_c, xbc_c, xbc_c, pd, dtb, alog)


def _ssd_scan_bwd(xbc_c, pd, dtb, alog, states, dy, dxs_skip, name):
    S = xbc_c.shape[0]
    nc = S // SSD_CHUNK

    def body(x_ref, b_ref, c_ref, dt_ref, dtb_ref, al_ref, st_ref, dy_ref, sk_ref,
             dx_ref, ddt_ref, ddtb_ref, dal_ref, dstate):
        c = pl.program_id(0)

        @pl.when(c == 0)
        def _():
            dstate[...] = jnp.zeros_like(dstate)
            ddtb_ref[...] = jnp.zeros_like(ddtb_ref)
            dal_ref[...] = jnp.zeros_like(dal_ref)

        xs = [x_ref[:, s * LANES:(s + 1) * LANES] for s in range(NSLAB)]
        Bs = [b_ref[:, g * SSD_N:(g + 1) * SSD_N] for g in range(2)]
        Cs = [c_ref[:, g * SSD_N:(g + 1) * SSD_N] for g in range(2)]
        prev = [st_ref[0, s * LANES:(s + 1) * LANES, :] for s in range(NSLAB)]
        _, vjp = jax.vjp(_ssd_chunk, xs, Bs, Cs, dt_ref[...], dtb_ref[...], al_ref[...], prev)
        dys = [dy_ref[:, s * LANES:(s + 1) * LANES] for s in range(NSLAB)]
        dnew = [dstate[s * LANES:(s + 1) * LANES, :] for s in range(NSLAB)]
        dxs, dBs, dCs, ddt, ddtb, dal, dprev = vjp((dys, dnew))
        for s in range(NSLAB):
            dx_ref[:, s * LANES:(s + 1) * LANES] = dxs[s] + sk_ref[:, s * LANES:(s + 1) * LANES]
            dstate[s * LANES:(s + 1) * LANES, :] = dprev[s]
        for g in range(2):
            dx_ref[:, D + g * SSD_N:D + (g + 1) * SSD_N] = dBs[g]
            dx_ref[:, D + 2 * SSD_N + g * SSD_N:D + 2 * SSD_N + (g + 1) * SSD_N] = dCs[g]
        ddt_ref[...] = ddt
        ddtb_ref[...] += ddtb
        dal_ref[...] += dal

    def rv(c):
        return nc - 1 - c

    return pl.pallas_call(
        body, name=name, grid=(nc,),
        in_specs=[pl.BlockSpec((SSD_CHUNK, D), lambda c: (rv(c), 0)),
                  pl.BlockSpec((SSD_CHUNK, 2 * SSD_N), lambda c: (rv(c), D // (2 * SSD_N))),
                  pl.BlockSpec((SSD_CHUNK, 2 * SSD_N), lambda c: (rv(c), D // (2 * SSD_N) + 1)),
                  pl.BlockSpec((SSD_CHUNK, LANES), lambda c: (rv(c), 0)),
                  pl.BlockSpec((1, LANES), lambda c: (0, 0)), pl.BlockSpec((1, LANES), lambda c: (0, 0)),
                  pl.BlockSpec((1, D, SSD_N), lambda c: (rv(c), 0, 0)),
                  pl.BlockSpec((SSD_CHUNK, D), lambda c: (rv(c), 0)),
                  pl.BlockSpec((SSD_CHUNK, D), lambda c: (rv(c), 0))],
        out_specs=[pl.BlockSpec((SSD_CHUNK, XBC), lambda c: (rv(c), 0)),
                   pl.BlockSpec((SSD_CHUNK, LANES), lambda c: (rv(c), 0)),
                   pl.BlockSpec((1, LANES), lambda c: (0, 0)), pl.BlockSpec((1, LANES), lambda c: (0, 0))],
        out_shape=[jax.ShapeDtypeStruct((S, XBC), F32), jax.ShapeDtypeStruct((S, LANES), F32),
                   jax.ShapeDtypeStruct((1, LANES), F32), jax.ShapeDtypeStruct((1, LANES), F32)],
        scratch_shapes=[pltpu.VMEM((D, SSD_N), F32)],
        compiler_params=pltpu.CompilerParams(dimension_semantics=("arbitrary",)),
    )(xbc_c, xbc_c, xbc_c, pd, dtb, alog, states, dy, dxs_skip)


def _ssd_post_core(y, xs, z, d128, nw):
    tm = y.shape[0]
    ex = (_iota((LANES, D), 1) // HD == _iota((LANES, D), 0)).astype(F32)
    d_x = jnp.sum(_fdot(jnp.broadcast_to(d128, (8, LANES)), ex), axis=0, keepdims=True) * 0.125
    y2 = (y + d_x * xs) * _silu(z)
    lo = _iota((tm, D), 1) < D // 2
    sq = y2 * y2
    ms0 = jnp.sum(jnp.where(lo, sq, 0.0), axis=-1, keepdims=True) / (D // 2)
    ms1 = jnp.sum(jnp.where(lo, 0.0, sq), axis=-1, keepdims=True) / (D // 2)
    r = jnp.where(lo, lax.rsqrt(ms0 + EPS), lax.rsqrt(ms1 + EPS))
    return y2 * r * nw


def _ssd_post_ins(y, xbc_c, pc, d128, nw):
    return [("row", y, None, 0), ("row", xbc_c, D, 0), ("row", pc, D, 0), ("const", d128, None, 0), ("const", nw, None, 0)]


def _ssd_post_fwd(y, xbc_c, pc, d128, nw, name):
    S = y.shape[0]
    return _rows(name, lambda i, j, *v: [_ssd_post_core(*v)], _ssd_post_ins(y, xbc_c, pc, d128, nw),
                 [(D, D, 0, BF16)], tm=128, nrows=S)[0]


def _ssd_post_bwd(y, xbc_c, pc, d128, nw, dout, name):
    S = y.shape[0]

    def fn(i, j, *v):
        _, vjp = jax.vjp(_ssd_post_core, *v[:5])
        return list(vjp(v[5]))

    return _rows(name, fn, _ssd_post_ins(y, xbc_c, pc, d128, nw) + [("row", dout, None, 0)],
                 [(D, D, 0, F32), (D, D, 0, F32), (D, D, 0, BF16)], [(1, LANES, LANES), (1, D, D)], tm=128, nrows=S)


def _gates_core(g0, g1, g2, b0, b1, b2, ya, yb, yc):
    return _sigmoid(g0 + b0) * ya + _sigmoid(g1 + b1) * yb + _sigmoid(g2 + b2) * yc


def _gate_parts(pdv, bv):
    gp = pltpu.roll(pdv, SEC_D - 16, 1)
    return [gp[:, k * D:(k + 1) * D] for k in range(3)] + [bv[:, k * D:(k + 1) * D] for k in range(3)]


def _gates_fwd(pd, bg, ya, yb, yc, name):
    S = pd.shape[0]

    def fn(i, j, pdv, bv, a, b, c):
        return [_gates_core(*_gate_parts(pdv, bv), a, b, c)]

    return _rows(name, fn, [("row", pd, None, 0), ("const", bg, None, 0), ("row", ya, None, 0), ("row", yb, None, 0),
                            ("row", yc, None, 0)], [(D, D, 0, BF16)], tm=128, nrows=S)[0]


def _gates_bwd(pd, bg, ya, yb, yc, dm, name):
    S = pd.shape[0]
    tm = 128

    def fn(i, j, pdv, bv, a, b, c, dmv):
        _, vjp = jax.vjp(_gates_core, *_gate_parts(pdv, bv), a, b, c)
        g = vjp(dmv)
        return [g[6], g[7], g[8], jnp.concatenate(g[0:3], axis=1), jnp.concatenate(g[3:6], axis=1)]

    return _rows(name, fn, [("row", pd, None, 0), ("const", bg, None, 0), ("row", ya, None, 0), ("row", yb, None, 0),
                            ("row", yc, None, 0), ("row", dm, None, 0)],
                 [(D, D, 0, BF16)] * 3 + [(3 * D, 3 * D, 0, BF16)], [(1, 3 * D, 3 * D)], tm=tm, nrows=S)


def _adamw(w, g, m, v, name):
    rows, C = w.shape
    tm = _pick(rows, [t for t in (512, 256, 128, 64, 32, 16, 8) if t * C <= ADAM_TILE])

    def fn(i, j, wv, gv, mv, vv):
        m2 = ADAM_B1 * mv + (1.0 - ADAM_B1) * gv
        v2 = ADAM_B2 * vv + (1.0 - ADAM_B2) * jnp.square(gv)
        m_hat = m2 / (1.0 - ADAM_B1 ** ADAM_STEP)
        v_hat = v2 / (1.0 - ADAM_B2 ** ADAM_STEP)
        delta = -ADAM_LR * (m_hat / (jnp.sqrt(v_hat) + ADAM_EPS) + ADAM_WD * wv)
        return [delta, m2, v2]

    return _rows(name, fn, [("row", a, None, 0) for a in (w, g, m, v)], [(C, C, 0, F32)] * 3, tm=tm, nrows=rows)


def _position():
    return lax.axis_index("x"), lax.axis_index("y"), lax.axis_index("c")


def _other_chips(x, y):
    return [(1 - x, y), (x, 1 - y), (1 - x, 1 - y)]


_HBM = pl.BlockSpec(memory_space=pltpu.HBM)


def _gather_weights(pack, name):
    R, C = pack.shape
    half = R // 2

    def body(p_ref, out_ref, send_sems, recv_sems):
        x, y, c = _position()
        sibling = (x, y, 1 - c)
        chips = _other_chips(x, y)

        def slab(chip, h):
            return out_ref.at[2 * chip[0] + chip[1], pl.ds(h * half, half), :]

        def copy(k, src, dst, to):
            return pltpu.make_async_remote_copy(src_ref=src, dst_ref=dst, send_sem=send_sems.at[k],
                                                recv_sem=recv_sems.at[k], device_id=to, device_id_type=MESH)

        first = [copy(j, p_ref.at[pl.ds(c * half, half), :], slab((x, y), c), (*chip, c)) for j, chip in enumerate(chips)]
        for cp in first:
            cp.start()
        passed = [copy(3 + j, slab(chip, c), slab(chip, c), sibling) for j, chip in enumerate(chips)]
        for j, chip in enumerate(chips):
            copy(j, slab(chip, c), slab(chip, c), (x, y, c)).wait_recv()
            passed[j].start()
        for j, chip in enumerate(chips):
            copy(3 + j, slab(chip, 1 - c), slab(chip, 1 - c), (x, y, c)).wait_recv()
        for cp in first + passed:
            cp.wait_send()

    return pl.pallas_call(
        body, name=name, in_specs=[_HBM], out_specs=_HBM,
        out_shape=jax.ShapeDtypeStruct((4, R, C), pack.dtype),
        scratch_shapes=[pltpu.SemaphoreType.DMA((6,)), pltpu.SemaphoreType.DMA((6,))],
    )(pack)


def _rs_pair_exchange(g, name):
    _, R, C = g.shape
    Rh = R // 2

    def body(g_ref, out_ref, send_sem, recv_sem):
        x, y, c = _position()
        src = g_ref.at[pl.ds(0, 4), pl.ds((1 - c) * Rh, Rh), :]
        cp = pltpu.make_async_remote_copy(src_ref=src, dst_ref=out_ref, send_sem=send_sem,
                                          recv_sem=recv_sem, device_id=(x, y, 1 - c), device_id_type=MESH)
        cp.start()
        cp.wait()

    return pl.pallas_call(
        body, name=name, in_specs=[_HBM], out_specs=_HBM,
        out_shape=jax.ShapeDtypeStruct((4, Rh, C), g.dtype),
        scratch_shapes=[pltpu.SemaphoreType.DMA, pltpu.SemaphoreType.DMA],
    )(g)


def _rs_chip_exchange(h, name):
    _, Rh, C = h.shape

    def body(h_ref, out_ref, send_sems, recv_sems):
        x, y, c = _position()
        chips = _other_chips(x, y)
        cps = [pltpu.make_async_remote_copy(src_ref=h_ref.at[2 * chip[0] + chip[1]], dst_ref=out_ref.at[j],
                                            send_sem=send_sems.at[j], recv_sem=recv_sems.at[j],
                                            device_id=(*chip, c), device_id_type=MESH)
               for j, chip in enumerate(chips)]
        for cp in cps:
            cp.start()
        for cp in cps:
            cp.wait()

    return pl.pallas_call(
        body, name=name, in_specs=[_HBM], out_specs=_HBM,
        out_shape=jax.ShapeDtypeStruct((3, Rh, C), h.dtype),
        scratch_shapes=[pltpu.SemaphoreType.DMA((3,)), pltpu.SemaphoreType.DMA((3,))],
    )(h)


def _rs_swap(r, name):
    Rh, C = r.shape

    def body(r_ref, out_ref, send_sem, recv_sem):
        x, y, c = _position()
        cp = pltpu.make_async_remote_copy(src_ref=r_ref, dst_ref=out_ref, send_sem=send_sem,
                                          recv_sem=recv_sem, device_id=(x, y, 1 - c), device_id_type=MESH)
        cp.start()
        cp.wait()

    return pl.pallas_call(
        body, name=name, in_specs=[_HBM], out_specs=_HBM,
        out_shape=jax.ShapeDtypeStruct((Rh, C), r.dtype),
        scratch_shapes=[pltpu.SemaphoreType.DMA, pltpu.SemaphoreType.DMA],
    )(r)


def _rs_add_pair(g, recv, cidx, name):
    _, R, C = g.shape
    Rh = R // 2
    tm = _pick(Rh, (400, 280, 200, 160, 80, 40, 16, 8))
    nt = Rh // tm

    def body(c_ref, g_ref, r_ref, o_ref):
        o_ref[...] = (g_ref[...].astype(F32) + r_ref[...].astype(F32)).astype(o_ref.dtype)

    return pl.pallas_call(
        body, name=name,
        grid_spec=pltpu.PrefetchScalarGridSpec(
            num_scalar_prefetch=1, grid=(4, nt),
            in_specs=[pl.BlockSpec((1, tm, C), lambda k, i, cr: (k, cr[0] * nt + i, 0)),
                      pl.BlockSpec((1, tm, C), lambda k, i, cr: (k, i, 0))],
            out_specs=pl.BlockSpec((1, tm, C), lambda k, i, cr: (k, i, 0))),
        out_shape=jax.ShapeDtypeStruct((4, Rh, C), BF16),
    )(cidx, g, recv)


def _rs_add_chips(h, recv, chip_idx, name):
    _, Rh, C = h.shape
    tm = _pick(Rh, (400, 280, 200, 160, 80, 40, 16, 8))

    def body(c_ref, h_ref, r_ref, o_ref):
        acc = h_ref[0].astype(F32)
        for j in range(3):
            acc = acc + r_ref[j].astype(F32)
        o_ref[...] = acc

    return pl.pallas_call(
        body, name=name,
        grid_spec=pltpu.PrefetchScalarGridSpec(
            num_scalar_prefetch=1, grid=(Rh // tm,),
            in_specs=[pl.BlockSpec((1, tm, C), lambda i, cr: (cr[0], i, 0)), pl.BlockSpec((3, tm, C), lambda i, cr: (0, i, 0))],
            out_specs=pl.BlockSpec((tm, C), lambda i, cr: (i, 0))),
        out_shape=jax.ShapeDtypeStruct((Rh, C), F32),
    )(chip_idx, h, recv)


def _all_reduce_small(vec, name):
    n, C = vec.shape

    def body(v_ref, out_ref, buf, send_sems, recv_sems):
        x, y, c = _position()

        def flip(k):
            return ((1 - x) if k & 4 else x, (1 - y) if k & 2 else y, (1 - c) if k & 1 else c)

        def idx(p):
            return 4 * p[0] + 2 * p[1] + p[2]

        me = idx((x, y, c))
        buf[me] = v_ref[...]
        cps = [pltpu.make_async_remote_copy(src_ref=v_ref, dst_ref=buf.at[me], send_sem=send_sems.at[k - 1],
                                            recv_sem=recv_sems.at[k - 1], device_id=flip(k), device_id_type=MESH)
               for k in range(1, 8)]
        for cp in cps:
            cp.start()
        for k in range(1, 8):
            pltpu.make_async_remote_copy(src_ref=v_ref, dst_ref=buf.at[idx(flip(k))], send_sem=send_sems.at[k - 1],
                                         recv_sem=recv_sems.at[k - 1], device_id=flip(k), device_id_type=MESH).wait_recv()
        for cp in cps:
            cp.wait_send()
        acc = buf[0]
        for s in range(1, 8):
            acc = acc + buf[s]
        out_ref[...] = acc

    return pl.pallas_call(
        body, name=name,
        in_specs=[pl.BlockSpec(memory_space=pltpu.VMEM)], out_specs=pl.BlockSpec(memory_space=pltpu.VMEM),
        out_shape=jax.ShapeDtypeStruct((n, C), F32),
        scratch_shapes=[pltpu.VMEM((8, n, C), F32), pltpu.SemaphoreType.DMA((7,)), pltpu.SemaphoreType.DMA((7,))],
    )(vec)


BIG = (("w_in", (D, IN_WIDTH // 4), "cols"), ("w_a", (GW, D // 4), "cols"), ("pool_w", (4, PG // 4, PG), "pool"),
       ("w_b", (D // 4, D), "rows"), ("w_c", (D // 4, D), "rows"), ("w_o", (D // 4, D), "rows"),
       ("ffn_w_up", (D, 2 * D_FF // 4), "cols"), ("ffn_w_down", (D_FF // 4, D), "rows"))
PACK_ROWS = sum(math.prod(s) for _, s, _ in BIG) // D
PACK_PAD = -(-PACK_ROWS // 32) * 32


def _pack_blocks(blocks, dtype):
    lead = blocks["w_in"].shape[:-2]
    flat = []
    for n, _, how in BIG:
        v = blocks[n].astype(dtype)
        if how == "cols":
            v = jnp.swapaxes(v, -1, -2)
        flat.append(v.reshape(*lead, -1, D))
    flat.append(jnp.zeros((*lead, PACK_PAD - PACK_ROWS, D), dtype))
    return jnp.concatenate(flat, axis=-2)


def _unpack_blocks(pack):
    out, r = {}, 0
    for n, s, how in BIG:
        k = math.prod(s) // D
        v = pack[r:r + k, :]
        out[n] = v.reshape(s[1], s[0]).T if how == "cols" else v.reshape(s)
        r += k
    return out


def _operands(allp):
    out, r = {}, 0
    for n, s, how in BIG:
        k = math.prod(s) // D
        v = allp[:, r:r + k, :]
        if how == "cols":
            out[n] = v.reshape(4 * s[1], s[0])
        elif how == "rows":
            out[n] = v.reshape(4 * s[0], s[1])
        else:
            out[n] = v.reshape(4, *s).transpose(1, 0, 2, 3).reshape(4, PG, PG)
        r += k
    return out


def _pack_operands(g, dtype):
    flat = []
    for n, s, how in BIG:
        v = g[n].astype(dtype)
        if how == "pool":
            v = v.reshape(4, 4, s[1], s[2]).transpose(1, 0, 2, 3)
        flat.append(v.reshape(4, -1, D))
    flat.append(jnp.zeros((4, PACK_PAD - PACK_ROWS, D), dtype))
    return jnp.concatenate(flat, axis=1)


def _layer_fwd(x, w, sm, bias):
    u = _rms_fwd(x, sm["ln1_g"], "rms1")
    pa = _mm(u, w["in_a"], tb=True, name="in_a")
    pb = _mm(u, w["in_b"], tb=True, name="in_b")
    pc = _mm(u, w["in_c"], tb=True, name="in_c")
    pd = _mm(u, w["in_d"], tb=True, name="in_d")
    os_, ls_ = [], []
    for gi in range(3):
        o, l = _attn_fwd(pa, bias[gi], gi, "attn_fwd%d" % gi)
        os_.append(o)
        ls_.append(l)
    att = _mix_fwd(os_, ls_, "mix_fwd")
    ya = _mm(att, w["w_a"], tb=True, name="mm_wa")
    pool_o = _pool_fwd(pb, w["pool_w"], sm["pool_scale"], "pool_fwd")
    yb = _mm(pool_o, w["w_b"], name="mm_wb")
    xbc_c = _ssd_conv_fwd(pc, sm["ssd_conv_w"], sm["ssd_conv_b"], "ssd_conv_fwd")
    y_scan, states = _ssd_scan_fwd(xbc_c, pd, sm["ssd_dt_bias"], sm["ssd_a_log"], "ssd_scan_fwd")
    ssd_o = _ssd_post_fwd(y_scan, xbc_c, pc, sm["ssd_d"], sm["ssd_norm_w"], "ssd_post_fwd")
    yc = _mm(ssd_o, w["w_c"], name="mm_wc")
    merged = _gates_fwd(pd, sm["b_gate"], ya, yb, yc, "gates_fwd")
    x1 = _mm(merged, w["w_o"], add=x, name="mm_wo")
    u2 = _rms_fwd(x1, sm["ln2_g"], "rms2")
    h = _mm(u2, w["ffn_w_up"], tb=True, name="mm_up")
    f = _ffn_act_fwd(h, sm["ffn_conv_w"], sm["ffn_conv_b"], "ffn_act_fwd")
    x2 = _mm(f, w["ffn_w_down"], add=x1, name="mm_down")
    saved = dict(x=x, u=u, pa=pa, pb=pb, pc=pc, pd=pd, os=os_, ls=ls_, att=att, ya=ya, yb=yb, yc=yc, pool_o=pool_o,
                 xbc_c=xbc_c, y_scan=y_scan, states=states, ssd_o=ssd_o, merged=merged, x1=x1, u2=u2, h=h, f=f)
    return x2, saved


def _layer_bwd(dx2, w, sm, bias, dbs, sv):
    gw, gs = {}, {}
    S = dx2.shape[0]

    def gmm(a, b, name):
        return _mm(a, b, ta=True, out_dtype=BF16, name=name)

    df = _mm(dx2, w["ffn_w_down"], tb=True, name="d_f")
    gw["ffn_w_down"] = gmm(sv["f"], dx2, "g_down")
    dha, dhv, gs["ffn_conv_w"], gs["ffn_conv_b"] = _ffn_act_bwd(sv["h"], sm["ffn_conv_w"], sm["ffn_conv_b"], df, "ffn_act_bwd")
    du2 = _mm(dha, w["up_a"], name="d_u2_a")
    du2 = _mm(dhv, w["up_v"], add=du2, name="d_u2_v")
    gw["ffn_w_up"] = jnp.concatenate([gmm(dha, sv["u2"], "g_up_a"), gmm(dhv, sv["u2"], "g_up_v")], axis=0)
    dx1, gs["ln2_g"] = _rms_bwd(sv["x1"], sm["ln2_g"], du2, dx2, "rms2_bwd")
    dmerged = _mm(dx1, w["w_o"], tb=True, name="d_merged")
    gw["w_o"] = gmm(sv["merged"], dx1, "g_wo")
    dya, dyb, dyc, dgate, gs["b_gate"] = _gates_bwd(
        sv["pd"], sm["b_gate"], sv["ya"], sv["yb"], sv["yc"], dmerged, "gates_bwd")
    dssd_o = _mm(dyc, w["w_c"], tb=True, name="d_ssd_o")
    gw["w_c"] = gmm(sv["ssd_o"], dyc, "g_wc")
    dy_scan, dxs_skip, dz, gs["ssd_d"], gs["ssd_norm_w"] = _ssd_post_bwd(
        sv["y_scan"], sv["xbc_c"], sv["pc"], sm["ssd_d"], sm["ssd_norm_w"], dssd_o, "ssd_post_bwd")
    dxbc_c, ddt, gs["ssd_dt_bias"], gs["ssd_a_log"] = _ssd_scan_bwd(
        sv["xbc_c"], sv["pd"], sm["ssd_dt_bias"], sm["ssd_a_log"], sv["states"], dy_scan, dxs_skip, "ssd_scan_bwd")
    dxbc, gs["ssd_conv_w"], gs["ssd_conv_b"] = _ssd_conv_bwd(sv["pc"], sm["ssd_conv_w"], sm["ssd_conv_b"], dxbc_c, "ssd_conv_bwd")
    dpool_o = _mm(dyb, w["w_b"], tb=True, name="d_pool_o")
    gw["w_b"] = gmm(sv["pool_o"], dyb, "g_wb")
    dpb, dpw, gs["pool_scale"] = _pool_bwd(sv["pb"], w["pool_w"], sm["pool_scale"], dpool_o, "pool_bwd")
    gw["pool_w"] = dpw.reshape(4, PG, PG)
    datt = _mm(dya, w["w_a"], name="d_att")
    gw["w_a"] = gmm(dya, sv["att"], "g_wa")
    dos, dls = _mix_bwd(sv["os"], sv["ls"], datt, "mix_bwd")
    dqkv = tuple(lax.empty((S, AW), F32) for _ in range(3))
    dbs = list(dbs)
    for gi in range(3):
        dqkv, dbs[gi] = _attn_bwd(sv["pa"], bias[gi], dos[gi], dls[gi], dbs[gi], dqkv, gi, "attn_bwd%d" % gi)
    u = sv["u"]
    pieces = [(dqkv[0], "wq"), (dqkv[1], "wk"), (dqkv[2], "wv"), (dpb, "in_b"), (dz, "wz"), (dxbc, "wxbc"),
              (ddt, "wdt"), (dgate, "wgate")]
    du = None
    g_in = []
    for dp, key in pieces:
        du = _mm(dp, w[key], add=du, name="d_u_" + key)
        g = gmm(dp, u, "g_in_" + key)
        g_in.append(g[:SSD_HEADS] if key == "wdt" else g)
    gw["w_in"] = jnp.concatenate(g_in, axis=0)
    dx, gs["ln1_g"] = _rms_bwd(sv["x"], sm["ln1_g"], du, dx1, "rms1_bwd")
    return dx, gw, gs, dbs


SMALL_LAYER = ("ln1_g", "b_gate", "pool_scale", "ssd_conv_w", "ssd_conv_b", "ssd_dt_bias", "ssd_a_log", "ssd_d",
               "ssd_norm_w", "ln2_g", "ffn_conv_w", "ffn_conv_b")


def _pad_lanes(v):
    return jnp.pad(v, (0, LANES - v.shape[0])).reshape(1, LANES)


def _layer_weights(ops):
    wt = ops["w_in"]
    o1, o2, o3 = SEC_A, SEC_A + SEC_B, SEC_A + SEC_B + SEC_C
    w = dict(ops)
    w["in_a"] = wt[:o1]
    w["in_b"] = wt[o1:o2]
    w["in_c"] = wt[o2:o3]
    w["in_d"] = jnp.pad(wt[o3:], ((0, SEC_D - (IN_WIDTH - o3)), (0, 0)))
    w["wq"], w["wk"], w["wv"] = wt[:AW], wt[AW:2 * AW], wt[2 * AW:o1]
    w["wz"], w["wxbc"] = wt[o2:o2 + D], wt[o2 + D:o3]
    w["wdt"] = jnp.pad(wt[o3:o3 + SSD_HEADS], ((0, LANES - SSD_HEADS), (0, 0)))
    w["wgate"] = wt[o3 + SSD_HEADS:]
    w["up_a"], w["up_v"] = ops["ffn_w_up"][:D_FF], ops["ffn_w_up"][D_FF:]
    return w


def _layer_small(p, i):
    sm = {n: p[n][i] for n in SMALL_LAYER}
    out = {}
    for n, v in sm.items():
        if n in ("ssd_dt_bias", "ssd_a_log", "ssd_d"):
            out[n] = _pad_lanes(v)
        elif v.ndim == 1:
            out[n] = v.reshape(1, -1)
        else:
            out[n] = v
    return out


def _local_step(x, target, rel_bias, final_g, layer_full, small):
    nl = small["ln1_g"].shape[0]
    buckets = [_buckets(d).astype(jnp.int32) for d in DILATIONS]
    bias = [_bias_table(rel_bias, buckets[gi], gi, "bias_table%d" % gi) for gi in range(3)]
    saved, ws, sms = [], [], []
    h = x
    for i in range(nl):
        w = _layer_weights(layer_full(i))
        sm = _layer_small(small, i)
        h, sv = _layer_fwd(h, w, sm, bias)
        saved.append(sv)
        ws.append(w)
        sms.append(sm)
    dh, dfinal, loss = _final_loss(h, target, final_g.reshape(1, D))
    gws, gss = [None] * nl, [None] * nl
    dbs = [jnp.zeros((6, WIN, 2 * WIN), F32)] * 3
    for i in reversed(range(nl)):
        dh, gws[i], gss[i], dbs = _layer_bwd(dh, ws[i], sms[i], bias, dbs, saved[i])
    drel = []
    for gi in range(3):
        onehot = jnp.pad(jax.nn.one_hot(buckets[gi].reshape(-1), REL_BUCKETS, dtype=BF16), ((0, 0), (0, LANES - REL_BUCKETS)))
        drel.append(_mm(dbs[gi].reshape(6, WIN * 2 * WIN), onehot, name="g_relb"))
    return loss, dh, gws, gss, dfinal, jnp.concatenate(drel, axis=0)


WEIGHTS = ("rel_bias", "ln1_g", "w_in", "b_gate", "w_a", "pool_w", "pool_scale", "w_b", "ssd_conv_w", "ssd_conv_b",
           "ssd_dt_bias", "ssd_a_log", "ssd_d", "ssd_norm_w", "w_c", "w_o", "ln2_g", "ffn_w_up", "ffn_conv_w",
           "ffn_conv_b", "ffn_w_down", "final_g")
BIG_NAMES = tuple(n for n, _, _ in BIG)
SHARDED_SMALL = {"ssd_conv_w": XBC // 4, "ffn_conv_w": 2 * D_FF // 4}


def _to_rows(flat):
    n = flat.shape[0]
    rows = -(-n // LANES)
    rows = -(-rows // 8) * 8
    return jnp.pad(flat, (0, rows * LANES - n)).reshape(rows, LANES)


def _flatten(tree, names):
    return jnp.concatenate([tree[n].reshape(-1) for n in names])


def _unflatten(flat, shapes, names):
    out, o = {}, 0
    for n in names:
        k = math.prod(shapes[n])
        out[n] = flat[o:o + k].reshape(shapes[n])
        o += k
    return out


def kernel(x, rel_bias, ln1_g, w_in, b_gate, w_a, pool_w, pool_scale, w_b, ssd_conv_w, ssd_conv_b, ssd_dt_bias, ssd_a_log, ssd_d, ssd_norm_w, w_c, w_o, ln2_g, ffn_w_up, ffn_conv_w, ffn_conv_b, ffn_w_down, final_g, loss_target, m_rel_bias, m_ln1_g, m_w_in, m_b_gate, m_w_a, m_pool_w, m_pool_scale, m_w_b, m_ssd_conv_w, m_ssd_conv_b, m_ssd_dt_bias, m_ssd_a_log, m_ssd_d, m_ssd_norm_w, m_w_c, m_w_o, m_ln2_g, m_ffn_w_up, m_ffn_conv_w, m_ffn_conv_b, m_ffn_w_down, m_final_g, v_rel_bias, v_ln1_g, v_w_in, v_b_gate, v_w_a, v_pool_w, v_pool_scale, v_w_b, v_ssd_conv_w, v_ssd_conv_b, v_ssd_dt_bias, v_ssd_a_log, v_ssd_d, v_ssd_norm_w, v_w_c, v_w_o, v_ln2_g, v_ffn_w_up, v_ffn_conv_w, v_ffn_conv_b, v_ffn_w_down, v_final_g):
    W = dict(rel_bias=rel_bias, ln1_g=ln1_g, w_in=w_in, b_gate=b_gate, w_a=w_a, pool_w=pool_w, pool_scale=pool_scale,
             w_b=w_b, ssd_conv_w=ssd_conv_w, ssd_conv_b=ssd_conv_b, ssd_dt_bias=ssd_dt_bias, ssd_a_log=ssd_a_log,
             ssd_d=ssd_d, ssd_norm_w=ssd_norm_w, w_c=w_c, w_o=w_o, ln2_g=ln2_g, ffn_w_up=ffn_w_up,
             ffn_conv_w=ffn_conv_w, ffn_conv_b=ffn_conv_b, ffn_w_down=ffn_w_down, final_g=final_g)
    M = dict(rel_bias=m_rel_bias, ln1_g=m_ln1_g, w_in=m_w_in, b_gate=m_b_gate, w_a=m_w_a, pool_w=m_pool_w,
             pool_scale=m_pool_scale, w_b=m_w_b, ssd_conv_w=m_ssd_conv_w, ssd_conv_b=m_ssd_conv_b,
             ssd_dt_bias=m_ssd_dt_bias, ssd_a_log=m_ssd_a_log, ssd_d=m_ssd_d, ssd_norm_w=m_ssd_norm_w, w_c=m_w_c,
             w_o=m_w_o, ln2_g=m_ln2_g, ffn_w_up=m_ffn_w_up, ffn_conv_w=m_ffn_conv_w, ffn_conv_b=m_ffn_conv_b,
             ffn_w_down=m_ffn_w_down, final_g=m_final_g)
    V = dict(rel_bias=v_rel_bias, ln1_g=v_ln1_g, w_in=v_w_in, b_gate=v_b_gate, w_a=v_w_a, pool_w=v_pool_w,
             pool_scale=v_pool_scale, w_b=v_w_b, ssd_conv_w=v_ssd_conv_w, ssd_conv_b=v_ssd_conv_b,
             ssd_dt_bias=v_ssd_dt_bias, ssd_a_log=v_ssd_a_log, ssd_d=v_ssd_d, ssd_norm_w=v_ssd_norm_w, w_c=v_w_c,
             w_o=v_w_o, ln2_g=v_ln2_g, ffn_w_up=v_ffn_w_up, ffn_conv_w=v_ffn_conv_w, ffn_conv_b=v_ffn_conv_b,
             ffn_w_down=v_ffn_w_down, final_g=v_final_g)
    nl = ln1_g.shape[0]
    px, py, pc_ = _position()
    chip = 2 * px + py
    cidx = jnp.reshape(pc_, (1,)).astype(jnp.int32)
    chip_idx = jnp.reshape(chip, (1,)).astype(jnp.int32)

    placed = {}
    for n, cs in SHARDED_SMALL.items():
        full = jnp.zeros(W[n].shape[:-1] + (4 * cs,), F32)
        full = lax.dynamic_update_slice(full, W[n], (0, 0, chip * cs))
        placed[n] = jnp.where(pc_ == 0, full, 0.0)
    names_sh = tuple(SHARDED_SMALL)
    shapes_sh = {n: placed[n].shape for n in names_sh}
    got = _all_reduce_small(_to_rows(_flatten(placed, names_sh)), "gather_small")
    small = {n: W[n] for n in SMALL_LAYER}
    small.update(_unflatten(got.reshape(-1), shapes_sh, names_sh))

    packs = _pack_blocks({n: W[n] for n in BIG_NAMES}, BF16)

    def layer_full(i):
        allp = _gather_weights(packs[i], "gather_w")
        allp = lax.dynamic_update_slice(allp, packs[i][None], (chip, 0, 0))
        return _operands(allp)

    loss, dx, gws, gss, dfinal, drel = _local_step(x[0], loss_target[0], rel_bias, final_g, layer_full, small)

    grads = {}
    red = []
    for i in range(nl):
        g = _pack_operands(gws[i], BF16)
        recv = _rs_pair_exchange(g, "rs_pair")
        hsum = _rs_add_pair(g, recv, cidx, "rs_add_pair")
        recv3 = _rs_chip_exchange(hsum, "rs_chips")
        r = _rs_add_chips(hsum, recv3, chip_idx, "rs_add_chips")
        other = _rs_swap(r, "rs_swap")
        both = jnp.concatenate([jnp.where(pc_ == 0, r, other), jnp.where(pc_ == 0, other, r)], axis=0)
        red.append(_unpack_blocks(both))
    for n in BIG_NAMES:
        grads[n] = jnp.stack([red[i][n] for i in range(nl)], axis=0)

    sg = {}
    for n in SMALL_LAYER:
        sg[n] = jnp.stack([gss[i][n] for i in range(nl)], axis=0)
    for n in ("ssd_dt_bias", "ssd_a_log", "ssd_d"):
        sg[n] = sg[n][:, 0, :SSD_HEADS]
    sg["rel_bias"] = drel[:, :REL_BUCKETS].T
    sg["final_g"] = dfinal.reshape(D)
    sg["loss"] = loss[0, :1]
    names_sg = tuple(sg)
    shapes_sg = {n: ((nl,) + W[n].shape[1:] if n in SMALL_LAYER and n not in SHARDED_SMALL else
                     (placed[n].shape if n in SHARDED_SMALL else sg[n].shape)) for n in names_sg}
    for n in names_sg:
        sg[n] = sg[n].reshape(shapes_sg[n])
    tot = _all_reduce_small(_to_rows(_flatten(sg, names_sg)), "allreduce_small")
    tot = _unflatten(tot.reshape(-1), shapes_sg, names_sg)
    loss_out = tot.pop("loss").reshape(())
    for n, cs in SHARDED_SMALL.items():
        tot[n] = lax.dynamic_slice(tot[n], (0, 0, chip * cs), tot[n].shape[:-1] + (cs,))
    grads.update(tot)

    delta, new_m, new_v = {}, {}, {}
    for n in BIG_NAMES:
        shp = W[n].shape
        r2 = lambda a: a.reshape(-1, shp[-1])
        dl, m2, v2 = _adamw(r2(W[n]), r2(grads[n]), r2(M[n]), r2(V[n]), "adamw_" + n)
        delta[n], new_m[n], new_v[n] = dl.reshape(shp), m2.reshape(shp), v2.reshape(shp)
    names_s = tuple(n for n in WEIGHTS if n not in BIG_NAMES)
    shapes_s = {n: W[n].shape for n in names_s}
    pk = lambda t: _to_rows(_flatten(t, names_s))
    dl, m2, v2 = _adamw(pk(W), pk(grads), pk(M), pk(V), "adamw_small")
    delta.update(_unflatten(dl.reshape(-1), shapes_s, names_s))
    new_m.update(_unflatten(m2.reshape(-1), shapes_s, names_s))
    new_v.update(_unflatten(v2.reshape(-1), shapes_s, names_s))

    return (loss_out, dx[None], *[grads[n] for n in WEIGHTS], *[delta[n] for n in WEIGHTS],
            *[new_m[n] for n in WEIGHTS], *[new_v[n] for n in WEIGHTS])
```

```python
import functools
import math

import jax
import jax.numpy as jnp
from jax import lax
from jax.experimental import pallas as pl
from jax.experimental.pallas import tpu as pltpu

F32 = jnp.float32
BF16 = jnp.bfloat16
MESH = pl.DeviceIdType.MESH

D = 1024
HD = 64
GW = 384
AW = 3 * GW
WIN = 128
DILATIONS = (1, 4, 16)
REL_BUCKETS = 32
REL_MAX_DISTANCE = 2048
POOL_WINDOWS = (2, 4, 8, 16)
PG = 256
SSD_HEADS = 16
SSD_N = 128
SSD_CHUNK = 128
XBC = 1536
D_FF = 2816
EPS = 1e-6
NEG = -1e30
HALO = 16
LANES = 128

SEC_A = 3 * AW
SEC_B = D
SEC_C = D + XBC
SEC_D = 3200
IN_WIDTH = SEC_A + SEC_B + SEC_C + 16 + 3 * D

ADAM_LR = 0.001
ADAM_B1 = 0.9
ADAM_B2 = 0.999
ADAM_EPS = 1e-08
ADAM_WD = 0.01
ADAM_STEP = 10
ADAM_TILE = 256 * 1024
MM_VMEM_BYTES = 40 * 1024 * 1024
MM_MAX_OUT_TILE = 1024 * 1024
HBM_BYTES_PER_US = 2.0e6
STEP_US = 0.35


_ANY = pl.BlockSpec(memory_space=pl.ANY)


def _pick(d, cands):
    for t in cands:
        if d % t == 0:
            return t
    return d


def _iota(shape, dim):
    return lax.broadcasted_iota(jnp.int32, shape, dim)


def _dg(a, b, ca, cb):
    return lax.dot_general(a.astype(BF16), b.astype(BF16), (((ca,), (cb,)), ((), ())),
                           preferred_element_type=F32)


@jax.custom_vjp
def _bdot_nn(a, b):
    return _dg(a, b, 1, 0)


def _nn_fwd(a, b):
    return _dg(a, b, 1, 0), (a, b)


def _nn_bwd(res, g):
    a, b = res
    return _dg(g, b, 1, 1), _dg(a, g, 0, 0)


_bdot_nn.defvjp(_nn_fwd, _nn_bwd)


@jax.custom_vjp
def _bdot_nt(a, b):
    return _dg(a, b, 1, 1)


def _nt_fwd(a, b):
    return _dg(a, b, 1, 1), (a, b)


def _nt_bwd(res, g):
    a, b = res
    return _dg(g, b, 1, 0), _dg(g, a, 0, 0)


_bdot_nt.defvjp(_nt_fwd, _nt_bwd)


@jax.custom_vjp
def _bdot_tn(a, b):
    return _dg(a, b, 0, 0)


def _tn_fwd(a, b):
    return _dg(a, b, 0, 0), (a, b)


def _tn_bwd(res, g):
    a, b = res
    return _dg(b, g, 1, 1), _dg(a, g, 1, 0)


_bdot_tn.defvjp(_tn_fwd, _tn_bwd)


def _fdot(a, b):
    return jnp.dot(a, b, preferred_element_type=F32, precision=lax.Precision.HIGHEST)


def _sigmoid(x):
    return 1.0 / (1.0 + jnp.exp(-x))


def _silu(x):
    return x * _sigmoid(x)


def _softplus(x):
    return jnp.maximum(x, 0.0) + jnp.log(1.0 + jnp.exp(-jnp.abs(x)))


def _lane_pick(m, h):
    return jnp.sum(jnp.where(_iota(m.shape, 1) == h, m, 0.0), axis=1, keepdims=True)


def _row_pick(m, h):
    return jnp.sum(jnp.where(_iota(m.shape, 0) == h, m, 0.0), axis=0, keepdims=True)


def _stack_rows(rows, n):
    c = rows[0].shape[1]
    r = _iota((n, c), 0)
    out = jnp.zeros((n, c), F32)
    for k, v in enumerate(rows):
        out = out + jnp.where(r == k, v, 0.0)
    return out


def _mm(a, b, *, ta=False, tb=False, add=None, out_dtype=F32, name, hook=None):
    if ta:
        K, M = a.shape
    else:
        M, K = a.shape
    if tb:
        N, Kb = b.shape
    else:
        Kb, N = b.shape
    assert K == Kb, (a.shape, b.shape, ta, tb)
    tm, tn, tk = _mm_tiles(M, N, K, a.dtype.itemsize, b.dtype.itemsize, jnp.dtype(out_dtype).itemsize,
                           0 if add is None else add.dtype.itemsize)
    ni, nj, nk = M // tm, N // tn, K // tk
    ca = 0 if ta else 1
    cb = 1 if tb else 0
    n_in = 2 if add is None else 3
    n_hin = 0 if hook is None else len(hook.inputs)
    n_hout = 0 if hook is None else len(hook.out_shapes)

    def body(*refs):
        a_ref, b_ref = refs[:2]
        add_ref = None if add is None else refs[2]
        o_ref = refs[n_in + n_hin]
        scr = refs[n_in + n_hin + 1 + n_hout:]
        acc_ref = scr[0] if nk > 1 else None
        hargs = (refs[n_in:n_in + n_hin], refs[n_in + n_hin + 1:n_in + n_hin + 1 + n_hout], scr[1 if nk > 1 else 0:])
        i, j, k = pl.program_id(0), pl.program_id(1), pl.program_id(2)
        if hook is not None:
            @pl.when((i == 0) & (j == 0) & (k == 0))
            def _():
                hook.start(*hargs)

        part = _dg(a_ref[...], b_ref[...], ca, cb)

        def finish(r):
            if add_ref is not None:
                r = r + add_ref[...].astype(F32)
            o_ref[...] = r.astype(o_ref.dtype)

        if nk == 1:
            finish(part)
        else:
            @pl.when(k == 0)
            def _():
                acc_ref[...] = part

            @pl.when((k > 0) & (k < nk - 1))
            def _():
                acc_ref[...] += part

            @pl.when(k == nk - 1)
            def _():
                finish(acc_ref[...] + part)

        if hook is not None:
            @pl.when((i == ni - 1) & (j == nj - 1) & (k == nk - 1))
            def _():
                hook.finish(*hargs)

    a_spec = pl.BlockSpec((tk, tm), lambda i, j, k: (k, i)) if ta else pl.BlockSpec((tm, tk), lambda i, j, k: (i, k))
    b_spec = pl.BlockSpec((tn, tk), lambda i, j, k: (j, k)) if tb else pl.BlockSpec((tk, tn), lambda i, j, k: (k, j))
    in_specs = [a_spec, b_spec]
    args = [a, b]
    if add is not None:
        in_specs.append(pl.BlockSpec((tm, tn), lambda i, j, k: (i, j)))
        args.append(add)
    out_specs = [pl.BlockSpec((tm, tn), lambda i, j, k: (i, j))]
    out_shape = [jax.ShapeDtypeStruct((M, N), out_dtype)]
    scratch = [pltpu.VMEM((tm, tn), F32)] if nk > 1 else []
    aliases = {}
    if hook is not None:
        in_specs += [_ANY] * n_hin
        args += list(hook.inputs)
        out_specs += [_ANY] * n_hout
        out_shape += list(hook.out_shapes)
        scratch += list(hook.scratch)
        aliases = {n_in + hi: 1 + ho for hi, ho in hook.aliases.items()}
    sem = ("parallel", "parallel", "arbitrary") if hook is None else ("arbitrary",) * 3
    res = pl.pallas_call(
        body, name=name, grid=(ni, nj, nk), in_specs=in_specs, out_specs=out_specs, out_shape=out_shape,
        scratch_shapes=scratch, input_output_aliases=aliases,
        compiler_params=pltpu.CompilerParams(dimension_semantics=sem),
    )(*args)
    if hook is not None:
        hook.done(res[1:])
    return res[0]


def _mm_tiles(M, N, K, sa, sb, so, sadd):
    best = None
    for tk in (K, 2048, 1024, 512, 640, 384, 256, 128):
        if K % tk:
            continue
        for tm in (2048, 1024, 512, 640, 384, 256, 128, M):
            if M % tm or tm > 2048:
                continue
            for tn in (1024, 512, 640, 384, 256, 128, N):
                if N % tn:
                    continue
                vmem = 2 * (tm * tk * sa + tk * tn * sb + tm * tn * (so + sadd)) + (tm * tn * 4 if tk < K else 0)
                if vmem > MM_VMEM_BYTES or tm * tn > MM_MAX_OUT_TILE:
                    continue
                a_reads = 1 if tk == K else N // tn
                traffic = M * K * sa * a_reads + K * N * sb * (M // tm) + M * N * (so + sadd)
                steps = (M // tm) * (N // tn) * (K // tk)
                cost = traffic / HBM_BYTES_PER_US + steps * STEP_US
                if best is None or cost < best[0]:
                    best = (cost, tm, tn, tk)
    assert best is not None, (M, N, K)
    return best[1:]


class _Hook:
    def __init__(self, inputs, out_shapes, aliases, scratch, start, finish, done):
        self.inputs, self.out_shapes, self.aliases, self.scratch = inputs, out_shapes, aliases, scratch
        self.start, self.finish, self.done = start, finish, done


def _rows(name, fn, ins, outs, accs=(), *, tm, nrows, ncol=1):
    nt = nrows // tm
    hb = tm // HALO
    nh = nrows // HALO
    in_specs, args = [], []
    for kind, arr, cw, base in ins:
        if kind == "row":
            cw = arr.shape[1] if cw is None else cw
            in_specs.append(pl.BlockSpec((tm, cw), lambda j, i, base=base: (i, base + j)))
        elif kind == "prev":
            in_specs.append(pl.BlockSpec((HALO, cw), lambda j, i, base=base: (jnp.maximum(i * hb - 1, 0), base + j)))
        elif kind == "next":
            in_specs.append(pl.BlockSpec((HALO, cw), lambda j, i, base=base: (jnp.minimum((i + 1) * hb, nh - 1), base + j)))
        elif kind == "const":
            in_specs.append(pl.BlockSpec(arr.shape, lambda j, i, nd=arr.ndim: (0,) * nd))
        elif kind == "ccol":
            in_specs.append(pl.BlockSpec((arr.shape[0], cw), lambda j, i, base=base: (0, base + j)))
        else:
            raise ValueError(kind)
        args.append(arr)
    out_specs, out_shape = [], []
    for ctot, cw, base, dt in outs:
        out_specs.append(pl.BlockSpec((tm, cw), lambda j, i, base=base: (i, base + j)))
        out_shape.append(jax.ShapeDtypeStruct((nrows, ctot), dt))
    for r, ctot, cw in accs:
        out_specs.append(pl.BlockSpec((r, cw), lambda j, i: (0, j)))
        out_shape.append(jax.ShapeDtypeStruct((r, ctot), F32))
    n_in, n_out = len(ins), len(outs)

    def body(*refs):
        j = pl.program_id(0)
        i = pl.program_id(1)
        res = fn(i, j, *[r[...] for r in refs[:n_in]])
        for r, v in zip(refs[n_in:n_in + n_out], res[:n_out]):
            r[...] = v.astype(r.dtype)
        for r, v in zip(refs[n_in + n_out:], res[n_out:]):
            @pl.when(i == 0)
            def _(r=r, v=v):
                r[...] = v

            @pl.when(i > 0)
            def _(r=r, v=v):
                r[...] += v

    res = pl.pallas_call(
        body, name=name, grid=(ncol, nt), in_specs=in_specs, out_specs=out_specs, out_shape=out_shape,
        compiler_params=pltpu.CompilerParams(dimension_semantics=("arbitrary", "arbitrary")),
    )(*args)
    return res


def _shift_down(xcat, k):
    return xcat if k == 0 else pltpu.roll(xcat, k, 0)


def _shift_up(xcat, k):
    return xcat if k == 0 else pltpu.roll(xcat, xcat.shape[0] - k, 0)


def _with_prev(i, halo, x):
    return jnp.concatenate([jnp.where(i == 0, 0.0, halo), x], axis=0)


def _with_next(i, nt, x, halo):
    return jnp.concatenate([x, jnp.where(i == nt - 1, 0.0, halo)], axis=0)


def _rms_core(x, g):
    r = lax.rsqrt(jnp.mean(x * x, axis=-1, keepdims=True) + EPS)
    return x * r * g


def _rms_fwd(x, g, name):
    S = x.shape[0]
    return _rows(name, lambda i, j, xv, gv: [_rms_core(xv, gv)],
                 [("row", x, None, 0), ("const", g, None, 0)], [(D, D, 0, BF16)], tm=256, nrows=S)[0]


def _rms_bwd(x, g, du, dres, name):
    S = x.shape[0]

    def fn(i, j, xv, gv, duv, drv):
        _, vjp = jax.vjp(_rms_core, xv, gv)
        dx, dg = vjp(duv)
        return [drv + dx, dg]

    return _rows(name, fn, [("row", x, None, 0), ("const", g, None, 0), ("row", du, None, 0), ("row", dres, None, 0)],
                 [(D, D, 0, F32)], [(1, D, D)], tm=256, nrows=S)


def _final_loss(x, target, g):
    S = x.shape[0]

    def fn(i, j, xv, tv, gv):
        def f(xx, gg):
            err = _rms_core(xx, gg) - tv
            return 0.5 * jnp.sum(err * err) / D

        loss, vjp = jax.vjp(f, xv, gv)
        dx, dg = vjp(jnp.ones((), F32))
        return [dx, dg, jnp.zeros((1, LANES), F32) + loss]

    return _rows("final_loss", fn, [("row", x, None, 0), ("row", target, None, 0), ("const", g, None, 0)],
                 [(D, D, 0, F32)], [(1, D, D), (1, LANES, LANES)], tm=256, nrows=S)


def _attn_valid(n):
    qi = _iota((WIN, 2 * WIN), 0)
    kk = _iota((WIN, 2 * WIN), 1)
    rel = qi + WIN - kk
    return (rel >= 0) & (rel <= WIN) & ((kk >= WIN) | (n > 0))


def _attn_block(q, kp, kc, vp, vc, b0, b1, valid):
    k = jnp.concatenate([kp, kc], axis=0)
    v = jnp.concatenate([vp, vc], axis=0)
    lo = _iota((WIN, LANES), 1) < HD
    scale = 1.0 / math.sqrt(HD)
    os_, ls_ = [], []
    for hh, b in ((0, b0), (1, b1)):
        qm = jnp.where(lo if hh == 0 else ~lo, q, 0.0)
        s = _bdot_nt(qm, k) * scale + b
        s = jnp.where(valid, s, NEG)
        m = lax.stop_gradient(jnp.max(s, axis=1, keepdims=True))
        p = jnp.exp(s - m)
        l = jnp.sum(p, axis=1, keepdims=True)
        os_.append(_bdot_nn(p, v) / l)
        ls_.append(m + jnp.log(l))
    return jnp.where(lo, os_[0], os_[1]), jnp.where(lo, ls_[0], ls_[1])


def _residue_rows(r, d):
    return pl.ds(0, WIN) if d == 1 else pl.ds(r, WIN, stride=d)


def _for_residues(d, fn):
    if d == 1:
        fn(0, 0)
    else:
        lax.fori_loop(0, d, fn, 0)


def _bias_table(rel_bias, bucket, gi, name):
    def body(t_ref, b_ref, o_ref):
        h = 6 * gi + pl.program_id(0)
        b = b_ref[...]
        acc = jnp.zeros(b.shape, F32)
        for k in range(REL_BUCKETS):
            acc = jnp.where(b == k, t_ref[k, h], acc)
        o_ref[0] = acc

    return pl.pallas_call(
        body, name=name, grid=(6,),
        in_specs=[pl.BlockSpec(memory_space=pltpu.SMEM), pl.BlockSpec((WIN, 2 * WIN), lambda h: (0, 0))],
        out_specs=pl.BlockSpec((1, WIN, 2 * WIN), lambda h: (h, 0, 0)),
        out_shape=jax.ShapeDtypeStruct((6, WIN, 2 * WIN), F32),
    )(rel_bias, bucket)


def _attn_fwd(pa, bias, gi, name):
    S = pa.shape[0]
    d = DILATIONS[gi]
    bt = WIN * d
    nb = S // bt
    qb = 3 * gi

    def body(q_ref, kp_ref, kc_ref, vp_ref, vc_ref, b_ref, o_ref, l_ref):
        valid = _attn_valid(pl.program_id(1))
        b0, b1 = b_ref[0], b_ref[1]

        def residue(r, carry):
            sl = _residue_rows(r, d)
            o, lse = _attn_block(q_ref[sl, :], kp_ref[sl, :], kc_ref[sl, :], vp_ref[sl, :], vc_ref[sl, :], b0, b1, valid)
            o_ref[sl, :] = o
            l_ref[sl, :] = lse
            return carry

        _for_residues(d, residue)

    def spec(off, prev):
        if prev:
            return pl.BlockSpec((bt, LANES), lambda hp, n: (jnp.maximum(n - 1, 0), off + qb + hp))
        return pl.BlockSpec((bt, LANES), lambda hp, n: (n, off + qb + hp))

    ospec = pl.BlockSpec((bt, LANES), lambda hp, n: (n, hp))
    return pl.pallas_call(
        body, name=name, grid=(3, nb),
        in_specs=[spec(0, False), spec(9, True), spec(9, False), spec(18, True), spec(18, False),
                  pl.BlockSpec((2, WIN, 2 * WIN), lambda hp, n: (hp, 0, 0))],
        out_specs=[ospec, ospec],
        out_shape=[jax.ShapeDtypeStruct((S, GW), F32)] * 2,
        compiler_params=pltpu.CompilerParams(dimension_semantics=("parallel", "arbitrary")),
    )(pa, pa, pa, pa, pa, bias)


def _attn_bwd(pa, bias, do, dlse, db_in, dqkv, gi, name):
    S = pa.shape[0]
    d = DILATIONS[gi]
    bt = WIN * d
    nb = S // bt
    qb = 3 * gi

    def body(q_ref, kp_ref, kc_ref, vp_ref, vc_ref, b_ref, do_ref, dl_ref, dbi_ref, dqi_ref, dki_ref, dvi_ref,
             dq_ref, dk_ref, dv_ref, db_ref, ck, cv):
        n = pl.program_id(1)

        @pl.when(n == 0)
        def _():
            db_ref[...] = dbi_ref[...]
            ck[...] = jnp.zeros_like(ck)
            cv[...] = jnp.zeros_like(cv)

        @pl.when(n < nb)
        def _():
            f = functools.partial(_attn_block, valid=_attn_valid(n))
            b0, b1 = b_ref[0], b_ref[1]

            def residue(r, carry):
                sl = _residue_rows(r, d)
                cs = pl.ds(pl.multiple_of(r * WIN, WIN), WIN)
                _, vjp = jax.vjp(f, q_ref[sl, :], kp_ref[sl, :], kc_ref[sl, :], vp_ref[sl, :], vc_ref[sl, :], b0, b1)
                dq, dkp, dkc, dvp, dvc, db0, db1 = vjp((do_ref[sl, :].astype(F32), dl_ref[sl, :]))
                dq_ref[sl, :] = dq
                dk_ref[sl, :] = ck[cs, :] + dkp
                dv_ref[sl, :] = cv[cs, :] + dvp
                ck[cs, :] = dkc
                cv[cs, :] = dvc
                db_ref[0] += db0
                db_ref[1] += db1
                return carry

            _for_residues(d, residue)

        @pl.when(n == nb)
        def _():
            def residue(r, carry):
                sl = _residue_rows(r, d)
                cs = pl.ds(pl.multiple_of(r * WIN, WIN), WIN)
                dk_ref[sl, :] = ck[cs, :]
                dv_ref[sl, :] = cv[cs, :]
                return carry

            _for_residues(d, residue)

    def cur(n):
        return jnp.minimum(n, nb - 1)

    def spec(off, prev):
        if prev:
            return pl.BlockSpec((bt, LANES), lambda hp, n: (jnp.maximum(cur(n) - 1, 0), off + qb + hp))
        return pl.BlockSpec((bt, LANES), lambda hp, n: (cur(n), off + qb + hp))

    gspec = pl.BlockSpec((bt, LANES), lambda hp, n: (cur(n), hp))
    bspec = pl.BlockSpec((2, WIN, 2 * WIN), lambda hp, n: (hp, 0, 0))
    anyspec = pl.BlockSpec(memory_space=pl.ANY)
    qspec = pl.BlockSpec((bt, LANES), lambda hp, n: (cur(n), qb + hp))
    kspec = pl.BlockSpec((bt, LANES), lambda hp, n: (jnp.maximum(n - 1, 0), qb + hp))
    dq, dk, dv, db = pl.pallas_call(
        body, name=name, grid=(3, nb + 1),
        in_specs=[spec(0, False), spec(9, True), spec(9, False), spec(18, True), spec(18, False),
                  bspec, gspec, gspec, bspec, anyspec, anyspec, anyspec],
        out_specs=[qspec, kspec, kspec, bspec],
        out_shape=[jax.ShapeDtypeStruct((S, AW), F32)] * 3 + [jax.ShapeDtypeStruct((6, WIN, 2 * WIN), F32)],
        scratch_shapes=[pltpu.VMEM((bt, LANES), F32), pltpu.VMEM((bt, LANES), F32)],
        input_output_aliases={9: 0, 10: 1, 11: 2},
        compiler_params=pltpu.CompilerParams(dimension_semantics=("arbitrary", "arbitrary")),
    )(pa, pa, pa, pa, pa, bias, do, dlse, db_in, *dqkv)
    return (dq, dk, dv), db


def _mix_core(o0, o1, o2, l0, l1, l2):
    m = lax.stop_gradient(jnp.maximum(jnp.maximum(l0, l1), l2))
    e0, e1, e2 = jnp.exp(l0 - m), jnp.exp(l1 - m), jnp.exp(l2 - m)
    return (e0 * o0 + e1 * o1 + e2 * o2) / (e0 + e1 + e2)


def _mix_fwd(os_, ls_, name):
    S = os_[0].shape[0]
    ins = [("row", a, None, 0) for a in (*os_, *ls_)]
    return _rows(name, lambda i, j, *v: [_mix_core(*v)], ins, [(GW, GW, 0, BF16)], tm=256, nrows=S)[0]


def _mix_bwd(os_, ls_, datt, name):
    S = datt.shape[0]

    def fn(i, j, *v):
        _, vjp = jax.vjp(_mix_core, *v[:6])
        return list(vjp(v[6]))

    ins = [("row", a, None, 0) for a in (*os_, *ls_, datt)]
    outs = [(GW, GW, 0, F32)] * 6
    r = _rows(name, fn, ins, outs, tm=256, nrows=S)
    return r[:3], r[3:]


def _t5_bucket(dist):
    max_exact = REL_BUCKETS // 2
    is_small = dist < max_exact
    nf = jnp.maximum(dist, 1).astype(F32)
    large = max_exact + (jnp.log(nf / max_exact) / math.log(REL_MAX_DISTANCE / max_exact)
                         * (REL_BUCKETS - max_exact)).astype(jnp.int32)
    large = jnp.minimum(large, REL_BUCKETS - 1)
    return jnp.where(is_small, dist, large)


def _buckets(d):
    qi = jnp.arange(WIN)[:, None]
    kk = jnp.arange(2 * WIN)[None, :]
    rel = qi + WIN - kk
    return _t5_bucket(jnp.clip(rel, 0, None) * d)


def _pool_cnt(i, tm, w):
    pos = i * tm + _iota((tm, PG), 0) + 1
    return jnp.minimum(pos, w).astype(F32)


def _pool_d(i, tm, halo, u):
    ds = []
    for g, w in enumerate(POOL_WINDOWS):
        ug = u[:, g * PG:(g + 1) * PG]
        s = _with_prev(i, halo[:, g * PG:(g + 1) * PG], ug)
        step = 1
        while step < w:
            s = s + _shift_down(s, step)
            step *= 2
        ds.append(s[HALO:] / _pool_cnt(i, tm, w) - ug)
    return ds


def _pool_lin(d0, d1, d2, d3, w0, w1, w2, w3, scale):
    y = jnp.concatenate([_bdot_nn(d0, w0), _bdot_nn(d1, w1), _bdot_nn(d2, w2), _bdot_nn(d3, w3)], axis=1)
    return y * scale


def _pool_fwd(pb, pw, scale, name):
    S = pb.shape[0]
    tm = 256

    def fn(i, j, halo, u, w, sc):
        ds = _pool_d(i, tm, halo, u)
        return [_pool_lin(*ds, *[w[k].astype(F32) for k in range(4)], sc)]

    return _rows(name, fn, [("prev", pb, D, 0), ("row", pb, None, 0), ("const", pw, None, 0), ("const", scale, None, 0)],
                 [(D, D, 0, BF16)], tm=tm, nrows=S)[0]


def _pool_bwd(pb, pw, scale, dpo, name):
    S = pb.shape[0]
    tm = 256
    nt = S // tm

    def fn1(i, j, halo, u, w, sc, dy):
        ds = _pool_d(i, tm, halo, u)
        _, vjp = jax.vjp(_pool_lin, *ds, *[w[k].astype(F32) for k in range(4)], sc)
        g = vjp(dy)
        e = jnp.concatenate([g[k] / _pool_cnt(i, tm, wd) for k, wd in enumerate(POOL_WINDOWS)], axis=1)
        return [e, jnp.concatenate(g[4:8], axis=0), g[8]]

    e, dpw, dsc = _rows(name + "_a", fn1,
                        [("prev", pb, D, 0), ("row", pb, None, 0), ("const", pw, None, 0), ("const", scale, None, 0),
                         ("row", dpo, None, 0)],
                        [(D, D, 0, F32)], [(4 * PG, PG, PG), (1, D, D)], tm=tm, nrows=S)

    def fn2(i, j, ev, halo):
        outs = []
        for g, w in enumerate(POOL_WINDOWS):
            eg = ev[:, g * PG:(g + 1) * PG]
            s = _with_next(i, nt, eg, halo[:, g * PG:(g + 1) * PG])
            step = 1
            while step < w:
                s = s + _shift_up(s, step)
                step *= 2
            outs.append(s[:tm] - eg * _pool_cnt(i, tm, w))
        return [jnp.concatenate(outs, axis=1)]

    du = _rows(name + "_b", fn2, [("row", e, None, 0), ("next", e, D, 0)], [(D, D, 0, BF16)], tm=tm, nrows=S)[0]
    return du, dpw, dsc


def _conv_taps(i, halo, x, K):
    cat = _with_prev(i, halo, x)
    return [_shift_down(cat, K - 1 - k)[HALO:] for k in range(K)]


def _conv_pre(taps, w, b):
    acc = b
    for k, t in enumerate(taps):
        acc = acc + t * _row_pick(w, k)
    return acc


def _conv_t(name, dpre, w, K, ncol, cw, out_dtype):
    S, C = dpre.shape
    tm = 256
    nt = S // tm

    def fn(i, j, dp, halo, wv):
        cat = _with_next(i, nt, dp, halo)
        acc = jnp.zeros((tm, cw), F32)
        for k in range(K):
            acc = acc + _shift_up(cat, K - 1 - k)[:tm] * _row_pick(wv, k)
        return [acc]

    return _rows(name, fn, [("row", dpre, cw, 0), ("next", dpre, cw, 0), ("ccol", w, cw, 0)],
                 [(C, cw, 0, out_dtype)], tm=tm, nrows=S, ncol=ncol)[0]


CW = 256


def _ssd_conv_fwd(pc, w, b, name):
    S = pc.shape[0]
    base = D // CW

    def fn(i, j, halo, x, wv, bv):
        return [_silu(_conv_pre(_conv_taps(i, halo, x, 4), wv, bv))]

    return _rows(name, fn, [("prev", pc, CW, base), ("row", pc, CW, base), ("ccol", w, CW, 0), ("ccol", b, CW, 0)],
                 [(XBC, CW, 0, F32)], tm=256, nrows=S, ncol=XBC // CW)[0]


def _ssd_conv_bwd(pc, w, b, dy, name):
    S = pc.shape[0]
    base = D // CW

    def fn(i, j, halo, x, wv, bv, dyv):
        taps = _conv_taps(i, halo, x, 4)
        pre = _conv_pre(taps, wv, bv)
        sg = _sigmoid(pre)
        dpre = dyv * sg * (1.0 + pre * (1.0 - sg))
        dw = _stack_rows([jnp.sum(dpre * t, axis=0, keepdims=True) for t in taps], 4)
        return [dpre, dw, jnp.sum(dpre, axis=0, keepdims=True)]

    dpre, dw, db = _rows(name + "_a", fn,
                         [("prev", pc, CW, base), ("row", pc, CW, base), ("ccol", w, CW, 0), ("ccol", b, CW, 0),
                          ("row", dy, CW, 0)],
                         [(XBC, CW, 0, F32)], [(4, XBC, CW), (1, XBC, CW)], tm=256, nrows=S, ncol=XBC // CW)
    dx = _conv_t(name + "_b", dpre, w, 4, XBC // CW, CW, BF16)
    return dx, dw, db


NFC = D_FF // CW


def _ffn_act_fwd(h, w, b, name):
    S = h.shape[0]

    def fn(i, j, ha, a, hv, v, wa, wv, ba, bv):
        pa = _conv_pre(_conv_taps(i, ha, a, 3), wa, ba)
        pv = _conv_pre(_conv_taps(i, hv, v, 3), wv, bv)
        return [_silu(pa) * pv]

    return _rows(name, fn,
                 [("prev", h, CW, 0), ("row", h, CW, 0), ("prev", h, CW, NFC), ("row", h, CW, NFC),
                  ("ccol", w, CW, 0), ("ccol", w, CW, NFC), ("ccol", b, CW, 0), ("ccol", b, CW, NFC)],
                 [(D_FF, CW, 0, BF16)], tm=256, nrows=S, ncol=NFC)[0]


def _ffn_act_bwd(h, w, b, df, name):
    S = h.shape[0]
    tm = 256
    nt = S // tm

    def ext_taps(i, prev, x, nxt):
        cat = jnp.concatenate([jnp.where(i == 0, 0.0, prev), x, jnp.where(i == nt - 1, 0.0, nxt)], axis=0)
        return [_shift_down(cat, 2 - k)[HALO:] for k in range(3)]

    def fn(i, j, pa_, a, na, pv_, v, nv, wa, wv, ba, bv, dfv, dfn):
        ta = ext_taps(i, pa_, a, na)
        tv = ext_taps(i, pv_, v, nv)
        pa = _conv_pre(ta, wa, ba)
        pv = _conv_pre(tv, wv, bv)
        sg = _sigmoid(pa)
        dfe = jnp.concatenate([dfv.astype(F32), jnp.where(i == nt - 1, 0.0, dfn.astype(F32))], axis=0)
        dpa = dfe * pv * sg * (1.0 + pa * (1.0 - sg))
        dpv = dfe * pa * sg
        res = []
        for dp, wv_ in ((dpa, wa), (dpv, wv)):
            acc = jnp.zeros((tm, CW), F32)
            for k in range(3):
                acc = acc + _shift_up(dp, 2 - k)[:tm] * _row_pick(wv_, k)
            res.append(acc)
        for dp, taps in ((dpa, ta), (dpv, tv)):
            res.append(_stack_rows([jnp.sum(dp[:tm] * t[:tm], axis=0, keepdims=True) for t in taps], 3))
        for dp in (dpa, dpv):
            res.append(jnp.sum(dp[:tm], axis=0, keepdims=True))
        return res

    ins = []
    for base in (0, NFC):
        ins += [("prev", h, CW, base), ("row", h, CW, base), ("next", h, CW, base)]
    ins += [("ccol", w, CW, 0), ("ccol", w, CW, NFC), ("ccol", b, CW, 0), ("ccol", b, CW, NFC),
            ("row", df, CW, 0), ("next", df, CW, 0)]
    dha, dhv, dwa, dwv, dba, dbv = _rows(
        name, fn, ins, [(D_FF, CW, 0, BF16)] * 2, [(3, D_FF, CW)] * 2 + [(1, D_FF, CW)] * 2, tm=tm, nrows=S, ncol=NFC)
    return dha, dhv, jnp.concatenate([dwa, dwv], axis=1), jnp.concatenate([dba, dbv], axis=1)


NSLAB = D // LANES


def _ssd_chunk(xs, Bs, Cs, dtraw, dtb, alog, prev):
    lsz = SSD_CHUNK
    lane = _iota((lsz, LANES), 1)
    row = _iota((lsz, LANES), 0)
    dt = jnp.where(lane < SSD_HEADS, _softplus(dtraw + dtb), 0.0)
    a = dt * (-jnp.exp(alog))
    tril = row >= lane
    a_cs = _fdot(tril.astype(F32), a)
    a_cst = a_cs.T
    a_last = jnp.sum(a, axis=0, keepdims=True)
    lo = lane < HD
    top = row < HD
    cbs = [_bdot_nt(Cs[g], Bs[g]) for g in range(2)]
    ys, news = [], []
    for s in range(NSLAB):
        g = s // (NSLAB // 2)
        cols, lms, dts, als = [], [], [], []
        for hh in range(2):
            h = 2 * s + hh
            col = _lane_pick(a_cs, h)
            seg = col - _row_pick(a_cst, h)
            lms.append(jnp.exp(jnp.where(tril, seg, NEG)))
            cols.append(col)
            dts.append(_lane_pick(dt, h))
            als.append(_lane_pick(a_last, h))
        col_x = jnp.where(lo, cols[0], cols[1])
        al_x = jnp.where(lo, als[0], als[1])
        xc = xs[s] * jnp.where(lo, dts[0], dts[1])
        yd = jnp.where(lo, _bdot_nn(cbs[g] * lms[0], xc), _bdot_nn(cbs[g] * lms[1], xc))
        yoff = _bdot_nt(Cs[g], prev[s]) * jnp.exp(col_x)
        ys.append(yd + yoff)
        st = _bdot_tn(xc * jnp.exp(al_x - col_x), Bs[g])
        news.append(prev[s] * jnp.exp(jnp.where(top, als[0], als[1])) + st)
    return ys, news


def _ssd_scan_fwd(xbc_c, pd, dtb, alog, name):
    S = xbc_c.shape[0]
    nc = S // SSD_CHUNK

    def body(x_ref, b_ref, c_ref, dt_ref, dtb_ref, al_ref, y_ref, st_ref, state):
        c = pl.program_id(0)

        @pl.when(c == 0)
        def _():
            state[...] = jnp.zeros_like(state)

        xs = [x_ref[:, s * LANES:(s + 1) * LANES] for s in range(NSLAB)]
        Bs = [b_ref[:, g * SSD_N:(g + 1) * SSD_N] for g in range(2)]
        Cs = [c_ref[:, g * SSD_N:(g + 1) * SSD_N] for g in range(2)]
        prev = [state[s * LANES:(s + 1) * LANES, :] for s in range(NSLAB)]
        ys, news = _ssd_chunk(xs, Bs, Cs, dt_ref[...], dtb_ref[...], al_ref[...], prev)
        st_ref[0] = state[...]
        for s in range(NSLAB):
            y_ref[:, s * LANES:(s + 1) * LANES] = ys[s]
            state[s * LANES:(s + 1) * LANES, :] = news[s]

    return pl.pallas_call(
        body, name=name, grid=(nc,),
        in_specs=[pl.BlockSpec((SSD_CHUNK, D), lambda c: (c, 0)),
                  pl.BlockSpec((SSD_CHUNK, 2 * SSD_N), lambda c: (c, D // (2 * SSD_N))),
                  pl.BlockSpec((SSD_CHUNK, 2 * SSD_N), lambda c: (c, D // (2 * SSD_N) + 1)),
                  pl.BlockSpec((SSD_CHUNK, LANES), lambda c: (c, 0)),
                  pl.BlockSpec((1, LANES), lambda c: (0, 0)), pl.BlockSpec((1, LANES), lambda c: (0, 0))],
        out_specs=[pl.BlockSpec((SSD_CHUNK, D), lambda c: (c, 0)), pl.BlockSpec((1, D, SSD_N), lambda c: (c, 0, 0))],
        out_shape=[jax.ShapeDtypeStruct((S, D), F32), jax.ShapeDtypeStruct((nc, D, SSD_N), F32)],
        scratch_shapes=[pltpu.VMEM((D, SSD_N), F32)],
        compiler_params=pltpu.CompilerParams(dimension_semantics=("arbitrary",)),
    )(xbc_c, xbc_c, xbc_c, pd, dtb, alog)


def _ssd_scan_bwd(xbc_c, pd, dtb, alog, states, dy, dxs_skip, name):
    S = xbc_c.shape[0]
    nc = S // SSD_CHUNK

    def body(x_ref, b_ref, c_ref, dt_ref, dtb_ref, al_ref, st_ref, dy_ref, sk_ref,
             dx_ref, ddt_ref, ddtb_ref, dal_ref, dstate):
        c = pl.program_id(0)

        @pl.when(c == 0)
        def _():
            dstate[...] = jnp.zeros_like(dstate)
            ddtb_ref[...] = jnp.zeros_like(ddtb_ref)
            dal_ref[...] = jnp.zeros_like(dal_ref)

        xs = [x_ref[:, s * LANES:(s + 1) * LANES] for s in range(NSLAB)]
        Bs = [b_ref[:, g * SSD_N:(g + 1) * SSD_N] for g in range(2)]
        Cs = [c_ref[:, g * SSD_N:(g + 1) * SSD_N] for g in range(2)]
        prev = [st_ref[0, s * LANES:(s + 1) * LANES, :] for s in range(NSLAB)]
        _, vjp = jax.vjp(_ssd_chunk, xs, Bs, Cs, dt_ref[...], dtb_ref[...], al_ref[...], prev)
        dys = [dy_ref[:, s * LANES:(s + 1) * LANES] for s in range(NSLAB)]
        dnew = [dstate[s * LANES:(s + 1) * LANES, :] for s in range(NSLAB)]
        dxs, dBs, dCs, ddt, ddtb, dal, dprev = vjp((dys, dnew))
        for s in range(NSLAB):
            dx_ref[:, s * LANES:(s + 1) * LANES] = dxs[s] + sk_ref[:, s * LANES:(s + 1) * LANES]
            dstate[s * LANES:(s + 1) * LANES, :] = dprev[s]
        for g in range(2):
            dx_ref[:, D + g * SSD_N:D + (g + 1) * SSD_N] = dBs[g]
            dx_ref[:, D + 2 * SSD_N + g * SSD_N:D + 2 * SSD_N + (g + 1) * SSD_N] = dCs[g]
        ddt_ref[...] = ddt
        ddtb_ref[...] += ddtb
        dal_ref[...] += dal

    def rv(c):
        return nc - 1 - c

    return pl.pallas_call(
        body, name=name, grid=(nc,),
        in_specs=[pl.BlockSpec((SSD_CHUNK, D), lambda c: (rv(c), 0)),
                  pl.BlockSpec((SSD_CHUNK, 2 * SSD_N), lambda c: (rv(c), D // (2 * SSD_N))),
                  pl.BlockSpec((SSD_CHUNK, 2 * SSD_N), lambda c: (rv(c), D // (2 * SSD_N) + 1)),
                  pl.BlockSpec((SSD_CHUNK, LANES), lambda c: (rv(c), 0)),
                  pl.BlockSpec((1, LANES), lambda c: (0, 0)), pl.BlockSpec((1, LANES), lambda c: (0, 0)),
                  pl.BlockSpec((1, D, SSD_N), lambda c: (rv(c), 0, 0)),
                  pl.BlockSpec((SSD_CHUNK, D), lambda c: (rv(c), 0)),
                  pl.BlockSpec((SSD_CHUNK, D), lambda c: (rv(c), 0))],
        out_specs=[pl.BlockSpec((SSD_CHUNK, XBC), lambda c: (rv(c), 0)),
                   pl.BlockSpec((SSD_CHUNK, LANES), lambda c: (rv(c), 0)),
                   pl.BlockSpec((1, LANES), lambda c: (0, 0)), pl.BlockSpec((1, LANES), lambda c: (0, 0))],
        out_shape=[jax.ShapeDtypeStruct((S, XBC), F32), jax.ShapeDtypeStruct((S, LANES), F32),
                   jax.ShapeDtypeStruct((1, LANES), F32), jax.ShapeDtypeStruct((1, LANES), F32)],
        scratch_shapes=[pltpu.VMEM((D, SSD_N), F32)],
        compiler_params=pltpu.CompilerParams(dimension_semantics=("arbitrary",)),
    )(xbc_c, xbc_c, xbc_c, pd, dtb, alog, states, dy, dxs_skip)


def _ssd_post_core(y, xs, z, d128, nw):
    tm = y.shape[0]
    ex = (_iota((LANES, D), 1) // HD == _iota((LANES, D), 0)).astype(F32)
    d_x = jnp.sum(_fdot(jnp.broadcast_to(d128, (8, LANES)), ex), axis=0, keepdims=True) * 0.125
    y2 = (y + d_x * xs) * _silu(z)
    lo = _iota((tm, D), 1) < D // 2
    sq = y2 * y2
    ms0 = jnp.sum(jnp.where(lo, sq, 0.0), axis=-1, keepdims=True) / (D // 2)
    ms1 = jnp.sum(jnp.where(lo, 0.0, sq), axis=-1, keepdims=True) / (D // 2)
    r = jnp.where(lo, lax.rsqrt(ms0 + EPS), lax.rsqrt(ms1 + EPS))
    return y2 * r * nw


def _ssd_post_ins(y, xbc_c, pc, d128, nw):
    return [("row", y, None, 0), ("row", xbc_c, D, 0), ("row", pc, D, 0), ("const", d128, None, 0), ("const", nw, None, 0)]


def _ssd_post_fwd(y, xbc_c, pc, d128, nw, name):
    S = y.shape[0]
    return _rows(name, lambda i, j, *v: [_ssd_post_core(*v)], _ssd_post_ins(y, xbc_c, pc, d128, nw),
                 [(D, D, 0, BF16)], tm=128, nrows=S)[0]


def _ssd_post_bwd(y, xbc_c, pc, d128, nw, dout, name):
    S = y.shape[0]

    def fn(i, j, *v):
        _, vjp = jax.vjp(_ssd_post_core, *v[:5])
        return list(vjp(v[5]))

    return _rows(name, fn, _ssd_post_ins(y, xbc_c, pc, d128, nw) + [("row", dout, None, 0)],
                 [(D, D, 0, F32), (D, D, 0, F32), (D, D, 0, BF16)], [(1, LANES, LANES), (1, D, D)], tm=128, nrows=S)


def _gates_core(g0, g1, g2, b0, b1, b2, ya, yb, yc):
    return _sigmoid(g0 + b0) * ya + _sigmoid(g1 + b1) * yb + _sigmoid(g2 + b2) * yc


def _gate_parts(pdv, bv):
    gp = pltpu.roll(pdv, SEC_D - 16, 1)
    return [gp[:, k * D:(k + 1) * D] for k in range(3)] + [bv[:, k * D:(k + 1) * D] for k in range(3)]


def _gates_fwd(pd, bg, ya, yb, yc, name):
    S = pd.shape[0]

    def fn(i, j, pdv, bv, a, b, c):
        return [_gates_core(*_gate_parts(pdv, bv), a, b, c)]

    return _rows(name, fn, [("row", pd, None, 0), ("const", bg, None, 0), ("row", ya, None, 0), ("row", yb, None, 0),
                            ("row", yc, None, 0)], [(D, D, 0, BF16)], tm=128, nrows=S)[0]


def _gates_bwd(pd, bg, ya, yb, yc, dm, name):
    S = pd.shape[0]
    tm = 128

    def fn(i, j, pdv, bv, a, b, c, dmv):
        _, vjp = jax.vjp(_gates_core, *_gate_parts(pdv, bv), a, b, c)
        g = vjp(dmv)
        return [g[6], g[7], g[8], jnp.concatenate(g[0:3], axis=1), jnp.concatenate(g[3:6], axis=1)]

    return _rows(name, fn, [("row", pd, None, 0), ("const", bg, None, 0), ("row", ya, None, 0), ("row", yb, None, 0),
                            ("row", yc, None, 0), ("row", dm, None, 0)],
                 [(D, D, 0, BF16)] * 3 + [(3 * D, 3 * D, 0, BF16)], [(1, 3 * D, 3 * D)], tm=tm, nrows=S)


def _adamw(w, g, m, v, name):
    rows, C = w.shape
    tm = _pick(rows, [t for t in (512, 256, 128, 64, 32, 16, 8) if t * C <= ADAM_TILE])

    def fn(i, j, wv, gv, mv, vv):
        m2 = ADAM_B1 * mv + (1.0 - ADAM_B1) * gv
        v2 = ADAM_B2 * vv + (1.0 - ADAM_B2) * jnp.square(gv)
        m_hat = m2 / (1.0 - ADAM_B1 ** ADAM_STEP)
        v_hat = v2 / (1.0 - ADAM_B2 ** ADAM_STEP)
        delta = -ADAM_LR * (m_hat / (jnp.sqrt(v_hat) + ADAM_EPS) + ADAM_WD * wv)
        return [delta, m2, v2]

    return _rows(name, fn, [("row", a, None, 0) for a in (w, g, m, v)], [(C, C, 0, F32)] * 3, tm=tm, nrows=rows)


def _position():
    return lax.axis_index("x"), lax.axis_index("y"), lax.axis_index("c")


def _other_chips(x, y):
    return [(1 - x, y), (x, 1 - y), (1 - x, 1 - y)]


_HBM = pl.BlockSpec(memory_space=pltpu.HBM)


def _gather_parts(half, lo, n):
    def copies(p_ref, out_ref, send_sems, recv_sems):
        x, y, c = _position()
        sibling = (x, y, 1 - c)
        chips = _other_chips(x, y)

        def slab(chip, h):
            return out_ref.at[2 * chip[0] + chip[1], pl.ds(h * half + lo, n), :]

        def copy(k, src, dst, to):
            return pltpu.make_async_remote_copy(src_ref=src, dst_ref=dst, send_sem=send_sems.at[k],
                                                recv_sem=recv_sems.at[k], device_id=to, device_id_type=MESH)

        first = [copy(j, p_ref.at[pl.ds(c * half + lo, n), :], slab((x, y), c), (*chip, c)) for j, chip in enumerate(chips)]
        passed = [copy(3 + j, slab(chip, c), slab(chip, c), sibling) for j, chip in enumerate(chips)]
        from_chips = [copy(j, slab(chip, c), slab(chip, c), (x, y, c)) for j, chip in enumerate(chips)]
        from_sibling = [copy(3 + j, slab(chip, 1 - c), slab(chip, 1 - c), (x, y, c)) for j, chip in enumerate(chips)]
        return first, passed, from_chips, from_sibling

    def start(ins, outs, scr):
        for cp in copies(ins[0], outs[0], *scr)[0]:
            cp.start()

    def finish(ins, outs, scr):
        first, passed, from_chips, from_sibling = copies(ins[0], outs[0], *scr)
        for j in range(3):
            from_chips[j].wait_recv()
            passed[j].start()
        for cp in from_sibling:
            cp.wait_recv()
        for cp in first + passed:
            cp.wait_send()

    return start, finish


def _rs_chip_parts(lo, n):
    def copies(h_ref, out_ref, send_sems, recv_sems):
        x, y, c = _position()
        return [pltpu.make_async_remote_copy(src_ref=h_ref.at[2 * chip[0] + chip[1], pl.ds(lo, n), :],
                                             dst_ref=out_ref.at[j, pl.ds(lo, n), :],
                                             send_sem=send_sems.at[j], recv_sem=recv_sems.at[j],
                                             device_id=(*chip, c), device_id_type=MESH)
                for j, chip in enumerate(_other_chips(x, y))]

    def start(ins, outs, scr):
        for cp in copies(ins[0], outs[0], *scr):
            cp.start()

    def finish(ins, outs, scr):
        for cp in copies(ins[0], outs[0], *scr):
            cp.wait()

    return start, finish


class _Stream:
    def __init__(self, src, buf, parts, nsem, units, name):
        self.src, self.buf, self.parts, self.nsem, self.name = src, buf, parts, nsem, name
        self.next, self.units = 0, units

    def _scratch(self):
        return [pltpu.SemaphoreType.DMA((self.nsem,)), pltpu.SemaphoreType.DMA((self.nsem,))]

    def _take(self, units):
        units = min(units, self.units - self.next)
        lo = self.next * 16
        self.next += units
        return lo, units * 16

    def _set(self, outs):
        self.buf = outs[0]

    def hook(self, units):
        lo, n = self._take(units)
        if n == 0:
            return None
        start, finish = self.parts(lo, n)
        return _Hook([self.src, self.buf], [jax.ShapeDtypeStruct(self.buf.shape, self.buf.dtype)], {1: 0},
                     self._scratch(), start, finish, self._set)

    def drain(self):
        lo, n = self._take(self.units)
        if n:
            start, finish = self.parts(lo, n)

            def body(s_ref, b_ref, o_ref, send_sems, recv_sems):
                args = ((s_ref, b_ref), (o_ref,), (send_sems, recv_sems))
                start(*args)
                finish(*args)

            self.buf = pl.pallas_call(
                body, name=self.name, in_specs=[_ANY, _ANY], out_specs=_ANY,
                out_shape=jax.ShapeDtypeStruct(self.buf.shape, self.buf.dtype),
                scratch_shapes=self._scratch(), input_output_aliases={1: 0},
            )(self.src, self.buf)
        return self.buf


def _rs_pair_exchange(g, name):
    _, R, C = g.shape
    Rh = R // 2

    def body(g_ref, out_ref, send_sem, recv_sem):
        x, y, c = _position()
        src = g_ref.at[pl.ds(0, 4), pl.ds((1 - c) * Rh, Rh), :]
        cp = pltpu.make_async_remote_copy(src_ref=src, dst_ref=out_ref, send_sem=send_sem,
                                          recv_sem=recv_sem, device_id=(x, y, 1 - c), device_id_type=MESH)
        cp.start()
        cp.wait()

    return pl.pallas_call(
        body, name=name, in_specs=[_HBM], out_specs=_HBM,
        out_shape=jax.ShapeDtypeStruct((4, Rh, C), g.dtype),
        scratch_shapes=[pltpu.SemaphoreType.DMA, pltpu.SemaphoreType.DMA],
    )(g)


def _rs_swap(r, name):
    Rh, C = r.shape

    def body(r_ref, out_ref, send_sem, recv_sem):
        x, y, c = _position()
        cp = pltpu.make_async_remote_copy(src_ref=r_ref, dst_ref=out_ref, send_sem=send_sem,
                                          recv_sem=recv_sem, device_id=(x, y, 1 - c), device_id_type=MESH)
        cp.start()
        cp.wait()

    return pl.pallas_call(
        body, name=name, in_specs=[_HBM], out_specs=_HBM,
        out_shape=jax.ShapeDtypeStruct((Rh, C), r.dtype),
        scratch_shapes=[pltpu.SemaphoreType.DMA, pltpu.SemaphoreType.DMA],
    )(r)


def _rs_add_pair(g, recv, cidx, name):
    _, R, C = g.shape
    Rh = R // 2
    tm = _pick(Rh, (400, 280, 200, 160, 80, 40, 16, 8))
    nt = Rh // tm

    def body(c_ref, g_ref, r_ref, o_ref):
        o_ref[...] = (g_ref[...].astype(F32) + r_ref[...].astype(F32)).astype(o_ref.dtype)

    return pl.pallas_call(
        body, name=name,
        grid_spec=pltpu.PrefetchScalarGridSpec(
            num_scalar_prefetch=1, grid=(4, nt),
            in_specs=[pl.BlockSpec((1, tm, C), lambda k, i, cr: (k, cr[0] * nt + i, 0)),
                      pl.BlockSpec((1, tm, C), lambda k, i, cr: (k, i, 0))],
            out_specs=pl.BlockSpec((1, tm, C), lambda k, i, cr: (k, i, 0))),
        out_shape=jax.ShapeDtypeStruct((4, Rh, C), BF16),
    )(cidx, g, recv)


def _rs_add_chips(h, recv, chip_idx, name):
    _, Rh, C = h.shape
    tm = _pick(Rh, (400, 280, 200, 160, 80, 40, 16, 8))

    def body(c_ref, h_ref, r_ref, o_ref):
        acc = h_ref[0].astype(F32)
        for j in range(3):
            acc = acc + r_ref[j].astype(F32)
        o_ref[...] = acc

    return pl.pallas_call(
        body, name=name,
        grid_spec=pltpu.PrefetchScalarGridSpec(
            num_scalar_prefetch=1, grid=(Rh // tm,),
            in_specs=[pl.BlockSpec((1, tm, C), lambda i, cr: (cr[0], i, 0)), pl.BlockSpec((3, tm, C), lambda i, cr: (0, i, 0))],
            out_specs=pl.BlockSpec((tm, C), lambda i, cr: (i, 0))),
        out_shape=jax.ShapeDtypeStruct((Rh, C), F32),
    )(chip_idx, h, recv)


def _all_reduce_small(vec, name):
    n, C = vec.shape

    def body(v_ref, out_ref, buf, send_sems, recv_sems):
        x, y, c = _position()

        def flip(k):
            return ((1 - x) if k & 4 else x, (1 - y) if k & 2 else y, (1 - c) if k & 1 else c)

        def idx(p):
            return 4 * p[0] + 2 * p[1] + p[2]

        me = idx((x, y, c))
        buf[me] = v_ref[...]
        cps = [pltpu.make_async_remote_copy(src_ref=v_ref, dst_ref=buf.at[me], send_sem=send_sems.at[k - 1],
                                            recv_sem=recv_sems.at[k - 1], device_id=flip(k), device_id_type=MESH)
               for k in range(1, 8)]
        for cp in cps:
            cp.start()
        for k in range(1, 8):
            pltpu.make_async_remote_copy(src_ref=v_ref, dst_ref=buf.at[idx(flip(k))], send_sem=send_sems.at[k - 1],
                                         recv_sem=recv_sems.at[k - 1], device_id=flip(k), device_id_type=MESH).wait_recv()
        for cp in cps:
            cp.wait_send()
        acc = buf[0]
        for s in range(1, 8):
            acc = acc + buf[s]
        out_ref[...] = acc

    return pl.pallas_call(
        body, name=name,
        in_specs=[pl.BlockSpec(memory_space=pltpu.VMEM)], out_specs=pl.BlockSpec(memory_space=pltpu.VMEM),
        out_shape=jax.ShapeDtypeStruct((n, C), F32),
        scratch_shapes=[pltpu.VMEM((8, n, C), F32), pltpu.SemaphoreType.DMA((7,)), pltpu.SemaphoreType.DMA((7,))],
    )(vec)


BIG = (("w_in", (D, IN_WIDTH // 4), "cols"), ("w_a", (GW, D // 4), "cols"), ("pool_w", (4, PG // 4, PG), "pool"),
       ("w_b", (D // 4, D), "rows"), ("w_c", (D // 4, D), "rows"), ("w_o", (D // 4, D), "rows"),
       ("ffn_w_up", (D, 2 * D_FF // 4), "cols"), ("ffn_w_down", (D_FF // 4, D), "rows"))
def _pack_rows(s):
    k = math.prod(s) // D
    return -(-k // 16) * 16, k


PACK_ROWS = sum(_pack_rows(s)[0] for _, s, _ in BIG)
PACK_PAD = -(-PACK_ROWS // 32) * 32


def _pad_rows(v, rows):
    pad = [(0, 0)] * v.ndim
    pad[-2] = (0, rows - v.shape[-2])
    return jnp.pad(v, pad) if rows > v.shape[-2] else v


def _pack_blocks(blocks, dtype):
    lead = blocks["w_in"].shape[:-2]
    flat = []
    for n, s, how in BIG:
        v = blocks[n].astype(dtype)
        if how == "cols":
            v = jnp.swapaxes(v, -1, -2)
        flat.append(_pad_rows(v.reshape(*lead, -1, D), _pack_rows(s)[0]))
    flat.append(jnp.zeros((*lead, PACK_PAD - PACK_ROWS, D), dtype))
    return jnp.concatenate(flat, axis=-2)


def _unpack_blocks(pack):
    out, r = {}, 0
    for n, s, how in BIG:
        rows, k = _pack_rows(s)
        v = pack[r:r + k, :]
        out[n] = v.reshape(s[1], s[0]).T if how == "cols" else v.reshape(s)
        r += rows
    return out


def _operands(allp):
    out, r = {}, 0
    for n, s, how in BIG:
        rows, k = _pack_rows(s)
        v = allp[:, r:r + k, :]
        if how == "cols":
            out[n] = v.reshape(4 * s[1], s[0])
        elif how == "rows":
            out[n] = v.reshape(4 * s[0], s[1])
        else:
            out[n] = v.reshape(4, *s).transpose(1, 0, 2, 3).reshape(4, PG, PG)
        r += rows
    return out


def _pack_operands(g, dtype):
    flat = []
    for n, s, how in BIG:
        v = g[n].astype(dtype)
        if how == "pool":
            v = v.reshape(4, 4, s[1], s[2]).transpose(1, 0, 2, 3)
        flat.append(_pad_rows(v.reshape(4, -1, D), _pack_rows(s)[0]))
    flat.append(jnp.zeros((4, PACK_PAD - PACK_ROWS, D), dtype))
    return jnp.concatenate(flat, axis=1)


def _layer_fwd(x, w, sm, bias, hk):
    u = _rms_fwd(x, sm["ln1_g"], "rms1")
    pa = _mm(u, w["in_a"], tb=True, name="in_a", hook=hk("in_a"))
    pb = _mm(u, w["in_b"], tb=True, name="in_b", hook=hk("in_b"))
    pc = _mm(u, w["in_c"], tb=True, name="in_c", hook=hk("in_c"))
    pd = _mm(u, w["in_d"], tb=True, name="in_d", hook=hk("in_d"))
    os_, ls_ = [], []
    for gi in range(3):
        o, l = _attn_fwd(pa, bias[gi], gi, "attn_fwd%d" % gi)
        os_.append(o)
        ls_.append(l)
    att = _mix_fwd(os_, ls_, "mix_fwd")
    ya = _mm(att, w["w_a"], tb=True, name="mm_wa")
    pool_o = _pool_fwd(pb, w["pool_w"], sm["pool_scale"], "pool_fwd")
    yb = _mm(pool_o, w["w_b"], name="mm_wb")
    xbc_c = _ssd_conv_fwd(pc, sm["ssd_conv_w"], sm["ssd_conv_b"], "ssd_conv_fwd")
    y_scan, states = _ssd_scan_fwd(xbc_c, pd, sm["ssd_dt_bias"], sm["ssd_a_log"], "ssd_scan_fwd")
    ssd_o = _ssd_post_fwd(y_scan, xbc_c, pc, sm["ssd_d"], sm["ssd_norm_w"], "ssd_post_fwd")
    yc = _mm(ssd_o, w["w_c"], name="mm_wc")
    merged = _gates_fwd(pd, sm["b_gate"], ya, yb, yc, "gates_fwd")
    x1 = _mm(merged, w["w_o"], add=x, name="mm_wo", hook=hk("mm_wo"))
    u2 = _rms_fwd(x1, sm["ln2_g"], "rms2")
    h = _mm(u2, w["ffn_w_up"], tb=True, name="mm_up", hook=hk("mm_up"))
    f = _ffn_act_fwd(h, sm["ffn_conv_w"], sm["ffn_conv_b"], "ffn_act_fwd")
    x2 = _mm(f, w["ffn_w_down"], add=x1, name="mm_down", hook=hk("mm_down"))
    saved = dict(x=x, u=u, pa=pa, pb=pb, pc=pc, pd=pd, os=os_, ls=ls_, att=att, ya=ya, yb=yb, yc=yc, pool_o=pool_o,
                 xbc_c=xbc_c, y_scan=y_scan, states=states, ssd_o=ssd_o, merged=merged, x1=x1, u2=u2, h=h, f=f)
    return x2, saved


def _layer_bwd(dx2, w, sm, bias, dbs, sv, hk):
    gw, gs = {}, {}
    S = dx2.shape[0]

    def gmm(a, b, name):
        return _mm(a, b, ta=True, out_dtype=BF16, name=name, hook=hk(name))

    df = _mm(dx2, w["ffn_w_down"], tb=True, name="d_f", hook=hk("d_f"))
    gw["ffn_w_down"] = gmm(sv["f"], dx2, "g_down")
    dha, dhv, gs["ffn_conv_w"], gs["ffn_conv_b"] = _ffn_act_bwd(sv["h"], sm["ffn_conv_w"], sm["ffn_conv_b"], df, "ffn_act_bwd")
    du2 = _mm(dha, w["up_a"], name="d_u2_a", hook=hk("d_u2_a"))
    du2 = _mm(dhv, w["up_v"], add=du2, name="d_u2_v", hook=hk("d_u2_v"))
    gw["ffn_w_up"] = jnp.concatenate([gmm(dha, sv["u2"], "g_up_a"), gmm(dhv, sv["u2"], "g_up_v")], axis=0)
    dx1, gs["ln2_g"] = _rms_bwd(sv["x1"], sm["ln2_g"], du2, dx2, "rms2_bwd")
    dmerged = _mm(dx1, w["w_o"], tb=True, name="d_merged", hook=hk("d_merged"))
    gw["w_o"] = gmm(sv["merged"], dx1, "g_wo")
    dya, dyb, dyc, dgate, gs["b_gate"] = _gates_bwd(
        sv["pd"], sm["b_gate"], sv["ya"], sv["yb"], sv["yc"], dmerged, "gates_bwd")
    dssd_o = _mm(dyc, w["w_c"], tb=True, name="d_ssd_o")
    gw["w_c"] = gmm(sv["ssd_o"], dyc, "g_wc")
    dy_scan, dxs_skip, dz, gs["ssd_d"], gs["ssd_norm_w"] = _ssd_post_bwd(
        sv["y_scan"], sv["xbc_c"], sv["pc"], sm["ssd_d"], sm["ssd_norm_w"], dssd_o, "ssd_post_bwd")
    dxbc_c, ddt, gs["ssd_dt_bias"], gs["ssd_a_log"] = _ssd_scan_bwd(
        sv["xbc_c"], sv["pd"], sm["ssd_dt_bias"], sm["ssd_a_log"], sv["states"], dy_scan, dxs_skip, "ssd_scan_bwd")
    dxbc, gs["ssd_conv_w"], gs["ssd_conv_b"] = _ssd_conv_bwd(sv["pc"], sm["ssd_conv_w"], sm["ssd_conv_b"], dxbc_c, "ssd_conv_bwd")
    dpool_o = _mm(dyb, w["w_b"], tb=True, name="d_pool_o")
    gw["w_b"] = gmm(sv["pool_o"], dyb, "g_wb")
    dpb, dpw, gs["pool_scale"] = _pool_bwd(sv["pb"], w["pool_w"], sm["pool_scale"], dpool_o, "pool_bwd")
    gw["pool_w"] = dpw.reshape(4, PG, PG)
    datt = _mm(dya, w["w_a"], name="d_att")
    gw["w_a"] = gmm(dya, sv["att"], "g_wa")
    dos, dls = _mix_bwd(sv["os"], sv["ls"], datt, "mix_bwd")
    dqkv = tuple(lax.empty((S, AW), F32) for _ in range(3))
    dbs = list(dbs)
    for gi in range(3):
        dqkv, dbs[gi] = _attn_bwd(sv["pa"], bias[gi], dos[gi], dls[gi], dbs[gi], dqkv, gi, "attn_bwd%d" % gi)
    u = sv["u"]
    pieces = [(dqkv[0], "wq"), (dqkv[1], "wk"), (dqkv[2], "wv"), (dpb, "in_b"), (dz, "wz"), (dxbc, "wxbc"),
              (ddt, "wdt"), (dgate, "wgate")]
    du = None
    g_in = []
    for dp, key in pieces:
        du = _mm(dp, w[key], add=du, name="d_u_" + key, hook=hk("d_u_" + key))
        g = gmm(dp, u, "g_in_" + key)
        g_in.append(g[:SSD_HEADS] if key == "wdt" else g)
    gw["w_in"] = jnp.concatenate(g_in, axis=0)
    dx, gs["ln1_g"] = _rms_bwd(sv["x"], sm["ln1_g"], du, dx1, "rms1_bwd")
    return dx, gw, gs, dbs


SMALL_LAYER = ("ln1_g", "b_gate", "pool_scale", "ssd_conv_w", "ssd_conv_b", "ssd_dt_bias", "ssd_a_log", "ssd_d",
               "ssd_norm_w", "ln2_g", "ffn_conv_w", "ffn_conv_b")


def _pad_lanes(v):
    return jnp.pad(v, (0, LANES - v.shape[0])).reshape(1, LANES)


def _layer_weights(ops):
    wt = ops["w_in"]
    o1, o2, o3 = SEC_A, SEC_A + SEC_B, SEC_A + SEC_B + SEC_C
    w = dict(ops)
    w["in_a"] = wt[:o1]
    w["in_b"] = wt[o1:o2]
    w["in_c"] = wt[o2:o3]
    w["in_d"] = jnp.pad(wt[o3:], ((0, SEC_D - (IN_WIDTH - o3)), (0, 0)))
    w["wq"], w["wk"], w["wv"] = wt[:AW], wt[AW:2 * AW], wt[2 * AW:o1]
    w["wz"], w["wxbc"] = wt[o2:o2 + D], wt[o2 + D:o3]
    w["wdt"] = jnp.pad(wt[o3:o3 + SSD_HEADS], ((0, LANES - SSD_HEADS), (0, 0)))
    w["wgate"] = wt[o3 + SSD_HEADS:]
    w["up_a"], w["up_v"] = ops["ffn_w_up"][:D_FF], ops["ffn_w_up"][D_FF:]
    return w


def _layer_small(p, i):
    sm = {n: p[n][i] for n in SMALL_LAYER}
    out = {}
    for n, v in sm.items():
        if n in ("ssd_dt_bias", "ssd_a_log", "ssd_d"):
            out[n] = _pad_lanes(v)
        elif v.ndim == 1:
            out[n] = v.reshape(1, -1)
        else:
            out[n] = v
    return out


def _local_step(x, target, rel_bias, final_g, layer_full, small, fwd_hooks=None, bwd_hooks=None, after_bwd=None):
    nl = small["ln1_g"].shape[0]
    buckets = [_buckets(d).astype(jnp.int32) for d in DILATIONS]
    bias = [_bias_table(rel_bias, buckets[gi], gi, "bias_table%d" % gi) for gi in range(3)]
    no_hooks = lambda i: (lambda name: None)
    fwd_hooks = fwd_hooks or no_hooks
    bwd_hooks = bwd_hooks or no_hooks
    saved, ws, sms = [], [], []
    h = x
    for i in range(nl):
        w = _layer_weights(layer_full(i))
        sm = _layer_small(small, i)
        h, sv = _layer_fwd(h, w, sm, bias, fwd_hooks(i))
        saved.append(sv)
        ws.append(w)
        sms.append(sm)
    dh, dfinal, loss = _final_loss(h, target, final_g.reshape(1, D))
    gws, gss = [None] * nl, [None] * nl
    dbs = [jnp.zeros((6, WIN, 2 * WIN), F32)] * 3
    for i in reversed(range(nl)):
        dh, gws[i], gss[i], dbs = _layer_bwd(dh, ws[i], sms[i], bias, dbs, saved[i], bwd_hooks(i))
        if after_bwd is not None:
            after_bwd(i, gws[i])
    drel = []
    for gi in range(3):
        onehot = jnp.pad(jax.nn.one_hot(buckets[gi].reshape(-1), REL_BUCKETS, dtype=BF16), ((0, 0), (0, LANES - REL_BUCKETS)))
        drel.append(_mm(dbs[gi].reshape(6, WIN * 2 * WIN), onehot, name="g_relb"))
    return loss, dh, gws, gss, dfinal, jnp.concatenate(drel, axis=0)


WEIGHTS = ("rel_bias", "ln1_g", "w_in", "b_gate", "w_a", "pool_w", "pool_scale", "w_b", "ssd_conv_w", "ssd_conv_b",
           "ssd_dt_bias", "ssd_a_log", "ssd_d", "ssd_norm_w", "w_c", "w_o", "ln2_g", "ffn_w_up", "ffn_conv_w",
           "ffn_conv_b", "ffn_w_down", "final_g")
BIG_NAMES = tuple(n for n, _, _ in BIG)
SHARDED_SMALL = {"ssd_conv_w": XBC // 4, "ffn_conv_w": 2 * D_FF // 4}


def _to_rows(flat):
    n = flat.shape[0]
    rows = -(-n // LANES)
    rows = -(-rows // 8) * 8
    return jnp.pad(flat, (0, rows * LANES - n)).reshape(rows, LANES)


def _flatten(tree, names):
    return jnp.concatenate([tree[n].reshape(-1) for n in names])


def _unflatten(flat, shapes, names):
    out, o = {}, 0
    for n in names:
        k = math.prod(shapes[n])
        out[n] = flat[o:o + k].reshape(shapes[n])
        o += k
    return out


def kernel(x, rel_bias, ln1_g, w_in, b_gate, w_a, pool_w, pool_scale, w_b, ssd_conv_w, ssd_conv_b, ssd_dt_bias, ssd_a_log, ssd_d, ssd_norm_w, w_c, w_o, ln2_g, ffn_w_up, ffn_conv_w, ffn_conv_b, ffn_w_down, final_g, loss_target, m_rel_bias, m_ln1_g, m_w_in, m_b_gate, m_w_a, m_pool_w, m_pool_scale, m_w_b, m_ssd_conv_w, m_ssd_conv_b, m_ssd_dt_bias, m_ssd_a_log, m_ssd_d, m_ssd_norm_w, m_w_c, m_w_o, m_ln2_g, m_ffn_w_up, m_ffn_conv_w, m_ffn_conv_b, m_ffn_w_down, m_final_g, v_rel_bias, v_ln1_g, v_w_in, v_b_gate, v_w_a, v_pool_w, v_pool_scale, v_w_b, v_ssd_conv_w, v_ssd_conv_b, v_ssd_dt_bias, v_ssd_a_log, v_ssd_d, v_ssd_norm_w, v_w_c, v_w_o, v_ln2_g, v_ffn_w_up, v_ffn_conv_w, v_ffn_conv_b, v_ffn_w_down, v_final_g):
    W = dict(rel_bias=rel_bias, ln1_g=ln1_g, w_in=w_in, b_gate=b_gate, w_a=w_a, pool_w=pool_w, pool_scale=pool_scale,
             w_b=w_b, ssd_conv_w=ssd_conv_w, ssd_conv_b=ssd_conv_b, ssd_dt_bias=ssd_dt_bias, ssd_a_log=ssd_a_log,
             ssd_d=ssd_d, ssd_norm_w=ssd_norm_w, w_c=w_c, w_o=w_o, ln2_g=ln2_g, ffn_w_up=ffn_w_up,
             ffn_conv_w=ffn_conv_w, ffn_conv_b=ffn_conv_b, ffn_w_down=ffn_w_down, final_g=final_g)
    M = dict(rel_bias=m_rel_bias, ln1_g=m_ln1_g, w_in=m_w_in, b_gate=m_b_gate, w_a=m_w_a, pool_w=m_pool_w,
             pool_scale=m_pool_scale, w_b=m_w_b, ssd_conv_w=m_ssd_conv_w, ssd_conv_b=m_ssd_conv_b,
             ssd_dt_bias=m_ssd_dt_bias, ssd_a_log=m_ssd_a_log, ssd_d=m_ssd_d, ssd_norm_w=m_ssd_norm_w, w_c=m_w_c,
             w_o=m_w_o, ln2_g=m_ln2_g, ffn_w_up=m_ffn_w_up, ffn_conv_w=m_ffn_conv_w, ffn_conv_b=m_ffn_conv_b,
             ffn_w_down=m_ffn_w_down, final_g=m_final_g)
    V = dict(rel_bias=v_rel_bias, ln1_g=v_ln1_g, w_in=v_w_in, b_gate=v_b_gate, w_a=v_w_a, pool_w=v_pool_w,
             pool_scale=v_pool_scale, w_b=v_w_b, ssd_conv_w=v_ssd_conv_w, ssd_conv_b=v_ssd_conv_b,
             ssd_dt_bias=v_ssd_dt_bias, ssd_a_log=v_ssd_a_log, ssd_d=v_ssd_d, ssd_norm_w=v_ssd_norm_w, w_c=v_w_c,
             w_o=v_w_o, ln2_g=v_ln2_g, ffn_w_up=v_ffn_w_up, ffn_conv_w=v_ffn_conv_w, ffn_conv_b=v_ffn_conv_b,
             ffn_w_down=v_ffn_w_down, final_g=v_final_g)
    nl = ln1_g.shape[0]
    px, py, pc_ = _position()
    chip = 2 * px + py
    cidx = jnp.reshape(pc_, (1,)).astype(jnp.int32)
    chip_idx = jnp.reshape(chip, (1,)).astype(jnp.int32)

    placed = {}
    for n, cs in SHARDED_SMALL.items():
        full = jnp.zeros(W[n].shape[:-1] + (4 * cs,), F32)
        full = lax.dynamic_update_slice(full, W[n], (0, 0, chip * cs))
        placed[n] = jnp.where(pc_ == 0, full, 0.0)
    names_sh = tuple(SHARDED_SMALL)
    shapes_sh = {n: placed[n].shape for n in names_sh}
    got = _all_reduce_small(_to_rows(_flatten(placed, names_sh)), "gather_small")
    small = {n: W[n] for n in SMALL_LAYER}
    small.update(_unflatten(got.reshape(-1), shapes_sh, names_sh))

    packs = _pack_blocks({n: W[n] for n in BIG_NAMES}, BF16)

    half = PACK_PAD // 2
    units = half // 16

    def share(weights, total):
        tot = sum(weights.values())
        return {n: math.ceil(total * v / tot) for n, v in weights.items()}

    gathers = {}

    def gather(i):
        if i not in gathers:
            buf = lax.dynamic_update_slice(lax.empty((4, PACK_PAD, D), BF16), packs[i][None], (chip, 0, 0))
            gathers[i] = _Stream(packs[i], buf, functools.partial(_gather_parts, half), 6, units, "gather_w")
        return gathers[i]

    def layer_full(i):
        return _operands(gather(i).drain())

    fwd_share = share(dict(in_a=89, in_b=26, in_c=57, in_d=66, mm_wo=28, mm_up=120, mm_down=46), units)

    def fwd_hooks(i):
        if i + 1 >= nl:
            return lambda name: None
        return lambda name: gather(i + 1).hook(fwd_share[name]) if name in fwd_share else None

    exchanges = {}
    bwd_share = share(dict(d_f=91, g_down=67, d_u2_a=42, d_u2_v=45, g_up_a=52, g_up_v=52, d_merged=29, g_wo=19,
                           d_u_wgate=48, g_in_wgate=41), units)

    def after_bwd(i, gw):
        g = _pack_operands(gw, BF16)
        recv = _rs_pair_exchange(g, "rs_pair")
        hsum = _rs_add_pair(g, recv, cidx, "rs_add_pair")
        exchanges[i] = (hsum, _Stream(hsum, lax.empty((3, half, D), BF16), _rs_chip_parts, 3, units, "rs_chips"))

    def bwd_hooks(i):
        if i + 1 >= nl:
            return lambda name: None
        return lambda name: exchanges[i + 1][1].hook(bwd_share[name]) if name in bwd_share else None

    loss, dx, gws, gss, dfinal, drel = _local_step(x[0], loss_target[0], rel_bias, final_g, layer_full, small,
                                                   fwd_hooks, bwd_hooks, after_bwd)

    grads = {}
    red = []
    for i in range(nl):
        hsum, stream = exchanges[i]
        r = _rs_add_chips(hsum, stream.drain(), chip_idx, "rs_add_chips")
        other = _rs_swap(r, "rs_swap")
        both = jnp.concatenate([jnp.where(pc_ == 0, r, other), jnp.where(pc_ == 0, other, r)], axis=0)
        red.append(_unpack_blocks(both))
    for n in BIG_NAMES:
        grads[n] = jnp.stack([red[i][n] for i in range(nl)], axis=0)

    sg = {}
    for n in SMALL_LAYER:
        sg[n] = jnp.stack([gss[i][n] for i in range(nl)], axis=0)
    for n in ("ssd_dt_bias", "ssd_a_log", "ssd_d"):
        sg[n] = sg[n][:, 0, :SSD_HEADS]
    sg["rel_bias"] = drel[:, :REL_BUCKETS].T
    sg["final_g"] = dfinal.reshape(D)
    sg["loss"] = loss[0, :1]
    names_sg = tuple(sg)
    shapes_sg = {n: ((nl,) + W[n].shape[1:] if n in SMALL_LAYER and n not in SHARDED_SMALL else
                     (placed[n].shape if n in SHARDED_SMALL else sg[n].shape)) for n in names_sg}
    for n in names_sg:
        sg[n] = sg[n].reshape(shapes_sg[n])
    tot = _all_reduce_small(_to_rows(_flatten(sg, names_sg)), "allreduce_small")
    tot = _unflatten(tot.reshape(-1), shapes_sg, names_sg)
    loss_out = tot.pop("loss").reshape(())
    for n, cs in SHARDED_SMALL.items():
        tot[n] = lax.dynamic_slice(tot[n], (0, 0, chip * cs), tot[n].shape[:-1] + (cs,))
    grads.update(tot)

    delta, new_m, new_v = {}, {}, {}
    for n in BIG_NAMES:
        shp = W[n].shape
        r2 = lambda a: a.reshape(-1, shp[-1])
        dl, m2, v2 = _adamw(r2(W[n]), r2(grads[n]), r2(M[n]), r2(V[n]), "adamw_" + n)
        delta[n], new_m[n], new_v[n] = dl.reshape(shp), m2.reshape(shp), v2.reshape(shp)
    names_s = tuple(n for n in WEIGHTS if n not in BIG_NAMES)
    shapes_s = {n: W[n].shape for n in names_s}
    pk = lambda t: _to_rows(_flatten(t, names_s))
    dl, m2, v2 = _adamw(pk(W), pk(grads), pk(M), pk(V), "adamw_small")
    delta.update(_unflatten(dl.reshape(-1), shapes_s, names_s))
    new_m.update(_unflatten(m2.reshape(-1), shapes_s, names_s))
    new_v.update(_unflatten(v2.reshape(-1), shapes_s, names_s))

    return (loss_out, dx[None], *[grads[n] for n in WEIGHTS], *[delta[n] for n in WEIGHTS],
            *[new_m[n] for n in WEIGHTS], *[new_v[n] for n in WEIGHTS])
```

```python
import functools
import math

import jax
import jax.numpy as jnp
from jax import lax
from jax.experimental import pallas as pl
from jax.experimental.pallas import tpu as pltpu

F32 = jnp.float32
BF16 = jnp.bfloat16
MESH = pl.DeviceIdType.MESH

D = 1024
HD = 64
GW = 384
AW = 3 * GW
WIN = 128
DILATIONS = (1, 4, 16)
REL_BUCKETS = 32
REL_MAX_DISTANCE = 2048
POOL_WINDOWS = (2, 4, 8, 16)
PG = 256
SSD_HEADS = 16
SSD_N = 128
SSD_CHUNK = 128
XBC = 1536
D_FF = 2816
EPS = 1e-6
NEG = -1e30
HALO = 16
LANES = 128

SEC_A = 3 * AW
SEC_B = D
SEC_C = D + XBC
SEC_D = 3200
IN_WIDTH = SEC_A + SEC_B + SEC_C + 16 + 3 * D

ADAM_LR = 0.001
ADAM_B1 = 0.9
ADAM_B2 = 0.999
ADAM_EPS = 1e-08
ADAM_WD = 0.01
ADAM_STEP = 10
ADAM_TILE = 256 * 1024
MM_VMEM_BYTES = 40 * 1024 * 1024
MM_MAX_OUT_TILE = 1024 * 1024
HBM_BYTES_PER_US = 2.0e6
STEP_US = 0.35


_ANY = pl.BlockSpec(memory_space=pl.ANY)


def _pick(d, cands):
    for t in cands:
        if d % t == 0:
            return t
    return d


def _iota(shape, dim):
    return lax.broadcasted_iota(jnp.int32, shape, dim)


def _dg(a, b, ca, cb):
    return lax.dot_general(a.astype(BF16), b.astype(BF16), (((ca,), (cb,)), ((), ())),
                           preferred_element_type=F32)


@jax.custom_vjp
def _bdot_nn(a, b):
    return _dg(a, b, 1, 0)


def _nn_fwd(a, b):
    return _dg(a, b, 1, 0), (a, b)


def _nn_bwd(res, g):
    a, b = res
    return _dg(g, b, 1, 1), _dg(a, g, 0, 0)


_bdot_nn.defvjp(_nn_fwd, _nn_bwd)


@jax.custom_vjp
def _bdot_nt(a, b):
    return _dg(a, b, 1, 1)


def _nt_fwd(a, b):
    return _dg(a, b, 1, 1), (a, b)


def _nt_bwd(res, g):
    a, b = res
    return _dg(g, b, 1, 0), _dg(g, a, 0, 0)


_bdot_nt.defvjp(_nt_fwd, _nt_bwd)


@jax.custom_vjp
def _bdot_tn(a, b):
    return _dg(a, b, 0, 0)


def _tn_fwd(a, b):
    return _dg(a, b, 0, 0), (a, b)


def _tn_bwd(res, g):
    a, b = res
    return _dg(b, g, 1, 1), _dg(a, g, 1, 0)


_bdot_tn.defvjp(_tn_fwd, _tn_bwd)


def _fdot(a, b):
    return jnp.dot(a, b, preferred_element_type=F32, precision=lax.Precision.HIGHEST)


def _sigmoid(x):
    return 0.5 * jnp.tanh(0.5 * x) + 0.5


def _silu(x):
    return x * _sigmoid(x)


def _softplus(x):
    return jnp.maximum(x, 0.0) + jnp.log(1.0 + jnp.exp(-jnp.abs(x)))


def _lane_pick(m, h):
    return jnp.sum(jnp.where(_iota(m.shape, 1) == h, m, 0.0), axis=1, keepdims=True)


def _row_pick(m, h):
    return jnp.sum(jnp.where(_iota(m.shape, 0) == h, m, 0.0), axis=0, keepdims=True)


def _stack_rows(rows, n):
    c = rows[0].shape[1]
    r = _iota((n, c), 0)
    out = jnp.zeros((n, c), F32)
    for k, v in enumerate(rows):
        out = out + jnp.where(r == k, v, 0.0)
    return out


def _mm(a, b, *, ta=False, tb=False, add=None, out_dtype=F32, name, hook=None):
    if ta:
        K, M = a.shape
    else:
        M, K = a.shape
    if tb:
        N, Kb = b.shape
    else:
        Kb, N = b.shape
    assert K == Kb, (a.shape, b.shape, ta, tb)
    tm, tn, tk = _mm_tiles(M, N, K, a.dtype.itemsize, b.dtype.itemsize, jnp.dtype(out_dtype).itemsize,
                           0 if add is None else add.dtype.itemsize)
    ni, nj, nk = M // tm, N // tn, K // tk
    ca = 0 if ta else 1
    cb = 1 if tb else 0
    n_in = 2 if add is None else 3
    n_hin = 0 if hook is None else len(hook.inputs)
    n_hout = 0 if hook is None else len(hook.out_shapes)

    def body(*refs):
        a_ref, b_ref = refs[:2]
        add_ref = None if add is None else refs[2]
        o_ref = refs[n_in + n_hin]
        scr = refs[n_in + n_hin + 1 + n_hout:]
        acc_ref = scr[0] if nk > 1 else None
        hargs = (refs[n_in:n_in + n_hin], refs[n_in + n_hin + 1:n_in + n_hin + 1 + n_hout], scr[1 if nk > 1 else 0:])
        i, j, k = pl.program_id(0), pl.program_id(1), pl.program_id(2)
        if hook is not None:
            @pl.when((i == 0) & (j == 0) & (k == 0))
            def _():
                hook.start(*hargs)

        part = _dg(a_ref[...], b_ref[...], ca, cb)

        def finish(r):
            if add_ref is not None:
                r = r + add_ref[...].astype(F32)
            o_ref[...] = r.astype(o_ref.dtype)

        if nk == 1:
            finish(part)
        else:
            @pl.when(k == 0)
            def _():
                acc_ref[...] = part

            @pl.when((k > 0) & (k < nk - 1))
            def _():
                acc_ref[...] += part

            @pl.when(k == nk - 1)
            def _():
                finish(acc_ref[...] + part)

        if hook is not None:
            @pl.when((i == ni - 1) & (j == nj - 1) & (k == nk - 1))
            def _():
                hook.finish(*hargs)

    a_spec = pl.BlockSpec((tk, tm), lambda i, j, k: (k, i)) if ta else pl.BlockSpec((tm, tk), lambda i, j, k: (i, k))
    b_spec = pl.BlockSpec((tn, tk), lambda i, j, k: (j, k)) if tb else pl.BlockSpec((tk, tn), lambda i, j, k: (k, j))
    in_specs = [a_spec, b_spec]
    args = [a, b]
    if add is not None:
        in_specs.append(pl.BlockSpec((tm, tn), lambda i, j, k: (i, j)))
        args.append(add)
    out_specs = [pl.BlockSpec((tm, tn), lambda i, j, k: (i, j))]
    out_shape = [jax.ShapeDtypeStruct((M, N), out_dtype)]
    scratch = [pltpu.VMEM((tm, tn), F32)] if nk > 1 else []
    aliases = {}
    if hook is not None:
        in_specs += [_ANY] * n_hin
        args += list(hook.inputs)
        out_specs += [_ANY] * n_hout
        out_shape += list(hook.out_shapes)
        scratch += list(hook.scratch)
        aliases = {n_in + hi: 1 + ho for hi, ho in hook.aliases.items()}
    sem = ("parallel", "parallel", "arbitrary") if hook is None else ("arbitrary",) * 3
    res = pl.pallas_call(
        body, name=name, grid=(ni, nj, nk), in_specs=in_specs, out_specs=out_specs, out_shape=out_shape,
        scratch_shapes=scratch, input_output_aliases=aliases,
        compiler_params=pltpu.CompilerParams(dimension_semantics=sem),
    )(*args)
    if hook is not None:
        hook.done(res[1:])
    return res[0]


def _mm_tiles(M, N, K, sa, sb, so, sadd):
    best = None
    for tk in (K, 2048, 1024, 512, 640, 384, 256, 128):
        if K % tk:
            continue
        for tm in (2048, 1024, 512, 640, 384, 256, 128, M):
            if M % tm or tm > 2048:
                continue
            for tn in (1024, 512, 640, 384, 256, 128, N):
                if N % tn:
                    continue
                vmem = 2 * (tm * tk * sa + tk * tn * sb + tm * tn * (so + sadd)) + (tm * tn * 4 if tk < K else 0)
                if vmem > MM_VMEM_BYTES or tm * tn > MM_MAX_OUT_TILE:
                    continue
                a_reads = 1 if tk == K else N // tn
                traffic = M * K * sa * a_reads + K * N * sb * (M // tm) + M * N * (so + sadd)
                steps = (M // tm) * (N // tn) * (K // tk)
                cost = traffic / HBM_BYTES_PER_US + steps * STEP_US
                if best is None or cost < best[0]:
                    best = (cost, tm, tn, tk)
    assert best is not None, (M, N, K)
    return best[1:]


class _Hook:
    def __init__(self, inputs, out_shapes, aliases, scratch, start, finish, done):
        self.inputs, self.out_shapes, self.aliases, self.scratch = inputs, out_shapes, aliases, scratch
        self.start, self.finish, self.done = start, finish, done


def _rows(name, fn, ins, outs, accs=(), *, tm, nrows, ncol=1):
    nt = nrows // tm
    hb = tm // HALO
    nh = nrows // HALO
    in_specs, args = [], []
    for kind, arr, cw, base in ins:
        if kind == "row":
            cw = arr.shape[1] if cw is None else cw
            in_specs.append(pl.BlockSpec((tm, cw), lambda j, i, base=base: (i, base + j)))
        elif kind == "prev":
            in_specs.append(pl.BlockSpec((HALO, cw), lambda j, i, base=base: (jnp.maximum(i * hb - 1, 0), base + j)))
        elif kind == "next":
            in_specs.append(pl.BlockSpec((HALO, cw), lambda j, i, base=base: (jnp.minimum((i + 1) * hb, nh - 1), base + j)))
        elif kind == "const":
            in_specs.append(pl.BlockSpec(arr.shape, lambda j, i, nd=arr.ndim: (0,) * nd))
        elif kind == "ccol":
            in_specs.append(pl.BlockSpec((arr.shape[0], cw), lambda j, i, base=base: (0, base + j)))
        else:
            raise ValueError(kind)
        args.append(arr)
    out_specs, out_shape = [], []
    for ctot, cw, base, dt in outs:
        out_specs.append(pl.BlockSpec((tm, cw), lambda j, i, base=base: (i, base + j)))
        out_shape.append(jax.ShapeDtypeStruct((nrows, ctot), dt))
    for r, ctot, cw in accs:
        out_specs.append(pl.BlockSpec((r, cw), lambda j, i: (0, j)))
        out_shape.append(jax.ShapeDtypeStruct((r, ctot), F32))
    n_in, n_out = len(ins), len(outs)

    def body(*refs):
        j = pl.program_id(0)
        i = pl.program_id(1)
        res = fn(i, j, *[r[...] for r in refs[:n_in]])
        for r, v in zip(refs[n_in:n_in + n_out], res[:n_out]):
            r[...] = v.astype(r.dtype)
        for r, v in zip(refs[n_in + n_out:], res[n_out:]):
            @pl.when(i == 0)
            def _(r=r, v=v):
                r[...] = v

            @pl.when(i > 0)
            def _(r=r, v=v):
                r[...] += v

    res = pl.pallas_call(
        body, name=name, grid=(ncol, nt), in_specs=in_specs, out_specs=out_specs, out_shape=out_shape,
        compiler_params=pltpu.CompilerParams(dimension_semantics=("arbitrary", "arbitrary")),
    )(*args)
    return res


def _shift_down(xcat, k):
    return xcat if k == 0 else pltpu.roll(xcat, k, 0)


def _shift_up(xcat, k):
    return xcat if k == 0 else pltpu.roll(xcat, xcat.shape[0] - k, 0)


def _with_prev(i, halo, x):
    return jnp.concatenate([jnp.where(i == 0, 0.0, halo), x], axis=0)


def _with_next(i, nt, x, halo):
    return jnp.concatenate([x, jnp.where(i == nt - 1, 0.0, halo)], axis=0)


def _rms_core(x, g):
    r = lax.rsqrt(jnp.mean(x * x, axis=-1, keepdims=True) + EPS)
    return x * r * g


def _rms_fwd(x, g, name):
    S = x.shape[0]
    return _rows(name, lambda i, j, xv, gv: [_rms_core(xv, gv)],
                 [("row", x, None, 0), ("const", g, None, 0)], [(D, D, 0, BF16)], tm=256, nrows=S)[0]


def _rms_bwd(x, g, du, dres, name):
    S = x.shape[0]

    def fn(i, j, xv, gv, duv, drv):
        _, vjp = jax.vjp(_rms_core, xv, gv)
        dx, dg = vjp(duv)
        return [drv + dx, dg]

    return _rows(name, fn, [("row", x, None, 0), ("const", g, None, 0), ("row", du, None, 0), ("row", dres, None, 0)],
                 [(D, D, 0, F32)], [(1, D, D)], tm=256, nrows=S)


def _final_loss(x, target, g):
    S = x.shape[0]

    def fn(i, j, xv, tv, gv):
        def f(xx, gg):
            err = _rms_core(xx, gg) - tv
            return 0.5 * jnp.sum(err * err) / D

        loss, vjp = jax.vjp(f, xv, gv)
        dx, dg = vjp(jnp.ones((), F32))
        return [dx, dg, jnp.zeros((1, LANES), F32) + loss]

    return _rows("final_loss", fn, [("row", x, None, 0), ("row", target, None, 0), ("const", g, None, 0)],
                 [(D, D, 0, F32)], [(1, D, D), (1, LANES, LANES)], tm=256, nrows=S)


def _attn_valid(n):
    qi = _iota((WIN, 2 * WIN), 0)
    kk = _iota((WIN, 2 * WIN), 1)
    rel = qi + WIN - kk
    return (rel >= 0) & (rel <= WIN) & ((kk >= WIN) | (n > 0))


def _attn_block(q, kp, kc, vp, vc, b0, b1, valid):
    k = jnp.concatenate([kp, kc], axis=0)
    v = jnp.concatenate([vp, vc], axis=0)
    lo = _iota((WIN, LANES), 1) < HD
    scale = 1.0 / math.sqrt(HD)
    os_, ls_ = [], []
    for hh, b in ((0, b0), (1, b1)):
        qm = jnp.where(lo if hh == 0 else ~lo, q, 0.0)
        s = _bdot_nt(qm, k) * scale + b
        s = jnp.where(valid, s, NEG)
        m = lax.stop_gradient(jnp.max(s, axis=1, keepdims=True))
        p = jnp.exp(s - m)
        l = jnp.sum(p, axis=1, keepdims=True)
        os_.append(_bdot_nn(p, v) / l)
        ls_.append(m + jnp.log(l))
    return jnp.where(lo, os_[0], os_[1]), jnp.where(lo, ls_[0], ls_[1])


def _residue_rows(r, d):
    return pl.ds(0, WIN) if d == 1 else pl.ds(r, WIN, stride=d)


def _for_residues(d, fn):
    if d == 1:
        fn(0, 0)
    else:
        lax.fori_loop(0, d, fn, 0, unroll=2)


def _pairs_per_step(d):
    return 3 if d == 1 else 1


def _bias_table(rel_bias, bucket, gi, name):
    def body(t_ref, b_ref, o_ref):
        h = 6 * gi + pl.program_id(0)
        b = b_ref[...]
        acc = jnp.zeros(b.shape, F32)
        for k in range(REL_BUCKETS):
            acc = jnp.where(b == k, t_ref[k, h], acc)
        o_ref[0] = acc

    return pl.pallas_call(
        body, name=name, grid=(6,),
        in_specs=[pl.BlockSpec(memory_space=pltpu.SMEM), pl.BlockSpec((WIN, 2 * WIN), lambda h: (0, 0))],
        out_specs=pl.BlockSpec((1, WIN, 2 * WIN), lambda h: (h, 0, 0)),
        out_shape=jax.ShapeDtypeStruct((6, WIN, 2 * WIN), F32),
    )(rel_bias, bucket)


def _attn_fwd(pa, bias, gi, name):
    S = pa.shape[0]
    d = DILATIONS[gi]
    bt = WIN * d
    nb = S // bt
    hpw = _pairs_per_step(d)
    bw = hpw * LANES
    cb = 3 * gi // hpw

    def body(q_ref, kp_ref, kc_ref, vp_ref, vc_ref, b_ref, o_ref, l_ref):
        valid = _attn_valid(pl.program_id(1))

        def residue(r, carry):
            sl = _residue_rows(r, d)
            for t in range(hpw):
                ln = pl.ds(t * LANES, LANES)
                o, lse = _attn_block(q_ref[sl, ln], kp_ref[sl, ln], kc_ref[sl, ln], vp_ref[sl, ln], vc_ref[sl, ln],
                                     b_ref[2 * t], b_ref[2 * t + 1], valid)
                o_ref[sl, ln] = o
                l_ref[sl, ln] = lse
            return carry

        _for_residues(d, residue)

    def spec(off, prev):
        if prev:
            return pl.BlockSpec((bt, bw), lambda hp, n: (jnp.maximum(n - 1, 0), off // hpw + cb + hp))
        return pl.BlockSpec((bt, bw), lambda hp, n: (n, off // hpw + cb + hp))

    ospec = pl.BlockSpec((bt, bw), lambda hp, n: (n, hp))
    return pl.pallas_call(
        body, name=name, grid=(3 // hpw, nb),
        in_specs=[spec(0, False), spec(9, True), spec(9, False), spec(18, True), spec(18, False),
                  pl.BlockSpec((2 * hpw, WIN, 2 * WIN), lambda hp, n: (hp, 0, 0))],
        out_specs=[ospec, ospec],
        out_shape=[jax.ShapeDtypeStruct((S, GW), F32)] * 2,
        compiler_params=pltpu.CompilerParams(dimension_semantics=("parallel", "arbitrary")),
    )(pa, pa, pa, pa, pa, bias)


def _attn_bwd(pa, bias, do, dlse, db_in, dqkv, gi, name):
    S = pa.shape[0]
    d = DILATIONS[gi]
    bt = WIN * d
    nb = S // bt
    hpw = _pairs_per_step(d)
    bw = hpw * LANES
    cb = 3 * gi // hpw

    def body(q_ref, kp_ref, kc_ref, vp_ref, vc_ref, b_ref, do_ref, dl_ref, dbi_ref, dqi_ref, dki_ref, dvi_ref,
             dq_ref, dk_ref, dv_ref, db_ref, ck, cv):
        n = pl.program_id(1)

        @pl.when(n == 0)
        def _():
            db_ref[...] = dbi_ref[...]
            ck[...] = jnp.zeros_like(ck)
            cv[...] = jnp.zeros_like(cv)

        @pl.when(n < nb)
        def _():
            f = functools.partial(_attn_block, valid=_attn_valid(n))

            def residue(r, carry):
                sl = _residue_rows(r, d)
                cs = pl.ds(pl.multiple_of(r * WIN, WIN), WIN)
                for t in range(hpw):
                    ln = pl.ds(t * LANES, LANES)
                    _, vjp = jax.vjp(f, q_ref[sl, ln], kp_ref[sl, ln], kc_ref[sl, ln], vp_ref[sl, ln], vc_ref[sl, ln],
                                     b_ref[2 * t], b_ref[2 * t + 1])
                    dq, dkp, dkc, dvp, dvc, db0, db1 = vjp((do_ref[sl, ln], dl_ref[sl, ln]))
                    dq_ref[sl, ln] = dq
                    dk_ref[sl, ln] = ck[cs, ln] + dkp
                    dv_ref[sl, ln] = cv[cs, ln] + dvp
                    ck[cs, ln] = dkc
                    cv[cs, ln] = dvc
                    db_ref[2 * t] += db0
                    db_ref[2 * t + 1] += db1
                return carry

            _for_residues(d, residue)

        @pl.when(n == nb)
        def _():
            def residue(r, carry):
                sl = _residue_rows(r, d)
                cs = pl.ds(pl.multiple_of(r * WIN, WIN), WIN)
                dk_ref[sl, :] = ck[cs, :]
                dv_ref[sl, :] = cv[cs, :]
                return carry

            _for_residues(d, residue)

    def cur(n):
        return jnp.minimum(n, nb - 1)

    def spec(off, prev):
        if prev:
            return pl.BlockSpec((bt, bw), lambda hp, n: (jnp.maximum(cur(n) - 1, 0), off // hpw + cb + hp))
        return pl.BlockSpec((bt, bw), lambda hp, n: (cur(n), off // hpw + cb + hp))

    gspec = pl.BlockSpec((bt, bw), lambda hp, n: (cur(n), hp))
    bspec = pl.BlockSpec((2 * hpw, WIN, 2 * WIN), lambda hp, n: (hp, 0, 0))
    qspec = pl.BlockSpec((bt, bw), lambda hp, n: (cur(n), cb + hp))
    kspec = pl.BlockSpec((bt, bw), lambda hp, n: (jnp.maximum(n - 1, 0), cb + hp))
    dq, dk, dv, db = pl.pallas_call(
        body, name=name, grid=(3 // hpw, nb + 1),
        in_specs=[spec(0, False), spec(9, True), spec(9, False), spec(18, True), spec(18, False),
                  bspec, gspec, gspec, bspec, _ANY, _ANY, _ANY],
        out_specs=[qspec, kspec, kspec, bspec],
        out_shape=[jax.ShapeDtypeStruct((S, AW), F32)] * 3 + [jax.ShapeDtypeStruct((6, WIN, 2 * WIN), F32)],
        scratch_shapes=[pltpu.VMEM((bt, bw), F32), pltpu.VMEM((bt, bw), F32)],
        input_output_aliases={9: 0, 10: 1, 11: 2},
        compiler_params=pltpu.CompilerParams(dimension_semantics=("arbitrary", "arbitrary")),
    )(pa, pa, pa, pa, pa, bias, do, dlse, db_in, *dqkv)
    return (dq, dk, dv), db


def _mix_core(o0, o1, o2, l0, l1, l2):
    m = lax.stop_gradient(jnp.maximum(jnp.maximum(l0, l1), l2))
    e0, e1, e2 = jnp.exp(l0 - m), jnp.exp(l1 - m), jnp.exp(l2 - m)
    return (e0 * o0 + e1 * o1 + e2 * o2) / (e0 + e1 + e2)


def _mix_fwd(os_, ls_, name):
    S = os_[0].shape[0]
    ins = [("row", a, None, 0) for a in (*os_, *ls_)]
    return _rows(name, lambda i, j, *v: [_mix_core(*v)], ins, [(GW, GW, 0, BF16)], tm=256, nrows=S)[0]


def _mix_bwd(os_, ls_, datt, name):
    S = datt.shape[0]

    def fn(i, j, *v):
        _, vjp = jax.vjp(_mix_core, *v[:6])
        return list(vjp(v[6]))

    ins = [("row", a, None, 0) for a in (*os_, *ls_, datt)]
    outs = [(GW, GW, 0, F32)] * 6
    r = _rows(name, fn, ins, outs, tm=256, nrows=S)
    return r[:3], r[3:]


def _t5_bucket(dist):
    max_exact = REL_BUCKETS // 2
    is_small = dist < max_exact
    nf = jnp.maximum(dist, 1).astype(F32)
    large = max_exact + (jnp.log(nf / max_exact) / math.log(REL_MAX_DISTANCE / max_exact)
                         * (REL_BUCKETS - max_exact)).astype(jnp.int32)
    large = jnp.minimum(large, REL_BUCKETS - 1)
    return jnp.where(is_small, dist, large)


def _buckets(d):
    qi = jnp.arange(WIN)[:, None]
    kk = jnp.arange(2 * WIN)[None, :]
    rel = qi + WIN - kk
    return _t5_bucket(jnp.clip(rel, 0, None) * d)


def _pool_cnt(i, tm, w):
    pos = i * tm + _iota((tm, PG), 0) + 1
    return jnp.minimum(pos, w).astype(F32)


def _pool_d(i, tm, halo, u):
    ds = []
    for g, w in enumerate(POOL_WINDOWS):
        ug = u[:, g * PG:(g + 1) * PG]
        s = _with_prev(i, halo[:, g * PG:(g + 1) * PG], ug)
        step = 1
        while step < w:
            s = s + _shift_down(s, step)
            step *= 2
        ds.append(s[HALO:] / _pool_cnt(i, tm, w) - ug)
    return ds


def _pool_lin(d0, d1, d2, d3, w0, w1, w2, w3, scale):
    y = jnp.concatenate([_bdot_nn(d0, w0), _bdot_nn(d1, w1), _bdot_nn(d2, w2), _bdot_nn(d3, w3)], axis=1)
    return y * scale


def _pool_fwd(pb, pw, scale, name):
    S = pb.shape[0]
    tm = 256

    def fn(i, j, halo, u, w, sc):
        ds = _pool_d(i, tm, halo, u)
        return [_pool_lin(*ds, *[w[k].astype(F32) for k in range(4)], sc)]

    return _rows(name, fn, [("prev", pb, D, 0), ("row", pb, None, 0), ("const", pw, None, 0), ("const", scale, None, 0)],
                 [(D, D, 0, BF16)], tm=tm, nrows=S)[0]


def _pool_bwd(pb, pw, scale, dpo, name):
    S = pb.shape[0]
    tm = 256
    nt = S // tm

    def fn1(i, j, halo, u, w, sc, dy):
        ds = _pool_d(i, tm, halo, u)
        _, vjp = jax.vjp(_pool_lin, *ds, *[w[k].astype(F32) for k in range(4)], sc)
        g = vjp(dy)
        e = jnp.concatenate([g[k] / _pool_cnt(i, tm, wd) for k, wd in enumerate(POOL_WINDOWS)], axis=1)
        return [e, jnp.concatenate(g[4:8], axis=0), g[8]]

    e, dpw, dsc = _rows(name + "_a", fn1,
                        [("prev", pb, D, 0), ("row", pb, None, 0), ("const", pw, None, 0), ("const", scale, None, 0),
                         ("row", dpo, None, 0)],
                        [(D, D, 0, F32)], [(4 * PG, PG, PG), (1, D, D)], tm=tm, nrows=S)

    def fn2(i, j, ev, halo):
        outs = []
        for g, w in enumerate(POOL_WINDOWS):
            eg = ev[:, g * PG:(g + 1) * PG]
            s = _with_next(i, nt, eg, halo[:, g * PG:(g + 1) * PG])
            step = 1
            while step < w:
                s = s + _shift_up(s, step)
                step *= 2
            outs.append(s[:tm] - eg * _pool_cnt(i, tm, w))
        return [jnp.concatenate(outs, axis=1)]

    du = _rows(name + "_b", fn2, [("row", e, None, 0), ("next", e, D, 0)], [(D, D, 0, BF16)], tm=tm, nrows=S)[0]
    return du, dpw, dsc


def _conv_taps(i, halo, x, K):
    cat = _with_prev(i, halo, x)
    return [_shift_down(cat, K - 1 - k)[HALO:] for k in range(K)]


def _conv_pre(taps, w, b):
    acc = b
    for k, t in enumerate(taps):
        acc = acc + t * _row_pick(w, k)
    return acc


CW = 256
CWS = 512
CONV_TM = 512


def _ext_taps(i, nt, prev, x, nxt, K):
    cat = jnp.concatenate([jnp.where(i == 0, 0.0, prev), x, jnp.where(i == nt - 1, 0.0, nxt)], axis=0)
    return [_shift_down(cat, K - 1 - k)[HALO:] for k in range(K)]


def _conv_t_rows(dp, w, K, tm):
    acc = jnp.zeros((tm, dp.shape[1]), F32)
    for k in range(K):
        acc = acc + _shift_up(dp, K - 1 - k)[:tm] * _row_pick(w, k)
    return acc


def _ssd_conv_fwd(pc, w, b, name):
    S = pc.shape[0]
    base = D // CWS

    def fn(i, j, halo, x, wv, bv):
        return [_silu(_conv_pre(_conv_taps(i, halo, x, 4), wv, bv))]

    return _rows(name, fn, [("prev", pc, CWS, base), ("row", pc, CWS, base), ("ccol", w, CWS, 0), ("ccol", b, CWS, 0)],
                 [(XBC, CWS, 0, F32)], tm=CONV_TM, nrows=S, ncol=XBC // CWS)[0]


def _ssd_conv_bwd(pc, w, b, dy, name):
    S = pc.shape[0]
    base = D // CWS
    tm = CONV_TM
    nt = S // tm

    def fn(i, j, prev, x, nxt, wv, bv, dyv, dyn):
        taps = _ext_taps(i, nt, prev, x, nxt, 4)
        pre = _conv_pre(taps, wv, bv)
        sg = _sigmoid(pre)
        dye = jnp.concatenate([dyv, jnp.where(i == nt - 1, 0.0, dyn)], axis=0)
        dpre = dye * sg * (1.0 + pre * (1.0 - sg))
        dw = _stack_rows([jnp.sum(dpre[:tm] * t[:tm], axis=0, keepdims=True) for t in taps], 4)
        return [_conv_t_rows(dpre, wv, 4, tm), dw, jnp.sum(dpre[:tm], axis=0, keepdims=True)]

    return _rows(name, fn,
                 [("prev", pc, CWS, base), ("row", pc, CWS, base), ("next", pc, CWS, base), ("ccol", w, CWS, 0),
                  ("ccol", b, CWS, 0), ("row", dy, CWS, 0), ("next", dy, CWS, 0)],
                 [(XBC, CWS, 0, BF16)], [(4, XBC, CWS), (1, XBC, CWS)], tm=tm, nrows=S, ncol=XBC // CWS)


NFC = D_FF // CW


def _ffn_act_fwd(h, w, b, name):
    S = h.shape[0]

    def fn(i, j, ha, a, hv, v, wa, wv, ba, bv):
        pa = _conv_pre(_conv_taps(i, ha, a, 3), wa, ba)
        pv = _conv_pre(_conv_taps(i, hv, v, 3), wv, bv)
        return [_silu(pa) * pv]

    return _rows(name, fn,
                 [("prev", h, CW, 0), ("row", h, CW, 0), ("prev", h, CW, NFC), ("row", h, CW, NFC),
                  ("ccol", w, CW, 0), ("ccol", w, CW, NFC), ("ccol", b, CW, 0), ("ccol", b, CW, NFC)],
                 [(D_FF, CW, 0, BF16)], tm=CONV_TM, nrows=S, ncol=NFC)[0]


def _ffn_act_bwd(h, w, b, df, name):
    S = h.shape[0]
    tm = CONV_TM
    nt = S // tm

    def fn(i, j, pa_, a, na, pv_, v, nv, wa, wv, ba, bv, dfv, dfn):
        ta = _ext_taps(i, nt, pa_, a, na, 3)
        tv = _ext_taps(i, nt, pv_, v, nv, 3)
        pa = _conv_pre(ta, wa, ba)
        pv = _conv_pre(tv, wv, bv)
        sg = _sigmoid(pa)
        dfe = jnp.concatenate([dfv.astype(F32), jnp.where(i == nt - 1, 0.0, dfn.astype(F32))], axis=0)
        dpa = dfe * pv * sg * (1.0 + pa * (1.0 - sg))
        dpv = dfe * pa * sg
        res = [_conv_t_rows(dpa, wa, 3, tm), _conv_t_rows(dpv, wv, 3, tm)]
        for dp, taps in ((dpa, ta), (dpv, tv)):
            res.append(_stack_rows([jnp.sum(dp[:tm] * t[:tm], axis=0, keepdims=True) for t in taps], 3))
        for dp in (dpa, dpv):
            res.append(jnp.sum(dp[:tm], axis=0, keepdims=True))
        return res

    ins = []
    for base in (0, NFC):
        ins += [("prev", h, CW, base), ("row", h, CW, base), ("next", h, CW, base)]
    ins += [("ccol", w, CW, 0), ("ccol", w, CW, NFC), ("ccol", b, CW, 0), ("ccol", b, CW, NFC),
            ("row", df, CW, 0), ("next", df, CW, 0)]
    dha, dhv, dwa, dwv, dba, dbv = _rows(
        name, fn, ins, [(D_FF, CW, 0, BF16)] * 2, [(3, D_FF, CW)] * 2 + [(1, D_FF, CW)] * 2, tm=tm, nrows=S, ncol=NFC)
    return dha, dhv, jnp.concatenate([dwa, dwv], axis=1), jnp.concatenate([dba, dbv], axis=1)


NSLAB = D // LANES


def _ssd_chunk(xs, Bs, Cs, dtraw, dtb, alog, prev):
    lsz = SSD_CHUNK
    lane = _iota((lsz, LANES), 1)
    row = _iota((lsz, LANES), 0)
    dt = jnp.where(lane < SSD_HEADS, _softplus(dtraw + dtb), 0.0)
    a = dt * (-jnp.exp(alog))
    tril = row >= lane
    a_cs = _fdot(tril.astype(F32), a)
    a_cst = a_cs.T
    a_last = jnp.sum(a, axis=0, keepdims=True)
    lo = lane < HD
    top = row < HD
    cbs = [_bdot_nt(Cs[g], Bs[g]) for g in range(2)]
    ys, news = [], []
    for s in range(NSLAB):
        g = s // (NSLAB // 2)
        cols, lms, dts, als = [], [], [], []
        for hh in range(2):
            h = 2 * s + hh
            col = _lane_pick(a_cs, h)
            seg = col - _row_pick(a_cst, h)
            lms.append(jnp.exp(jnp.where(tril, seg, NEG)))
            cols.append(col)
            dts.append(_lane_pick(dt, h))
            als.append(_lane_pick(a_last, h))
        col_x = jnp.where(lo, cols[0], cols[1])
        al_x = jnp.where(lo, als[0], als[1])
        xc = xs[s] * jnp.where(lo, dts[0], dts[1])
        yd = jnp.where(lo, _bdot_nn(cbs[g] * lms[0], xc), _bdot_nn(cbs[g] * lms[1], xc))
        yoff = _bdot_nt(Cs[g], prev[s]) * jnp.exp(col_x)
        ys.append(yd + yoff)
        st = _bdot_tn(xc * jnp.exp(al_x - col_x), Bs[g])
        news.append(prev[s] * jnp.exp(jnp.where(top, als[0], als[1])) + st)
    return ys, news


def _ssd_scan_fwd(xbc_c, pd, dtb, alog, name):
    S = xbc_c.shape[0]
    nc = S // SSD_CHUNK

    def body(x_ref, b_ref, c_ref, dt_ref, dtb_ref, al_ref, y_ref, st_ref, state):
        c = pl.program_id(0)

        @pl.when(c == 0)
        def _():
            state[...] = jnp.zeros_like(state)

        xs = [x_ref[:, s * LANES:(s + 1) * LANES] for s in range(NSLAB)]
        Bs = [b_ref[:, g * SSD_N:(g + 1) * SSD_N] for g in range(2)]
        Cs = [c_ref[:, g * SSD_N:(g + 1) * SSD_N] for g in range(2)]
        prev = [state[s * LANES:(s + 1) * LANES, :] for s in range(NSLAB)]
        ys, news = _ssd_chunk(xs, Bs, Cs, dt_ref[...], dtb_ref[...], al_ref[...], prev)
        st_ref[0] = state[...]
        for s in range(NSLAB):
            y_ref[:, s * LANES:(s + 1) * LANES] = ys[s]
            state[s * LANES:(s + 1) * LANES, :] = news[s]

    return pl.pallas_call(
        body, name=name, grid=(nc,),
        in_specs=[pl.BlockSpec((SSD_CHUNK, D), lambda c: (c, 0)),
                  pl.BlockSpec((SSD_CHUNK, 2 * SSD_N), lambda c: (c, D // (2 * SSD_N))),
                  pl.BlockSpec((SSD_CHUNK, 2 * SSD_N), lambda c: (c, D // (2 * SSD_N) + 1)),
                  pl.BlockSpec((SSD_CHUNK, LANES), lambda c: (c, 0)),
                  pl.BlockSpec((1, LANES), lambda c: (0, 0)), pl.BlockSpec((1, LANES), lambda c: (0, 0))],
        out_specs=[pl.BlockSpec((SSD_CHUNK, D), lambda c: (c, 0)), pl.BlockSpec((1, D, SSD_N), lambda c: (c, 0, 0))],
        out_shape=[jax.ShapeDtypeStruct((S, D), F32), jax.ShapeDtypeStruct((nc, D, SSD_N), F32)],
        scratch_shapes=[pltpu.VMEM((D, SSD_N), F32)],
        compiler_params=pltpu.CompilerParams(dimension_semantics=("arbitrary",)),
    )(xbc_c, xbc_c, xbc_c, pd, dtb, alog)


def _ssd_scan_bwd(xbc_c, pd, dtb, alog, states, dy, dxs_skip, name):
    S = xbc_c.shape[0]
    nc = S // SSD_CHUNK

    def body(x_ref, b_ref, c_ref, dt_ref, dtb_ref, al_ref, st_ref, dy_ref, sk_ref,
             dx_ref, ddt_ref, ddtb_ref, dal_ref, dstate):
        c = pl.program_id(0)

        @pl.when(c == 0)
        def _():
            dstate[...] = jnp.zeros_like(dstate)
            ddtb_ref[...] = jnp.zeros_like(ddtb_ref)
            dal_ref[...] = jnp.zeros_like(dal_ref)

        xs = [x_ref[:, s * LANES:(s + 1) * LANES] for s in range(NSLAB)]
        Bs = [b_ref[:, g * SSD_N:(g + 1) * SSD_N] for g in range(2)]
        Cs = [c_ref[:, g * SSD_N:(g + 1) * SSD_N] for g in range(2)]
        prev = [st_ref[0, s * LANES:(s + 1) * LANES, :] for s in range(NSLAB)]
        _, vjp = jax.vjp(_ssd_chunk, xs, Bs, Cs, dt_ref[...], dtb_ref[...], al_ref[...], prev)
        dys = [dy_ref[:, s * LANES:(s + 1) * LANES] for s in range(NSLAB)]
        dnew = [dstate[s * LANES:(s + 1) * LANES, :] for s in range(NSLAB)]
        dxs, dBs, dCs, ddt, ddtb, dal, dprev = vjp((dys, dnew))
        for s in range(NSLAB):
            dx_ref[:, s * LANES:(s + 1) * LANES] = dxs[s] + sk_ref[:, s * LANES:(s + 1) * LANES]
            dstate[s * LANES:(s + 1) * LANES, :] = dprev[s]
        for g in range(2):
            dx_ref[:, D + g * SSD_N:D + (g + 1) * SSD_N] = dBs[g]
            dx_ref[:, D + 2 * SSD_N + g * SSD_N:D + 2 * SSD_N + (g + 1) * SSD_N] = dCs[g]
        ddt_ref[...] = ddt
        ddtb_ref[...] += ddtb
        dal_ref[...] += dal

    def rv(c):
        return nc - 1 - c

    return pl.pallas_call(
        body, name=name, grid=(nc,),
        in_specs=[pl.BlockSpec((SSD_CHUNK, D), lambda c: (rv(c), 0)),
                  pl.BlockSpec((SSD_CHUNK, 2 * SSD_N), lambda c: (rv(c), D // (2 * SSD_N))),
                  pl.BlockSpec((SSD_CHUNK, 2 * SSD_N), lambda c: (rv(c), D // (2 * SSD_N) + 1)),
                  pl.BlockSpec((SSD_CHUNK, LANES), lambda c: (rv(c), 0)),
                  pl.BlockSpec((1, LANES), lambda c: (0, 0)), pl.BlockSpec((1, LANES), lambda c: (0, 0)),
                  pl.BlockSpec((1, D, SSD_N), lambda c: (rv(c), 0, 0)),
                  pl.BlockSpec((SSD_CHUNK, D), lambda c: (rv(c), 0)),
                  pl.BlockSpec((SSD_CHUNK, D), lambda c: (rv(c), 0))],
        out_specs=[pl.BlockSpec((SSD_CHUNK, XBC), lambda c: (rv(c), 0)),
                   pl.BlockSpec((SSD_CHUNK, LANES), lambda c: (rv(c), 0)),
                   pl.BlockSpec((1, LANES), lambda c: (0, 0)), pl.BlockSpec((1, LANES), lambda c: (0, 0))],
        out_shape=[jax.ShapeDtypeStruct((S, XBC), F32), jax.ShapeDtypeStruct((S, LANES), F32),
                   jax.ShapeDtypeStruct((1, LANES), F32), jax.ShapeDtypeStruct((1, LANES), F32)],
        scratch_shapes=[pltpu.VMEM((D, SSD_N), F32)],
        compiler_params=pltpu.CompilerParams(dimension_semantics=("arbitrary",)),
    )(xbc_c, xbc_c, xbc_c, pd, dtb, alog, states, dy, dxs_skip)


def _ssd_post_core(y, xs, z, d128, nw):
    tm = y.shape[0]
    ex = (_iota((LANES, D), 1) // HD == _iota((LANES, D), 0)).astype(F32)
    d_x = jnp.sum(_fdot(jnp.broadcast_to(d128, (8, LANES)), ex), axis=0, keepdims=True) * 0.125
    y2 = (y + d_x * xs) * _silu(z)
    lo = _iota((tm, D), 1) < D // 2
    sq = y2 * y2
    ms0 = jnp.sum(jnp.where(lo, sq, 0.0), axis=-1, keepdims=True) / (D // 2)
    ms1 = jnp.sum(jnp.where(lo, 0.0, sq), axis=-1, keepdims=True) / (D // 2)
    r = jnp.where(lo, lax.rsqrt(ms0 + EPS), lax.rsqrt(ms1 + EPS))
    return y2 * r * nw


def _ssd_post_ins(y, xbc_c, pc, d128, nw):
    return [("row", y, None, 0), ("row", xbc_c, D, 0), ("row", pc, D, 0), ("const", d128, None, 0), ("const", nw, None, 0)]


def _ssd_post_fwd(y, xbc_c, pc, d128, nw, name):
    S = y.shape[0]
    return _rows(name, lambda i, j, *v: [_ssd_post_core(*v)], _ssd_post_ins(y, xbc_c, pc, d128, nw),
                 [(D, D, 0, BF16)], tm=128, nrows=S)[0]


def _ssd_post_bwd(y, xbc_c, pc, d128, nw, dout, name):
    S = y.shape[0]

    def fn(i, j, *v):
        _, vjp = jax.vjp(_ssd_post_core, *v[:5])
        return list(vjp(v[5]))

    return _rows(name, fn, _ssd_post_ins(y, xbc_c, pc, d128, nw) + [("row", dout, None, 0)],
                 [(D, D, 0, F32), (D, D, 0, F32), (D, D, 0, BF16)], [(1, LANES, LANES), (1, D, D)], tm=128, nrows=S)


def _gates_core(g0, g1, g2, b0, b1, b2, ya, yb, yc):
    return _sigmoid(g0 + b0) * ya + _sigmoid(g1 + b1) * yb + _sigmoid(g2 + b2) * yc


def _gate_parts(pdv, bv):
    gp = pltpu.roll(pdv, SEC_D - 16, 1)
    return [gp[:, k * D:(k + 1) * D] for k in range(3)] + [bv[:, k * D:(k + 1) * D] for k in range(3)]


def _gates_fwd(pd, bg, ya, yb, yc, name):
    S = pd.shape[0]

    def fn(i, j, pdv, bv, a, b, c):
        return [_gates_core(*_gate_parts(pdv, bv), a, b, c)]

    return _rows(name, fn, [("row", pd, None, 0), ("const", bg, None, 0), ("row", ya, None, 0), ("row", yb, None, 0),
                            ("row", yc, None, 0)], [(D, D, 0, BF16)], tm=128, nrows=S)[0]


def _gates_bwd(pd, bg, ya, yb, yc, dm, name):
    S = pd.shape[0]
    tm = 128

    def fn(i, j, pdv, bv, a, b, c, dmv):
        _, vjp = jax.vjp(_gates_core, *_gate_parts(pdv, bv), a, b, c)
        g = vjp(dmv)
        return [g[6], g[7], g[8], jnp.concatenate(g[0:3], axis=1), jnp.concatenate(g[3:6], axis=1)]

    return _rows(name, fn, [("row", pd, None, 0), ("const", bg, None, 0), ("row", ya, None, 0), ("row", yb, None, 0),
                            ("row", yc, None, 0), ("row", dm, None, 0)],
                 [(D, D, 0, BF16)] * 3 + [(3 * D, 3 * D, 0, BF16)], [(1, 3 * D, 3 * D)], tm=tm, nrows=S)


def _adamw(w, g, m, v, name):
    rows, C = w.shape
    tm = _pick(rows, [t for t in (512, 256, 128, 64, 32, 16, 8) if t * C <= ADAM_TILE])

    def fn(i, j, wv, gv, mv, vv):
        m2 = ADAM_B1 * mv + (1.0 - ADAM_B1) * gv
        v2 = ADAM_B2 * vv + (1.0 - ADAM_B2) * jnp.square(gv)
        m_hat = m2 / (1.0 - ADAM_B1 ** ADAM_STEP)
        v_hat = v2 / (1.0 - ADAM_B2 ** ADAM_STEP)
        delta = -ADAM_LR * (m_hat / (jnp.sqrt(v_hat) + ADAM_EPS) + ADAM_WD * wv)
        return [delta, m2, v2]

    return _rows(name, fn, [("row", a, None, 0) for a in (w, g, m, v)], [(C, C, 0, F32)] * 3, tm=tm, nrows=rows)


def _position():
    return lax.axis_index("x"), lax.axis_index("y"), lax.axis_index("c")


def _other_chips(x, y):
    return [(1 - x, y), (x, 1 - y), (1 - x, 1 - y)]


_HBM = pl.BlockSpec(memory_space=pltpu.HBM)


def _gather_parts(half, lo, n):
    def copies(p_ref, out_ref, send_sems, recv_sems):
        x, y, c = _position()
        sibling = (x, y, 1 - c)
        chips = _other_chips(x, y)

        def slab(chip, h):
            return out_ref.at[2 * chip[0] + chip[1], pl.ds(h * half + lo, n), :]

        def copy(k, src, dst, to):
            return pltpu.make_async_remote_copy(src_ref=src, dst_ref=dst, send_sem=send_sems.at[k],
                                                recv_sem=recv_sems.at[k], device_id=to, device_id_type=MESH)

        first = [copy(j, p_ref.at[pl.ds(c * half + lo, n), :], slab((x, y), c), (*chip, c)) for j, chip in enumerate(chips)]
        passed = [copy(3 + j, slab(chip, c), slab(chip, c), sibling) for j, chip in enumerate(chips)]
        from_chips = [copy(j, slab(chip, c), slab(chip, c), (x, y, c)) for j, chip in enumerate(chips)]
        from_sibling = [copy(3 + j, slab(chip, 1 - c), slab(chip, 1 - c), (x, y, c)) for j, chip in enumerate(chips)]
        return first, passed, from_chips, from_sibling

    def start(ins, outs, scr):
        for cp in copies(ins[0], outs[0], *scr)[0]:
            cp.start()

    def finish(ins, outs, scr):
        first, passed, from_chips, from_sibling = copies(ins[0], outs[0], *scr)
        for j in range(3):
            from_chips[j].wait_recv()
            passed[j].start()
        for cp in from_sibling:
            cp.wait_recv()
        for cp in first + passed:
            cp.wait_send()

    return start, finish


def _rs_chip_parts(lo, n):
    def copies(h_ref, out_ref, send_sems, recv_sems):
        x, y, c = _position()
        return [pltpu.make_async_remote_copy(src_ref=h_ref.at[2 * chip[0] + chip[1], pl.ds(lo, n), :],
                                             dst_ref=out_ref.at[j, pl.ds(lo, n), :],
                                             send_sem=send_sems.at[j], recv_sem=recv_sems.at[j],
                                             device_id=(*chip, c), device_id_type=MESH)
                for j, chip in enumerate(_other_chips(x, y))]

    def start(ins, outs, scr):
        for cp in copies(ins[0], outs[0], *scr):
            cp.start()

    def finish(ins, outs, scr):
        for cp in copies(ins[0], outs[0], *scr):
            cp.wait()

    return start, finish


class _Stream:
    def __init__(self, src, buf, parts, nsem, units, name):
        self.src, self.buf, self.parts, self.nsem, self.name = src, buf, parts, nsem, name
        self.next, self.units = 0, units

    def _scratch(self):
        return [pltpu.SemaphoreType.DMA((self.nsem,)), pltpu.SemaphoreType.DMA((self.nsem,))]

    def _take(self, units):
        units = min(units, self.units - self.next)
        lo = self.next * 16
        self.next += units
        return lo, units * 16

    def _set(self, outs):
        self.buf = outs[0]

    def hook(self, units):
        lo, n = self._take(units)
        if n == 0:
            return None
        start, finish = self.parts(lo, n)
        return _Hook([self.src, self.buf], [jax.ShapeDtypeStruct(self.buf.shape, self.buf.dtype)], {1: 0},
                     self._scratch(), start, finish, self._set)

    def drain(self):
        lo, n = self._take(self.units)
        if n:
            start, finish = self.parts(lo, n)

            def body(s_ref, b_ref, o_ref, send_sems, recv_sems):
                args = ((s_ref, b_ref), (o_ref,), (send_sems, recv_sems))
                start(*args)
                finish(*args)

            self.buf = pl.pallas_call(
                body, name=self.name, in_specs=[_ANY, _ANY], out_specs=_ANY,
                out_shape=jax.ShapeDtypeStruct(self.buf.shape, self.buf.dtype),
                scratch_shapes=self._scratch(), input_output_aliases={1: 0},
            )(self.src, self.buf)
        return self.buf


def _rs_pair_exchange(g, name):
    _, R, C = g.shape
    Rh = R // 2

    def body(g_ref, out_ref, send_sem, recv_sem):
        x, y, c = _position()
        src = g_ref.at[pl.ds(0, 4), pl.ds((1 - c) * Rh, Rh), :]
        cp = pltpu.make_async_remote_copy(src_ref=src, dst_ref=out_ref, send_sem=send_sem,
                                          recv_sem=recv_sem, device_id=(x, y, 1 - c), device_id_type=MESH)
        cp.start()
        cp.wait()

    return pl.pallas_call(
        body, name=name, in_specs=[_HBM], out_specs=_HBM,
        out_shape=jax.ShapeDtypeStruct((4, Rh, C), g.dtype),
        scratch_shapes=[pltpu.SemaphoreType.DMA, pltpu.SemaphoreType.DMA],
    )(g)


def _rs_swap(r, name):
    Rh, C = r.shape

    def body(r_ref, out_ref, send_sem, recv_sem):
        x, y, c = _position()
        cp = pltpu.make_async_remote_copy(src_ref=r_ref, dst_ref=out_ref, send_sem=send_sem,
                                          recv_sem=recv_sem, device_id=(x, y, 1 - c), device_id_type=MESH)
        cp.start()
        cp.wait()

    return pl.pallas_call(
        body, name=name, in_specs=[_HBM], out_specs=_HBM,
        out_shape=jax.ShapeDtypeStruct((Rh, C), r.dtype),
        scratch_shapes=[pltpu.SemaphoreType.DMA, pltpu.SemaphoreType.DMA],
    )(r)


def _rs_add_pair(g, recv, cidx, name):
    _, R, C = g.shape
    Rh = R // 2
    tm = _pick(Rh, (400, 280, 200, 160, 80, 40, 16, 8))
    nt = Rh // tm

    def body(c_ref, g_ref, r_ref, o_ref):
        o_ref[...] = (g_ref[...].astype(F32) + r_ref[...].astype(F32)).astype(o_ref.dtype)

    return pl.pallas_call(
        body, name=name,
        grid_spec=pltpu.PrefetchScalarGridSpec(
            num_scalar_prefetch=1, grid=(4, nt),
            in_specs=[pl.BlockSpec((1, tm, C), lambda k, i, cr: (k, cr[0] * nt + i, 0)),
                      pl.BlockSpec((1, tm, C), lambda k, i, cr: (k, i, 0))],
            out_specs=pl.BlockSpec((1, tm, C), lambda k, i, cr: (k, i, 0))),
        out_shape=jax.ShapeDtypeStruct((4, Rh, C), BF16),
    )(cidx, g, recv)


def _rs_add_chips(h, recv, chip_idx, name):
    _, Rh, C = h.shape
    tm = _pick(Rh, (400, 280, 200, 160, 80, 40, 16, 8))

    def body(c_ref, h_ref, r_ref, o_ref):
        acc = h_ref[0].astype(F32)
        for j in range(3):
            acc = acc + r_ref[j].astype(F32)
        o_ref[...] = acc

    return pl.pallas_call(
        body, name=name,
        grid_spec=pltpu.PrefetchScalarGridSpec(
            num_scalar_prefetch=1, grid=(Rh // tm,),
            in_specs=[pl.BlockSpec((1, tm, C), lambda i, cr: (cr[0], i, 0)), pl.BlockSpec((3, tm, C), lambda i, cr: (0, i, 0))],
            out_specs=pl.BlockSpec((tm, C), lambda i, cr: (i, 0))),
        out_shape=jax.ShapeDtypeStruct((Rh, C), F32),
    )(chip_idx, h, recv)


def _all_reduce_small(vec, name):
    n, C = vec.shape

    def body(v_ref, out_ref, buf, send_sems, recv_sems):
        x, y, c = _position()

        def flip(k):
            return ((1 - x) if k & 4 else x, (1 - y) if k & 2 else y, (1 - c) if k & 1 else c)

        def idx(p):
            return 4 * p[0] + 2 * p[1] + p[2]

        me = idx((x, y, c))
        buf[me] = v_ref[...]
        cps = [pltpu.make_async_remote_copy(src_ref=v_ref, dst_ref=buf.at[me], send_sem=send_sems.at[k - 1],
                                            recv_sem=recv_sems.at[k - 1], device_id=flip(k), device_id_type=MESH)
               for k in range(1, 8)]
        for cp in cps:
            cp.start()
        for k in range(1, 8):
            pltpu.make_async_remote_copy(src_ref=v_ref, dst_ref=buf.at[idx(flip(k))], send_sem=send_sems.at[k - 1],
                                         recv_sem=recv_sems.at[k - 1], device_id=flip(k), device_id_type=MESH).wait_recv()
        for cp in cps:
            cp.wait_send()
        acc = buf[0]
        for s in range(1, 8):
            acc = acc + buf[s]
        out_ref[...] = acc

    return pl.pallas_call(
        body, name=name,
        in_specs=[pl.BlockSpec(memory_space=pltpu.VMEM)], out_specs=pl.BlockSpec(memory_space=pltpu.VMEM),
        out_shape=jax.ShapeDtypeStruct((n, C), F32),
        scratch_shapes=[pltpu.VMEM((8, n, C), F32), pltpu.SemaphoreType.DMA((7,)), pltpu.SemaphoreType.DMA((7,))],
    )(vec)


BIG = (("w_in", (D, IN_WIDTH // 4), "cols"), ("w_a", (GW, D // 4), "cols"), ("pool_w", (4, PG // 4, PG), "pool"),
       ("w_b", (D // 4, D), "rows"), ("w_c", (D // 4, D), "rows"), ("w_o", (D // 4, D), "rows"),
       ("ffn_w_up", (D, 2 * D_FF // 4), "cols"), ("ffn_w_down", (D_FF // 4, D), "rows"))
def _pack_rows(s):
    k = math.prod(s) // D
    return -(-k // 16) * 16, k


PACK_ROWS = sum(_pack_rows(s)[0] for _, s, _ in BIG)
PACK_PAD = -(-PACK_ROWS // 32) * 32


def _pad_rows(v, rows):
    pad = [(0, 0)] * v.ndim
    pad[-2] = (0, rows - v.shape[-2])
    return jnp.pad(v, pad) if rows > v.shape[-2] else v


def _pack_blocks(blocks, dtype):
    lead = blocks["w_in"].shape[:-2]
    flat = []
    for n, s, how in BIG:
        v = blocks[n].astype(dtype)
        if how == "cols":
            v = jnp.swapaxes(v, -1, -2)
        flat.append(_pad_rows(v.reshape(*lead, -1, D), _pack_rows(s)[0]))
    flat.append(jnp.zeros((*lead, PACK_PAD - PACK_ROWS, D), dtype))
    return jnp.concatenate(flat, axis=-2)


def _unpack_blocks(pack):
    out, r = {}, 0
    for n, s, how in BIG:
        rows, k = _pack_rows(s)
        v = pack[r:r + k, :]
        out[n] = v.reshape(s[1], s[0]).T if how == "cols" else v.reshape(s)
        r += rows
    return out


def _operands(allp):
    out, r = {}, 0
    for n, s, how in BIG:
        rows, k = _pack_rows(s)
        v = allp[:, r:r + k, :]
        if how == "cols":
            out[n] = v.reshape(4 * s[1], s[0])
        elif how == "rows":
            out[n] = v.reshape(4 * s[0], s[1])
        else:
            out[n] = v.reshape(4, *s).transpose(1, 0, 2, 3).reshape(4, PG, PG)
        r += rows
    return out


def _pack_operands(g, dtype):
    flat = []
    for n, s, how in BIG:
        v = g[n].astype(dtype)
        if how == "pool":
            v = v.reshape(4, 4, s[1], s[2]).transpose(1, 0, 2, 3)
        flat.append(_pad_rows(v.reshape(4, -1, D), _pack_rows(s)[0]))
    flat.append(jnp.zeros((4, PACK_PAD - PACK_ROWS, D), dtype))
    return jnp.concatenate(flat, axis=1)


def _layer_fwd(x, w, sm, bias, hk):
    u = _rms_fwd(x, sm["ln1_g"], "rms1")
    pa = _mm(u, w["in_a"], tb=True, name="in_a", hook=hk("in_a"))
    pb = _mm(u, w["in_b"], tb=True, name="in_b", hook=hk("in_b"))
    pc = _mm(u, w["in_c"], tb=True, name="in_c", hook=hk("in_c"))
    pd = _mm(u, w["in_d"], tb=True, name="in_d", hook=hk("in_d"))
    os_, ls_ = [], []
    for gi in range(3):
        o, l = _attn_fwd(pa, bias[gi], gi, "attn_fwd%d" % gi)
        os_.append(o)
        ls_.append(l)
    att = _mix_fwd(os_, ls_, "mix_fwd")
    ya = _mm(att, w["w_a"], tb=True, name="mm_wa")
    pool_o = _pool_fwd(pb, w["pool_w"], sm["pool_scale"], "pool_fwd")
    yb = _mm(pool_o, w["w_b"], name="mm_wb")
    xbc_c = _ssd_conv_fwd(pc, sm["ssd_conv_w"], sm["ssd_conv_b"], "ssd_conv_fwd")
    y_scan, states = _ssd_scan_fwd(xbc_c, pd, sm["ssd_dt_bias"], sm["ssd_a_log"], "ssd_scan_fwd")
    ssd_o = _ssd_post_fwd(y_scan, xbc_c, pc, sm["ssd_d"], sm["ssd_norm_w"], "ssd_post_fwd")
    yc = _mm(ssd_o, w["w_c"], name="mm_wc")
    merged = _gates_fwd(pd, sm["b_gate"], ya, yb, yc, "gates_fwd")
    x1 = _mm(merged, w["w_o"], add=x, name="mm_wo", hook=hk("mm_wo"))
    u2 = _rms_fwd(x1, sm["ln2_g"], "rms2")
    h = _mm(u2, w["ffn_w_up"], tb=True, name="mm_up", hook=hk("mm_up"))
    f = _ffn_act_fwd(h, sm["ffn_conv_w"], sm["ffn_conv_b"], "ffn_act_fwd")
    x2 = _mm(f, w["ffn_w_down"], add=x1, name="mm_down", hook=hk("mm_down"))
    saved = dict(x=x, u=u, pa=pa, pb=pb, pc=pc, pd=pd, os=os_, ls=ls_, att=att, ya=ya, yb=yb, yc=yc, pool_o=pool_o,
                 xbc_c=xbc_c, y_scan=y_scan, states=states, ssd_o=ssd_o, merged=merged, x1=x1, u2=u2, h=h, f=f)
    return x2, saved


def _layer_bwd(dx2, w, sm, bias, dbs, sv, hk):
    gw, gs = {}, {}
    S = dx2.shape[0]

    def gmm(a, b, name):
        return _mm(a, b, ta=True, out_dtype=BF16, name=name, hook=hk(name))

    df = _mm(dx2, w["ffn_w_down"], tb=True, name="d_f", hook=hk("d_f"))
    gw["ffn_w_down"] = gmm(sv["f"], dx2, "g_down")
    dha, dhv, gs["ffn_conv_w"], gs["ffn_conv_b"] = _ffn_act_bwd(sv["h"], sm["ffn_conv_w"], sm["ffn_conv_b"], df, "ffn_act_bwd")
    du2 = _mm(dha, w["up_a"], name="d_u2_a", hook=hk("d_u2_a"))
    du2 = _mm(dhv, w["up_v"], add=du2, name="d_u2_v", hook=hk("d_u2_v"))
    gw["ffn_w_up"] = jnp.concatenate([gmm(dha, sv["u2"], "g_up_a"), gmm(dhv, sv["u2"], "g_up_v")], axis=0)
    dx1, gs["ln2_g"] = _rms_bwd(sv["x1"], sm["ln2_g"], du2, dx2, "rms2_bwd")
    dmerged = _mm(dx1, w["w_o"], tb=True, name="d_merged", hook=hk("d_merged"))
    gw["w_o"] = gmm(sv["merged"], dx1, "g_wo")
    dya, dyb, dyc, dgate, gs["b_gate"] = _gates_bwd(
        sv["pd"], sm["b_gate"], sv["ya"], sv["yb"], sv["yc"], dmerged, "gates_bwd")
    dssd_o = _mm(dyc, w["w_c"], tb=True, name="d_ssd_o")
    gw["w_c"] = gmm(sv["ssd_o"], dyc, "g_wc")
    dy_scan, dxs_skip, dz, gs["ssd_d"], gs["ssd_norm_w"] = _ssd_post_bwd(
        sv["y_scan"], sv["xbc_c"], sv["pc"], sm["ssd_d"], sm["ssd_norm_w"], dssd_o, "ssd_post_bwd")
    dxbc_c, ddt, gs["ssd_dt_bias"], gs["ssd_a_log"] = _ssd_scan_bwd(
        sv["xbc_c"], sv["pd"], sm["ssd_dt_bias"], sm["ssd_a_log"], sv["states"], dy_scan, dxs_skip, "ssd_scan_bwd")
    dxbc, gs["ssd_conv_w"], gs["ssd_conv_b"] = _ssd_conv_bwd(sv["pc"], sm["ssd_conv_w"], sm["ssd_conv_b"], dxbc_c, "ssd_conv_bwd")
    dpool_o = _mm(dyb, w["w_b"], tb=True, name="d_pool_o")
    gw["w_b"] = gmm(sv["pool_o"], dyb, "g_wb")
    dpb, dpw, gs["pool_scale"] = _pool_bwd(sv["pb"], w["pool_w"], sm["pool_scale"], dpool_o, "pool_bwd")
    gw["pool_w"] = dpw.reshape(4, PG, PG)
    datt = _mm(dya, w["w_a"], name="d_att")
    gw["w_a"] = gmm(dya, sv["att"], "g_wa")
    dos, dls = _mix_bwd(sv["os"], sv["ls"], datt, "mix_bwd")
    dqkv = tuple(lax.empty((S, AW), F32) for _ in range(3))
    dbs = list(dbs)
    for gi in range(3):
        dqkv, dbs[gi] = _attn_bwd(sv["pa"], bias[gi], dos[gi], dls[gi], dbs[gi], dqkv, gi, "attn_bwd%d" % gi)
    u = sv["u"]
    pieces = [(dqkv[0], "wq"), (dqkv[1], "wk"), (dqkv[2], "wv"), (dpb, "in_b"), (dz, "wz"), (dxbc, "wxbc"),
              (ddt, "wdt"), (dgate, "wgate")]
    du = None
    g_in = []
    for dp, key in pieces:
        du = _mm(dp, w[key], add=du, name="d_u_" + key, hook=hk("d_u_" + key))
        g = gmm(dp, u, "g_in_" + key)
        g_in.append(g[:SSD_HEADS] if key == "wdt" else g)
    gw["w_in"] = jnp.concatenate(g_in, axis=0)
    dx, gs["ln1_g"] = _rms_bwd(sv["x"], sm["ln1_g"], du, dx1, "rms1_bwd")
    return dx, gw, gs, dbs


SMALL_LAYER = ("ln1_g", "b_gate", "pool_scale", "ssd_conv_w", "ssd_conv_b", "ssd_dt_bias", "ssd_a_log", "ssd_d",
               "ssd_norm_w", "ln2_g", "ffn_conv_w", "ffn_conv_b")


def _pad_lanes(v):
    return jnp.pad(v, (0, LANES - v.shape[0])).reshape(1, LANES)


def _layer_weights(ops):
    wt = ops["w_in"]
    o1, o2, o3 = SEC_A, SEC_A + SEC_B, SEC_A + SEC_B + SEC_C
    w = dict(ops)
    w["in_a"] = wt[:o1]
    w["in_b"] = wt[o1:o2]
    w["in_c"] = wt[o2:o3]
    w["in_d"] = jnp.pad(wt[o3:], ((0, SEC_D - (IN_WIDTH - o3)), (0, 0)))
    w["wq"], w["wk"], w["wv"] = wt[:AW], wt[AW:2 * AW], wt[2 * AW:o1]
    w["wz"], w["wxbc"] = wt[o2:o2 + D], wt[o2 + D:o3]
    w["wdt"] = jnp.pad(wt[o3:o3 + SSD_HEADS], ((0, LANES - SSD_HEADS), (0, 0)))
    w["wgate"] = wt[o3 + SSD_HEADS:]
    w["up_a"], w["up_v"] = ops["ffn_w_up"][:D_FF], ops["ffn_w_up"][D_FF:]
    return w


def _layer_small(p, i):
    sm = {n: p[n][i] for n in SMALL_LAYER}
    out = {}
    for n, v in sm.items():
        if n in ("ssd_dt_bias", "ssd_a_log", "ssd_d"):
            out[n] = _pad_lanes(v)
        elif v.ndim == 1:
            out[n] = v.reshape(1, -1)
        else:
            out[n] = v
    return out


def _local_step(x, target, rel_bias, final_g, layer_full, small, fwd_hooks=None, bwd_hooks=None, after_bwd=None):
    nl = small["ln1_g"].shape[0]
    buckets = [_buckets(d).astype(jnp.int32) for d in DILATIONS]
    bias = [_bias_table(rel_bias, buckets[gi], gi, "bias_table%d" % gi) for gi in range(3)]
    no_hooks = lambda i: (lambda name: None)
    fwd_hooks = fwd_hooks or no_hooks
    bwd_hooks = bwd_hooks or no_hooks
    saved, ws, sms = [], [], []
    h = x
    for i in range(nl):
        w = _layer_weights(layer_full(i))
        sm = _layer_small(small, i)
        h, sv = _layer_fwd(h, w, sm, bias, fwd_hooks(i))
        saved.append(sv)
        ws.append(w)
        sms.append(sm)
    dh, dfinal, loss = _final_loss(h, target, final_g.reshape(1, D))
    gws, gss = [None] * nl, [None] * nl
    dbs = [jnp.zeros((6, WIN, 2 * WIN), F32)] * 3
    for i in reversed(range(nl)):
        dh, gws[i], gss[i], dbs = _layer_bwd(dh, ws[i], sms[i], bias, dbs, saved[i], bwd_hooks(i))
        if after_bwd is not None:
            after_bwd(i, gws[i])
    drel = []
    for gi in range(3):
        onehot = jnp.pad(jax.nn.one_hot(buckets[gi].reshape(-1), REL_BUCKETS, dtype=BF16), ((0, 0), (0, LANES - REL_BUCKETS)))
        drel.append(_mm(dbs[gi].reshape(6, WIN * 2 * WIN), onehot, name="g_relb"))
    return loss, dh, gws, gss, dfinal, jnp.concatenate(drel, axis=0)


WEIGHTS = ("rel_bias", "ln1_g", "w_in", "b_gate", "w_a", "pool_w", "pool_scale", "w_b", "ssd_conv_w", "ssd_conv_b",
           "ssd_dt_bias", "ssd_a_log", "ssd_d", "ssd_norm_w", "w_c", "w_o", "ln2_g", "ffn_w_up", "ffn_conv_w",
           "ffn_conv_b", "ffn_w_down", "final_g")
BIG_NAMES = tuple(n for n, _, _ in BIG)
SHARDED_SMALL = {"ssd_conv_w": XBC // 4, "ffn_conv_w": 2 * D_FF // 4}


def _to_rows(flat):
    n = flat.shape[0]
    rows = -(-n // LANES)
    rows = -(-rows // 8) * 8
    return jnp.pad(flat, (0, rows * LANES - n)).reshape(rows, LANES)


def _flatten(tree, names):
    return jnp.concatenate([tree[n].reshape(-1) for n in names])


def _unflatten(flat, shapes, names):
    out, o = {}, 0
    for n in names:
        k = math.prod(shapes[n])
        out[n] = flat[o:o + k].reshape(shapes[n])
        o += k
    return out


def kernel(x, rel_bias, ln1_g, w_in, b_gate, w_a, pool_w, pool_scale, w_b, ssd_conv_w, ssd_conv_b, ssd_dt_bias, ssd_a_log, ssd_d, ssd_norm_w, w_c, w_o, ln2_g, ffn_w_up, ffn_conv_w, ffn_conv_b, ffn_w_down, final_g, loss_target, m_rel_bias, m_ln1_g, m_w_in, m_b_gate, m_w_a, m_pool_w, m_pool_scale, m_w_b, m_ssd_conv_w, m_ssd_conv_b, m_ssd_dt_bias, m_ssd_a_log, m_ssd_d, m_ssd_norm_w, m_w_c, m_w_o, m_ln2_g, m_ffn_w_up, m_ffn_conv_w, m_ffn_conv_b, m_ffn_w_down, m_final_g, v_rel_bias, v_ln1_g, v_w_in, v_b_gate, v_w_a, v_pool_w, v_pool_scale, v_w_b, v_ssd_conv_w, v_ssd_conv_b, v_ssd_dt_bias, v_ssd_a_log, v_ssd_d, v_ssd_norm_w, v_w_c, v_w_o, v_ln2_g, v_ffn_w_up, v_ffn_conv_w, v_ffn_conv_b, v_ffn_w_down, v_final_g):
    W = dict(rel_bias=rel_bias, ln1_g=ln1_g, w_in=w_in, b_gate=b_gate, w_a=w_a, pool_w=pool_w, pool_scale=pool_scale,
             w_b=w_b, ssd_conv_w=ssd_conv_w, ssd_conv_b=ssd_conv_b, ssd_dt_bias=ssd_dt_bias, ssd_a_log=ssd_a_log,
             ssd_d=ssd_d, ssd_norm_w=ssd_norm_w, w_c=w_c, w_o=w_o, ln2_g=ln2_g, ffn_w_up=ffn_w_up,
             ffn_conv_w=ffn_conv_w, ffn_conv_b=ffn_conv_b, ffn_w_down=ffn_w_down, final_g=final_g)
    M = dict(rel_bias=m_rel_bias, ln1_g=m_ln1_g, w_in=m_w_in, b_gate=m_b_gate, w_a=m_w_a, pool_w=m_pool_w,
             pool_scale=m_pool_scale, w_b=m_w_b, ssd_conv_w=m_ssd_conv_w, ssd_conv_b=m_ssd_conv_b,
             ssd_dt_bias=m_ssd_dt_bias, ssd_a_log=m_ssd_a_log, ssd_d=m_ssd_d, ssd_norm_w=m_ssd_norm_w, w_c=m_w_c,
             w_o=m_w_o, ln2_g=m_ln2_g, ffn_w_up=m_ffn_w_up, ffn_conv_w=m_ffn_conv_w, ffn_conv_b=m_ffn_conv_b,
             ffn_w_down=m_ffn_w_down, final_g=m_final_g)
    V = dict(rel_bias=v_rel_bias, ln1_g=v_ln1_g, w_in=v_w_in, b_gate=v_b_gate, w_a=v_w_a, pool_w=v_pool_w,
             pool_scale=v_pool_scale, w_b=v_w_b, ssd_conv_w=v_ssd_conv_w, ssd_conv_b=v_ssd_conv_b,
             ssd_dt_bias=v_ssd_dt_bias, ssd_a_log=v_ssd_a_log, ssd_d=v_ssd_d, ssd_norm_w=v_ssd_norm_w, w_c=v_w_c,
             w_o=v_w_o, ln2_g=v_ln2_g, ffn_w_up=v_ffn_w_up, ffn_conv_w=v_ffn_conv_w, ffn_conv_b=v_ffn_conv_b,
             ffn_w_down=v_ffn_w_down, final_g=v_final_g)
    nl = ln1_g.shape[0]
    px, py, pc_ = _position()
    chip = 2 * px + py
    cidx = jnp.reshape(pc_, (1,)).astype(jnp.int32)
    chip_idx = jnp.reshape(chip, (1,)).astype(jnp.int32)

    placed = {}
    for n, cs in SHARDED_SMALL.items():
        full = jnp.zeros(W[n].shape[:-1] + (4 * cs,), F32)
        full = lax.dynamic_update_slice(full, W[n], (0, 0, chip * cs))
        placed[n] = jnp.where(pc_ == 0, full, 0.0)
    names_sh = tuple(SHARDED_SMALL)
    shapes_sh = {n: placed[n].shape for n in names_sh}
    got = _all_reduce_small(_to_rows(_flatten(placed, names_sh)), "gather_small")
    small = {n: W[n] for n in SMALL_LAYER}
    small.update(_unflatten(got.reshape(-1), shapes_sh, names_sh))

    packs = _pack_blocks({n: W[n] for n in BIG_NAMES}, BF16)

    half = PACK_PAD // 2
    units = half // 16

    def share(weights, total):
        tot = sum(weights.values())
        return {n: math.ceil(total * v / tot) for n, v in weights.items()}

    gathers = {}

    def gather(i):
        if i not in gathers:
            buf = lax.dynamic_update_slice(lax.empty((4, PACK_PAD, D), BF16), packs[i][None], (chip, 0, 0))
            gathers[i] = _Stream(packs[i], buf, functools.partial(_gather_parts, half), 6, units, "gather_w")
        return gathers[i]

    def layer_full(i):
        return _operands(gather(i).drain())

    fwd_share = share(dict(in_a=89, in_b=26, in_c=57, in_d=66, mm_wo=28, mm_up=120, mm_down=46), units)

    def fwd_hooks(i):
        if i + 1 >= nl:
            return lambda name: None
        return lambda name: gather(i + 1).hook(fwd_share[name]) if name in fwd_share else None

    exchanges = {}
    bwd_share = share(dict(d_f=91, g_down=67, d_u2_a=42, d_u2_v=45, g_up_a=52, g_up_v=52, d_merged=29, g_wo=19,
                           d_u_wgate=48, g_in_wgate=41), units)

    def after_bwd(i, gw):
        g = _pack_operands(gw, BF16)
        recv = _rs_pair_exchange(g, "rs_pair")
        hsum = _rs_add_pair(g, recv, cidx, "rs_add_pair")
        exchanges[i] = (hsum, _Stream(hsum, lax.empty((3, half, D), BF16), _rs_chip_parts, 3, units, "rs_chips"))

    def bwd_hooks(i):
        if i + 1 >= nl:
            return lambda name: None
        return lambda name: exchanges[i + 1][1].hook(bwd_share[name]) if name in bwd_share else None

    loss, dx, gws, gss, dfinal, drel = _local_step(x[0], loss_target[0], rel_bias, final_g, layer_full, small,
                                                   fwd_hooks, bwd_hooks, after_bwd)

    grads = {}
    red = []
    for i in range(nl):
        hsum, stream = exchanges[i]
        r = _rs_add_chips(hsum, stream.drain(), chip_idx, "rs_add_chips")
        other = _rs_swap(r, "rs_swap")
        both = jnp.concatenate([jnp.where(pc_ == 0, r, other), jnp.where(pc_ == 0, other, r)], axis=0)
        red.append(_unpack_blocks(both))
    for n in BIG_NAMES:
        grads[n] = jnp.stack([red[i][n] for i in range(nl)], axis=0)

    sg = {}
    for n in SMALL_LAYER:
        sg[n] = jnp.stack([gss[i][n] for i in range(nl)], axis=0)
    for n in ("ssd_dt_bias", "ssd_a_log", "ssd_d"):
        sg[n] = sg[n][:, 0, :SSD_HEADS]
    sg["rel_bias"] = drel[:, :REL_BUCKETS].T
    sg["final_g"] = dfinal.reshape(D)
    sg["loss"] = loss[0, :1]
    names_sg = tuple(sg)
    shapes_sg = {n: ((nl,) + W[n].shape[1:] if n in SMALL_LAYER and n not in SHARDED_SMALL else
                     (placed[n].shape if n in SHARDED_SMALL else sg[n].shape)) for n in names_sg}
    for n in names_sg:
        sg[n] = sg[n].reshape(shapes_sg[n])
    tot = _all_reduce_small(_to_rows(_flatten(sg, names_sg)), "allreduce_small")
    tot = _unflatten(tot.reshape(-1), shapes_sg, names_sg)
    loss_out = tot.pop("loss").reshape(())
    for n, cs in SHARDED_SMALL.items():
        tot[n] = lax.dynamic_slice(tot[n], (0, 0, chip * cs), tot[n].shape[:-1] + (cs,))
    grads.update(tot)

    delta, new_m, new_v = {}, {}, {}
    for n in BIG_NAMES:
        shp = W[n].shape
        r2 = lambda a: a.reshape(-1, shp[-1])
        dl, m2, v2 = _adamw(r2(W[n]), r2(grads[n]), r2(M[n]), r2(V[n]), "adamw_" + n)
        delta[n], new_m[n], new_v[n] = dl.reshape(shp), m2.reshape(shp), v2.reshape(shp)
    names_s = tuple(n for n in WEIGHTS if n not in BIG_NAMES)
    shapes_s = {n: W[n].shape for n in names_s}
    pk = lambda t: _to_rows(_flatten(t, names_s))
    dl, m2, v2 = _adamw(pk(W), pk(grads), pk(M), pk(V), "adamw_small")
    delta.update(_unflatten(dl.reshape(-1), shapes_s, names_s))
    new_m.update(_unflatten(m2.reshape(-1), shapes_s, names_s))
    new_v.update(_unflatten(v2.reshape(-1), shapes_s, names_s))

    return (loss_out, dx[None], *[grads[n] for n in WEIGHTS], *[delta[n] for n in WEIGHTS],
            *[new_m[n] for n in WEIGHTS], *[new_v[n] for n in WEIGHTS])
```

```python
import functools
import math

import jax
import jax.numpy as jnp
from jax import lax
from jax.experimental import pallas as pl
from jax.experimental.pallas import tpu as pltpu

F32 = jnp.float32
BF16 = jnp.bfloat16
MESH = pl.DeviceIdType.MESH

D = 1024
HD = 64
GW = 384
AW = 3 * GW
WIN = 128
DILATIONS = (1, 4, 16)
REL_BUCKETS = 32
REL_MAX_DISTANCE = 2048
POOL_WINDOWS = (2, 4, 8, 16)
PG = 256
SSD_HEADS = 16
SSD_N = 128
SSD_CHUNK = 128
XBC = 1536
D_FF = 2816
EPS = 1e-6
NEG = -1e30
HALO = 16
LANES = 128

SEC_A = 3 * AW
SEC_B = D
SEC_C = D + XBC
SEC_D = 3328
SEC_A_PAD = 3584
IN_WIDTH = SEC_A + SEC_B + SEC_C + 16 + 3 * D

ADAM_LR = 0.001
ADAM_B1 = 0.9
ADAM_B2 = 0.999
ADAM_EPS = 1e-08
ADAM_WD = 0.01
ADAM_STEP = 10
ADAM_TILE = 256 * 1024
MM_VMEM_BYTES = 40 * 1024 * 1024
MM_MAX_OUT_TILE = 1024 * 1024
HBM_BYTES_PER_US = 2.0e6
STEP_US = 0.35
MXU_WIDTH = 256
MXU_FLOPS_PER_US = 0.65e6


_ANY = pl.BlockSpec(memory_space=pl.ANY)


def _pick(d, cands):
    for t in cands:
        if d % t == 0:
            return t
    return d


def _iota(shape, dim):
    return lax.broadcasted_iota(jnp.int32, shape, dim)


def _dg(a, b, ca, cb):
    return lax.dot_general(a.astype(BF16), b.astype(BF16), (((ca,), (cb,)), ((), ())),
                           preferred_element_type=F32)


@jax.custom_vjp
def _bdot_nn(a, b):
    return _dg(a, b, 1, 0)


def _nn_fwd(a, b):
    return _dg(a, b, 1, 0), (a, b)


def _nn_bwd(res, g):
    a, b = res
    return _dg(g, b, 1, 1), _dg(a, g, 0, 0)


_bdot_nn.defvjp(_nn_fwd, _nn_bwd)


@jax.custom_vjp
def _bdot_nt(a, b):
    return _dg(a, b, 1, 1)


def _nt_fwd(a, b):
    return _dg(a, b, 1, 1), (a, b)


def _nt_bwd(res, g):
    a, b = res
    return _dg(g, b, 1, 0), _dg(g, a, 0, 0)


_bdot_nt.defvjp(_nt_fwd, _nt_bwd)


@jax.custom_vjp
def _bdot_tn(a, b):
    return _dg(a, b, 0, 0)


def _tn_fwd(a, b):
    return _dg(a, b, 0, 0), (a, b)


def _tn_bwd(res, g):
    a, b = res
    return _dg(b, g, 1, 1), _dg(a, g, 1, 0)


_bdot_tn.defvjp(_tn_fwd, _tn_bwd)


def _fdot(a, b):
    return jnp.dot(a, b, preferred_element_type=F32, precision=lax.Precision.HIGHEST)


def _sigmoid(x):
    return 0.5 * jnp.tanh(0.5 * x) + 0.5


def _silu(x):
    return x * _sigmoid(x)


def _softplus(x):
    return jnp.maximum(x, 0.0) + jnp.log(1.0 + jnp.exp(-jnp.abs(x)))


def _lane_pick(m, h):
    return jnp.sum(jnp.where(_iota(m.shape, 1) == h, m, 0.0), axis=1, keepdims=True)


def _row_pick(m, h):
    return jnp.sum(jnp.where(_iota(m.shape, 0) == h, m, 0.0), axis=0, keepdims=True)


def _stack_rows(rows, n):
    c = rows[0].shape[1]
    r = _iota((n, c), 0)
    out = jnp.zeros((n, c), F32)
    for k, v in enumerate(rows):
        out = out + jnp.where(r == k, v, 0.0)
    return out


def _mm(a, b, *, ta=False, tb=False, add=None, out_dtype=F32, name, hook=None):
    if ta:
        K, M = a.shape
    else:
        M, K = a.shape
    if tb:
        N, Kb = b.shape
    else:
        Kb, N = b.shape
    assert K == Kb, (a.shape, b.shape, ta, tb)
    tm, tn, tk = _mm_tiles(M, N, K, a.dtype.itemsize, b.dtype.itemsize, jnp.dtype(out_dtype).itemsize,
                           0 if add is None else add.dtype.itemsize)
    ni, nj, nk = M // tm, N // tn, K // tk
    ca = 0 if ta else 1
    cb = 1 if tb else 0
    n_in = 2 if add is None else 3
    n_hin = 0 if hook is None else len(hook.inputs)
    n_hout = 0 if hook is None else len(hook.out_shapes)

    def body(*refs):
        a_ref, b_ref = refs[:2]
        add_ref = None if add is None else refs[2]
        o_ref = refs[n_in + n_hin]
        scr = refs[n_in + n_hin + 1 + n_hout:]
        acc_ref = scr[0] if nk > 1 else None
        hargs = (refs[n_in:n_in + n_hin], refs[n_in + n_hin + 1:n_in + n_hin + 1 + n_hout], scr[1 if nk > 1 else 0:])
        i, j, k = pl.program_id(0), pl.program_id(1), pl.program_id(2)
        if hook is not None:
            @pl.when((i == 0) & (j == 0) & (k == 0))
            def _():
                hook.start(*hargs)

        part = _dg(a_ref[...], b_ref[...], ca, cb)

        def finish(r):
            if add_ref is not None:
                r = r + add_ref[...].astype(F32)
            o_ref[...] = r.astype(o_ref.dtype)

        if nk == 1:
            finish(part)
        else:
            @pl.when(k == 0)
            def _():
                acc_ref[...] = part

            @pl.when((k > 0) & (k < nk - 1))
            def _():
                acc_ref[...] += part

            @pl.when(k == nk - 1)
            def _():
                finish(acc_ref[...] + part)

        if hook is not None:
            @pl.when((i == ni - 1) & (j == nj - 1) & (k == nk - 1))
            def _():
                hook.finish(*hargs)

    a_spec = pl.BlockSpec((tk, tm), lambda i, j, k: (k, i)) if ta else pl.BlockSpec((tm, tk), lambda i, j, k: (i, k))
    b_spec = pl.BlockSpec((tn, tk), lambda i, j, k: (j, k)) if tb else pl.BlockSpec((tk, tn), lambda i, j, k: (k, j))
    in_specs = [a_spec, b_spec]
    args = [a, b]
    if add is not None:
        in_specs.append(pl.BlockSpec((tm, tn), lambda i, j, k: (i, j)))
        args.append(add)
    out_specs = [pl.BlockSpec((tm, tn), lambda i, j, k: (i, j))]
    out_shape = [jax.ShapeDtypeStruct((M, N), out_dtype)]
    scratch = [pltpu.VMEM((tm, tn), F32)] if nk > 1 else []
    aliases = {}
    if hook is not None:
        in_specs += [_ANY] * n_hin
        args += list(hook.inputs)
        out_specs += [_ANY] * n_hout
        out_shape += list(hook.out_shapes)
        scratch += list(hook.scratch)
        aliases = {n_in + hi: 1 + ho for hi, ho in hook.aliases.items()}
    sem = ("parallel", "parallel", "arbitrary") if hook is None else ("arbitrary",) * 3
    res = pl.pallas_call(
        body, name=name, grid=(ni, nj, nk), in_specs=in_specs, out_specs=out_specs, out_shape=out_shape,
        scratch_shapes=scratch, input_output_aliases=aliases,
        compiler_params=pltpu.CompilerParams(dimension_semantics=sem),
    )(*args)
    if hook is not None:
        hook.done(res[1:])
    return res[0]


def _mm_tiles(M, N, K, sa, sb, so, sadd):
    def tiles(d):
        return [t for t in range(LANES, min(d, 2048) + 1, LANES) if d % t == 0] or [d]

    best = None
    for tk in [K] + [t for t in tiles(K) if t < K]:
        for tm in tiles(M):
            for tn in tiles(N):
                vmem = 2 * (tm * tk * sa + tk * tn * sb + tm * tn * (so + sadd)) + (tm * tn * 4 if tk < K else 0)
                if vmem > MM_VMEM_BYTES or tm * tn > MM_MAX_OUT_TILE:
                    continue
                a_reads = 1 if tk == K else N // tn
                traffic = M * K * sa * a_reads + K * N * sb * (M // tm) + M * N * (so + sadd)
                steps = (M // tm) * (N // tn) * (K // tk)
                width = -(-tn // MXU_WIDTH) * MXU_WIDTH
                mxu = 2.0 * M * K * N * (width / tn) / MXU_FLOPS_PER_US
                cost = max(traffic / HBM_BYTES_PER_US, mxu) + steps * STEP_US
                if best is None or cost < best[0]:
                    best = (cost, tm, tn, tk)
    assert best is not None, (M, N, K)
    return best[1:]


class _Hook:
    def __init__(self, inputs, out_shapes, aliases, scratch, start, finish, done):
        self.inputs, self.out_shapes, self.aliases, self.scratch = inputs, out_shapes, aliases, scratch
        self.start, self.finish, self.done = start, finish, done


def _rows(name, fn, ins, outs, accs=(), *, tm, nrows, ncol=1):
    nt = nrows // tm
    hb = tm // HALO
    nh = nrows // HALO
    in_specs, args = [], []
    for kind, arr, cw, base in ins:
        if kind == "row":
            cw = arr.shape[1] if cw is None else cw
            in_specs.append(pl.BlockSpec((tm, cw), lambda j, i, base=base: (i, base + j)))
        elif kind == "prev":
            in_specs.append(pl.BlockSpec((HALO, cw), lambda j, i, base=base: (jnp.maximum(i * hb - 1, 0), base + j)))
        elif kind == "next":
            in_specs.append(pl.BlockSpec((HALO, cw), lambda j, i, base=base: (jnp.minimum((i + 1) * hb, nh - 1), base + j)))
        elif kind == "const":
            in_specs.append(pl.BlockSpec(arr.shape, lambda j, i, nd=arr.ndim: (0,) * nd))
        elif kind == "ccol":
            in_specs.append(pl.BlockSpec((arr.shape[0], cw), lambda j, i, base=base: (0, base + j)))
        else:
            raise ValueError(kind)
        args.append(arr)
    out_specs, out_shape = [], []
    for ctot, cw, base, dt in outs:
        out_specs.append(pl.BlockSpec((tm, cw), lambda j, i, base=base: (i, base + j)))
        out_shape.append(jax.ShapeDtypeStruct((nrows, ctot), dt))
    for r, ctot, cw in accs:
        out_specs.append(pl.BlockSpec((r, cw), lambda j, i: (0, j)))
        out_shape.append(jax.ShapeDtypeStruct((r, ctot), F32))
    n_in, n_out = len(ins), len(outs)

    def body(*refs):
        j = pl.program_id(0)
        i = pl.program_id(1)
        res = fn(i, j, *[r[...] for r in refs[:n_in]])
        for r, v in zip(refs[n_in:n_in + n_out], res[:n_out]):
            r[...] = v.astype(r.dtype)
        for r, v in zip(refs[n_in + n_out:], res[n_out:]):
            @pl.when(i == 0)
            def _(r=r, v=v):
                r[...] = v

            @pl.when(i > 0)
            def _(r=r, v=v):
                r[...] += v

    res = pl.pallas_call(
        body, name=name, grid=(ncol, nt), in_specs=in_specs, out_specs=out_specs, out_shape=out_shape,
        compiler_params=pltpu.CompilerParams(dimension_semantics=("arbitrary", "arbitrary")),
    )(*args)
    return res


def _shift_down(xcat, k):
    return xcat if k == 0 else pltpu.roll(xcat, k, 0)


def _shift_up(xcat, k):
    return xcat if k == 0 else pltpu.roll(xcat, xcat.shape[0] - k, 0)


def _with_prev(i, halo, x):
    return jnp.concatenate([jnp.where(i == 0, 0.0, halo), x], axis=0)


def _with_next(i, nt, x, halo):
    return jnp.concatenate([x, jnp.where(i == nt - 1, 0.0, halo)], axis=0)


def _rms_core(x, g):
    r = lax.rsqrt(jnp.mean(x * x, axis=-1, keepdims=True) + EPS)
    return x * r * g


def _rms_fwd(x, g, name):
    S = x.shape[0]
    return _rows(name, lambda i, j, xv, gv: [_rms_core(xv, gv)],
                 [("row", x, None, 0), ("const", g, None, 0)], [(D, D, 0, BF16)], tm=256, nrows=S)[0]


def _rms_bwd(x, g, du, dres, name):
    S = x.shape[0]

    def fn(i, j, xv, gv, duv, drv):
        _, vjp = jax.vjp(_rms_core, xv, gv)
        dx, dg = vjp(duv)
        return [drv + dx, dg]

    return _rows(name, fn, [("row", x, None, 0), ("const", g, None, 0), ("row", du, None, 0), ("row", dres, None, 0)],
                 [(D, D, 0, F32)], [(1, D, D)], tm=256, nrows=S)


def _final_loss(x, target, g):
    S = x.shape[0]

    def fn(i, j, xv, tv, gv):
        def f(xx, gg):
            err = _rms_core(xx, gg) - tv
            return 0.5 * jnp.sum(err * err) / D

        loss, vjp = jax.vjp(f, xv, gv)
        dx, dg = vjp(jnp.ones((), F32))
        return [dx, dg, jnp.zeros((1, LANES), F32) + loss]

    return _rows("final_loss", fn, [("row", x, None, 0), ("row", target, None, 0), ("const", g, None, 0)],
                 [(D, D, 0, F32)], [(1, D, D), (1, LANES, LANES)], tm=256, nrows=S)


def _attn_valid(n):
    qi = _iota((WIN, 2 * WIN), 0)
    kk = _iota((WIN, 2 * WIN), 1)
    rel = qi + WIN - kk
    return (rel >= 0) & (rel <= WIN) & ((kk >= WIN) | (n > 0))


def _attn_block(q, kp, kc, vp, vc, b0, b1, valid):
    k = jnp.concatenate([kp, kc], axis=0)
    v = jnp.concatenate([vp, vc], axis=0)
    lo = _iota((WIN, LANES), 1) < HD
    scale = 1.0 / math.sqrt(HD)
    os_, ls_ = [], []
    for hh, b in ((0, b0), (1, b1)):
        qm = jnp.where(lo if hh == 0 else ~lo, q, 0.0)
        s = _bdot_nt(qm, k) * scale + b
        s = jnp.where(valid, s, NEG)
        m = lax.stop_gradient(jnp.max(s, axis=1, keepdims=True))
        p = jnp.exp(s - m)
        l = jnp.sum(p, axis=1, keepdims=True)
        os_.append(_bdot_nn(p, v) / l)
        ls_.append(m + jnp.log(l))
    return jnp.where(lo, os_[0], os_[1]), jnp.where(lo, ls_[0], ls_[1])


def _residue_rows(r, d):
    return pl.ds(0, WIN) if d == 1 else pl.ds(r, WIN, stride=d)


def _for_residues(d, fn):
    if d == 1:
        fn(0, 0)
    else:
        lax.fori_loop(0, d, fn, 0, unroll=4)


def _pairs_per_step(d):
    return 3 if d == 1 else 1


def _bias_table(rel_bias, bucket, gi, name):
    def body(t_ref, b_ref, o_ref):
        h = 6 * gi + pl.program_id(0)
        b = b_ref[...]
        acc = jnp.zeros(b.shape, F32)
        for k in range(REL_BUCKETS):
            acc = jnp.where(b == k, t_ref[k, h], acc)
        o_ref[0] = acc

    return pl.pallas_call(
        body, name=name, grid=(6,),
        in_specs=[pl.BlockSpec(memory_space=pltpu.SMEM), pl.BlockSpec((WIN, 2 * WIN), lambda h: (0, 0))],
        out_specs=pl.BlockSpec((1, WIN, 2 * WIN), lambda h: (h, 0, 0)),
        out_shape=jax.ShapeDtypeStruct((6, WIN, 2 * WIN), F32),
    )(rel_bias, bucket)


def _attn_fwd(pa, bias, gi, name):
    S = pa.shape[0]
    d = DILATIONS[gi]
    bt = WIN * d
    nb = S // bt
    hpw = _pairs_per_step(d)
    bw = hpw * LANES
    cb = 3 * gi // hpw

    def body(q_ref, kp_ref, kc_ref, vp_ref, vc_ref, b_ref, o_ref, l_ref):
        valid = _attn_valid(pl.program_id(1))

        def residue(r, carry):
            sl = _residue_rows(r, d)
            for t in range(hpw):
                ln = pl.ds(t * LANES, LANES)
                o, lse = _attn_block(q_ref[sl, ln], kp_ref[sl, ln], kc_ref[sl, ln], vp_ref[sl, ln], vc_ref[sl, ln],
                                     b_ref[2 * t], b_ref[2 * t + 1], valid)
                o_ref[sl, ln] = o
                l_ref[sl, ln] = lse
            return carry

        _for_residues(d, residue)

    def spec(off, prev):
        if prev:
            return pl.BlockSpec((bt, bw), lambda hp, n: (jnp.maximum(n - 1, 0), off // hpw + cb + hp))
        return pl.BlockSpec((bt, bw), lambda hp, n: (n, off // hpw + cb + hp))

    ospec = pl.BlockSpec((bt, bw), lambda hp, n: (n, hp))
    return pl.pallas_call(
        body, name=name, grid=(3 // hpw, nb),
        in_specs=[spec(0, False), spec(9, True), spec(9, False), spec(18, True), spec(18, False),
                  pl.BlockSpec((2 * hpw, WIN, 2 * WIN), lambda hp, n: (hp, 0, 0))],
        out_specs=[ospec, ospec],
        out_shape=[jax.ShapeDtypeStruct((S, GW), F32)] * 2,
        compiler_params=pltpu.CompilerParams(dimension_semantics=("parallel", "arbitrary")),
    )(pa, pa, pa, pa, pa, bias)


def _attn_bwd(pa, bias, do, dlse, db_in, dqkv, gi, name):
    S = pa.shape[0]
    d = DILATIONS[gi]
    bt = WIN * d
    nb = S // bt
    hpw = _pairs_per_step(d)
    bw = hpw * LANES
    cb = 3 * gi // hpw

    def body(q_ref, kp_ref, kc_ref, vp_ref, vc_ref, b_ref, do_ref, dl_ref, dbi_ref, dqi_ref, dki_ref, dvi_ref,
             dq_ref, dk_ref, dv_ref, db_ref, ck, cv):
        n = pl.program_id(1)

        @pl.when(n == 0)
        def _():
            db_ref[...] = dbi_ref[...]
            ck[...] = jnp.zeros_like(ck)
            cv[...] = jnp.zeros_like(cv)

        @pl.when(n < nb)
        def _():
            f = functools.partial(_attn_block, valid=_attn_valid(n))

            def residue(r, carry):
                sl = _residue_rows(r, d)
                cs = pl.ds(pl.multiple_of(r * WIN, WIN), WIN)
                for t in range(hpw):
                    ln = pl.ds(t * LANES, LANES)
                    _, vjp = jax.vjp(f, q_ref[sl, ln], kp_ref[sl, ln], kc_ref[sl, ln], vp_ref[sl, ln], vc_ref[sl, ln],
                                     b_ref[2 * t], b_ref[2 * t + 1])
                    dq, dkp, dkc, dvp, dvc, db0, db1 = vjp((do_ref[sl, ln], dl_ref[sl, ln]))
                    dq_ref[sl, ln] = dq
                    dk_ref[sl, ln] = ck[cs, ln] + dkp
                    dv_ref[sl, ln] = cv[cs, ln] + dvp
                    ck[cs, ln] = dkc
                    cv[cs, ln] = dvc
                    db_ref[2 * t] += db0
                    db_ref[2 * t + 1] += db1
                return carry

            _for_residues(d, residue)

        @pl.when(n == nb)
        def _():
            def residue(r, carry):
                sl = _residue_rows(r, d)
                cs = pl.ds(pl.multiple_of(r * WIN, WIN), WIN)
                dk_ref[sl, :] = ck[cs, :]
                dv_ref[sl, :] = cv[cs, :]
                return carry

            _for_residues(d, residue)

    def cur(n):
        return jnp.minimum(n, nb - 1)

    def spec(off, prev):
        if prev:
            return pl.BlockSpec((bt, bw), lambda hp, n: (jnp.maximum(cur(n) - 1, 0), off // hpw + cb + hp))
        return pl.BlockSpec((bt, bw), lambda hp, n: (cur(n), off // hpw + cb + hp))

    gspec = pl.BlockSpec((bt, bw), lambda hp, n: (cur(n), hp))
    bspec = pl.BlockSpec((2 * hpw, WIN, 2 * WIN), lambda hp, n: (hp, 0, 0))
    qspec = pl.BlockSpec((bt, bw), lambda hp, n: (cur(n), cb + hp))
    kspec = pl.BlockSpec((bt, bw), lambda hp, n: (jnp.maximum(n - 1, 0), cb + hp))
    dq, dk, dv, db = pl.pallas_call(
        body, name=name, grid=(3 // hpw, nb + 1),
        in_specs=[spec(0, False), spec(9, True), spec(9, False), spec(18, True), spec(18, False),
                  bspec, gspec, gspec, bspec, _ANY, _ANY, _ANY],
        out_specs=[qspec, kspec, kspec, bspec],
        out_shape=[jax.ShapeDtypeStruct((S, AW), F32)] * 3 + [jax.ShapeDtypeStruct((6, WIN, 2 * WIN), F32)],
        scratch_shapes=[pltpu.VMEM((bt, bw), F32), pltpu.VMEM((bt, bw), F32)],
        input_output_aliases={9: 0, 10: 1, 11: 2},
        compiler_params=pltpu.CompilerParams(dimension_semantics=("arbitrary", "arbitrary")),
    )(pa, pa, pa, pa, pa, bias, do, dlse, db_in, *dqkv)
    return (dq, dk, dv), db


def _mix_core(o0, o1, o2, l0, l1, l2):
    m = lax.stop_gradient(jnp.maximum(jnp.maximum(l0, l1), l2))
    e0, e1, e2 = jnp.exp(l0 - m), jnp.exp(l1 - m), jnp.exp(l2 - m)
    return (e0 * o0 + e1 * o1 + e2 * o2) / (e0 + e1 + e2)


def _mix_fwd(os_, ls_, name):
    S = os_[0].shape[0]
    ins = [("row", a, None, 0) for a in (*os_, *ls_)]
    return _rows(name, lambda i, j, *v: [_mix_core(*v)], ins, [(GW, GW, 0, BF16)], tm=256, nrows=S)[0]


def _mix_bwd(os_, ls_, datt, name):
    S = datt.shape[0]

    def fn(i, j, *v):
        _, vjp = jax.vjp(_mix_core, *v[:6])
        return list(vjp(v[6]))

    ins = [("row", a, None, 0) for a in (*os_, *ls_, datt)]
    outs = [(GW, GW, 0, F32)] * 6
    r = _rows(name, fn, ins, outs, tm=256, nrows=S)
    return r[:3], r[3:]


def _t5_bucket(dist):
    max_exact = REL_BUCKETS // 2
    is_small = dist < max_exact
    nf = jnp.maximum(dist, 1).astype(F32)
    large = max_exact + (jnp.log(nf / max_exact) / math.log(REL_MAX_DISTANCE / max_exact)
                         * (REL_BUCKETS - max_exact)).astype(jnp.int32)
    large = jnp.minimum(large, REL_BUCKETS - 1)
    return jnp.where(is_small, dist, large)


def _buckets(d):
    qi = jnp.arange(WIN)[:, None]
    kk = jnp.arange(2 * WIN)[None, :]
    rel = qi + WIN - kk
    return _t5_bucket(jnp.clip(rel, 0, None) * d)


def _pool_cnt(i, tm, w):
    pos = i * tm + _iota((tm, PG), 0) + 1
    return jnp.minimum(pos, w).astype(F32)


def _pool_d(i, tm, halo, u):
    ds = []
    for g, w in enumerate(POOL_WINDOWS):
        ug = u[:, g * PG:(g + 1) * PG]
        s = _with_prev(i, halo[:, g * PG:(g + 1) * PG], ug)
        step = 1
        while step < w:
            s = s + _shift_down(s, step)
            step *= 2
        ds.append(s[HALO:] / _pool_cnt(i, tm, w) - ug)
    return ds


def _pool_lin(d0, d1, d2, d3, w0, w1, w2, w3, scale):
    y = jnp.concatenate([_bdot_nn(d0, w0), _bdot_nn(d1, w1), _bdot_nn(d2, w2), _bdot_nn(d3, w3)], axis=1)
    return y * scale


def _pool_fwd(pb, pw, scale, name):
    S = pb.shape[0]
    tm = 256

    def fn(i, j, halo, u, w, sc):
        ds = _pool_d(i, tm, halo, u)
        return [_pool_lin(*ds, *[w[k].astype(F32) for k in range(4)], sc)]

    return _rows(name, fn, [("prev", pb, D, 0), ("row", pb, None, 0), ("const", pw, None, 0), ("const", scale, None, 0)],
                 [(D, D, 0, BF16)], tm=tm, nrows=S)[0]


def _pool_bwd(pb, pw, scale, dpo, name):
    S = pb.shape[0]
    tm = 256
    nt = S // tm

    def fn1(i, j, halo, u, w, sc, dy):
        ds = _pool_d(i, tm, halo, u)
        _, vjp = jax.vjp(_pool_lin, *ds, *[w[k].astype(F32) for k in range(4)], sc)
        g = vjp(dy)
        e = jnp.concatenate([g[k] / _pool_cnt(i, tm, wd) for k, wd in enumerate(POOL_WINDOWS)], axis=1)
        return [e, jnp.concatenate(g[4:8], axis=0), g[8]]

    e, dpw, dsc = _rows(name + "_a", fn1,
                        [("prev", pb, D, 0), ("row", pb, None, 0), ("const", pw, None, 0), ("const", scale, None, 0),
                         ("row", dpo, None, 0)],
                        [(D, D, 0, F32)], [(4 * PG, PG, PG), (1, D, D)], tm=tm, nrows=S)

    def fn2(i, j, ev, halo):
        outs = []
        for g, w in enumerate(POOL_WINDOWS):
            eg = ev[:, g * PG:(g + 1) * PG]
            s = _with_next(i, nt, eg, halo[:, g * PG:(g + 1) * PG])
            step = 1
            while step < w:
                s = s + _shift_up(s, step)
                step *= 2
            outs.append(s[:tm] - eg * _pool_cnt(i, tm, w))
        return [jnp.concatenate(outs, axis=1)]

    du = _rows(name + "_b", fn2, [("row", e, None, 0), ("next", e, D, 0)], [(D, D, 0, BF16)], tm=tm, nrows=S)[0]
    return du, dpw, dsc


def _conv_taps(i, halo, x, K):
    cat = _with_prev(i, halo, x)
    return [_shift_down(cat, K - 1 - k)[HALO:] for k in range(K)]


def _conv_pre(taps, w, b):
    acc = b
    for k, t in enumerate(taps):
        acc = acc + t * _row_pick(w, k)
    return acc


CW = 256
CWS = 512
CONV_TM = 512


def _ext_taps(i, nt, prev, x, nxt, K):
    cat = jnp.concatenate([jnp.where(i == 0, 0.0, prev), x, jnp.where(i == nt - 1, 0.0, nxt)], axis=0)
    return [_shift_down(cat, K - 1 - k)[HALO:] for k in range(K)]


def _conv_t_rows(dp, w, K, tm):
    acc = jnp.zeros((tm, dp.shape[1]), F32)
    for k in range(K):
        acc = acc + _shift_up(dp, K - 1 - k)[:tm] * _row_pick(w, k)
    return acc


def _ssd_conv_fwd(pc, w, b, name):
    S = pc.shape[0]
    base = D // CWS

    def fn(i, j, halo, x, wv, bv):
        return [_silu(_conv_pre(_conv_taps(i, halo, x, 4), wv, bv))]

    return _rows(name, fn, [("prev", pc, CWS, base), ("row", pc, CWS, base), ("ccol", w, CWS, 0), ("ccol", b, CWS, 0)],
                 [(XBC, CWS, 0, F32)], tm=CONV_TM, nrows=S, ncol=XBC // CWS)[0]


def _ssd_conv_bwd(pc, w, b, dy, name):
    S = pc.shape[0]
    base = D // CWS
    tm = CONV_TM
    nt = S // tm

    def fn(i, j, prev, x, nxt, wv, bv, dyv, dyn):
        taps = _ext_taps(i, nt, prev, x, nxt, 4)
        pre = _conv_pre(taps, wv, bv)
        sg = _sigmoid(pre)
        dye = jnp.concatenate([dyv, jnp.where(i == nt - 1, 0.0, dyn)], axis=0)
        dpre = dye * sg * (1.0 + pre * (1.0 - sg))
        dw = _stack_rows([jnp.sum(dpre[:tm] * t[:tm], axis=0, keepdims=True) for t in taps], 4)
        return [_conv_t_rows(dpre, wv, 4, tm), dw, jnp.sum(dpre[:tm], axis=0, keepdims=True)]

    return _rows(name, fn,
                 [("prev", pc, CWS, base), ("row", pc, CWS, base), ("next", pc, CWS, base), ("ccol", w, CWS, 0),
                  ("ccol", b, CWS, 0), ("row", dy, CWS, 0), ("next", dy, CWS, 0)],
                 [(XBC, CWS, 0, BF16)], [(4, XBC, CWS), (1, XBC, CWS)], tm=tm, nrows=S, ncol=XBC // CWS)


NFC = D_FF // CW


def _ffn_act_fwd(h, w, b, name):
    S = h.shape[0]

    def fn(i, j, ha, a, hv, v, wa, wv, ba, bv):
        pa = _conv_pre(_conv_taps(i, ha, a, 3), wa, ba)
        pv = _conv_pre(_conv_taps(i, hv, v, 3), wv, bv)
        return [_silu(pa) * pv]

    return _rows(name, fn,
                 [("prev", h, CW, 0), ("row", h, CW, 0), ("prev", h, CW, NFC), ("row", h, CW, NFC),
                  ("ccol", w, CW, 0), ("ccol", w, CW, NFC), ("ccol", b, CW, 0), ("ccol", b, CW, NFC)],
                 [(D_FF, CW, 0, BF16)], tm=CONV_TM, nrows=S, ncol=NFC)[0]


def _ffn_act_bwd(h, w, b, df, name):
    S = h.shape[0]
    tm = CONV_TM
    nt = S // tm

    def fn(i, j, pa_, a, na, pv_, v, nv, wa, wv, ba, bv, dfv, dfn):
        ta = _ext_taps(i, nt, pa_, a, na, 3)
        tv = _ext_taps(i, nt, pv_, v, nv, 3)
        pa = _conv_pre(ta, wa, ba)
        pv = _conv_pre(tv, wv, bv)
        sg = _sigmoid(pa)
        dfe = jnp.concatenate([dfv.astype(F32), jnp.where(i == nt - 1, 0.0, dfn.astype(F32))], axis=0)
        dpa = dfe * pv * sg * (1.0 + pa * (1.0 - sg))
        dpv = dfe * pa * sg
        res = [_conv_t_rows(dpa, wa, 3, tm), _conv_t_rows(dpv, wv, 3, tm)]
        for dp, taps in ((dpa, ta), (dpv, tv)):
            res.append(_stack_rows([jnp.sum(dp[:tm] * t[:tm], axis=0, keepdims=True) for t in taps], 3))
        for dp in (dpa, dpv):
            res.append(jnp.sum(dp[:tm], axis=0, keepdims=True))
        return res

    ins = []
    for base in (0, NFC):
        ins += [("prev", h, CW, base), ("row", h, CW, base), ("next", h, CW, base)]
    ins += [("ccol", w, CW, 0), ("ccol", w, CW, NFC), ("ccol", b, CW, 0), ("ccol", b, CW, NFC),
            ("row", df, CW, 0), ("next", df, CW, 0)]
    dha, dhv, dwa, dwv, dba, dbv = _rows(
        name, fn, ins, [(D_FF, CW, 0, BF16)] * 2, [(3, D_FF, CW)] * 2 + [(1, D_FF, CW)] * 2, tm=tm, nrows=S, ncol=NFC)
    return dha, dhv, jnp.concatenate([dwa, dwv], axis=1), jnp.concatenate([dba, dbv], axis=1)


NSLAB = D // LANES
CPS = 2


def _ssd_chunk(xs, Bs, Cs, dtraw, dtb, alog, prev):
    lsz = SSD_CHUNK
    lane = _iota((lsz, LANES), 1)
    row = _iota((lsz, LANES), 0)
    dt = jnp.where(lane < SSD_HEADS, _softplus(dtraw + dtb), 0.0)
    a = dt * (-jnp.exp(alog))
    tril = row >= lane
    a_cs = _fdot(tril.astype(F32), a)
    a_cst = a_cs.T
    a_last = jnp.sum(a, axis=0, keepdims=True)
    lo = lane < HD
    top = row < HD
    cbs = [_bdot_nt(Cs[g], Bs[g]) for g in range(2)]
    ys, news = [], []
    for s in range(NSLAB):
        g = s // (NSLAB // 2)
        cols, lms, dts, als = [], [], [], []
        for hh in range(2):
            h = 2 * s + hh
            col = _lane_pick(a_cs, h)
            seg = col - _row_pick(a_cst, h)
            lms.append(jnp.exp(jnp.where(tril, seg, NEG)))
            cols.append(col)
            dts.append(_lane_pick(dt, h))
            als.append(_lane_pick(a_last, h))
        col_x = jnp.where(lo, cols[0], cols[1])
        al_x = jnp.where(lo, als[0], als[1])
        xc = xs[s] * jnp.where(lo, dts[0], dts[1])
        yd = jnp.where(lo, _bdot_nn(cbs[g] * lms[0], xc), _bdot_nn(cbs[g] * lms[1], xc))
        yoff = _bdot_nt(Cs[g], prev[s]) * jnp.exp(col_x)
        ys.append(yd + yoff)
        st = _bdot_tn(xc * jnp.exp(al_x - col_x), Bs[g])
        news.append(prev[s] * jnp.exp(jnp.where(top, als[0], als[1])) + st)
    return ys, news


def _ssd_scan_fwd(xbc_c, pd, dtb, alog, name):
    S = xbc_c.shape[0]
    nc = S // SSD_CHUNK
    rows_ = CPS * SSD_CHUNK

    def body(x_ref, b_ref, c_ref, dt_ref, dtb_ref, al_ref, y_ref, st_ref, state):
        c = pl.program_id(0)

        @pl.when(c == 0)
        def _():
            state[...] = jnp.zeros_like(state)

        prev = [state[s * LANES:(s + 1) * LANES, :] for s in range(NSLAB)]
        for u in range(CPS):
            rw = pl.ds(u * SSD_CHUNK, SSD_CHUNK)
            xs = [x_ref[rw, s * LANES:(s + 1) * LANES] for s in range(NSLAB)]
            Bs = [b_ref[rw, g * SSD_N:(g + 1) * SSD_N] for g in range(2)]
            Cs = [c_ref[rw, g * SSD_N:(g + 1) * SSD_N] for g in range(2)]
            for s in range(NSLAB):
                st_ref[u, s * LANES:(s + 1) * LANES, :] = prev[s]
            ys, prev = _ssd_chunk(xs, Bs, Cs, dt_ref[rw, :], dtb_ref[...], al_ref[...], prev)
            for s in range(NSLAB):
                y_ref[rw, s * LANES:(s + 1) * LANES] = ys[s]
        for s in range(NSLAB):
            state[s * LANES:(s + 1) * LANES, :] = prev[s]

    return pl.pallas_call(
        body, name=name, grid=(nc // CPS,),
        in_specs=[pl.BlockSpec((rows_, D), lambda c: (c, 0)),
                  pl.BlockSpec((rows_, 2 * SSD_N), lambda c: (c, D // (2 * SSD_N))),
                  pl.BlockSpec((rows_, 2 * SSD_N), lambda c: (c, D // (2 * SSD_N) + 1)),
                  pl.BlockSpec((rows_, LANES), lambda c: (c, 0)),
                  pl.BlockSpec((1, LANES), lambda c: (0, 0)), pl.BlockSpec((1, LANES), lambda c: (0, 0))],
        out_specs=[pl.BlockSpec((rows_, D), lambda c: (c, 0)), pl.BlockSpec((CPS, D, SSD_N), lambda c: (c, 0, 0))],
        out_shape=[jax.ShapeDtypeStruct((S, D), F32), jax.ShapeDtypeStruct((nc, D, SSD_N), F32)],
        scratch_shapes=[pltpu.VMEM((D, SSD_N), F32)],
        compiler_params=pltpu.CompilerParams(dimension_semantics=("arbitrary",)),
    )(xbc_c, xbc_c, xbc_c, pd, dtb, alog)


def _ssd_scan_bwd(xbc_c, pd, dtb, alog, states, dy, dxs_skip, name):
    S = xbc_c.shape[0]
    nc = S // SSD_CHUNK
    rows_ = CPS * SSD_CHUNK

    def body(x_ref, b_ref, c_ref, dt_ref, dtb_ref, al_ref, st_ref, dy_ref, sk_ref,
             dx_ref, ddt_ref, ddtb_ref, dal_ref, dstate):
        c = pl.program_id(0)

        @pl.when(c == 0)
        def _():
            dstate[...] = jnp.zeros_like(dstate)
            ddtb_ref[...] = jnp.zeros_like(ddtb_ref)
            dal_ref[...] = jnp.zeros_like(dal_ref)

        dnew = [dstate[s * LANES:(s + 1) * LANES, :] for s in range(NSLAB)]
        for u in reversed(range(CPS)):
            rw = pl.ds(u * SSD_CHUNK, SSD_CHUNK)
            xs = [x_ref[rw, s * LANES:(s + 1) * LANES] for s in range(NSLAB)]
            Bs = [b_ref[rw, g * SSD_N:(g + 1) * SSD_N] for g in range(2)]
            Cs = [c_ref[rw, g * SSD_N:(g + 1) * SSD_N] for g in range(2)]
            prev = [st_ref[u, s * LANES:(s + 1) * LANES, :] for s in range(NSLAB)]
            _, vjp = jax.vjp(_ssd_chunk, xs, Bs, Cs, dt_ref[rw, :], dtb_ref[...], al_ref[...], prev)
            dys = [dy_ref[rw, s * LANES:(s + 1) * LANES] for s in range(NSLAB)]
            dxs, dBs, dCs, ddt, ddtb, dal, dnew = vjp((dys, dnew))
            for s in range(NSLAB):
                dx_ref[rw, s * LANES:(s + 1) * LANES] = dxs[s] + sk_ref[rw, s * LANES:(s + 1) * LANES]
            for g in range(2):
                dx_ref[rw, D + g * SSD_N:D + (g + 1) * SSD_N] = dBs[g]
                dx_ref[rw, D + 2 * SSD_N + g * SSD_N:D + 2 * SSD_N + (g + 1) * SSD_N] = dCs[g]
            ddt_ref[rw, :] = ddt
            ddtb_ref[...] += ddtb
            dal_ref[...] += dal
        for s in range(NSLAB):
            dstate[s * LANES:(s + 1) * LANES, :] = dnew[s]

    def rv(c):
        return nc // CPS - 1 - c

    return pl.pallas_call(
        body, name=name, grid=(nc // CPS,),
        in_specs=[pl.BlockSpec((rows_, D), lambda c: (rv(c), 0)),
                  pl.BlockSpec((rows_, 2 * SSD_N), lambda c: (rv(c), D // (2 * SSD_N))),
                  pl.BlockSpec((rows_, 2 * SSD_N), lambda c: (rv(c), D // (2 * SSD_N) + 1)),
                  pl.BlockSpec((rows_, LANES), lambda c: (rv(c), 0)),
                  pl.BlockSpec((1, LANES), lambda c: (0, 0)), pl.BlockSpec((1, LANES), lambda c: (0, 0)),
                  pl.BlockSpec((CPS, D, SSD_N), lambda c: (rv(c), 0, 0)),
                  pl.BlockSpec((rows_, D), lambda c: (rv(c), 0)),
                  pl.BlockSpec((rows_, D), lambda c: (rv(c), 0))],
        out_specs=[pl.BlockSpec((rows_, XBC), lambda c: (rv(c), 0)),
                   pl.BlockSpec((rows_, LANES), lambda c: (rv(c), 0)),
                   pl.BlockSpec((1, LANES), lambda c: (0, 0)), pl.BlockSpec((1, LANES), lambda c: (0, 0))],
        out_shape=[jax.ShapeDtypeStruct((S, XBC), F32), jax.ShapeDtypeStruct((S, LANES), F32),
                   jax.ShapeDtypeStruct((1, LANES), F32), jax.ShapeDtypeStruct((1, LANES), F32)],
        scratch_shapes=[pltpu.VMEM((D, SSD_N), F32)],
        compiler_params=pltpu.CompilerParams(dimension_semantics=("arbitrary",)),
    )(xbc_c, xbc_c, xbc_c, pd, dtb, alog, states, dy, dxs_skip)


def _ssd_post_core(y, xs, z, d128, nw):
    tm = y.shape[0]
    ex = (_iota((LANES, D), 1) // HD == _iota((LANES, D), 0)).astype(F32)
    d_x = jnp.sum(_fdot(jnp.broadcast_to(d128, (8, LANES)), ex), axis=0, keepdims=True) * 0.125
    y2 = (y + d_x * xs) * _silu(z)
    lo = _iota((tm, D), 1) < D // 2
    sq = y2 * y2
    ms0 = jnp.sum(jnp.where(lo, sq, 0.0), axis=-1, keepdims=True) / (D // 2)
    ms1 = jnp.sum(jnp.where(lo, 0.0, sq), axis=-1, keepdims=True) / (D // 2)
    r = jnp.where(lo, lax.rsqrt(ms0 + EPS), lax.rsqrt(ms1 + EPS))
    return y2 * r * nw


def _ssd_post_ins(y, xbc_c, pc, d128, nw):
    return [("row", y, None, 0), ("row", xbc_c, D, 0), ("row", pc, D, 0), ("const", d128, None, 0), ("const", nw, None, 0)]


def _ssd_post_fwd(y, xbc_c, pc, d128, nw, name):
    S = y.shape[0]
    return _rows(name, lambda i, j, *v: [_ssd_post_core(*v)], _ssd_post_ins(y, xbc_c, pc, d128, nw),
                 [(D, D, 0, BF16)], tm=128, nrows=S)[0]


def _ssd_post_bwd(y, xbc_c, pc, d128, nw, dout, name):
    S = y.shape[0]

    def fn(i, j, *v):
        _, vjp = jax.vjp(_ssd_post_core, *v[:5])
        return list(vjp(v[5]))

    return _rows(name, fn, _ssd_post_ins(y, xbc_c, pc, d128, nw) + [("row", dout, None, 0)],
                 [(D, D, 0, F32), (D, D, 0, F32), (D, D, 0, BF16)], [(1, LANES, LANES), (1, D, D)], tm=128, nrows=S)


def _gates_core(g0, g1, g2, b0, b1, b2, ya, yb, yc):
    return _sigmoid(g0 + b0) * ya + _sigmoid(g1 + b1) * yb + _sigmoid(g2 + b2) * yc


def _gate_parts(pdv, bv):
    gp = pltpu.roll(pdv, SEC_D - 16, 1)
    return [gp[:, k * D:(k + 1) * D] for k in range(3)] + [bv[:, k * D:(k + 1) * D] for k in range(3)]


def _gates_fwd(pd, bg, ya, yb, yc, name):
    S = pd.shape[0]

    def fn(i, j, pdv, bv, a, b, c):
        return [_gates_core(*_gate_parts(pdv, bv), a, b, c)]

    return _rows(name, fn, [("row", pd, None, 0), ("const", bg, None, 0), ("row", ya, None, 0), ("row", yb, None, 0),
                            ("row", yc, None, 0)], [(D, D, 0, BF16)], tm=128, nrows=S)[0]


def _gates_bwd(pd, bg, ya, yb, yc, dm, name):
    S = pd.shape[0]
    tm = 128

    def fn(i, j, pdv, bv, a, b, c, dmv):
        _, vjp = jax.vjp(_gates_core, *_gate_parts(pdv, bv), a, b, c)
        g = vjp(dmv)
        return [g[6], g[7], g[8], jnp.concatenate(g[0:3], axis=1), jnp.concatenate(g[3:6], axis=1)]

    return _rows(name, fn, [("row", pd, None, 0), ("const", bg, None, 0), ("row", ya, None, 0), ("row", yb, None, 0),
                            ("row", yc, None, 0), ("row", dm, None, 0)],
                 [(D, D, 0, BF16)] * 3 + [(3 * D, 3 * D, 0, BF16)], [(1, 3 * D, 3 * D)], tm=tm, nrows=S)


def _adamw(w, g, m, v, name):
    rows, C = w.shape
    tm = _pick(rows, [t for t in (512, 256, 128, 64, 32, 16, 8) if t * C <= ADAM_TILE])

    def fn(i, j, wv, gv, mv, vv):
        m2 = ADAM_B1 * mv + (1.0 - ADAM_B1) * gv
        v2 = ADAM_B2 * vv + (1.0 - ADAM_B2) * jnp.square(gv)
        m_hat = m2 / (1.0 - ADAM_B1 ** ADAM_STEP)
        v_hat = v2 / (1.0 - ADAM_B2 ** ADAM_STEP)
        delta = -ADAM_LR * (m_hat / (jnp.sqrt(v_hat) + ADAM_EPS) + ADAM_WD * wv)
        return [delta, m2, v2]

    return _rows(name, fn, [("row", a, None, 0) for a in (w, g, m, v)], [(C, C, 0, F32)] * 3, tm=tm, nrows=rows)


def _position():
    return lax.axis_index("x"), lax.axis_index("y"), lax.axis_index("c")


def _other_chips(x, y):
    return [(1 - x, y), (x, 1 - y), (1 - x, 1 - y)]


_HBM = pl.BlockSpec(memory_space=pltpu.HBM)


def _gather_parts(half, lo, n):
    def copies(p_ref, out_ref, send_sems, recv_sems):
        x, y, c = _position()
        sibling = (x, y, 1 - c)
        chips = _other_chips(x, y)

        def slab(chip, h):
            return out_ref.at[2 * chip[0] + chip[1], pl.ds(h * half + lo, n), :]

        def copy(k, src, dst, to):
            return pltpu.make_async_remote_copy(src_ref=src, dst_ref=dst, send_sem=send_sems.at[k],
                                                recv_sem=recv_sems.at[k], device_id=to, device_id_type=MESH)

        first = [copy(j, p_ref.at[pl.ds(c * half + lo, n), :], slab((x, y), c), (*chip, c)) for j, chip in enumerate(chips)]
        passed = [copy(3 + j, slab(chip, c), slab(chip, c), sibling) for j, chip in enumerate(chips)]
        from_chips = [copy(j, slab(chip, c), slab(chip, c), (x, y, c)) for j, chip in enumerate(chips)]
        from_sibling = [copy(3 + j, slab(chip, 1 - c), slab(chip, 1 - c), (x, y, c)) for j, chip in enumerate(chips)]
        return first, passed, from_chips, from_sibling

    def start(ins, outs, scr):
        for cp in copies(ins[0], outs[0], *scr)[0]:
            cp.start()

    def finish(ins, outs, scr):
        first, passed, from_chips, from_sibling = copies(ins[0], outs[0], *scr)
        for j in range(3):
            from_chips[j].wait_recv()
            passed[j].start()
        for cp in from_sibling:
            cp.wait_recv()
        for cp in first + passed:
            cp.wait_send()

    return start, finish


def _rs_chip_parts(lo, n):
    def copies(h_ref, out_ref, send_sems, recv_sems):
        x, y, c = _position()
        return [pltpu.make_async_remote_copy(src_ref=h_ref.at[2 * chip[0] + chip[1], pl.ds(lo, n), :],
                                             dst_ref=out_ref.at[j, pl.ds(lo, n), :],
                                             send_sem=send_sems.at[j], recv_sem=recv_sems.at[j],
                                             device_id=(*chip, c), device_id_type=MESH)
                for j, chip in enumerate(_other_chips(x, y))]

    def start(ins, outs, scr):
        for cp in copies(ins[0], outs[0], *scr):
            cp.start()

    def finish(ins, outs, scr):
        for cp in copies(ins[0], outs[0], *scr):
            cp.wait()

    return start, finish


class _Stream:
    def __init__(self, src, buf, parts, nsem, units, name):
        self.src, self.buf, self.parts, self.nsem, self.name = src, buf, parts, nsem, name
        self.next, self.units = 0, units

    def _scratch(self):
        return [pltpu.SemaphoreType.DMA((self.nsem,)), pltpu.SemaphoreType.DMA((self.nsem,))]

    def _take(self, units):
        units = min(units, self.units - self.next)
        lo = self.next * 16
        self.next += units
        return lo, units * 16

    def _set(self, outs):
        self.buf = outs[0]

    def hook(self, units):
        lo, n = self._take(units)
        if n == 0:
            return None
        start, finish = self.parts(lo, n)
        return _Hook([self.src, self.buf], [jax.ShapeDtypeStruct(self.buf.shape, self.buf.dtype)], {1: 0},
                     self._scratch(), start, finish, self._set)

    def drain(self):
        lo, n = self._take(self.units)
        if n:
            start, finish = self.parts(lo, n)

            def body(s_ref, b_ref, o_ref, send_sems, recv_sems):
                args = ((s_ref, b_ref), (o_ref,), (send_sems, recv_sems))
                start(*args)
                finish(*args)

            self.buf = pl.pallas_call(
                body, name=self.name, in_specs=[_ANY, _ANY], out_specs=_ANY,
                out_shape=jax.ShapeDtypeStruct(self.buf.shape, self.buf.dtype),
                scratch_shapes=self._scratch(), input_output_aliases={1: 0},
            )(self.src, self.buf)
        return self.buf


def _rs_pair_exchange(g, name):
    _, R, C = g.shape
    Rh = R // 2

    def body(g_ref, out_ref, send_sem, recv_sem):
        x, y, c = _position()
        src = g_ref.at[pl.ds(0, 4), pl.ds((1 - c) * Rh, Rh), :]
        cp = pltpu.make_async_remote_copy(src_ref=src, dst_ref=out_ref, send_sem=send_sem,
                                          recv_sem=recv_sem, device_id=(x, y, 1 - c), device_id_type=MESH)
        cp.start()
        cp.wait()

    return pl.pallas_call(
        body, name=name, in_specs=[_HBM], out_specs=_HBM,
        out_shape=jax.ShapeDtypeStruct((4, Rh, C), g.dtype),
        scratch_shapes=[pltpu.SemaphoreType.DMA, pltpu.SemaphoreType.DMA],
    )(g)


def _rs_swap(r, name):
    Rh, C = r.shape

    def body(r_ref, out_ref, send_sem, recv_sem):
        x, y, c = _position()
        cp = pltpu.make_async_remote_copy(src_ref=r_ref, dst_ref=out_ref, send_sem=send_sem,
                                          recv_sem=recv_sem, device_id=(x, y, 1 - c), device_id_type=MESH)
        cp.start()
        cp.wait()

    return pl.pallas_call(
        body, name=name, in_specs=[_HBM], out_specs=_HBM,
        out_shape=jax.ShapeDtypeStruct((Rh, C), r.dtype),
        scratch_shapes=[pltpu.SemaphoreType.DMA, pltpu.SemaphoreType.DMA],
    )(r)


def _rs_add_pair(g, recv, cidx, name):
    _, R, C = g.shape
    Rh = R // 2
    tm = _pick(Rh, (400, 280, 200, 160, 80, 40, 16, 8))
    nt = Rh // tm

    def body(c_ref, g_ref, r_ref, o_ref):
        o_ref[...] = (g_ref[...].astype(F32) + r_ref[...].astype(F32)).astype(o_ref.dtype)

    return pl.pallas_call(
        body, name=name,
        grid_spec=pltpu.PrefetchScalarGridSpec(
            num_scalar_prefetch=1, grid=(4, nt),
            in_specs=[pl.BlockSpec((1, tm, C), lambda k, i, cr: (k, cr[0] * nt + i, 0)),
                      pl.BlockSpec((1, tm, C), lambda k, i, cr: (k, i, 0))],
            out_specs=pl.BlockSpec((1, tm, C), lambda k, i, cr: (k, i, 0))),
        out_shape=jax.ShapeDtypeStruct((4, Rh, C), BF16),
    )(cidx, g, recv)


def _rs_add_chips(h, recv, chip_idx, name):
    _, Rh, C = h.shape
    tm = _pick(Rh, (400, 280, 200, 160, 80, 40, 16, 8))

    def body(c_ref, h_ref, r_ref, o_ref):
        acc = h_ref[0].astype(F32)
        for j in range(3):
            acc = acc + r_ref[j].astype(F32)
        o_ref[...] = acc

    return pl.pallas_call(
        body, name=name,
        grid_spec=pltpu.PrefetchScalarGridSpec(
            num_scalar_prefetch=1, grid=(Rh // tm,),
            in_specs=[pl.BlockSpec((1, tm, C), lambda i, cr: (cr[0], i, 0)), pl.BlockSpec((3, tm, C), lambda i, cr: (0, i, 0))],
            out_specs=pl.BlockSpec((tm, C), lambda i, cr: (i, 0))),
        out_shape=jax.ShapeDtypeStruct((Rh, C), F32),
    )(chip_idx, h, recv)


def _all_reduce_small(vec, name):
    n, C = vec.shape

    def body(v_ref, out_ref, buf, send_sems, recv_sems):
        x, y, c = _position()

        def flip(k):
            return ((1 - x) if k & 4 else x, (1 - y) if k & 2 else y, (1 - c) if k & 1 else c)

        def idx(p):
            return 4 * p[0] + 2 * p[1] + p[2]

        me = idx((x, y, c))
        buf[me] = v_ref[...]
        cps = [pltpu.make_async_remote_copy(src_ref=v_ref, dst_ref=buf.at[me], send_sem=send_sems.at[k - 1],
                                            recv_sem=recv_sems.at[k - 1], device_id=flip(k), device_id_type=MESH)
               for k in range(1, 8)]
        for cp in cps:
            cp.start()
        for k in range(1, 8):
            pltpu.make_async_remote_copy(src_ref=v_ref, dst_ref=buf.at[idx(flip(k))], send_sem=send_sems.at[k - 1],
                                         recv_sem=recv_sems.at[k - 1], device_id=flip(k), device_id_type=MESH).wait_recv()
        for cp in cps:
            cp.wait_send()
        acc = buf[0]
        for s in range(1, 8):
            acc = acc + buf[s]
        out_ref[...] = acc

    return pl.pallas_call(
        body, name=name,
        in_specs=[pl.BlockSpec(memory_space=pltpu.VMEM)], out_specs=pl.BlockSpec(memory_space=pltpu.VMEM),
        out_shape=jax.ShapeDtypeStruct((n, C), F32),
        scratch_shapes=[pltpu.VMEM((8, n, C), F32), pltpu.SemaphoreType.DMA((7,)), pltpu.SemaphoreType.DMA((7,))],
    )(vec)


BIG = (("w_in", (D, IN_WIDTH // 4), "cols"), ("w_a", (GW, D // 4), "cols"), ("pool_w", (4, PG // 4, PG), "pool"),
       ("w_b", (D // 4, D), "rows"), ("w_c", (D // 4, D), "rows"), ("w_o", (D // 4, D), "rows"),
       ("ffn_w_up", (D, 2 * D_FF // 4), "cols"), ("ffn_w_down", (D_FF // 4, D), "rows"))
def _pack_rows(s):
    k = math.prod(s) // D
    return -(-k // 16) * 16, k


PACK_ROWS = sum(_pack_rows(s)[0] for _, s, _ in BIG)
PACK_PAD = -(-PACK_ROWS // 32) * 32


def _pad_rows(v, rows):
    pad = [(0, 0)] * v.ndim
    pad[-2] = (0, rows - v.shape[-2])
    return jnp.pad(v, pad) if rows > v.shape[-2] else v


def _pack_blocks(blocks, dtype):
    lead = blocks["w_in"].shape[:-2]
    flat = []
    for n, s, how in BIG:
        v = blocks[n].astype(dtype)
        if how == "cols":
            v = jnp.swapaxes(v, -1, -2)
        flat.append(_pad_rows(v.reshape(*lead, -1, D), _pack_rows(s)[0]))
    flat.append(jnp.zeros((*lead, PACK_PAD - PACK_ROWS, D), dtype))
    return jnp.concatenate(flat, axis=-2)


def _unpack_blocks(pack):
    out, r = {}, 0
    for n, s, how in BIG:
        rows, k = _pack_rows(s)
        v = pack[r:r + k, :]
        out[n] = v.reshape(s[1], s[0]).T if how == "cols" else v.reshape(s)
        r += rows
    return out


def _operands(allp):
    out, r = {}, 0
    for n, s, how in BIG:
        rows, k = _pack_rows(s)
        v = allp[:, r:r + k, :]
        if how == "cols":
            out[n] = v.reshape(4 * s[1], s[0])
        elif how == "rows":
            out[n] = v.reshape(4 * s[0], s[1])
        else:
            out[n] = v.reshape(4, *s).transpose(1, 0, 2, 3).reshape(4, PG, PG)
        r += rows
    return out


def _pack_operands(g, dtype):
    flat = []
    for n, s, how in BIG:
        v = g[n].astype(dtype)
        if how == "pool":
            v = v.reshape(4, 4, s[1], s[2]).transpose(1, 0, 2, 3)
        flat.append(_pad_rows(v.reshape(4, -1, D), _pack_rows(s)[0]))
    flat.append(jnp.zeros((4, PACK_PAD - PACK_ROWS, D), dtype))
    return jnp.concatenate(flat, axis=1)


def _layer_fwd(x, w, sm, bias, hk):
    u = _rms_fwd(x, sm["ln1_g"], "rms1")
    pa = _mm(u, w["in_a"], tb=True, name="in_a", hook=hk("in_a"))
    pb = _mm(u, w["in_b"], tb=True, name="in_b", hook=hk("in_b"))
    pc = _mm(u, w["in_c"], tb=True, name="in_c", hook=hk("in_c"))
    pd = _mm(u, w["in_d"], tb=True, name="in_d", hook=hk("in_d"))
    os_, ls_ = [], []
    for gi in range(3):
        o, l = _attn_fwd(pa, bias[gi], gi, "attn_fwd%d" % gi)
        os_.append(o)
        ls_.append(l)
    att = _mix_fwd(os_, ls_, "mix_fwd")
    ya = _mm(att, w["w_a"], tb=True, name="mm_wa")
    pool_o = _pool_fwd(pb, w["pool_w"], sm["pool_scale"], "pool_fwd")
    yb = _mm(pool_o, w["w_b"], name="mm_wb")
    xbc_c = _ssd_conv_fwd(pc, sm["ssd_conv_w"], sm["ssd_conv_b"], "ssd_conv_fwd")
    y_scan, states = _ssd_scan_fwd(xbc_c, pd, sm["ssd_dt_bias"], sm["ssd_a_log"], "ssd_scan_fwd")
    ssd_o = _ssd_post_fwd(y_scan, xbc_c, pc, sm["ssd_d"], sm["ssd_norm_w"], "ssd_post_fwd")
    yc = _mm(ssd_o, w["w_c"], name="mm_wc")
    merged = _gates_fwd(pd, sm["b_gate"], ya, yb, yc, "gates_fwd")
    x1 = _mm(merged, w["w_o"], add=x, name="mm_wo", hook=hk("mm_wo"))
    u2 = _rms_fwd(x1, sm["ln2_g"], "rms2")
    h = _mm(u2, w["ffn_w_up"], tb=True, name="mm_up", hook=hk("mm_up"))
    f = _ffn_act_fwd(h, sm["ffn_conv_w"], sm["ffn_conv_b"], "ffn_act_fwd")
    x2 = _mm(f, w["ffn_w_down"], add=x1, name="mm_down", hook=hk("mm_down"))
    saved = dict(x=x, u=u, pa=pa, pb=pb, pc=pc, pd=pd, os=os_, ls=ls_, att=att, ya=ya, yb=yb, yc=yc, pool_o=pool_o,
                 xbc_c=xbc_c, y_scan=y_scan, states=states, ssd_o=ssd_o, merged=merged, x1=x1, u2=u2, h=h, f=f)
    return x2, saved


def _layer_bwd(dx2, w, sm, bias, dbs, sv, hk):
    gw, gs = {}, {}
    S = dx2.shape[0]

    def gmm(a, b, name):
        return _mm(a, b, ta=True, out_dtype=BF16, name=name, hook=hk(name))

    df = _mm(dx2, w["ffn_w_down"], tb=True, name="d_f", hook=hk("d_f"))
    gw["ffn_w_down"] = gmm(sv["f"], dx2, "g_down")
    dha, dhv, gs["ffn_conv_w"], gs["ffn_conv_b"] = _ffn_act_bwd(sv["h"], sm["ffn_conv_w"], sm["ffn_conv_b"], df, "ffn_act_bwd")
    du2 = _mm(dha, w["up_a"], name="d_u2_a", hook=hk("d_u2_a"))
    du2 = _mm(dhv, w["up_v"], add=du2, name="d_u2_v", hook=hk("d_u2_v"))
    gw["ffn_w_up"] = jnp.concatenate([gmm(dha, sv["u2"], "g_up_a"), gmm(dhv, sv["u2"], "g_up_v")], axis=0)
    dx1, gs["ln2_g"] = _rms_bwd(sv["x1"], sm["ln2_g"], du2, dx2, "rms2_bwd")
    dmerged = _mm(dx1, w["w_o"], tb=True, name="d_merged", hook=hk("d_merged"))
    gw["w_o"] = gmm(sv["merged"], dx1, "g_wo")
    dya, dyb, dyc, dgate, gs["b_gate"] = _gates_bwd(
        sv["pd"], sm["b_gate"], sv["ya"], sv["yb"], sv["yc"], dmerged, "gates_bwd")
    dssd_o = _mm(dyc, w["w_c"], tb=True, name="d_ssd_o")
    gw["w_c"] = gmm(sv["ssd_o"], dyc, "g_wc")
    dy_scan, dxs_skip, dz, gs["ssd_d"], gs["ssd_norm_w"] = _ssd_post_bwd(
        sv["y_scan"], sv["xbc_c"], sv["pc"], sm["ssd_d"], sm["ssd_norm_w"], dssd_o, "ssd_post_bwd")
    dxbc_c, ddt, gs["ssd_dt_bias"], gs["ssd_a_log"] = _ssd_scan_bwd(
        sv["xbc_c"], sv["pd"], sm["ssd_dt_bias"], sm["ssd_a_log"], sv["states"], dy_scan, dxs_skip, "ssd_scan_bwd")
    dxbc, gs["ssd_conv_w"], gs["ssd_conv_b"] = _ssd_conv_bwd(sv["pc"], sm["ssd_conv_w"], sm["ssd_conv_b"], dxbc_c, "ssd_conv_bwd")
    dpool_o = _mm(dyb, w["w_b"], tb=True, name="d_pool_o")
    gw["w_b"] = gmm(sv["pool_o"], dyb, "g_wb")
    dpb, dpw, gs["pool_scale"] = _pool_bwd(sv["pb"], w["pool_w"], sm["pool_scale"], dpool_o, "pool_bwd")
    gw["pool_w"] = dpw.reshape(4, PG, PG)
    datt = _mm(dya, w["w_a"], name="d_att")
    gw["w_a"] = gmm(dya, sv["att"], "g_wa")
    dos, dls = _mix_bwd(sv["os"], sv["ls"], datt, "mix_bwd")
    dqkv = tuple(lax.empty((S, AW), F32) for _ in range(3))
    dbs = list(dbs)
    for gi in range(3):
        dqkv, dbs[gi] = _attn_bwd(sv["pa"], bias[gi], dos[gi], dls[gi], dbs[gi], dqkv, gi, "attn_bwd%d" % gi)
    u = sv["u"]
    pieces = [(dqkv[0], "wq"), (dqkv[1], "wk"), (dqkv[2], "wv"), (dpb, "in_b"), (dz, "wz"), (dxbc, "wxbc"),
              (ddt, "wdt"), (dgate, "wgate")]
    du = None
    g_in = []
    for dp, key in pieces:
        du = _mm(dp, w[key], add=du, name="d_u_" + key, hook=hk("d_u_" + key))
        g = gmm(dp, u, "g_in_" + key)
        g_in.append(g[:SSD_HEADS] if key == "wdt" else g)
    gw["w_in"] = jnp.concatenate(g_in, axis=0)
    dx, gs["ln1_g"] = _rms_bwd(sv["x"], sm["ln1_g"], du, dx1, "rms1_bwd")
    return dx, gw, gs, dbs


SMALL_LAYER = ("ln1_g", "b_gate", "pool_scale", "ssd_conv_w", "ssd_conv_b", "ssd_dt_bias", "ssd_a_log", "ssd_d",
               "ssd_norm_w", "ln2_g", "ffn_conv_w", "ffn_conv_b")


def _pad_lanes(v):
    return jnp.pad(v, (0, LANES - v.shape[0])).reshape(1, LANES)


def _layer_weights(ops):
    wt = ops["w_in"]
    o1, o2, o3 = SEC_A, SEC_A + SEC_B, SEC_A + SEC_B + SEC_C
    w = dict(ops)
    w["in_a"] = jnp.pad(wt[:o1], ((0, SEC_A_PAD - o1), (0, 0)))
    w["in_b"] = wt[o1:o2]
    w["in_c"] = wt[o2:o3]
    w["in_d"] = jnp.pad(wt[o3:], ((0, SEC_D - (IN_WIDTH - o3)), (0, 0)))
    w["wq"], w["wk"], w["wv"] = wt[:AW], wt[AW:2 * AW], wt[2 * AW:o1]
    w["wz"], w["wxbc"] = wt[o2:o2 + D], wt[o2 + D:o3]
    w["wdt"] = jnp.pad(wt[o3:o3 + SSD_HEADS], ((0, LANES - SSD_HEADS), (0, 0)))
    w["wgate"] = wt[o3 + SSD_HEADS:]
    w["up_a"], w["up_v"] = ops["ffn_w_up"][:D_FF], ops["ffn_w_up"][D_FF:]
    return w


def _layer_small(p, i):
    sm = {n: p[n][i] for n in SMALL_LAYER}
    out = {}
    for n, v in sm.items():
        if n in ("ssd_dt_bias", "ssd_a_log", "ssd_d"):
            out[n] = _pad_lanes(v)
        elif v.ndim == 1:
            out[n] = v.reshape(1, -1)
        else:
            out[n] = v
    return out


def _local_step(x, target, rel_bias, final_g, layer_full, small, fwd_hooks=None, bwd_hooks=None, after_bwd=None):
    nl = small["ln1_g"].shape[0]
    buckets = [_buckets(d).astype(jnp.int32) for d in DILATIONS]
    bias = [_bias_table(rel_bias, buckets[gi], gi, "bias_table%d" % gi) for gi in range(3)]
    no_hooks = lambda i: (lambda name: None)
    fwd_hooks = fwd_hooks or no_hooks
    bwd_hooks = bwd_hooks or no_hooks
    saved, ws, sms = [], [], []
    h = x
    for i in range(nl):
        w = _layer_weights(layer_full(i))
        sm = _layer_small(small, i)
        h, sv = _layer_fwd(h, w, sm, bias, fwd_hooks(i))
        saved.append(sv)
        ws.append(w)
        sms.append(sm)
    dh, dfinal, loss = _final_loss(h, target, final_g.reshape(1, D))
    gws, gss = [None] * nl, [None] * nl
    dbs = [jnp.zeros((6, WIN, 2 * WIN), F32)] * 3
    for i in reversed(range(nl)):
        dh, gws[i], gss[i], dbs = _layer_bwd(dh, ws[i], sms[i], bias, dbs, saved[i], bwd_hooks(i))
        if after_bwd is not None:
            after_bwd(i, gws[i])
    drel = []
    for gi in range(3):
        onehot = jnp.pad(jax.nn.one_hot(buckets[gi].reshape(-1), REL_BUCKETS, dtype=BF16), ((0, 0), (0, LANES - REL_BUCKETS)))
        drel.append(_mm(dbs[gi].reshape(6, WIN * 2 * WIN), onehot, name="g_relb"))
    return loss, dh, gws, gss, dfinal, jnp.concatenate(drel, axis=0)


WEIGHTS = ("rel_bias", "ln1_g", "w_in", "b_gate", "w_a", "pool_w", "pool_scale", "w_b", "ssd_conv_w", "ssd_conv_b",
           "ssd_dt_bias", "ssd_a_log", "ssd_d", "ssd_norm_w", "w_c", "w_o", "ln2_g", "ffn_w_up", "ffn_conv_w",
           "ffn_conv_b", "ffn_w_down", "final_g")
BIG_NAMES = tuple(n for n, _, _ in BIG)
SHARDED_SMALL = {"ssd_conv_w": XBC // 4, "ffn_conv_w": 2 * D_FF // 4}


def _to_rows(flat):
    n = flat.shape[0]
    rows = -(-n // LANES)
    rows = -(-rows // 8) * 8
    return jnp.pad(flat, (0, rows * LANES - n)).reshape(rows, LANES)


def _flatten(tree, names):
    return jnp.concatenate([tree[n].reshape(-1) for n in names])


def _unflatten(flat, shapes, names):
    out, o = {}, 0
    for n in names:
        k = math.prod(shapes[n])
        out[n] = flat[o:o + k].reshape(shapes[n])
        o += k
    return out


def kernel(x, rel_bias, ln1_g, w_in, b_gate, w_a, pool_w, pool_scale, w_b, ssd_conv_w, ssd_conv_b, ssd_dt_bias, ssd_a_log, ssd_d, ssd_norm_w, w_c, w_o, ln2_g, ffn_w_up, ffn_conv_w, ffn_conv_b, ffn_w_down, final_g, loss_target, m_rel_bias, m_ln1_g, m_w_in, m_b_gate, m_w_a, m_pool_w, m_pool_scale, m_w_b, m_ssd_conv_w, m_ssd_conv_b, m_ssd_dt_bias, m_ssd_a_log, m_ssd_d, m_ssd_norm_w, m_w_c, m_w_o, m_ln2_g, m_ffn_w_up, m_ffn_conv_w, m_ffn_conv_b, m_ffn_w_down, m_final_g, v_rel_bias, v_ln1_g, v_w_in, v_b_gate, v_w_a, v_pool_w, v_pool_scale, v_w_b, v_ssd_conv_w, v_ssd_conv_b, v_ssd_dt_bias, v_ssd_a_log, v_ssd_d, v_ssd_norm_w, v_w_c, v_w_o, v_ln2_g, v_ffn_w_up, v_ffn_conv_w, v_ffn_conv_b, v_ffn_w_down, v_final_g):
    W = dict(rel_bias=rel_bias, ln1_g=ln1_g, w_in=w_in, b_gate=b_gate, w_a=w_a, pool_w=pool_w, pool_scale=pool_scale,
             w_b=w_b, ssd_conv_w=ssd_conv_w, ssd_conv_b=ssd_conv_b, ssd_dt_bias=ssd_dt_bias, ssd_a_log=ssd_a_log,
             ssd_d=ssd_d, ssd_norm_w=ssd_norm_w, w_c=w_c, w_o=w_o, ln2_g=ln2_g, ffn_w_up=ffn_w_up,
             ffn_conv_w=ffn_conv_w, ffn_conv_b=ffn_conv_b, ffn_w_down=ffn_w_down, final_g=final_g)
    M = dict(rel_bias=m_rel_bias, ln1_g=m_ln1_g, w_in=m_w_in, b_gate=m_b_gate, w_a=m_w_a, pool_w=m_pool_w,
             pool_scale=m_pool_scale, w_b=m_w_b, ssd_conv_w=m_ssd_conv_w, ssd_conv_b=m_ssd_conv_b,
             ssd_dt_bias=m_ssd_dt_bias, ssd_a_log=m_ssd_a_log, ssd_d=m_ssd_d, ssd_norm_w=m_ssd_norm_w, w_c=m_w_c,
             w_o=m_w_o, ln2_g=m_ln2_g, ffn_w_up=m_ffn_w_up, ffn_conv_w=m_ffn_conv_w, ffn_conv_b=m_ffn_conv_b,
             ffn_w_down=m_ffn_w_down, final_g=m_final_g)
    V = dict(rel_bias=v_rel_bias, ln1_g=v_ln1_g, w_in=v_w_in, b_gate=v_b_gate, w_a=v_w_a, pool_w=v_pool_w,
             pool_scale=v_pool_scale, w_b=v_w_b, ssd_conv_w=v_ssd_conv_w, ssd_conv_b=v_ssd_conv_b,
             ssd_dt_bias=v_ssd_dt_bias, ssd_a_log=v_ssd_a_log, ssd_d=v_ssd_d, ssd_norm_w=v_ssd_norm_w, w_c=v_w_c,
             w_o=v_w_o, ln2_g=v_ln2_g, ffn_w_up=v_ffn_w_up, ffn_conv_w=v_ffn_conv_w, ffn_conv_b=v_ffn_conv_b,
             ffn_w_down=v_ffn_w_down, final_g=v_final_g)
    nl = ln1_g.shape[0]
    px, py, pc_ = _position()
    chip = 2 * px + py
    cidx = jnp.reshape(pc_, (1,)).astype(jnp.int32)
    chip_idx = jnp.reshape(chip, (1,)).astype(jnp.int32)

    placed = {}
    for n, cs in SHARDED_SMALL.items():
        full = jnp.zeros(W[n].shape[:-1] + (4 * cs,), F32)
        full = lax.dynamic_update_slice(full, W[n], (0, 0, chip * cs))
        placed[n] = jnp.where(pc_ == 0, full, 0.0)
    names_sh = tuple(SHARDED_SMALL)
    shapes_sh = {n: placed[n].shape for n in names_sh}
    got = _all_reduce_small(_to_rows(_flatten(placed, names_sh)), "gather_small")
    small = {n: W[n] for n in SMALL_LAYER}
    small.update(_unflatten(got.reshape(-1), shapes_sh, names_sh))

    packs = _pack_blocks({n: W[n] for n in BIG_NAMES}, BF16)

    half = PACK_PAD // 2
    units = half // 16

    def share(weights, total):
        tot = sum(weights.values())
        return {n: math.ceil(total * v / tot) for n, v in weights.items()}

    gathers = {}

    def gather(i):
        if i not in gathers:
            buf = lax.dynamic_update_slice(lax.empty((4, PACK_PAD, D), BF16), packs[i][None], (chip, 0, 0))
            gathers[i] = _Stream(packs[i], buf, functools.partial(_gather_parts, half), 6, units, "gather_w")
        return gathers[i]

    def layer_full(i):
        return _operands(gather(i).drain())

    fwd_share = share(dict(in_a=89, in_b=26, in_c=57, in_d=66, mm_wo=28, mm_up=120, mm_down=46), units)

    def fwd_hooks(i):
        if i + 1 >= nl:
            return lambda name: None
        return lambda name: gather(i + 1).hook(fwd_share[name]) if name in fwd_share else None

    exchanges = {}
    bwd_share = share(dict(d_f=91, g_down=67, d_u2_a=42, d_u2_v=45, g_up_a=52, g_up_v=52, d_merged=29, g_wo=19,
                           d_u_wgate=48, g_in_wgate=41), units)

    def after_bwd(i, gw):
        g = _pack_operands(gw, BF16)
        recv = _rs_pair_exchange(g, "rs_pair")
        hsum = _rs_add_pair(g, recv, cidx, "rs_add_pair")
        exchanges[i] = (hsum, _Stream(hsum, lax.empty((3, half, D), BF16), _rs_chip_parts, 3, units, "rs_chips"))

    def bwd_hooks(i):
        if i + 1 >= nl:
            return lambda name: None
        return lambda name: exchanges[i + 1][1].hook(bwd_share[name]) if name in bwd_share else None

    loss, dx, gws, gss, dfinal, drel = _local_step(x[0], loss_target[0], rel_bias, final_g, layer_full, small,
                                                   fwd_hooks, bwd_hooks, after_bwd)

    grads = {}
    red = []
    for i in range(nl):
        hsum, stream = exchanges[i]
        r = _rs_add_chips(hsum, stream.drain(), chip_idx, "rs_add_chips")
        other = _rs_swap(r, "rs_swap")
        both = jnp.concatenate([jnp.where(pc_ == 0, r, other), jnp.where(pc_ == 0, other, r)], axis=0)
        red.append(_unpack_blocks(both))
    for n in BIG_NAMES:
        grads[n] = jnp.stack([red[i][n] for i in range(nl)], axis=0)

    sg = {}
    for n in SMALL_LAYER:
        sg[n] = jnp.stack([gss[i][n] for i in range(nl)], axis=0)
    for n in ("ssd_dt_bias", "ssd_a_log", "ssd_d"):
        sg[n] = sg[n][:, 0, :SSD_HEADS]
    sg["rel_bias"] = drel[:, :REL_BUCKETS].T
    sg["final_g"] = dfinal.reshape(D)
    sg["loss"] = loss[0, :1]
    names_sg = tuple(sg)
    shapes_sg = {n: ((nl,) + W[n].shape[1:] if n in SMALL_LAYER and n not in SHARDED_SMALL else
                     (placed[n].shape if n in SHARDED_SMALL else sg[n].shape)) for n in names_sg}
    for n in names_sg:
        sg[n] = sg[n].reshape(shapes_sg[n])
    tot = _all_reduce_small(_to_rows(_flatten(sg, names_sg)), "allreduce_small")
    tot = _unflatten(tot.reshape(-1), shapes_sg, names_sg)
    loss_out = tot.pop("loss").reshape(())
    for n, cs in SHARDED_SMALL.items():
        tot[n] = lax.dynamic_slice(tot[n], (0, 0, chip * cs), tot[n].shape[:-1] + (cs,))
    grads.update(tot)

    delta, new_m, new_v = {}, {}, {}
    for n in BIG_NAMES:
        shp = W[n].shape
        r2 = lambda a: a.reshape(-1, shp[-1])
        dl, m2, v2 = _adamw(r2(W[n]), r2(grads[n]), r2(M[n]), r2(V[n]), "adamw_" + n)
        delta[n], new_m[n], new_v[n] = dl.reshape(shp), m2.reshape(shp), v2.reshape(shp)
    names_s = tuple(n for n in WEIGHTS if n not in BIG_NAMES)
    shapes_s = {n: W[n].shape for n in names_s}
    pk = lambda t: _to_rows(_flatten(t, names_s))
    dl, m2, v2 = _adamw(pk(W), pk(grads), pk(M), pk(V), "adamw_small")
    delta.update(_unflatten(dl.reshape(-1), shapes_s, names_s))
    new_m.update(_unflatten(m2.reshape(-1), shapes_s, names_s))
    new_v.update(_unflatten(v2.reshape(-1), shapes_s, names_s))

    return (loss_out, dx[None], *[grads[n] for n in WEIGHTS], *[delta[n] for n in WEIGHTS],
            *[new_m[n] for n in WEIGHTS], *[new_v[n] for n in WEIGHTS])
```

```python
import functools
import math

import jax
import jax.numpy as jnp
from jax import lax
from jax.experimental import pallas as pl
from jax.experimental.pallas import tpu as pltpu

F32 = jnp.float32
BF16 = jnp.bfloat16
MESH = pl.DeviceIdType.MESH

D = 1024
HD = 64
GW = 384
AW = 3 * GW
WIN = 128
DILATIONS = (1, 4, 16)
REL_BUCKETS = 32
REL_MAX_DISTANCE = 2048
POOL_WINDOWS = (2, 4, 8, 16)
PG = 256
SSD_HEADS = 16
SSD_N = 128
SSD_CHUNK = 128
XBC = 1536
D_FF = 2816
EPS = 1e-6
NEG = -1e30
HALO = 16
LANES = 128

SEC_A = 3 * AW
SEC_B = D
SEC_C = D + XBC
SEC_D = 3328
SEC_A_PAD = 3584
IN_WIDTH = SEC_A + SEC_B + SEC_C + 16 + 3 * D

ADAM_LR = 0.001
ADAM_B1 = 0.9
ADAM_B2 = 0.999
ADAM_EPS = 1e-08
ADAM_WD = 0.01
ADAM_STEP = 10
ADAM_TILE = 256 * 1024
MM_VMEM_BYTES = 40 * 1024 * 1024
MM_MAX_OUT_TILE = 1024 * 1024
HBM_BYTES_PER_US = 2.0e6
STEP_US = 0.35
MXU_WIDTH = 256
MXU_FLOPS_PER_US = 0.65e6


_ANY = pl.BlockSpec(memory_space=pl.ANY)


def _pick(d, cands):
    for t in cands:
        if d % t == 0:
            return t
    return d


def _iota(shape, dim):
    return lax.broadcasted_iota(jnp.int32, shape, dim)


def _dg(a, b, ca, cb):
    return lax.dot_general(a.astype(BF16), b.astype(BF16), (((ca,), (cb,)), ((), ())),
                           preferred_element_type=F32)


@jax.custom_vjp
def _bdot_nn(a, b):
    return _dg(a, b, 1, 0)


def _nn_fwd(a, b):
    return _dg(a, b, 1, 0), (a, b)


def _nn_bwd(res, g):
    a, b = res
    return _dg(g, b, 1, 1), _dg(a, g, 0, 0)


_bdot_nn.defvjp(_nn_fwd, _nn_bwd)


@jax.custom_vjp
def _bdot_nt(a, b):
    return _dg(a, b, 1, 1)


def _nt_fwd(a, b):
    return _dg(a, b, 1, 1), (a, b)


def _nt_bwd(res, g):
    a, b = res
    return _dg(g, b, 1, 0), _dg(g, a, 0, 0)


_bdot_nt.defvjp(_nt_fwd, _nt_bwd)


@jax.custom_vjp
def _bdot_tn(a, b):
    return _dg(a, b, 0, 0)


def _tn_fwd(a, b):
    return _dg(a, b, 0, 0), (a, b)


def _tn_bwd(res, g):
    a, b = res
    return _dg(b, g, 1, 1), _dg(a, g, 1, 0)


_bdot_tn.defvjp(_tn_fwd, _tn_bwd)


def _fdot(a, b):
    return jnp.dot(a, b, preferred_element_type=F32, precision=lax.Precision.HIGHEST)


def _sigmoid(x):
    return 0.5 * jnp.tanh(0.5 * x) + 0.5


def _silu(x):
    return x * _sigmoid(x)


def _softplus(x):
    return jnp.maximum(x, 0.0) + jnp.log(1.0 + jnp.exp(-jnp.abs(x)))


def _lane_pick(m, h):
    return jnp.sum(jnp.where(_iota(m.shape, 1) == h, m, 0.0), axis=1, keepdims=True)


def _row_pick(m, h):
    return jnp.sum(jnp.where(_iota(m.shape, 0) == h, m, 0.0), axis=0, keepdims=True)


def _stack_rows(rows, n):
    c = rows[0].shape[1]
    r = _iota((n, c), 0)
    out = jnp.zeros((n, c), F32)
    for k, v in enumerate(rows):
        out = out + jnp.where(r == k, v, 0.0)
    return out


def _mm(a, b, *, ta=False, tb=False, add=None, out_dtype=F32, name, hook=None):
    if ta:
        K, M = a.shape
    else:
        M, K = a.shape
    if tb:
        N, Kb = b.shape
    else:
        Kb, N = b.shape
    assert K == Kb, (a.shape, b.shape, ta, tb)
    tm, tn, tk = _mm_tiles(M, N, K, a.dtype.itemsize, b.dtype.itemsize, jnp.dtype(out_dtype).itemsize,
                           0 if add is None else add.dtype.itemsize)
    ni, nj, nk = M // tm, N // tn, K // tk
    ca = 0 if ta else 1
    cb = 1 if tb else 0
    n_in = 2 if add is None else 3
    n_hin = 0 if hook is None else len(hook.inputs)
    n_hout = 0 if hook is None else len(hook.out_shapes)

    def body(*refs):
        a_ref, b_ref = refs[:2]
        add_ref = None if add is None else refs[2]
        o_ref = refs[n_in + n_hin]
        scr = refs[n_in + n_hin + 1 + n_hout:]
        acc_ref = scr[0] if nk > 1 else None
        hargs = (refs[n_in:n_in + n_hin], refs[n_in + n_hin + 1:n_in + n_hin + 1 + n_hout], scr[1 if nk > 1 else 0:])
        i, j, k = pl.program_id(0), pl.program_id(1), pl.program_id(2)
        if hook is not None:
            @pl.when((i == 0) & (j == 0) & (k == 0))
            def _():
                hook.start(*hargs)

        part = _dg(a_ref[...], b_ref[...], ca, cb)

        def finish(r):
            if add_ref is not None:
                r = r + add_ref[...].astype(F32)
            o_ref[...] = r.astype(o_ref.dtype)

        if nk == 1:
            finish(part)
        else:
            @pl.when(k == 0)
            def _():
                acc_ref[...] = part

            @pl.when((k > 0) & (k < nk - 1))
            def _():
                acc_ref[...] += part

            @pl.when(k == nk - 1)
            def _():
                finish(acc_ref[...] + part)

        if hook is not None:
            @pl.when((i == ni - 1) & (j == nj - 1) & (k == nk - 1))
            def _():
                hook.finish(*hargs)

    a_spec = pl.BlockSpec((tk, tm), lambda i, j, k: (k, i)) if ta else pl.BlockSpec((tm, tk), lambda i, j, k: (i, k))
    b_spec = pl.BlockSpec((tn, tk), lambda i, j, k: (j, k)) if tb else pl.BlockSpec((tk, tn), lambda i, j, k: (k, j))
    in_specs = [a_spec, b_spec]
    args = [a, b]
    if add is not None:
        in_specs.append(pl.BlockSpec((tm, tn), lambda i, j, k: (i, j)))
        args.append(add)
    out_specs = [pl.BlockSpec((tm, tn), lambda i, j, k: (i, j))]
    out_shape = [jax.ShapeDtypeStruct((M, N), out_dtype)]
    scratch = [pltpu.VMEM((tm, tn), F32)] if nk > 1 else []
    aliases = {}
    if hook is not None:
        in_specs += [_ANY] * n_hin
        args += list(hook.inputs)
        out_specs += [_ANY] * n_hout
        out_shape += list(hook.out_shapes)
        scratch += list(hook.scratch)
        aliases = {n_in + hi: 1 + ho for hi, ho in hook.aliases.items()}
    sem = ("parallel", "parallel", "arbitrary") if hook is None else ("arbitrary",) * 3
    res = pl.pallas_call(
        body, name=name, grid=(ni, nj, nk), in_specs=in_specs, out_specs=out_specs, out_shape=out_shape,
        scratch_shapes=scratch, input_output_aliases=aliases,
        compiler_params=pltpu.CompilerParams(dimension_semantics=sem),
    )(*args)
    if hook is not None:
        hook.done(res[1:])
    return res[0]


def _mm_tiles(M, N, K, sa, sb, so, sadd):
    def tiles(d):
        return [t for t in range(LANES, min(d, 2048) + 1, LANES) if d % t == 0] or [d]

    best = None
    for tk in [K] + [t for t in tiles(K) if t < K]:
        for tm in tiles(M):
            for tn in tiles(N):
                vmem = 2 * (tm * tk * sa + tk * tn * sb + tm * tn * (so + sadd)) + (tm * tn * 4 if tk < K else 0)
                if vmem > MM_VMEM_BYTES or tm * tn > MM_MAX_OUT_TILE:
                    continue
                a_reads = 1 if tk == K else N // tn
                traffic = M * K * sa * a_reads + K * N * sb * (M // tm) + M * N * (so + sadd)
                steps = (M // tm) * (N // tn) * (K // tk)
                width = -(-tn // MXU_WIDTH) * MXU_WIDTH
                mxu = 2.0 * M * K * N * (width / tn) / MXU_FLOPS_PER_US
                edge = tm * tk * sa + tk * tn * sb + tm * tn * (so + sadd)
                cost = max(traffic / HBM_BYTES_PER_US, mxu) + steps * STEP_US + edge / HBM_BYTES_PER_US
                if best is None or cost < best[0]:
                    best = (cost, tm, tn, tk)
    assert best is not None, (M, N, K)
    return best[1:]


class _Hook:
    def __init__(self, inputs, out_shapes, aliases, scratch, start, finish, done):
        self.inputs, self.out_shapes, self.aliases, self.scratch = inputs, out_shapes, aliases, scratch
        self.start, self.finish, self.done = start, finish, done


def _rows(name, fn, ins, outs, accs=(), *, tm, nrows, ncol=1):
    nt = nrows // tm
    hb = tm // HALO
    nh = nrows // HALO
    in_specs, args = [], []
    for kind, arr, cw, base in ins:
        if kind == "row":
            cw = arr.shape[1] if cw is None else cw
            in_specs.append(pl.BlockSpec((tm, cw), lambda j, i, base=base: (i, base + j)))
        elif kind == "prev":
            in_specs.append(pl.BlockSpec((HALO, cw), lambda j, i, base=base: (jnp.maximum(i * hb - 1, 0), base + j)))
        elif kind == "next":
            in_specs.append(pl.BlockSpec((HALO, cw), lambda j, i, base=base: (jnp.minimum((i + 1) * hb, nh - 1), base + j)))
        elif kind == "const":
            in_specs.append(pl.BlockSpec(arr.shape, lambda j, i, nd=arr.ndim: (0,) * nd))
        elif kind == "ccol":
            in_specs.append(pl.BlockSpec((arr.shape[0], cw), lambda j, i, base=base: (0, base + j)))
        else:
            raise ValueError(kind)
        args.append(arr)
    out_specs, out_shape = [], []
    for ctot, cw, base, dt in outs:
        out_specs.append(pl.BlockSpec((tm, cw), lambda j, i, base=base: (i, base + j)))
        out_shape.append(jax.ShapeDtypeStruct((nrows, ctot), dt))
    for r, ctot, cw in accs:
        out_specs.append(pl.BlockSpec((r, cw), lambda j, i: (0, j)))
        out_shape.append(jax.ShapeDtypeStruct((r, ctot), F32))
    n_in, n_out = len(ins), len(outs)

    def body(*refs):
        j = pl.program_id(0)
        i = pl.program_id(1)
        vals = [r[...] for r in refs[:n_in]]
        res = fn(i, j, *[v.astype(F32) if v.dtype == BF16 else v for v in vals])
        for r, v in zip(refs[n_in:n_in + n_out], res[:n_out]):
            r[...] = v.astype(r.dtype)
        for r, v in zip(refs[n_in + n_out:], res[n_out:]):
            @pl.when(i == 0)
            def _(r=r, v=v):
                r[...] = v

            @pl.when(i > 0)
            def _(r=r, v=v):
                r[...] += v

    res = pl.pallas_call(
        body, name=name, grid=(ncol, nt), in_specs=in_specs, out_specs=out_specs, out_shape=out_shape,
        compiler_params=pltpu.CompilerParams(dimension_semantics=("arbitrary", "arbitrary")),
    )(*args)
    return res


def _shift_down(xcat, k):
    return xcat if k == 0 else pltpu.roll(xcat, k, 0)


def _shift_up(xcat, k):
    return xcat if k == 0 else pltpu.roll(xcat, xcat.shape[0] - k, 0)


def _with_prev(i, halo, x):
    return jnp.concatenate([jnp.where(i == 0, 0.0, halo), x], axis=0)


def _with_next(i, nt, x, halo):
    return jnp.concatenate([x, jnp.where(i == nt - 1, 0.0, halo)], axis=0)


def _rms_core(x, g):
    r = lax.rsqrt(jnp.mean(x * x, axis=-1, keepdims=True) + EPS)
    return x * r * g


def _rms_fwd(x, g, name):
    S = x.shape[0]
    return _rows(name, lambda i, j, xv, gv: [_rms_core(xv, gv)],
                 [("row", x, None, 0), ("const", g, None, 0)], [(D, D, 0, BF16)], tm=256, nrows=S)[0]


def _rms_bwd(x, g, du, dres, name):
    S = x.shape[0]

    def fn(i, j, xv, gv, duv, drv):
        _, vjp = jax.vjp(_rms_core, xv, gv)
        dx, dg = vjp(duv)
        return [drv + dx, dg]

    return _rows(name, fn, [("row", x, None, 0), ("const", g, None, 0), ("row", du, None, 0), ("row", dres, None, 0)],
                 [(D, D, 0, F32)], [(1, D, D)], tm=256, nrows=S)


def _final_loss(x, target, g):
    S = x.shape[0]

    def fn(i, j, xv, tv, gv):
        def f(xx, gg):
            err = _rms_core(xx, gg) - tv
            return 0.5 * jnp.sum(err * err) / D

        loss, vjp = jax.vjp(f, xv, gv)
        dx, dg = vjp(jnp.ones((), F32))
        return [dx, dg, jnp.zeros((1, LANES), F32) + loss]

    return _rows("final_loss", fn, [("row", x, None, 0), ("row", target, None, 0), ("const", g, None, 0)],
                 [(D, D, 0, F32)], [(1, D, D), (1, LANES, LANES)], tm=256, nrows=S)


def _attn_valid(n):
    qi = _iota((WIN, 2 * WIN), 0)
    kk = _iota((WIN, 2 * WIN), 1)
    rel = qi + WIN - kk
    return (rel >= 0) & (rel <= WIN) & ((kk >= WIN) | (n > 0))


def _attn_block(q, kp, kc, vp, vc, b0, b1, valid):
    k = jnp.concatenate([kp, kc], axis=0)
    v = jnp.concatenate([vp, vc], axis=0)
    lo = _iota((WIN, LANES), 1) < HD
    scale = 1.0 / math.sqrt(HD)
    os_, ls_ = [], []
    for hh, b in ((0, b0), (1, b1)):
        qm = jnp.where(lo if hh == 0 else ~lo, q, 0.0)
        s = _bdot_nt(qm, k) * scale + b
        s = jnp.where(valid, s, NEG)
        m = lax.stop_gradient(jnp.max(s, axis=1, keepdims=True))
        p = jnp.exp(s - m)
        l = jnp.sum(p, axis=1, keepdims=True)
        os_.append(_bdot_nn(p, v) / l)
        ls_.append(m + jnp.log(l))
    return jnp.where(lo, os_[0], os_[1]), jnp.where(lo, ls_[0], ls_[1])


def _residue_rows(r, d):
    return pl.ds(0, WIN) if d == 1 else pl.ds(r, WIN, stride=d)


def _for_residues(d, fn):
    if d == 1:
        fn(0, 0)
    else:
        lax.fori_loop(0, d, fn, 0, unroll=4)


def _pairs_per_step(d):
    return 3 if d == 1 else 1


def _bias_table(rel_bias, bucket, gi, name):
    def body(t_ref, b_ref, o_ref):
        h = 6 * gi + pl.program_id(0)
        b = b_ref[...]
        acc = jnp.zeros(b.shape, F32)
        for k in range(REL_BUCKETS):
            acc = jnp.where(b == k, t_ref[k, h], acc)
        o_ref[0] = acc

    return pl.pallas_call(
        body, name=name, grid=(6,),
        in_specs=[pl.BlockSpec(memory_space=pltpu.SMEM), pl.BlockSpec((WIN, 2 * WIN), lambda h: (0, 0))],
        out_specs=pl.BlockSpec((1, WIN, 2 * WIN), lambda h: (h, 0, 0)),
        out_shape=jax.ShapeDtypeStruct((6, WIN, 2 * WIN), F32),
    )(rel_bias, bucket)


def _attn_fwd(pa, bias, gi, name):
    S = pa.shape[0]
    d = DILATIONS[gi]
    bt = WIN * d
    nb = S // bt
    hpw = _pairs_per_step(d)
    bw = hpw * LANES
    cb = 3 * gi // hpw

    def body(q_ref, kp_ref, kc_ref, vp_ref, vc_ref, b_ref, o_ref, l_ref):
        valid = _attn_valid(pl.program_id(1))

        def residue(r, carry):
            sl = _residue_rows(r, d)
            for t in range(hpw):
                ln = pl.ds(t * LANES, LANES)
                o, lse = _attn_block(q_ref[sl, ln], kp_ref[sl, ln], kc_ref[sl, ln], vp_ref[sl, ln], vc_ref[sl, ln],
                                     b_ref[2 * t], b_ref[2 * t + 1], valid)
                o_ref[sl, ln] = o
                l_ref[sl, ln] = lse
            return carry

        _for_residues(d, residue)

    def spec(off, prev):
        if prev:
            return pl.BlockSpec((bt, bw), lambda hp, n: (jnp.maximum(n - 1, 0), off // hpw + cb + hp))
        return pl.BlockSpec((bt, bw), lambda hp, n: (n, off // hpw + cb + hp))

    ospec = pl.BlockSpec((bt, bw), lambda hp, n: (n, hp))
    return pl.pallas_call(
        body, name=name, grid=(3 // hpw, nb),
        in_specs=[spec(0, False), spec(9, True), spec(9, False), spec(18, True), spec(18, False),
                  pl.BlockSpec((2 * hpw, WIN, 2 * WIN), lambda hp, n: (hp, 0, 0))],
        out_specs=[ospec, ospec],
        out_shape=[jax.ShapeDtypeStruct((S, GW), F32)] * 2,
        compiler_params=pltpu.CompilerParams(dimension_semantics=("parallel", "arbitrary")),
    )(pa, pa, pa, pa, pa, bias)


def _attn_bwd(pa, bias, do, dlse, db_in, dqkv, gi, name):
    S = pa.shape[0]
    d = DILATIONS[gi]
    bt = WIN * d
    nb = S // bt
    hpw = _pairs_per_step(d)
    bw = hpw * LANES
    cb = 3 * gi // hpw

    def body(q_ref, kp_ref, kc_ref, vp_ref, vc_ref, b_ref, do_ref, dl_ref, dbi_ref, dqi_ref, dki_ref, dvi_ref,
             dq_ref, dk_ref, dv_ref, db_ref, ck, cv):
        n = pl.program_id(1)

        @pl.when(n == 0)
        def _():
            db_ref[...] = dbi_ref[...]
            ck[...] = jnp.zeros_like(ck)
            cv[...] = jnp.zeros_like(cv)

        @pl.when(n < nb)
        def _():
            f = functools.partial(_attn_block, valid=_attn_valid(n))

            def residue(r, carry):
                sl = _residue_rows(r, d)
                cs = pl.ds(pl.multiple_of(r * WIN, WIN), WIN)
                for t in range(hpw):
                    ln = pl.ds(t * LANES, LANES)
                    _, vjp = jax.vjp(f, q_ref[sl, ln], kp_ref[sl, ln], kc_ref[sl, ln], vp_ref[sl, ln], vc_ref[sl, ln],
                                     b_ref[2 * t], b_ref[2 * t + 1])
                    dq, dkp, dkc, dvp, dvc, db0, db1 = vjp((do_ref[sl, ln], dl_ref[sl, ln]))
                    dq_ref[sl, ln] = dq
                    dk_ref[sl, ln] = ck[cs, ln] + dkp
                    dv_ref[sl, ln] = cv[cs, ln] + dvp
                    ck[cs, ln] = dkc
                    cv[cs, ln] = dvc
                    db_ref[2 * t] += db0
                    db_ref[2 * t + 1] += db1
                return carry

            _for_residues(d, residue)

        @pl.when(n == nb)
        def _():
            def residue(r, carry):
                sl = _residue_rows(r, d)
                cs = pl.ds(pl.multiple_of(r * WIN, WIN), WIN)
                dk_ref[sl, :] = ck[cs, :]
                dv_ref[sl, :] = cv[cs, :]
                return carry

            _for_residues(d, residue)

    def cur(n):
        return jnp.minimum(n, nb - 1)

    def spec(off, prev):
        if prev:
            return pl.BlockSpec((bt, bw), lambda hp, n: (jnp.maximum(cur(n) - 1, 0), off // hpw + cb + hp))
        return pl.BlockSpec((bt, bw), lambda hp, n: (cur(n), off // hpw + cb + hp))

    gspec = pl.BlockSpec((bt, bw), lambda hp, n: (cur(n), hp))
    bspec = pl.BlockSpec((2 * hpw, WIN, 2 * WIN), lambda hp, n: (hp, 0, 0))
    qspec = pl.BlockSpec((bt, bw), lambda hp, n: (cur(n), cb + hp))
    kspec = pl.BlockSpec((bt, bw), lambda hp, n: (jnp.maximum(n - 1, 0), cb + hp))
    dq, dk, dv, db = pl.pallas_call(
        body, name=name, grid=(3 // hpw, nb + 1),
        in_specs=[spec(0, False), spec(9, True), spec(9, False), spec(18, True), spec(18, False),
                  bspec, gspec, gspec, bspec, _ANY, _ANY, _ANY],
        out_specs=[qspec, kspec, kspec, bspec],
        out_shape=[jax.ShapeDtypeStruct((S, AW), F32)] * 3 + [jax.ShapeDtypeStruct((6, WIN, 2 * WIN), F32)],
        scratch_shapes=[pltpu.VMEM((bt, bw), F32), pltpu.VMEM((bt, bw), F32)],
        input_output_aliases={9: 0, 10: 1, 11: 2},
        compiler_params=pltpu.CompilerParams(dimension_semantics=("arbitrary", "arbitrary")),
    )(pa, pa, pa, pa, pa, bias, do, dlse, db_in, *dqkv)
    return (dq, dk, dv), db


def _mix_core(o0, o1, o2, l0, l1, l2):
    m = lax.stop_gradient(jnp.maximum(jnp.maximum(l0, l1), l2))
    e0, e1, e2 = jnp.exp(l0 - m), jnp.exp(l1 - m), jnp.exp(l2 - m)
    return (e0 * o0 + e1 * o1 + e2 * o2) / (e0 + e1 + e2)


def _mix_fwd(os_, ls_, name):
    S = os_[0].shape[0]
    ins = [("row", a, None, 0) for a in (*os_, *ls_)]
    return _rows(name, lambda i, j, *v: [_mix_core(*v)], ins, [(GW, GW, 0, BF16)], tm=256, nrows=S)[0]


def _mix_bwd(os_, ls_, datt, name):
    S = datt.shape[0]

    def fn(i, j, *v):
        _, vjp = jax.vjp(_mix_core, *v[:6])
        return list(vjp(v[6]))

    ins = [("row", a, None, 0) for a in (*os_, *ls_, datt)]
    outs = [(GW, GW, 0, F32)] * 6
    r = _rows(name, fn, ins, outs, tm=256, nrows=S)
    return r[:3], r[3:]


def _t5_bucket(dist):
    max_exact = REL_BUCKETS // 2
    is_small = dist < max_exact
    nf = jnp.maximum(dist, 1).astype(F32)
    large = max_exact + (jnp.log(nf / max_exact) / math.log(REL_MAX_DISTANCE / max_exact)
                         * (REL_BUCKETS - max_exact)).astype(jnp.int32)
    large = jnp.minimum(large, REL_BUCKETS - 1)
    return jnp.where(is_small, dist, large)


def _buckets(d):
    qi = jnp.arange(WIN)[:, None]
    kk = jnp.arange(2 * WIN)[None, :]
    rel = qi + WIN - kk
    return _t5_bucket(jnp.clip(rel, 0, None) * d)


def _pool_cnt(i, tm, w):
    pos = i * tm + _iota((tm, PG), 0) + 1
    return jnp.minimum(pos, w).astype(F32)


def _pool_d(i, tm, halo, u):
    ds = []
    for g, w in enumerate(POOL_WINDOWS):
        ug = u[:, g * PG:(g + 1) * PG]
        s = _with_prev(i, halo[:, g * PG:(g + 1) * PG], ug)
        step = 1
        while step < w:
            s = s + _shift_down(s, step)
            step *= 2
        ds.append(s[HALO:] / _pool_cnt(i, tm, w) - ug)
    return ds


def _pool_lin(d0, d1, d2, d3, w0, w1, w2, w3, scale):
    y = jnp.concatenate([_bdot_nn(d0, w0), _bdot_nn(d1, w1), _bdot_nn(d2, w2), _bdot_nn(d3, w3)], axis=1)
    return y * scale


def _pool_fwd(pb, pw, scale, name):
    S = pb.shape[0]
    tm = 256

    def fn(i, j, halo, u, w, sc):
        ds = _pool_d(i, tm, halo, u)
        return [_pool_lin(*ds, *[w[k].astype(F32) for k in range(4)], sc)]

    return _rows(name, fn, [("prev", pb, D, 0), ("row", pb, None, 0), ("const", pw, None, 0), ("const", scale, None, 0)],
                 [(D, D, 0, BF16)], tm=tm, nrows=S)[0]


def _pool_bwd(pb, pw, scale, dpo, name):
    S = pb.shape[0]
    tm = 256
    nt = S // tm

    def fn1(i, j, halo, u, w, sc, dy):
        ds = _pool_d(i, tm, halo, u)
        _, vjp = jax.vjp(_pool_lin, *ds, *[w[k].astype(F32) for k in range(4)], sc)
        g = vjp(dy)
        e = jnp.concatenate([g[k] / _pool_cnt(i, tm, wd) for k, wd in enumerate(POOL_WINDOWS)], axis=1)
        return [e, jnp.concatenate(g[4:8], axis=0), g[8]]

    e, dpw, dsc = _rows(name + "_a", fn1,
                        [("prev", pb, D, 0), ("row", pb, None, 0), ("const", pw, None, 0), ("const", scale, None, 0),
                         ("row", dpo, None, 0)],
                        [(D, D, 0, F32)], [(4 * PG, PG, PG), (1, D, D)], tm=tm, nrows=S)

    def fn2(i, j, ev, halo):
        outs = []
        for g, w in enumerate(POOL_WINDOWS):
            eg = ev[:, g * PG:(g + 1) * PG]
            s = _with_next(i, nt, eg, halo[:, g * PG:(g + 1) * PG])
            step = 1
            while step < w:
                s = s + _shift_up(s, step)
                step *= 2
            outs.append(s[:tm] - eg * _pool_cnt(i, tm, w))
        return [jnp.concatenate(outs, axis=1)]

    du = _rows(name + "_b", fn2, [("row", e, None, 0), ("next", e, D, 0)], [(D, D, 0, BF16)], tm=tm, nrows=S)[0]
    return du, dpw, dsc


def _conv_taps(i, halo, x, K):
    cat = _with_prev(i, halo, x)
    return [_shift_down(cat, K - 1 - k)[HALO:] for k in range(K)]


def _conv_pre(taps, w, b):
    acc = b
    for k, t in enumerate(taps):
        acc = acc + t * _row_pick(w, k)
    return acc


CW = 256
CWS = 512
CONV_TM = 512


def _ext_taps(i, nt, prev, x, nxt, K):
    cat = jnp.concatenate([jnp.where(i == 0, 0.0, prev), x, jnp.where(i == nt - 1, 0.0, nxt)], axis=0)
    return [_shift_down(cat, K - 1 - k)[HALO:] for k in range(K)]


def _conv_t_rows(dp, w, K, tm):
    acc = jnp.zeros((tm, dp.shape[1]), F32)
    for k in range(K):
        acc = acc + _shift_up(dp, K - 1 - k)[:tm] * _row_pick(w, k)
    return acc


def _ssd_conv_fwd(pc, w, b, name):
    S = pc.shape[0]
    base = D // CWS

    def fn(i, j, halo, x, wv, bv):
        return [_silu(_conv_pre(_conv_taps(i, halo, x, 4), wv, bv))]

    return _rows(name, fn, [("prev", pc, CWS, base), ("row", pc, CWS, base), ("ccol", w, CWS, 0), ("ccol", b, CWS, 0)],
                 [(XBC, CWS, 0, F32)], tm=CONV_TM, nrows=S, ncol=XBC // CWS)[0]


def _ssd_conv_bwd(pc, w, b, dy, name):
    S = pc.shape[0]
    base = D // CWS
    tm = CONV_TM
    nt = S // tm

    def fn(i, j, prev, x, nxt, wv, bv, dyv, dyn):
        taps = _ext_taps(i, nt, prev, x, nxt, 4)
        pre = _conv_pre(taps, wv, bv)
        sg = _sigmoid(pre)
        dye = jnp.concatenate([dyv, jnp.where(i == nt - 1, 0.0, dyn)], axis=0)
        dpre = dye * sg * (1.0 + pre * (1.0 - sg))
        dw = _stack_rows([jnp.sum(dpre[:tm] * t[:tm], axis=0, keepdims=True) for t in taps], 4)
        return [_conv_t_rows(dpre, wv, 4, tm), dw, jnp.sum(dpre[:tm], axis=0, keepdims=True)]

    return _rows(name, fn,
                 [("prev", pc, CWS, base), ("row", pc, CWS, base), ("next", pc, CWS, base), ("ccol", w, CWS, 0),
                  ("ccol", b, CWS, 0), ("row", dy, CWS, 0), ("next", dy, CWS, 0)],
                 [(XBC, CWS, 0, BF16)], [(4, XBC, CWS), (1, XBC, CWS)], tm=tm, nrows=S, ncol=XBC // CWS)


NFC = D_FF // CW


def _ffn_act_fwd(h, w, b, name):
    S = h.shape[0]

    def fn(i, j, ha, a, hv, v, wa, wv, ba, bv):
        pa = _conv_pre(_conv_taps(i, ha, a, 3), wa, ba)
        pv = _conv_pre(_conv_taps(i, hv, v, 3), wv, bv)
        return [_silu(pa) * pv]

    return _rows(name, fn,
                 [("prev", h, CW, 0), ("row", h, CW, 0), ("prev", h, CW, NFC), ("row", h, CW, NFC),
                  ("ccol", w, CW, 0), ("ccol", w, CW, NFC), ("ccol", b, CW, 0), ("ccol", b, CW, NFC)],
                 [(D_FF, CW, 0, BF16)], tm=CONV_TM, nrows=S, ncol=NFC)[0]


def _ffn_act_bwd(h, w, b, df, name):
    S = h.shape[0]
    tm = CONV_TM
    nt = S // tm

    def fn(i, j, pa_, a, na, pv_, v, nv, wa, wv, ba, bv, dfv, dfn):
        ta = _ext_taps(i, nt, pa_, a, na, 3)
        tv = _ext_taps(i, nt, pv_, v, nv, 3)
        pa = _conv_pre(ta, wa, ba)
        pv = _conv_pre(tv, wv, bv)
        sg = _sigmoid(pa)
        dfe = jnp.concatenate([dfv.astype(F32), jnp.where(i == nt - 1, 0.0, dfn.astype(F32))], axis=0)
        dpa = dfe * pv * sg * (1.0 + pa * (1.0 - sg))
        dpv = dfe * pa * sg
        res = [_conv_t_rows(dpa, wa, 3, tm), _conv_t_rows(dpv, wv, 3, tm)]
        for dp, taps in ((dpa, ta), (dpv, tv)):
            res.append(_stack_rows([jnp.sum(dp[:tm] * t[:tm], axis=0, keepdims=True) for t in taps], 3))
        for dp in (dpa, dpv):
            res.append(jnp.sum(dp[:tm], axis=0, keepdims=True))
        return res

    ins = []
    for base in (0, NFC):
        ins += [("prev", h, CW, base), ("row", h, CW, base), ("next", h, CW, base)]
    ins += [("ccol", w, CW, 0), ("ccol", w, CW, NFC), ("ccol", b, CW, 0), ("ccol", b, CW, NFC),
            ("row", df, CW, 0), ("next", df, CW, 0)]
    dha, dhv, dwa, dwv, dba, dbv = _rows(
        name, fn, ins, [(D_FF, CW, 0, BF16)] * 2, [(3, D_FF, CW)] * 2 + [(1, D_FF, CW)] * 2, tm=tm, nrows=S, ncol=NFC)
    return dha, dhv, jnp.concatenate([dwa, dwv], axis=1), jnp.concatenate([dba, dbv], axis=1)


NSLAB = D // LANES
CPS = 2


def _ssd_chunk(xs, Bs, Cs, dtraw, dtb, alog, prev):
    lsz = SSD_CHUNK
    lane = _iota((lsz, LANES), 1)
    row = _iota((lsz, LANES), 0)
    dt = jnp.where(lane < SSD_HEADS, _softplus(dtraw + dtb), 0.0)
    a = dt * (-jnp.exp(alog))
    tril = row >= lane
    a_cs = _fdot(tril.astype(F32), a)
    a_cst = a_cs.T
    a_last = jnp.sum(a, axis=0, keepdims=True)
    lo = lane < HD
    top = row < HD
    cbs = [_bdot_nt(Cs[g], Bs[g]) for g in range(2)]
    ys, news = [], []
    for s in range(NSLAB):
        g = s // (NSLAB // 2)
        cols, lms, dts, als = [], [], [], []
        for hh in range(2):
            h = 2 * s + hh
            col = _lane_pick(a_cs, h)
            seg = col - _row_pick(a_cst, h)
            lms.append(jnp.exp(jnp.where(tril, seg, NEG)))
            cols.append(col)
            dts.append(_lane_pick(dt, h))
            als.append(_lane_pick(a_last, h))
        col_x = jnp.where(lo, cols[0], cols[1])
        al_x = jnp.where(lo, als[0], als[1])
        xc = xs[s] * jnp.where(lo, dts[0], dts[1])
        yd = jnp.where(lo, _bdot_nn(cbs[g] * lms[0], xc), _bdot_nn(cbs[g] * lms[1], xc))
        yoff = _bdot_nt(Cs[g], prev[s]) * jnp.exp(col_x)
        ys.append(yd + yoff)
        st = _bdot_tn(xc * jnp.exp(al_x - col_x), Bs[g])
        news.append(prev[s] * jnp.exp(jnp.where(top, als[0], als[1])) + st)
    return ys, news


def _ssd_scan_fwd(xbc_c, pd, dtb, alog, name):
    S = xbc_c.shape[0]
    nc = S // SSD_CHUNK
    rows_ = CPS * SSD_CHUNK

    def body(x_ref, b_ref, c_ref, dt_ref, dtb_ref, al_ref, y_ref, st_ref, state):
        c = pl.program_id(0)

        @pl.when(c == 0)
        def _():
            state[...] = jnp.zeros_like(state)

        prev = [state[s * LANES:(s + 1) * LANES, :] for s in range(NSLAB)]
        for u in range(CPS):
            rw = pl.ds(u * SSD_CHUNK, SSD_CHUNK)
            xs = [x_ref[rw, s * LANES:(s + 1) * LANES] for s in range(NSLAB)]
            Bs = [b_ref[rw, g * SSD_N:(g + 1) * SSD_N] for g in range(2)]
            Cs = [c_ref[rw, g * SSD_N:(g + 1) * SSD_N] for g in range(2)]
            for s in range(NSLAB):
                st_ref[u, s * LANES:(s + 1) * LANES, :] = prev[s]
            ys, prev = _ssd_chunk(xs, Bs, Cs, dt_ref[rw, :].astype(F32), dtb_ref[...], al_ref[...], prev)
            for s in range(NSLAB):
                y_ref[rw, s * LANES:(s + 1) * LANES] = ys[s]
        for s in range(NSLAB):
            state[s * LANES:(s + 1) * LANES, :] = prev[s]

    return pl.pallas_call(
        body, name=name, grid=(nc // CPS,),
        in_specs=[pl.BlockSpec((rows_, D), lambda c: (c, 0)),
                  pl.BlockSpec((rows_, 2 * SSD_N), lambda c: (c, D // (2 * SSD_N))),
                  pl.BlockSpec((rows_, 2 * SSD_N), lambda c: (c, D // (2 * SSD_N) + 1)),
                  pl.BlockSpec((rows_, LANES), lambda c: (c, 0)),
                  pl.BlockSpec((1, LANES), lambda c: (0, 0)), pl.BlockSpec((1, LANES), lambda c: (0, 0))],
        out_specs=[pl.BlockSpec((rows_, D), lambda c: (c, 0)), pl.BlockSpec((CPS, D, SSD_N), lambda c: (c, 0, 0))],
        out_shape=[jax.ShapeDtypeStruct((S, D), F32), jax.ShapeDtypeStruct((nc, D, SSD_N), F32)],
        scratch_shapes=[pltpu.VMEM((D, SSD_N), F32)],
        compiler_params=pltpu.CompilerParams(dimension_semantics=("arbitrary",)),
    )(xbc_c, xbc_c, xbc_c, pd, dtb, alog)


def _ssd_scan_bwd(xbc_c, pd, dtb, alog, states, dy, dxs_skip, name):
    S = xbc_c.shape[0]
    nc = S // SSD_CHUNK
    rows_ = CPS * SSD_CHUNK

    def body(x_ref, b_ref, c_ref, dt_ref, dtb_ref, al_ref, st_ref, dy_ref, sk_ref,
             dx_ref, ddt_ref, ddtb_ref, dal_ref, dstate):
        c = pl.program_id(0)

        @pl.when(c == 0)
        def _():
            dstate[...] = jnp.zeros_like(dstate)
            ddtb_ref[...] = jnp.zeros_like(ddtb_ref)
            dal_ref[...] = jnp.zeros_like(dal_ref)

        dnew = [dstate[s * LANES:(s + 1) * LANES, :] for s in range(NSLAB)]
        for u in reversed(range(CPS)):
            rw = pl.ds(u * SSD_CHUNK, SSD_CHUNK)
            xs = [x_ref[rw, s * LANES:(s + 1) * LANES] for s in range(NSLAB)]
            Bs = [b_ref[rw, g * SSD_N:(g + 1) * SSD_N] for g in range(2)]
            Cs = [c_ref[rw, g * SSD_N:(g + 1) * SSD_N] for g in range(2)]
            prev = [st_ref[u, s * LANES:(s + 1) * LANES, :] for s in range(NSLAB)]
            _, vjp = jax.vjp(_ssd_chunk, xs, Bs, Cs, dt_ref[rw, :].astype(F32), dtb_ref[...], al_ref[...], prev)
            dys = [dy_ref[rw, s * LANES:(s + 1) * LANES] for s in range(NSLAB)]
            dxs, dBs, dCs, ddt, ddtb, dal, dnew = vjp((dys, dnew))
            for s in range(NSLAB):
                dx_ref[rw, s * LANES:(s + 1) * LANES] = dxs[s] + sk_ref[rw, s * LANES:(s + 1) * LANES]
            for g in range(2):
                dx_ref[rw, D + g * SSD_N:D + (g + 1) * SSD_N] = dBs[g]
                dx_ref[rw, D + 2 * SSD_N + g * SSD_N:D + 2 * SSD_N + (g + 1) * SSD_N] = dCs[g]
            ddt_ref[rw, :] = ddt
            ddtb_ref[...] += ddtb
            dal_ref[...] += dal
        for s in range(NSLAB):
            dstate[s * LANES:(s + 1) * LANES, :] = dnew[s]

    def rv(c):
        return nc // CPS - 1 - c

    return pl.pallas_call(
        body, name=name, grid=(nc // CPS,),
        in_specs=[pl.BlockSpec((rows_, D), lambda c: (rv(c), 0)),
                  pl.BlockSpec((rows_, 2 * SSD_N), lambda c: (rv(c), D // (2 * SSD_N))),
                  pl.BlockSpec((rows_, 2 * SSD_N), lambda c: (rv(c), D // (2 * SSD_N) + 1)),
                  pl.BlockSpec((rows_, LANES), lambda c: (rv(c), 0)),
                  pl.BlockSpec((1, LANES), lambda c: (0, 0)), pl.BlockSpec((1, LANES), lambda c: (0, 0)),
                  pl.BlockSpec((CPS, D, SSD_N), lambda c: (rv(c), 0, 0)),
                  pl.BlockSpec((rows_, D), lambda c: (rv(c), 0)),
                  pl.BlockSpec((rows_, D), lambda c: (rv(c), 0))],
        out_specs=[pl.BlockSpec((rows_, XBC), lambda c: (rv(c), 0)),
                   pl.BlockSpec((rows_, LANES), lambda c: (rv(c), 0)),
                   pl.BlockSpec((1, LANES), lambda c: (0, 0)), pl.BlockSpec((1, LANES), lambda c: (0, 0))],
        out_shape=[jax.ShapeDtypeStruct((S, XBC), F32), jax.ShapeDtypeStruct((S, LANES), F32),
                   jax.ShapeDtypeStruct((1, LANES), F32), jax.ShapeDtypeStruct((1, LANES), F32)],
        scratch_shapes=[pltpu.VMEM((D, SSD_N), F32)],
        compiler_params=pltpu.CompilerParams(dimension_semantics=("arbitrary",)),
    )(xbc_c, xbc_c, xbc_c, pd, dtb, alog, states, dy, dxs_skip)


def _ssd_post_core(y, xs, z, d128, nw):
    tm = y.shape[0]
    ex = (_iota((LANES, D), 1) // HD == _iota((LANES, D), 0)).astype(F32)
    d_x = jnp.sum(_fdot(jnp.broadcast_to(d128, (8, LANES)), ex), axis=0, keepdims=True) * 0.125
    y2 = (y + d_x * xs) * _silu(z)
    lo = _iota((tm, D), 1) < D // 2
    sq = y2 * y2
    ms0 = jnp.sum(jnp.where(lo, sq, 0.0), axis=-1, keepdims=True) / (D // 2)
    ms1 = jnp.sum(jnp.where(lo, 0.0, sq), axis=-1, keepdims=True) / (D // 2)
    r = jnp.where(lo, lax.rsqrt(ms0 + EPS), lax.rsqrt(ms1 + EPS))
    return y2 * r * nw


def _ssd_post_ins(y, xbc_c, pc, d128, nw):
    return [("row", y, None, 0), ("row", xbc_c, D, 0), ("row", pc, D, 0), ("const", d128, None, 0), ("const", nw, None, 0)]


def _ssd_post_fwd(y, xbc_c, pc, d128, nw, name):
    S = y.shape[0]
    return _rows(name, lambda i, j, *v: [_ssd_post_core(*v)], _ssd_post_ins(y, xbc_c, pc, d128, nw),
                 [(D, D, 0, BF16)], tm=128, nrows=S)[0]


def _ssd_post_bwd(y, xbc_c, pc, d128, nw, dout, name):
    S = y.shape[0]

    def fn(i, j, *v):
        _, vjp = jax.vjp(_ssd_post_core, *v[:5])
        return list(vjp(v[5]))

    return _rows(name, fn, _ssd_post_ins(y, xbc_c, pc, d128, nw) + [("row", dout, None, 0)],
                 [(D, D, 0, F32), (D, D, 0, F32), (D, D, 0, BF16)], [(1, LANES, LANES), (1, D, D)], tm=128, nrows=S)


def _gates_core(g0, g1, g2, b0, b1, b2, ya, yb, yc):
    return _sigmoid(g0 + b0) * ya + _sigmoid(g1 + b1) * yb + _sigmoid(g2 + b2) * yc


def _gate_parts(pdv, bv):
    gp = pltpu.roll(pdv, SEC_D - 16, 1)
    return [gp[:, k * D:(k + 1) * D] for k in range(3)] + [bv[:, k * D:(k + 1) * D] for k in range(3)]


def _gates_fwd(pd, bg, ya, yb, yc, name):
    S = pd.shape[0]

    def fn(i, j, pdv, bv, a, b, c):
        return [_gates_core(*_gate_parts(pdv, bv), a, b, c)]

    return _rows(name, fn, [("row", pd, None, 0), ("const", bg, None, 0), ("row", ya, None, 0), ("row", yb, None, 0),
                            ("row", yc, None, 0)], [(D, D, 0, BF16)], tm=128, nrows=S)[0]


def _gates_bwd(pd, bg, ya, yb, yc, dm, name):
    S = pd.shape[0]
    tm = 128

    def fn(i, j, pdv, bv, a, b, c, dmv):
        _, vjp = jax.vjp(_gates_core, *_gate_parts(pdv, bv), a, b, c)
        g = vjp(dmv)
        return [g[6], g[7], g[8], jnp.concatenate(g[0:3], axis=1), jnp.concatenate(g[3:6], axis=1)]

    return _rows(name, fn, [("row", pd, None, 0), ("const", bg, None, 0), ("row", ya, None, 0), ("row", yb, None, 0),
                            ("row", yc, None, 0), ("row", dm, None, 0)],
                 [(D, D, 0, BF16)] * 3 + [(3 * D, 3 * D, 0, BF16)], [(1, 3 * D, 3 * D)], tm=tm, nrows=S)


def _adamw(w, g, m, v, name):
    rows, C = w.shape
    tm = _pick(rows, [t for t in (512, 256, 128, 64, 32, 16, 8) if t * C <= ADAM_TILE])

    def fn(i, j, wv, gv, mv, vv):
        m2 = ADAM_B1 * mv + (1.0 - ADAM_B1) * gv
        v2 = ADAM_B2 * vv + (1.0 - ADAM_B2) * jnp.square(gv)
        m_hat = m2 / (1.0 - ADAM_B1 ** ADAM_STEP)
        v_hat = v2 / (1.0 - ADAM_B2 ** ADAM_STEP)
        delta = -ADAM_LR * (m_hat / (jnp.sqrt(v_hat) + ADAM_EPS) + ADAM_WD * wv)
        return [delta, m2, v2]

    return _rows(name, fn, [("row", a, None, 0) for a in (w, g, m, v)], [(C, C, 0, F32)] * 3, tm=tm, nrows=rows)


def _position():
    return lax.axis_index("x"), lax.axis_index("y"), lax.axis_index("c")


def _other_chips(x, y):
    return [(1 - x, y), (x, 1 - y), (1 - x, 1 - y)]


_HBM = pl.BlockSpec(memory_space=pltpu.HBM)


def _gather_parts(half, lo, n):
    def copies(p_ref, out_ref, send_sems, recv_sems):
        x, y, c = _position()
        sibling = (x, y, 1 - c)
        chips = _other_chips(x, y)

        def slab(chip, h):
            return out_ref.at[2 * chip[0] + chip[1], pl.ds(h * half + lo, n), :]

        def copy(k, src, dst, to):
            return pltpu.make_async_remote_copy(src_ref=src, dst_ref=dst, send_sem=send_sems.at[k],
                                                recv_sem=recv_sems.at[k], device_id=to, device_id_type=MESH)

        first = [copy(j, p_ref.at[pl.ds(c * half + lo, n), :], slab((x, y), c), (*chip, c)) for j, chip in enumerate(chips)]
        passed = [copy(3 + j, slab(chip, c), slab(chip, c), sibling) for j, chip in enumerate(chips)]
        from_chips = [copy(j, slab(chip, c), slab(chip, c), (x, y, c)) for j, chip in enumerate(chips)]
        from_sibling = [copy(3 + j, slab(chip, 1 - c), slab(chip, 1 - c), (x, y, c)) for j, chip in enumerate(chips)]
        return first, passed, from_chips, from_sibling

    def start(ins, outs, scr):
        for cp in copies(ins[0], outs[0], *scr)[0]:
            cp.start()

    def finish(ins, outs, scr):
        first, passed, from_chips, from_sibling = copies(ins[0], outs[0], *scr)
        for j in range(3):
            from_chips[j].wait_recv()
            passed[j].start()
        for cp in from_sibling:
            cp.wait_recv()
        for cp in first + passed:
            cp.wait_send()

    return start, finish


def _rs_chip_parts(lo, n):
    def copies(h_ref, out_ref, send_sems, recv_sems):
        x, y, c = _position()
        return [pltpu.make_async_remote_copy(src_ref=h_ref.at[2 * chip[0] + chip[1], pl.ds(lo, n), :],
                                             dst_ref=out_ref.at[j, pl.ds(lo, n), :],
                                             send_sem=send_sems.at[j], recv_sem=recv_sems.at[j],
                                             device_id=(*chip, c), device_id_type=MESH)
                for j, chip in enumerate(_other_chips(x, y))]

    def start(ins, outs, scr):
        for cp in copies(ins[0], outs[0], *scr):
            cp.start()

    def finish(ins, outs, scr):
        for cp in copies(ins[0], outs[0], *scr):
            cp.wait()

    return start, finish


class _Stream:
    def __init__(self, src, buf, parts, nsem, units, name):
        self.src, self.buf, self.parts, self.nsem, self.name = src, buf, parts, nsem, name
        self.next, self.units = 0, units

    def _scratch(self):
        return [pltpu.SemaphoreType.DMA((self.nsem,)), pltpu.SemaphoreType.DMA((self.nsem,))]

    def _take(self, units):
        units = min(units, self.units - self.next)
        lo = self.next * 16
        self.next += units
        return lo, units * 16

    def _set(self, outs):
        self.buf = outs[0]

    def hook(self, units):
        lo, n = self._take(units)
        if n == 0:
            return None
        start, finish = self.parts(lo, n)
        return _Hook([self.src, self.buf], [jax.ShapeDtypeStruct(self.buf.shape, self.buf.dtype)], {1: 0},
                     self._scratch(), start, finish, self._set)

    def drain(self):
        lo, n = self._take(self.units)
        if n:
            start, finish = self.parts(lo, n)

            def body(s_ref, b_ref, o_ref, send_sems, recv_sems):
                args = ((s_ref, b_ref), (o_ref,), (send_sems, recv_sems))
                start(*args)
                finish(*args)

            self.buf = pl.pallas_call(
                body, name=self.name, in_specs=[_ANY, _ANY], out_specs=_ANY,
                out_shape=jax.ShapeDtypeStruct(self.buf.shape, self.buf.dtype),
                scratch_shapes=self._scratch(), input_output_aliases={1: 0},
            )(self.src, self.buf)
        return self.buf


def _rs_pair_exchange(g, name):
    _, R, C = g.shape
    Rh = R // 2

    def body(g_ref, out_ref, send_sem, recv_sem):
        x, y, c = _position()
        src = g_ref.at[pl.ds(0, 4), pl.ds((1 - c) * Rh, Rh), :]
        cp = pltpu.make_async_remote_copy(src_ref=src, dst_ref=out_ref, send_sem=send_sem,
                                          recv_sem=recv_sem, device_id=(x, y, 1 - c), device_id_type=MESH)
        cp.start()
        cp.wait()

    return pl.pallas_call(
        body, name=name, in_specs=[_HBM], out_specs=_HBM,
        out_shape=jax.ShapeDtypeStruct((4, Rh, C), g.dtype),
        scratch_shapes=[pltpu.SemaphoreType.DMA, pltpu.SemaphoreType.DMA],
    )(g)


def _rs_swap(r, name):
    Rh, C = r.shape

    def body(r_ref, out_ref, send_sem, recv_sem):
        x, y, c = _position()
        cp = pltpu.make_async_remote_copy(src_ref=r_ref, dst_ref=out_ref, send_sem=send_sem,
                                          recv_sem=recv_sem, device_id=(x, y, 1 - c), device_id_type=MESH)
        cp.start()
        cp.wait()

    return pl.pallas_call(
        body, name=name, in_specs=[_HBM], out_specs=_HBM,
        out_shape=jax.ShapeDtypeStruct((Rh, C), r.dtype),
        scratch_shapes=[pltpu.SemaphoreType.DMA, pltpu.SemaphoreType.DMA],
    )(r)


def _rs_add_pair(g, recv, cidx, name):
    _, R, C = g.shape
    Rh = R // 2
    tm = _pick(Rh, (400, 280, 200, 160, 80, 40, 16, 8))
    nt = Rh // tm

    def body(c_ref, g_ref, r_ref, o_ref):
        o_ref[...] = (g_ref[...].astype(F32) + r_ref[...].astype(F32)).astype(o_ref.dtype)

    return pl.pallas_call(
        body, name=name,
        grid_spec=pltpu.PrefetchScalarGridSpec(
            num_scalar_prefetch=1, grid=(4, nt),
            in_specs=[pl.BlockSpec((1, tm, C), lambda k, i, cr: (k, cr[0] * nt + i, 0)),
                      pl.BlockSpec((1, tm, C), lambda k, i, cr: (k, i, 0))],
            out_specs=pl.BlockSpec((1, tm, C), lambda k, i, cr: (k, i, 0))),
        out_shape=jax.ShapeDtypeStruct((4, Rh, C), BF16),
    )(cidx, g, recv)


def _rs_add_chips(h, recv, chip_idx, name):
    _, Rh, C = h.shape
    tm = _pick(Rh, (400, 280, 200, 160, 80, 40, 16, 8))

    def body(c_ref, h_ref, r_ref, o_ref):
        acc = h_ref[0].astype(F32)
        for j in range(3):
            acc = acc + r_ref[j].astype(F32)
        o_ref[...] = acc

    return pl.pallas_call(
        body, name=name,
        grid_spec=pltpu.PrefetchScalarGridSpec(
            num_scalar_prefetch=1, grid=(Rh // tm,),
            in_specs=[pl.BlockSpec((1, tm, C), lambda i, cr: (cr[0], i, 0)), pl.BlockSpec((3, tm, C), lambda i, cr: (0, i, 0))],
            out_specs=pl.BlockSpec((tm, C), lambda i, cr: (i, 0))),
        out_shape=jax.ShapeDtypeStruct((Rh, C), F32),
    )(chip_idx, h, recv)


def _all_reduce_small(vec, name):
    n, C = vec.shape

    def body(v_ref, out_ref, buf, send_sems, recv_sems):
        x, y, c = _position()

        def flip(k):
            return ((1 - x) if k & 4 else x, (1 - y) if k & 2 else y, (1 - c) if k & 1 else c)

        def idx(p):
            return 4 * p[0] + 2 * p[1] + p[2]

        me = idx((x, y, c))
        buf[me] = v_ref[...]
        cps = [pltpu.make_async_remote_copy(src_ref=v_ref, dst_ref=buf.at[me], send_sem=send_sems.at[k - 1],
                                            recv_sem=recv_sems.at[k - 1], device_id=flip(k), device_id_type=MESH)
               for k in range(1, 8)]
        for cp in cps:
            cp.start()
        for k in range(1, 8):
            pltpu.make_async_remote_copy(src_ref=v_ref, dst_ref=buf.at[idx(flip(k))], send_sem=send_sems.at[k - 1],
                                         recv_sem=recv_sems.at[k - 1], device_id=flip(k), device_id_type=MESH).wait_recv()
        for cp in cps:
            cp.wait_send()
        acc = buf[0]
        for s in range(1, 8):
            acc = acc + buf[s]
        out_ref[...] = acc

    return pl.pallas_call(
        body, name=name,
        in_specs=[pl.BlockSpec(memory_space=pltpu.VMEM)], out_specs=pl.BlockSpec(memory_space=pltpu.VMEM),
        out_shape=jax.ShapeDtypeStruct((n, C), F32),
        scratch_shapes=[pltpu.VMEM((8, n, C), F32), pltpu.SemaphoreType.DMA((7,)), pltpu.SemaphoreType.DMA((7,))],
    )(vec)


BIG = (("w_in", (D, IN_WIDTH // 4), "cols"), ("w_a", (GW, D // 4), "cols"), ("pool_w", (4, PG // 4, PG), "pool"),
       ("w_b", (D // 4, D), "rows"), ("w_c", (D // 4, D), "rows"), ("w_o", (D // 4, D), "rows"),
       ("ffn_w_up", (D, 2 * D_FF // 4), "cols"), ("ffn_w_down", (D_FF // 4, D), "rows"))
def _pack_rows(s):
    k = math.prod(s) // D
    return -(-k // 16) * 16, k


PACK_ROWS = sum(_pack_rows(s)[0] for _, s, _ in BIG)
PACK_PAD = -(-PACK_ROWS // 32) * 32


def _pad_rows(v, rows):
    pad = [(0, 0)] * v.ndim
    pad[-2] = (0, rows - v.shape[-2])
    return jnp.pad(v, pad) if rows > v.shape[-2] else v


def _pack_blocks(blocks, dtype):
    lead = blocks["w_in"].shape[:-2]
    flat = []
    for n, s, how in BIG:
        v = blocks[n].astype(dtype)
        if how == "cols":
            v = jnp.swapaxes(v, -1, -2)
        flat.append(_pad_rows(v.reshape(*lead, -1, D), _pack_rows(s)[0]))
    flat.append(jnp.zeros((*lead, PACK_PAD - PACK_ROWS, D), dtype))
    return jnp.concatenate(flat, axis=-2)


def _unpack_blocks(pack):
    out, r = {}, 0
    for n, s, how in BIG:
        rows, k = _pack_rows(s)
        v = pack[r:r + k, :]
        out[n] = v.reshape(s[1], s[0]).T if how == "cols" else v.reshape(s)
        r += rows
    return out


def _operands(allp):
    out, r = {}, 0
    for n, s, how in BIG:
        rows, k = _pack_rows(s)
        v = allp[:, r:r + k, :]
        if how == "cols":
            out[n] = v.reshape(4 * s[1], s[0])
        elif how == "rows":
            out[n] = v.reshape(4 * s[0], s[1])
        else:
            out[n] = v.reshape(4, *s).transpose(1, 0, 2, 3).reshape(4, PG, PG)
        r += rows
    return out


def _pack_operands(g, dtype):
    flat = []
    for n, s, how in BIG:
        v = g[n].astype(dtype)
        if how == "pool":
            v = v.reshape(4, 4, s[1], s[2]).transpose(1, 0, 2, 3)
        flat.append(_pad_rows(v.reshape(4, -1, D), _pack_rows(s)[0]))
    flat.append(jnp.zeros((4, PACK_PAD - PACK_ROWS, D), dtype))
    return jnp.concatenate(flat, axis=1)


def _layer_fwd(x, w, sm, bias, hk):
    u = _rms_fwd(x, sm["ln1_g"], "rms1")
    pa = _mm(u, w["in_a"], tb=True, name="in_a", hook=hk("in_a"))
    pb = _mm(u, w["in_b"], tb=True, out_dtype=BF16, name="in_b", hook=hk("in_b"))
    pc = _mm(u, w["in_c"], tb=True, out_dtype=BF16, name="in_c", hook=hk("in_c"))
    pd = _mm(u, w["in_d"], tb=True, out_dtype=BF16, name="in_d", hook=hk("in_d"))
    os_, ls_ = [], []
    for gi in range(3):
        o, l = _attn_fwd(pa, bias[gi], gi, "attn_fwd%d" % gi)
        os_.append(o)
        ls_.append(l)
    att = _mix_fwd(os_, ls_, "mix_fwd")
    ya = _mm(att, w["w_a"], tb=True, out_dtype=BF16, name="mm_wa")
    pool_o = _pool_fwd(pb, w["pool_w"], sm["pool_scale"], "pool_fwd")
    yb = _mm(pool_o, w["w_b"], out_dtype=BF16, name="mm_wb")
    xbc_c = _ssd_conv_fwd(pc, sm["ssd_conv_w"], sm["ssd_conv_b"], "ssd_conv_fwd")
    y_scan, states = _ssd_scan_fwd(xbc_c, pd, sm["ssd_dt_bias"], sm["ssd_a_log"], "ssd_scan_fwd")
    ssd_o = _ssd_post_fwd(y_scan, xbc_c, pc, sm["ssd_d"], sm["ssd_norm_w"], "ssd_post_fwd")
    yc = _mm(ssd_o, w["w_c"], out_dtype=BF16, name="mm_wc")
    merged = _gates_fwd(pd, sm["b_gate"], ya, yb, yc, "gates_fwd")
    x1 = _mm(merged, w["w_o"], add=x, name="mm_wo", hook=hk("mm_wo"))
    u2 = _rms_fwd(x1, sm["ln2_g"], "rms2")
    h = _mm(u2, w["ffn_w_up"], tb=True, out_dtype=BF16, name="mm_up", hook=hk("mm_up"))
    f = _ffn_act_fwd(h, sm["ffn_conv_w"], sm["ffn_conv_b"], "ffn_act_fwd")
    x2 = _mm(f, w["ffn_w_down"], add=x1, name="mm_down", hook=hk("mm_down"))
    saved = dict(x=x, u=u, pa=pa, pb=pb, pc=pc, pd=pd, os=os_, ls=ls_, att=att, ya=ya, yb=yb, yc=yc, pool_o=pool_o,
                 xbc_c=xbc_c, y_scan=y_scan, states=states, ssd_o=ssd_o, merged=merged, x1=x1, u2=u2, h=h, f=f)
    return x2, saved


def _layer_bwd(dx2, w, sm, bias, dbs, sv, hk):
    gw, gs = {}, {}
    S = dx2.shape[0]

    def gmm(a, b, name):
        return _mm(a, b, ta=True, out_dtype=BF16, name=name, hook=hk(name))

    df = _mm(dx2, w["ffn_w_down"], tb=True, out_dtype=BF16, name="d_f", hook=hk("d_f"))
    gw["ffn_w_down"] = gmm(sv["f"], dx2, "g_down")
    dha, dhv, gs["ffn_conv_w"], gs["ffn_conv_b"] = _ffn_act_bwd(sv["h"], sm["ffn_conv_w"], sm["ffn_conv_b"], df, "ffn_act_bwd")
    du2 = _mm(dha, w["up_a"], name="d_u2_a", hook=hk("d_u2_a"))
    du2 = _mm(dhv, w["up_v"], add=du2, name="d_u2_v", hook=hk("d_u2_v"))
    gw["ffn_w_up"] = jnp.concatenate([gmm(dha, sv["u2"], "g_up_a"), gmm(dhv, sv["u2"], "g_up_v")], axis=0)
    dx1, gs["ln2_g"] = _rms_bwd(sv["x1"], sm["ln2_g"], du2, dx2, "rms2_bwd")
    dmerged = _mm(dx1, w["w_o"], tb=True, out_dtype=BF16, name="d_merged", hook=hk("d_merged"))
    gw["w_o"] = gmm(sv["merged"], dx1, "g_wo")
    dya, dyb, dyc, dgate, gs["b_gate"] = _gates_bwd(
        sv["pd"], sm["b_gate"], sv["ya"], sv["yb"], sv["yc"], dmerged, "gates_bwd")
    dssd_o = _mm(dyc, w["w_c"], tb=True, name="d_ssd_o")
    gw["w_c"] = gmm(sv["ssd_o"], dyc, "g_wc")
    dy_scan, dxs_skip, dz, gs["ssd_d"], gs["ssd_norm_w"] = _ssd_post_bwd(
        sv["y_scan"], sv["xbc_c"], sv["pc"], sm["ssd_d"], sm["ssd_norm_w"], dssd_o, "ssd_post_bwd")
    dxbc_c, ddt, gs["ssd_dt_bias"], gs["ssd_a_log"] = _ssd_scan_bwd(
        sv["xbc_c"], sv["pd"], sm["ssd_dt_bias"], sm["ssd_a_log"], sv["states"], dy_scan, dxs_skip, "ssd_scan_bwd")
    dxbc, gs["ssd_conv_w"], gs["ssd_conv_b"] = _ssd_conv_bwd(sv["pc"], sm["ssd_conv_w"], sm["ssd_conv_b"], dxbc_c, "ssd_conv_bwd")
    dpool_o = _mm(dyb, w["w_b"], tb=True, name="d_pool_o")
    gw["w_b"] = gmm(sv["pool_o"], dyb, "g_wb")
    dpb, dpw, gs["pool_scale"] = _pool_bwd(sv["pb"], w["pool_w"], sm["pool_scale"], dpool_o, "pool_bwd")
    gw["pool_w"] = dpw.reshape(4, PG, PG)
    datt = _mm(dya, w["w_a"], name="d_att")
    gw["w_a"] = gmm(dya, sv["att"], "g_wa")
    dos, dls = _mix_bwd(sv["os"], sv["ls"], datt, "mix_bwd")
    dqkv = tuple(lax.empty((S, AW), F32) for _ in range(3))
    dbs = list(dbs)
    for gi in range(3):
        dqkv, dbs[gi] = _attn_bwd(sv["pa"], bias[gi], dos[gi], dls[gi], dbs[gi], dqkv, gi, "attn_bwd%d" % gi)
    u = sv["u"]
    pieces = [(dqkv[0], "wq"), (dqkv[1], "wk"), (dqkv[2], "wv"), (dpb, "in_b"), (dz, "wz"), (dxbc, "wxbc"),
              (ddt, "wdt"), (dgate, "wgate")]
    du = None
    g_in = []
    for dp, key in pieces:
        du = _mm(dp, w[key], add=du, name="d_u_" + key, hook=hk("d_u_" + key))
        g = gmm(dp, u, "g_in_" + key)
        g_in.append(g[:SSD_HEADS] if key == "wdt" else g)
    gw["w_in"] = jnp.concatenate(g_in, axis=0)
    dx, gs["ln1_g"] = _rms_bwd(sv["x"], sm["ln1_g"], du, dx1, "rms1_bwd")
    return dx, gw, gs, dbs


SMALL_LAYER = ("ln1_g", "b_gate", "pool_scale", "ssd_conv_w", "ssd_conv_b", "ssd_dt_bias", "ssd_a_log", "ssd_d",
               "ssd_norm_w", "ln2_g", "ffn_conv_w", "ffn_conv_b")


def _pad_lanes(v):
    return jnp.pad(v, (0, LANES - v.shape[0])).reshape(1, LANES)


def _layer_weights(ops):
    wt = ops["w_in"]
    o1, o2, o3 = SEC_A, SEC_A + SEC_B, SEC_A + SEC_B + SEC_C
    w = dict(ops)
    w["in_a"] = jnp.pad(wt[:o1], ((0, SEC_A_PAD - o1), (0, 0)))
    w["in_b"] = wt[o1:o2]
    w["in_c"] = wt[o2:o3]
    w["in_d"] = jnp.pad(wt[o3:], ((0, SEC_D - (IN_WIDTH - o3)), (0, 0)))
    w["wq"], w["wk"], w["wv"] = wt[:AW], wt[AW:2 * AW], wt[2 * AW:o1]
    w["wz"], w["wxbc"] = wt[o2:o2 + D], wt[o2 + D:o3]
    w["wdt"] = jnp.pad(wt[o3:o3 + SSD_HEADS], ((0, LANES - SSD_HEADS), (0, 0)))
    w["wgate"] = wt[o3 + SSD_HEADS:]
    w["up_a"], w["up_v"] = ops["ffn_w_up"][:D_FF], ops["ffn_w_up"][D_FF:]
    return w


def _layer_small(p, i):
    sm = {n: p[n][i] for n in SMALL_LAYER}
    out = {}
    for n, v in sm.items():
        if n in ("ssd_dt_bias", "ssd_a_log", "ssd_d"):
            out[n] = _pad_lanes(v)
        elif v.ndim == 1:
            out[n] = v.reshape(1, -1)
        else:
            out[n] = v
    return out


def _local_step(x, target, rel_bias, final_g, layer_full, small, fwd_hooks=None, bwd_hooks=None, after_bwd=None):
    nl = small["ln1_g"].shape[0]
    buckets = [_buckets(d).astype(jnp.int32) for d in DILATIONS]
    bias = [_bias_table(rel_bias, buckets[gi], gi, "bias_table%d" % gi) for gi in range(3)]
    no_hooks = lambda i: (lambda name: None)
    fwd_hooks = fwd_hooks or no_hooks
    bwd_hooks = bwd_hooks or no_hooks
    saved, ws, sms = [], [], []
    h = x
    for i in range(nl):
        w = _layer_weights(layer_full(i))
        sm = _layer_small(small, i)
        h, sv = _layer_fwd(h, w, sm, bias, fwd_hooks(i))
        saved.append(sv)
        ws.append(w)
        sms.append(sm)
    dh, dfinal, loss = _final_loss(h, target, final_g.reshape(1, D))
    gws, gss = [None] * nl, [None] * nl
    dbs = [jnp.zeros((6, WIN, 2 * WIN), F32)] * 3
    for i in reversed(range(nl)):
        dh, gws[i], gss[i], dbs = _layer_bwd(dh, ws[i], sms[i], bias, dbs, saved[i], bwd_hooks(i))
        if after_bwd is not None:
            after_bwd(i, gws[i])
    drel = []
    for gi in range(3):
        onehot = jnp.pad(jax.nn.one_hot(buckets[gi].reshape(-1), REL_BUCKETS, dtype=BF16), ((0, 0), (0, LANES - REL_BUCKETS)))
        drel.append(_mm(dbs[gi].reshape(6, WIN * 2 * WIN), onehot, name="g_relb"))
    return loss, dh, gws, gss, dfinal, jnp.concatenate(drel, axis=0)


WEIGHTS = ("rel_bias", "ln1_g", "w_in", "b_gate", "w_a", "pool_w", "pool_scale", "w_b", "ssd_conv_w", "ssd_conv_b",
           "ssd_dt_bias", "ssd_a_log", "ssd_d", "ssd_norm_w", "w_c", "w_o", "ln2_g", "ffn_w_up", "ffn_conv_w",
           "ffn_conv_b", "ffn_w_down", "final_g")
BIG_NAMES = tuple(n for n, _, _ in BIG)
SHARDED_SMALL = {"ssd_conv_w": XBC // 4, "ffn_conv_w": 2 * D_FF // 4}


def _to_rows(flat):
    n = flat.shape[0]
    rows = -(-n // LANES)
    rows = -(-rows // 8) * 8
    return jnp.pad(flat, (0, rows * LANES - n)).reshape(rows, LANES)


def _flatten(tree, names):
    return jnp.concatenate([tree[n].reshape(-1) for n in names])


def _unflatten(flat, shapes, names):
    out, o = {}, 0
    for n in names:
        k = math.prod(shapes[n])
        out[n] = flat[o:o + k].reshape(shapes[n])
        o += k
    return out


def kernel(x, rel_bias, ln1_g, w_in, b_gate, w_a, pool_w, pool_scale, w_b, ssd_conv_w, ssd_conv_b, ssd_dt_bias, ssd_a_log, ssd_d, ssd_norm_w, w_c, w_o, ln2_g, ffn_w_up, ffn_conv_w, ffn_conv_b, ffn_w_down, final_g, loss_target, m_rel_bias, m_ln1_g, m_w_in, m_b_gate, m_w_a, m_pool_w, m_pool_scale, m_w_b, m_ssd_conv_w, m_ssd_conv_b, m_ssd_dt_bias, m_ssd_a_log, m_ssd_d, m_ssd_norm_w, m_w_c, m_w_o, m_ln2_g, m_ffn_w_up, m_ffn_conv_w, m_ffn_conv_b, m_ffn_w_down, m_final_g, v_rel_bias, v_ln1_g, v_w_in, v_b_gate, v_w_a, v_pool_w, v_pool_scale, v_w_b, v_ssd_conv_w, v_ssd_conv_b, v_ssd_dt_bias, v_ssd_a_log, v_ssd_d, v_ssd_norm_w, v_w_c, v_w_o, v_ln2_g, v_ffn_w_up, v_ffn_conv_w, v_ffn_conv_b, v_ffn_w_down, v_final_g):
    W = dict(rel_bias=rel_bias, ln1_g=ln1_g, w_in=w_in, b_gate=b_gate, w_a=w_a, pool_w=pool_w, pool_scale=pool_scale,
             w_b=w_b, ssd_conv_w=ssd_conv_w, ssd_conv_b=ssd_conv_b, ssd_dt_bias=ssd_dt_bias, ssd_a_log=ssd_a_log,
             ssd_d=ssd_d, ssd_norm_w=ssd_norm_w, w_c=w_c, w_o=w_o, ln2_g=ln2_g, ffn_w_up=ffn_w_up,
             ffn_conv_w=ffn_conv_w, ffn_conv_b=ffn_conv_b, ffn_w_down=ffn_w_down, final_g=final_g)
    M = dict(rel_bias=m_rel_bias, ln1_g=m_ln1_g, w_in=m_w_in, b_gate=m_b_gate, w_a=m_w_a, pool_w=m_pool_w,
             pool_scale=m_pool_scale, w_b=m_w_b, ssd_conv_w=m_ssd_conv_w, ssd_conv_b=m_ssd_conv_b,
             ssd_dt_bias=m_ssd_dt_bias, ssd_a_log=m_ssd_a_log, ssd_d=m_ssd_d, ssd_norm_w=m_ssd_norm_w, w_c=m_w_c,
             w_o=m_w_o, ln2_g=m_ln2_g, ffn_w_up=m_ffn_w_up, ffn_conv_w=m_ffn_conv_w, ffn_conv_b=m_ffn_conv_b,
             ffn_w_down=m_ffn_w_down, final_g=m_final_g)
    V = dict(rel_bias=v_rel_bias, ln1_g=v_ln1_g, w_in=v_w_in, b_gate=v_b_gate, w_a=v_w_a, pool_w=v_pool_w,
             pool_scale=v_pool_scale, w_b=v_w_b, ssd_conv_w=v_ssd_conv_w, ssd_conv_b=v_ssd_conv_b,
             ssd_dt_bias=v_ssd_dt_bias, ssd_a_log=v_ssd_a_log, ssd_d=v_ssd_d, ssd_norm_w=v_ssd_norm_w, w_c=v_w_c,
             w_o=v_w_o, ln2_g=v_ln2_g, ffn_w_up=v_ffn_w_up, ffn_conv_w=v_ffn_conv_w, ffn_conv_b=v_ffn_conv_b,
             ffn_w_down=v_ffn_w_down, final_g=v_final_g)
    nl = ln1_g.shape[0]
    px, py, pc_ = _position()
    chip = 2 * px + py
    cidx = jnp.reshape(pc_, (1,)).astype(jnp.int32)
    chip_idx = jnp.reshape(chip, (1,)).astype(jnp.int32)

    placed = {}
    for n, cs in SHARDED_SMALL.items():
        full = jnp.zeros(W[n].shape[:-1] + (4 * cs,), F32)
        full = lax.dynamic_update_slice(full, W[n], (0, 0, chip * cs))
        placed[n] = jnp.where(pc_ == 0, full, 0.0)
    names_sh = tuple(SHARDED_SMALL)
    shapes_sh = {n: placed[n].shape for n in names_sh}
    got = _all_reduce_small(_to_rows(_flatten(placed, names_sh)), "gather_small")
    small = {n: W[n] for n in SMALL_LAYER}
    small.update(_unflatten(got.reshape(-1), shapes_sh, names_sh))

    packs = _pack_blocks({n: W[n] for n in BIG_NAMES}, BF16)

    half = PACK_PAD // 2
    units = half // 16

    def share(weights, total):
        tot = sum(weights.values())
        return {n: math.ceil(total * v / tot) for n, v in weights.items()}

    gathers = {}

    def gather(i):
        if i not in gathers:
            buf = lax.dynamic_update_slice(lax.empty((4, PACK_PAD, D), BF16), packs[i][None], (chip, 0, 0))
            gathers[i] = _Stream(packs[i], buf, functools.partial(_gather_parts, half), 6, units, "gather_w")
        return gathers[i]

    def layer_full(i):
        return _operands(gather(i).drain())

    fwd_share = share(dict(in_a=89, in_b=26, in_c=57, in_d=66, mm_wo=28, mm_up=120, mm_down=46), units)

    def fwd_hooks(i):
        if i + 1 >= nl:
            return lambda name: None
        return lambda name: gather(i + 1).hook(fwd_share[name]) if name in fwd_share else None

    exchanges = {}
    bwd_share = share(dict(d_f=91, g_down=67, d_u2_a=42, d_u2_v=45, g_up_a=52, g_up_v=52, d_merged=29, g_wo=19,
                           d_u_wgate=48, g_in_wgate=41), units)

    def after_bwd(i, gw):
        g = _pack_operands(gw, BF16)
        recv = _rs_pair_exchange(g, "rs_pair")
        hsum = _rs_add_pair(g, recv, cidx, "rs_add_pair")
        exchanges[i] = (hsum, _Stream(hsum, lax.empty((3, half, D), BF16), _rs_chip_parts, 3, units, "rs_chips"))

    def bwd_hooks(i):
        if i + 1 >= nl:
            return lambda name: None
        return lambda name: exchanges[i + 1][1].hook(bwd_share[name]) if name in bwd_share else None

    loss, dx, gws, gss, dfinal, drel = _local_step(x[0], loss_target[0], rel_bias, final_g, layer_full, small,
                                                   fwd_hooks, bwd_hooks, after_bwd)

    grads = {}
    red = []
    for i in range(nl):
        hsum, stream = exchanges[i]
        r = _rs_add_chips(hsum, stream.drain(), chip_idx, "rs_add_chips")
        other = _rs_swap(r, "rs_swap")
        both = jnp.concatenate([jnp.where(pc_ == 0, r, other), jnp.where(pc_ == 0, other, r)], axis=0)
        red.append(_unpack_blocks(both))
    for n in BIG_NAMES:
        grads[n] = jnp.stack([red[i][n] for i in range(nl)], axis=0)

    sg = {}
    for n in SMALL_LAYER:
        sg[n] = jnp.stack([gss[i][n] for i in range(nl)], axis=0)
    for n in ("ssd_dt_bias", "ssd_a_log", "ssd_d"):
        sg[n] = sg[n][:, 0, :SSD_HEADS]
    sg["rel_bias"] = drel[:, :REL_BUCKETS].T
    sg["final_g"] = dfinal.reshape(D)
    sg["loss"] = loss[0, :1]
    names_sg = tuple(sg)
    shapes_sg = {n: ((nl,) + W[n].shape[1:] if n in SMALL_LAYER and n not in SHARDED_SMALL else
                     (placed[n].shape if n in SHARDED_SMALL else sg[n].shape)) for n in names_sg}
    for n in names_sg:
        sg[n] = sg[n].reshape(shapes_sg[n])
    tot = _all_reduce_small(_to_rows(_flatten(sg, names_sg)), "allreduce_small")
    tot = _unflatten(tot.reshape(-1), shapes_sg, names_sg)
    loss_out = tot.pop("loss").reshape(())
    for n, cs in SHARDED_SMALL.items():
        tot[n] = lax.dynamic_slice(tot[n], (0, 0, chip * cs), tot[n].shape[:-1] + (cs,))
    grads.update(tot)

    delta, new_m, new_v = {}, {}, {}
    for n in BIG_NAMES:
        shp = W[n].shape
        r2 = lambda a: a.reshape(-1, shp[-1])
        dl, m2, v2 = _adamw(r2(W[n]), r2(grads[n]), r2(M[n]), r2(V[n]), "adamw_" + n)
        delta[n], new_m[n], new_v[n] = dl.reshape(shp), m2.reshape(shp), v2.reshape(shp)
    names_s = tuple(n for n in WEIGHTS if n not in BIG_NAMES)
    shapes_s = {n: W[n].shape for n in names_s}
    pk = lambda t: _to_rows(_flatten(t, names_s))
    dl, m2, v2 = _adamw(pk(W), pk(grads), pk(M), pk(V), "adamw_small")
    delta.update(_unflatten(dl.reshape(-1), shapes_s, names_s))
    new_m.update(_unflatten(m2.reshape(-1), shapes_s, names_s))
    new_v.update(_unflatten(v2.reshape(-1), shapes_s, names_s))

    return (loss_out, dx[None], *[grads[n] for n in WEIGHTS], *[delta[n] for n in WEIGHTS],
            *[new_m[n] for n in WEIGHTS], *[new_v[n] for n in WEIGHTS])
```

```python
import functools
import math

import jax
import jax.numpy as jnp
from jax import lax
from jax.experimental import pallas as pl
from jax.experimental.pallas import tpu as pltpu

F32 = jnp.float32
BF16 = jnp.bfloat16
MESH = pl.DeviceIdType.MESH

D = 1024
HD = 64
GW = 384
AW = 3 * GW
WIN = 128
DILATIONS = (1, 4, 16)
REL_BUCKETS = 32
REL_MAX_DISTANCE = 2048
POOL_WINDOWS = (2, 4, 8, 16)
PG = 256
SSD_HEADS = 16
SSD_N = 128
SSD_CHUNK = 128
XBC = 1536
D_FF = 2816
EPS = 1e-6
NEG = -1e30
HALO = 16
LANES = 128

SEC_A = 3 * AW
SEC_B = D
SEC_C = D + XBC
SEC_D = 3328
SEC_A_PAD = 3584
IN_WIDTH = SEC_A + SEC_B + SEC_C + 16 + 3 * D

ADAM_LR = 0.001
ADAM_B1 = 0.9
ADAM_B2 = 0.999
ADAM_EPS = 1e-08
ADAM_WD = 0.01
ADAM_STEP = 10
ADAM_TILE = 256 * 1024
MM_VMEM_BYTES = 40 * 1024 * 1024
MM_MAX_OUT_TILE = 1024 * 1024
HBM_BYTES_PER_US = 2.0e6
STEP_US = 0.35
MXU_WIDTH = 256
MXU_FLOPS_PER_US = 0.65e6


_ANY = pl.BlockSpec(memory_space=pl.ANY)


def _pick(d, cands):
    for t in cands:
        if d % t == 0:
            return t
    return d


def _iota(shape, dim):
    return lax.broadcasted_iota(jnp.int32, shape, dim)


def _dg(a, b, ca, cb):
    return lax.dot_general(a.astype(BF16), b.astype(BF16), (((ca,), (cb,)), ((), ())),
                           preferred_element_type=F32)


@jax.custom_vjp
def _bdot_nn(a, b):
    return _dg(a, b, 1, 0)


def _nn_fwd(a, b):
    return _dg(a, b, 1, 0), (a, b)


def _nn_bwd(res, g):
    a, b = res
    return _dg(g, b, 1, 1), _dg(a, g, 0, 0)


_bdot_nn.defvjp(_nn_fwd, _nn_bwd)


@jax.custom_vjp
def _bdot_nt(a, b):
    return _dg(a, b, 1, 1)


def _nt_fwd(a, b):
    return _dg(a, b, 1, 1), (a, b)


def _nt_bwd(res, g):
    a, b = res
    return _dg(g, b, 1, 0), _dg(g, a, 0, 0)


_bdot_nt.defvjp(_nt_fwd, _nt_bwd)


@jax.custom_vjp
def _bdot_tn(a, b):
    return _dg(a, b, 0, 0)


def _tn_fwd(a, b):
    return _dg(a, b, 0, 0), (a, b)


def _tn_bwd(res, g):
    a, b = res
    return _dg(b, g, 1, 1), _dg(a, g, 1, 0)


_bdot_tn.defvjp(_tn_fwd, _tn_bwd)


def _fdot(a, b):
    return jnp.dot(a, b, preferred_element_type=F32, precision=lax.Precision.HIGHEST)


def _sigmoid(x):
    return 0.5 * jnp.tanh(0.5 * x) + 0.5


def _silu(x):
    return x * _sigmoid(x)


def _softplus(x):
    return jnp.maximum(x, 0.0) + jnp.log(1.0 + jnp.exp(-jnp.abs(x)))


def _lane_pick(m, h):
    return jnp.sum(jnp.where(_iota(m.shape, 1) == h, m, 0.0), axis=1, keepdims=True)


def _row_pick(m, h):
    return jnp.sum(jnp.where(_iota(m.shape, 0) == h, m, 0.0), axis=0, keepdims=True)


def _stack_rows(rows, n):
    c = rows[0].shape[1]
    r = _iota((n, c), 0)
    out = jnp.zeros((n, c), F32)
    for k, v in enumerate(rows):
        out = out + jnp.where(r == k, v, 0.0)
    return out


def _mm(a, b, *, ta=False, tb=False, add=None, out_dtype=F32, name, hook=None):
    if ta:
        K, M = a.shape
    else:
        M, K = a.shape
    if tb:
        N, Kb = b.shape
    else:
        Kb, N = b.shape
    assert K == Kb, (a.shape, b.shape, ta, tb)
    tm, tn, tk = _mm_tiles(M, N, K, a.dtype.itemsize, b.dtype.itemsize, jnp.dtype(out_dtype).itemsize,
                           0 if add is None else add.dtype.itemsize)
    ni, nj, nk = M // tm, N // tn, K // tk
    ca = 0 if ta else 1
    cb = 1 if tb else 0
    n_in = 2 if add is None else 3
    n_hin = 0 if hook is None else len(hook.inputs)
    n_hout = 0 if hook is None else len(hook.out_shapes)

    def body(*refs):
        a_ref, b_ref = refs[:2]
        add_ref = None if add is None else refs[2]
        o_ref = refs[n_in + n_hin]
        scr = refs[n_in + n_hin + 1 + n_hout:]
        acc_ref = scr[0] if nk > 1 else None
        hargs = (refs[n_in:n_in + n_hin], refs[n_in + n_hin + 1:n_in + n_hin + 1 + n_hout], scr[1 if nk > 1 else 0:])
        i, j, k = pl.program_id(0), pl.program_id(1), pl.program_id(2)
        if hook is not None:
            @pl.when((i == 0) & (j == 0) & (k == 0))
            def _():
                hook.start(*hargs)

        part = _dg(a_ref[...], b_ref[...], ca, cb)

        def finish(r):
            if add_ref is not None:
                r = r + add_ref[...].astype(F32)
            o_ref[...] = r.astype(o_ref.dtype)

        if nk == 1:
            finish(part)
        else:
            @pl.when(k == 0)
            def _():
                acc_ref[...] = part

            @pl.when((k > 0) & (k < nk - 1))
            def _():
                acc_ref[...] += part

            @pl.when(k == nk - 1)
            def _():
                finish(acc_ref[...] + part)

        if hook is not None:
            @pl.when((i == ni - 1) & (j == nj - 1) & (k == nk - 1))
            def _():
                hook.finish(*hargs)

    a_spec = pl.BlockSpec((tk, tm), lambda i, j, k: (k, i)) if ta else pl.BlockSpec((tm, tk), lambda i, j, k: (i, k))
    b_spec = pl.BlockSpec((tn, tk), lambda i, j, k: (j, k)) if tb else pl.BlockSpec((tk, tn), lambda i, j, k: (k, j))
    in_specs = [a_spec, b_spec]
    args = [a, b]
    if add is not None:
        in_specs.append(pl.BlockSpec((tm, tn), lambda i, j, k: (i, j)))
        args.append(add)
    out_specs = [pl.BlockSpec((tm, tn), lambda i, j, k: (i, j))]
    out_shape = [jax.ShapeDtypeStruct((M, N), out_dtype)]
    scratch = [pltpu.VMEM((tm, tn), F32)] if nk > 1 else []
    aliases = {}
    if hook is not None:
        in_specs += [_ANY] * n_hin
        args += list(hook.inputs)
        out_specs += [_ANY] * n_hout
        out_shape += list(hook.out_shapes)
        scratch += list(hook.scratch)
        aliases = {n_in + hi: 1 + ho for hi, ho in hook.aliases.items()}
    sem = ("parallel", "parallel", "arbitrary") if hook is None else ("arbitrary",) * 3
    res = pl.pallas_call(
        body, name=name, grid=(ni, nj, nk), in_specs=in_specs, out_specs=out_specs, out_shape=out_shape,
        scratch_shapes=scratch, input_output_aliases=aliases,
        compiler_params=pltpu.CompilerParams(dimension_semantics=sem),
    )(*args)
    if hook is not None:
        hook.done(res[1:])
    return res[0]


def _wide(v):
    return v.astype(F32) if v.dtype == BF16 else v


def _mmf(a, b, *, tb=False, add=None, pre=None, post=None, out_dtype=F32, name, tm, hook=None):
    if tb:
        N, K = b.shape
    else:
        K, N = b.shape
    M = pre[1][0].shape[0] if pre else a.shape[0]
    tn = N if post else _pick(N, (512, 256, LANES))
    ni, nj = M // tm, N // tn
    cb = 1 if tb else 0
    pre_fn, pre_rows, pre_consts = pre if pre else (None, [], [])
    post_fn, post_rows, post_consts, post_outs, post_accs = post if post else (None, [], [], [], [])
    hook_in = [] if hook is None else list(hook.inputs)
    hook_out = [] if hook is None else list(hook.out_shapes)

    def row_spec(arr):
        return pl.BlockSpec((tm, arr.shape[1]), lambda i, j: (i, 0))

    def const_spec(arr):
        return pl.BlockSpec(arr.shape, lambda i, j, nd=arr.ndim: (0,) * nd)

    args, in_specs = [], []
    for arr in ([a] if not pre else pre_rows):
        args.append(arr)
        in_specs.append(row_spec(arr))
    for arr in pre_consts:
        args.append(arr)
        in_specs.append(const_spec(arr))
    args.append(b)
    in_specs.append(pl.BlockSpec((tn, K), lambda i, j: (j, 0)) if tb else pl.BlockSpec((K, tn), lambda i, j: (0, j)))
    if add is not None:
        args.append(add)
        in_specs.append(pl.BlockSpec((tm, tn), lambda i, j: (i, j)))
    for arr in post_rows:
        args.append(arr)
        in_specs.append(row_spec(arr))
    for arr in post_consts:
        args.append(arr)
        in_specs.append(const_spec(arr))
    n_main = len(args)
    args += hook_in
    in_specs += [_ANY] * len(hook_in)

    out_shape, out_specs = [], []
    if post:
        for c, dt in post_outs:
            out_shape.append(jax.ShapeDtypeStruct((M, c), dt))
            out_specs.append(pl.BlockSpec((tm, c), lambda i, j: (i, 0)))
        for r, c in post_accs:
            out_shape.append(jax.ShapeDtypeStruct((r, c), F32))
            out_specs.append(pl.BlockSpec((r, c), lambda i, j: (0, 0)))
    else:
        out_shape.append(jax.ShapeDtypeStruct((M, N), out_dtype))
        out_specs.append(pl.BlockSpec((tm, tn), lambda i, j: (i, j)))
    if pre:
        out_shape.append(jax.ShapeDtypeStruct((M, K), BF16))
        out_specs.append(pl.BlockSpec((tm, K), lambda i, j: (i, 0)))
    n_out = len(out_shape)
    out_shape += hook_out
    out_specs += [_ANY] * len(hook_out)
    scratch = ([pltpu.VMEM((tm, K), BF16)] if pre else []) + ([] if hook is None else list(hook.scratch))
    aliases = {} if hook is None else {n_main + hi: n_out + ho for hi, ho in hook.aliases.items()}

    def body(*refs):
        ins, outs, scr = refs[:n_main], refs[len(args):len(args) + n_out], refs[len(args) + len(out_shape):]
        hargs = (refs[n_main:len(args)], refs[len(args) + n_out:len(args) + len(out_shape)], scr[1 if pre else 0:])
        i, j = pl.program_id(0), pl.program_id(1)
        if hook is not None:
            @pl.when((i == 0) & (j == 0))
            def _():
                hook.start(*hargs)

        it = iter(ins)
        if pre:
            rows_ = [next(it) for _ in pre_rows]
            consts_ = [next(it) for _ in pre_consts]

            @pl.when(j == 0)
            def _():
                av = pre_fn(*[_wide(r[...]) for r in rows_], *[_wide(r[...]) for r in consts_]).astype(BF16)
                scr[0][...] = av
                outs[-1][...] = av

            at = scr[0][...]
        else:
            at = next(it)[...]
        p = _dg(at, next(it)[...], 1, cb)
        if add is not None:
            p = p + next(it)[...].astype(F32)
        if post:
            rows_ = [next(it) for _ in post_rows]
            consts_ = [next(it) for _ in post_consts]
            res = post_fn(p, *[_wide(r[...]) for r in rows_], *[_wide(r[...]) for r in consts_])
            for r, v in zip(outs[:len(post_outs)], res[:len(post_outs)]):
                r[...] = v.astype(r.dtype)
            for r, v in zip(outs[len(post_outs):], res[len(post_outs):]):
                @pl.when(i == 0)
                def _(r=r, v=v):
                    r[...] = v

                @pl.when(i > 0)
                def _(r=r, v=v):
                    r[...] += v
        else:
            outs[0][...] = p.astype(outs[0].dtype)
        if hook is not None:
            @pl.when((i == ni - 1) & (j == nj - 1))
            def _():
                hook.finish(*hargs)

    res = pl.pallas_call(
        body, name=name, grid=(ni, nj), in_specs=in_specs, out_specs=out_specs, out_shape=out_shape,
        scratch_shapes=scratch, input_output_aliases=aliases,
        compiler_params=pltpu.CompilerParams(dimension_semantics=("arbitrary", "arbitrary")),
    )(*args)
    if hook is not None:
        hook.done(res[n_out:])
    return res[:n_out]


def _mm_tiles(M, N, K, sa, sb, so, sadd):
    def tiles(d):
        return [t for t in range(LANES, min(d, 2048) + 1, LANES) if d % t == 0] or [d]

    best = None
    for tk in [K] + [t for t in tiles(K) if t < K]:
        for tm in tiles(M):
            for tn in tiles(N):
                vmem = 2 * (tm * tk * sa + tk * tn * sb + tm * tn * (so + sadd)) + (tm * tn * 4 if tk < K else 0)
                if vmem > MM_VMEM_BYTES or tm * tn > MM_MAX_OUT_TILE:
                    continue
                a_reads = 1 if tk == K else N // tn
                traffic = M * K * sa * a_reads + K * N * sb * (M // tm) + M * N * (so + sadd)
                steps = (M // tm) * (N // tn) * (K // tk)
                width = -(-tn // MXU_WIDTH) * MXU_WIDTH
                mxu = 2.0 * M * K * N * (width / tn) / MXU_FLOPS_PER_US
                edge = tm * tk * sa + tk * tn * sb + tm * tn * (so + sadd)
                cost = max(traffic / HBM_BYTES_PER_US, mxu) + steps * STEP_US + edge / HBM_BYTES_PER_US
                if best is None or cost < best[0]:
                    best = (cost, tm, tn, tk)
    assert best is not None, (M, N, K)
    return best[1:]


class _Hook:
    def __init__(self, inputs, out_shapes, aliases, scratch, start, finish, done):
        self.inputs, self.out_shapes, self.aliases, self.scratch = inputs, out_shapes, aliases, scratch
        self.start, self.finish, self.done = start, finish, done


def _rows(name, fn, ins, outs, accs=(), *, tm, nrows, ncol=1):
    nt = nrows // tm
    hb = tm // HALO
    nh = nrows // HALO
    in_specs, args = [], []
    for kind, arr, cw, base in ins:
        if kind == "row":
            cw = arr.shape[1] if cw is None else cw
            in_specs.append(pl.BlockSpec((tm, cw), lambda j, i, base=base: (i, base + j)))
        elif kind == "prev":
            in_specs.append(pl.BlockSpec((HALO, cw), lambda j, i, base=base: (jnp.maximum(i * hb - 1, 0), base + j)))
        elif kind == "next":
            in_specs.append(pl.BlockSpec((HALO, cw), lambda j, i, base=base: (jnp.minimum((i + 1) * hb, nh - 1), base + j)))
        elif kind == "const":
            in_specs.append(pl.BlockSpec(arr.shape, lambda j, i, nd=arr.ndim: (0,) * nd))
        elif kind == "ccol":
            in_specs.append(pl.BlockSpec((arr.shape[0], cw), lambda j, i, base=base: (0, base + j)))
        else:
            raise ValueError(kind)
        args.append(arr)
    out_specs, out_shape = [], []
    for ctot, cw, base, dt in outs:
        out_specs.append(pl.BlockSpec((tm, cw), lambda j, i, base=base: (i, base + j)))
        out_shape.append(jax.ShapeDtypeStruct((nrows, ctot), dt))
    for r, ctot, cw in accs:
        out_specs.append(pl.BlockSpec((r, cw), lambda j, i: (0, j)))
        out_shape.append(jax.ShapeDtypeStruct((r, ctot), F32))
    n_in, n_out = len(ins), len(outs)

    def body(*refs):
        j = pl.program_id(0)
        i = pl.program_id(1)
        vals = [r[...] for r in refs[:n_in]]
        res = fn(i, j, *[v.astype(F32) if v.dtype == BF16 else v for v in vals])
        for r, v in zip(refs[n_in:n_in + n_out], res[:n_out]):
            r[...] = v.astype(r.dtype)
        for r, v in zip(refs[n_in + n_out:], res[n_out:]):
            @pl.when(i == 0)
            def _(r=r, v=v):
                r[...] = v

            @pl.when(i > 0)
            def _(r=r, v=v):
                r[...] += v

    res = pl.pallas_call(
        body, name=name, grid=(ncol, nt), in_specs=in_specs, out_specs=out_specs, out_shape=out_shape,
        compiler_params=pltpu.CompilerParams(dimension_semantics=("arbitrary", "arbitrary")),
    )(*args)
    return res


def _shift_down(xcat, k):
    return xcat if k == 0 else pltpu.roll(xcat, k, 0)


def _shift_up(xcat, k):
    return xcat if k == 0 else pltpu.roll(xcat, xcat.shape[0] - k, 0)


def _with_prev(i, halo, x):
    return jnp.concatenate([jnp.where(i == 0, 0.0, halo), x], axis=0)


def _with_next(i, nt, x, halo):
    return jnp.concatenate([x, jnp.where(i == nt - 1, 0.0, halo)], axis=0)


def _rms_core(x, g):
    r = lax.rsqrt(jnp.mean(x * x, axis=-1, keepdims=True) + EPS)
    return x * r * g


def _rms_post(du, xv, drv, gv):
    _, vjp = jax.vjp(_rms_core, xv, gv)
    dx, dg = vjp(du)
    return [drv + dx, dg]


def _final_loss(x, target, g):
    S = x.shape[0]

    def fn(i, j, xv, tv, gv):
        def f(xx, gg):
            err = _rms_core(xx, gg) - tv
            return 0.5 * jnp.sum(err * err) / D

        loss, vjp = jax.vjp(f, xv, gv)
        dx, dg = vjp(jnp.ones((), F32))
        return [dx, dg, jnp.zeros((1, LANES), F32) + loss]

    return _rows("final_loss", fn, [("row", x, None, 0), ("row", target, None, 0), ("const", g, None, 0)],
                 [(D, D, 0, F32)], [(1, D, D), (1, LANES, LANES)], tm=256, nrows=S)


def _attn_valid(n):
    qi = _iota((WIN, 2 * WIN), 0)
    kk = _iota((WIN, 2 * WIN), 1)
    rel = qi + WIN - kk
    return (rel >= 0) & (rel <= WIN) & ((kk >= WIN) | (n > 0))


def _attn_block(q, kp, kc, vp, vc, b0, b1, valid):
    k = jnp.concatenate([kp, kc], axis=0)
    v = jnp.concatenate([vp, vc], axis=0)
    lo = _iota((WIN, LANES), 1) < HD
    scale = 1.0 / math.sqrt(HD)
    os_, ls_ = [], []
    for hh, b in ((0, b0), (1, b1)):
        qm = jnp.where(lo if hh == 0 else ~lo, q, 0.0)
        s = _bdot_nt(qm, k) * scale + b
        s = jnp.where(valid, s, NEG)
        m = lax.stop_gradient(jnp.max(s, axis=1, keepdims=True))
        p = jnp.exp(s - m)
        l = jnp.sum(p, axis=1, keepdims=True)
        os_.append(_bdot_nn(p, v) / l)
        ls_.append(m + jnp.log(l))
    return jnp.where(lo, os_[0], os_[1]), jnp.where(lo, ls_[0], ls_[1])


def _residue_rows(r, d):
    return pl.ds(0, WIN) if d == 1 else pl.ds(r, WIN, stride=d)


def _for_residues(d, fn):
    if d == 1:
        fn(0, 0)
    else:
        lax.fori_loop(0, d, fn, 0, unroll=min(d, 8))


def _pairs_per_step(d):
    return 3 if d == 1 else 1


def _bias_table(rel_bias, bucket, gi, name):
    def body(t_ref, b_ref, o_ref):
        h = 6 * gi + pl.program_id(0)
        b = b_ref[...]
        acc = jnp.zeros(b.shape, F32)
        for k in range(REL_BUCKETS):
            acc = jnp.where(b == k, t_ref[k, h], acc)
        o_ref[0] = acc

    return pl.pallas_call(
        body, name=name, grid=(6,),
        in_specs=[pl.BlockSpec(memory_space=pltpu.SMEM), pl.BlockSpec((WIN, 2 * WIN), lambda h: (0, 0))],
        out_specs=pl.BlockSpec((1, WIN, 2 * WIN), lambda h: (h, 0, 0)),
        out_shape=jax.ShapeDtypeStruct((6, WIN, 2 * WIN), F32),
    )(rel_bias, bucket)


def _attn_fwd(pa, bias, gi, name):
    S = pa.shape[0]
    d = DILATIONS[gi]
    bt = WIN * d
    nb = S // bt
    hpw = _pairs_per_step(d)
    bw = hpw * LANES
    cb = 3 * gi // hpw

    def body(q_ref, kp_ref, kc_ref, vp_ref, vc_ref, b_ref, o_ref, l_ref):
        valid = _attn_valid(pl.program_id(1))

        def residue(r, carry):
            sl = _residue_rows(r, d)
            for t in range(hpw):
                ln = pl.ds(t * LANES, LANES)
                o, lse = _attn_block(q_ref[sl, ln], kp_ref[sl, ln], kc_ref[sl, ln], vp_ref[sl, ln], vc_ref[sl, ln],
                                     b_ref[2 * t], b_ref[2 * t + 1], valid)
                o_ref[sl, ln] = o
                l_ref[sl, ln] = lse
            return carry

        _for_residues(d, residue)

    def spec(off, prev):
        if prev:
            return pl.BlockSpec((bt, bw), lambda hp, n: (jnp.maximum(n - 1, 0), off // hpw + cb + hp))
        return pl.BlockSpec((bt, bw), lambda hp, n: (n, off // hpw + cb + hp))

    ospec = pl.BlockSpec((bt, bw), lambda hp, n: (n, hp))
    return pl.pallas_call(
        body, name=name, grid=(3 // hpw, nb),
        in_specs=[spec(0, False), spec(9, True), spec(9, False), spec(18, True), spec(18, False),
                  pl.BlockSpec((2 * hpw, WIN, 2 * WIN), lambda hp, n: (hp, 0, 0))],
        out_specs=[ospec, ospec],
        out_shape=[jax.ShapeDtypeStruct((S, GW), F32)] * 2,
        compiler_params=pltpu.CompilerParams(dimension_semantics=("parallel", "arbitrary")),
    )(pa, pa, pa, pa, pa, bias)


def _attn_bwd(pa, bias, do, dlse, db_in, dqkv, gi, name):
    S = pa.shape[0]
    d = DILATIONS[gi]
    bt = WIN * d
    nb = S // bt
    hpw = _pairs_per_step(d)
    bw = hpw * LANES
    cb = 3 * gi // hpw

    def body(q_ref, kp_ref, kc_ref, vp_ref, vc_ref, b_ref, do_ref, dl_ref, dbi_ref, dqi_ref, dki_ref, dvi_ref,
             dq_ref, dk_ref, dv_ref, db_ref, ck, cv):
        n = pl.program_id(1)

        @pl.when(n == 0)
        def _():
            db_ref[...] = dbi_ref[...]
            ck[...] = jnp.zeros_like(ck)
            cv[...] = jnp.zeros_like(cv)

        @pl.when(n < nb)
        def _():
            f = functools.partial(_attn_block, valid=_attn_valid(n))

            def residue(r, carry):
                sl = _residue_rows(r, d)
                cs = pl.ds(pl.multiple_of(r * WIN, WIN), WIN)
                for t in range(hpw):
                    ln = pl.ds(t * LANES, LANES)
                    _, vjp = jax.vjp(f, q_ref[sl, ln], kp_ref[sl, ln], kc_ref[sl, ln], vp_ref[sl, ln], vc_ref[sl, ln],
                                     b_ref[2 * t], b_ref[2 * t + 1])
                    dq, dkp, dkc, dvp, dvc, db0, db1 = vjp((do_ref[sl, ln], dl_ref[sl, ln]))
                    dq_ref[sl, ln] = dq
                    dk_ref[sl, ln] = ck[cs, ln] + dkp
                    dv_ref[sl, ln] = cv[cs, ln] + dvp
                    ck[cs, ln] = dkc
                    cv[cs, ln] = dvc
                    db_ref[2 * t] += db0
                    db_ref[2 * t + 1] += db1
                return carry

            _for_residues(d, residue)

        @pl.when(n == nb)
        def _():
            def residue(r, carry):
                sl = _residue_rows(r, d)
                cs = pl.ds(pl.multiple_of(r * WIN, WIN), WIN)
                dk_ref[sl, :] = ck[cs, :]
                dv_ref[sl, :] = cv[cs, :]
                return carry

            _for_residues(d, residue)

    def cur(n):
        return jnp.minimum(n, nb - 1)

    def spec(off, prev):
        if prev:
            return pl.BlockSpec((bt, bw), lambda hp, n: (jnp.maximum(cur(n) - 1, 0), off // hpw + cb + hp))
        return pl.BlockSpec((bt, bw), lambda hp, n: (cur(n), off // hpw + cb + hp))

    gspec = pl.BlockSpec((bt, bw), lambda hp, n: (cur(n), hp))
    bspec = pl.BlockSpec((2 * hpw, WIN, 2 * WIN), lambda hp, n: (hp, 0, 0))
    qspec = pl.BlockSpec((bt, bw), lambda hp, n: (cur(n), cb + hp))
    kspec = pl.BlockSpec((bt, bw), lambda hp, n: (jnp.maximum(n - 1, 0), cb + hp))
    dq, dk, dv, db = pl.pallas_call(
        body, name=name, grid=(3 // hpw, nb + 1),
        in_specs=[spec(0, False), spec(9, True), spec(9, False), spec(18, True), spec(18, False),
                  bspec, gspec, gspec, bspec, _ANY, _ANY, _ANY],
        out_specs=[qspec, kspec, kspec, bspec],
        out_shape=[jax.ShapeDtypeStruct((S, AW), F32)] * 3 + [jax.ShapeDtypeStruct((6, WIN, 2 * WIN), F32)],
        scratch_shapes=[pltpu.VMEM((bt, bw), F32), pltpu.VMEM((bt, bw), F32)],
        input_output_aliases={9: 0, 10: 1, 11: 2},
        compiler_params=pltpu.CompilerParams(dimension_semantics=("arbitrary", "arbitrary")),
    )(pa, pa, pa, pa, pa, bias, do, dlse, db_in, *dqkv)
    return (dq, dk, dv), db


def _mix_core(o0, o1, o2, l0, l1, l2):
    m = lax.stop_gradient(jnp.maximum(jnp.maximum(l0, l1), l2))
    e0, e1, e2 = jnp.exp(l0 - m), jnp.exp(l1 - m), jnp.exp(l2 - m)
    return (e0 * o0 + e1 * o1 + e2 * o2) / (e0 + e1 + e2)


def _mix_fwd(os_, ls_, name):
    S = os_[0].shape[0]
    ins = [("row", a, None, 0) for a in (*os_, *ls_)]
    return _rows(name, lambda i, j, *v: [_mix_core(*v)], ins, [(GW, GW, 0, BF16)], tm=256, nrows=S)[0]


def _mix_bwd(os_, ls_, datt, name):
    S = datt.shape[0]

    def fn(i, j, *v):
        _, vjp = jax.vjp(_mix_core, *v[:6])
        return list(vjp(v[6]))

    ins = [("row", a, None, 0) for a in (*os_, *ls_, datt)]
    outs = [(GW, GW, 0, F32)] * 6
    r = _rows(name, fn, ins, outs, tm=256, nrows=S)
    return r[:3], r[3:]


def _t5_bucket(dist):
    max_exact = REL_BUCKETS // 2
    is_small = dist < max_exact
    nf = jnp.maximum(dist, 1).astype(F32)
    large = max_exact + (jnp.log(nf / max_exact) / math.log(REL_MAX_DISTANCE / max_exact)
                         * (REL_BUCKETS - max_exact)).astype(jnp.int32)
    large = jnp.minimum(large, REL_BUCKETS - 1)
    return jnp.where(is_small, dist, large)


def _buckets(d):
    qi = jnp.arange(WIN)[:, None]
    kk = jnp.arange(2 * WIN)[None, :]
    rel = qi + WIN - kk
    return _t5_bucket(jnp.clip(rel, 0, None) * d)


def _pool_cnt(i, tm, w):
    pos = i * tm + _iota((tm, PG), 0) + 1
    return jnp.minimum(pos, w).astype(F32)


def _pool_d(i, tm, halo, u):
    ds = []
    for g, w in enumerate(POOL_WINDOWS):
        ug = u[:, g * PG:(g + 1) * PG]
        s = _with_prev(i, halo[:, g * PG:(g + 1) * PG], ug)
        step = 1
        while step < w:
            s = s + _shift_down(s, step)
            step *= 2
        ds.append(s[HALO:] / _pool_cnt(i, tm, w) - ug)
    return ds


def _pool_lin(d0, d1, d2, d3, w0, w1, w2, w3, scale):
    y = jnp.concatenate([_bdot_nn(d0, w0), _bdot_nn(d1, w1), _bdot_nn(d2, w2), _bdot_nn(d3, w3)], axis=1)
    return y * scale


def _pool_fwd(pb, pw, scale, name):
    S = pb.shape[0]
    tm = 256

    def fn(i, j, halo, u, w, sc):
        ds = _pool_d(i, tm, halo, u)
        return [_pool_lin(*ds, *[w[k].astype(F32) for k in range(4)], sc)]

    return _rows(name, fn, [("prev", pb, D, 0), ("row", pb, None, 0), ("const", pw, None, 0), ("const", scale, None, 0)],
                 [(D, D, 0, BF16)], tm=tm, nrows=S)[0]


def _pool_bwd(pb, pw, scale, dpo, name):
    S = pb.shape[0]
    tm = 256
    nt = S // tm

    def fn1(i, j, halo, u, w, sc, dy):
        ds = _pool_d(i, tm, halo, u)
        _, vjp = jax.vjp(_pool_lin, *ds, *[w[k].astype(F32) for k in range(4)], sc)
        g = vjp(dy)
        e = jnp.concatenate([g[k] / _pool_cnt(i, tm, wd) for k, wd in enumerate(POOL_WINDOWS)], axis=1)
        return [e, jnp.concatenate(g[4:8], axis=0), g[8]]

    e, dpw, dsc = _rows(name + "_a", fn1,
                        [("prev", pb, D, 0), ("row", pb, None, 0), ("const", pw, None, 0), ("const", scale, None, 0),
                         ("row", dpo, None, 0)],
                        [(D, D, 0, F32)], [(4 * PG, PG, PG), (1, D, D)], tm=tm, nrows=S)

    def fn2(i, j, ev, halo):
        outs = []
        for g, w in enumerate(POOL_WINDOWS):
            eg = ev[:, g * PG:(g + 1) * PG]
            s = _with_next(i, nt, eg, halo[:, g * PG:(g + 1) * PG])
            step = 1
            while step < w:
                s = s + _shift_up(s, step)
                step *= 2
            outs.append(s[:tm] - eg * _pool_cnt(i, tm, w))
        return [jnp.concatenate(outs, axis=1)]

    du = _rows(name + "_b", fn2, [("row", e, None, 0), ("next", e, D, 0)], [(D, D, 0, BF16)], tm=tm, nrows=S)[0]
    return du, dpw, dsc


def _conv_taps(i, halo, x, K):
    cat = _with_prev(i, halo, x)
    return [_shift_down(cat, K - 1 - k)[HALO:] for k in range(K)]


def _conv_pre(taps, w, b):
    acc = b
    for k, t in enumerate(taps):
        acc = acc + t * _row_pick(w, k)
    return acc


CW = 256
CWS = 512
CONV_TM = 512


def _ext_taps(i, nt, prev, x, nxt, K):
    cat = jnp.concatenate([jnp.where(i == 0, 0.0, prev), x, jnp.where(i == nt - 1, 0.0, nxt)], axis=0)
    return [_shift_down(cat, K - 1 - k)[HALO:] for k in range(K)]


def _conv_t_rows(dp, w, K, tm):
    acc = jnp.zeros((tm, dp.shape[1]), F32)
    for k in range(K):
        acc = acc + _shift_up(dp, K - 1 - k)[:tm] * _row_pick(w, k)
    return acc


def _ssd_conv_fwd(pc, w, b, name):
    S = pc.shape[0]
    base = D // CWS

    def fn(i, j, halo, x, wv, bv):
        return [_silu(_conv_pre(_conv_taps(i, halo, x, 4), wv, bv))]

    return _rows(name, fn, [("prev", pc, CWS, base), ("row", pc, CWS, base), ("ccol", w, CWS, 0), ("ccol", b, CWS, 0)],
                 [(XBC, CWS, 0, F32)], tm=CONV_TM, nrows=S, ncol=XBC // CWS)[0]


def _ssd_conv_bwd(pc, w, b, dy, name):
    S = pc.shape[0]
    base = D // CWS
    tm = CONV_TM
    nt = S // tm

    def fn(i, j, prev, x, nxt, wv, bv, dyv, dyn):
        taps = _ext_taps(i, nt, prev, x, nxt, 4)
        pre = _conv_pre(taps, wv, bv)
        sg = _sigmoid(pre)
        dye = jnp.concatenate([dyv, jnp.where(i == nt - 1, 0.0, dyn)], axis=0)
        dpre = dye * sg * (1.0 + pre * (1.0 - sg))
        dw = _stack_rows([jnp.sum(dpre[:tm] * t[:tm], axis=0, keepdims=True) for t in taps], 4)
        return [_conv_t_rows(dpre, wv, 4, tm), dw, jnp.sum(dpre[:tm], axis=0, keepdims=True)]

    return _rows(name, fn,
                 [("prev", pc, CWS, base), ("row", pc, CWS, base), ("next", pc, CWS, base), ("ccol", w, CWS, 0),
                  ("ccol", b, CWS, 0), ("row", dy, CWS, 0), ("next", dy, CWS, 0)],
                 [(XBC, CWS, 0, BF16)], [(4, XBC, CWS), (1, XBC, CWS)], tm=tm, nrows=S, ncol=XBC // CWS)


NFC = D_FF // CW


def _ffn_act_fwd(h, w, b, name):
    S = h.shape[0]

    def fn(i, j, ha, a, hv, v, wa, wv, ba, bv):
        pa = _conv_pre(_conv_taps(i, ha, a, 3), wa, ba)
        pv = _conv_pre(_conv_taps(i, hv, v, 3), wv, bv)
        return [_silu(pa) * pv]

    return _rows(name, fn,
                 [("prev", h, CW, 0), ("row", h, CW, 0), ("prev", h, CW, NFC), ("row", h, CW, NFC),
                  ("ccol", w, CW, 0), ("ccol", w, CW, NFC), ("ccol", b, CW, 0), ("ccol", b, CW, NFC)],
                 [(D_FF, CW, 0, BF16)], tm=CONV_TM, nrows=S, ncol=NFC)[0]


def _ffn_act_bwd(h, w, b, df, name):
    S = h.shape[0]
    tm = CONV_TM
    nt = S // tm

    def fn(i, j, pa_, a, na, pv_, v, nv, wa, wv, ba, bv, dfv, dfn):
        ta = _ext_taps(i, nt, pa_, a, na, 3)
        tv = _ext_taps(i, nt, pv_, v, nv, 3)
        pa = _conv_pre(ta, wa, ba)
        pv = _conv_pre(tv, wv, bv)
        sg = _sigmoid(pa)
        dfe = jnp.concatenate([dfv.astype(F32), jnp.where(i == nt - 1, 0.0, dfn.astype(F32))], axis=0)
        dpa = dfe * pv * sg * (1.0 + pa * (1.0 - sg))
        dpv = dfe * pa * sg
        res = [_conv_t_rows(dpa, wa, 3, tm), _conv_t_rows(dpv, wv, 3, tm)]
        for dp, taps in ((dpa, ta), (dpv, tv)):
            res.append(_stack_rows([jnp.sum(dp[:tm] * t[:tm], axis=0, keepdims=True) for t in taps], 3))
        for dp in (dpa, dpv):
            res.append(jnp.sum(dp[:tm], axis=0, keepdims=True))
        return res

    ins = []
    for base in (0, NFC):
        ins += [("prev", h, CW, base), ("row", h, CW, base), ("next", h, CW, base)]
    ins += [("ccol", w, CW, 0), ("ccol", w, CW, NFC), ("ccol", b, CW, 0), ("ccol", b, CW, NFC),
            ("row", df, CW, 0), ("next", df, CW, 0)]
    dha, dhv, dwa, dwv, dba, dbv = _rows(
        name, fn, ins, [(D_FF, CW, 0, BF16)] * 2, [(3, D_FF, CW)] * 2 + [(1, D_FF, CW)] * 2, tm=tm, nrows=S, ncol=NFC)
    return dha, dhv, jnp.concatenate([dwa, dwv], axis=1), jnp.concatenate([dba, dbv], axis=1)


NSLAB = D // LANES
CPS = 2


def _ssd_chunk(xs, Bs, Cs, dtraw, dtb, alog, prev):
    lsz = SSD_CHUNK
    lane = _iota((lsz, LANES), 1)
    row = _iota((lsz, LANES), 0)
    dt = jnp.where(lane < SSD_HEADS, _softplus(dtraw + dtb), 0.0)
    a = dt * (-jnp.exp(alog))
    tril = row >= lane
    a_cs = _fdot(tril.astype(F32), a)
    a_cst = a_cs.T
    a_last = jnp.sum(a, axis=0, keepdims=True)
    lo = lane < HD
    top = row < HD
    cbs = [_bdot_nt(Cs[g], Bs[g]) for g in range(2)]
    ys, news = [], []
    for s in range(NSLAB):
        g = s // (NSLAB // 2)
        cols, lms, dts, als = [], [], [], []
        for hh in range(2):
            h = 2 * s + hh
            col = _lane_pick(a_cs, h)
            seg = col - _row_pick(a_cst, h)
            lms.append(jnp.exp(jnp.where(tril, seg, NEG)))
            cols.append(col)
            dts.append(_lane_pick(dt, h))
            als.append(_lane_pick(a_last, h))
        col_x = jnp.where(lo, cols[0], cols[1])
        al_x = jnp.where(lo, als[0], als[1])
        xc = xs[s] * jnp.where(lo, dts[0], dts[1])
        yd = jnp.where(lo, _bdot_nn(cbs[g] * lms[0], xc), _bdot_nn(cbs[g] * lms[1], xc))
        yoff = _bdot_nt(Cs[g], prev[s]) * jnp.exp(col_x)
        ys.append(yd + yoff)
        st = _bdot_tn(xc * jnp.exp(al_x - col_x), Bs[g])
        news.append(prev[s] * jnp.exp(jnp.where(top, als[0], als[1])) + st)
    return ys, news


def _ssd_scan_fwd(xbc_c, pd, dtb, alog, name):
    S = xbc_c.shape[0]
    nc = S // SSD_CHUNK
    rows_ = CPS * SSD_CHUNK

    def body(x_ref, b_ref, c_ref, dt_ref, dtb_ref, al_ref, y_ref, st_ref, state):
        c = pl.program_id(0)

        @pl.when(c == 0)
        def _():
            state[...] = jnp.zeros_like(state)

        prev = [state[s * LANES:(s + 1) * LANES, :] for s in range(NSLAB)]
        for u in range(CPS):
            rw = pl.ds(u * SSD_CHUNK, SSD_CHUNK)
            xs = [x_ref[rw, s * LANES:(s + 1) * LANES] for s in range(NSLAB)]
            Bs = [b_ref[rw, g * SSD_N:(g + 1) * SSD_N] for g in range(2)]
            Cs = [c_ref[rw, g * SSD_N:(g + 1) * SSD_N] for g in range(2)]
            for s in range(NSLAB):
                st_ref[u, s * LANES:(s + 1) * LANES, :] = prev[s]
            ys, prev = _ssd_chunk(xs, Bs, Cs, dt_ref[rw, :].astype(F32), dtb_ref[...], al_ref[...], prev)
            for s in range(NSLAB):
                y_ref[rw, s * LANES:(s + 1) * LANES] = ys[s]
        for s in range(NSLAB):
            state[s * LANES:(s + 1) * LANES, :] = prev[s]

    return pl.pallas_call(
        body, name=name, grid=(nc // CPS,),
        in_specs=[pl.BlockSpec((rows_, D), lambda c: (c, 0)),
                  pl.BlockSpec((rows_, 2 * SSD_N), lambda c: (c, D // (2 * SSD_N))),
                  pl.BlockSpec((rows_, 2 * SSD_N), lambda c: (c, D // (2 * SSD_N) + 1)),
                  pl.BlockSpec((rows_, LANES), lambda c: (c, 0)),
                  pl.BlockSpec((1, LANES), lambda c: (0, 0)), pl.BlockSpec((1, LANES), lambda c: (0, 0))],
        out_specs=[pl.BlockSpec((rows_, D), lambda c: (c, 0)), pl.BlockSpec((CPS, D, SSD_N), lambda c: (c, 0, 0))],
        out_shape=[jax.ShapeDtypeStruct((S, D), F32), jax.ShapeDtypeStruct((nc, D, SSD_N), F32)],
        scratch_shapes=[pltpu.VMEM((D, SSD_N), F32)],
        compiler_params=pltpu.CompilerParams(dimension_semantics=("arbitrary",)),
    )(xbc_c, xbc_c, xbc_c, pd, dtb, alog)


def _ssd_scan_bwd(xbc_c, pd, dtb, alog, states, dy, dxs_skip, name):
    S = xbc_c.shape[0]
    nc = S // SSD_CHUNK
    rows_ = CPS * SSD_CHUNK

    def body(x_ref, b_ref, c_ref, dt_ref, dtb_ref, al_ref, st_ref, dy_ref, sk_ref,
             dx_ref, ddt_ref, ddtb_ref, dal_ref, dstate):
        c = pl.program_id(0)

        @pl.when(c == 0)
        def _():
            dstate[...] = jnp.zeros_like(dstate)
            ddtb_ref[...] = jnp.zeros_like(ddtb_ref)
            dal_ref[...] = jnp.zeros_like(dal_ref)

        dnew = [dstate[s * LANES:(s + 1) * LANES, :] for s in range(NSLAB)]
        for u in reversed(range(CPS)):
            rw = pl.ds(u * SSD_CHUNK, SSD_CHUNK)
            xs = [x_ref[rw, s * LANES:(s + 1) * LANES] for s in range(NSLAB)]
            Bs = [b_ref[rw, g * SSD_N:(g + 1) * SSD_N] for g in range(2)]
            Cs = [c_ref[rw, g * SSD_N:(g + 1) * SSD_N] for g in range(2)]
            prev = [st_ref[u, s * LANES:(s + 1) * LANES, :] for s in range(NSLAB)]
            _, vjp = jax.vjp(_ssd_chunk, xs, Bs, Cs, dt_ref[rw, :].astype(F32), dtb_ref[...], al_ref[...], prev)
            dys = [dy_ref[rw, s * LANES:(s + 1) * LANES] for s in range(NSLAB)]
            dxs, dBs, dCs, ddt, ddtb, dal, dnew = vjp((dys, dnew))
            for s in range(NSLAB):
                dx_ref[rw, s * LANES:(s + 1) * LANES] = dxs[s] + sk_ref[rw, s * LANES:(s + 1) * LANES]
            for g in range(2):
                dx_ref[rw, D + g * SSD_N:D + (g + 1) * SSD_N] = dBs[g]
                dx_ref[rw, D + 2 * SSD_N + g * SSD_N:D + 2 * SSD_N + (g + 1) * SSD_N] = dCs[g]
            ddt_ref[rw, :] = ddt
            ddtb_ref[...] += ddtb
            dal_ref[...] += dal
        for s in range(NSLAB):
            dstate[s * LANES:(s + 1) * LANES, :] = dnew[s]

    def rv(c):
        return nc // CPS - 1 - c

    return pl.pallas_call(
        body, name=name, grid=(nc // CPS,),
        in_specs=[pl.BlockSpec((rows_, D), lambda c: (rv(c), 0)),
                  pl.BlockSpec((rows_, 2 * SSD_N), lambda c: (rv(c), D // (2 * SSD_N))),
                  pl.BlockSpec((rows_, 2 * SSD_N), lambda c: (rv(c), D // (2 * SSD_N) + 1)),
                  pl.BlockSpec((rows_, LANES), lambda c: (rv(c), 0)),
                  pl.BlockSpec((1, LANES), lambda c: (0, 0)), pl.BlockSpec((1, LANES), lambda c: (0, 0)),
                  pl.BlockSpec((CPS, D, SSD_N), lambda c: (rv(c), 0, 0)),
                  pl.BlockSpec((rows_, D), lambda c: (rv(c), 0)),
                  pl.BlockSpec((rows_, D), lambda c: (rv(c), 0))],
        out_specs=[pl.BlockSpec((rows_, XBC), lambda c: (rv(c), 0)),
                   pl.BlockSpec((rows_, LANES), lambda c: (rv(c), 0)),
                   pl.BlockSpec((1, LANES), lambda c: (0, 0)), pl.BlockSpec((1, LANES), lambda c: (0, 0))],
        out_shape=[jax.ShapeDtypeStruct((S, XBC), F32), jax.ShapeDtypeStruct((S, LANES), F32),
                   jax.ShapeDtypeStruct((1, LANES), F32), jax.ShapeDtypeStruct((1, LANES), F32)],
        scratch_shapes=[pltpu.VMEM((D, SSD_N), F32)],
        compiler_params=pltpu.CompilerParams(dimension_semantics=("arbitrary",)),
    )(xbc_c, xbc_c, xbc_c, pd, dtb, alog, states, dy, dxs_skip)


def _ssd_post_core(y, xs, z, d128, nw):
    tm = y.shape[0]
    ex = (_iota((LANES, D), 1) // HD == _iota((LANES, D), 0)).astype(F32)
    d_x = jnp.sum(_fdot(jnp.broadcast_to(d128, (8, LANES)), ex), axis=0, keepdims=True) * 0.125
    y2 = (y + d_x * xs) * _silu(z)
    lo = _iota((tm, D), 1) < D // 2
    sq = y2 * y2
    ms0 = jnp.sum(jnp.where(lo, sq, 0.0), axis=-1, keepdims=True) / (D // 2)
    ms1 = jnp.sum(jnp.where(lo, 0.0, sq), axis=-1, keepdims=True) / (D // 2)
    r = jnp.where(lo, lax.rsqrt(ms0 + EPS), lax.rsqrt(ms1 + EPS))
    return y2 * r * nw


def _ssd_post_ins(y, xbc_c, pc, d128, nw):
    return [("row", y, None, 0), ("row", xbc_c, D, 0), ("row", pc, D, 0), ("const", d128, None, 0), ("const", nw, None, 0)]


def _ssd_post_fwd(y, xbc_c, pc, d128, nw, name):
    S = y.shape[0]
    return _rows(name, lambda i, j, *v: [_ssd_post_core(*v)], _ssd_post_ins(y, xbc_c, pc, d128, nw),
                 [(D, D, 0, BF16)], tm=128, nrows=S)[0]


def _ssd_post_bwd(y, xbc_c, pc, d128, nw, dout, name):
    S = y.shape[0]

    def fn(i, j, *v):
        _, vjp = jax.vjp(_ssd_post_core, *v[:5])
        return list(vjp(v[5]))

    return _rows(name, fn, _ssd_post_ins(y, xbc_c, pc, d128, nw) + [("row", dout, None, 0)],
                 [(D, D, 0, F32), (D, D, 0, F32), (D, D, 0, BF16)], [(1, LANES, LANES), (1, D, D)], tm=128, nrows=S)


def _gates_core(g0, g1, g2, b0, b1, b2, ya, yb, yc):
    return _sigmoid(g0 + b0) * ya + _sigmoid(g1 + b1) * yb + _sigmoid(g2 + b2) * yc


def _gate_parts(pdv, bv):
    gp = pltpu.roll(pdv, SEC_D - 16, 1)
    return [gp[:, k * D:(k + 1) * D] for k in range(3)] + [bv[:, k * D:(k + 1) * D] for k in range(3)]


def _gates_pre(pdv, a, b, c, bv):
    return _gates_core(*_gate_parts(pdv, bv), a, b, c)


def _gates_post(dm, pdv, a, b, c, bv):
    _, vjp = jax.vjp(_gates_core, *_gate_parts(pdv, bv), a, b, c)
    g = vjp(dm)
    return [g[6], g[7], g[8], jnp.concatenate(g[0:3], axis=1), jnp.concatenate(g[3:6], axis=1)]


def _adamw(w, g, m, v, name):
    rows, C = w.shape
    tm = _pick(rows, [t for t in (512, 256, 128, 64, 32, 16, 8) if t * C <= ADAM_TILE])

    def fn(i, j, wv, gv, mv, vv):
        m2 = ADAM_B1 * mv + (1.0 - ADAM_B1) * gv
        v2 = ADAM_B2 * vv + (1.0 - ADAM_B2) * jnp.square(gv)
        m_hat = m2 / (1.0 - ADAM_B1 ** ADAM_STEP)
        v_hat = v2 / (1.0 - ADAM_B2 ** ADAM_STEP)
        delta = -ADAM_LR * (m_hat / (jnp.sqrt(v_hat) + ADAM_EPS) + ADAM_WD * wv)
        return [delta, m2, v2]

    return _rows(name, fn, [("row", a, None, 0) for a in (w, g, m, v)], [(C, C, 0, F32)] * 3, tm=tm, nrows=rows)


def _position():
    return lax.axis_index("x"), lax.axis_index("y"), lax.axis_index("c")


def _other_chips(x, y):
    return [(1 - x, y), (x, 1 - y), (1 - x, 1 - y)]


_HBM = pl.BlockSpec(memory_space=pltpu.HBM)


def _gather_parts(half, lo, n):
    def copies(p_ref, out_ref, send_sems, recv_sems):
        x, y, c = _position()
        sibling = (x, y, 1 - c)
        chips = _other_chips(x, y)

        def slab(chip, h):
            return out_ref.at[2 * chip[0] + chip[1], pl.ds(h * half + lo, n), :]

        def copy(k, src, dst, to):
            return pltpu.make_async_remote_copy(src_ref=src, dst_ref=dst, send_sem=send_sems.at[k],
                                                recv_sem=recv_sems.at[k], device_id=to, device_id_type=MESH)

        first = [copy(j, p_ref.at[pl.ds(c * half + lo, n), :], slab((x, y), c), (*chip, c)) for j, chip in enumerate(chips)]
        passed = [copy(3 + j, slab(chip, c), slab(chip, c), sibling) for j, chip in enumerate(chips)]
        from_chips = [copy(j, slab(chip, c), slab(chip, c), (x, y, c)) for j, chip in enumerate(chips)]
        from_sibling = [copy(3 + j, slab(chip, 1 - c), slab(chip, 1 - c), (x, y, c)) for j, chip in enumerate(chips)]
        return first, passed, from_chips, from_sibling

    def start(ins, outs, scr):
        for cp in copies(ins[0], outs[0], *scr)[0]:
            cp.start()

    def finish(ins, outs, scr):
        first, passed, from_chips, from_sibling = copies(ins[0], outs[0], *scr)
        for j in range(3):
            from_chips[j].wait_recv()
            passed[j].start()
        for cp in from_sibling:
            cp.wait_recv()
        for cp in first + passed:
            cp.wait_send()

    return start, finish


def _rs_chip_parts(lo, n):
    def copies(h_ref, out_ref, send_sems, recv_sems):
        x, y, c = _position()
        return [pltpu.make_async_remote_copy(src_ref=h_ref.at[2 * chip[0] + chip[1], pl.ds(lo, n), :],
                                             dst_ref=out_ref.at[j, pl.ds(lo, n), :],
                                             send_sem=send_sems.at[j], recv_sem=recv_sems.at[j],
                                             device_id=(*chip, c), device_id_type=MESH)
                for j, chip in enumerate(_other_chips(x, y))]

    def start(ins, outs, scr):
        for cp in copies(ins[0], outs[0], *scr):
            cp.start()

    def finish(ins, outs, scr):
        for cp in copies(ins[0], outs[0], *scr):
            cp.wait()

    return start, finish


class _Stream:
    def __init__(self, src, buf, parts, nsem, units, name):
        self.src, self.buf, self.parts, self.nsem, self.name = src, buf, parts, nsem, name
        self.next, self.units = 0, units

    def _scratch(self):
        return [pltpu.SemaphoreType.DMA((self.nsem,)), pltpu.SemaphoreType.DMA((self.nsem,))]

    def _take(self, units):
        units = min(units, self.units - self.next)
        lo = self.next * 16
        self.next += units
        return lo, units * 16

    def _set(self, outs):
        self.buf = outs[0]

    def hook(self, units):
        lo, n = self._take(units)
        if n == 0:
            return None
        start, finish = self.parts(lo, n)
        return _Hook([self.src, self.buf], [jax.ShapeDtypeStruct(self.buf.shape, self.buf.dtype)], {1: 0},
                     self._scratch(), start, finish, self._set)

    def drain(self):
        lo, n = self._take(self.units)
        if n:
            start, finish = self.parts(lo, n)

            def body(s_ref, b_ref, o_ref, send_sems, recv_sems):
                args = ((s_ref, b_ref), (o_ref,), (send_sems, recv_sems))
                start(*args)
                finish(*args)

            self.buf = pl.pallas_call(
                body, name=self.name, in_specs=[_ANY, _ANY], out_specs=_ANY,
                out_shape=jax.ShapeDtypeStruct(self.buf.shape, self.buf.dtype),
                scratch_shapes=self._scratch(), input_output_aliases={1: 0},
            )(self.src, self.buf)
        return self.buf


def _rs_pair_exchange(g, name):
    _, R, C = g.shape
    Rh = R // 2

    def body(g_ref, out_ref, send_sem, recv_sem):
        x, y, c = _position()
        src = g_ref.at[pl.ds(0, 4), pl.ds((1 - c) * Rh, Rh), :]
        cp = pltpu.make_async_remote_copy(src_ref=src, dst_ref=out_ref, send_sem=send_sem,
                                          recv_sem=recv_sem, device_id=(x, y, 1 - c), device_id_type=MESH)
        cp.start()
        cp.wait()

    return pl.pallas_call(
        body, name=name, in_specs=[_HBM], out_specs=_HBM,
        out_shape=jax.ShapeDtypeStruct((4, Rh, C), g.dtype),
        scratch_shapes=[pltpu.SemaphoreType.DMA, pltpu.SemaphoreType.DMA],
    )(g)


def _rs_swap(r, name):
    Rh, C = r.shape

    def body(r_ref, out_ref, send_sem, recv_sem):
        x, y, c = _position()
        cp = pltpu.make_async_remote_copy(src_ref=r_ref, dst_ref=out_ref, send_sem=send_sem,
                                          recv_sem=recv_sem, device_id=(x, y, 1 - c), device_id_type=MESH)
        cp.start()
        cp.wait()

    return pl.pallas_call(
        body, name=name, in_specs=[_HBM], out_specs=_HBM,
        out_shape=jax.ShapeDtypeStruct((Rh, C), r.dtype),
        scratch_shapes=[pltpu.SemaphoreType.DMA, pltpu.SemaphoreType.DMA],
    )(r)


def _rs_add_pair(g, recv, cidx, name):
    _, R, C = g.shape
    Rh = R // 2
    tm = _pick(Rh, (400, 280, 200, 160, 80, 40, 16, 8))
    nt = Rh // tm

    def body(c_ref, g_ref, r_ref, o_ref):
        o_ref[...] = (g_ref[...].astype(F32) + r_ref[...].astype(F32)).astype(o_ref.dtype)

    return pl.pallas_call(
        body, name=name,
        grid_spec=pltpu.PrefetchScalarGridSpec(
            num_scalar_prefetch=1, grid=(4, nt),
            in_specs=[pl.BlockSpec((1, tm, C), lambda k, i, cr: (k, cr[0] * nt + i, 0)),
                      pl.BlockSpec((1, tm, C), lambda k, i, cr: (k, i, 0))],
            out_specs=pl.BlockSpec((1, tm, C), lambda k, i, cr: (k, i, 0))),
        out_shape=jax.ShapeDtypeStruct((4, Rh, C), BF16),
    )(cidx, g, recv)


def _rs_add_chips(h, recv, chip_idx, name):
    _, Rh, C = h.shape
    tm = _pick(Rh, (400, 280, 200, 160, 80, 40, 16, 8))

    def body(c_ref, h_ref, r_ref, o_ref):
        acc = h_ref[0].astype(F32)
        for j in range(3):
            acc = acc + r_ref[j].astype(F32)
        o_ref[...] = acc

    return pl.pallas_call(
        body, name=name,
        grid_spec=pltpu.PrefetchScalarGridSpec(
            num_scalar_prefetch=1, grid=(Rh // tm,),
            in_specs=[pl.BlockSpec((1, tm, C), lambda i, cr: (cr[0], i, 0)), pl.BlockSpec((3, tm, C), lambda i, cr: (0, i, 0))],
            out_specs=pl.BlockSpec((tm, C), lambda i, cr: (i, 0))),
        out_shape=jax.ShapeDtypeStruct((Rh, C), F32),
    )(chip_idx, h, recv)


def _all_reduce_small(vec, name):
    n, C = vec.shape

    def body(v_ref, out_ref, buf, send_sems, recv_sems):
        x, y, c = _position()

        def flip(k):
            return ((1 - x) if k & 4 else x, (1 - y) if k & 2 else y, (1 - c) if k & 1 else c)

        def idx(p):
            return 4 * p[0] + 2 * p[1] + p[2]

        me = idx((x, y, c))
        buf[me] = v_ref[...]
        cps = [pltpu.make_async_remote_copy(src_ref=v_ref, dst_ref=buf.at[me], send_sem=send_sems.at[k - 1],
                                            recv_sem=recv_sems.at[k - 1], device_id=flip(k), device_id_type=MESH)
               for k in range(1, 8)]
        for cp in cps:
            cp.start()
        for k in range(1, 8):
            pltpu.make_async_remote_copy(src_ref=v_ref, dst_ref=buf.at[idx(flip(k))], send_sem=send_sems.at[k - 1],
                                         recv_sem=recv_sems.at[k - 1], device_id=flip(k), device_id_type=MESH).wait_recv()
        for cp in cps:
            cp.wait_send()
        acc = buf[0]
        for s in range(1, 8):
            acc = acc + buf[s]
        out_ref[...] = acc

    return pl.pallas_call(
        body, name=name,
        in_specs=[pl.BlockSpec(memory_space=pltpu.VMEM)], out_specs=pl.BlockSpec(memory_space=pltpu.VMEM),
        out_shape=jax.ShapeDtypeStruct((n, C), F32),
        scratch_shapes=[pltpu.VMEM((8, n, C), F32), pltpu.SemaphoreType.DMA((7,)), pltpu.SemaphoreType.DMA((7,))],
    )(vec)


BIG = (("w_in", (D, IN_WIDTH // 4), "cols"), ("w_a", (GW, D // 4), "cols"), ("pool_w", (4, PG // 4, PG), "pool"),
       ("w_b", (D // 4, D), "rows"), ("w_c", (D // 4, D), "rows"), ("w_o", (D // 4, D), "rows"),
       ("ffn_w_up", (D, 2 * D_FF // 4), "cols"), ("ffn_w_down", (D_FF // 4, D), "rows"))
def _pack_rows(s):
    k = math.prod(s) // D
    return -(-k // 16) * 16, k


PACK_ROWS = sum(_pack_rows(s)[0] for _, s, _ in BIG)
PACK_PAD = -(-PACK_ROWS // 32) * 32


def _pad_rows(v, rows):
    pad = [(0, 0)] * v.ndim
    pad[-2] = (0, rows - v.shape[-2])
    return jnp.pad(v, pad) if rows > v.shape[-2] else v


def _pack_blocks(blocks, dtype):
    lead = blocks["w_in"].shape[:-2]
    flat = []
    for n, s, how in BIG:
        v = blocks[n].astype(dtype)
        if how == "cols":
            v = jnp.swapaxes(v, -1, -2)
        flat.append(_pad_rows(v.reshape(*lead, -1, D), _pack_rows(s)[0]))
    flat.append(jnp.zeros((*lead, PACK_PAD - PACK_ROWS, D), dtype))
    return jnp.concatenate(flat, axis=-2)


def _unpack_blocks(pack):
    out, r = {}, 0
    for n, s, how in BIG:
        rows, k = _pack_rows(s)
        v = pack[r:r + k, :]
        out[n] = v.reshape(s[1], s[0]).T if how == "cols" else v.reshape(s)
        r += rows
    return out


def _operands(allp):
    out, r = {}, 0
    for n, s, how in BIG:
        rows, k = _pack_rows(s)
        v = allp[:, r:r + k, :]
        if how == "cols":
            out[n] = v.reshape(4 * s[1], s[0])
        elif how == "rows":
            out[n] = v.reshape(4 * s[0], s[1])
        else:
            out[n] = v.reshape(4, *s).transpose(1, 0, 2, 3).reshape(4, PG, PG)
        r += rows
    return out


def _pack_operands(g, dtype):
    flat = []
    for n, s, how in BIG:
        v = g[n].astype(dtype)
        if how == "pool":
            v = v.reshape(4, 4, s[1], s[2]).transpose(1, 0, 2, 3)
        flat.append(_pad_rows(v.reshape(4, -1, D), _pack_rows(s)[0]))
    flat.append(jnp.zeros((4, PACK_PAD - PACK_ROWS, D), dtype))
    return jnp.concatenate(flat, axis=1)


def _layer_fwd(x, w, sm, bias, hk):
    pa, u = _mmf(None, w["in_a"], tb=True, pre=(_rms_core, [x], [sm["ln1_g"]]), name="in_a", tm=1024, hook=hk("in_a"))
    pb = _mm(u, w["in_b"], tb=True, out_dtype=BF16, name="in_b", hook=hk("in_b"))
    pc = _mm(u, w["in_c"], tb=True, out_dtype=BF16, name="in_c", hook=hk("in_c"))
    pd = _mm(u, w["in_d"], tb=True, out_dtype=BF16, name="in_d", hook=hk("in_d"))
    os_, ls_ = [], []
    for gi in range(3):
        o, l = _attn_fwd(pa, bias[gi], gi, "attn_fwd%d" % gi)
        os_.append(o)
        ls_.append(l)
    att = _mix_fwd(os_, ls_, "mix_fwd")
    ya = _mm(att, w["w_a"], tb=True, out_dtype=BF16, name="mm_wa")
    pool_o = _pool_fwd(pb, w["pool_w"], sm["pool_scale"], "pool_fwd")
    yb = _mm(pool_o, w["w_b"], out_dtype=BF16, name="mm_wb")
    xbc_c = _ssd_conv_fwd(pc, sm["ssd_conv_w"], sm["ssd_conv_b"], "ssd_conv_fwd")
    y_scan, states = _ssd_scan_fwd(xbc_c, pd, sm["ssd_dt_bias"], sm["ssd_a_log"], "ssd_scan_fwd")
    ssd_o = _ssd_post_fwd(y_scan, xbc_c, pc, sm["ssd_d"], sm["ssd_norm_w"], "ssd_post_fwd")
    yc = _mm(ssd_o, w["w_c"], out_dtype=BF16, name="mm_wc")
    x1, merged = _mmf(None, w["w_o"], add=x, pre=(_gates_pre, [pd, ya, yb, yc], [sm["b_gate"]]), name="mm_wo", tm=256,
                      hook=hk("mm_wo"))
    h, u2 = _mmf(None, w["ffn_w_up"], tb=True, pre=(_rms_core, [x1], [sm["ln2_g"]]), out_dtype=BF16, name="mm_up",
                 tm=1024, hook=hk("mm_up"))
    f = _ffn_act_fwd(h, sm["ffn_conv_w"], sm["ffn_conv_b"], "ffn_act_fwd")
    x2 = _mm(f, w["ffn_w_down"], add=x1, name="mm_down", hook=hk("mm_down"))
    saved = dict(x=x, u=u, pa=pa, pb=pb, pc=pc, pd=pd, os=os_, ls=ls_, att=att, ya=ya, yb=yb, yc=yc, pool_o=pool_o,
                 xbc_c=xbc_c, y_scan=y_scan, states=states, ssd_o=ssd_o, merged=merged, x1=x1, u2=u2, h=h, f=f)
    return x2, saved


def _layer_bwd(dx2, w, sm, bias, dbs, sv, hk):
    gw, gs = {}, {}
    S = dx2.shape[0]

    def gmm(a, b, name):
        return _mm(a, b, ta=True, out_dtype=BF16, name=name, hook=hk(name))

    df = _mm(dx2, w["ffn_w_down"], tb=True, out_dtype=BF16, name="d_f", hook=hk("d_f"))
    gw["ffn_w_down"] = gmm(sv["f"], dx2, "g_down")
    dha, dhv, gs["ffn_conv_w"], gs["ffn_conv_b"] = _ffn_act_bwd(sv["h"], sm["ffn_conv_w"], sm["ffn_conv_b"], df, "ffn_act_bwd")
    du2 = _mm(dha, w["up_a"], name="d_u2_a", hook=hk("d_u2_a"))
    dx1, gs["ln2_g"] = _mmf(dhv, w["up_v"], add=du2, name="d_u2_v", tm=256, hook=hk("d_u2_v"),
                            post=(_rms_post, [sv["x1"], dx2], [sm["ln2_g"]], [(D, F32)], [(1, D)]))
    gw["ffn_w_up"] = jnp.concatenate([gmm(dha, sv["u2"], "g_up_a"), gmm(dhv, sv["u2"], "g_up_v")], axis=0)
    dya, dyb, dyc, dgate, gs["b_gate"] = _mmf(
        dx1, w["w_o"], tb=True, name="d_merged", tm=256, hook=hk("d_merged"),
        post=(_gates_post, [sv["pd"], sv["ya"], sv["yb"], sv["yc"]], [sm["b_gate"]],
              [(D, BF16)] * 3 + [(3 * D, BF16)], [(1, 3 * D)]))
    gw["w_o"] = gmm(sv["merged"], dx1, "g_wo")
    dssd_o = _mm(dyc, w["w_c"], tb=True, name="d_ssd_o")
    gw["w_c"] = gmm(sv["ssd_o"], dyc, "g_wc")
    dy_scan, dxs_skip, dz, gs["ssd_d"], gs["ssd_norm_w"] = _ssd_post_bwd(
        sv["y_scan"], sv["xbc_c"], sv["pc"], sm["ssd_d"], sm["ssd_norm_w"], dssd_o, "ssd_post_bwd")
    dxbc_c, ddt, gs["ssd_dt_bias"], gs["ssd_a_log"] = _ssd_scan_bwd(
        sv["xbc_c"], sv["pd"], sm["ssd_dt_bias"], sm["ssd_a_log"], sv["states"], dy_scan, dxs_skip, "ssd_scan_bwd")
    dxbc, gs["ssd_conv_w"], gs["ssd_conv_b"] = _ssd_conv_bwd(sv["pc"], sm["ssd_conv_w"], sm["ssd_conv_b"], dxbc_c, "ssd_conv_bwd")
    dpool_o = _mm(dyb, w["w_b"], tb=True, name="d_pool_o")
    gw["w_b"] = gmm(sv["pool_o"], dyb, "g_wb")
    dpb, dpw, gs["pool_scale"] = _pool_bwd(sv["pb"], w["pool_w"], sm["pool_scale"], dpool_o, "pool_bwd")
    gw["pool_w"] = dpw.reshape(4, PG, PG)
    datt = _mm(dya, w["w_a"], name="d_att")
    gw["w_a"] = gmm(dya, sv["att"], "g_wa")
    dos, dls = _mix_bwd(sv["os"], sv["ls"], datt, "mix_bwd")
    dqkv = tuple(lax.empty((S, AW), F32) for _ in range(3))
    dbs = list(dbs)
    for gi in range(3):
        dqkv, dbs[gi] = _attn_bwd(sv["pa"], bias[gi], dos[gi], dls[gi], dbs[gi], dqkv, gi, "attn_bwd%d" % gi)
    u = sv["u"]
    pieces = [(dqkv[0], "wq"), (dqkv[1], "wk"), (dqkv[2], "wv"), (dpb, "in_b"), (dz, "wz"), (dxbc, "wxbc"),
              (ddt, "wdt"), (dgate, "wgate")]
    du = None
    g_in = []
    for dp, key in pieces:
        if key == pieces[-1][1]:
            dx, gs["ln1_g"] = _mmf(dp, w[key], add=du, name="d_u_" + key, tm=256, hook=hk("d_u_" + key),
                                   post=(_rms_post, [sv["x"], dx1], [sm["ln1_g"]], [(D, F32)], [(1, D)]))
        else:
            du = _mm(dp, w[key], add=du, name="d_u_" + key, hook=hk("d_u_" + key))
        g = gmm(dp, u, "g_in_" + key)
        g_in.append(g[:SSD_HEADS] if key == "wdt" else g)
    gw["w_in"] = jnp.concatenate(g_in, axis=0)
    return dx, gw, gs, dbs


SMALL_LAYER = ("ln1_g", "b_gate", "pool_scale", "ssd_conv_w", "ssd_conv_b", "ssd_dt_bias", "ssd_a_log", "ssd_d",
               "ssd_norm_w", "ln2_g", "ffn_conv_w", "ffn_conv_b")


def _pad_lanes(v):
    return jnp.pad(v, (0, LANES - v.shape[0])).reshape(1, LANES)


def _layer_weights(ops):
    wt = ops["w_in"]
    o1, o2, o3 = SEC_A, SEC_A + SEC_B, SEC_A + SEC_B + SEC_C
    w = dict(ops)
    w["in_a"] = jnp.pad(wt[:o1], ((0, SEC_A_PAD - o1), (0, 0)))
    w["in_b"] = wt[o1:o2]
    w["in_c"] = wt[o2:o3]
    w["in_d"] = jnp.pad(wt[o3:], ((0, SEC_D - (IN_WIDTH - o3)), (0, 0)))
    w["wq"], w["wk"], w["wv"] = wt[:AW], wt[AW:2 * AW], wt[2 * AW:o1]
    w["wz"], w["wxbc"] = wt[o2:o2 + D], wt[o2 + D:o3]
    w["wdt"] = jnp.pad(wt[o3:o3 + SSD_HEADS], ((0, LANES - SSD_HEADS), (0, 0)))
    w["wgate"] = wt[o3 + SSD_HEADS:]
    w["up_a"], w["up_v"] = ops["ffn_w_up"][:D_FF], ops["ffn_w_up"][D_FF:]
    return w


def _layer_small(p, i):
    sm = {n: p[n][i] for n in SMALL_LAYER}
    out = {}
    for n, v in sm.items():
        if n in ("ssd_dt_bias", "ssd_a_log", "ssd_d"):
            out[n] = _pad_lanes(v)
        elif v.ndim == 1:
            out[n] = v.reshape(1, -1)
        else:
            out[n] = v
    return out


def _local_step(x, target, rel_bias, final_g, layer_full, small, fwd_hooks=None, bwd_hooks=None, after_bwd=None):
    nl = small["ln1_g"].shape[0]
    buckets = [_buckets(d).astype(jnp.int32) for d in DILATIONS]
    bias = [_bias_table(rel_bias, buckets[gi], gi, "bias_table%d" % gi) for gi in range(3)]
    no_hooks = lambda i: (lambda name: None)
    fwd_hooks = fwd_hooks or no_hooks
    bwd_hooks = bwd_hooks or no_hooks
    saved, ws, sms = [], [], []
    h = x
    for i in range(nl):
        w = _layer_weights(layer_full(i))
        sm = _layer_small(small, i)
        h, sv = _layer_fwd(h, w, sm, bias, fwd_hooks(i))
        saved.append(sv)
        ws.append(w)
        sms.append(sm)
    dh, dfinal, loss = _final_loss(h, target, final_g.reshape(1, D))
    gws, gss = [None] * nl, [None] * nl
    dbs = [jnp.zeros((6, WIN, 2 * WIN), F32)] * 3
    for i in reversed(range(nl)):
        dh, gws[i], gss[i], dbs = _layer_bwd(dh, ws[i], sms[i], bias, dbs, saved[i], bwd_hooks(i))
        if after_bwd is not None:
            after_bwd(i, gws[i])
    drel = []
    for gi in range(3):
        onehot = jnp.pad(jax.nn.one_hot(buckets[gi].reshape(-1), REL_BUCKETS, dtype=BF16), ((0, 0), (0, LANES - REL_BUCKETS)))
        drel.append(_mm(dbs[gi].reshape(6, WIN * 2 * WIN), onehot, name="g_relb"))
    return loss, dh, gws, gss, dfinal, jnp.concatenate(drel, axis=0)


WEIGHTS = ("rel_bias", "ln1_g", "w_in", "b_gate", "w_a", "pool_w", "pool_scale", "w_b", "ssd_conv_w", "ssd_conv_b",
           "ssd_dt_bias", "ssd_a_log", "ssd_d", "ssd_norm_w", "w_c", "w_o", "ln2_g", "ffn_w_up", "ffn_conv_w",
           "ffn_conv_b", "ffn_w_down", "final_g")
BIG_NAMES = tuple(n for n, _, _ in BIG)
SHARDED_SMALL = {"ssd_conv_w": XBC // 4, "ffn_conv_w": 2 * D_FF // 4}


def _to_rows(flat):
    n = flat.shape[0]
    rows = -(-n // LANES)
    rows = -(-rows // 8) * 8
    return jnp.pad(flat, (0, rows * LANES - n)).reshape(rows, LANES)


def _flatten(tree, names):
    return jnp.concatenate([tree[n].reshape(-1) for n in names])


def _unflatten(flat, shapes, names):
    out, o = {}, 0
    for n in names:
        k = math.prod(shapes[n])
        out[n] = flat[o:o + k].reshape(shapes[n])
        o += k
    return out


def kernel(x, rel_bias, ln1_g, w_in, b_gate, w_a, pool_w, pool_scale, w_b, ssd_conv_w, ssd_conv_b, ssd_dt_bias, ssd_a_log, ssd_d, ssd_norm_w, w_c, w_o, ln2_g, ffn_w_up, ffn_conv_w, ffn_conv_b, ffn_w_down, final_g, loss_target, m_rel_bias, m_ln1_g, m_w_in, m_b_gate, m_w_a, m_pool_w, m_pool_scale, m_w_b, m_ssd_conv_w, m_ssd_conv_b, m_ssd_dt_bias, m_ssd_a_log, m_ssd_d, m_ssd_norm_w, m_w_c, m_w_o, m_ln2_g, m_ffn_w_up, m_ffn_conv_w, m_ffn_conv_b, m_ffn_w_down, m_final_g, v_rel_bias, v_ln1_g, v_w_in, v_b_gate, v_w_a, v_pool_w, v_pool_scale, v_w_b, v_ssd_conv_w, v_ssd_conv_b, v_ssd_dt_bias, v_ssd_a_log, v_ssd_d, v_ssd_norm_w, v_w_c, v_w_o, v_ln2_g, v_ffn_w_up, v_ffn_conv_w, v_ffn_conv_b, v_ffn_w_down, v_final_g):
    W = dict(rel_bias=rel_bias, ln1_g=ln1_g, w_in=w_in, b_gate=b_gate, w_a=w_a, pool_w=pool_w, pool_scale=pool_scale,
             w_b=w_b, ssd_conv_w=ssd_conv_w, ssd_conv_b=ssd_conv_b, ssd_dt_bias=ssd_dt_bias, ssd_a_log=ssd_a_log,
             ssd_d=ssd_d, ssd_norm_w=ssd_norm_w, w_c=w_c, w_o=w_o, ln2_g=ln2_g, ffn_w_up=ffn_w_up,
             ffn_conv_w=ffn_conv_w, ffn_conv_b=ffn_conv_b, ffn_w_down=ffn_w_down, final_g=final_g)
    M = dict(rel_bias=m_rel_bias, ln1_g=m_ln1_g, w_in=m_w_in, b_gate=m_b_gate, w_a=m_w_a, pool_w=m_pool_w,
             pool_scale=m_pool_scale, w_b=m_w_b, ssd_conv_w=m_ssd_conv_w, ssd_conv_b=m_ssd_conv_b,
             ssd_dt_bias=m_ssd_dt_bias, ssd_a_log=m_ssd_a_log, ssd_d=m_ssd_d, ssd_norm_w=m_ssd_norm_w, w_c=m_w_c,
             w_o=m_w_o, ln2_g=m_ln2_g, ffn_w_up=m_ffn_w_up, ffn_conv_w=m_ffn_conv_w, ffn_conv_b=m_ffn_conv_b,
             ffn_w_down=m_ffn_w_down, final_g=m_final_g)
    V = dict(rel_bias=v_rel_bias, ln1_g=v_ln1_g, w_in=v_w_in, b_gate=v_b_gate, w_a=v_w_a, pool_w=v_pool_w,
             pool_scale=v_pool_scale, w_b=v_w_b, ssd_conv_w=v_ssd_conv_w, ssd_conv_b=v_ssd_conv_b,
             ssd_dt_bias=v_ssd_dt_bias, ssd_a_log=v_ssd_a_log, ssd_d=v_ssd_d, ssd_norm_w=v_ssd_norm_w, w_c=v_w_c,
             w_o=v_w_o, ln2_g=v_ln2_g, ffn_w_up=v_ffn_w_up, ffn_conv_w=v_ffn_conv_w, ffn_conv_b=v_ffn_conv_b,
             ffn_w_down=v_ffn_w_down, final_g=v_final_g)
    nl = ln1_g.shape[0]
    px, py, pc_ = _position()
    chip = 2 * px + py
    cidx = jnp.reshape(pc_, (1,)).astype(jnp.int32)
    chip_idx = jnp.reshape(chip, (1,)).astype(jnp.int32)

    placed = {}
    for n, cs in SHARDED_SMALL.items():
        full = jnp.zeros(W[n].shape[:-1] + (4 * cs,), F32)
        full = lax.dynamic_update_slice(full, W[n], (0, 0, chip * cs))
        placed[n] = jnp.where(pc_ == 0, full, 0.0)
    names_sh = tuple(SHARDED_SMALL)
    shapes_sh = {n: placed[n].shape for n in names_sh}
    got = _all_reduce_small(_to_rows(_flatten(placed, names_sh)), "gather_small")
    small = {n: W[n] for n in SMALL_LAYER}
    small.update(_unflatten(got.reshape(-1), shapes_sh, names_sh))

    packs = _pack_blocks({n: W[n] for n in BIG_NAMES}, BF16)

    half = PACK_PAD // 2
    units = half // 16

    def share(weights, total):
        tot = sum(weights.values())
        return {n: math.ceil(total * v / tot) for n, v in weights.items()}

    gathers = {}

    def gather(i):
        if i not in gathers:
            buf = lax.dynamic_update_slice(lax.empty((4, PACK_PAD, D), BF16), packs[i][None], (chip, 0, 0))
            gathers[i] = _Stream(packs[i], buf, functools.partial(_gather_parts, half), 6, units, "gather_w")
        return gathers[i]

    def layer_full(i):
        return _operands(gather(i).drain())

    fwd_share = share(dict(in_a=89, in_b=26, in_c=57, in_d=66, mm_wo=28, mm_up=120, mm_down=46), units)

    def fwd_hooks(i):
        if i + 1 >= nl:
            return lambda name: None
        return lambda name: gather(i + 1).hook(fwd_share[name]) if name in fwd_share else None

    exchanges = {}
    bwd_share = share(dict(d_f=91, g_down=67, d_u2_a=42, d_u2_v=45, g_up_a=52, g_up_v=52, d_merged=29, g_wo=19,
                           d_u_wgate=48, g_in_wgate=41), units)

    def after_bwd(i, gw):
        g = _pack_operands(gw, BF16)
        recv = _rs_pair_exchange(g, "rs_pair")
        hsum = _rs_add_pair(g, recv, cidx, "rs_add_pair")
        exchanges[i] = (hsum, _Stream(hsum, lax.empty((3, half, D), BF16), _rs_chip_parts, 3, units, "rs_chips"))

    def bwd_hooks(i):
        if i + 1 >= nl:
            return lambda name: None
        return lambda name: exchanges[i + 1][1].hook(bwd_share[name]) if name in bwd_share else None

    loss, dx, gws, gss, dfinal, drel = _local_step(x[0], loss_target[0], rel_bias, final_g, layer_full, small,
                                                   fwd_hooks, bwd_hooks, after_bwd)

    grads = {}
    red = []
    for i in range(nl):
        hsum, stream = exchanges[i]
        r = _rs_add_chips(hsum, stream.drain(), chip_idx, "rs_add_chips")
        other = _rs_swap(r, "rs_swap")
        both = jnp.concatenate([jnp.where(pc_ == 0, r, other), jnp.where(pc_ == 0, other, r)], axis=0)
        red.append(_unpack_blocks(both))
    for n in BIG_NAMES:
        grads[n] = jnp.stack([red[i][n] for i in range(nl)], axis=0)

    sg = {}
    for n in SMALL_LAYER:
        sg[n] = jnp.stack([gss[i][n] for i in range(nl)], axis=0)
    for n in ("ssd_dt_bias", "ssd_a_log", "ssd_d"):
        sg[n] = sg[n][:, 0, :SSD_HEADS]
    sg["rel_bias"] = drel[:, :REL_BUCKETS].T
    sg["final_g"] = dfinal.reshape(D)
    sg["loss"] = loss[0, :1]
    names_sg = tuple(sg)
    shapes_sg = {n: ((nl,) + W[n].shape[1:] if n in SMALL_LAYER and n not in SHARDED_SMALL else
                     (placed[n].shape if n in SHARDED_SMALL else sg[n].shape)) for n in names_sg}
    for n in names_sg:
        sg[n] = sg[n].reshape(shapes_sg[n])
    tot = _all_reduce_small(_to_rows(_flatten(sg, names_sg)), "allreduce_small")
    tot = _unflatten(tot.reshape(-1), shapes_sg, names_sg)
    loss_out = tot.pop("loss").reshape(())
    for n, cs in SHARDED_SMALL.items():
        tot[n] = lax.dynamic_slice(tot[n], (0, 0, chip * cs), tot[n].shape[:-1] + (cs,))
    grads.update(tot)

    delta, new_m, new_v = {}, {}, {}
    for n in BIG_NAMES:
        shp = W[n].shape
        r2 = lambda a: a.reshape(-1, shp[-1])
        dl, m2, v2 = _adamw(r2(W[n]), r2(grads[n]), r2(M[n]), r2(V[n]), "adamw_" + n)
        delta[n], new_m[n], new_v[n] = dl.reshape(shp), m2.reshape(shp), v2.reshape(shp)
    names_s = tuple(n for n in WEIGHTS if n not in BIG_NAMES)
    shapes_s = {n: W[n].shape for n in names_s}
    pk = lambda t: _to_rows(_flatten(t, names_s))
    dl, m2, v2 = _adamw(pk(W), pk(grads), pk(M), pk(V), "adamw_small")
    delta.update(_unflatten(dl.reshape(-1), shapes_s, names_s))
    new_m.update(_unflatten(m2.reshape(-1), shapes_s, names_s))
    new_v.update(_unflatten(v2.reshape(-1), shapes_s, names_s))

    return (loss_out, dx[None], *[grads[n] for n in WEIGHTS], *[delta[n] for n in WEIGHTS],
            *[new_m[n] for n in WEIGHTS], *[new_v[n] for n in WEIGHTS])
```

```python
import functools
import math

import jax
import jax.numpy as jnp
from jax import lax
from jax.experimental import pallas as pl
from jax.experimental.pallas import tpu as pltpu

F32 = jnp.float32
BF16 = jnp.bfloat16
MESH = pl.DeviceIdType.MESH

D = 1024
HD = 64
GW = 384
AW = 3 * GW
WIN = 128
DILATIONS = (1, 4, 16)
REL_BUCKETS = 32
REL_MAX_DISTANCE = 2048
POOL_WINDOWS = (2, 4, 8, 16)
PG = 256
SSD_HEADS = 16
SSD_N = 128
SSD_CHUNK = 128
XBC = 1536
D_FF = 2816
EPS = 1e-6
NEG = -1e30
HALO = 16
LANES = 128

SEC_A = 3 * AW
SEC_B = D
SEC_C = D + XBC
SEC_D = 3328
SEC_A_PAD = 3584
IN_WIDTH = SEC_A + SEC_B + SEC_C + 16 + 3 * D

ADAM_LR = 0.001
ADAM_B1 = 0.9
ADAM_B2 = 0.999
ADAM_EPS = 1e-08
ADAM_WD = 0.01
ADAM_STEP = 10
ADAM_TILE = 256 * 1024
MM_VMEM_BYTES = 40 * 1024 * 1024
MM_MAX_OUT_TILE = 1024 * 1024
HBM_BYTES_PER_US = 2.0e6
STEP_US = 0.35
MXU_WIDTH = 256
MXU_FLOPS_PER_US = 0.65e6


_ANY = pl.BlockSpec(memory_space=pl.ANY)


def _pick(d, cands):
    for t in cands:
        if d % t == 0:
            return t
    return d


def _iota(shape, dim):
    return lax.broadcasted_iota(jnp.int32, shape, dim)


def _dg(a, b, ca, cb):
    return lax.dot_general(a.astype(BF16), b.astype(BF16), (((ca,), (cb,)), ((), ())),
                           preferred_element_type=F32)


@jax.custom_vjp
def _bdot_nn(a, b):
    return _dg(a, b, 1, 0)


def _nn_fwd(a, b):
    return _dg(a, b, 1, 0), (a, b)


def _nn_bwd(res, g):
    a, b = res
    return _dg(g, b, 1, 1), _dg(a, g, 0, 0)


_bdot_nn.defvjp(_nn_fwd, _nn_bwd)


@jax.custom_vjp
def _bdot_nt(a, b):
    return _dg(a, b, 1, 1)


def _nt_fwd(a, b):
    return _dg(a, b, 1, 1), (a, b)


def _nt_bwd(res, g):
    a, b = res
    return _dg(g, b, 1, 0), _dg(g, a, 0, 0)


_bdot_nt.defvjp(_nt_fwd, _nt_bwd)


@jax.custom_vjp
def _bdot_tn(a, b):
    return _dg(a, b, 0, 0)


def _tn_fwd(a, b):
    return _dg(a, b, 0, 0), (a, b)


def _tn_bwd(res, g):
    a, b = res
    return _dg(b, g, 1, 1), _dg(a, g, 1, 0)


_bdot_tn.defvjp(_tn_fwd, _tn_bwd)


def _fdot(a, b):
    return jnp.dot(a, b, preferred_element_type=F32, precision=lax.Precision.HIGHEST)


def _sigmoid(x):
    return 0.5 * jnp.tanh(0.5 * x) + 0.5


def _silu(x):
    return x * _sigmoid(x)


def _softplus(x):
    return jnp.maximum(x, 0.0) + jnp.log(1.0 + jnp.exp(-jnp.abs(x)))


def _lane_pick(m, h):
    return jnp.sum(jnp.where(_iota(m.shape, 1) == h, m, 0.0), axis=1, keepdims=True)


def _row_pick(m, h):
    return jnp.sum(jnp.where(_iota(m.shape, 0) == h, m, 0.0), axis=0, keepdims=True)


def _stack_rows(rows, n):
    c = rows[0].shape[1]
    r = _iota((n, c), 0)
    out = jnp.zeros((n, c), F32)
    for k, v in enumerate(rows):
        out = out + jnp.where(r == k, v, 0.0)
    return out


def _mm(a, b, *, ta=False, tb=False, add=None, out_dtype=F32, name, hook=None):
    if ta:
        K, M = a.shape
    else:
        M, K = a.shape
    if tb:
        N, Kb = b.shape
    else:
        Kb, N = b.shape
    assert K == Kb, (a.shape, b.shape, ta, tb)
    tm, tn, tk = _mm_tiles(M, N, K, a.dtype.itemsize, b.dtype.itemsize, jnp.dtype(out_dtype).itemsize,
                           0 if add is None else add.dtype.itemsize)
    ni, nj, nk = M // tm, N // tn, K // tk
    ca = 0 if ta else 1
    cb = 1 if tb else 0
    n_in = 2 if add is None else 3
    n_hin = 0 if hook is None else len(hook.inputs)
    n_hout = 0 if hook is None else len(hook.out_shapes)

    def body(*refs):
        a_ref, b_ref = refs[:2]
        add_ref = None if add is None else refs[2]
        o_ref = refs[n_in + n_hin]
        scr = refs[n_in + n_hin + 1 + n_hout:]
        acc_ref = scr[0] if nk > 1 else None
        hargs = (refs[n_in:n_in + n_hin], refs[n_in + n_hin + 1:n_in + n_hin + 1 + n_hout], scr[1 if nk > 1 else 0:])
        i, j, k = pl.program_id(0), pl.program_id(1), pl.program_id(2)
        if hook is not None:
            @pl.when((i == 0) & (j == 0) & (k == 0))
            def _():
                hook.start(*hargs)

        part = _dg(a_ref[...], b_ref[...], ca, cb)

        def finish(r):
            if add_ref is not None:
                r = r + add_ref[...].astype(F32)
            o_ref[...] = r.astype(o_ref.dtype)

        if nk == 1:
            finish(part)
        else:
            @pl.when(k == 0)
            def _():
                acc_ref[...] = part

            @pl.when((k > 0) & (k < nk - 1))
            def _():
                acc_ref[...] += part

            @pl.when(k == nk - 1)
            def _():
                finish(acc_ref[...] + part)

        if hook is not None:
            @pl.when((i == ni - 1) & (j == nj - 1) & (k == nk - 1))
            def _():
                hook.finish(*hargs)

    a_spec = pl.BlockSpec((tk, tm), lambda i, j, k: (k, i)) if ta else pl.BlockSpec((tm, tk), lambda i, j, k: (i, k))
    b_spec = pl.BlockSpec((tn, tk), lambda i, j, k: (j, k)) if tb else pl.BlockSpec((tk, tn), lambda i, j, k: (k, j))
    in_specs = [a_spec, b_spec]
    args = [a, b]
    if add is not None:
        in_specs.append(pl.BlockSpec((tm, tn), lambda i, j, k: (i, j)))
        args.append(add)
    out_specs = [pl.BlockSpec((tm, tn), lambda i, j, k: (i, j))]
    out_shape = [jax.ShapeDtypeStruct((M, N), out_dtype)]
    scratch = [pltpu.VMEM((tm, tn), F32)] if nk > 1 else []
    aliases = {}
    if hook is not None:
        in_specs += [_ANY] * n_hin
        args += list(hook.inputs)
        out_specs += [_ANY] * n_hout
        out_shape += list(hook.out_shapes)
        scratch += list(hook.scratch)
        aliases = {n_in + hi: 1 + ho for hi, ho in hook.aliases.items()}
    sem = ("parallel", "parallel", "arbitrary") if hook is None else ("arbitrary",) * 3
    res = pl.pallas_call(
        body, name=name, grid=(ni, nj, nk), in_specs=in_specs, out_specs=out_specs, out_shape=out_shape,
        scratch_shapes=scratch, input_output_aliases=aliases,
        compiler_params=pltpu.CompilerParams(dimension_semantics=sem),
    )(*args)
    if hook is not None:
        hook.done(res[1:])
    return res[0]


def _wide(v):
    return v.astype(F32) if v.dtype == BF16 else v


def _mmf(a, b, *, tb=False, add=None, pre=None, post=None, out_dtype=F32, name, tm, hook=None):
    if tb:
        N, K = b.shape
    else:
        K, N = b.shape
    M = pre[1][0].shape[0] if pre else a.shape[0]
    tn = N if post else _pick(N, (512, 256, LANES))
    ni, nj = M // tm, N // tn
    cb = 1 if tb else 0
    pre_fn, pre_rows, pre_consts = pre if pre else (None, [], [])
    post_fn, post_rows, post_consts, post_outs, post_accs = post if post else (None, [], [], [], [])
    hook_in = [] if hook is None else list(hook.inputs)
    hook_out = [] if hook is None else list(hook.out_shapes)

    def row_spec(arr):
        return pl.BlockSpec((tm, arr.shape[1]), lambda i, j: (i, 0))

    def const_spec(arr):
        return pl.BlockSpec(arr.shape, lambda i, j, nd=arr.ndim: (0,) * nd)

    args, in_specs = [], []
    for arr in ([a] if not pre else pre_rows):
        args.append(arr)
        in_specs.append(row_spec(arr))
    for arr in pre_consts:
        args.append(arr)
        in_specs.append(const_spec(arr))
    args.append(b)
    in_specs.append(pl.BlockSpec((tn, K), lambda i, j: (j, 0)) if tb else pl.BlockSpec((K, tn), lambda i, j: (0, j)))
    if add is not None:
        args.append(add)
        in_specs.append(pl.BlockSpec((tm, tn), lambda i, j: (i, j)))
    for arr in post_rows:
        args.append(arr)
        in_specs.append(row_spec(arr))
    for arr in post_consts:
        args.append(arr)
        in_specs.append(const_spec(arr))
    n_main = len(args)
    args += hook_in
    in_specs += [_ANY] * len(hook_in)

    out_shape, out_specs = [], []
    if post:
        for c, dt in post_outs:
            out_shape.append(jax.ShapeDtypeStruct((M, c), dt))
            out_specs.append(pl.BlockSpec((tm, c), lambda i, j: (i, 0)))
        for r, c in post_accs:
            out_shape.append(jax.ShapeDtypeStruct((r, c), F32))
            out_specs.append(pl.BlockSpec((r, c), lambda i, j: (0, 0)))
    else:
        out_shape.append(jax.ShapeDtypeStruct((M, N), out_dtype))
        out_specs.append(pl.BlockSpec((tm, tn), lambda i, j: (i, j)))
    if pre:
        out_shape.append(jax.ShapeDtypeStruct((M, K), BF16))
        out_specs.append(pl.BlockSpec((tm, K), lambda i, j: (i, 0)))
    n_out = len(out_shape)
    out_shape += hook_out
    out_specs += [_ANY] * len(hook_out)
    scratch = ([pltpu.VMEM((tm, K), BF16)] if pre else []) + ([] if hook is None else list(hook.scratch))
    aliases = {} if hook is None else {n_main + hi: n_out + ho for hi, ho in hook.aliases.items()}

    def body(*refs):
        ins, outs, scr = refs[:n_main], refs[len(args):len(args) + n_out], refs[len(args) + len(out_shape):]
        hargs = (refs[n_main:len(args)], refs[len(args) + n_out:len(args) + len(out_shape)], scr[1 if pre else 0:])
        i, j = pl.program_id(0), pl.program_id(1)
        if hook is not None:
            @pl.when((i == 0) & (j == 0))
            def _():
                hook.start(*hargs)

        it = iter(ins)
        if pre:
            rows_ = [next(it) for _ in pre_rows]
            consts_ = [next(it) for _ in pre_consts]

            @pl.when(j == 0)
            def _():
                av = pre_fn(*[_wide(r[...]) for r in rows_], *[_wide(r[...]) for r in consts_]).astype(BF16)
                scr[0][...] = av
                outs[-1][...] = av

            at = scr[0][...]
        else:
            at = next(it)[...]
        p = _dg(at, next(it)[...], 1, cb)
        if add is not None:
            p = p + next(it)[...].astype(F32)
        if post:
            rows_ = [next(it) for _ in post_rows]
            consts_ = [next(it) for _ in post_consts]
            res = post_fn(p, *[_wide(r[...]) for r in rows_], *[_wide(r[...]) for r in consts_])
            for r, v in zip(outs[:len(post_outs)], res[:len(post_outs)]):
                r[...] = v.astype(r.dtype)
            for r, v in zip(outs[len(post_outs):], res[len(post_outs):]):
                @pl.when(i == 0)
                def _(r=r, v=v):
                    r[...] = v

                @pl.when(i > 0)
                def _(r=r, v=v):
                    r[...] += v
        else:
            outs[0][...] = p.astype(outs[0].dtype)
        if hook is not None:
            @pl.when((i == ni - 1) & (j == nj - 1))
            def _():
                hook.finish(*hargs)

    res = pl.pallas_call(
        body, name=name, grid=(ni, nj), in_specs=in_specs, out_specs=out_specs, out_shape=out_shape,
        scratch_shapes=scratch, input_output_aliases=aliases,
        compiler_params=pltpu.CompilerParams(dimension_semantics=("arbitrary", "arbitrary")),
    )(*args)
    if hook is not None:
        hook.done(res[n_out:])
    return res[:n_out]


def _mm_tiles(M, N, K, sa, sb, so, sadd):
    def tiles(d):
        return [t for t in range(LANES, min(d, 2048) + 1, LANES) if d % t == 0] or [d]

    best = None
    for tk in [K] + [t for t in tiles(K) if t < K]:
        for tm in tiles(M):
            for tn in tiles(N):
                vmem = 2 * (tm * tk * sa + tk * tn * sb + tm * tn * (so + sadd)) + (tm * tn * 4 if tk < K else 0)
                if vmem > MM_VMEM_BYTES or tm * tn > MM_MAX_OUT_TILE:
                    continue
                a_reads = 1 if tk == K else N // tn
                traffic = M * K * sa * a_reads + K * N * sb * (M // tm) + M * N * (so + sadd)
                steps = (M // tm) * (N // tn) * (K // tk)
                width = -(-tn // MXU_WIDTH) * MXU_WIDTH
                mxu = 2.0 * M * K * N * (width / tn) / MXU_FLOPS_PER_US
                edge = tm * tk * sa + tk * tn * sb + tm * tn * (so + sadd)
                cost = max(traffic / HBM_BYTES_PER_US, mxu) + steps * STEP_US + edge / HBM_BYTES_PER_US
                if best is None or cost < best[0]:
                    best = (cost, tm, tn, tk)
    assert best is not None, (M, N, K)
    return best[1:]


class _Hook:
    def __init__(self, inputs, out_shapes, aliases, scratch, start, finish, done):
        self.inputs, self.out_shapes, self.aliases, self.scratch = inputs, out_shapes, aliases, scratch
        self.start, self.finish, self.done = start, finish, done


class _Ctx:
    def __init__(self, first, last, row0, rows):
        self.first, self.last, self.row0, self.rows = first, last, row0, rows


def _rows(name, fn, ins, outs, accs=(), *, tm, nrows, ncol=1, chunk=None):
    nt = nrows // tm
    hb = tm // HALO
    nh = nrows // HALO
    ch = chunk or tm
    nch = tm // ch
    ins = [(kind, arr, arr.shape[1] if kind == "row" and cw is None else cw, base) for kind, arr, cw, base in ins]

    def row_of(k):
        return next(q for q, s in enumerate(ins) if s[0] == "row" and s[1] is ins[k][1] and s[2:] == ins[k][2:])
    in_specs, args = [], []
    for kind, arr, cw, base in ins:
        if kind == "row":
            in_specs.append(pl.BlockSpec((tm, cw), lambda j, i, base=base: (i, base + j)))
        elif kind == "prev":
            in_specs.append(pl.BlockSpec((HALO, cw), lambda j, i, base=base: (jnp.maximum(i * hb - 1, 0), base + j)))
        elif kind == "next":
            in_specs.append(pl.BlockSpec((HALO, cw), lambda j, i, base=base: (jnp.minimum((i + 1) * hb, nh - 1), base + j)))
        elif kind in ("const", "raw"):
            in_specs.append(pl.BlockSpec(arr.shape, lambda j, i, nd=arr.ndim: (0,) * nd))
        elif kind == "ccol":
            in_specs.append(pl.BlockSpec((arr.shape[0], cw), lambda j, i, base=base: (0, base + j)))
        else:
            raise ValueError(kind)
        args.append(arr)
    out_specs, out_shape = [], []
    for ctot, cw, base, dt in outs:
        out_specs.append(pl.BlockSpec((tm, cw), lambda j, i, base=base: (i, base + j)))
        out_shape.append(jax.ShapeDtypeStruct((nrows, ctot), dt))
    for r, ctot, cw in accs:
        out_specs.append(pl.BlockSpec((r, cw), lambda j, i: (0, j)))
        out_shape.append(jax.ShapeDtypeStruct((r, ctot), F32))
    n_in, n_out = len(ins), len(outs)

    def body(*refs):
        i = pl.program_id(1)
        in_refs, out_refs, acc_refs = refs[:n_in], refs[n_in:n_in + n_out], refs[n_in + n_out:]
        if acc_refs:
            @pl.when(i == 0)
            def _():
                for r in acc_refs:
                    r[...] = jnp.zeros_like(r)

        whole = {k: (in_refs[k][...] if s[0] == "raw" else _wide(in_refs[k][...]))
                 for k, s in enumerate(ins) if s[0] in ("const", "ccol", "raw")}

        def do_chunk(c, carry):
            r0 = pl.multiple_of(c * ch, ch) if nch > 1 else 0
            rows_ = pl.ds(r0, ch)
            vals = []
            for k, (kind, _, _, _) in enumerate(ins):
                r = in_refs[k]
                if k in whole:
                    vals.append(whole[k])
                    continue
                if kind == "row":
                    v = r[rows_, :]
                elif nch == 1:
                    v = r[...]
                elif kind == "prev":
                    inner = in_refs[row_of(k)][pl.ds(pl.multiple_of(jnp.maximum(r0 - HALO, 0), HALO), HALO), :]
                    v = jnp.where(c == 0, r[...], inner)
                else:
                    inner = in_refs[row_of(k)][pl.ds(pl.multiple_of(jnp.minimum(r0 + ch, tm - HALO), HALO), HALO), :]
                    v = jnp.where(c == nch - 1, r[...], inner)
                vals.append(_wide(v))
            ctx = _Ctx((i == 0) & (c == 0), (i == nt - 1) & (c == nch - 1), i * tm + r0, ch)
            res = fn(ctx, *vals)
            for r, v in zip(out_refs, res[:n_out]):
                r[rows_, :] = v.astype(r.dtype)
            for r, v in zip(acc_refs, res[n_out:]):
                r[...] += v
            return carry

        if nch == 1:
            do_chunk(0, 0)
        else:
            lax.fori_loop(0, nch, do_chunk, 0)

    res = pl.pallas_call(
        body, name=name, grid=(ncol, nt), in_specs=in_specs, out_specs=out_specs, out_shape=out_shape,
        compiler_params=pltpu.CompilerParams(dimension_semantics=("arbitrary", "arbitrary")),
    )(*args)
    return res


def _shift_down(xcat, k):
    return xcat if k == 0 else pltpu.roll(xcat, k, 0)


def _shift_up(xcat, k):
    return xcat if k == 0 else pltpu.roll(xcat, xcat.shape[0] - k, 0)


def _with_prev(ctx, halo, x):
    return jnp.concatenate([jnp.where(ctx.first, 0.0, halo), x], axis=0)


def _with_next(ctx, x, halo):
    return jnp.concatenate([x, jnp.where(ctx.last, 0.0, halo)], axis=0)


def _rms_core(x, g):
    r = lax.rsqrt(jnp.mean(x * x, axis=-1, keepdims=True) + EPS)
    return x * r * g


def _rms_post(du, xv, drv, gv):
    _, vjp = jax.vjp(_rms_core, xv, gv)
    dx, dg = vjp(du)
    return [drv + dx, dg]


def _final_loss(x, target, g):
    S = x.shape[0]

    def fn(ctx, xv, tv, gv):
        def f(xx, gg):
            err = _rms_core(xx, gg) - tv
            return 0.5 * jnp.sum(err * err) / D

        loss, vjp = jax.vjp(f, xv, gv)
        dx, dg = vjp(jnp.ones((), F32))
        return [dx, dg, jnp.zeros((1, LANES), F32) + loss]

    return _rows("final_loss", fn, [("row", x, None, 0), ("row", target, None, 0), ("const", g, None, 0)],
                 [(D, D, 0, F32)], [(1, D, D), (1, LANES, LANES)], tm=256, nrows=S, chunk=CHUNK_WIDE)


def _attn_valid(n):
    qi = _iota((WIN, 2 * WIN), 0)
    kk = _iota((WIN, 2 * WIN), 1)
    rel = qi + WIN - kk
    return (rel >= 0) & (rel <= WIN) & ((kk >= WIN) | (n > 0))


def _attn_block(q, kp, kc, vp, vc, b0, b1, valid):
    k = jnp.concatenate([kp, kc], axis=0)
    v = jnp.concatenate([vp, vc], axis=0)
    lo = _iota((WIN, LANES), 1) < HD
    scale = 1.0 / math.sqrt(HD)
    os_, ls_ = [], []
    for hh, b in ((0, b0), (1, b1)):
        qm = jnp.where(lo if hh == 0 else ~lo, q, 0.0)
        s = _bdot_nt(qm, k) * scale + b
        s = jnp.where(valid, s, NEG)
        m = lax.stop_gradient(jnp.max(s, axis=1, keepdims=True))
        p = jnp.exp(s - m)
        l = jnp.sum(p, axis=1, keepdims=True)
        os_.append(_bdot_nn(p, v) / l)
        ls_.append(m + jnp.log(l))
    return jnp.where(lo, os_[0], os_[1]), jnp.where(lo, ls_[0], ls_[1])


def _residue_rows(r, d):
    return pl.ds(0, WIN) if d == 1 else pl.ds(r, WIN, stride=d)


def _for_residues(d, fn):
    if d == 1:
        fn(0, 0)
    else:
        lax.fori_loop(0, d, fn, 0, unroll=min(d, 8))


def _pairs_per_step(d):
    return 3 if d == 1 else 1


def _bias_table(rel_bias, bucket, gi, name):
    def body(t_ref, b_ref, o_ref):
        h = 6 * gi + pl.program_id(0)
        b = b_ref[...]
        acc = jnp.zeros(b.shape, F32)
        for k in range(REL_BUCKETS):
            acc = jnp.where(b == k, t_ref[k, h], acc)
        o_ref[0] = acc

    return pl.pallas_call(
        body, name=name, grid=(6,),
        in_specs=[pl.BlockSpec(memory_space=pltpu.SMEM), pl.BlockSpec((WIN, 2 * WIN), lambda h: (0, 0))],
        out_specs=pl.BlockSpec((1, WIN, 2 * WIN), lambda h: (h, 0, 0)),
        out_shape=jax.ShapeDtypeStruct((6, WIN, 2 * WIN), F32),
    )(rel_bias, bucket)


def _attn_fwd(pa, bias, gi, name):
    S = pa.shape[0]
    d = DILATIONS[gi]
    bt = WIN * d
    nb = S // bt
    hpw = _pairs_per_step(d)
    bw = hpw * LANES
    cb = 3 * gi // hpw

    def body(q_ref, kp_ref, kc_ref, vp_ref, vc_ref, b_ref, o_ref, l_ref):
        valid = _attn_valid(pl.program_id(1))

        def residue(r, carry):
            sl = _residue_rows(r, d)
            for t in range(hpw):
                ln = pl.ds(t * LANES, LANES)
                o, lse = _attn_block(q_ref[sl, ln], kp_ref[sl, ln], kc_ref[sl, ln], vp_ref[sl, ln], vc_ref[sl, ln],
                                     b_ref[2 * t], b_ref[2 * t + 1], valid)
                o_ref[sl, ln] = o
                l_ref[sl, ln] = lse
            return carry

        _for_residues(d, residue)

    def spec(off, prev):
        if prev:
            return pl.BlockSpec((bt, bw), lambda hp, n: (jnp.maximum(n - 1, 0), off // hpw + cb + hp))
        return pl.BlockSpec((bt, bw), lambda hp, n: (n, off // hpw + cb + hp))

    ospec = pl.BlockSpec((bt, bw), lambda hp, n: (n, hp))
    return pl.pallas_call(
        body, name=name, grid=(3 // hpw, nb),
        in_specs=[spec(0, False), spec(9, True), spec(9, False), spec(18, True), spec(18, False),
                  pl.BlockSpec((2 * hpw, WIN, 2 * WIN), lambda hp, n: (hp, 0, 0))],
        out_specs=[ospec, ospec],
        out_shape=[jax.ShapeDtypeStruct((S, GW), F32)] * 2,
        compiler_params=pltpu.CompilerParams(dimension_semantics=("parallel", "arbitrary")),
    )(pa, pa, pa, pa, pa, bias)


def _attn_bwd(pa, bias, do, dlse, db_in, dqkv, gi, name):
    S = pa.shape[0]
    d = DILATIONS[gi]
    bt = WIN * d
    nb = S // bt
    hpw = _pairs_per_step(d)
    bw = hpw * LANES
    cb = 3 * gi // hpw

    def body(q_ref, kp_ref, kc_ref, vp_ref, vc_ref, b_ref, do_ref, dl_ref, dbi_ref, dqi_ref, dki_ref, dvi_ref,
             dq_ref, dk_ref, dv_ref, db_ref, ck, cv):
        n = pl.program_id(1)

        @pl.when(n == 0)
        def _():
            db_ref[...] = dbi_ref[...]
            ck[...] = jnp.zeros_like(ck)
            cv[...] = jnp.zeros_like(cv)

        @pl.when(n < nb)
        def _():
            f = functools.partial(_attn_block, valid=_attn_valid(n))

            def residue(r, carry):
                sl = _residue_rows(r, d)
                cs = pl.ds(pl.multiple_of(r * WIN, WIN), WIN)
                for t in range(hpw):
                    ln = pl.ds(t * LANES, LANES)
                    _, vjp = jax.vjp(f, q_ref[sl, ln], kp_ref[sl, ln], kc_ref[sl, ln], vp_ref[sl, ln], vc_ref[sl, ln],
                                     b_ref[2 * t], b_ref[2 * t + 1])
                    dq, dkp, dkc, dvp, dvc, db0, db1 = vjp((do_ref[sl, ln], dl_ref[sl, ln]))
                    dq_ref[sl, ln] = dq
                    dk_ref[sl, ln] = ck[cs, ln] + dkp
                    dv_ref[sl, ln] = cv[cs, ln] + dvp
                    ck[cs, ln] = dkc
                    cv[cs, ln] = dvc
                    db_ref[2 * t] += db0
                    db_ref[2 * t + 1] += db1
                return carry

            _for_residues(d, residue)

        @pl.when(n == nb)
        def _():
            def residue(r, carry):
                sl = _residue_rows(r, d)
                cs = pl.ds(pl.multiple_of(r * WIN, WIN), WIN)
                dk_ref[sl, :] = ck[cs, :]
                dv_ref[sl, :] = cv[cs, :]
                return carry

            _for_residues(d, residue)

    def cur(n):
        return jnp.minimum(n, nb - 1)

    def spec(off, prev):
        if prev:
            return pl.BlockSpec((bt, bw), lambda hp, n: (jnp.maximum(cur(n) - 1, 0), off // hpw + cb + hp))
        return pl.BlockSpec((bt, bw), lambda hp, n: (cur(n), off // hpw + cb + hp))

    gspec = pl.BlockSpec((bt, bw), lambda hp, n: (cur(n), hp))
    bspec = pl.BlockSpec((2 * hpw, WIN, 2 * WIN), lambda hp, n: (hp, 0, 0))
    qspec = pl.BlockSpec((bt, bw), lambda hp, n: (cur(n), cb + hp))
    kspec = pl.BlockSpec((bt, bw), lambda hp, n: (jnp.maximum(n - 1, 0), cb + hp))
    dq, dk, dv, db = pl.pallas_call(
        body, name=name, grid=(3 // hpw, nb + 1),
        in_specs=[spec(0, False), spec(9, True), spec(9, False), spec(18, True), spec(18, False),
                  bspec, gspec, gspec, bspec, _ANY, _ANY, _ANY],
        out_specs=[qspec, kspec, kspec, bspec],
        out_shape=[jax.ShapeDtypeStruct((S, AW), F32)] * 3 + [jax.ShapeDtypeStruct((6, WIN, 2 * WIN), F32)],
        scratch_shapes=[pltpu.VMEM((bt, bw), F32), pltpu.VMEM((bt, bw), F32)],
        input_output_aliases={9: 0, 10: 1, 11: 2},
        compiler_params=pltpu.CompilerParams(dimension_semantics=("arbitrary", "arbitrary")),
    )(pa, pa, pa, pa, pa, bias, do, dlse, db_in, *dqkv)
    return (dq, dk, dv), db


def _mix_core(o0, o1, o2, l0, l1, l2):
    m = lax.stop_gradient(jnp.maximum(jnp.maximum(l0, l1), l2))
    e0, e1, e2 = jnp.exp(l0 - m), jnp.exp(l1 - m), jnp.exp(l2 - m)
    return (e0 * o0 + e1 * o1 + e2 * o2) / (e0 + e1 + e2)


def _mix_fwd(os_, ls_, name):
    S = os_[0].shape[0]
    ins = [("row", a, None, 0) for a in (*os_, *ls_)]
    return _rows(name, lambda ctx, *v: [_mix_core(*v)], ins, [(GW, GW, 0, BF16)], tm=256, nrows=S, chunk=CHUNK_NARROW)[0]


def _mix_bwd(os_, ls_, datt, name):
    S = datt.shape[0]

    def fn(ctx, *v):
        _, vjp = jax.vjp(_mix_core, *v[:6])
        return list(vjp(v[6]))

    ins = [("row", a, None, 0) for a in (*os_, *ls_, datt)]
    outs = [(GW, GW, 0, F32)] * 6
    r = _rows(name, fn, ins, outs, tm=256, nrows=S, chunk=CHUNK_NARROW)
    return r[:3], r[3:]


def _t5_bucket(dist):
    max_exact = REL_BUCKETS // 2
    is_small = dist < max_exact
    nf = jnp.maximum(dist, 1).astype(F32)
    large = max_exact + (jnp.log(nf / max_exact) / math.log(REL_MAX_DISTANCE / max_exact)
                         * (REL_BUCKETS - max_exact)).astype(jnp.int32)
    large = jnp.minimum(large, REL_BUCKETS - 1)
    return jnp.where(is_small, dist, large)


def _buckets(d):
    qi = jnp.arange(WIN)[:, None]
    kk = jnp.arange(2 * WIN)[None, :]
    rel = qi + WIN - kk
    return _t5_bucket(jnp.clip(rel, 0, None) * d)


def _pool_cnt(ctx, w):
    pos = ctx.row0 + _iota((ctx.rows, PG), 0) + 1
    return jnp.minimum(pos, w).astype(F32)


def _pool_d(ctx, halo, u):
    ds = []
    for g, w in enumerate(POOL_WINDOWS):
        ug = u[:, g * PG:(g + 1) * PG]
        s = _with_prev(ctx, halo[:, g * PG:(g + 1) * PG], ug)
        step = 1
        while step < w:
            s = s + _shift_down(s, step)
            step *= 2
        ds.append(s[HALO:] / _pool_cnt(ctx, w) - ug)
    return ds


def _pool_fwd(pb, pw, scale, name):
    S = pb.shape[0]

    def fn(ctx, halo, u, w, sc):
        ds = _pool_d(ctx, halo, u)
        return [jnp.concatenate([_dg(ds[k], w[k], 1, 0) for k in range(4)], axis=1) * sc]

    return _rows(name, fn, [("prev", pb, D, 0), ("row", pb, None, 0), ("raw", pw, None, 0), ("const", scale, None, 0)],
                 [(D, D, 0, BF16)], tm=256, nrows=S, chunk=CHUNK_POOL)[0]


def _pool_bwd(pb, pw, scale, dpo, name):
    S = pb.shape[0]

    def fn1(ctx, halo, u, w, sc, dy):
        ds = _pool_d(ctx, halo, u)
        dyp = dy * sc
        y = jnp.concatenate([_dg(ds[k], w[k], 1, 0) for k in range(4)], axis=1)
        es, dws = [], []
        for k, wd in enumerate(POOL_WINDOWS):
            cols = slice(k * PG, (k + 1) * PG)
            es.append(_dg(dyp[:, cols], w[k], 1, 1) / _pool_cnt(ctx, wd))
            dws.append(_dg(ds[k], dyp[:, cols], 0, 0))
        return [jnp.concatenate(es, axis=1), jnp.concatenate(dws, axis=0), jnp.sum(dy * y, axis=0, keepdims=True)]

    e, dpw, dsc = _rows(name + "_a", fn1,
                        [("prev", pb, D, 0), ("row", pb, None, 0), ("raw", pw, None, 0), ("const", scale, None, 0),
                         ("row", dpo, None, 0)],
                        [(D, D, 0, F32)], [(4 * PG, PG, PG), (1, D, D)], tm=256, nrows=S, chunk=CHUNK_POOL)

    def fn2(ctx, ev, halo):
        outs = []
        for g, w in enumerate(POOL_WINDOWS):
            eg = ev[:, g * PG:(g + 1) * PG]
            s = _with_next(ctx, eg, halo[:, g * PG:(g + 1) * PG])
            step = 1
            while step < w:
                s = s + _shift_up(s, step)
                step *= 2
            outs.append(s[:ctx.rows] - eg * _pool_cnt(ctx, w))
        return [jnp.concatenate(outs, axis=1)]

    du = _rows(name + "_b", fn2, [("row", e, None, 0), ("next", e, D, 0)], [(D, D, 0, BF16)], tm=256, nrows=S,
               chunk=CHUNK_POOL)[0]
    return du, dpw, dsc


def _conv_taps(ctx, halo, x, K):
    cat = _with_prev(ctx, halo, x)
    return [_shift_down(cat, K - 1 - k)[HALO:] for k in range(K)]


def _conv_pre(taps, w, b):
    acc = b
    for k, t in enumerate(taps):
        acc = acc + t * _row_pick(w, k)
    return acc


CW = 256
CWS = 512
CONV_TM = 512
CHUNK_NARROW = 32
CHUNK_POOL = 16
CHUNK_WIDE = 16


def _ext_taps(ctx, prev, x, nxt, K):
    cat = jnp.concatenate([jnp.where(ctx.first, 0.0, prev), x, jnp.where(ctx.last, 0.0, nxt)], axis=0)
    return [_shift_down(cat, K - 1 - k)[HALO:] for k in range(K)]


def _conv_t_rows(dp, w, K, tm):
    acc = jnp.zeros((tm, dp.shape[1]), F32)
    for k in range(K):
        acc = acc + _shift_up(dp, K - 1 - k)[:tm] * _row_pick(w, k)
    return acc


def _ssd_conv_fwd(pc, w, b, name):
    S = pc.shape[0]
    base = D // CWS

    def fn(ctx, halo, x, wv, bv):
        return [_silu(_conv_pre(_conv_taps(ctx, halo, x, 4), wv, bv))]

    return _rows(name, fn, [("prev", pc, CWS, base), ("row", pc, CWS, base), ("ccol", w, CWS, 0), ("ccol", b, CWS, 0)],
                 [(XBC, CWS, 0, F32)], tm=CONV_TM, nrows=S, ncol=XBC // CWS, chunk=CHUNK_POOL)[0]


def _ssd_conv_bwd(pc, w, b, dy, name):
    S = pc.shape[0]
    base = D // CWS

    def fn(ctx, prev, x, nxt, wv, bv, dyv, dyn):
        n = ctx.rows
        taps = _ext_taps(ctx, prev, x, nxt, 4)
        pre = _conv_pre(taps, wv, bv)
        sg = _sigmoid(pre)
        dye = jnp.concatenate([dyv, jnp.where(ctx.last, 0.0, dyn)], axis=0)
        dpre = dye * sg * (1.0 + pre * (1.0 - sg))
        dw = _stack_rows([jnp.sum(dpre[:n] * t[:n], axis=0, keepdims=True) for t in taps], 4)
        return [_conv_t_rows(dpre, wv, 4, n), dw, jnp.sum(dpre[:n], axis=0, keepdims=True)]

    return _rows(name, fn,
                 [("prev", pc, CWS, base), ("row", pc, CWS, base), ("next", pc, CWS, base), ("ccol", w, CWS, 0),
                  ("ccol", b, CWS, 0), ("row", dy, CWS, 0), ("next", dy, CWS, 0)],
                 [(XBC, CWS, 0, BF16)], [(4, XBC, CWS), (1, XBC, CWS)], tm=CONV_TM, nrows=S, ncol=XBC // CWS,
                 chunk=CHUNK_POOL)


NFC = D_FF // CW


def _ffn_act_fwd(h, w, b, name):
    S = h.shape[0]

    def fn(ctx, ha, a, hv, v, wa, wv, ba, bv):
        pa = _conv_pre(_conv_taps(ctx, ha, a, 3), wa, ba)
        pv = _conv_pre(_conv_taps(ctx, hv, v, 3), wv, bv)
        return [_silu(pa) * pv]

    return _rows(name, fn,
                 [("prev", h, CW, 0), ("row", h, CW, 0), ("prev", h, CW, NFC), ("row", h, CW, NFC),
                  ("ccol", w, CW, 0), ("ccol", w, CW, NFC), ("ccol", b, CW, 0), ("ccol", b, CW, NFC)],
                 [(D_FF, CW, 0, BF16)], tm=CONV_TM, nrows=S, ncol=NFC, chunk=CHUNK_NARROW)[0]


def _ffn_act_bwd(h, w, b, df, name):
    S = h.shape[0]

    def fn(ctx, pa_, a, na, pv_, v, nv, wa, wv, ba, bv, dfv, dfn):
        n = ctx.rows
        ta = _ext_taps(ctx, pa_, a, na, 3)
        tv = _ext_taps(ctx, pv_, v, nv, 3)
        pa = _conv_pre(ta, wa, ba)
        pv = _conv_pre(tv, wv, bv)
        sg = _sigmoid(pa)
        dfe = jnp.concatenate([dfv, jnp.where(ctx.last, 0.0, dfn)], axis=0)
        dpa = dfe * pv * sg * (1.0 + pa * (1.0 - sg))
        dpv = dfe * pa * sg
        res = [_conv_t_rows(dpa, wa, 3, n), _conv_t_rows(dpv, wv, 3, n)]
        for dp, taps in ((dpa, ta), (dpv, tv)):
            res.append(_stack_rows([jnp.sum(dp[:n] * t[:n], axis=0, keepdims=True) for t in taps], 3))
        for dp in (dpa, dpv):
            res.append(jnp.sum(dp[:n], axis=0, keepdims=True))
        return res

    ins = []
    for base in (0, NFC):
        ins += [("prev", h, CW, base), ("row", h, CW, base), ("next", h, CW, base)]
    ins += [("ccol", w, CW, 0), ("ccol", w, CW, NFC), ("ccol", b, CW, 0), ("ccol", b, CW, NFC),
            ("row", df, CW, 0), ("next", df, CW, 0)]
    dha, dhv, dwa, dwv, dba, dbv = _rows(
        name, fn, ins, [(D_FF, CW, 0, BF16)] * 2, [(3, D_FF, CW)] * 2 + [(1, D_FF, CW)] * 2, tm=CONV_TM, nrows=S, ncol=NFC,
        chunk=CHUNK_NARROW)
    return dha, dhv, jnp.concatenate([dwa, dwv], axis=1), jnp.concatenate([dba, dbv], axis=1)


NSLAB = D // LANES
CPS = 2


def _ssd_chunk(xs, Bs, Cs, dtraw, dtb, alog, prev):
    lsz = SSD_CHUNK
    lane = _iota((lsz, LANES), 1)
    row = _iota((lsz, LANES), 0)
    dt = jnp.where(lane < SSD_HEADS, _softplus(dtraw + dtb), 0.0)
    a = dt * (-jnp.exp(alog))
    tril = row >= lane
    a_cs = _fdot(tril.astype(F32), a)
    a_cst = a_cs.T
    a_last = jnp.sum(a, axis=0, keepdims=True)
    lo = lane < HD
    top = row < HD
    cbs = [_bdot_nt(Cs[g], Bs[g]) for g in range(2)]
    ys, news = [], []
    for s in range(NSLAB):
        g = s // (NSLAB // 2)
        cols, lms, dts, als = [], [], [], []
        for hh in range(2):
            h = 2 * s + hh
            col = _lane_pick(a_cs, h)
            seg = col - _row_pick(a_cst, h)
            lms.append(jnp.exp(jnp.where(tril, seg, NEG)))
            cols.append(col)
            dts.append(_lane_pick(dt, h))
            als.append(_lane_pick(a_last, h))
        col_x = jnp.where(lo, cols[0], cols[1])
        al_x = jnp.where(lo, als[0], als[1])
        xc = xs[s] * jnp.where(lo, dts[0], dts[1])
        yd = jnp.where(lo, _bdot_nn(cbs[g] * lms[0], xc), _bdot_nn(cbs[g] * lms[1], xc))
        yoff = _bdot_nt(Cs[g], prev[s]) * jnp.exp(col_x)
        ys.append(yd + yoff)
        st = _bdot_tn(xc * jnp.exp(al_x - col_x), Bs[g])
        news.append(prev[s] * jnp.exp(jnp.where(top, als[0], als[1])) + st)
    return ys, news


def _ssd_scan_fwd(xbc_c, pd, dtb, alog, name):
    S = xbc_c.shape[0]
    nc = S // SSD_CHUNK
    rows_ = CPS * SSD_CHUNK

    def body(x_ref, b_ref, c_ref, dt_ref, dtb_ref, al_ref, y_ref, st_ref, state):
        c = pl.program_id(0)

        @pl.when(c == 0)
        def _():
            state[...] = jnp.zeros_like(state)

        prev = [state[s * LANES:(s + 1) * LANES, :] for s in range(NSLAB)]
        for u in range(CPS):
            rw = pl.ds(u * SSD_CHUNK, SSD_CHUNK)
            xs = [x_ref[rw, s * LANES:(s + 1) * LANES] for s in range(NSLAB)]
            Bs = [b_ref[rw, g * SSD_N:(g + 1) * SSD_N] for g in range(2)]
            Cs = [c_ref[rw, g * SSD_N:(g + 1) * SSD_N] for g in range(2)]
            for s in range(NSLAB):
                st_ref[u, s * LANES:(s + 1) * LANES, :] = prev[s]
            ys, prev = _ssd_chunk(xs, Bs, Cs, dt_ref[rw, :].astype(F32), dtb_ref[...], al_ref[...], prev)
            for s in range(NSLAB):
                y_ref[rw, s * LANES:(s + 1) * LANES] = ys[s]
        for s in range(NSLAB):
            state[s * LANES:(s + 1) * LANES, :] = prev[s]

    return pl.pallas_call(
        body, name=name, grid=(nc // CPS,),
        in_specs=[pl.BlockSpec((rows_, D), lambda c: (c, 0)),
                  pl.BlockSpec((rows_, 2 * SSD_N), lambda c: (c, D // (2 * SSD_N))),
                  pl.BlockSpec((rows_, 2 * SSD_N), lambda c: (c, D // (2 * SSD_N) + 1)),
                  pl.BlockSpec((rows_, LANES), lambda c: (c, 0)),
                  pl.BlockSpec((1, LANES), lambda c: (0, 0)), pl.BlockSpec((1, LANES), lambda c: (0, 0))],
        out_specs=[pl.BlockSpec((rows_, D), lambda c: (c, 0)), pl.BlockSpec((CPS, D, SSD_N), lambda c: (c, 0, 0))],
        out_shape=[jax.ShapeDtypeStruct((S, D), F32), jax.ShapeDtypeStruct((nc, D, SSD_N), F32)],
        scratch_shapes=[pltpu.VMEM((D, SSD_N), F32)],
        compiler_params=pltpu.CompilerParams(dimension_semantics=("arbitrary",)),
    )(xbc_c, xbc_c, xbc_c, pd, dtb, alog)


def _ssd_scan_bwd(xbc_c, pd, dtb, alog, states, dy, dxs_skip, name):
    S = xbc_c.shape[0]
    nc = S // SSD_CHUNK
    rows_ = CPS * SSD_CHUNK

    def body(x_ref, b_ref, c_ref, dt_ref, dtb_ref, al_ref, st_ref, dy_ref, sk_ref,
             dx_ref, ddt_ref, ddtb_ref, dal_ref, dstate):
        c = pl.program_id(0)

        @pl.when(c == 0)
        def _():
            dstate[...] = jnp.zeros_like(dstate)
            ddtb_ref[...] = jnp.zeros_like(ddtb_ref)
            dal_ref[...] = jnp.zeros_like(dal_ref)

        dnew = [dstate[s * LANES:(s + 1) * LANES, :] for s in range(NSLAB)]
        for u in reversed(range(CPS)):
            rw = pl.ds(u * SSD_CHUNK, SSD_CHUNK)
            xs = [x_ref[rw, s * LANES:(s + 1) * LANES] for s in range(NSLAB)]
            Bs = [b_ref[rw, g * SSD_N:(g + 1) * SSD_N] for g in range(2)]
            Cs = [c_ref[rw, g * SSD_N:(g + 1) * SSD_N] for g in range(2)]
            prev = [st_ref[u, s * LANES:(s + 1) * LANES, :] for s in range(NSLAB)]
            _, vjp = jax.vjp(_ssd_chunk, xs, Bs, Cs, dt_ref[rw, :].astype(F32), dtb_ref[...], al_ref[...], prev)
            dys = [dy_ref[rw, s * LANES:(s + 1) * LANES] for s in range(NSLAB)]
            dxs, dBs, dCs, ddt, ddtb, dal, dnew = vjp((dys, dnew))
            for s in range(NSLAB):
                dx_ref[rw, s * LANES:(s + 1) * LANES] = dxs[s] + sk_ref[rw, s * LANES:(s + 1) * LANES]
            for g in range(2):
                dx_ref[rw, D + g * SSD_N:D + (g + 1) * SSD_N] = dBs[g]
                dx_ref[rw, D + 2 * SSD_N + g * SSD_N:D + 2 * SSD_N + (g + 1) * SSD_N] = dCs[g]
            ddt_ref[rw, :] = ddt
            ddtb_ref[...] += ddtb
            dal_ref[...] += dal
        for s in range(NSLAB):
            dstate[s * LANES:(s + 1) * LANES, :] = dnew[s]

    def rv(c):
        return nc // CPS - 1 - c

    return pl.pallas_call(
        body, name=name, grid=(nc // CPS,),
        in_specs=[pl.BlockSpec((rows_, D), lambda c: (rv(c), 0)),
                  pl.BlockSpec((rows_, 2 * SSD_N), lambda c: (rv(c), D // (2 * SSD_N))),
                  pl.BlockSpec((rows_, 2 * SSD_N), lambda c: (rv(c), D // (2 * SSD_N) + 1)),
                  pl.BlockSpec((rows_, LANES), lambda c: (rv(c), 0)),
                  pl.BlockSpec((1, LANES), lambda c: (0, 0)), pl.BlockSpec((1, LANES), lambda c: (0, 0)),
                  pl.BlockSpec((CPS, D, SSD_N), lambda c: (rv(c), 0, 0)),
                  pl.BlockSpec((rows_, D), lambda c: (rv(c), 0)),
                  pl.BlockSpec((rows_, D), lambda c: (rv(c), 0))],
        out_specs=[pl.BlockSpec((rows_, XBC), lambda c: (rv(c), 0)),
                   pl.BlockSpec((rows_, LANES), lambda c: (rv(c), 0)),
                   pl.BlockSpec((1, LANES), lambda c: (0, 0)), pl.BlockSpec((1, LANES), lambda c: (0, 0))],
        out_shape=[jax.ShapeDtypeStruct((S, XBC), F32), jax.ShapeDtypeStruct((S, LANES), F32),
                   jax.ShapeDtypeStruct((1, LANES), F32), jax.ShapeDtypeStruct((1, LANES), F32)],
        scratch_shapes=[pltpu.VMEM((D, SSD_N), F32)],
        compiler_params=pltpu.CompilerParams(dimension_semantics=("arbitrary",)),
    )(xbc_c, xbc_c, xbc_c, pd, dtb, alog, states, dy, dxs_skip)


def _ssd_post_core(y, xs, z, d128, nw):
    tm = y.shape[0]
    ex = (_iota((LANES, D), 1) // HD == _iota((LANES, D), 0)).astype(F32)
    d_x = jnp.sum(_fdot(jnp.broadcast_to(d128, (8, LANES)), ex), axis=0, keepdims=True) * 0.125
    y2 = (y + d_x * xs) * _silu(z)
    lo = _iota((tm, D), 1) < D // 2
    sq = y2 * y2
    ms0 = jnp.sum(jnp.where(lo, sq, 0.0), axis=-1, keepdims=True) / (D // 2)
    ms1 = jnp.sum(jnp.where(lo, 0.0, sq), axis=-1, keepdims=True) / (D // 2)
    r = jnp.where(lo, lax.rsqrt(ms0 + EPS), lax.rsqrt(ms1 + EPS))
    return y2 * r * nw


def _ssd_post_ins(y, xbc_c, pc, d128, nw):
    return [("row", y, None, 0), ("row", xbc_c, D, 0), ("row", pc, D, 0), ("const", d128, None, 0), ("const", nw, None, 0)]


def _ssd_post_fwd(y, xbc_c, pc, d128, nw, name):
    S = y.shape[0]
    return _rows(name, lambda ctx, *v: [_ssd_post_core(*v)], _ssd_post_ins(y, xbc_c, pc, d128, nw),
                 [(D, D, 0, BF16)], tm=256, nrows=S, chunk=CHUNK_WIDE)[0]


def _ssd_post_bwd(y, xbc_c, pc, d128, nw, dout, name):
    S = y.shape[0]

    def fn(ctx, *v):
        _, vjp = jax.vjp(_ssd_post_core, *v[:5])
        return list(vjp(v[5]))

    return _rows(name, fn, _ssd_post_ins(y, xbc_c, pc, d128, nw) + [("row", dout, None, 0)],
                 [(D, D, 0, F32), (D, D, 0, F32), (D, D, 0, BF16)], [(1, LANES, LANES), (1, D, D)], tm=256, nrows=S,
                 chunk=CHUNK_WIDE)


def _gates_core(g0, g1, g2, b0, b1, b2, ya, yb, yc):
    return _sigmoid(g0 + b0) * ya + _sigmoid(g1 + b1) * yb + _sigmoid(g2 + b2) * yc


def _gate_parts(pdv, bv):
    gp = pltpu.roll(pdv, SEC_D - 16, 1)
    return [gp[:, k * D:(k + 1) * D] for k in range(3)] + [bv[:, k * D:(k + 1) * D] for k in range(3)]


def _gates_fwd(pd, bg, ya, yb, yc, name):
    S = pd.shape[0]

    def fn(ctx, pdv, bv, a, b, c):
        return [_gates_core(*_gate_parts(pdv, bv), a, b, c)]

    return _rows(name, fn, [("row", pd, None, 0), ("const", bg, None, 0), ("row", ya, None, 0), ("row", yb, None, 0),
                            ("row", yc, None, 0)], [(D, D, 0, BF16)], tm=256, nrows=S, chunk=CHUNK_WIDE)[0]


def _gates_post(dm, pdv, a, b, c, bv):
    _, vjp = jax.vjp(_gates_core, *_gate_parts(pdv, bv), a, b, c)
    g = vjp(dm)
    return [g[6], g[7], g[8], jnp.concatenate(g[0:3], axis=1), jnp.concatenate(g[3:6], axis=1)]


def _adamw(w, g, m, v, name):
    rows, C = w.shape
    tm = _pick(rows, [t for t in (512, 256, 128, 64, 32, 16, 8) if t * C <= ADAM_TILE])

    def fn(ctx, wv, gv, mv, vv):
        m2 = ADAM_B1 * mv + (1.0 - ADAM_B1) * gv
        v2 = ADAM_B2 * vv + (1.0 - ADAM_B2) * jnp.square(gv)
        m_hat = m2 / (1.0 - ADAM_B1 ** ADAM_STEP)
        v_hat = v2 / (1.0 - ADAM_B2 ** ADAM_STEP)
        delta = -ADAM_LR * (m_hat / (jnp.sqrt(v_hat) + ADAM_EPS) + ADAM_WD * wv)
        return [delta, m2, v2]

    return _rows(name, fn, [("row", a, None, 0) for a in (w, g, m, v)], [(C, C, 0, F32)] * 3, tm=tm, nrows=rows)


def _position():
    return lax.axis_index("x"), lax.axis_index("y"), lax.axis_index("c")


def _other_chips(x, y):
    return [(1 - x, y), (x, 1 - y), (1 - x, 1 - y)]


_HBM = pl.BlockSpec(memory_space=pltpu.HBM)


def _gather_parts(half, lo, n):
    def copies(p_ref, out_ref, send_sems, recv_sems):
        x, y, c = _position()
        sibling = (x, y, 1 - c)
        chips = _other_chips(x, y)

        def slab(chip, h):
            return out_ref.at[2 * chip[0] + chip[1], pl.ds(h * half + lo, n), :]

        def copy(k, src, dst, to):
            return pltpu.make_async_remote_copy(src_ref=src, dst_ref=dst, send_sem=send_sems.at[k],
                                                recv_sem=recv_sems.at[k], device_id=to, device_id_type=MESH)

        first = [copy(j, p_ref.at[pl.ds(c * half + lo, n), :], slab((x, y), c), (*chip, c)) for j, chip in enumerate(chips)]
        passed = [copy(3 + j, slab(chip, c), slab(chip, c), sibling) for j, chip in enumerate(chips)]
        from_chips = [copy(j, slab(chip, c), slab(chip, c), (x, y, c)) for j, chip in enumerate(chips)]
        from_sibling = [copy(3 + j, slab(chip, 1 - c), slab(chip, 1 - c), (x, y, c)) for j, chip in enumerate(chips)]
        return first, passed, from_chips, from_sibling

    def start(ins, outs, scr):
        for cp in copies(ins[0], outs[0], *scr)[0]:
            cp.start()

    def finish(ins, outs, scr):
        first, passed, from_chips, from_sibling = copies(ins[0], outs[0], *scr)
        for j in range(3):
            from_chips[j].wait_recv()
            passed[j].start()
        for cp in from_sibling:
            cp.wait_recv()
        for cp in first + passed:
            cp.wait_send()

    return start, finish


def _rs_chip_parts(lo, n):
    def copies(h_ref, out_ref, send_sems, recv_sems):
        x, y, c = _position()
        return [pltpu.make_async_remote_copy(src_ref=h_ref.at[2 * chip[0] + chip[1], pl.ds(lo, n), :],
                                             dst_ref=out_ref.at[j, pl.ds(lo, n), :],
                                             send_sem=send_sems.at[j], recv_sem=recv_sems.at[j],
                                             device_id=(*chip, c), device_id_type=MESH)
                for j, chip in enumerate(_other_chips(x, y))]

    def start(ins, outs, scr):
        for cp in copies(ins[0], outs[0], *scr):
            cp.start()

    def finish(ins, outs, scr):
        for cp in copies(ins[0], outs[0], *scr):
            cp.wait()

    return start, finish


class _Stream:
    def __init__(self, src, buf, parts, nsem, units, name):
        self.src, self.buf, self.parts, self.nsem, self.name = src, buf, parts, nsem, name
        self.next, self.units = 0, units

    def _scratch(self):
        return [pltpu.SemaphoreType.DMA((self.nsem,)), pltpu.SemaphoreType.DMA((self.nsem,))]

    def _take(self, units):
        units = min(units, self.units - self.next)
        lo = self.next * 16
        self.next += units
        return lo, units * 16

    def _set(self, outs):
        self.buf = outs[0]

    def hook(self, units):
        lo, n = self._take(units)
        if n == 0:
            return None
        start, finish = self.parts(lo, n)
        return _Hook([self.src, self.buf], [jax.ShapeDtypeStruct(self.buf.shape, self.buf.dtype)], {1: 0},
                     self._scratch(), start, finish, self._set)

    def drain(self):
        lo, n = self._take(self.units)
        if n:
            start, finish = self.parts(lo, n)

            def body(s_ref, b_ref, o_ref, send_sems, recv_sems):
                args = ((s_ref, b_ref), (o_ref,), (send_sems, recv_sems))
                start(*args)
                finish(*args)

            self.buf = pl.pallas_call(
                body, name=self.name, in_specs=[_ANY, _ANY], out_specs=_ANY,
                out_shape=jax.ShapeDtypeStruct(self.buf.shape, self.buf.dtype),
                scratch_shapes=self._scratch(), input_output_aliases={1: 0},
            )(self.src, self.buf)
        return self.buf


def _rs_pair_exchange(g, name):
    _, R, C = g.shape
    Rh = R // 2

    def body(g_ref, out_ref, send_sem, recv_sem):
        x, y, c = _position()
        src = g_ref.at[pl.ds(0, 4), pl.ds((1 - c) * Rh, Rh), :]
        cp = pltpu.make_async_remote_copy(src_ref=src, dst_ref=out_ref, send_sem=send_sem,
                                          recv_sem=recv_sem, device_id=(x, y, 1 - c), device_id_type=MESH)
        cp.start()
        cp.wait()

    return pl.pallas_call(
        body, name=name, in_specs=[_HBM], out_specs=_HBM,
        out_shape=jax.ShapeDtypeStruct((4, Rh, C), g.dtype),
        scratch_shapes=[pltpu.SemaphoreType.DMA, pltpu.SemaphoreType.DMA],
    )(g)


def _rs_swap(r, name):
    Rh, C = r.shape

    def body(r_ref, out_ref, send_sem, recv_sem):
        x, y, c = _position()
        cp = pltpu.make_async_remote_copy(src_ref=r_ref, dst_ref=out_ref, send_sem=send_sem,
                                          recv_sem=recv_sem, device_id=(x, y, 1 - c), device_id_type=MESH)
        cp.start()
        cp.wait()

    return pl.pallas_call(
        body, name=name, in_specs=[_HBM], out_specs=_HBM,
        out_shape=jax.ShapeDtypeStruct((Rh, C), r.dtype),
        scratch_shapes=[pltpu.SemaphoreType.DMA, pltpu.SemaphoreType.DMA],
    )(r)


def _rs_add_pair(g, recv, cidx, name):
    _, R, C = g.shape
    Rh = R // 2
    tm = _pick(Rh, (400, 280, 200, 160, 80, 40, 16, 8))
    nt = Rh // tm

    def body(c_ref, g_ref, r_ref, o_ref):
        o_ref[...] = (g_ref[...].astype(F32) + r_ref[...].astype(F32)).astype(o_ref.dtype)

    return pl.pallas_call(
        body, name=name,
        grid_spec=pltpu.PrefetchScalarGridSpec(
            num_scalar_prefetch=1, grid=(4, nt),
            in_specs=[pl.BlockSpec((1, tm, C), lambda k, i, cr: (k, cr[0] * nt + i, 0)),
                      pl.BlockSpec((1, tm, C), lambda k, i, cr: (k, i, 0))],
            out_specs=pl.BlockSpec((1, tm, C), lambda k, i, cr: (k, i, 0))),
        out_shape=jax.ShapeDtypeStruct((4, Rh, C), BF16),
    )(cidx, g, recv)


def _rs_add_chips(h, recv, chip_idx, name):
    _, Rh, C = h.shape
    tm = _pick(Rh, (400, 280, 200, 160, 80, 40, 16, 8))

    def body(c_ref, h_ref, r_ref, o_ref):
        acc = h_ref[0].astype(F32)
        for j in range(3):
            acc = acc + r_ref[j].astype(F32)
        o_ref[...] = acc

    return pl.pallas_call(
        body, name=name,
        grid_spec=pltpu.PrefetchScalarGridSpec(
            num_scalar_prefetch=1, grid=(Rh // tm,),
            in_specs=[pl.BlockSpec((1, tm, C), lambda i, cr: (cr[0], i, 0)), pl.BlockSpec((3, tm, C), lambda i, cr: (0, i, 0))],
            out_specs=pl.BlockSpec((tm, C), lambda i, cr: (i, 0))),
        out_shape=jax.ShapeDtypeStruct((Rh, C), F32),
    )(chip_idx, h, recv)


def _all_reduce_small(vec, name):
    n, C = vec.shape

    def body(v_ref, out_ref, buf, send_sems, recv_sems):
        x, y, c = _position()

        def flip(k):
            return ((1 - x) if k & 4 else x, (1 - y) if k & 2 else y, (1 - c) if k & 1 else c)

        def idx(p):
            return 4 * p[0] + 2 * p[1] + p[2]

        me = idx((x, y, c))
        buf[me] = v_ref[...]
        cps = [pltpu.make_async_remote_copy(src_ref=v_ref, dst_ref=buf.at[me], send_sem=send_sems.at[k - 1],
                                            recv_sem=recv_sems.at[k - 1], device_id=flip(k), device_id_type=MESH)
               for k in range(1, 8)]
        for cp in cps:
            cp.start()
        for k in range(1, 8):
            pltpu.make_async_remote_copy(src_ref=v_ref, dst_ref=buf.at[idx(flip(k))], send_sem=send_sems.at[k - 1],
                                         recv_sem=recv_sems.at[k - 1], device_id=flip(k), device_id_type=MESH).wait_recv()
        for cp in cps:
            cp.wait_send()
        acc = buf[0]
        for s in range(1, 8):
            acc = acc + buf[s]
        out_ref[...] = acc

    return pl.pallas_call(
        body, name=name,
        in_specs=[pl.BlockSpec(memory_space=pltpu.VMEM)], out_specs=pl.BlockSpec(memory_space=pltpu.VMEM),
        out_shape=jax.ShapeDtypeStruct((n, C), F32),
        scratch_shapes=[pltpu.VMEM((8, n, C), F32), pltpu.SemaphoreType.DMA((7,)), pltpu.SemaphoreType.DMA((7,))],
    )(vec)


BIG = (("w_in", (D, IN_WIDTH // 4), "cols"), ("w_a", (GW, D // 4), "cols"), ("pool_w", (4, PG // 4, PG), "pool"),
       ("w_b", (D // 4, D), "rows"), ("w_c", (D // 4, D), "rows"), ("w_o", (D // 4, D), "rows"),
       ("ffn_w_up", (D, 2 * D_FF // 4), "cols"), ("ffn_w_down", (D_FF // 4, D), "rows"))
def _pack_rows(s):
    k = math.prod(s) // D
    return -(-k // 16) * 16, k


PACK_ROWS = sum(_pack_rows(s)[0] for _, s, _ in BIG)
PACK_PAD = -(-PACK_ROWS // 32) * 32


def _pad_rows(v, rows):
    pad = [(0, 0)] * v.ndim
    pad[-2] = (0, rows - v.shape[-2])
    return jnp.pad(v, pad) if rows > v.shape[-2] else v


def _pack_blocks(blocks, dtype):
    lead = blocks["w_in"].shape[:-2]
    flat = []
    for n, s, how in BIG:
        v = blocks[n].astype(dtype)
        if how == "cols":
            v = jnp.swapaxes(v, -1, -2)
        flat.append(_pad_rows(v.reshape(*lead, -1, D), _pack_rows(s)[0]))
    flat.append(jnp.zeros((*lead, PACK_PAD - PACK_ROWS, D), dtype))
    return jnp.concatenate(flat, axis=-2)


def _unpack_blocks(pack):
    out, r = {}, 0
    for n, s, how in BIG:
        rows, k = _pack_rows(s)
        v = pack[r:r + k, :]
        out[n] = v.reshape(s[1], s[0]).T if how == "cols" else v.reshape(s)
        r += rows
    return out


def _operands(allp):
    out, r = {}, 0
    for n, s, how in BIG:
        rows, k = _pack_rows(s)
        v = allp[:, r:r + k, :]
        if how == "cols":
            out[n] = v.reshape(4 * s[1], s[0])
        elif how == "rows":
            out[n] = v.reshape(4 * s[0], s[1])
        else:
            out[n] = v.reshape(4, *s).transpose(1, 0, 2, 3).reshape(4, PG, PG)
        r += rows
    return out


def _pack_operands(g, dtype):
    flat = []
    for n, s, how in BIG:
        v = g[n].astype(dtype)
        if how == "pool":
            v = v.reshape(4, 4, s[1], s[2]).transpose(1, 0, 2, 3)
        flat.append(_pad_rows(v.reshape(4, -1, D), _pack_rows(s)[0]))
    flat.append(jnp.zeros((4, PACK_PAD - PACK_ROWS, D), dtype))
    return jnp.concatenate(flat, axis=1)


def _layer_fwd(x, w, sm, bias, hk):
    pa, u = _mmf(None, w["in_a"], tb=True, pre=(_rms_core, [x], [sm["ln1_g"]]), name="in_a", tm=1024, hook=hk("in_a"))
    pb = _mm(u, w["in_b"], tb=True, out_dtype=BF16, name="in_b", hook=hk("in_b"))
    pc = _mm(u, w["in_c"], tb=True, out_dtype=BF16, name="in_c", hook=hk("in_c"))
    pd = _mm(u, w["in_d"], tb=True, out_dtype=BF16, name="in_d", hook=hk("in_d"))
    os_, ls_ = [], []
    for gi in range(3):
        o, l = _attn_fwd(pa, bias[gi], gi, "attn_fwd%d" % gi)
        os_.append(o)
        ls_.append(l)
    att = _mix_fwd(os_, ls_, "mix_fwd")
    ya = _mm(att, w["w_a"], tb=True, out_dtype=BF16, name="mm_wa")
    pool_o = _pool_fwd(pb, w["pool_w"], sm["pool_scale"], "pool_fwd")
    yb = _mm(pool_o, w["w_b"], out_dtype=BF16, name="mm_wb")
    xbc_c = _ssd_conv_fwd(pc, sm["ssd_conv_w"], sm["ssd_conv_b"], "ssd_conv_fwd")
    y_scan, states = _ssd_scan_fwd(xbc_c, pd, sm["ssd_dt_bias"], sm["ssd_a_log"], "ssd_scan_fwd")
    ssd_o = _ssd_post_fwd(y_scan, xbc_c, pc, sm["ssd_d"], sm["ssd_norm_w"], "ssd_post_fwd")
    yc = _mm(ssd_o, w["w_c"], out_dtype=BF16, name="mm_wc")
    merged = _gates_fwd(pd, sm["b_gate"], ya, yb, yc, "gates_fwd")
    x1 = _mm(merged, w["w_o"], add=x, name="mm_wo", hook=hk("mm_wo"))
    h, u2 = _mmf(None, w["ffn_w_up"], tb=True, pre=(_rms_core, [x1], [sm["ln2_g"]]), out_dtype=BF16, name="mm_up",
                 tm=1024, hook=hk("mm_up"))
    f = _ffn_act_fwd(h, sm["ffn_conv_w"], sm["ffn_conv_b"], "ffn_act_fwd")
    x2 = _mm(f, w["ffn_w_down"], add=x1, name="mm_down", hook=hk("mm_down"))
    saved = dict(x=x, u=u, pa=pa, pb=pb, pc=pc, pd=pd, os=os_, ls=ls_, att=att, ya=ya, yb=yb, yc=yc, pool_o=pool_o,
                 xbc_c=xbc_c, y_scan=y_scan, states=states, ssd_o=ssd_o, merged=merged, x1=x1, u2=u2, h=h, f=f)
    return x2, saved


def _layer_bwd(dx2, w, sm, bias, dbs, sv, hk):
    gw, gs = {}, {}
    S = dx2.shape[0]

    def gmm(a, b, name):
        return _mm(a, b, ta=True, out_dtype=BF16, name=name, hook=hk(name))

    df = _mm(dx2, w["ffn_w_down"], tb=True, out_dtype=BF16, name="d_f", hook=hk("d_f"))
    gw["ffn_w_down"] = gmm(sv["f"], dx2, "g_down")
    dha, dhv, gs["ffn_conv_w"], gs["ffn_conv_b"] = _ffn_act_bwd(sv["h"], sm["ffn_conv_w"], sm["ffn_conv_b"], df, "ffn_act_bwd")
    du2 = _mm(dha, w["up_a"], name="d_u2_a", hook=hk("d_u2_a"))
    dx1, gs["ln2_g"] = _mmf(dhv, w["up_v"], add=du2, name="d_u2_v", tm=256, hook=hk("d_u2_v"),
                            post=(_rms_post, [sv["x1"], dx2], [sm["ln2_g"]], [(D, F32)], [(1, D)]))
    gw["ffn_w_up"] = jnp.concatenate([gmm(dha, sv["u2"], "g_up_a"), gmm(dhv, sv["u2"], "g_up_v")], axis=0)
    dya, dyb, dyc, dgate, gs["b_gate"] = _mmf(
        dx1, w["w_o"], tb=True, name="d_merged", tm=256, hook=hk("d_merged"),
        post=(_gates_post, [sv["pd"], sv["ya"], sv["yb"], sv["yc"]], [sm["b_gate"]],
              [(D, BF16)] * 3 + [(3 * D, BF16)], [(1, 3 * D)]))
    gw["w_o"] = gmm(sv["merged"], dx1, "g_wo")
    dssd_o = _mm(dyc, w["w_c"], tb=True, name="d_ssd_o")
    gw["w_c"] = gmm(sv["ssd_o"], dyc, "g_wc")
    dy_scan, dxs_skip, dz, gs["ssd_d"], gs["ssd_norm_w"] = _ssd_post_bwd(
        sv["y_scan"], sv["xbc_c"], sv["pc"], sm["ssd_d"], sm["ssd_norm_w"], dssd_o, "ssd_post_bwd")
    dxbc_c, ddt, gs["ssd_dt_bias"], gs["ssd_a_log"] = _ssd_scan_bwd(
        sv["xbc_c"], sv["pd"], sm["ssd_dt_bias"], sm["ssd_a_log"], sv["states"], dy_scan, dxs_skip, "ssd_scan_bwd")
    dxbc, gs["ssd_conv_w"], gs["ssd_conv_b"] = _ssd_conv_bwd(sv["pc"], sm["ssd_conv_w"], sm["ssd_conv_b"], dxbc_c, "ssd_conv_bwd")
    dpool_o = _mm(dyb, w["w_b"], tb=True, name="d_pool_o")
    gw["w_b"] = gmm(sv["pool_o"], dyb, "g_wb")
    dpb, dpw, gs["pool_scale"] = _pool_bwd(sv["pb"], w["pool_w"], sm["pool_scale"], dpool_o, "pool_bwd")
    gw["pool_w"] = dpw.reshape(4, PG, PG)
    datt = _mm(dya, w["w_a"], name="d_att")
    gw["w_a"] = gmm(dya, sv["att"], "g_wa")
    dos, dls = _mix_bwd(sv["os"], sv["ls"], datt, "mix_bwd")
    dqkv = tuple(lax.empty((S, AW), F32) for _ in range(3))
    dbs = list(dbs)
    for gi in range(3):
        dqkv, dbs[gi] = _attn_bwd(sv["pa"], bias[gi], dos[gi], dls[gi], dbs[gi], dqkv, gi, "attn_bwd%d" % gi)
    u = sv["u"]
    pieces = [(dqkv[0], "wq"), (dqkv[1], "wk"), (dqkv[2], "wv"), (dpb, "in_b"), (dz, "wz"), (dxbc, "wxbc"),
              (ddt, "wdt"), (dgate, "wgate")]
    du = None
    g_in = []
    for dp, key in pieces:
        if key == pieces[-1][1]:
            dx, gs["ln1_g"] = _mmf(dp, w[key], add=du, name="d_u_" + key, tm=256, hook=hk("d_u_" + key),
                                   post=(_rms_post, [sv["x"], dx1], [sm["ln1_g"]], [(D, F32)], [(1, D)]))
        else:
            du = _mm(dp, w[key], add=du, name="d_u_" + key, hook=hk("d_u_" + key))
        g = gmm(dp, u, "g_in_" + key)
        g_in.append(g[:SSD_HEADS] if key == "wdt" else g)
    gw["w_in"] = jnp.concatenate(g_in, axis=0)
    return dx, gw, gs, dbs


SMALL_LAYER = ("ln1_g", "b_gate", "pool_scale", "ssd_conv_w", "ssd_conv_b", "ssd_dt_bias", "ssd_a_log", "ssd_d",
               "ssd_norm_w", "ln2_g", "ffn_conv_w", "ffn_conv_b")


def _pad_lanes(v):
    return jnp.pad(v, (0, LANES - v.shape[0])).reshape(1, LANES)


def _layer_weights(ops):
    wt = ops["w_in"]
    o1, o2, o3 = SEC_A, SEC_A + SEC_B, SEC_A + SEC_B + SEC_C
    w = dict(ops)
    w["in_a"] = jnp.pad(wt[:o1], ((0, SEC_A_PAD - o1), (0, 0)))
    w["in_b"] = wt[o1:o2]
    w["in_c"] = wt[o2:o3]
    w["in_d"] = jnp.pad(wt[o3:], ((0, SEC_D - (IN_WIDTH - o3)), (0, 0)))
    w["wq"], w["wk"], w["wv"] = wt[:AW], wt[AW:2 * AW], wt[2 * AW:o1]
    w["wz"], w["wxbc"] = wt[o2:o2 + D], wt[o2 + D:o3]
    w["wdt"] = jnp.pad(wt[o3:o3 + SSD_HEADS], ((0, LANES - SSD_HEADS), (0, 0)))
    w["wgate"] = wt[o3 + SSD_HEADS:]
    w["up_a"], w["up_v"] = ops["ffn_w_up"][:D_FF], ops["ffn_w_up"][D_FF:]
    return w


def _layer_small(p, i):
    sm = {n: p[n][i] for n in SMALL_LAYER}
    out = {}
    for n, v in sm.items():
        if n in ("ssd_dt_bias", "ssd_a_log", "ssd_d"):
            out[n] = _pad_lanes(v)
        elif v.ndim == 1:
            out[n] = v.reshape(1, -1)
        else:
            out[n] = v
    return out


def _local_step(x, target, rel_bias, final_g, layer_full, small, fwd_hooks=None, bwd_hooks=None, after_bwd=None):
    nl = small["ln1_g"].shape[0]
    buckets = [_buckets(d).astype(jnp.int32) for d in DILATIONS]
    bias = [_bias_table(rel_bias, buckets[gi], gi, "bias_table%d" % gi) for gi in range(3)]
    no_hooks = lambda i: (lambda name: None)
    fwd_hooks = fwd_hooks or no_hooks
    bwd_hooks = bwd_hooks or no_hooks
    saved, ws, sms = [], [], []
    h = x
    for i in range(nl):
        w = _layer_weights(layer_full(i))
        sm = _layer_small(small, i)
        h, sv = _layer_fwd(h, w, sm, bias, fwd_hooks(i))
        saved.append(sv)
        ws.append(w)
        sms.append(sm)
    dh, dfinal, loss = _final_loss(h, target, final_g.reshape(1, D))
    gws, gss = [None] * nl, [None] * nl
    dbs = [jnp.zeros((6, WIN, 2 * WIN), F32)] * 3
    for i in reversed(range(nl)):
        dh, gws[i], gss[i], dbs = _layer_bwd(dh, ws[i], sms[i], bias, dbs, saved[i], bwd_hooks(i))
        if after_bwd is not None:
            after_bwd(i, gws[i])
    drel = []
    for gi in range(3):
        onehot = jnp.pad(jax.nn.one_hot(buckets[gi].reshape(-1), REL_BUCKETS, dtype=BF16), ((0, 0), (0, LANES - REL_BUCKETS)))
        drel.append(_mm(dbs[gi].reshape(6, WIN * 2 * WIN), onehot, name="g_relb"))
    return loss, dh, gws, gss, dfinal, jnp.concatenate(drel, axis=0)


WEIGHTS = ("rel_bias", "ln1_g", "w_in", "b_gate", "w_a", "pool_w", "pool_scale", "w_b", "ssd_conv_w", "ssd_conv_b",
           "ssd_dt_bias", "ssd_a_log", "ssd_d", "ssd_norm_w", "w_c", "w_o", "ln2_g", "ffn_w_up", "ffn_conv_w",
           "ffn_conv_b", "ffn_w_down", "final_g")
BIG_NAMES = tuple(n for n, _, _ in BIG)
SHARDED_SMALL = {"ssd_conv_w": XBC // 4, "ffn_conv_w": 2 * D_FF // 4}


def _to_rows(flat):
    n = flat.shape[0]
    rows = -(-n // LANES)
    rows = -(-rows // 8) * 8
    return jnp.pad(flat, (0, rows * LANES - n)).reshape(rows, LANES)


def _flatten(tree, names):
    return jnp.concatenate([tree[n].reshape(-1) for n in names])


def _unflatten(flat, shapes, names):
    out, o = {}, 0
    for n in names:
        k = math.prod(shapes[n])
        out[n] = flat[o:o + k].reshape(shapes[n])
        o += k
    return out


def kernel(x, rel_bias, ln1_g, w_in, b_gate, w_a, pool_w, pool_scale, w_b, ssd_conv_w, ssd_conv_b, ssd_dt_bias, ssd_a_log, ssd_d, ssd_norm_w, w_c, w_o, ln2_g, ffn_w_up, ffn_conv_w, ffn_conv_b, ffn_w_down, final_g, loss_target, m_rel_bias, m_ln1_g, m_w_in, m_b_gate, m_w_a, m_pool_w, m_pool_scale, m_w_b, m_ssd_conv_w, m_ssd_conv_b, m_ssd_dt_bias, m_ssd_a_log, m_ssd_d, m_ssd_norm_w, m_w_c, m_w_o, m_ln2_g, m_ffn_w_up, m_ffn_conv_w, m_ffn_conv_b, m_ffn_w_down, m_final_g, v_rel_bias, v_ln1_g, v_w_in, v_b_gate, v_w_a, v_pool_w, v_pool_scale, v_w_b, v_ssd_conv_w, v_ssd_conv_b, v_ssd_dt_bias, v_ssd_a_log, v_ssd_d, v_ssd_norm_w, v_w_c, v_w_o, v_ln2_g, v_ffn_w_up, v_ffn_conv_w, v_ffn_conv_b, v_ffn_w_down, v_final_g):
    W = dict(rel_bias=rel_bias, ln1_g=ln1_g, w_in=w_in, b_gate=b_gate, w_a=w_a, pool_w=pool_w, pool_scale=pool_scale,
             w_b=w_b, ssd_conv_w=ssd_conv_w, ssd_conv_b=ssd_conv_b, ssd_dt_bias=ssd_dt_bias, ssd_a_log=ssd_a_log,
             ssd_d=ssd_d, ssd_norm_w=ssd_norm_w, w_c=w_c, w_o=w_o, ln2_g=ln2_g, ffn_w_up=ffn_w_up,
             ffn_conv_w=ffn_conv_w, ffn_conv_b=ffn_conv_b, ffn_w_down=ffn_w_down, final_g=final_g)
    M = dict(rel_bias=m_rel_bias, ln1_g=m_ln1_g, w_in=m_w_in, b_gate=m_b_gate, w_a=m_w_a, pool_w=m_pool_w,
             pool_scale=m_pool_scale, w_b=m_w_b, ssd_conv_w=m_ssd_conv_w, ssd_conv_b=m_ssd_conv_b,
             ssd_dt_bias=m_ssd_dt_bias, ssd_a_log=m_ssd_a_log, ssd_d=m_ssd_d, ssd_norm_w=m_ssd_norm_w, w_c=m_w_c,
             w_o=m_w_o, ln2_g=m_ln2_g, ffn_w_up=m_ffn_w_up, ffn_conv_w=m_ffn_conv_w, ffn_conv_b=m_ffn_conv_b,
             ffn_w_down=m_ffn_w_down, final_g=m_final_g)
    V = dict(rel_bias=v_rel_bias, ln1_g=v_ln1_g, w_in=v_w_in, b_gate=v_b_gate, w_a=v_w_a, pool_w=v_pool_w,
             pool_scale=v_pool_scale, w_b=v_w_b, ssd_conv_w=v_ssd_conv_w, ssd_conv_b=v_ssd_conv_b,
             ssd_dt_bias=v_ssd_dt_bias, ssd_a_log=v_ssd_a_log, ssd_d=v_ssd_d, ssd_norm_w=v_ssd_norm_w, w_c=v_w_c,
             w_o=v_w_o, ln2_g=v_ln2_g, ffn_w_up=v_ffn_w_up, ffn_conv_w=v_ffn_conv_w, ffn_conv_b=v_ffn_conv_b,
             ffn_w_down=v_ffn_w_down, final_g=v_final_g)
    nl = ln1_g.shape[0]
    px, py, pc_ = _position()
    chip = 2 * px + py
    cidx = jnp.reshape(pc_, (1,)).astype(jnp.int32)
    chip_idx = jnp.reshape(chip, (1,)).astype(jnp.int32)

    placed = {}
    for n, cs in SHARDED_SMALL.items():
        full = jnp.zeros(W[n].shape[:-1] + (4 * cs,), F32)
        full = lax.dynamic_update_slice(full, W[n], (0, 0, chip * cs))
        placed[n] = jnp.where(pc_ == 0, full, 0.0)
    names_sh = tuple(SHARDED_SMALL)
    shapes_sh = {n: placed[n].shape for n in names_sh}
    got = _all_reduce_small(_to_rows(_flatten(placed, names_sh)), "gather_small")
    small = {n: W[n] for n in SMALL_LAYER}
    small.update(_unflatten(got.reshape(-1), shapes_sh, names_sh))

    packs = _pack_blocks({n: W[n] for n in BIG_NAMES}, BF16)

    half = PACK_PAD // 2
    units = half // 16

    def share(weights, total):
        tot = sum(weights.values())
        return {n: math.ceil(total * v / tot) for n, v in weights.items()}

    gathers = {}

    def gather(i):
        if i not in gathers:
            buf = lax.dynamic_update_slice(lax.empty((4, PACK_PAD, D), BF16), packs[i][None], (chip, 0, 0))
            gathers[i] = _Stream(packs[i], buf, functools.partial(_gather_parts, half), 6, units, "gather_w")
        return gathers[i]

    def layer_full(i):
        return _operands(gather(i).drain())

    fwd_share = share(dict(in_a=89, in_b=26, in_c=57, in_d=66, mm_wo=28, mm_up=120, mm_down=46), units)

    def fwd_hooks(i):
        if i + 1 >= nl:
            return lambda name: None
        return lambda name: gather(i + 1).hook(fwd_share[name]) if name in fwd_share else None

    exchanges = {}
    bwd_share = share(dict(d_f=91, g_down=67, d_u2_a=42, d_u2_v=45, g_up_a=52, g_up_v=52, d_merged=29, g_wo=19,
                           d_u_wgate=48, g_in_wgate=41), units)

    def after_bwd(i, gw):
        g = _pack_operands(gw, BF16)
        recv = _rs_pair_exchange(g, "rs_pair")
        hsum = _rs_add_pair(g, recv, cidx, "rs_add_pair")
        exchanges[i] = (hsum, _Stream(hsum, lax.empty((3, half, D), BF16), _rs_chip_parts, 3, units, "rs_chips"))

    def bwd_hooks(i):
        if i + 1 >= nl:
            return lambda name: None
        return lambda name: exchanges[i + 1][1].hook(bwd_share[name]) if name in bwd_share else None

    loss, dx, gws, gss, dfinal, drel = _local_step(x[0], loss_target[0], rel_bias, final_g, layer_full, small,
                                                   fwd_hooks, bwd_hooks, after_bwd)

    grads = {}
    red = []
    for i in range(nl):
        hsum, stream = exchanges[i]
        r = _rs_add_chips(hsum, stream.drain(), chip_idx, "rs_add_chips")
        other = _rs_swap(r, "rs_swap")
        both = jnp.concatenate([jnp.where(pc_ == 0, r, other), jnp.where(pc_ == 0, other, r)], axis=0)
        red.append(_unpack_blocks(both))
    for n in BIG_NAMES:
        grads[n] = jnp.stack([red[i][n] for i in range(nl)], axis=0)

    sg = {}
    for n in SMALL_LAYER:
        sg[n] = jnp.stack([gss[i][n] for i in range(nl)], axis=0)
    for n in ("ssd_dt_bias", "ssd_a_log", "ssd_d"):
        sg[n] = sg[n][:, 0, :SSD_HEADS]
    sg["rel_bias"] = drel[:, :REL_BUCKETS].T
    sg["final_g"] = dfinal.reshape(D)
    sg["loss"] = loss[0, :1]
    names_sg = tuple(sg)
    shapes_sg = {n: ((nl,) + W[n].shape[1:] if n in SMALL_LAYER and n not in SHARDED_SMALL else
                     (placed[n].shape if n in SHARDED_SMALL else sg[n].shape)) for n in names_sg}
    for n in names_sg:
        sg[n] = sg[n].reshape(shapes_sg[n])
    tot = _all_reduce_small(_to_rows(_flatten(sg, names_sg)), "allreduce_small")
    tot = _unflatten(tot.reshape(-1), shapes_sg, names_sg)
    loss_out = tot.pop("loss").reshape(())
    for n, cs in SHARDED_SMALL.items():
        tot[n] = lax.dynamic_slice(tot[n], (0, 0, chip * cs), tot[n].shape[:-1] + (cs,))
    grads.update(tot)

    delta, new_m, new_v = {}, {}, {}
    for n in BIG_NAMES:
        shp = W[n].shape
        r2 = lambda a: a.reshape(-1, shp[-1])
        dl, m2, v2 = _adamw(r2(W[n]), r2(grads[n]), r2(M[n]), r2(V[n]), "adamw_" + n)
        delta[n], new_m[n], new_v[n] = dl.reshape(shp), m2.reshape(shp), v2.reshape(shp)
    names_s = tuple(n for n in WEIGHTS if n not in BIG_NAMES)
    shapes_s = {n: W[n].shape for n in names_s}
    pk = lambda t: _to_rows(_flatten(t, names_s))
    dl, m2, v2 = _adamw(pk(W), pk(grads), pk(M), pk(V), "adamw_small")
    delta.update(_unflatten(dl.reshape(-1), shapes_s, names_s))
    new_m.update(_unflatten(m2.reshape(-1), shapes_s, names_s))
    new_v.update(_unflatten(v2.reshape(-1), shapes_s, names_s))

    return (loss_out, dx[None], *[grads[n] for n in WEIGHTS], *[delta[n] for n in WEIGHTS],
            *[new_m[n] for n in WEIGHTS], *[new_v[n] for n in WEIGHTS])
```

```python
import functools
import math

import jax
import jax.numpy as jnp
from jax import lax
from jax.experimental import pallas as pl
from jax.experimental.pallas import tpu as pltpu

F32 = jnp.float32
BF16 = jnp.bfloat16
MESH = pl.DeviceIdType.MESH

D = 1024
HD = 64
GW = 384
AW = 3 * GW
WIN = 128
DILATIONS = (1, 4, 16)
REL_BUCKETS = 32
REL_MAX_DISTANCE = 2048
POOL_WINDOWS = (2, 4, 8, 16)
PG = 256
SSD_HEADS = 16
SSD_N = 128
SSD_CHUNK = 128
XBC = 1536
D_FF = 2816
EPS = 1e-6
NEG = -1e30
HALO = 16
LANES = 128

SEC_A = 3 * AW
SEC_B = D
SEC_C = D + XBC
SEC_D = 3328
SEC_A_PAD = 3584
IN_WIDTH = SEC_A + SEC_B + SEC_C + 16 + 3 * D

ADAM_LR = 0.001
ADAM_B1 = 0.9
ADAM_B2 = 0.999
ADAM_EPS = 1e-08
ADAM_WD = 0.01
ADAM_STEP = 10
ADAM_TILE = 256 * 1024
MM_VMEM_BYTES = 40 * 1024 * 1024
MM_MAX_OUT_TILE = 1024 * 1024
HBM_BYTES_PER_US = 2.0e6
STEP_US = 0.35
MXU_WIDTH = 256
MXU_FLOPS_PER_US = 0.65e6


_ANY = pl.BlockSpec(memory_space=pl.ANY)


def _pick(d, cands):
    for t in cands:
        if d % t == 0:
            return t
    return d


def _iota(shape, dim):
    return lax.broadcasted_iota(jnp.int32, shape, dim)


def _dg(a, b, ca, cb):
    return lax.dot_general(a.astype(BF16), b.astype(BF16), (((ca,), (cb,)), ((), ())),
                           preferred_element_type=F32)


@jax.custom_vjp
def _bdot_nn(a, b):
    return _dg(a, b, 1, 0)


def _nn_fwd(a, b):
    return _dg(a, b, 1, 0), (a, b)


def _nn_bwd(res, g):
    a, b = res
    return _dg(g, b, 1, 1), _dg(a, g, 0, 0)


_bdot_nn.defvjp(_nn_fwd, _nn_bwd)


@jax.custom_vjp
def _bdot_nt(a, b):
    return _dg(a, b, 1, 1)


def _nt_fwd(a, b):
    return _dg(a, b, 1, 1), (a, b)


def _nt_bwd(res, g):
    a, b = res
    return _dg(g, b, 1, 0), _dg(g, a, 0, 0)


_bdot_nt.defvjp(_nt_fwd, _nt_bwd)


@jax.custom_vjp
def _bdot_tn(a, b):
    return _dg(a, b, 0, 0)


def _tn_fwd(a, b):
    return _dg(a, b, 0, 0), (a, b)


def _tn_bwd(res, g):
    a, b = res
    return _dg(b, g, 1, 1), _dg(a, g, 1, 0)


_bdot_tn.defvjp(_tn_fwd, _tn_bwd)


def _fdot(a, b):
    return jnp.dot(a, b, preferred_element_type=F32, precision=lax.Precision.HIGHEST)


def _sigmoid(x):
    return 0.5 * jnp.tanh(0.5 * x) + 0.5


def _silu(x):
    return x * _sigmoid(x)


def _softplus(x):
    return jnp.maximum(x, 0.0) + jnp.log(1.0 + jnp.exp(-jnp.abs(x)))


def _lane_pick(m, h):
    return jnp.sum(jnp.where(_iota(m.shape, 1) == h, m, 0.0), axis=1, keepdims=True)


def _row_pick(m, h):
    return jnp.sum(jnp.where(_iota(m.shape, 0) == h, m, 0.0), axis=0, keepdims=True)


def _stack_rows(rows, n):
    c = rows[0].shape[1]
    r = _iota((n, c), 0)
    out = jnp.zeros((n, c), F32)
    for k, v in enumerate(rows):
        out = out + jnp.where(r == k, v, 0.0)
    return out


def _mm(a, b, *, ta=False, tb=False, add=None, out_dtype=F32, name, hook=None):
    if ta:
        K, M = a.shape
    else:
        M, K = a.shape
    if tb:
        N, Kb = b.shape
    else:
        Kb, N = b.shape
    assert K == Kb, (a.shape, b.shape, ta, tb)
    tm, tn, tk = _mm_tiles(M, N, K, a.dtype.itemsize, b.dtype.itemsize, jnp.dtype(out_dtype).itemsize,
                           0 if add is None else add.dtype.itemsize)
    ni, nj, nk = M // tm, N // tn, K // tk
    ca = 0 if ta else 1
    cb = 1 if tb else 0
    n_in = 2 if add is None else 3
    n_hin = 0 if hook is None else len(hook.inputs)
    n_hout = 0 if hook is None else len(hook.out_shapes)

    def body(*refs):
        a_ref, b_ref = refs[:2]
        add_ref = None if add is None else refs[2]
        o_ref = refs[n_in + n_hin]
        scr = refs[n_in + n_hin + 1 + n_hout:]
        acc_ref = scr[0] if nk > 1 else None
        hargs = (refs[n_in:n_in + n_hin], refs[n_in + n_hin + 1:n_in + n_hin + 1 + n_hout], scr[1 if nk > 1 else 0:])
        i, j, k = pl.program_id(0), pl.program_id(1), pl.program_id(2)
        if hook is not None:
            @pl.when((i == 0) & (j == 0) & (k == 0))
            def _():
                hook.start(*hargs)

        part = _dg(a_ref[...], b_ref[...], ca, cb)

        def finish(r):
            if add_ref is not None:
                r = r + add_ref[...].astype(F32)
            o_ref[...] = r.astype(o_ref.dtype)

        if nk == 1:
            finish(part)
        else:
            @pl.when(k == 0)
            def _():
                acc_ref[...] = part

            @pl.when((k > 0) & (k < nk - 1))
            def _():
                acc_ref[...] += part

            @pl.when(k == nk - 1)
            def _():
                finish(acc_ref[...] + part)

        if hook is not None:
            @pl.when((i == ni - 1) & (j == nj - 1) & (k == nk - 1))
            def _():
                hook.finish(*hargs)

    a_spec = pl.BlockSpec((tk, tm), lambda i, j, k: (k, i)) if ta else pl.BlockSpec((tm, tk), lambda i, j, k: (i, k))
    b_spec = pl.BlockSpec((tn, tk), lambda i, j, k: (j, k)) if tb else pl.BlockSpec((tk, tn), lambda i, j, k: (k, j))
    in_specs = [a_spec, b_spec]
    args = [a, b]
    if add is not None:
        in_specs.append(pl.BlockSpec((tm, tn), lambda i, j, k: (i, j)))
        args.append(add)
    out_specs = [pl.BlockSpec((tm, tn), lambda i, j, k: (i, j))]
    out_shape = [jax.ShapeDtypeStruct((M, N), out_dtype)]
    scratch = [pltpu.VMEM((tm, tn), F32)] if nk > 1 else []
    aliases = {}
    if hook is not None:
        in_specs += [_ANY] * n_hin
        args += list(hook.inputs)
        out_specs += [_ANY] * n_hout
        out_shape += list(hook.out_shapes)
        scratch += list(hook.scratch)
        aliases = {n_in + hi: 1 + ho for hi, ho in hook.aliases.items()}
    sem = ("parallel", "parallel", "arbitrary") if hook is None else ("arbitrary",) * 3
    res = pl.pallas_call(
        body, name=name, grid=(ni, nj, nk), in_specs=in_specs, out_specs=out_specs, out_shape=out_shape,
        scratch_shapes=scratch, input_output_aliases=aliases,
        compiler_params=pltpu.CompilerParams(dimension_semantics=sem),
    )(*args)
    if hook is not None:
        hook.done(res[1:])
    return res[0]


def _wide(v):
    return v.astype(F32) if v.dtype == BF16 else v


def _mmf(a, b, *, tb=False, add=None, pre=None, post=None, out_dtype=F32, name, tm, hook=None):
    if tb:
        N, K = b.shape
    else:
        K, N = b.shape
    M = pre[1][0].shape[0] if pre else a.shape[0]
    tn = N if post else _pick(N, (512, 256, LANES))
    ni, nj = M // tm, N // tn
    cb = 1 if tb else 0
    pre_fn, pre_rows, pre_consts = pre if pre else (None, [], [])
    post_fn, post_rows, post_consts, post_outs, post_accs = post if post else (None, [], [], [], [])
    hook_in = [] if hook is None else list(hook.inputs)
    hook_out = [] if hook is None else list(hook.out_shapes)

    def row_spec(arr):
        return pl.BlockSpec((tm, arr.shape[1]), lambda i, j: (i, 0))

    def const_spec(arr):
        return pl.BlockSpec(arr.shape, lambda i, j, nd=arr.ndim: (0,) * nd)

    args, in_specs = [], []
    for arr in ([a] if not pre else pre_rows):
        args.append(arr)
        in_specs.append(row_spec(arr))
    for arr in pre_consts:
        args.append(arr)
        in_specs.append(const_spec(arr))
    args.append(b)
    in_specs.append(pl.BlockSpec((tn, K), lambda i, j: (j, 0)) if tb else pl.BlockSpec((K, tn), lambda i, j: (0, j)))
    if add is not None:
        args.append(add)
        in_specs.append(pl.BlockSpec((tm, tn), lambda i, j: (i, j)))
    for arr in post_rows:
        args.append(arr)
        in_specs.append(row_spec(arr))
    for arr in post_consts:
        args.append(arr)
        in_specs.append(const_spec(arr))
    n_main = len(args)
    args += hook_in
    in_specs += [_ANY] * len(hook_in)

    out_shape, out_specs = [], []
    if post:
        for c, dt in post_outs:
            out_shape.append(jax.ShapeDtypeStruct((M, c), dt))
            out_specs.append(pl.BlockSpec((tm, c), lambda i, j: (i, 0)))
        for r, c in post_accs:
            out_shape.append(jax.ShapeDtypeStruct((r, c), F32))
            out_specs.append(pl.BlockSpec((r, c), lambda i, j: (0, 0)))
    else:
        out_shape.append(jax.ShapeDtypeStruct((M, N), out_dtype))
        out_specs.append(pl.BlockSpec((tm, tn), lambda i, j: (i, j)))
    if pre:
        out_shape.append(jax.ShapeDtypeStruct((M, K), BF16))
        out_specs.append(pl.BlockSpec((tm, K), lambda i, j: (i, 0)))
    n_out = len(out_shape)
    out_shape += hook_out
    out_specs += [_ANY] * len(hook_out)
    scratch = ([pltpu.VMEM((tm, K), BF16)] if pre else []) + ([] if hook is None else list(hook.scratch))
    aliases = {} if hook is None else {n_main + hi: n_out + ho for hi, ho in hook.aliases.items()}

    def body(*refs):
        ins, outs, scr = refs[:n_main], refs[len(args):len(args) + n_out], refs[len(args) + len(out_shape):]
        hargs = (refs[n_main:len(args)], refs[len(args) + n_out:len(args) + len(out_shape)], scr[1 if pre else 0:])
        i, j = pl.program_id(0), pl.program_id(1)
        if hook is not None:
            @pl.when((i == 0) & (j == 0))
            def _():
                hook.start(*hargs)

        it = iter(ins)
        if pre:
            rows_ = [next(it) for _ in pre_rows]
            consts_ = [next(it) for _ in pre_consts]

            @pl.when(j == 0)
            def _():
                av = pre_fn(*[_wide(r[...]) for r in rows_], *[_wide(r[...]) for r in consts_]).astype(BF16)
                scr[0][...] = av
                outs[-1][...] = av

            at = scr[0][...]
        else:
            at = next(it)[...]
        p = _dg(at, next(it)[...], 1, cb)
        if add is not None:
            p = p + next(it)[...].astype(F32)
        if post:
            rows_ = [next(it) for _ in post_rows]
            consts_ = [next(it) for _ in post_consts]
            res = post_fn(p, *[_wide(r[...]) for r in rows_], *[_wide(r[...]) for r in consts_])
            for r, v in zip(outs[:len(post_outs)], res[:len(post_outs)]):
                r[...] = v.astype(r.dtype)
            for r, v in zip(outs[len(post_outs):], res[len(post_outs):]):
                @pl.when(i == 0)
                def _(r=r, v=v):
                    r[...] = v

                @pl.when(i > 0)
                def _(r=r, v=v):
                    r[...] += v
        else:
            outs[0][...] = p.astype(outs[0].dtype)
        if hook is not None:
            @pl.when((i == ni - 1) & (j == nj - 1))
            def _():
                hook.finish(*hargs)

    res = pl.pallas_call(
        body, name=name, grid=(ni, nj), in_specs=in_specs, out_specs=out_specs, out_shape=out_shape,
        scratch_shapes=scratch, input_output_aliases=aliases,
        compiler_params=pltpu.CompilerParams(dimension_semantics=("arbitrary", "arbitrary")),
    )(*args)
    if hook is not None:
        hook.done(res[n_out:])
    return res[:n_out]


def _mm_tiles(M, N, K, sa, sb, so, sadd):
    def tiles(d):
        return [t for t in range(LANES, min(d, 2048) + 1, LANES) if d % t == 0] or [d]

    best = None
    for tk in [K] + [t for t in tiles(K) if t < K]:
        for tm in tiles(M):
            for tn in tiles(N):
                vmem = 2 * (tm * tk * sa + tk * tn * sb + tm * tn * (so + sadd)) + (tm * tn * 4 if tk < K else 0)
                if vmem > MM_VMEM_BYTES or tm * tn > MM_MAX_OUT_TILE:
                    continue
                a_reads = 1 if tk == K else N // tn
                traffic = M * K * sa * a_reads + K * N * sb * (M // tm) + M * N * (so + sadd)
                steps = (M // tm) * (N // tn) * (K // tk)
                width = -(-tn // MXU_WIDTH) * MXU_WIDTH
                mxu = 2.0 * M * K * N * (width / tn) / MXU_FLOPS_PER_US
                edge = tm * tk * sa + tk * tn * sb + tm * tn * (so + sadd)
                cost = max(traffic / HBM_BYTES_PER_US, mxu) + steps * STEP_US + edge / HBM_BYTES_PER_US
                if best is None or cost < best[0]:
                    best = (cost, tm, tn, tk)
    assert best is not None, (M, N, K)
    return best[1:]


class _Hook:
    def __init__(self, inputs, out_shapes, aliases, scratch, start, finish, done):
        self.inputs, self.out_shapes, self.aliases, self.scratch = inputs, out_shapes, aliases, scratch
        self.start, self.finish, self.done = start, finish, done


class _Ctx:
    def __init__(self, first, last, row0, rows):
        self.first, self.last, self.row0, self.rows = first, last, row0, rows


def _rows(name, fn, ins, outs, accs=(), *, tm, nrows, ncol=1, chunk=None):
    nt = nrows // tm
    hb = tm // HALO
    nh = nrows // HALO
    ch = chunk or tm
    nch = tm // ch
    ins = [(kind, arr, arr.shape[1] if kind == "row" and cw is None else cw, base) for kind, arr, cw, base in ins]

    def row_of(k):
        return next(q for q, s in enumerate(ins) if s[0] == "row" and s[1] is ins[k][1] and s[2:] == ins[k][2:])
    in_specs, args = [], []
    for kind, arr, cw, base in ins:
        if kind == "row":
            in_specs.append(pl.BlockSpec((tm, cw), lambda j, i, base=base: (i, base + j)))
        elif kind == "prev":
            in_specs.append(pl.BlockSpec((HALO, cw), lambda j, i, base=base: (jnp.maximum(i * hb - 1, 0), base + j)))
        elif kind == "next":
            in_specs.append(pl.BlockSpec((HALO, cw), lambda j, i, base=base: (jnp.minimum((i + 1) * hb, nh - 1), base + j)))
        elif kind in ("const", "raw"):
            in_specs.append(pl.BlockSpec(arr.shape, lambda j, i, nd=arr.ndim: (0,) * nd))
        elif kind == "ccol":
            in_specs.append(pl.BlockSpec((arr.shape[0], cw), lambda j, i, base=base: (0, base + j)))
        else:
            raise ValueError(kind)
        args.append(arr)
    out_specs, out_shape = [], []
    for ctot, cw, base, dt in outs:
        out_specs.append(pl.BlockSpec((tm, cw), lambda j, i, base=base: (i, base + j)))
        out_shape.append(jax.ShapeDtypeStruct((nrows, ctot), dt))
    for r, ctot, cw in accs:
        out_specs.append(pl.BlockSpec((r, cw), lambda j, i: (0, j)))
        out_shape.append(jax.ShapeDtypeStruct((r, ctot), F32))
    n_in, n_out = len(ins), len(outs)

    def body(*refs):
        i = pl.program_id(1)
        in_refs, out_refs, acc_refs = refs[:n_in], refs[n_in:n_in + n_out], refs[n_in + n_out:]
        if acc_refs:
            @pl.when(i == 0)
            def _():
                for r in acc_refs:
                    r[...] = jnp.zeros_like(r)

        whole = {k: (in_refs[k][...] if s[0] == "raw" else _wide(in_refs[k][...]))
                 for k, s in enumerate(ins) if s[0] in ("const", "ccol", "raw")}

        def do_chunk(c, carry):
            r0 = pl.multiple_of(c * ch, ch) if nch > 1 else 0
            rows_ = pl.ds(r0, ch)
            vals = []
            for k, (kind, _, _, _) in enumerate(ins):
                r = in_refs[k]
                if k in whole:
                    vals.append(whole[k])
                    continue
                if kind == "row":
                    v = r[rows_, :]
                elif nch == 1:
                    v = r[...]
                elif kind == "prev":
                    inner = in_refs[row_of(k)][pl.ds(pl.multiple_of(jnp.maximum(r0 - HALO, 0), HALO), HALO), :]
                    v = jnp.where(c == 0, r[...], inner)
                else:
                    inner = in_refs[row_of(k)][pl.ds(pl.multiple_of(jnp.minimum(r0 + ch, tm - HALO), HALO), HALO), :]
                    v = jnp.where(c == nch - 1, r[...], inner)
                vals.append(_wide(v))
            ctx = _Ctx((i == 0) & (c == 0), (i == nt - 1) & (c == nch - 1), i * tm + r0, ch)
            res = fn(ctx, *vals)
            for r, v in zip(out_refs, res[:n_out]):
                r[rows_, :] = v.astype(r.dtype)
            for r, v in zip(acc_refs, res[n_out:]):
                r[...] += v
            return carry

        if nch == 1:
            do_chunk(0, 0)
        else:
            lax.fori_loop(0, nch, do_chunk, 0)

    res = pl.pallas_call(
        body, name=name, grid=(ncol, nt), in_specs=in_specs, out_specs=out_specs, out_shape=out_shape,
        compiler_params=pltpu.CompilerParams(dimension_semantics=("arbitrary", "arbitrary")),
    )(*args)
    return res


def _shift_down(xcat, k):
    return xcat if k == 0 else pltpu.roll(xcat, k, 0)


def _shift_up(xcat, k):
    return xcat if k == 0 else pltpu.roll(xcat, xcat.shape[0] - k, 0)


def _with_prev(ctx, halo, x):
    return jnp.concatenate([jnp.where(ctx.first, 0.0, halo), x], axis=0)


def _with_next(ctx, x, halo):
    return jnp.concatenate([x, jnp.where(ctx.last, 0.0, halo)], axis=0)


def _rms_core(x, g):
    r = lax.rsqrt(jnp.mean(x * x, axis=-1, keepdims=True) + EPS)
    return x * r * g


def _rms_post(du, xv, drv, gv):
    _, vjp = jax.vjp(_rms_core, xv, gv)
    dx, dg = vjp(du)
    return [drv + dx, dg]


def _final_loss(x, target, g):
    S = x.shape[0]

    def fn(ctx, xv, tv, gv):
        def f(xx, gg):
            err = _rms_core(xx, gg) - tv
            return 0.5 * jnp.sum(err * err) / D

        loss, vjp = jax.vjp(f, xv, gv)
        dx, dg = vjp(jnp.ones((), F32))
        return [dx, dg, jnp.zeros((1, LANES), F32) + loss]

    return _rows("final_loss", fn, [("row", x, None, 0), ("row", target, None, 0), ("const", g, None, 0)],
                 [(D, D, 0, F32)], [(1, D, D), (1, LANES, LANES)], tm=256, nrows=S, chunk=CHUNK_WIDE)


def _attn_valid(n):
    qi = _iota((WIN, 2 * WIN), 0)
    kk = _iota((WIN, 2 * WIN), 1)
    rel = qi + WIN - kk
    return (rel >= 0) & (rel <= WIN) & ((kk >= WIN) | (n > 0))


def _attn_block(q, kp, kc, vp, vc, b0, b1, valid):
    k = jnp.concatenate([kp, kc], axis=0)
    v = jnp.concatenate([vp, vc], axis=0)
    lo = _iota((WIN, LANES), 1) < HD
    scale = 1.0 / math.sqrt(HD)
    os_, ls_ = [], []
    for hh, b in ((0, b0), (1, b1)):
        qm = jnp.where(lo if hh == 0 else ~lo, q, 0.0)
        s = _bdot_nt(qm, k) * scale + b
        s = jnp.where(valid, s, NEG)
        m = lax.stop_gradient(jnp.max(s, axis=1, keepdims=True))
        p = jnp.exp(s - m)
        l = jnp.sum(p, axis=1, keepdims=True)
        os_.append(_bdot_nn(p, v) / l)
        ls_.append(m + jnp.log(l))
    return jnp.where(lo, os_[0], os_[1]), jnp.where(lo, ls_[0], ls_[1])


def _residue_rows(r, d):
    return pl.ds(0, WIN) if d == 1 else pl.ds(r, WIN, stride=d)


def _for_residues(d, fn):
    if d == 1:
        fn(0, 0)
    else:
        lax.fori_loop(0, d, fn, 0, unroll=min(d, 8))


def _pairs_per_step(d):
    return 3 if d == 1 else 1


def _bias_table(rel_bias, bucket, gi, name):
    def body(t_ref, b_ref, o_ref):
        h = 6 * gi + pl.program_id(0)
        b = b_ref[...]
        acc = jnp.zeros(b.shape, F32)
        for k in range(REL_BUCKETS):
            acc = jnp.where(b == k, t_ref[k, h], acc)
        o_ref[0] = acc

    return pl.pallas_call(
        body, name=name, grid=(6,),
        in_specs=[pl.BlockSpec(memory_space=pltpu.SMEM), pl.BlockSpec((WIN, 2 * WIN), lambda h: (0, 0))],
        out_specs=pl.BlockSpec((1, WIN, 2 * WIN), lambda h: (h, 0, 0)),
        out_shape=jax.ShapeDtypeStruct((6, WIN, 2 * WIN), F32),
    )(rel_bias, bucket)


def _attn_fwd(pa, bias, gi, name):
    S = pa.shape[0]
    d = DILATIONS[gi]
    bt = WIN * d
    nb = S // bt
    hpw = _pairs_per_step(d)
    bw = hpw * LANES
    cb = 3 * gi // hpw

    def body(q_ref, kp_ref, kc_ref, vp_ref, vc_ref, b_ref, o_ref, l_ref):
        valid = _attn_valid(pl.program_id(1))

        def residue(r, carry):
            sl = _residue_rows(r, d)
            for t in range(hpw):
                ln = pl.ds(t * LANES, LANES)
                o, lse = _attn_block(q_ref[sl, ln], kp_ref[sl, ln], kc_ref[sl, ln], vp_ref[sl, ln], vc_ref[sl, ln],
                                     b_ref[2 * t], b_ref[2 * t + 1], valid)
                o_ref[sl, ln] = o
                l_ref[sl, ln] = lse
            return carry

        _for_residues(d, residue)

    def spec(off, prev):
        if prev:
            return pl.BlockSpec((bt, bw), lambda hp, n: (jnp.maximum(n - 1, 0), off // hpw + cb + hp))
        return pl.BlockSpec((bt, bw), lambda hp, n: (n, off // hpw + cb + hp))

    ospec = pl.BlockSpec((bt, bw), lambda hp, n: (n, hp))
    return pl.pallas_call(
        body, name=name, grid=(3 // hpw, nb),
        in_specs=[spec(0, False), spec(9, True), spec(9, False), spec(18, True), spec(18, False),
                  pl.BlockSpec((2 * hpw, WIN, 2 * WIN), lambda hp, n: (hp, 0, 0))],
        out_specs=[ospec, ospec],
        out_shape=[jax.ShapeDtypeStruct((S, GW), F32)] * 2,
        compiler_params=pltpu.CompilerParams(dimension_semantics=("parallel", "arbitrary")),
    )(pa, pa, pa, pa, pa, bias)


def _attn_bwd(pa, bias, do, dlse, db_in, dqkv, gi, name):
    S = pa.shape[0]
    d = DILATIONS[gi]
    bt = WIN * d
    nb = S // bt
    hpw = _pairs_per_step(d)
    bw = hpw * LANES
    cb = 3 * gi // hpw

    def body(q_ref, kp_ref, kc_ref, vp_ref, vc_ref, b_ref, do_ref, dl_ref, dbi_ref, dqi_ref, dki_ref, dvi_ref,
             dq_ref, dk_ref, dv_ref, db_ref, ck, cv):
        n = pl.program_id(1)

        @pl.when(n == 0)
        def _():
            db_ref[...] = dbi_ref[...]
            ck[...] = jnp.zeros_like(ck)
            cv[...] = jnp.zeros_like(cv)

        @pl.when(n < nb)
        def _():
            f = functools.partial(_attn_block, valid=_attn_valid(n))

            def residue(r, carry):
                sl = _residue_rows(r, d)
                cs = pl.ds(pl.multiple_of(r * WIN, WIN), WIN)
                for t in range(hpw):
                    ln = pl.ds(t * LANES, LANES)
                    _, vjp = jax.vjp(f, q_ref[sl, ln], kp_ref[sl, ln], kc_ref[sl, ln], vp_ref[sl, ln], vc_ref[sl, ln],
                                     b_ref[2 * t], b_ref[2 * t + 1])
                    dq, dkp, dkc, dvp, dvc, db0, db1 = vjp((do_ref[sl, ln], dl_ref[sl, ln]))
                    dq_ref[sl, ln] = dq
                    dk_ref[sl, ln] = ck[cs, ln] + dkp
                    dv_ref[sl, ln] = cv[cs, ln] + dvp
                    ck[cs, ln] = dkc
                    cv[cs, ln] = dvc
                    db_ref[2 * t] += db0
                    db_ref[2 * t + 1] += db1
                return carry

            _for_residues(d, residue)

        @pl.when(n == nb)
        def _():
            def residue(r, carry):
                sl = _residue_rows(r, d)
                cs = pl.ds(pl.multiple_of(r * WIN, WIN), WIN)
                dk_ref[sl, :] = ck[cs, :]
                dv_ref[sl, :] = cv[cs, :]
                return carry

            _for_residues(d, residue)

    def cur(n):
        return jnp.minimum(n, nb - 1)

    def spec(off, prev):
        if prev:
            return pl.BlockSpec((bt, bw), lambda hp, n: (jnp.maximum(cur(n) - 1, 0), off // hpw + cb + hp))
        return pl.BlockSpec((bt, bw), lambda hp, n: (cur(n), off // hpw + cb + hp))

    gspec = pl.BlockSpec((bt, bw), lambda hp, n: (cur(n), hp))
    bspec = pl.BlockSpec((2 * hpw, WIN, 2 * WIN), lambda hp, n: (hp, 0, 0))
    qspec = pl.BlockSpec((bt, bw), lambda hp, n: (cur(n), cb + hp))
    kspec = pl.BlockSpec((bt, bw), lambda hp, n: (jnp.maximum(n - 1, 0), cb + hp))
    dq, dk, dv, db = pl.pallas_call(
        body, name=name, grid=(3 // hpw, nb + 1),
        in_specs=[spec(0, False), spec(9, True), spec(9, False), spec(18, True), spec(18, False),
                  bspec, gspec, gspec, bspec, _ANY, _ANY, _ANY],
        out_specs=[qspec, kspec, kspec, bspec],
        out_shape=[jax.ShapeDtypeStruct((S, AW), F32)] * 3 + [jax.ShapeDtypeStruct((6, WIN, 2 * WIN), F32)],
        scratch_shapes=[pltpu.VMEM((bt, bw), F32), pltpu.VMEM((bt, bw), F32)],
        input_output_aliases={9: 0, 10: 1, 11: 2},
        compiler_params=pltpu.CompilerParams(dimension_semantics=("arbitrary", "arbitrary")),
    )(pa, pa, pa, pa, pa, bias, do, dlse, db_in, *dqkv)
    return (dq, dk, dv), db


def _mix_core(o0, o1, o2, l0, l1, l2):
    m = lax.stop_gradient(jnp.maximum(jnp.maximum(l0, l1), l2))
    e0, e1, e2 = jnp.exp(l0 - m), jnp.exp(l1 - m), jnp.exp(l2 - m)
    return (e0 * o0 + e1 * o1 + e2 * o2) / (e0 + e1 + e2)


def _mix_fwd(os_, ls_, name):
    S = os_[0].shape[0]
    ins = [("row", a, None, 0) for a in (*os_, *ls_)]
    return _rows(name, lambda ctx, *v: [_mix_core(*v)], ins, [(GW, GW, 0, BF16)], tm=256, nrows=S, chunk=CHUNK_NARROW)[0]


def _mix_bwd(os_, ls_, datt, name):
    S = datt.shape[0]

    def fn(ctx, *v):
        _, vjp = jax.vjp(_mix_core, *v[:6])
        return list(vjp(v[6]))

    ins = [("row", a, None, 0) for a in (*os_, *ls_, datt)]
    outs = [(GW, GW, 0, F32)] * 6
    r = _rows(name, fn, ins, outs, tm=256, nrows=S, chunk=CHUNK_NARROW)
    return r[:3], r[3:]


def _t5_bucket(dist):
    max_exact = REL_BUCKETS // 2
    is_small = dist < max_exact
    nf = jnp.maximum(dist, 1).astype(F32)
    large = max_exact + (jnp.log(nf / max_exact) / math.log(REL_MAX_DISTANCE / max_exact)
                         * (REL_BUCKETS - max_exact)).astype(jnp.int32)
    large = jnp.minimum(large, REL_BUCKETS - 1)
    return jnp.where(is_small, dist, large)


def _buckets(d):
    qi = jnp.arange(WIN)[:, None]
    kk = jnp.arange(2 * WIN)[None, :]
    rel = qi + WIN - kk
    return _t5_bucket(jnp.clip(rel, 0, None) * d)


def _pool_cnt(ctx, w):
    pos = ctx.row0 + _iota((ctx.rows, PG), 0) + 1
    return jnp.minimum(pos, w).astype(F32)


def _pool_d(ctx, halo, u):
    ds = []
    for g, w in enumerate(POOL_WINDOWS):
        ug = u[:, g * PG:(g + 1) * PG]
        s = _with_prev(ctx, halo[:, g * PG:(g + 1) * PG], ug)
        step = 1
        while step < w:
            s = s + _shift_down(s, step)
            step *= 2
        ds.append(s[HALO:] / _pool_cnt(ctx, w) - ug)
    return ds


def _pool_fwd(pb, pw, scale, name):
    S = pb.shape[0]

    def fn(ctx, halo, u, w, sc):
        ds = _pool_d(ctx, halo, u)
        return [jnp.concatenate([_dg(ds[k], w[k], 1, 0) for k in range(4)], axis=1) * sc]

    return _rows(name, fn, [("prev", pb, D, 0), ("row", pb, None, 0), ("raw", pw, None, 0), ("const", scale, None, 0)],
                 [(D, D, 0, BF16)], tm=256, nrows=S, chunk=CHUNK_POOL)[0]


def _pool_bwd(pb, pw, scale, dpo, name):
    S = pb.shape[0]

    def fn1(ctx, halo, u, w, sc, dy):
        ds = _pool_d(ctx, halo, u)
        dyp = dy * sc
        y = jnp.concatenate([_dg(ds[k], w[k], 1, 0) for k in range(4)], axis=1)
        es, dws = [], []
        for k, wd in enumerate(POOL_WINDOWS):
            cols = slice(k * PG, (k + 1) * PG)
            es.append(_dg(dyp[:, cols], w[k], 1, 1) / _pool_cnt(ctx, wd))
            dws.append(_dg(ds[k], dyp[:, cols], 0, 0))
        return [jnp.concatenate(es, axis=1), jnp.concatenate(dws, axis=0), jnp.sum(dy * y, axis=0, keepdims=True)]

    e, dpw, dsc = _rows(name + "_a", fn1,
                        [("prev", pb, D, 0), ("row", pb, None, 0), ("raw", pw, None, 0), ("const", scale, None, 0),
                         ("row", dpo, None, 0)],
                        [(D, D, 0, F32)], [(4 * PG, PG, PG), (1, D, D)], tm=256, nrows=S, chunk=CHUNK_POOL)

    def fn2(ctx, ev, halo):
        outs = []
        for g, w in enumerate(POOL_WINDOWS):
            eg = ev[:, g * PG:(g + 1) * PG]
            s = _with_next(ctx, eg, halo[:, g * PG:(g + 1) * PG])
            step = 1
            while step < w:
                s = s + _shift_up(s, step)
                step *= 2
            outs.append(s[:ctx.rows] - eg * _pool_cnt(ctx, w))
        return [jnp.concatenate(outs, axis=1)]

    du = _rows(name + "_b", fn2, [("row", e, None, 0), ("next", e, D, 0)], [(D, D, 0, BF16)], tm=256, nrows=S,
               chunk=CHUNK_POOL)[0]
    return du, dpw, dsc


def _conv_taps(ctx, halo, x, K):
    cat = _with_prev(ctx, halo, x)
    return [_shift_down(cat, K - 1 - k)[HALO:] for k in range(K)]


def _conv_pre(taps, w, b):
    acc = b
    for k, t in enumerate(taps):
        acc = acc + t * _row_pick(w, k)
    return acc


CW = 256
CWS = 512
CONV_TM = 512
CHUNK_NARROW = None
CHUNK_POOL = None
CHUNK_WIDE = None


def _ext_taps(ctx, prev, x, nxt, K):
    cat = jnp.concatenate([jnp.where(ctx.first, 0.0, prev), x, jnp.where(ctx.last, 0.0, nxt)], axis=0)
    return [_shift_down(cat, K - 1 - k)[HALO:] for k in range(K)]


def _conv_t_rows(dp, w, K, tm):
    acc = jnp.zeros((tm, dp.shape[1]), F32)
    for k in range(K):
        acc = acc + _shift_up(dp, K - 1 - k)[:tm] * _row_pick(w, k)
    return acc


def _ssd_conv_fwd(pc, w, b, name):
    S = pc.shape[0]
    base = D // CWS

    def fn(ctx, halo, x, wv, bv):
        return [_silu(_conv_pre(_conv_taps(ctx, halo, x, 4), wv, bv))]

    return _rows(name, fn, [("prev", pc, CWS, base), ("row", pc, CWS, base), ("ccol", w, CWS, 0), ("ccol", b, CWS, 0)],
                 [(XBC, CWS, 0, F32)], tm=CONV_TM, nrows=S, ncol=XBC // CWS, chunk=CHUNK_POOL)[0]


def _ssd_conv_bwd(pc, w, b, dy, name):
    S = pc.shape[0]
    base = D // CWS

    def fn(ctx, prev, x, nxt, wv, bv, dyv, dyn):
        n = ctx.rows
        taps = _ext_taps(ctx, prev, x, nxt, 4)
        pre = _conv_pre(taps, wv, bv)
        sg = _sigmoid(pre)
        dye = jnp.concatenate([dyv, jnp.where(ctx.last, 0.0, dyn)], axis=0)
        dpre = dye * sg * (1.0 + pre * (1.0 - sg))
        dw = _stack_rows([jnp.sum(dpre[:n] * t[:n], axis=0, keepdims=True) for t in taps], 4)
        return [_conv_t_rows(dpre, wv, 4, n), dw, jnp.sum(dpre[:n], axis=0, keepdims=True)]

    return _rows(name, fn,
                 [("prev", pc, CWS, base), ("row", pc, CWS, base), ("next", pc, CWS, base), ("ccol", w, CWS, 0),
                  ("ccol", b, CWS, 0), ("row", dy, CWS, 0), ("next", dy, CWS, 0)],
                 [(XBC, CWS, 0, BF16)], [(4, XBC, CWS), (1, XBC, CWS)], tm=CONV_TM, nrows=S, ncol=XBC // CWS,
                 chunk=CHUNK_POOL)


NFC = D_FF // CW


def _ffn_act_fwd(h, w, b, name):
    S = h.shape[0]

    def fn(ctx, ha, a, hv, v, wa, wv, ba, bv):
        pa = _conv_pre(_conv_taps(ctx, ha, a, 3), wa, ba)
        pv = _conv_pre(_conv_taps(ctx, hv, v, 3), wv, bv)
        return [_silu(pa) * pv]

    return _rows(name, fn,
                 [("prev", h, CW, 0), ("row", h, CW, 0), ("prev", h, CW, NFC), ("row", h, CW, NFC),
                  ("ccol", w, CW, 0), ("ccol", w, CW, NFC), ("ccol", b, CW, 0), ("ccol", b, CW, NFC)],
                 [(D_FF, CW, 0, BF16)], tm=CONV_TM, nrows=S, ncol=NFC, chunk=CHUNK_NARROW)[0]


def _ffn_act_bwd(h, w, b, df, name):
    S = h.shape[0]

    def fn(ctx, pa_, a, na, pv_, v, nv, wa, wv, ba, bv, dfv, dfn):
        n = ctx.rows
        ta = _ext_taps(ctx, pa_, a, na, 3)
        tv = _ext_taps(ctx, pv_, v, nv, 3)
        pa = _conv_pre(ta, wa, ba)
        pv = _conv_pre(tv, wv, bv)
        sg = _sigmoid(pa)
        dfe = jnp.concatenate([dfv, jnp.where(ctx.last, 0.0, dfn)], axis=0)
        dpa = dfe * pv * sg * (1.0 + pa * (1.0 - sg))
        dpv = dfe * pa * sg
        res = [_conv_t_rows(dpa, wa, 3, n), _conv_t_rows(dpv, wv, 3, n)]
        for dp, taps in ((dpa, ta), (dpv, tv)):
            res.append(_stack_rows([jnp.sum(dp[:n] * t[:n], axis=0, keepdims=True) for t in taps], 3))
        for dp in (dpa, dpv):
            res.append(jnp.sum(dp[:n], axis=0, keepdims=True))
        return res

    ins = []
    for base in (0, NFC):
        ins += [("prev", h, CW, base), ("row", h, CW, base), ("next", h, CW, base)]
    ins += [("ccol", w, CW, 0), ("ccol", w, CW, NFC), ("ccol", b, CW, 0), ("ccol", b, CW, NFC),
            ("row", df, CW, 0), ("next", df, CW, 0)]
    dha, dhv, dwa, dwv, dba, dbv = _rows(
        name, fn, ins, [(D_FF, CW, 0, BF16)] * 2, [(3, D_FF, CW)] * 2 + [(1, D_FF, CW)] * 2, tm=CONV_TM, nrows=S, ncol=NFC,
        chunk=CHUNK_NARROW)
    return dha, dhv, jnp.concatenate([dwa, dwv], axis=1), jnp.concatenate([dba, dbv], axis=1)


NSLAB = D // LANES
CPS = 2


def _ssd_chunk(xs, Bs, Cs, dtraw, dtb, alog, prev):
    lsz = SSD_CHUNK
    lane = _iota((lsz, LANES), 1)
    row = _iota((lsz, LANES), 0)
    dt = jnp.where(lane < SSD_HEADS, _softplus(dtraw + dtb), 0.0)
    a = dt * (-jnp.exp(alog))
    tril = row >= lane
    a_cs = _fdot(tril.astype(F32), a)
    a_cst = a_cs.T
    a_last = jnp.sum(a, axis=0, keepdims=True)
    lo = lane < HD
    top = row < HD
    cbs = [_bdot_nt(Cs[g], Bs[g]) for g in range(2)]
    ys, news = [], []
    for s in range(NSLAB):
        g = s // (NSLAB // 2)
        cols, lms, dts, als = [], [], [], []
        for hh in range(2):
            h = 2 * s + hh
            col = _lane_pick(a_cs, h)
            seg = col - _row_pick(a_cst, h)
            lms.append(jnp.exp(jnp.where(tril, seg, NEG)))
            cols.append(col)
            dts.append(_lane_pick(dt, h))
            als.append(_lane_pick(a_last, h))
        col_x = jnp.where(lo, cols[0], cols[1])
        al_x = jnp.where(lo, als[0], als[1])
        xc = xs[s] * jnp.where(lo, dts[0], dts[1])
        yd = jnp.where(lo, _bdot_nn(cbs[g] * lms[0], xc), _bdot_nn(cbs[g] * lms[1], xc))
        yoff = _bdot_nt(Cs[g], prev[s]) * jnp.exp(col_x)
        ys.append(yd + yoff)
        st = _bdot_tn(xc * jnp.exp(al_x - col_x), Bs[g])
        news.append(prev[s] * jnp.exp(jnp.where(top, als[0], als[1])) + st)
    return ys, news


def _ssd_scan_fwd(xbc_c, pd, dtb, alog, name):
    S = xbc_c.shape[0]
    nc = S // SSD_CHUNK
    rows_ = CPS * SSD_CHUNK

    def body(x_ref, b_ref, c_ref, dt_ref, dtb_ref, al_ref, y_ref, st_ref, state):
        c = pl.program_id(0)

        @pl.when(c == 0)
        def _():
            state[...] = jnp.zeros_like(state)

        prev = [state[s * LANES:(s + 1) * LANES, :] for s in range(NSLAB)]
        for u in range(CPS):
            rw = pl.ds(u * SSD_CHUNK, SSD_CHUNK)
            xs = [x_ref[rw, s * LANES:(s + 1) * LANES] for s in range(NSLAB)]
            Bs = [b_ref[rw, g * SSD_N:(g + 1) * SSD_N] for g in range(2)]
            Cs = [c_ref[rw, g * SSD_N:(g + 1) * SSD_N] for g in range(2)]
            for s in range(NSLAB):
                st_ref[u, s * LANES:(s + 1) * LANES, :] = prev[s]
            ys, prev = _ssd_chunk(xs, Bs, Cs, dt_ref[rw, :].astype(F32), dtb_ref[...], al_ref[...], prev)
            for s in range(NSLAB):
                y_ref[rw, s * LANES:(s + 1) * LANES] = ys[s]
        for s in range(NSLAB):
            state[s * LANES:(s + 1) * LANES, :] = prev[s]

    return pl.pallas_call(
        body, name=name, grid=(nc // CPS,),
        in_specs=[pl.BlockSpec((rows_, D), lambda c: (c, 0)),
                  pl.BlockSpec((rows_, 2 * SSD_N), lambda c: (c, D // (2 * SSD_N))),
                  pl.BlockSpec((rows_, 2 * SSD_N), lambda c: (c, D // (2 * SSD_N) + 1)),
                  pl.BlockSpec((rows_, LANES), lambda c: (c, 0)),
                  pl.BlockSpec((1, LANES), lambda c: (0, 0)), pl.BlockSpec((1, LANES), lambda c: (0, 0))],
        out_specs=[pl.BlockSpec((rows_, D), lambda c: (c, 0)), pl.BlockSpec((CPS, D, SSD_N), lambda c: (c, 0, 0))],
        out_shape=[jax.ShapeDtypeStruct((S, D), F32), jax.ShapeDtypeStruct((nc, D, SSD_N), F32)],
        scratch_shapes=[pltpu.VMEM((D, SSD_N), F32)],
        compiler_params=pltpu.CompilerParams(dimension_semantics=("arbitrary",)),
    )(xbc_c, xbc_c, xbc_c, pd, dtb, alog)


def _ssd_scan_bwd(xbc_c, pd, dtb, alog, states, dy, dxs_skip, name):
    S = xbc_c.shape[0]
    nc = S // SSD_CHUNK
    rows_ = CPS * SSD_CHUNK

    def body(x_ref, b_ref, c_ref, dt_ref, dtb_ref, al_ref, st_ref, dy_ref, sk_ref,
             dx_ref, ddt_ref, ddtb_ref, dal_ref, dstate):
        c = pl.program_id(0)

        @pl.when(c == 0)
        def _():
            dstate[...] = jnp.zeros_like(dstate)
            ddtb_ref[...] = jnp.zeros_like(ddtb_ref)
            dal_ref[...] = jnp.zeros_like(dal_ref)

        dnew = [dstate[s * LANES:(s + 1) * LANES, :] for s in range(NSLAB)]
        for u in reversed(range(CPS)):
            rw = pl.ds(u * SSD_CHUNK, SSD_CHUNK)
            xs = [x_ref[rw, s * LANES:(s + 1) * LANES] for s in range(NSLAB)]
            Bs = [b_ref[rw, g * SSD_N:(g + 1) * SSD_N] for g in range(2)]
            Cs = [c_ref[rw, g * SSD_N:(g + 1) * SSD_N] for g in range(2)]
            prev = [st_ref[u, s * LANES:(s + 1) * LANES, :] for s in range(NSLAB)]
            _, vjp = jax.vjp(_ssd_chunk, xs, Bs, Cs, dt_ref[rw, :].astype(F32), dtb_ref[...], al_ref[...], prev)
            dys = [dy_ref[rw, s * LANES:(s + 1) * LANES] for s in range(NSLAB)]
            dxs, dBs, dCs, ddt, ddtb, dal, dnew = vjp((dys, dnew))
            for s in range(NSLAB):
                dx_ref[rw, s * LANES:(s + 1) * LANES] = dxs[s] + sk_ref[rw, s * LANES:(s + 1) * LANES]
            for g in range(2):
                dx_ref[rw, D + g * SSD_N:D + (g + 1) * SSD_N] = dBs[g]
                dx_ref[rw, D + 2 * SSD_N + g * SSD_N:D + 2 * SSD_N + (g + 1) * SSD_N] = dCs[g]
            ddt_ref[rw, :] = ddt
            ddtb_ref[...] += ddtb
            dal_ref[...] += dal
        for s in range(NSLAB):
            dstate[s * LANES:(s + 1) * LANES, :] = dnew[s]

    def rv(c):
        return nc // CPS - 1 - c

    return pl.pallas_call(
        body, name=name, grid=(nc // CPS,),
        in_specs=[pl.BlockSpec((rows_, D), lambda c: (rv(c), 0)),
                  pl.BlockSpec((rows_, 2 * SSD_N), lambda c: (rv(c), D // (2 * SSD_N))),
                  pl.BlockSpec((rows_, 2 * SSD_N), lambda c: (rv(c), D // (2 * SSD_N) + 1)),
                  pl.BlockSpec((rows_, LANES), lambda c: (rv(c), 0)),
                  pl.BlockSpec((1, LANES), lambda c: (0, 0)), pl.BlockSpec((1, LANES), lambda c: (0, 0)),
                  pl.BlockSpec((CPS, D, SSD_N), lambda c: (rv(c), 0, 0)),
                  pl.BlockSpec((rows_, D), lambda c: (rv(c), 0)),
                  pl.BlockSpec((rows_, D), lambda c: (rv(c), 0))],
        out_specs=[pl.BlockSpec((rows_, XBC), lambda c: (rv(c), 0)),
                   pl.BlockSpec((rows_, LANES), lambda c: (rv(c), 0)),
                   pl.BlockSpec((1, LANES), lambda c: (0, 0)), pl.BlockSpec((1, LANES), lambda c: (0, 0))],
        out_shape=[jax.ShapeDtypeStruct((S, XBC), F32), jax.ShapeDtypeStruct((S, LANES), F32),
                   jax.ShapeDtypeStruct((1, LANES), F32), jax.ShapeDtypeStruct((1, LANES), F32)],
        scratch_shapes=[pltpu.VMEM((D, SSD_N), F32)],
        compiler_params=pltpu.CompilerParams(dimension_semantics=("arbitrary",)),
    )(xbc_c, xbc_c, xbc_c, pd, dtb, alog, states, dy, dxs_skip)


def _ssd_post_core(y, xs, z, d128, nw):
    tm = y.shape[0]
    ex = (_iota((LANES, D), 1) // HD == _iota((LANES, D), 0)).astype(F32)
    d_x = jnp.sum(_fdot(jnp.broadcast_to(d128, (8, LANES)), ex), axis=0, keepdims=True) * 0.125
    y2 = (y + d_x * xs) * _silu(z)
    lo = _iota((tm, D), 1) < D // 2
    sq = y2 * y2
    ms0 = jnp.sum(jnp.where(lo, sq, 0.0), axis=-1, keepdims=True) / (D // 2)
    ms1 = jnp.sum(jnp.where(lo, 0.0, sq), axis=-1, keepdims=True) / (D // 2)
    r = jnp.where(lo, lax.rsqrt(ms0 + EPS), lax.rsqrt(ms1 + EPS))
    return y2 * r * nw


def _ssd_post_ins(y, xbc_c, pc, d128, nw):
    return [("row", y, None, 0), ("row", xbc_c, D, 0), ("row", pc, D, 0), ("const", d128, None, 0), ("const", nw, None, 0)]


def _ssd_post_fwd(y, xbc_c, pc, d128, nw, name):
    S = y.shape[0]
    return _rows(name, lambda ctx, *v: [_ssd_post_core(*v)], _ssd_post_ins(y, xbc_c, pc, d128, nw),
                 [(D, D, 0, BF16)], tm=256, nrows=S, chunk=CHUNK_WIDE)[0]


def _ssd_post_bwd(y, xbc_c, pc, d128, nw, dout, name):
    S = y.shape[0]

    def fn(ctx, *v):
        _, vjp = jax.vjp(_ssd_post_core, *v[:5])
        return list(vjp(v[5]))

    return _rows(name, fn, _ssd_post_ins(y, xbc_c, pc, d128, nw) + [("row", dout, None, 0)],
                 [(D, D, 0, F32), (D, D, 0, F32), (D, D, 0, BF16)], [(1, LANES, LANES), (1, D, D)], tm=256, nrows=S,
                 chunk=CHUNK_WIDE)


def _gates_core(g0, g1, g2, b0, b1, b2, ya, yb, yc):
    return _sigmoid(g0 + b0) * ya + _sigmoid(g1 + b1) * yb + _sigmoid(g2 + b2) * yc


def _gate_parts(pdv, bv):
    gp = pltpu.roll(pdv, SEC_D - 16, 1)
    return [gp[:, k * D:(k + 1) * D] for k in range(3)] + [bv[:, k * D:(k + 1) * D] for k in range(3)]


def _gates_fwd(pd, bg, ya, yb, yc, name):
    S = pd.shape[0]

    def fn(ctx, pdv, bv, a, b, c):
        return [_gates_core(*_gate_parts(pdv, bv), a, b, c)]

    return _rows(name, fn, [("row", pd, None, 0), ("const", bg, None, 0), ("row", ya, None, 0), ("row", yb, None, 0),
                            ("row", yc, None, 0)], [(D, D, 0, BF16)], tm=256, nrows=S, chunk=CHUNK_WIDE)[0]


def _gates_post(dm, pdv, a, b, c, bv):
    _, vjp = jax.vjp(_gates_core, *_gate_parts(pdv, bv), a, b, c)
    g = vjp(dm)
    return [g[6], g[7], g[8], jnp.concatenate(g[0:3], axis=1), jnp.concatenate(g[3:6], axis=1)]


def _adamw(w, g, m, v, name):
    rows, C = w.shape
    tm = _pick(rows, [t for t in (512, 256, 128, 64, 32, 16, 8) if t * C <= ADAM_TILE])

    def fn(ctx, wv, gv, mv, vv):
        m2 = ADAM_B1 * mv + (1.0 - ADAM_B1) * gv
        v2 = ADAM_B2 * vv + (1.0 - ADAM_B2) * jnp.square(gv)
        m_hat = m2 / (1.0 - ADAM_B1 ** ADAM_STEP)
        v_hat = v2 / (1.0 - ADAM_B2 ** ADAM_STEP)
        delta = -ADAM_LR * (m_hat / (jnp.sqrt(v_hat) + ADAM_EPS) + ADAM_WD * wv)
        return [delta, m2, v2]

    return _rows(name, fn, [("row", a, None, 0) for a in (w, g, m, v)], [(C, C, 0, F32)] * 3, tm=tm, nrows=rows)


def _position():
    return lax.axis_index("x"), lax.axis_index("y"), lax.axis_index("c")


def _other_chips(x, y):
    return [(1 - x, y), (x, 1 - y), (1 - x, 1 - y)]


_HBM = pl.BlockSpec(memory_space=pltpu.HBM)


def _gather_parts(half, lo, n):
    def copies(p_ref, out_ref, send_sems, recv_sems):
        x, y, c = _position()
        sibling = (x, y, 1 - c)
        chips = _other_chips(x, y)

        def slab(chip, h):
            return out_ref.at[2 * chip[0] + chip[1], pl.ds(h * half + lo, n), :]

        def copy(k, src, dst, to):
            return pltpu.make_async_remote_copy(src_ref=src, dst_ref=dst, send_sem=send_sems.at[k],
                                                recv_sem=recv_sems.at[k], device_id=to, device_id_type=MESH)

        first = [copy(j, p_ref.at[pl.ds(c * half + lo, n), :], slab((x, y), c), (*chip, c)) for j, chip in enumerate(chips)]
        passed = [copy(3 + j, slab(chip, c), slab(chip, c), sibling) for j, chip in enumerate(chips)]
        from_chips = [copy(j, slab(chip, c), slab(chip, c), (x, y, c)) for j, chip in enumerate(chips)]
        from_sibling = [copy(3 + j, slab(chip, 1 - c), slab(chip, 1 - c), (x, y, c)) for j, chip in enumerate(chips)]
        return first, passed, from_chips, from_sibling

    def start(ins, outs, scr):
        for cp in copies(ins[0], outs[0], *scr)[0]:
            cp.start()

    def finish(ins, outs, scr):
        first, passed, from_chips, from_sibling = copies(ins[0], outs[0], *scr)
        for j in range(3):
            from_chips[j].wait_recv()
            passed[j].start()
        for cp in from_sibling:
            cp.wait_recv()
        for cp in first + passed:
            cp.wait_send()

    return start, finish


def _rs_chip_parts(lo, n):
    def copies(h_ref, out_ref, send_sems, recv_sems):
        x, y, c = _position()
        return [pltpu.make_async_remote_copy(src_ref=h_ref.at[2 * chip[0] + chip[1], pl.ds(lo, n), :],
                                             dst_ref=out_ref.at[j, pl.ds(lo, n), :],
                                             send_sem=send_sems.at[j], recv_sem=recv_sems.at[j],
                                             device_id=(*chip, c), device_id_type=MESH)
                for j, chip in enumerate(_other_chips(x, y))]

    def start(ins, outs, scr):
        for cp in copies(ins[0], outs[0], *scr):
            cp.start()

    def finish(ins, outs, scr):
        for cp in copies(ins[0], outs[0], *scr):
            cp.wait()

    return start, finish


class _Stream:
    def __init__(self, src, buf, parts, nsem, units, name):
        self.src, self.buf, self.parts, self.nsem, self.name = src, buf, parts, nsem, name
        self.next, self.units = 0, units

    def _scratch(self):
        return [pltpu.SemaphoreType.DMA((self.nsem,)), pltpu.SemaphoreType.DMA((self.nsem,))]

    def _take(self, units):
        units = min(units, self.units - self.next)
        lo = self.next * 16
        self.next += units
        return lo, units * 16

    def _set(self, outs):
        self.buf = outs[0]

    def hook(self, units):
        lo, n = self._take(units)
        if n == 0:
            return None
        start, finish = self.parts(lo, n)
        return _Hook([self.src, self.buf], [jax.ShapeDtypeStruct(self.buf.shape, self.buf.dtype)], {1: 0},
                     self._scratch(), start, finish, self._set)

    def drain(self):
        lo, n = self._take(self.units)
        if n:
            start, finish = self.parts(lo, n)

            def body(s_ref, b_ref, o_ref, send_sems, recv_sems):
                args = ((s_ref, b_ref), (o_ref,), (send_sems, recv_sems))
                start(*args)
                finish(*args)

            self.buf = pl.pallas_call(
                body, name=self.name, in_specs=[_ANY, _ANY], out_specs=_ANY,
                out_shape=jax.ShapeDtypeStruct(self.buf.shape, self.buf.dtype),
                scratch_shapes=self._scratch(), input_output_aliases={1: 0},
            )(self.src, self.buf)
        return self.buf


def _rs_pair_parts(half, lo, n):
    def copy(g_ref, out_ref, send_sems, recv_sems):
        x, y, c = _position()
        return pltpu.make_async_remote_copy(
            src_ref=g_ref.at[pl.ds(0, 4), pl.ds((1 - c) * half + lo, n), :], dst_ref=out_ref.at[pl.ds(0, 4), pl.ds(lo, n), :],
            send_sem=send_sems.at[0], recv_sem=recv_sems.at[0], device_id=(x, y, 1 - c), device_id_type=MESH)

    def start(ins, outs, scr):
        copy(ins[0], outs[0], *scr).start()

    def finish(ins, outs, scr):
        copy(ins[0], outs[0], *scr).wait()

    return start, finish


def _rs_swap(r, name):
    Rh, C = r.shape

    def body(r_ref, out_ref, send_sem, recv_sem):
        x, y, c = _position()
        cp = pltpu.make_async_remote_copy(src_ref=r_ref, dst_ref=out_ref, send_sem=send_sem,
                                          recv_sem=recv_sem, device_id=(x, y, 1 - c), device_id_type=MESH)
        cp.start()
        cp.wait()

    return pl.pallas_call(
        body, name=name, in_specs=[_HBM], out_specs=_HBM,
        out_shape=jax.ShapeDtypeStruct((Rh, C), r.dtype),
        scratch_shapes=[pltpu.SemaphoreType.DMA, pltpu.SemaphoreType.DMA],
    )(r)


def _rs_add_pair(g, recv, cidx, name):
    _, R, C = g.shape
    Rh = R // 2
    tm = _pick(Rh, (400, 280, 200, 160, 80, 40, 16, 8))
    nt = Rh // tm

    def body(c_ref, g_ref, r_ref, o_ref):
        o_ref[...] = (g_ref[...].astype(F32) + r_ref[...].astype(F32)).astype(o_ref.dtype)

    return pl.pallas_call(
        body, name=name,
        grid_spec=pltpu.PrefetchScalarGridSpec(
            num_scalar_prefetch=1, grid=(4, nt),
            in_specs=[pl.BlockSpec((1, tm, C), lambda k, i, cr: (k, cr[0] * nt + i, 0)),
                      pl.BlockSpec((1, tm, C), lambda k, i, cr: (k, i, 0))],
            out_specs=pl.BlockSpec((1, tm, C), lambda k, i, cr: (k, i, 0))),
        out_shape=jax.ShapeDtypeStruct((4, Rh, C), BF16),
    )(cidx, g, recv)


def _rs_add_chips(h, recv, chip_idx, name):
    _, Rh, C = h.shape
    tm = _pick(Rh, (400, 280, 200, 160, 80, 40, 16, 8))

    def body(c_ref, h_ref, r_ref, o_ref):
        acc = h_ref[0].astype(F32)
        for j in range(3):
            acc = acc + r_ref[j].astype(F32)
        o_ref[...] = acc

    return pl.pallas_call(
        body, name=name,
        grid_spec=pltpu.PrefetchScalarGridSpec(
            num_scalar_prefetch=1, grid=(Rh // tm,),
            in_specs=[pl.BlockSpec((1, tm, C), lambda i, cr: (cr[0], i, 0)), pl.BlockSpec((3, tm, C), lambda i, cr: (0, i, 0))],
            out_specs=pl.BlockSpec((tm, C), lambda i, cr: (i, 0))),
        out_shape=jax.ShapeDtypeStruct((Rh, C), F32),
    )(chip_idx, h, recv)


def _all_reduce_small(vec, name):
    n, C = vec.shape

    def body(v_ref, out_ref, buf, send_sems, recv_sems):
        x, y, c = _position()

        def flip(k):
            return ((1 - x) if k & 4 else x, (1 - y) if k & 2 else y, (1 - c) if k & 1 else c)

        def idx(p):
            return 4 * p[0] + 2 * p[1] + p[2]

        me = idx((x, y, c))
        buf[me] = v_ref[...]
        cps = [pltpu.make_async_remote_copy(src_ref=v_ref, dst_ref=buf.at[me], send_sem=send_sems.at[k - 1],
                                            recv_sem=recv_sems.at[k - 1], device_id=flip(k), device_id_type=MESH)
               for k in range(1, 8)]
        for cp in cps:
            cp.start()
        for k in range(1, 8):
            pltpu.make_async_remote_copy(src_ref=v_ref, dst_ref=buf.at[idx(flip(k))], send_sem=send_sems.at[k - 1],
                                         recv_sem=recv_sems.at[k - 1], device_id=flip(k), device_id_type=MESH).wait_recv()
        for cp in cps:
            cp.wait_send()
        acc = buf[0]
        for s in range(1, 8):
            acc = acc + buf[s]
        out_ref[...] = acc

    return pl.pallas_call(
        body, name=name,
        in_specs=[pl.BlockSpec(memory_space=pltpu.VMEM)], out_specs=pl.BlockSpec(memory_space=pltpu.VMEM),
        out_shape=jax.ShapeDtypeStruct((n, C), F32),
        scratch_shapes=[pltpu.VMEM((8, n, C), F32), pltpu.SemaphoreType.DMA((7,)), pltpu.SemaphoreType.DMA((7,))],
    )(vec)


BIG = (("w_in", (D, IN_WIDTH // 4), "cols"), ("w_a", (GW, D // 4), "cols"), ("pool_w", (4, PG // 4, PG), "pool"),
       ("w_b", (D // 4, D), "rows"), ("w_c", (D // 4, D), "rows"), ("w_o", (D // 4, D), "rows"),
       ("ffn_w_up", (D, 2 * D_FF // 4), "cols"), ("ffn_w_down", (D_FF // 4, D), "rows"))
def _pack_rows(s):
    k = math.prod(s) // D
    return -(-k // 16) * 16, k


PACK_ROWS = sum(_pack_rows(s)[0] for _, s, _ in BIG)
PACK_PAD = -(-PACK_ROWS // 32) * 32


def _pad_rows(v, rows):
    pad = [(0, 0)] * v.ndim
    pad[-2] = (0, rows - v.shape[-2])
    return jnp.pad(v, pad) if rows > v.shape[-2] else v


def _pack_blocks(blocks, dtype):
    lead = blocks["w_in"].shape[:-2]
    flat = []
    for n, s, how in BIG:
        v = blocks[n].astype(dtype)
        if how == "cols":
            v = jnp.swapaxes(v, -1, -2)
        flat.append(_pad_rows(v.reshape(*lead, -1, D), _pack_rows(s)[0]))
    flat.append(jnp.zeros((*lead, PACK_PAD - PACK_ROWS, D), dtype))
    return jnp.concatenate(flat, axis=-2)


def _unpack_blocks(pack):
    out, r = {}, 0
    for n, s, how in BIG:
        rows, k = _pack_rows(s)
        v = pack[r:r + k, :]
        out[n] = v.reshape(s[1], s[0]).T if how == "cols" else v.reshape(s)
        r += rows
    return out


def _operands(allp):
    out, r = {}, 0
    for n, s, how in BIG:
        rows, k = _pack_rows(s)
        v = allp[:, r:r + k, :]
        if how == "cols":
            out[n] = v.reshape(4 * s[1], s[0])
        elif how == "rows":
            out[n] = v.reshape(4 * s[0], s[1])
        else:
            out[n] = v.reshape(4, *s).transpose(1, 0, 2, 3).reshape(4, PG, PG)
        r += rows
    return out


def _pack_operands(g, dtype):
    flat = []
    for n, s, how in BIG:
        v = g[n].astype(dtype)
        if how == "pool":
            v = v.reshape(4, 4, s[1], s[2]).transpose(1, 0, 2, 3)
        flat.append(_pad_rows(v.reshape(4, -1, D), _pack_rows(s)[0]))
    flat.append(jnp.zeros((4, PACK_PAD - PACK_ROWS, D), dtype))
    return jnp.concatenate(flat, axis=1)


def _layer_fwd(x, w, sm, bias, hk):
    pa, u = _mmf(None, w["in_a"], tb=True, pre=(_rms_core, [x], [sm["ln1_g"]]), name="in_a", tm=1024, hook=hk("in_a"))
    pb = _mm(u, w["in_b"], tb=True, out_dtype=BF16, name="in_b", hook=hk("in_b"))
    pc = _mm(u, w["in_c"], tb=True, out_dtype=BF16, name="in_c", hook=hk("in_c"))
    pd = _mm(u, w["in_d"], tb=True, out_dtype=BF16, name="in_d", hook=hk("in_d"))
    os_, ls_ = [], []
    for gi in range(3):
        o, l = _attn_fwd(pa, bias[gi], gi, "attn_fwd%d" % gi)
        os_.append(o)
        ls_.append(l)
    att = _mix_fwd(os_, ls_, "mix_fwd")
    ya = _mm(att, w["w_a"], tb=True, out_dtype=BF16, name="mm_wa")
    pool_o = _pool_fwd(pb, w["pool_w"], sm["pool_scale"], "pool_fwd")
    yb = _mm(pool_o, w["w_b"], out_dtype=BF16, name="mm_wb")
    xbc_c = _ssd_conv_fwd(pc, sm["ssd_conv_w"], sm["ssd_conv_b"], "ssd_conv_fwd")
    y_scan, states = _ssd_scan_fwd(xbc_c, pd, sm["ssd_dt_bias"], sm["ssd_a_log"], "ssd_scan_fwd")
    ssd_o = _ssd_post_fwd(y_scan, xbc_c, pc, sm["ssd_d"], sm["ssd_norm_w"], "ssd_post_fwd")
    yc = _mm(ssd_o, w["w_c"], out_dtype=BF16, name="mm_wc")
    merged = _gates_fwd(pd, sm["b_gate"], ya, yb, yc, "gates_fwd")
    x1 = _mm(merged, w["w_o"], add=x, name="mm_wo", hook=hk("mm_wo"))
    h, u2 = _mmf(None, w["ffn_w_up"], tb=True, pre=(_rms_core, [x1], [sm["ln2_g"]]), out_dtype=BF16, name="mm_up",
                 tm=1024, hook=hk("mm_up"))
    f = _ffn_act_fwd(h, sm["ffn_conv_w"], sm["ffn_conv_b"], "ffn_act_fwd")
    x2 = _mm(f, w["ffn_w_down"], add=x1, name="mm_down", hook=hk("mm_down"))
    saved = dict(x=x, u=u, pa=pa, pb=pb, pc=pc, pd=pd, os=os_, ls=ls_, att=att, ya=ya, yb=yb, yc=yc, pool_o=pool_o,
                 xbc_c=xbc_c, y_scan=y_scan, states=states, ssd_o=ssd_o, merged=merged, x1=x1, u2=u2, h=h, f=f)
    return x2, saved


def _layer_bwd(dx2, w, sm, bias, dbs, sv, hk):
    gw, gs = {}, {}
    S = dx2.shape[0]

    def gmm(a, b, name):
        return _mm(a, b, ta=True, out_dtype=BF16, name=name, hook=hk(name))

    df = _mm(dx2, w["ffn_w_down"], tb=True, out_dtype=BF16, name="d_f", hook=hk("d_f"))
    gw["ffn_w_down"] = gmm(sv["f"], dx2, "g_down")
    dha, dhv, gs["ffn_conv_w"], gs["ffn_conv_b"] = _ffn_act_bwd(sv["h"], sm["ffn_conv_w"], sm["ffn_conv_b"], df, "ffn_act_bwd")
    du2 = _mm(dha, w["up_a"], name="d_u2_a", hook=hk("d_u2_a"))
    dx1, gs["ln2_g"] = _mmf(dhv, w["up_v"], add=du2, name="d_u2_v", tm=256, hook=hk("d_u2_v"),
                            post=(_rms_post, [sv["x1"], dx2], [sm["ln2_g"]], [(D, F32)], [(1, D)]))
    gw["ffn_w_up"] = jnp.concatenate([gmm(dha, sv["u2"], "g_up_a"), gmm(dhv, sv["u2"], "g_up_v")], axis=0)
    dya, dyb, dyc, dgate, gs["b_gate"] = _mmf(
        dx1, w["w_o"], tb=True, name="d_merged", tm=256, hook=hk("d_merged"),
        post=(_gates_post, [sv["pd"], sv["ya"], sv["yb"], sv["yc"]], [sm["b_gate"]],
              [(D, BF16)] * 3 + [(3 * D, BF16)], [(1, 3 * D)]))
    gw["w_o"] = gmm(sv["merged"], dx1, "g_wo")
    dssd_o = _mm(dyc, w["w_c"], tb=True, name="d_ssd_o")
    gw["w_c"] = gmm(sv["ssd_o"], dyc, "g_wc")
    dy_scan, dxs_skip, dz, gs["ssd_d"], gs["ssd_norm_w"] = _ssd_post_bwd(
        sv["y_scan"], sv["xbc_c"], sv["pc"], sm["ssd_d"], sm["ssd_norm_w"], dssd_o, "ssd_post_bwd")
    dxbc_c, ddt, gs["ssd_dt_bias"], gs["ssd_a_log"] = _ssd_scan_bwd(
        sv["xbc_c"], sv["pd"], sm["ssd_dt_bias"], sm["ssd_a_log"], sv["states"], dy_scan, dxs_skip, "ssd_scan_bwd")
    dxbc, gs["ssd_conv_w"], gs["ssd_conv_b"] = _ssd_conv_bwd(sv["pc"], sm["ssd_conv_w"], sm["ssd_conv_b"], dxbc_c, "ssd_conv_bwd")
    dpool_o = _mm(dyb, w["w_b"], tb=True, name="d_pool_o")
    gw["w_b"] = gmm(sv["pool_o"], dyb, "g_wb")
    dpb, dpw, gs["pool_scale"] = _pool_bwd(sv["pb"], w["pool_w"], sm["pool_scale"], dpool_o, "pool_bwd")
    gw["pool_w"] = dpw.reshape(4, PG, PG)
    datt = _mm(dya, w["w_a"], name="d_att")
    gw["w_a"] = gmm(dya, sv["att"], "g_wa")
    dos, dls = _mix_bwd(sv["os"], sv["ls"], datt, "mix_bwd")
    dqkv = tuple(lax.empty((S, AW), F32) for _ in range(3))
    dbs = list(dbs)
    for gi in range(3):
        dqkv, dbs[gi] = _attn_bwd(sv["pa"], bias[gi], dos[gi], dls[gi], dbs[gi], dqkv, gi, "attn_bwd%d" % gi)
    u = sv["u"]
    pieces = [(dqkv[0], "wq"), (dqkv[1], "wk"), (dqkv[2], "wv"), (dpb, "in_b"), (dz, "wz"), (dxbc, "wxbc"),
              (ddt, "wdt"), (dgate, "wgate")]
    du = None
    g_in = []
    for dp, key in pieces:
        if key == pieces[-1][1]:
            dx, gs["ln1_g"] = _mmf(dp, w[key], add=du, name="d_u_" + key, tm=256, hook=hk("d_u_" + key),
                                   post=(_rms_post, [sv["x"], dx1], [sm["ln1_g"]], [(D, F32)], [(1, D)]))
        else:
            du = _mm(dp, w[key], add=du, name="d_u_" + key, hook=hk("d_u_" + key))
        g = gmm(dp, u, "g_in_" + key)
        g_in.append(g[:SSD_HEADS] if key == "wdt" else g)
    gw["w_in"] = jnp.concatenate(g_in, axis=0)
    return dx, gw, gs, dbs


SMALL_LAYER = ("ln1_g", "b_gate", "pool_scale", "ssd_conv_w", "ssd_conv_b", "ssd_dt_bias", "ssd_a_log", "ssd_d",
               "ssd_norm_w", "ln2_g", "ffn_conv_w", "ffn_conv_b")


def _pad_lanes(v):
    return jnp.pad(v, (0, LANES - v.shape[0])).reshape(1, LANES)


def _layer_weights(ops):
    wt = ops["w_in"]
    o1, o2, o3 = SEC_A, SEC_A + SEC_B, SEC_A + SEC_B + SEC_C
    w = dict(ops)
    w["in_a"] = jnp.pad(wt[:o1], ((0, SEC_A_PAD - o1), (0, 0)))
    w["in_b"] = wt[o1:o2]
    w["in_c"] = wt[o2:o3]
    w["in_d"] = jnp.pad(wt[o3:], ((0, SEC_D - (IN_WIDTH - o3)), (0, 0)))
    w["wq"], w["wk"], w["wv"] = wt[:AW], wt[AW:2 * AW], wt[2 * AW:o1]
    w["wz"], w["wxbc"] = wt[o2:o2 + D], wt[o2 + D:o3]
    w["wdt"] = jnp.pad(wt[o3:o3 + SSD_HEADS], ((0, LANES - SSD_HEADS), (0, 0)))
    w["wgate"] = wt[o3 + SSD_HEADS:]
    w["up_a"], w["up_v"] = ops["ffn_w_up"][:D_FF], ops["ffn_w_up"][D_FF:]
    return w


def _layer_small(p, i):
    sm = {n: p[n][i] for n in SMALL_LAYER}
    out = {}
    for n, v in sm.items():
        if n in ("ssd_dt_bias", "ssd_a_log", "ssd_d"):
            out[n] = _pad_lanes(v)
        elif v.ndim == 1:
            out[n] = v.reshape(1, -1)
        else:
            out[n] = v
    return out


def _local_step(x, target, rel_bias, final_g, layer_full, small, fwd_hooks=None, bwd_hooks=None, after_bwd=None):
    nl = small["ln1_g"].shape[0]
    buckets = [_buckets(d).astype(jnp.int32) for d in DILATIONS]
    bias = [_bias_table(rel_bias, buckets[gi], gi, "bias_table%d" % gi) for gi in range(3)]
    no_hooks = lambda i: (lambda name: None)
    fwd_hooks = fwd_hooks or no_hooks
    bwd_hooks = bwd_hooks or no_hooks
    saved, ws, sms = [], [], []
    h = x
    for i in range(nl):
        w = _layer_weights(layer_full(i))
        sm = _layer_small(small, i)
        h, sv = _layer_fwd(h, w, sm, bias, fwd_hooks(i))
        saved.append(sv)
        ws.append(w)
        sms.append(sm)
    dh, dfinal, loss = _final_loss(h, target, final_g.reshape(1, D))
    gws, gss = [None] * nl, [None] * nl
    dbs = [jnp.zeros((6, WIN, 2 * WIN), F32)] * 3
    for i in reversed(range(nl)):
        dh, gws[i], gss[i], dbs = _layer_bwd(dh, ws[i], sms[i], bias, dbs, saved[i], bwd_hooks(i))
        if after_bwd is not None:
            after_bwd(i, gws[i])
    drel = []
    for gi in range(3):
        onehot = jnp.pad(jax.nn.one_hot(buckets[gi].reshape(-1), REL_BUCKETS, dtype=BF16), ((0, 0), (0, LANES - REL_BUCKETS)))
        drel.append(_mm(dbs[gi].reshape(6, WIN * 2 * WIN), onehot, name="g_relb"))
    return loss, dh, gws, gss, dfinal, jnp.concatenate(drel, axis=0)


WEIGHTS = ("rel_bias", "ln1_g", "w_in", "b_gate", "w_a", "pool_w", "pool_scale", "w_b", "ssd_conv_w", "ssd_conv_b",
           "ssd_dt_bias", "ssd_a_log", "ssd_d", "ssd_norm_w", "w_c", "w_o", "ln2_g", "ffn_w_up", "ffn_conv_w",
           "ffn_conv_b", "ffn_w_down", "final_g")
BIG_NAMES = tuple(n for n, _, _ in BIG)
SHARDED_SMALL = {"ssd_conv_w": XBC // 4, "ffn_conv_w": 2 * D_FF // 4}


def _to_rows(flat):
    n = flat.shape[0]
    rows = -(-n // LANES)
    rows = -(-rows // 8) * 8
    return jnp.pad(flat, (0, rows * LANES - n)).reshape(rows, LANES)


def _flatten(tree, names):
    return jnp.concatenate([tree[n].reshape(-1) for n in names])


def _unflatten(flat, shapes, names):
    out, o = {}, 0
    for n in names:
        k = math.prod(shapes[n])
        out[n] = flat[o:o + k].reshape(shapes[n])
        o += k
    return out


def kernel(x, rel_bias, ln1_g, w_in, b_gate, w_a, pool_w, pool_scale, w_b, ssd_conv_w, ssd_conv_b, ssd_dt_bias, ssd_a_log, ssd_d, ssd_norm_w, w_c, w_o, ln2_g, ffn_w_up, ffn_conv_w, ffn_conv_b, ffn_w_down, final_g, loss_target, m_rel_bias, m_ln1_g, m_w_in, m_b_gate, m_w_a, m_pool_w, m_pool_scale, m_w_b, m_ssd_conv_w, m_ssd_conv_b, m_ssd_dt_bias, m_ssd_a_log, m_ssd_d, m_ssd_norm_w, m_w_c, m_w_o, m_ln2_g, m_ffn_w_up, m_ffn_conv_w, m_ffn_conv_b, m_ffn_w_down, m_final_g, v_rel_bias, v_ln1_g, v_w_in, v_b_gate, v_w_a, v_pool_w, v_pool_scale, v_w_b, v_ssd_conv_w, v_ssd_conv_b, v_ssd_dt_bias, v_ssd_a_log, v_ssd_d, v_ssd_norm_w, v_w_c, v_w_o, v_ln2_g, v_ffn_w_up, v_ffn_conv_w, v_ffn_conv_b, v_ffn_w_down, v_final_g):
    W = dict(rel_bias=rel_bias, ln1_g=ln1_g, w_in=w_in, b_gate=b_gate, w_a=w_a, pool_w=pool_w, pool_scale=pool_scale,
             w_b=w_b, ssd_conv_w=ssd_conv_w, ssd_conv_b=ssd_conv_b, ssd_dt_bias=ssd_dt_bias, ssd_a_log=ssd_a_log,
             ssd_d=ssd_d, ssd_norm_w=ssd_norm_w, w_c=w_c, w_o=w_o, ln2_g=ln2_g, ffn_w_up=ffn_w_up,
             ffn_conv_w=ffn_conv_w, ffn_conv_b=ffn_conv_b, ffn_w_down=ffn_w_down, final_g=final_g)
    M = dict(rel_bias=m_rel_bias, ln1_g=m_ln1_g, w_in=m_w_in, b_gate=m_b_gate, w_a=m_w_a, pool_w=m_pool_w,
             pool_scale=m_pool_scale, w_b=m_w_b, ssd_conv_w=m_ssd_conv_w, ssd_conv_b=m_ssd_conv_b,
             ssd_dt_bias=m_ssd_dt_bias, ssd_a_log=m_ssd_a_log, ssd_d=m_ssd_d, ssd_norm_w=m_ssd_norm_w, w_c=m_w_c,
             w_o=m_w_o, ln2_g=m_ln2_g, ffn_w_up=m_ffn_w_up, ffn_conv_w=m_ffn_conv_w, ffn_conv_b=m_ffn_conv_b,
             ffn_w_down=m_ffn_w_down, final_g=m_final_g)
    V = dict(rel_bias=v_rel_bias, ln1_g=v_ln1_g, w_in=v_w_in, b_gate=v_b_gate, w_a=v_w_a, pool_w=v_pool_w,
             pool_scale=v_pool_scale, w_b=v_w_b, ssd_conv_w=v_ssd_conv_w, ssd_conv_b=v_ssd_conv_b,
             ssd_dt_bias=v_ssd_dt_bias, ssd_a_log=v_ssd_a_log, ssd_d=v_ssd_d, ssd_norm_w=v_ssd_norm_w, w_c=v_w_c,
             w_o=v_w_o, ln2_g=v_ln2_g, ffn_w_up=v_ffn_w_up, ffn_conv_w=v_ffn_conv_w, ffn_conv_b=v_ffn_conv_b,
             ffn_w_down=v_ffn_w_down, final_g=v_final_g)
    nl = ln1_g.shape[0]
    px, py, pc_ = _position()
    chip = 2 * px + py
    cidx = jnp.reshape(pc_, (1,)).astype(jnp.int32)
    chip_idx = jnp.reshape(chip, (1,)).astype(jnp.int32)

    placed = {}
    for n, cs in SHARDED_SMALL.items():
        full = jnp.zeros(W[n].shape[:-1] + (4 * cs,), F32)
        full = lax.dynamic_update_slice(full, W[n], (0, 0, chip * cs))
        placed[n] = jnp.where(pc_ == 0, full, 0.0)
    names_sh = tuple(SHARDED_SMALL)
    shapes_sh = {n: placed[n].shape for n in names_sh}
    got = _all_reduce_small(_to_rows(_flatten(placed, names_sh)), "gather_small")
    small = {n: W[n] for n in SMALL_LAYER}
    small.update(_unflatten(got.reshape(-1), shapes_sh, names_sh))

    packs = _pack_blocks({n: W[n] for n in BIG_NAMES}, BF16)

    half = PACK_PAD // 2
    units = half // 16

    def share(weights, total):
        tot = sum(weights.values())
        return {n: math.ceil(total * v / tot) for n, v in weights.items()}

    gathers = {}

    def gather(i):
        if i not in gathers:
            buf = lax.dynamic_update_slice(lax.empty((4, PACK_PAD, D), BF16), packs[i][None], (chip, 0, 0))
            gathers[i] = _Stream(packs[i], buf, functools.partial(_gather_parts, half), 6, units, "gather_w")
        return gathers[i]

    def layer_full(i):
        return _operands(gather(i).drain())

    fwd_share = share(dict(in_a=89, in_b=26, in_c=57, in_d=66, mm_wo=28, mm_up=120, mm_down=46), units)

    def fwd_hooks(i):
        if i + 1 >= nl:
            return lambda name: None
        return lambda name: gather(i + 1).hook(fwd_share[name]) if name in fwd_share else None

    exchanges = {}
    bwd_share = share(dict(g_down=67, d_u2_a=42, d_u2_v=45, g_up_a=52, g_up_v=52, d_merged=29, g_wo=19,
                           d_u_wgate=48, g_in_wgate=41), units)

    class Exchange:
        def __init__(self, g):
            self.g = g
            self.pair = _Stream(g, lax.empty((4, half, D), BF16), functools.partial(_rs_pair_parts, half), 1, units, "rs_pair")
            self.hsum = self.chips = None

        def to_chips(self):
            if self.chips is None:
                self.hsum = _rs_add_pair(self.g, self.pair.drain(), cidx, "rs_add_pair")
                self.chips = _Stream(self.hsum, lax.empty((3, half, D), BF16), _rs_chip_parts, 3, units, "rs_chips")
            return self.chips

    def after_bwd(i, gw):
        exchanges[i] = Exchange(_pack_operands(gw, BF16))

    def bwd_hooks(i):
        if i + 1 >= nl:
            return lambda name: None

        def hk(name):
            if name == "d_f":
                return exchanges[i + 1].pair.hook(units)
            return exchanges[i + 1].to_chips().hook(bwd_share[name]) if name in bwd_share else None

        return hk

    loss, dx, gws, gss, dfinal, drel = _local_step(x[0], loss_target[0], rel_bias, final_g, layer_full, small,
                                                   fwd_hooks, bwd_hooks, after_bwd)

    grads = {}
    red = []
    for i in range(nl):
        recv3 = exchanges[i].to_chips().drain()
        r = _rs_add_chips(exchanges[i].hsum, recv3, chip_idx, "rs_add_chips")
        other = _rs_swap(r, "rs_swap")
        both = jnp.concatenate([jnp.where(pc_ == 0, r, other), jnp.where(pc_ == 0, other, r)], axis=0)
        red.append(_unpack_blocks(both))
    for n in BIG_NAMES:
        grads[n] = jnp.stack([red[i][n] for i in range(nl)], axis=0)

    sg = {}
    for n in SMALL_LAYER:
        sg[n] = jnp.stack([gss[i][n] for i in range(nl)], axis=0)
    for n in ("ssd_dt_bias", "ssd_a_log", "ssd_d"):
        sg[n] = sg[n][:, 0, :SSD_HEADS]
    sg["rel_bias"] = drel[:, :REL_BUCKETS].T
    sg["final_g"] = dfinal.reshape(D)
    sg["loss"] = loss[0, :1]
    names_sg = tuple(sg)
    shapes_sg = {n: ((nl,) + W[n].shape[1:] if n in SMALL_LAYER and n not in SHARDED_SMALL else
                     (placed[n].shape if n in SHARDED_SMALL else sg[n].shape)) for n in names_sg}
    for n in names_sg:
        sg[n] = sg[n].reshape(shapes_sg[n])
    tot = _all_reduce_small(_to_rows(_flatten(sg, names_sg)), "allreduce_small")
    tot = _unflatten(tot.reshape(-1), shapes_sg, names_sg)
    loss_out = tot.pop("loss").reshape(())
    for n, cs in SHARDED_SMALL.items():
        tot[n] = lax.dynamic_slice(tot[n], (0, 0, chip * cs), tot[n].shape[:-1] + (cs,))
    grads.update(tot)

    delta, new_m, new_v = {}, {}, {}
    for n in BIG_NAMES:
        shp = W[n].shape
        r2 = lambda a: a.reshape(-1, shp[-1])
        dl, m2, v2 = _adamw(r2(W[n]), r2(grads[n]), r2(M[n]), r2(V[n]), "adamw_" + n)
        delta[n], new_m[n], new_v[n] = dl.reshape(shp), m2.reshape(shp), v2.reshape(shp)
    names_s = tuple(n for n in WEIGHTS if n not in BIG_NAMES)
    shapes_s = {n: W[n].shape for n in names_s}
    pk = lambda t: _to_rows(_flatten(t, names_s))
    dl, m2, v2 = _adamw(pk(W), pk(grads), pk(M), pk(V), "adamw_small")
    delta.update(_unflatten(dl.reshape(-1), shapes_s, names_s))
    new_m.update(_unflatten(m2.reshape(-1), shapes_s, names_s))
    new_v.update(_unflatten(v2.reshape(-1), shapes_s, names_s))

    return (loss_out, dx[None], *[grads[n] for n in WEIGHTS], *[delta[n] for n in WEIGHTS],
            *[new_m[n] for n in WEIGHTS], *[new_v[n] for n in WEIGHTS])
```

```python
import functools
import math

import jax
import jax.numpy as jnp
from jax import lax
from jax.experimental import pallas as pl
from jax.experimental.pallas import tpu as pltpu

F32 = jnp.float32
BF16 = jnp.bfloat16
MESH = pl.DeviceIdType.MESH

D = 1024
HD = 64
GW = 384
AW = 3 * GW
WIN = 128
DILATIONS = (1, 4, 16)
REL_BUCKETS = 32
REL_MAX_DISTANCE = 2048
POOL_WINDOWS = (2, 4, 8, 16)
PG = 256
SSD_HEADS = 16
SSD_N = 128
SSD_CHUNK = 128
XBC = 1536
D_FF = 2816
EPS = 1e-6
NEG = -1e30
HALO = 16
LANES = 128

SEC_A = 3 * AW
SEC_B = D
SEC_C = D + XBC
SEC_D = 3328
SEC_A_PAD = 3584
IN_WIDTH = SEC_A + SEC_B + SEC_C + 16 + 3 * D

ADAM_LR = 0.001
ADAM_B1 = 0.9
ADAM_B2 = 0.999
ADAM_EPS = 1e-08
ADAM_WD = 0.01
ADAM_STEP = 10
ADAM_TILE = 256 * 1024
MM_VMEM_BYTES = 40 * 1024 * 1024
MM_MAX_OUT_TILE = 1024 * 1024
HBM_BYTES_PER_US = 2.0e6
STEP_US = 0.35
MXU_WIDTH = 256
MXU_FLOPS_PER_US = 0.65e6


_ANY = pl.BlockSpec(memory_space=pl.ANY)


def _pick(d, cands):
    for t in cands:
        if d % t == 0:
            return t
    return d


def _iota(shape, dim):
    return lax.broadcasted_iota(jnp.int32, shape, dim)


def _dg(a, b, ca, cb):
    return lax.dot_general(a.astype(BF16), b.astype(BF16), (((ca,), (cb,)), ((), ())),
                           preferred_element_type=F32)


@jax.custom_vjp
def _bdot_nn(a, b):
    return _dg(a, b, 1, 0)


def _nn_fwd(a, b):
    return _dg(a, b, 1, 0), (a, b)


def _nn_bwd(res, g):
    a, b = res
    return _dg(g, b, 1, 1), _dg(a, g, 0, 0)


_bdot_nn.defvjp(_nn_fwd, _nn_bwd)


@jax.custom_vjp
def _bdot_nt(a, b):
    return _dg(a, b, 1, 1)


def _nt_fwd(a, b):
    return _dg(a, b, 1, 1), (a, b)


def _nt_bwd(res, g):
    a, b = res
    return _dg(g, b, 1, 0), _dg(g, a, 0, 0)


_bdot_nt.defvjp(_nt_fwd, _nt_bwd)


@jax.custom_vjp
def _bdot_tn(a, b):
    return _dg(a, b, 0, 0)


def _tn_fwd(a, b):
    return _dg(a, b, 0, 0), (a, b)


def _tn_bwd(res, g):
    a, b = res
    return _dg(b, g, 1, 1), _dg(a, g, 1, 0)


_bdot_tn.defvjp(_tn_fwd, _tn_bwd)


def _fdot(a, b):
    return jnp.dot(a, b, preferred_element_type=F32, precision=lax.Precision.HIGHEST)


def _sigmoid(x):
    return 0.5 * jnp.tanh(0.5 * x) + 0.5


def _silu(x):
    return x * _sigmoid(x)


def _softplus(x):
    return jnp.maximum(x, 0.0) + jnp.log(1.0 + jnp.exp(-jnp.abs(x)))


def _lane_pick(m, h):
    return jnp.sum(jnp.where(_iota(m.shape, 1) == h, m, 0.0), axis=1, keepdims=True)


def _row_pick(m, h):
    return jnp.sum(jnp.where(_iota(m.shape, 0) == h, m, 0.0), axis=0, keepdims=True)


def _stack_rows(rows, n):
    c = rows[0].shape[1]
    r = _iota((n, c), 0)
    out = jnp.zeros((n, c), F32)
    for k, v in enumerate(rows):
        out = out + jnp.where(r == k, v, 0.0)
    return out


def _mm(a, b, *, ta=False, tb=False, add=None, out_dtype=F32, name, hook=None):
    if ta:
        K, M = a.shape
    else:
        M, K = a.shape
    if tb:
        N, Kb = b.shape
    else:
        Kb, N = b.shape
    assert K == Kb, (a.shape, b.shape, ta, tb)
    tm, tn, tk = _mm_tiles(M, N, K, a.dtype.itemsize, b.dtype.itemsize, jnp.dtype(out_dtype).itemsize,
                           0 if add is None else add.dtype.itemsize)
    ni, nj, nk = M // tm, N // tn, K // tk
    ca = 0 if ta else 1
    cb = 1 if tb else 0
    n_in = 2 if add is None else 3
    n_hin = 0 if hook is None else len(hook.inputs)
    n_hout = 0 if hook is None else len(hook.out_shapes)

    def body(*refs):
        a_ref, b_ref = refs[:2]
        add_ref = None if add is None else refs[2]
        o_ref = refs[n_in + n_hin]
        scr = refs[n_in + n_hin + 1 + n_hout:]
        acc_ref = scr[0] if nk > 1 else None
        hargs = (refs[n_in:n_in + n_hin], refs[n_in + n_hin + 1:n_in + n_hin + 1 + n_hout], scr[1 if nk > 1 else 0:])
        i, j, k = pl.program_id(0), pl.program_id(1), pl.program_id(2)
        if hook is not None:
            @pl.when((i == 0) & (j == 0) & (k == 0))
            def _():
                hook.start(*hargs)

        part = _dg(a_ref[...], b_ref[...], ca, cb)

        def finish(r):
            if add_ref is not None:
                r = r + add_ref[...].astype(F32)
            o_ref[...] = r.astype(o_ref.dtype)

        if nk == 1:
            finish(part)
        else:
            @pl.when(k == 0)
            def _():
                acc_ref[...] = part

            @pl.when((k > 0) & (k < nk - 1))
            def _():
                acc_ref[...] += part

            @pl.when(k == nk - 1)
            def _():
                finish(acc_ref[...] + part)

        if hook is not None:
            @pl.when((i == ni - 1) & (j == nj - 1) & (k == nk - 1))
            def _():
                hook.finish(*hargs)

    a_spec = pl.BlockSpec((tk, tm), lambda i, j, k: (k, i)) if ta else pl.BlockSpec((tm, tk), lambda i, j, k: (i, k))
    b_spec = pl.BlockSpec((tn, tk), lambda i, j, k: (j, k)) if tb else pl.BlockSpec((tk, tn), lambda i, j, k: (k, j))
    in_specs = [a_spec, b_spec]
    args = [a, b]
    if add is not None:
        in_specs.append(pl.BlockSpec((tm, tn), lambda i, j, k: (i, j)))
        args.append(add)
    out_specs = [pl.BlockSpec((tm, tn), lambda i, j, k: (i, j))]
    out_shape = [jax.ShapeDtypeStruct((M, N), out_dtype)]
    scratch = [pltpu.VMEM((tm, tn), F32)] if nk > 1 else []
    aliases = {}
    if hook is not None:
        in_specs += [_ANY] * n_hin
        args += list(hook.inputs)
        out_specs += [_ANY] * n_hout
        out_shape += list(hook.out_shapes)
        scratch += list(hook.scratch)
        aliases = {n_in + hi: 1 + ho for hi, ho in hook.aliases.items()}
    sem = ("parallel", "parallel", "arbitrary") if hook is None else ("arbitrary",) * 3
    res = pl.pallas_call(
        body, name=name, grid=(ni, nj, nk), in_specs=in_specs, out_specs=out_specs, out_shape=out_shape,
        scratch_shapes=scratch, input_output_aliases=aliases,
        compiler_params=pltpu.CompilerParams(dimension_semantics=sem),
    )(*args)
    if hook is not None:
        hook.done(res[1:])
    return res[0]


def _wide(v):
    return v.astype(F32) if v.dtype == BF16 else v


def _mmf(a, b, *, tb=False, add=None, pre=None, post=None, out_dtype=F32, name, tm, hook=None):
    a_list = list(a) if isinstance(a, (list, tuple)) else [a]
    b_list = list(b) if isinstance(b, (list, tuple)) else [b]
    assert len(a_list) == len(b_list) and (len(b_list) == 1 or not (tb or pre))
    b = b_list[0]
    if tb:
        N, K = b.shape
    else:
        K, N = b.shape
    M = pre[1][0].shape[0] if pre else a_list[0].shape[0]
    tn = N if post else _pick(N, (512, 256, LANES))
    ni, nj = M // tm, N // tn
    cb = 1 if tb else 0
    pre_fn, pre_rows, pre_consts = pre if pre else (None, [], [])
    post_fn, post_rows, post_consts, post_outs, post_accs = post if post else (None, [], [], [], [])
    hook_in = [] if hook is None else list(hook.inputs)
    hook_out = [] if hook is None else list(hook.out_shapes)

    def row_spec(arr):
        return pl.BlockSpec((tm, arr.shape[1]), lambda i, j: (i, 0))

    def const_spec(arr):
        return pl.BlockSpec(arr.shape, lambda i, j, nd=arr.ndim: (0,) * nd)

    args, in_specs = [], []
    for arr in (a_list if not pre else pre_rows):
        args.append(arr)
        in_specs.append(row_spec(arr))
    for arr in pre_consts:
        args.append(arr)
        in_specs.append(const_spec(arr))
    for arr in b_list:
        args.append(arr)
        in_specs.append(pl.BlockSpec((tn, K), lambda i, j: (j, 0)) if tb else
                        pl.BlockSpec((arr.shape[0], tn), lambda i, j: (0, j)))
    if add is not None:
        args.append(add)
        in_specs.append(pl.BlockSpec((tm, tn), lambda i, j: (i, j)))
    for arr in post_rows:
        args.append(arr)
        in_specs.append(row_spec(arr))
    for arr in post_consts:
        args.append(arr)
        in_specs.append(const_spec(arr))
    n_main = len(args)
    args += hook_in
    in_specs += [_ANY] * len(hook_in)

    out_shape, out_specs = [], []
    if post:
        for c, dt in post_outs:
            out_shape.append(jax.ShapeDtypeStruct((M, c), dt))
            out_specs.append(pl.BlockSpec((tm, c), lambda i, j: (i, 0)))
        for r, c in post_accs:
            out_shape.append(jax.ShapeDtypeStruct((r, c), F32))
            out_specs.append(pl.BlockSpec((r, c), lambda i, j: (0, 0)))
    else:
        out_shape.append(jax.ShapeDtypeStruct((M, N), out_dtype))
        out_specs.append(pl.BlockSpec((tm, tn), lambda i, j: (i, j)))
    if pre:
        out_shape.append(jax.ShapeDtypeStruct((M, K), BF16))
        out_specs.append(pl.BlockSpec((tm, K), lambda i, j: (i, 0)))
    n_out = len(out_shape)
    out_shape += hook_out
    out_specs += [_ANY] * len(hook_out)
    scratch = ([pltpu.VMEM((tm, K), BF16)] if pre else []) + ([] if hook is None else list(hook.scratch))
    aliases = {} if hook is None else {n_main + hi: n_out + ho for hi, ho in hook.aliases.items()}

    def body(*refs):
        ins, outs, scr = refs[:n_main], refs[len(args):len(args) + n_out], refs[len(args) + len(out_shape):]
        hargs = (refs[n_main:len(args)], refs[len(args) + n_out:len(args) + len(out_shape)], scr[1 if pre else 0:])
        i, j = pl.program_id(0), pl.program_id(1)
        if hook is not None:
            @pl.when((i == 0) & (j == 0))
            def _():
                hook.start(*hargs)

        it = iter(ins)
        if pre:
            rows_ = [next(it) for _ in pre_rows]
            consts_ = [next(it) for _ in pre_consts]

            @pl.when(j == 0)
            def _():
                av = pre_fn(*[_wide(r[...]) for r in rows_], *[_wide(r[...]) for r in consts_]).astype(BF16)
                scr[0][...] = av
                outs[-1][...] = av

            ats = [scr[0][...]]
        else:
            ats = [next(it)[...] for _ in a_list]
        p = None
        for at in ats:
            part = _dg(at, next(it)[...], 1, cb)
            p = part if p is None else p + part
        if add is not None:
            p = p + next(it)[...].astype(F32)
        if post:
            rows_ = [next(it) for _ in post_rows]
            consts_ = [next(it) for _ in post_consts]
            res = post_fn(p, *[_wide(r[...]) for r in rows_], *[_wide(r[...]) for r in consts_])
            for r, v in zip(outs[:len(post_outs)], res[:len(post_outs)]):
                r[...] = v.astype(r.dtype)
            for r, v in zip(outs[len(post_outs):], res[len(post_outs):]):
                @pl.when(i == 0)
                def _(r=r, v=v):
                    r[...] = v

                @pl.when(i > 0)
                def _(r=r, v=v):
                    r[...] += v
        else:
            outs[0][...] = p.astype(outs[0].dtype)
        if hook is not None:
            @pl.when((i == ni - 1) & (j == nj - 1))
            def _():
                hook.finish(*hargs)

    res = pl.pallas_call(
        body, name=name, grid=(ni, nj), in_specs=in_specs, out_specs=out_specs, out_shape=out_shape,
        scratch_shapes=scratch, input_output_aliases=aliases,
        compiler_params=pltpu.CompilerParams(dimension_semantics=("arbitrary", "arbitrary")),
    )(*args)
    if hook is not None:
        hook.done(res[n_out:])
    return res[:n_out]


def _mm_tiles(M, N, K, sa, sb, so, sadd):
    def tiles(d):
        return [t for t in range(LANES, min(d, 2048) + 1, LANES) if d % t == 0] or [d]

    best = None
    for tk in [K] + [t for t in tiles(K) if t < K]:
        for tm in tiles(M):
            for tn in tiles(N):
                vmem = 2 * (tm * tk * sa + tk * tn * sb + tm * tn * (so + sadd)) + (tm * tn * 4 if tk < K else 0)
                if vmem > MM_VMEM_BYTES or tm * tn > MM_MAX_OUT_TILE:
                    continue
                a_reads = 1 if tk == K else N // tn
                traffic = M * K * sa * a_reads + K * N * sb * (M // tm) + M * N * (so + sadd)
                steps = (M // tm) * (N // tn) * (K // tk)
                width = -(-tn // MXU_WIDTH) * MXU_WIDTH
                mxu = 2.0 * M * K * N * (width / tn) / MXU_FLOPS_PER_US
                edge = tm * tk * sa + tk * tn * sb + tm * tn * (so + sadd)
                cost = max(traffic / HBM_BYTES_PER_US, mxu) + steps * STEP_US + edge / HBM_BYTES_PER_US
                if best is None or cost < best[0]:
                    best = (cost, tm, tn, tk)
    assert best is not None, (M, N, K)
    return best[1:]


class _Hook:
    def __init__(self, inputs, out_shapes, aliases, scratch, start, finish, done):
        self.inputs, self.out_shapes, self.aliases, self.scratch = inputs, out_shapes, aliases, scratch
        self.start, self.finish, self.done = start, finish, done


class _Ctx:
    def __init__(self, first, last, row0, rows):
        self.first, self.last, self.row0, self.rows = first, last, row0, rows


def _rows(name, fn, ins, outs, accs=(), *, tm, nrows, ncol=1, chunk=None):
    nt = nrows // tm
    hb = tm // HALO
    nh = nrows // HALO
    ch = chunk or tm
    nch = tm // ch
    ins = [(kind, arr, arr.shape[1] if kind == "row" and cw is None else cw, base) for kind, arr, cw, base in ins]

    def row_of(k):
        return next(q for q, s in enumerate(ins) if s[0] == "row" and s[1] is ins[k][1] and s[2:] == ins[k][2:])
    in_specs, args = [], []
    for kind, arr, cw, base in ins:
        if kind == "row":
            in_specs.append(pl.BlockSpec((tm, cw), lambda j, i, base=base: (i, base + j)))
        elif kind == "prev":
            in_specs.append(pl.BlockSpec((HALO, cw), lambda j, i, base=base: (jnp.maximum(i * hb - 1, 0), base + j)))
        elif kind == "next":
            in_specs.append(pl.BlockSpec((HALO, cw), lambda j, i, base=base: (jnp.minimum((i + 1) * hb, nh - 1), base + j)))
        elif kind in ("const", "raw"):
            in_specs.append(pl.BlockSpec(arr.shape, lambda j, i, nd=arr.ndim: (0,) * nd))
        elif kind == "ccol":
            in_specs.append(pl.BlockSpec((arr.shape[0], cw), lambda j, i, base=base: (0, base + j)))
        else:
            raise ValueError(kind)
        args.append(arr)
    out_specs, out_shape = [], []
    for ctot, cw, base, dt in outs:
        out_specs.append(pl.BlockSpec((tm, cw), lambda j, i, base=base: (i, base + j)))
        out_shape.append(jax.ShapeDtypeStruct((nrows, ctot), dt))
    for r, ctot, cw in accs:
        out_specs.append(pl.BlockSpec((r, cw), lambda j, i: (0, j)))
        out_shape.append(jax.ShapeDtypeStruct((r, ctot), F32))
    n_in, n_out = len(ins), len(outs)

    def body(*refs):
        i = pl.program_id(1)
        in_refs, out_refs, acc_refs = refs[:n_in], refs[n_in:n_in + n_out], refs[n_in + n_out:]
        if acc_refs:
            @pl.when(i == 0)
            def _():
                for r in acc_refs:
                    r[...] = jnp.zeros_like(r)

        whole = {k: (in_refs[k][...] if s[0] == "raw" else _wide(in_refs[k][...]))
                 for k, s in enumerate(ins) if s[0] in ("const", "ccol", "raw")}

        def do_chunk(c, carry):
            r0 = pl.multiple_of(c * ch, ch) if nch > 1 else 0
            rows_ = pl.ds(r0, ch)
            vals = []
            for k, (kind, _, _, _) in enumerate(ins):
                r = in_refs[k]
                if k in whole:
                    vals.append(whole[k])
                    continue
                if kind == "row":
                    v = r[rows_, :]
                elif nch == 1:
                    v = r[...]
                elif kind == "prev":
                    inner = in_refs[row_of(k)][pl.ds(pl.multiple_of(jnp.maximum(r0 - HALO, 0), HALO), HALO), :]
                    v = jnp.where(c == 0, r[...], inner)
                else:
                    inner = in_refs[row_of(k)][pl.ds(pl.multiple_of(jnp.minimum(r0 + ch, tm - HALO), HALO), HALO), :]
                    v = jnp.where(c == nch - 1, r[...], inner)
                vals.append(_wide(v))
            ctx = _Ctx((i == 0) & (c == 0), (i == nt - 1) & (c == nch - 1), i * tm + r0, ch)
            res = fn(ctx, *vals)
            for r, v in zip(out_refs, res[:n_out]):
                r[rows_, :] = v.astype(r.dtype)
            for r, v in zip(acc_refs, res[n_out:]):
                r[...] += v
            return carry

        if nch == 1:
            do_chunk(0, 0)
        else:
            lax.fori_loop(0, nch, do_chunk, 0)

    res = pl.pallas_call(
        body, name=name, grid=(ncol, nt), in_specs=in_specs, out_specs=out_specs, out_shape=out_shape,
        compiler_params=pltpu.CompilerParams(dimension_semantics=("arbitrary", "arbitrary")),
    )(*args)
    return res


def _shift_down(xcat, k):
    return xcat if k == 0 else pltpu.roll(xcat, k, 0)


def _shift_up(xcat, k):
    return xcat if k == 0 else pltpu.roll(xcat, xcat.shape[0] - k, 0)


def _with_prev(ctx, halo, x):
    return jnp.concatenate([jnp.where(ctx.first, 0.0, halo), x], axis=0)


def _with_next(ctx, x, halo):
    return jnp.concatenate([x, jnp.where(ctx.last, 0.0, halo)], axis=0)


def _rms_core(x, g):
    r = lax.rsqrt(jnp.mean(x * x, axis=-1, keepdims=True) + EPS)
    return x * r * g


def _rms_post(du, xv, drv, gv):
    _, vjp = jax.vjp(_rms_core, xv, gv)
    dx, dg = vjp(du)
    return [drv + dx, dg]


def _final_loss(x, target, g):
    S = x.shape[0]

    def fn(ctx, xv, tv, gv):
        def f(xx, gg):
            err = _rms_core(xx, gg) - tv
            return 0.5 * jnp.sum(err * err) / D

        loss, vjp = jax.vjp(f, xv, gv)
        dx, dg = vjp(jnp.ones((), F32))
        return [dx, dg, jnp.zeros((1, LANES), F32) + loss]

    return _rows("final_loss", fn, [("row", x, None, 0), ("row", target, None, 0), ("const", g, None, 0)],
                 [(D, D, 0, F32)], [(1, D, D), (1, LANES, LANES)], tm=256, nrows=S, chunk=CHUNK_WIDE)


def _attn_valid(n):
    qi = _iota((WIN, 2 * WIN), 0)
    kk = _iota((WIN, 2 * WIN), 1)
    rel = qi + WIN - kk
    return (rel >= 0) & (rel <= WIN) & ((kk >= WIN) | (n > 0))


def _attn_block(q, kp, kc, vp, vc, b0, b1, valid):
    k = jnp.concatenate([kp, kc], axis=0)
    v = jnp.concatenate([vp, vc], axis=0)
    lo = _iota((WIN, LANES), 1) < HD
    scale = 1.0 / math.sqrt(HD)
    os_, ls_ = [], []
    for hh, b in ((0, b0), (1, b1)):
        qm = jnp.where(lo if hh == 0 else ~lo, q, 0.0)
        s = _bdot_nt(qm, k) * scale + b
        s = jnp.where(valid, s, NEG)
        m = lax.stop_gradient(jnp.max(s, axis=1, keepdims=True))
        p = jnp.exp(s - m)
        l = jnp.sum(p, axis=1, keepdims=True)
        os_.append(_bdot_nn(p, v) / l)
        ls_.append(m + jnp.log(l))
    return jnp.where(lo, os_[0], os_[1]), jnp.where(lo, ls_[0], ls_[1])


def _residue_rows(r, d):
    return pl.ds(0, WIN) if d == 1 else pl.ds(r, WIN, stride=d)


def _for_residues(d, fn):
    if d == 1:
        fn(0, 0)
    else:
        lax.fori_loop(0, d, fn, 0, unroll=min(d, 8))


def _pairs_per_step(d):
    return 3 if d == 1 else 1


def _bias_table(rel_bias, bucket, gi, name):
    def body(t_ref, b_ref, o_ref):
        h = 6 * gi + pl.program_id(0)
        b = b_ref[...]
        acc = jnp.zeros(b.shape, F32)
        for k in range(REL_BUCKETS):
            acc = jnp.where(b == k, t_ref[k, h], acc)
        o_ref[0] = acc

    return pl.pallas_call(
        body, name=name, grid=(6,),
        in_specs=[pl.BlockSpec(memory_space=pltpu.SMEM), pl.BlockSpec((WIN, 2 * WIN), lambda h: (0, 0))],
        out_specs=pl.BlockSpec((1, WIN, 2 * WIN), lambda h: (h, 0, 0)),
        out_shape=jax.ShapeDtypeStruct((6, WIN, 2 * WIN), F32),
    )(rel_bias, bucket)


def _attn_fwd(pa, bias, gi, name):
    S = pa.shape[0]
    d = DILATIONS[gi]
    bt = WIN * d
    nb = S // bt
    hpw = _pairs_per_step(d)
    bw = hpw * LANES
    cb = 3 * gi // hpw

    def body(q_ref, kp_ref, kc_ref, vp_ref, vc_ref, b_ref, o_ref, l_ref):
        valid = _attn_valid(pl.program_id(1))

        def residue(r, carry):
            sl = _residue_rows(r, d)
            for t in range(hpw):
                ln = pl.ds(t * LANES, LANES)
                o, lse = _attn_block(q_ref[sl, ln], kp_ref[sl, ln], kc_ref[sl, ln], vp_ref[sl, ln], vc_ref[sl, ln],
                                     b_ref[2 * t], b_ref[2 * t + 1], valid)
                o_ref[sl, ln] = o
                l_ref[sl, ln] = lse
            return carry

        _for_residues(d, residue)

    def spec(off, prev):
        if prev:
            return pl.BlockSpec((bt, bw), lambda hp, n: (jnp.maximum(n - 1, 0), off // hpw + cb + hp))
        return pl.BlockSpec((bt, bw), lambda hp, n: (n, off // hpw + cb + hp))

    ospec = pl.BlockSpec((bt, bw), lambda hp, n: (n, hp))
    return pl.pallas_call(
        body, name=name, grid=(3 // hpw, nb),
        in_specs=[spec(0, False), spec(9, True), spec(9, False), spec(18, True), spec(18, False),
                  pl.BlockSpec((2 * hpw, WIN, 2 * WIN), lambda hp, n: (hp, 0, 0))],
        out_specs=[ospec, ospec],
        out_shape=[jax.ShapeDtypeStruct((S, GW), F32)] * 2,
        compiler_params=pltpu.CompilerParams(dimension_semantics=("parallel", "arbitrary")),
    )(pa, pa, pa, pa, pa, bias)


def _attn_bwd(pa, bias, do, dlse, db_in, dqkv, gi, name):
    S = pa.shape[0]
    d = DILATIONS[gi]
    bt = WIN * d
    nb = S // bt
    hpw = _pairs_per_step(d)
    bw = hpw * LANES
    cb = 3 * gi // hpw

    def body(q_ref, kp_ref, kc_ref, vp_ref, vc_ref, b_ref, do_ref, dl_ref, dbi_ref, dqi_ref, dki_ref, dvi_ref,
             dq_ref, dk_ref, dv_ref, db_ref, ck, cv):
        n = pl.program_id(1)

        @pl.when(n == 0)
        def _():
            db_ref[...] = dbi_ref[...]
            ck[...] = jnp.zeros_like(ck)
            cv[...] = jnp.zeros_like(cv)

        @pl.when(n < nb)
        def _():
            f = functools.partial(_attn_block, valid=_attn_valid(n))

            def residue(r, carry):
                sl = _residue_rows(r, d)
                cs = pl.ds(pl.multiple_of(r * WIN, WIN), WIN)
                for t in range(hpw):
                    ln = pl.ds(t * LANES, LANES)
                    _, vjp = jax.vjp(f, q_ref[sl, ln], kp_ref[sl, ln], kc_ref[sl, ln], vp_ref[sl, ln], vc_ref[sl, ln],
                                     b_ref[2 * t], b_ref[2 * t + 1])
                    dq, dkp, dkc, dvp, dvc, db0, db1 = vjp((do_ref[sl, ln], dl_ref[sl, ln]))
                    dq_ref[sl, ln] = dq
                    dk_ref[sl, ln] = ck[cs, ln] + dkp
                    dv_ref[sl, ln] = cv[cs, ln] + dvp
                    ck[cs, ln] = dkc
                    cv[cs, ln] = dvc
                    db_ref[2 * t] += db0
                    db_ref[2 * t + 1] += db1
                return carry

            _for_residues(d, residue)

        @pl.when(n == nb)
        def _():
            def residue(r, carry):
                sl = _residue_rows(r, d)
                cs = pl.ds(pl.multiple_of(r * WIN, WIN), WIN)
                dk_ref[sl, :] = ck[cs, :]
                dv_ref[sl, :] = cv[cs, :]
                return carry

            _for_residues(d, residue)

    def cur(n):
        return jnp.minimum(n, nb - 1)

    def spec(off, prev):
        if prev:
            return pl.BlockSpec((bt, bw), lambda hp, n: (jnp.maximum(cur(n) - 1, 0), off // hpw + cb + hp))
        return pl.BlockSpec((bt, bw), lambda hp, n: (cur(n), off // hpw + cb + hp))

    gspec = pl.BlockSpec((bt, bw), lambda hp, n: (cur(n), hp))
    bspec = pl.BlockSpec((2 * hpw, WIN, 2 * WIN), lambda hp, n: (hp, 0, 0))
    qspec = pl.BlockSpec((bt, bw), lambda hp, n: (cur(n), cb + hp))
    kspec = pl.BlockSpec((bt, bw), lambda hp, n: (jnp.maximum(n - 1, 0), cb + hp))
    dq, dk, dv, db = pl.pallas_call(
        body, name=name, grid=(3 // hpw, nb + 1),
        in_specs=[spec(0, False), spec(9, True), spec(9, False), spec(18, True), spec(18, False),
                  bspec, gspec, gspec, bspec, _ANY, _ANY, _ANY],
        out_specs=[qspec, kspec, kspec, bspec],
        out_shape=[jax.ShapeDtypeStruct((S, AW), F32)] * 3 + [jax.ShapeDtypeStruct((6, WIN, 2 * WIN), F32)],
        scratch_shapes=[pltpu.VMEM((bt, bw), F32), pltpu.VMEM((bt, bw), F32)],
        input_output_aliases={9: 0, 10: 1, 11: 2},
        compiler_params=pltpu.CompilerParams(dimension_semantics=("arbitrary", "arbitrary")),
    )(pa, pa, pa, pa, pa, bias, do, dlse, db_in, *dqkv)
    return (dq, dk, dv), db


def _mix_core(o0, o1, o2, l0, l1, l2):
    m = lax.stop_gradient(jnp.maximum(jnp.maximum(l0, l1), l2))
    e0, e1, e2 = jnp.exp(l0 - m), jnp.exp(l1 - m), jnp.exp(l2 - m)
    return (e0 * o0 + e1 * o1 + e2 * o2) / (e0 + e1 + e2)


def _mix_fwd(os_, ls_, name):
    S = os_[0].shape[0]
    ins = [("row", a, None, 0) for a in (*os_, *ls_)]
    return _rows(name, lambda ctx, *v: [_mix_core(*v)], ins, [(GW, GW, 0, BF16)], tm=256, nrows=S, chunk=CHUNK_NARROW)[0]


def _mix_bwd(os_, ls_, datt, name):
    S = datt.shape[0]

    def fn(ctx, *v):
        _, vjp = jax.vjp(_mix_core, *v[:6])
        return list(vjp(v[6]))

    ins = [("row", a, None, 0) for a in (*os_, *ls_, datt)]
    outs = [(GW, GW, 0, F32)] * 6
    r = _rows(name, fn, ins, outs, tm=256, nrows=S, chunk=CHUNK_NARROW)
    return r[:3], r[3:]


def _t5_bucket(dist):
    max_exact = REL_BUCKETS // 2
    is_small = dist < max_exact
    nf = jnp.maximum(dist, 1).astype(F32)
    large = max_exact + (jnp.log(nf / max_exact) / math.log(REL_MAX_DISTANCE / max_exact)
                         * (REL_BUCKETS - max_exact)).astype(jnp.int32)
    large = jnp.minimum(large, REL_BUCKETS - 1)
    return jnp.where(is_small, dist, large)


def _buckets(d):
    qi = jnp.arange(WIN)[:, None]
    kk = jnp.arange(2 * WIN)[None, :]
    rel = qi + WIN - kk
    return _t5_bucket(jnp.clip(rel, 0, None) * d)


def _pool_cnt(ctx, w):
    pos = ctx.row0 + _iota((ctx.rows, PG), 0) + 1
    return jnp.minimum(pos, w).astype(F32)


def _pool_d(ctx, halo, u):
    ds = []
    for g, w in enumerate(POOL_WINDOWS):
        ug = u[:, g * PG:(g + 1) * PG]
        s = _with_prev(ctx, halo[:, g * PG:(g + 1) * PG], ug)
        step = 1
        while step < w:
            s = s + _shift_down(s, step)
            step *= 2
        ds.append(s[HALO:] / _pool_cnt(ctx, w) - ug)
    return ds


def _pool_fwd(pb, pw, scale, name):
    S = pb.shape[0]

    def fn(ctx, halo, u, w, sc):
        ds = _pool_d(ctx, halo, u)
        return [jnp.concatenate([_dg(ds[k], w[k], 1, 0) for k in range(4)], axis=1) * sc]

    return _rows(name, fn, [("prev", pb, D, 0), ("row", pb, None, 0), ("raw", pw, None, 0), ("const", scale, None, 0)],
                 [(D, D, 0, BF16)], tm=256, nrows=S, chunk=CHUNK_POOL)[0]


def _pool_bwd(pb, pw, scale, dpo, name):
    S = pb.shape[0]

    def fn1(ctx, halo, u, w, sc, dy):
        ds = _pool_d(ctx, halo, u)
        dyp = dy * sc
        y = jnp.concatenate([_dg(ds[k], w[k], 1, 0) for k in range(4)], axis=1)
        es, dws = [], []
        for k, wd in enumerate(POOL_WINDOWS):
            cols = slice(k * PG, (k + 1) * PG)
            es.append(_dg(dyp[:, cols], w[k], 1, 1) / _pool_cnt(ctx, wd))
            dws.append(_dg(ds[k], dyp[:, cols], 0, 0))
        return [jnp.concatenate(es, axis=1), jnp.concatenate(dws, axis=0), jnp.sum(dy * y, axis=0, keepdims=True)]

    e, dpw, dsc = _rows(name + "_a", fn1,
                        [("prev", pb, D, 0), ("row", pb, None, 0), ("raw", pw, None, 0), ("const", scale, None, 0),
                         ("row", dpo, None, 0)],
                        [(D, D, 0, F32)], [(4 * PG, PG, PG), (1, D, D)], tm=256, nrows=S, chunk=CHUNK_POOL)

    def fn2(ctx, ev, halo):
        outs = []
        for g, w in enumerate(POOL_WINDOWS):
            eg = ev[:, g * PG:(g + 1) * PG]
            s = _with_next(ctx, eg, halo[:, g * PG:(g + 1) * PG])
            step = 1
            while step < w:
                s = s + _shift_up(s, step)
                step *= 2
            outs.append(s[:ctx.rows] - eg * _pool_cnt(ctx, w))
        return [jnp.concatenate(outs, axis=1)]

    du = _rows(name + "_b", fn2, [("row", e, None, 0), ("next", e, D, 0)], [(D, D, 0, BF16)], tm=256, nrows=S,
               chunk=CHUNK_POOL)[0]
    return du, dpw, dsc


def _conv_taps(ctx, halo, x, K):
    cat = _with_prev(ctx, halo, x)
    return [_shift_down(cat, K - 1 - k)[HALO:] for k in range(K)]


def _conv_pre(taps, w, b):
    acc = b
    for k, t in enumerate(taps):
        acc = acc + t * _row_pick(w, k)
    return acc


CW = 256
CWS = 512
CONV_TM = 512
CHUNK_NARROW = None
CHUNK_POOL = None
CHUNK_WIDE = None


def _ext_taps(ctx, prev, x, nxt, K):
    cat = jnp.concatenate([jnp.where(ctx.first, 0.0, prev), x, jnp.where(ctx.last, 0.0, nxt)], axis=0)
    return [_shift_down(cat, K - 1 - k)[HALO:] for k in range(K)]


def _conv_t_rows(dp, w, K, tm):
    acc = jnp.zeros((tm, dp.shape[1]), F32)
    for k in range(K):
        acc = acc + _shift_up(dp, K - 1 - k)[:tm] * _row_pick(w, k)
    return acc


def _ssd_conv_fwd(pc, w, b, name):
    S = pc.shape[0]
    base = D // CWS

    def fn(ctx, halo, x, wv, bv):
        return [_silu(_conv_pre(_conv_taps(ctx, halo, x, 4), wv, bv))]

    return _rows(name, fn, [("prev", pc, CWS, base), ("row", pc, CWS, base), ("ccol", w, CWS, 0), ("ccol", b, CWS, 0)],
                 [(XBC, CWS, 0, F32)], tm=CONV_TM, nrows=S, ncol=XBC // CWS, chunk=CHUNK_POOL)[0]


def _ssd_conv_bwd(pc, w, b, dy, name):
    S = pc.shape[0]
    base = D // CWS

    def fn(ctx, prev, x, nxt, wv, bv, dyv, dyn):
        n = ctx.rows
        taps = _ext_taps(ctx, prev, x, nxt, 4)
        pre = _conv_pre(taps, wv, bv)
        sg = _sigmoid(pre)
        dye = jnp.concatenate([dyv, jnp.where(ctx.last, 0.0, dyn)], axis=0)
        dpre = dye * sg * (1.0 + pre * (1.0 - sg))
        dw = _stack_rows([jnp.sum(dpre[:n] * t[:n], axis=0, keepdims=True) for t in taps], 4)
        return [_conv_t_rows(dpre, wv, 4, n), dw, jnp.sum(dpre[:n], axis=0, keepdims=True)]

    return _rows(name, fn,
                 [("prev", pc, CWS, base), ("row", pc, CWS, base), ("next", pc, CWS, base), ("ccol", w, CWS, 0),
                  ("ccol", b, CWS, 0), ("row", dy, CWS, 0), ("next", dy, CWS, 0)],
                 [(XBC, CWS, 0, BF16)], [(4, XBC, CWS), (1, XBC, CWS)], tm=CONV_TM, nrows=S, ncol=XBC // CWS,
                 chunk=CHUNK_POOL)


NFC = D_FF // CW


def _ffn_act_fwd(h, w, b, name):
    S = h.shape[0]

    def fn(ctx, ha, a, hv, v, wa, wv, ba, bv):
        pa = _conv_pre(_conv_taps(ctx, ha, a, 3), wa, ba)
        pv = _conv_pre(_conv_taps(ctx, hv, v, 3), wv, bv)
        return [_silu(pa) * pv]

    return _rows(name, fn,
                 [("prev", h, CW, 0), ("row", h, CW, 0), ("prev", h, CW, NFC), ("row", h, CW, NFC),
                  ("ccol", w, CW, 0), ("ccol", w, CW, NFC), ("ccol", b, CW, 0), ("ccol", b, CW, NFC)],
                 [(D_FF, CW, 0, BF16)], tm=CONV_TM, nrows=S, ncol=NFC, chunk=CHUNK_NARROW)[0]


def _ffn_act_bwd(h, w, b, df, name):
    S = h.shape[0]

    def fn(ctx, pa_, a, na, pv_, v, nv, wa, wv, ba, bv, dfv, dfn):
        n = ctx.rows
        ta = _ext_taps(ctx, pa_, a, na, 3)
        tv = _ext_taps(ctx, pv_, v, nv, 3)
        pa = _conv_pre(ta, wa, ba)
        pv = _conv_pre(tv, wv, bv)
        sg = _sigmoid(pa)
        dfe = jnp.concatenate([dfv, jnp.where(ctx.last, 0.0, dfn)], axis=0)
        dpa = dfe * pv * sg * (1.0 + pa * (1.0 - sg))
        dpv = dfe * pa * sg
        res = [_conv_t_rows(dpa, wa, 3, n), _conv_t_rows(dpv, wv, 3, n)]
        for dp, taps in ((dpa, ta), (dpv, tv)):
            res.append(_stack_rows([jnp.sum(dp[:n] * t[:n], axis=0, keepdims=True) for t in taps], 3))
        for dp in (dpa, dpv):
            res.append(jnp.sum(dp[:n], axis=0, keepdims=True))
        return res

    ins = []
    for base in (0, NFC):
        ins += [("prev", h, CW, base), ("row", h, CW, base), ("next", h, CW, base)]
    ins += [("ccol", w, CW, 0), ("ccol", w, CW, NFC), ("ccol", b, CW, 0), ("ccol", b, CW, NFC),
            ("row", df, CW, 0), ("next", df, CW, 0)]
    dha, dhv, dwa, dwv, dba, dbv = _rows(
        name, fn, ins, [(D_FF, CW, 0, BF16)] * 2, [(3, D_FF, CW)] * 2 + [(1, D_FF, CW)] * 2, tm=CONV_TM, nrows=S, ncol=NFC,
        chunk=CHUNK_NARROW)
    return dha, dhv, jnp.concatenate([dwa, dwv], axis=1), jnp.concatenate([dba, dbv], axis=1)


NSLAB = D // LANES
CPS = 2


def _ssd_chunk(xs, Bs, Cs, dtraw, dtb, alog, prev):
    lsz = SSD_CHUNK
    lane = _iota((lsz, LANES), 1)
    row = _iota((lsz, LANES), 0)
    dt = jnp.where(lane < SSD_HEADS, _softplus(dtraw + dtb), 0.0)
    a = dt * (-jnp.exp(alog))
    tril = row >= lane
    a_cs = _fdot(tril.astype(F32), a)
    a_cst = a_cs.T
    a_last = jnp.sum(a, axis=0, keepdims=True)
    lo = lane < HD
    top = row < HD
    cbs = [_bdot_nt(Cs[g], Bs[g]) for g in range(2)]
    ys, news = [], []
    for s in range(NSLAB):
        g = s // (NSLAB // 2)
        cols, lms, dts, als = [], [], [], []
        for hh in range(2):
            h = 2 * s + hh
            col = _lane_pick(a_cs, h)
            seg = col - _row_pick(a_cst, h)
            lms.append(jnp.exp(jnp.where(tril, seg, NEG)))
            cols.append(col)
            dts.append(_lane_pick(dt, h))
            als.append(_lane_pick(a_last, h))
        col_x = jnp.where(lo, cols[0], cols[1])
        al_x = jnp.where(lo, als[0], als[1])
        xc = xs[s] * jnp.where(lo, dts[0], dts[1])
        yd = jnp.where(lo, _bdot_nn(cbs[g] * lms[0], xc), _bdot_nn(cbs[g] * lms[1], xc))
        yoff = _bdot_nt(Cs[g], prev[s]) * jnp.exp(col_x)
        ys.append(yd + yoff)
        st = _bdot_tn(xc * jnp.exp(al_x - col_x), Bs[g])
        news.append(prev[s] * jnp.exp(jnp.where(top, als[0], als[1])) + st)
    return ys, news


def _ssd_scan_fwd(xbc_c, pd, dtb, alog, name):
    S = xbc_c.shape[0]
    nc = S // SSD_CHUNK
    rows_ = CPS * SSD_CHUNK

    def body(x_ref, b_ref, c_ref, dt_ref, dtb_ref, al_ref, y_ref, st_ref, state):
        c = pl.program_id(0)

        @pl.when(c == 0)
        def _():
            state[...] = jnp.zeros_like(state)

        prev = [state[s * LANES:(s + 1) * LANES, :] for s in range(NSLAB)]
        for u in range(CPS):
            rw = pl.ds(u * SSD_CHUNK, SSD_CHUNK)
            xs = [x_ref[rw, s * LANES:(s + 1) * LANES] for s in range(NSLAB)]
            Bs = [b_ref[rw, g * SSD_N:(g + 1) * SSD_N] for g in range(2)]
            Cs = [c_ref[rw, g * SSD_N:(g + 1) * SSD_N] for g in range(2)]
            for s in range(NSLAB):
                st_ref[u, s * LANES:(s + 1) * LANES, :] = prev[s]
            ys, prev = _ssd_chunk(xs, Bs, Cs, dt_ref[rw, :].astype(F32), dtb_ref[...], al_ref[...], prev)
            for s in range(NSLAB):
                y_ref[rw, s * LANES:(s + 1) * LANES] = ys[s]
        for s in range(NSLAB):
            state[s * LANES:(s + 1) * LANES, :] = prev[s]

    return pl.pallas_call(
        body, name=name, grid=(nc // CPS,),
        in_specs=[pl.BlockSpec((rows_, D), lambda c: (c, 0)),
                  pl.BlockSpec((rows_, 2 * SSD_N), lambda c: (c, D // (2 * SSD_N))),
                  pl.BlockSpec((rows_, 2 * SSD_N), lambda c: (c, D // (2 * SSD_N) + 1)),
                  pl.BlockSpec((rows_, LANES), lambda c: (c, 0)),
                  pl.BlockSpec((1, LANES), lambda c: (0, 0)), pl.BlockSpec((1, LANES), lambda c: (0, 0))],
        out_specs=[pl.BlockSpec((rows_, D), lambda c: (c, 0)), pl.BlockSpec((CPS, D, SSD_N), lambda c: (c, 0, 0))],
        out_shape=[jax.ShapeDtypeStruct((S, D), F32), jax.ShapeDtypeStruct((nc, D, SSD_N), F32)],
        scratch_shapes=[pltpu.VMEM((D, SSD_N), F32)],
        compiler_params=pltpu.CompilerParams(dimension_semantics=("arbitrary",)),
    )(xbc_c, xbc_c, xbc_c, pd, dtb, alog)


def _ssd_scan_bwd(xbc_c, pd, dtb, alog, states, dy, dxs_skip, name):
    S = xbc_c.shape[0]
    nc = S // SSD_CHUNK
    rows_ = CPS * SSD_CHUNK

    def body(x_ref, b_ref, c_ref, dt_ref, dtb_ref, al_ref, st_ref, dy_ref, sk_ref,
             dx_ref, ddt_ref, ddtb_ref, dal_ref, dstate):
        c = pl.program_id(0)

        @pl.when(c == 0)
        def _():
            dstate[...] = jnp.zeros_like(dstate)
            ddtb_ref[...] = jnp.zeros_like(ddtb_ref)
            dal_ref[...] = jnp.zeros_like(dal_ref)

        dnew = [dstate[s * LANES:(s + 1) * LANES, :] for s in range(NSLAB)]
        for u in reversed(range(CPS)):
            rw = pl.ds(u * SSD_CHUNK, SSD_CHUNK)
            xs = [x_ref[rw, s * LANES:(s + 1) * LANES] for s in range(NSLAB)]
            Bs = [b_ref[rw, g * SSD_N:(g + 1) * SSD_N] for g in range(2)]
            Cs = [c_ref[rw, g * SSD_N:(g + 1) * SSD_N] for g in range(2)]
            prev = [st_ref[u, s * LANES:(s + 1) * LANES, :] for s in range(NSLAB)]
            _, vjp = jax.vjp(_ssd_chunk, xs, Bs, Cs, dt_ref[rw, :].astype(F32), dtb_ref[...], al_ref[...], prev)
            dys = [dy_ref[rw, s * LANES:(s + 1) * LANES] for s in range(NSLAB)]
            dxs, dBs, dCs, ddt, ddtb, dal, dnew = vjp((dys, dnew))
            for s in range(NSLAB):
                dx_ref[rw, s * LANES:(s + 1) * LANES] = dxs[s] + sk_ref[rw, s * LANES:(s + 1) * LANES]
            for g in range(2):
                dx_ref[rw, D + g * SSD_N:D + (g + 1) * SSD_N] = dBs[g]
                dx_ref[rw, D + 2 * SSD_N + g * SSD_N:D + 2 * SSD_N + (g + 1) * SSD_N] = dCs[g]
            ddt_ref[rw, :] = ddt
            ddtb_ref[...] += ddtb
            dal_ref[...] += dal
        for s in range(NSLAB):
            dstate[s * LANES:(s + 1) * LANES, :] = dnew[s]

    def rv(c):
        return nc // CPS - 1 - c

    return pl.pallas_call(
        body, name=name, grid=(nc // CPS,),
        in_specs=[pl.BlockSpec((rows_, D), lambda c: (rv(c), 0)),
                  pl.BlockSpec((rows_, 2 * SSD_N), lambda c: (rv(c), D // (2 * SSD_N))),
                  pl.BlockSpec((rows_, 2 * SSD_N), lambda c: (rv(c), D // (2 * SSD_N) + 1)),
                  pl.BlockSpec((rows_, LANES), lambda c: (rv(c), 0)),
                  pl.BlockSpec((1, LANES), lambda c: (0, 0)), pl.BlockSpec((1, LANES), lambda c: (0, 0)),
                  pl.BlockSpec((CPS, D, SSD_N), lambda c: (rv(c), 0, 0)),
                  pl.BlockSpec((rows_, D), lambda c: (rv(c), 0)),
                  pl.BlockSpec((rows_, D), lambda c: (rv(c), 0))],
        out_specs=[pl.BlockSpec((rows_, XBC), lambda c: (rv(c), 0)),
                   pl.BlockSpec((rows_, LANES), lambda c: (rv(c), 0)),
                   pl.BlockSpec((1, LANES), lambda c: (0, 0)), pl.BlockSpec((1, LANES), lambda c: (0, 0))],
        out_shape=[jax.ShapeDtypeStruct((S, XBC), F32), jax.ShapeDtypeStruct((S, LANES), F32),
                   jax.ShapeDtypeStruct((1, LANES), F32), jax.ShapeDtypeStruct((1, LANES), F32)],
        scratch_shapes=[pltpu.VMEM((D, SSD_N), F32)],
        compiler_params=pltpu.CompilerParams(dimension_semantics=("arbitrary",)),
    )(xbc_c, xbc_c, xbc_c, pd, dtb, alog, states, dy, dxs_skip)


def _ssd_post_core(y, xs, z, d128, nw):
    tm = y.shape[0]
    ex = (_iota((LANES, D), 1) // HD == _iota((LANES, D), 0)).astype(F32)
    d_x = jnp.sum(_fdot(jnp.broadcast_to(d128, (8, LANES)), ex), axis=0, keepdims=True) * 0.125
    y2 = (y + d_x * xs) * _silu(z)
    lo = _iota((tm, D), 1) < D // 2
    sq = y2 * y2
    ms0 = jnp.sum(jnp.where(lo, sq, 0.0), axis=-1, keepdims=True) / (D // 2)
    ms1 = jnp.sum(jnp.where(lo, 0.0, sq), axis=-1, keepdims=True) / (D // 2)
    r = jnp.where(lo, lax.rsqrt(ms0 + EPS), lax.rsqrt(ms1 + EPS))
    return y2 * r * nw


def _ssd_post_ins(y, xbc_c, pc, d128, nw):
    return [("row", y, None, 0), ("row", xbc_c, D, 0), ("row", pc, D, 0), ("const", d128, None, 0), ("const", nw, None, 0)]


def _ssd_post_fwd(y, xbc_c, pc, d128, nw, name):
    S = y.shape[0]
    return _rows(name, lambda ctx, *v: [_ssd_post_core(*v)], _ssd_post_ins(y, xbc_c, pc, d128, nw),
                 [(D, D, 0, BF16)], tm=256, nrows=S, chunk=CHUNK_WIDE)[0]


def _ssd_post_bwd(y, xbc_c, pc, d128, nw, dout, name):
    S = y.shape[0]

    def fn(ctx, *v):
        _, vjp = jax.vjp(_ssd_post_core, *v[:5])
        return list(vjp(v[5]))

    return _rows(name, fn, _ssd_post_ins(y, xbc_c, pc, d128, nw) + [("row", dout, None, 0)],
                 [(D, D, 0, F32), (D, D, 0, F32), (D, D, 0, BF16)], [(1, LANES, LANES), (1, D, D)], tm=256, nrows=S,
                 chunk=CHUNK_WIDE)


def _gates_core(g0, g1, g2, b0, b1, b2, ya, yb, yc):
    return _sigmoid(g0 + b0) * ya + _sigmoid(g1 + b1) * yb + _sigmoid(g2 + b2) * yc


def _gate_parts(pdv, bv):
    gp = pltpu.roll(pdv, SEC_D - 16, 1)
    return [gp[:, k * D:(k + 1) * D] for k in range(3)] + [bv[:, k * D:(k + 1) * D] for k in range(3)]


def _gates_fwd(pd, bg, ya, yb, yc, name):
    S = pd.shape[0]

    def fn(ctx, pdv, bv, a, b, c):
        return [_gates_core(*_gate_parts(pdv, bv), a, b, c)]

    return _rows(name, fn, [("row", pd, None, 0), ("const", bg, None, 0), ("row", ya, None, 0), ("row", yb, None, 0),
                            ("row", yc, None, 0)], [(D, D, 0, BF16)], tm=256, nrows=S, chunk=CHUNK_WIDE)[0]


def _gates_post(dm, pdv, a, b, c, bv):
    _, vjp = jax.vjp(_gates_core, *_gate_parts(pdv, bv), a, b, c)
    g = vjp(dm)
    return [g[6], g[7], g[8], jnp.concatenate(g[0:3], axis=1), jnp.concatenate(g[3:6], axis=1)]


def _adamw(w, g, m, v, name):
    rows, C = w.shape
    tm = _pick(rows, [t for t in (512, 256, 128, 64, 32, 16, 8) if t * C <= ADAM_TILE])

    def fn(ctx, wv, gv, mv, vv):
        m2 = ADAM_B1 * mv + (1.0 - ADAM_B1) * gv
        v2 = ADAM_B2 * vv + (1.0 - ADAM_B2) * jnp.square(gv)
        m_hat = m2 / (1.0 - ADAM_B1 ** ADAM_STEP)
        v_hat = v2 / (1.0 - ADAM_B2 ** ADAM_STEP)
        delta = -ADAM_LR * (m_hat / (jnp.sqrt(v_hat) + ADAM_EPS) + ADAM_WD * wv)
        return [delta, m2, v2]

    return _rows(name, fn, [("row", a, None, 0) for a in (w, g, m, v)], [(C, C, 0, F32)] * 3, tm=tm, nrows=rows)


def _position():
    return lax.axis_index("x"), lax.axis_index("y"), lax.axis_index("c")


def _other_chips(x, y):
    return [(1 - x, y), (x, 1 - y), (1 - x, 1 - y)]


_HBM = pl.BlockSpec(memory_space=pltpu.HBM)


def _gather_parts(half, lo, n):
    def copies(p_ref, out_ref, send_sems, recv_sems):
        x, y, c = _position()
        sibling = (x, y, 1 - c)
        chips = _other_chips(x, y)

        def slab(chip, h):
            return out_ref.at[2 * chip[0] + chip[1], pl.ds(h * half + lo, n), :]

        def copy(k, src, dst, to):
            return pltpu.make_async_remote_copy(src_ref=src, dst_ref=dst, send_sem=send_sems.at[k],
                                                recv_sem=recv_sems.at[k], device_id=to, device_id_type=MESH)

        first = [copy(j, p_ref.at[pl.ds(c * half + lo, n), :], slab((x, y), c), (*chip, c)) for j, chip in enumerate(chips)]
        passed = [copy(3 + j, slab(chip, c), slab(chip, c), sibling) for j, chip in enumerate(chips)]
        from_chips = [copy(j, slab(chip, c), slab(chip, c), (x, y, c)) for j, chip in enumerate(chips)]
        from_sibling = [copy(3 + j, slab(chip, 1 - c), slab(chip, 1 - c), (x, y, c)) for j, chip in enumerate(chips)]
        return first, passed, from_chips, from_sibling

    def start(ins, outs, scr):
        for cp in copies(ins[0], outs[0], *scr)[0]:
            cp.start()

    def finish(ins, outs, scr):
        first, passed, from_chips, from_sibling = copies(ins[0], outs[0], *scr)
        for j in range(3):
            from_chips[j].wait_recv()
            passed[j].start()
        for cp in from_sibling:
            cp.wait_recv()
        for cp in first + passed:
            cp.wait_send()

    return start, finish


def _rs_chip_parts(lo, n):
    def copies(h_ref, out_ref, send_sems, recv_sems):
        x, y, c = _position()
        return [pltpu.make_async_remote_copy(src_ref=h_ref.at[2 * chip[0] + chip[1], pl.ds(lo, n), :],
                                             dst_ref=out_ref.at[j, pl.ds(lo, n), :],
                                             send_sem=send_sems.at[j], recv_sem=recv_sems.at[j],
                                             device_id=(*chip, c), device_id_type=MESH)
                for j, chip in enumerate(_other_chips(x, y))]

    def start(ins, outs, scr):
        for cp in copies(ins[0], outs[0], *scr):
            cp.start()

    def finish(ins, outs, scr):
        for cp in copies(ins[0], outs[0], *scr):
            cp.wait()

    return start, finish


class _Stream:
    def __init__(self, src, buf, parts, nsem, units, name):
        self.src, self.buf, self.parts, self.nsem, self.name = src, buf, parts, nsem, name
        self.next, self.units = 0, units

    def _scratch(self):
        return [pltpu.SemaphoreType.DMA((self.nsem,)), pltpu.SemaphoreType.DMA((self.nsem,))]

    def _take(self, units):
        units = min(units, self.units - self.next)
        lo = self.next * 16
        self.next += units
        return lo, units * 16

    def _set(self, outs):
        self.buf = outs[0]

    def hook(self, units):
        lo, n = self._take(units)
        if n == 0:
            return None
        start, finish = self.parts(lo, n)
        return _Hook([self.src, self.buf], [jax.ShapeDtypeStruct(self.buf.shape, self.buf.dtype)], {1: 0},
                     self._scratch(), start, finish, self._set)

    def drain(self):
        lo, n = self._take(self.units)
        if n:
            start, finish = self.parts(lo, n)

            def body(s_ref, b_ref, o_ref, send_sems, recv_sems):
                args = ((s_ref, b_ref), (o_ref,), (send_sems, recv_sems))
                start(*args)
                finish(*args)

            self.buf = pl.pallas_call(
                body, name=self.name, in_specs=[_ANY, _ANY], out_specs=_ANY,
                out_shape=jax.ShapeDtypeStruct(self.buf.shape, self.buf.dtype),
                scratch_shapes=self._scratch(), input_output_aliases={1: 0},
            )(self.src, self.buf)
        return self.buf


def _rs_pair_parts(half, lo, n):
    def copy(g_ref, out_ref, send_sems, recv_sems):
        x, y, c = _position()
        return pltpu.make_async_remote_copy(
            src_ref=g_ref.at[pl.ds(0, 4), pl.ds((1 - c) * half + lo, n), :], dst_ref=out_ref.at[pl.ds(0, 4), pl.ds(lo, n), :],
            send_sem=send_sems.at[0], recv_sem=recv_sems.at[0], device_id=(x, y, 1 - c), device_id_type=MESH)

    def start(ins, outs, scr):
        copy(ins[0], outs[0], *scr).start()

    def finish(ins, outs, scr):
        copy(ins[0], outs[0], *scr).wait()

    return start, finish


def _rs_swap(r, name):
    Rh, C = r.shape

    def body(r_ref, out_ref, send_sem, recv_sem):
        x, y, c = _position()
        cp = pltpu.make_async_remote_copy(src_ref=r_ref, dst_ref=out_ref, send_sem=send_sem,
                                          recv_sem=recv_sem, device_id=(x, y, 1 - c), device_id_type=MESH)
        cp.start()
        cp.wait()

    return pl.pallas_call(
        body, name=name, in_specs=[_HBM], out_specs=_HBM,
        out_shape=jax.ShapeDtypeStruct((Rh, C), r.dtype),
        scratch_shapes=[pltpu.SemaphoreType.DMA, pltpu.SemaphoreType.DMA],
    )(r)


def _rs_add_pair(g, recv, cidx, name):
    _, R, C = g.shape
    Rh = R // 2
    tm = _pick(Rh, (400, 280, 200, 160, 80, 40, 16, 8))
    nt = Rh // tm

    def body(c_ref, g_ref, r_ref, o_ref):
        o_ref[...] = (g_ref[...].astype(F32) + r_ref[...].astype(F32)).astype(o_ref.dtype)

    return pl.pallas_call(
        body, name=name,
        grid_spec=pltpu.PrefetchScalarGridSpec(
            num_scalar_prefetch=1, grid=(4, nt),
            in_specs=[pl.BlockSpec((1, tm, C), lambda k, i, cr: (k, cr[0] * nt + i, 0)),
                      pl.BlockSpec((1, tm, C), lambda k, i, cr: (k, i, 0))],
            out_specs=pl.BlockSpec((1, tm, C), lambda k, i, cr: (k, i, 0))),
        out_shape=jax.ShapeDtypeStruct((4, Rh, C), BF16),
    )(cidx, g, recv)


def _rs_add_chips(h, recv, chip_idx, name):
    _, Rh, C = h.shape
    tm = _pick(Rh, (400, 280, 200, 160, 80, 40, 16, 8))

    def body(c_ref, h_ref, r_ref, o_ref):
        acc = h_ref[0].astype(F32)
        for j in range(3):
            acc = acc + r_ref[j].astype(F32)
        o_ref[...] = acc

    return pl.pallas_call(
        body, name=name,
        grid_spec=pltpu.PrefetchScalarGridSpec(
            num_scalar_prefetch=1, grid=(Rh // tm,),
            in_specs=[pl.BlockSpec((1, tm, C), lambda i, cr: (cr[0], i, 0)), pl.BlockSpec((3, tm, C), lambda i, cr: (0, i, 0))],
            out_specs=pl.BlockSpec((tm, C), lambda i, cr: (i, 0))),
        out_shape=jax.ShapeDtypeStruct((Rh, C), F32),
    )(chip_idx, h, recv)


def _all_reduce_small(vec, name):
    n, C = vec.shape

    def body(v_ref, out_ref, buf, send_sems, recv_sems):
        x, y, c = _position()

        def flip(k):
            return ((1 - x) if k & 4 else x, (1 - y) if k & 2 else y, (1 - c) if k & 1 else c)

        def idx(p):
            return 4 * p[0] + 2 * p[1] + p[2]

        me = idx((x, y, c))
        buf[me] = v_ref[...]
        cps = [pltpu.make_async_remote_copy(src_ref=v_ref, dst_ref=buf.at[me], send_sem=send_sems.at[k - 1],
                                            recv_sem=recv_sems.at[k - 1], device_id=flip(k), device_id_type=MESH)
               for k in range(1, 8)]
        for cp in cps:
            cp.start()
        for k in range(1, 8):
            pltpu.make_async_remote_copy(src_ref=v_ref, dst_ref=buf.at[idx(flip(k))], send_sem=send_sems.at[k - 1],
                                         recv_sem=recv_sems.at[k - 1], device_id=flip(k), device_id_type=MESH).wait_recv()
        for cp in cps:
            cp.wait_send()
        acc = buf[0]
        for s in range(1, 8):
            acc = acc + buf[s]
        out_ref[...] = acc

    return pl.pallas_call(
        body, name=name,
        in_specs=[pl.BlockSpec(memory_space=pltpu.VMEM)], out_specs=pl.BlockSpec(memory_space=pltpu.VMEM),
        out_shape=jax.ShapeDtypeStruct((n, C), F32),
        scratch_shapes=[pltpu.VMEM((8, n, C), F32), pltpu.SemaphoreType.DMA((7,)), pltpu.SemaphoreType.DMA((7,))],
    )(vec)


BIG = (("w_in", (D, IN_WIDTH // 4), "cols"), ("w_a", (GW, D // 4), "cols"), ("pool_w", (4, PG // 4, PG), "pool"),
       ("w_b", (D // 4, D), "rows"), ("w_c", (D // 4, D), "rows"), ("w_o", (D // 4, D), "rows"),
       ("ffn_w_up", (D, 2 * D_FF // 4), "cols"), ("ffn_w_down", (D_FF // 4, D), "rows"))
def _pack_rows(s):
    k = math.prod(s) // D
    return -(-k // 16) * 16, k


PACK_ROWS = sum(_pack_rows(s)[0] for _, s, _ in BIG)
PACK_PAD = -(-PACK_ROWS // 32) * 32


def _pad_rows(v, rows):
    pad = [(0, 0)] * v.ndim
    pad[-2] = (0, rows - v.shape[-2])
    return jnp.pad(v, pad) if rows > v.shape[-2] else v


def _pack_blocks(blocks, dtype):
    lead = blocks["w_in"].shape[:-2]
    flat = []
    for n, s, how in BIG:
        v = blocks[n].astype(dtype)
        if how == "cols":
            v = jnp.swapaxes(v, -1, -2)
        flat.append(_pad_rows(v.reshape(*lead, -1, D), _pack_rows(s)[0]))
    flat.append(jnp.zeros((*lead, PACK_PAD - PACK_ROWS, D), dtype))
    return jnp.concatenate(flat, axis=-2)


def _unpack_blocks(pack):
    out, r = {}, 0
    for n, s, how in BIG:
        rows, k = _pack_rows(s)
        v = pack[r:r + k, :]
        out[n] = v.reshape(s[1], s[0]).T if how == "cols" else v.reshape(s)
        r += rows
    return out


def _operands(allp):
    out, r = {}, 0
    for n, s, how in BIG:
        rows, k = _pack_rows(s)
        v = allp[:, r:r + k, :]
        if how == "cols":
            out[n] = v.reshape(4 * s[1], s[0])
        elif how == "rows":
            out[n] = v.reshape(4 * s[0], s[1])
        else:
            out[n] = v.reshape(4, *s).transpose(1, 0, 2, 3).reshape(4, PG, PG)
        r += rows
    return out


def _pack_operands(g, dtype):
    flat = []
    for n, s, how in BIG:
        v = g[n].astype(dtype)
        if how == "pool":
            v = v.reshape(4, 4, s[1], s[2]).transpose(1, 0, 2, 3)
        flat.append(_pad_rows(v.reshape(4, -1, D), _pack_rows(s)[0]))
    flat.append(jnp.zeros((4, PACK_PAD - PACK_ROWS, D), dtype))
    return jnp.concatenate(flat, axis=1)


def _layer_fwd(x, w, sm, bias, hk):
    pa, u = _mmf(None, w["in_a"], tb=True, pre=(_rms_core, [x], [sm["ln1_g"]]), name="in_a", tm=1024, hook=hk("in_a"))
    pb = _mm(u, w["in_b"], tb=True, out_dtype=BF16, name="in_b", hook=hk("in_b"))
    pc = _mm(u, w["in_c"], tb=True, out_dtype=BF16, name="in_c", hook=hk("in_c"))
    pd = _mm(u, w["in_d"], tb=True, out_dtype=BF16, name="in_d", hook=hk("in_d"))
    os_, ls_ = [], []
    for gi in range(3):
        o, l = _attn_fwd(pa, bias[gi], gi, "attn_fwd%d" % gi)
        os_.append(o)
        ls_.append(l)
    att = _mix_fwd(os_, ls_, "mix_fwd")
    ya = _mm(att, w["w_a"], tb=True, out_dtype=BF16, name="mm_wa")
    pool_o = _pool_fwd(pb, w["pool_w"], sm["pool_scale"], "pool_fwd")
    yb = _mm(pool_o, w["w_b"], out_dtype=BF16, name="mm_wb")
    xbc_c = _ssd_conv_fwd(pc, sm["ssd_conv_w"], sm["ssd_conv_b"], "ssd_conv_fwd")
    y_scan, states = _ssd_scan_fwd(xbc_c, pd, sm["ssd_dt_bias"], sm["ssd_a_log"], "ssd_scan_fwd")
    ssd_o = _ssd_post_fwd(y_scan, xbc_c, pc, sm["ssd_d"], sm["ssd_norm_w"], "ssd_post_fwd")
    yc = _mm(ssd_o, w["w_c"], out_dtype=BF16, name="mm_wc")
    merged = _gates_fwd(pd, sm["b_gate"], ya, yb, yc, "gates_fwd")
    x1 = _mm(merged, w["w_o"], add=x, name="mm_wo", hook=hk("mm_wo"))
    h, u2 = _mmf(None, w["ffn_w_up"], tb=True, pre=(_rms_core, [x1], [sm["ln2_g"]]), out_dtype=BF16, name="mm_up",
                 tm=1024, hook=hk("mm_up"))
    f = _ffn_act_fwd(h, sm["ffn_conv_w"], sm["ffn_conv_b"], "ffn_act_fwd")
    x2 = _mm(f, w["ffn_w_down"], add=x1, name="mm_down", hook=hk("mm_down"))
    saved = dict(x=x, u=u, pa=pa, pb=pb, pc=pc, pd=pd, os=os_, ls=ls_, att=att, ya=ya, yb=yb, yc=yc, pool_o=pool_o,
                 xbc_c=xbc_c, y_scan=y_scan, states=states, ssd_o=ssd_o, merged=merged, x1=x1, u2=u2, h=h, f=f)
    return x2, saved


def _layer_bwd(dx2, w, sm, bias, dbs, sv, hk):
    gw, gs = {}, {}
    S = dx2.shape[0]

    def gmm(a, b, name):
        return _mm(a, b, ta=True, out_dtype=BF16, name=name, hook=hk(name))

    df = _mm(dx2, w["ffn_w_down"], tb=True, out_dtype=BF16, name="d_f", hook=hk("d_f"))
    gw["ffn_w_down"] = gmm(sv["f"], dx2, "g_down")
    dha, dhv, gs["ffn_conv_w"], gs["ffn_conv_b"] = _ffn_act_bwd(sv["h"], sm["ffn_conv_w"], sm["ffn_conv_b"], df, "ffn_act_bwd")
    dx1, gs["ln2_g"] = _mmf([dha, dhv], [w["up_a"], w["up_v"]], name="d_u2_v", tm=256, hook=hk("d_u2_v"),
                            post=(_rms_post, [sv["x1"], dx2], [sm["ln2_g"]], [(D, F32)], [(1, D)]))
    gw["ffn_w_up"] = jnp.concatenate([gmm(dha, sv["u2"], "g_up_a"), gmm(dhv, sv["u2"], "g_up_v")], axis=0)
    dya, dyb, dyc, dgate, gs["b_gate"] = _mmf(
        dx1, w["w_o"], tb=True, name="d_merged", tm=256, hook=hk("d_merged"),
        post=(_gates_post, [sv["pd"], sv["ya"], sv["yb"], sv["yc"]], [sm["b_gate"]],
              [(D, BF16)] * 3 + [(3 * D, BF16)], [(1, 3 * D)]))
    gw["w_o"] = gmm(sv["merged"], dx1, "g_wo")
    dssd_o = _mm(dyc, w["w_c"], tb=True, name="d_ssd_o")
    gw["w_c"] = gmm(sv["ssd_o"], dyc, "g_wc")
    dy_scan, dxs_skip, dz, gs["ssd_d"], gs["ssd_norm_w"] = _ssd_post_bwd(
        sv["y_scan"], sv["xbc_c"], sv["pc"], sm["ssd_d"], sm["ssd_norm_w"], dssd_o, "ssd_post_bwd")
    dxbc_c, ddt, gs["ssd_dt_bias"], gs["ssd_a_log"] = _ssd_scan_bwd(
        sv["xbc_c"], sv["pd"], sm["ssd_dt_bias"], sm["ssd_a_log"], sv["states"], dy_scan, dxs_skip, "ssd_scan_bwd")
    dxbc, gs["ssd_conv_w"], gs["ssd_conv_b"] = _ssd_conv_bwd(sv["pc"], sm["ssd_conv_w"], sm["ssd_conv_b"], dxbc_c, "ssd_conv_bwd")
    dpool_o = _mm(dyb, w["w_b"], tb=True, name="d_pool_o")
    gw["w_b"] = gmm(sv["pool_o"], dyb, "g_wb")
    dpb, dpw, gs["pool_scale"] = _pool_bwd(sv["pb"], w["pool_w"], sm["pool_scale"], dpool_o, "pool_bwd")
    gw["pool_w"] = dpw.reshape(4, PG, PG)
    datt = _mm(dya, w["w_a"], name="d_att")
    gw["w_a"] = gmm(dya, sv["att"], "g_wa")
    dos, dls = _mix_bwd(sv["os"], sv["ls"], datt, "mix_bwd")
    dqkv = tuple(lax.empty((S, AW), F32) for _ in range(3))
    dbs = list(dbs)
    for gi in range(3):
        dqkv, dbs[gi] = _attn_bwd(sv["pa"], bias[gi], dos[gi], dls[gi], dbs[gi], dqkv, gi, "attn_bwd%d" % gi)
    u = sv["u"]
    pieces = [(dqkv[0], "wq"), (dqkv[1], "wk"), (dqkv[2], "wv"), (dpb, "in_b"), (dz, "wz"), (dxbc, "wxbc"),
              (ddt, "wdt"), (dgate, "wgate")]
    du = _mmf([dp for dp, _ in pieces[:4]], [w[key] for _, key in pieces[:4]], name="d_u_a", tm=256, hook=hk("d_u_a"))[0]
    dx, gs["ln1_g"] = _mmf([dp for dp, _ in pieces[4:]], [w[key] for _, key in pieces[4:]], add=du, name="d_u_wgate",
                           tm=256, hook=hk("d_u_wgate"),
                           post=(_rms_post, [sv["x"], dx1], [sm["ln1_g"]], [(D, F32)], [(1, D)]))
    g_in = []
    for dp, key in pieces:
        g = gmm(dp, u, "g_in_" + key)
        g_in.append(g[:SSD_HEADS] if key == "wdt" else g)
    gw["w_in"] = jnp.concatenate(g_in, axis=0)
    return dx, gw, gs, dbs


SMALL_LAYER = ("ln1_g", "b_gate", "pool_scale", "ssd_conv_w", "ssd_conv_b", "ssd_dt_bias", "ssd_a_log", "ssd_d",
               "ssd_norm_w", "ln2_g", "ffn_conv_w", "ffn_conv_b")


def _pad_lanes(v):
    return jnp.pad(v, (0, LANES - v.shape[0])).reshape(1, LANES)


def _layer_weights(ops):
    wt = ops["w_in"]
    o1, o2, o3 = SEC_A, SEC_A + SEC_B, SEC_A + SEC_B + SEC_C
    w = dict(ops)
    w["in_a"] = jnp.pad(wt[:o1], ((0, SEC_A_PAD - o1), (0, 0)))
    w["in_b"] = wt[o1:o2]
    w["in_c"] = wt[o2:o3]
    w["in_d"] = jnp.pad(wt[o3:], ((0, SEC_D - (IN_WIDTH - o3)), (0, 0)))
    w["wq"], w["wk"], w["wv"] = wt[:AW], wt[AW:2 * AW], wt[2 * AW:o1]
    w["wz"], w["wxbc"] = wt[o2:o2 + D], wt[o2 + D:o3]
    w["wdt"] = jnp.pad(wt[o3:o3 + SSD_HEADS], ((0, LANES - SSD_HEADS), (0, 0)))
    w["wgate"] = wt[o3 + SSD_HEADS:]
    w["up_a"], w["up_v"] = ops["ffn_w_up"][:D_FF], ops["ffn_w_up"][D_FF:]
    return w


def _layer_small(p, i):
    sm = {n: p[n][i] for n in SMALL_LAYER}
    out = {}
    for n, v in sm.items():
        if n in ("ssd_dt_bias", "ssd_a_log", "ssd_d"):
            out[n] = _pad_lanes(v)
        elif v.ndim == 1:
            out[n] = v.reshape(1, -1)
        else:
            out[n] = v
    return out


def _local_step(x, target, rel_bias, final_g, layer_full, small, fwd_hooks=None, bwd_hooks=None, after_bwd=None):
    nl = small["ln1_g"].shape[0]
    buckets = [_buckets(d).astype(jnp.int32) for d in DILATIONS]
    bias = [_bias_table(rel_bias, buckets[gi], gi, "bias_table%d" % gi) for gi in range(3)]
    no_hooks = lambda i: (lambda name: None)
    fwd_hooks = fwd_hooks or no_hooks
    bwd_hooks = bwd_hooks or no_hooks
    saved, ws, sms = [], [], []
    h = x
    for i in range(nl):
        w = _layer_weights(layer_full(i))
        sm = _layer_small(small, i)
        h, sv = _layer_fwd(h, w, sm, bias, fwd_hooks(i))
        saved.append(sv)
        ws.append(w)
        sms.append(sm)
    dh, dfinal, loss = _final_loss(h, target, final_g.reshape(1, D))
    gws, gss = [None] * nl, [None] * nl
    dbs = [jnp.zeros((6, WIN, 2 * WIN), F32)] * 3
    for i in reversed(range(nl)):
        dh, gws[i], gss[i], dbs = _layer_bwd(dh, ws[i], sms[i], bias, dbs, saved[i], bwd_hooks(i))
        if after_bwd is not None:
            after_bwd(i, gws[i])
    drel = []
    for gi in range(3):
        onehot = jnp.pad(jax.nn.one_hot(buckets[gi].reshape(-1), REL_BUCKETS, dtype=BF16), ((0, 0), (0, LANES - REL_BUCKETS)))
        drel.append(_mm(dbs[gi].reshape(6, WIN * 2 * WIN), onehot, name="g_relb"))
    return loss, dh, gws, gss, dfinal, jnp.concatenate(drel, axis=0)


WEIGHTS = ("rel_bias", "ln1_g", "w_in", "b_gate", "w_a", "pool_w", "pool_scale", "w_b", "ssd_conv_w", "ssd_conv_b",
           "ssd_dt_bias", "ssd_a_log", "ssd_d", "ssd_norm_w", "w_c", "w_o", "ln2_g", "ffn_w_up", "ffn_conv_w",
           "ffn_conv_b", "ffn_w_down", "final_g")
BIG_NAMES = tuple(n for n, _, _ in BIG)
SHARDED_SMALL = {"ssd_conv_w": XBC // 4, "ffn_conv_w": 2 * D_FF // 4}


def _to_rows(flat):
    n = flat.shape[0]
    rows = -(-n // LANES)
    rows = -(-rows // 8) * 8
    return jnp.pad(flat, (0, rows * LANES - n)).reshape(rows, LANES)


def _flatten(tree, names):
    return jnp.concatenate([tree[n].reshape(-1) for n in names])


def _unflatten(flat, shapes, names):
    out, o = {}, 0
    for n in names:
        k = math.prod(shapes[n])
        out[n] = flat[o:o + k].reshape(shapes[n])
        o += k
    return out


def kernel(x, rel_bias, ln1_g, w_in, b_gate, w_a, pool_w, pool_scale, w_b, ssd_conv_w, ssd_conv_b, ssd_dt_bias, ssd_a_log, ssd_d, ssd_norm_w, w_c, w_o, ln2_g, ffn_w_up, ffn_conv_w, ffn_conv_b, ffn_w_down, final_g, loss_target, m_rel_bias, m_ln1_g, m_w_in, m_b_gate, m_w_a, m_pool_w, m_pool_scale, m_w_b, m_ssd_conv_w, m_ssd_conv_b, m_ssd_dt_bias, m_ssd_a_log, m_ssd_d, m_ssd_norm_w, m_w_c, m_w_o, m_ln2_g, m_ffn_w_up, m_ffn_conv_w, m_ffn_conv_b, m_ffn_w_down, m_final_g, v_rel_bias, v_ln1_g, v_w_in, v_b_gate, v_w_a, v_pool_w, v_pool_scale, v_w_b, v_ssd_conv_w, v_ssd_conv_b, v_ssd_dt_bias, v_ssd_a_log, v_ssd_d, v_ssd_norm_w, v_w_c, v_w_o, v_ln2_g, v_ffn_w_up, v_ffn_conv_w, v_ffn_conv_b, v_ffn_w_down, v_final_g):
    W = dict(rel_bias=rel_bias, ln1_g=ln1_g, w_in=w_in, b_gate=b_gate, w_a=w_a, pool_w=pool_w, pool_scale=pool_scale,
             w_b=w_b, ssd_conv_w=ssd_conv_w, ssd_conv_b=ssd_conv_b, ssd_dt_bias=ssd_dt_bias, ssd_a_log=ssd_a_log,
             ssd_d=ssd_d, ssd_norm_w=ssd_norm_w, w_c=w_c, w_o=w_o, ln2_g=ln2_g, ffn_w_up=ffn_w_up,
             ffn_conv_w=ffn_conv_w, ffn_conv_b=ffn_conv_b, ffn_w_down=ffn_w_down, final_g=final_g)
    M = dict(rel_bias=m_rel_bias, ln1_g=m_ln1_g, w_in=m_w_in, b_gate=m_b_gate, w_a=m_w_a, pool_w=m_pool_w,
             pool_scale=m_pool_scale, w_b=m_w_b, ssd_conv_w=m_ssd_conv_w, ssd_conv_b=m_ssd_conv_b,
             ssd_dt_bias=m_ssd_dt_bias, ssd_a_log=m_ssd_a_log, ssd_d=m_ssd_d, ssd_norm_w=m_ssd_norm_w, w_c=m_w_c,
             w_o=m_w_o, ln2_g=m_ln2_g, ffn_w_up=m_ffn_w_up, ffn_conv_w=m_ffn_conv_w, ffn_conv_b=m_ffn_conv_b,
             ffn_w_down=m_ffn_w_down, final_g=m_final_g)
    V = dict(rel_bias=v_rel_bias, ln1_g=v_ln1_g, w_in=v_w_in, b_gate=v_b_gate, w_a=v_w_a, pool_w=v_pool_w,
             pool_scale=v_pool_scale, w_b=v_w_b, ssd_conv_w=v_ssd_conv_w, ssd_conv_b=v_ssd_conv_b,
             ssd_dt_bias=v_ssd_dt_bias, ssd_a_log=v_ssd_a_log, ssd_d=v_ssd_d, ssd_norm_w=v_ssd_norm_w, w_c=v_w_c,
             w_o=v_w_o, ln2_g=v_ln2_g, ffn_w_up=v_ffn_w_up, ffn_conv_w=v_ffn_conv_w, ffn_conv_b=v_ffn_conv_b,
             ffn_w_down=v_ffn_w_down, final_g=v_final_g)
    nl = ln1_g.shape[0]
    px, py, pc_ = _position()
    chip = 2 * px + py
    cidx = jnp.reshape(pc_, (1,)).astype(jnp.int32)
    chip_idx = jnp.reshape(chip, (1,)).astype(jnp.int32)

    placed = {}
    for n, cs in SHARDED_SMALL.items():
        full = jnp.zeros(W[n].shape[:-1] + (4 * cs,), F32)
        full = lax.dynamic_update_slice(full, W[n], (0, 0, chip * cs))
        placed[n] = jnp.where(pc_ == 0, full, 0.0)
    names_sh = tuple(SHARDED_SMALL)
    shapes_sh = {n: placed[n].shape for n in names_sh}
    got = _all_reduce_small(_to_rows(_flatten(placed, names_sh)), "gather_small")
    small = {n: W[n] for n in SMALL_LAYER}
    small.update(_unflatten(got.reshape(-1), shapes_sh, names_sh))

    packs = _pack_blocks({n: W[n] for n in BIG_NAMES}, BF16)

    half = PACK_PAD // 2
    units = half // 16

    def share(weights, total):
        tot = sum(weights.values())
        return {n: math.ceil(total * v / tot) for n, v in weights.items()}

    gathers = {}

    def gather(i):
        if i not in gathers:
            buf = lax.dynamic_update_slice(lax.empty((4, PACK_PAD, D), BF16), packs[i][None], (chip, 0, 0))
            gathers[i] = _Stream(packs[i], buf, functools.partial(_gather_parts, half), 6, units, "gather_w")
        return gathers[i]

    def layer_full(i):
        return _operands(gather(i).drain())

    fwd_share = share(dict(in_a=89, in_b=26, in_c=57, in_d=66, mm_wo=28, mm_up=120, mm_down=46), units)

    def fwd_hooks(i):
        if i + 1 >= nl:
            return lambda name: None
        return lambda name: gather(i + 1).hook(fwd_share[name]) if name in fwd_share else None

    exchanges = {}
    bwd_share = share(dict(g_down=46, d_u2_v=80, g_up_a=40, g_up_v=40, d_merged=55, g_wo=20, d_u_a=60,
                           d_u_wgate=90, g_in_wgate=40), units)

    class Exchange:
        def __init__(self, g):
            self.g = g
            self.pair = _Stream(g, lax.empty((4, half, D), BF16), functools.partial(_rs_pair_parts, half), 1, units, "rs_pair")
            self.hsum = self.chips = None

        def to_chips(self):
            if self.chips is None:
                self.hsum = _rs_add_pair(self.g, self.pair.drain(), cidx, "rs_add_pair")
                self.chips = _Stream(self.hsum, lax.empty((3, half, D), BF16), _rs_chip_parts, 3, units, "rs_chips")
            return self.chips

    def after_bwd(i, gw):
        exchanges[i] = Exchange(_pack_operands(gw, BF16))

    def bwd_hooks(i):
        if i + 1 >= nl:
            return lambda name: None

        def hk(name):
            if name == "d_f":
                return exchanges[i + 1].pair.hook(units)
            return exchanges[i + 1].to_chips().hook(bwd_share[name]) if name in bwd_share else None

        return hk

    loss, dx, gws, gss, dfinal, drel = _local_step(x[0], loss_target[0], rel_bias, final_g, layer_full, small,
                                                   fwd_hooks, bwd_hooks, after_bwd)

    grads = {}
    red = []
    for i in range(nl):
        recv3 = exchanges[i].to_chips().drain()
        r = _rs_add_chips(exchanges[i].hsum, recv3, chip_idx, "rs_add_chips")
        other = _rs_swap(r, "rs_swap")
        both = jnp.concatenate([jnp.where(pc_ == 0, r, other), jnp.where(pc_ == 0, other, r)], axis=0)
        red.append(_unpack_blocks(both))
    for n in BIG_NAMES:
        grads[n] = jnp.stack([red[i][n] for i in range(nl)], axis=0)

    sg = {}
    for n in SMALL_LAYER:
        sg[n] = jnp.stack([gss[i][n] for i in range(nl)], axis=0)
    for n in ("ssd_dt_bias", "ssd_a_log", "ssd_d"):
        sg[n] = sg[n][:, 0, :SSD_HEADS]
    sg["rel_bias"] = drel[:, :REL_BUCKETS].T
    sg["final_g"] = dfinal.reshape(D)
    sg["loss"] = loss[0, :1]
    names_sg = tuple(sg)
    shapes_sg = {n: ((nl,) + W[n].shape[1:] if n in SMALL_LAYER and n not in SHARDED_SMALL else
                     (placed[n].shape if n in SHARDED_SMALL else sg[n].shape)) for n in names_sg}
    for n in names_sg:
        sg[n] = sg[n].reshape(shapes_sg[n])
    tot = _all_reduce_small(_to_rows(_flatten(sg, names_sg)), "allreduce_small")
    tot = _unflatten(tot.reshape(-1), shapes_sg, names_sg)
    loss_out = tot.pop("loss").reshape(())
    for n, cs in SHARDED_SMALL.items():
        tot[n] = lax.dynamic_slice(tot[n], (0, 0, chip * cs), tot[n].shape[:-1] + (cs,))
    grads.update(tot)

    delta, new_m, new_v = {}, {}, {}
    for n in BIG_NAMES:
        shp = W[n].shape
        r2 = lambda a: a.reshape(-1, shp[-1])
        dl, m2, v2 = _adamw(r2(W[n]), r2(grads[n]), r2(M[n]), r2(V[n]), "adamw_" + n)
        delta[n], new_m[n], new_v[n] = dl.reshape(shp), m2.reshape(shp), v2.reshape(shp)
    names_s = tuple(n for n in WEIGHTS if n not in BIG_NAMES)
    shapes_s = {n: W[n].shape for n in names_s}
    pk = lambda t: _to_rows(_flatten(t, names_s))
    dl, m2, v2 = _adamw(pk(W), pk(grads), pk(M), pk(V), "adamw_small")
    delta.update(_unflatten(dl.reshape(-1), shapes_s, names_s))
    new_m.update(_unflatten(m2.reshape(-1), shapes_s, names_s))
    new_v.update(_unflatten(v2.reshape(-1), shapes_s, names_s))

    return (loss_out, dx[None], *[grads[n] for n in WEIGHTS], *[delta[n] for n in WEIGHTS],
            *[new_m[n] for n in WEIGHTS], *[new_v[n] for n in WEIGHTS])
```

```python
import functools
import math

import jax
import jax.numpy as jnp
from jax import lax
from jax.experimental import pallas as pl
from jax.experimental.pallas import tpu as pltpu

F32 = jnp.float32
BF16 = jnp.bfloat16
MESH = pl.DeviceIdType.MESH

D = 1024
HD = 64
GW = 384
AW = 3 * GW
WIN = 128
DILATIONS = (1, 4, 16)
REL_BUCKETS = 32
REL_MAX_DISTANCE = 2048
POOL_WINDOWS = (2, 4, 8, 16)
PG = 256
SSD_HEADS = 16
SSD_N = 128
SSD_CHUNK = 128
XBC = 1536
D_FF = 2816
EPS = 1e-6
NEG = -1e30
HALO = 16
LANES = 128

SEC_A = 3 * AW
SEC_B = D
SEC_C = D + XBC
SEC_D = 3328
SEC_A_PAD = 3584
IN_WIDTH = SEC_A + SEC_B + SEC_C + 16 + 3 * D

ADAM_LR = 0.001
ADAM_B1 = 0.9
ADAM_B2 = 0.999
ADAM_EPS = 1e-08
ADAM_WD = 0.01
ADAM_STEP = 10
ADAM_TILE = 256 * 1024
MM_VMEM_BYTES = 40 * 1024 * 1024
MM_MAX_OUT_TILE = 1024 * 1024
HBM_BYTES_PER_US = 2.0e6
STEP_US = 0.35
MXU_WIDTH = 256
MXU_FLOPS_PER_US = 0.65e6


_ANY = pl.BlockSpec(memory_space=pl.ANY)


def _pick(d, cands):
    for t in cands:
        if d % t == 0:
            return t
    return d


def _iota(shape, dim):
    return lax.broadcasted_iota(jnp.int32, shape, dim)


def _dg(a, b, ca, cb):
    return lax.dot_general(a.astype(BF16), b.astype(BF16), (((ca,), (cb,)), ((), ())),
                           preferred_element_type=F32)


@jax.custom_vjp
def _bdot_nn(a, b):
    return _dg(a, b, 1, 0)


def _nn_fwd(a, b):
    return _dg(a, b, 1, 0), (a, b)


def _nn_bwd(res, g):
    a, b = res
    return _dg(g, b, 1, 1), _dg(a, g, 0, 0)


_bdot_nn.defvjp(_nn_fwd, _nn_bwd)


@jax.custom_vjp
def _bdot_nt(a, b):
    return _dg(a, b, 1, 1)


def _nt_fwd(a, b):
    return _dg(a, b, 1, 1), (a, b)


def _nt_bwd(res, g):
    a, b = res
    return _dg(g, b, 1, 0), _dg(g, a, 0, 0)


_bdot_nt.defvjp(_nt_fwd, _nt_bwd)


@jax.custom_vjp
def _bdot_tn(a, b):
    return _dg(a, b, 0, 0)


def _tn_fwd(a, b):
    return _dg(a, b, 0, 0), (a, b)


def _tn_bwd(res, g):
    a, b = res
    return _dg(b, g, 1, 1), _dg(a, g, 1, 0)


_bdot_tn.defvjp(_tn_fwd, _tn_bwd)


def _fdot(a, b):
    return jnp.dot(a, b, preferred_element_type=F32, precision=lax.Precision.HIGHEST)


def _sigmoid(x):
    return 0.5 * jnp.tanh(0.5 * x) + 0.5


def _silu(x):
    return x * _sigmoid(x)


def _softplus(x):
    return jnp.maximum(x, 0.0) + jnp.log(1.0 + jnp.exp(-jnp.abs(x)))


def _lane_pick(m, h):
    return jnp.sum(jnp.where(_iota(m.shape, 1) == h, m, 0.0), axis=1, keepdims=True)


def _row_pick(m, h):
    return jnp.sum(jnp.where(_iota(m.shape, 0) == h, m, 0.0), axis=0, keepdims=True)


def _stack_rows(rows, n):
    c = rows[0].shape[1]
    r = _iota((n, c), 0)
    out = jnp.zeros((n, c), F32)
    for k, v in enumerate(rows):
        out = out + jnp.where(r == k, v, 0.0)
    return out


def _mm(a, b, *, ta=False, tb=False, add=None, out_dtype=F32, name, hook=None):
    if ta:
        K, M = a.shape
    else:
        M, K = a.shape
    if tb:
        N, Kb = b.shape
    else:
        Kb, N = b.shape
    assert K == Kb, (a.shape, b.shape, ta, tb)
    tm, tn, tk = _mm_tiles(M, N, K, a.dtype.itemsize, b.dtype.itemsize, jnp.dtype(out_dtype).itemsize,
                           0 if add is None else add.dtype.itemsize)
    ni, nj, nk = M // tm, N // tn, K // tk
    ca = 0 if ta else 1
    cb = 1 if tb else 0
    n_in = 2 if add is None else 3
    n_hin = 0 if hook is None else len(hook.inputs)
    n_hout = 0 if hook is None else len(hook.out_shapes)

    def body(*refs):
        a_ref, b_ref = refs[:2]
        add_ref = None if add is None else refs[2]
        o_ref = refs[n_in + n_hin]
        scr = refs[n_in + n_hin + 1 + n_hout:]
        acc_ref = scr[0] if nk > 1 else None
        hargs = (refs[n_in:n_in + n_hin], refs[n_in + n_hin + 1:n_in + n_hin + 1 + n_hout], scr[1 if nk > 1 else 0:])
        i, j, k = pl.program_id(0), pl.program_id(1), pl.program_id(2)
        if hook is not None:
            @pl.when((i == 0) & (j == 0) & (k == 0))
            def _():
                hook.start(*hargs)

        part = _dg(a_ref[...], b_ref[...], ca, cb)

        def finish(r):
            if add_ref is not None:
                r = r + add_ref[...].astype(F32)
            o_ref[...] = r.astype(o_ref.dtype)

        if nk == 1:
            finish(part)
        else:
            @pl.when(k == 0)
            def _():
                acc_ref[...] = part

            @pl.when((k > 0) & (k < nk - 1))
            def _():
                acc_ref[...] += part

            @pl.when(k == nk - 1)
            def _():
                finish(acc_ref[...] + part)

        if hook is not None:
            @pl.when((i == ni - 1) & (j == nj - 1) & (k == nk - 1))
            def _():
                hook.finish(*hargs)

    a_spec = pl.BlockSpec((tk, tm), lambda i, j, k: (k, i)) if ta else pl.BlockSpec((tm, tk), lambda i, j, k: (i, k))
    b_spec = pl.BlockSpec((tn, tk), lambda i, j, k: (j, k)) if tb else pl.BlockSpec((tk, tn), lambda i, j, k: (k, j))
    in_specs = [a_spec, b_spec]
    args = [a, b]
    if add is not None:
        in_specs.append(pl.BlockSpec((tm, tn), lambda i, j, k: (i, j)))
        args.append(add)
    out_specs = [pl.BlockSpec((tm, tn), lambda i, j, k: (i, j))]
    out_shape = [jax.ShapeDtypeStruct((M, N), out_dtype)]
    scratch = [pltpu.VMEM((tm, tn), F32)] if nk > 1 else []
    aliases = {}
    if hook is not None:
        in_specs += [_ANY] * n_hin
        args += list(hook.inputs)
        out_specs += [_ANY] * n_hout
        out_shape += list(hook.out_shapes)
        scratch += list(hook.scratch)
        aliases = {n_in + hi: 1 + ho for hi, ho in hook.aliases.items()}
    sem = ("parallel", "parallel", "arbitrary") if hook is None else ("arbitrary",) * 3
    res = pl.pallas_call(
        body, name=name, grid=(ni, nj, nk), in_specs=in_specs, out_specs=out_specs, out_shape=out_shape,
        scratch_shapes=scratch, input_output_aliases=aliases,
        compiler_params=pltpu.CompilerParams(dimension_semantics=sem),
    )(*args)
    if hook is not None:
        hook.done(res[1:])
    return res[0]


def _wide(v):
    return v.astype(F32) if v.dtype == BF16 else v


def _mmf(a, b, *, tb=False, add=None, pre=None, post=None, out_dtype=F32, name, tm, hook=None):
    a_list = list(a) if isinstance(a, (list, tuple)) else [a]
    b_list = list(b) if isinstance(b, (list, tuple)) else [b]
    assert len(a_list) == len(b_list) and (len(b_list) == 1 or not (tb or pre))
    b = b_list[0]
    if tb:
        N, K = b.shape
    else:
        K, N = b.shape
    M = pre[1][0].shape[0] if pre else a_list[0].shape[0]
    tn = N if post else _pick(N, (512, 256, LANES))
    ni, nj = M // tm, N // tn
    cb = 1 if tb else 0
    pre_fn, pre_rows, pre_consts = pre if pre else (None, [], [])
    post_fn, post_rows, post_consts, post_outs, post_accs = post if post else (None, [], [], [], [])
    hook_in = [] if hook is None else list(hook.inputs)
    hook_out = [] if hook is None else list(hook.out_shapes)

    def row_spec(arr):
        return pl.BlockSpec((tm, arr.shape[1]), lambda i, j: (i, 0))

    def const_spec(arr):
        return pl.BlockSpec(arr.shape, lambda i, j, nd=arr.ndim: (0,) * nd)

    args, in_specs = [], []
    for arr in (a_list if not pre else pre_rows):
        args.append(arr)
        in_specs.append(row_spec(arr))
    for arr in pre_consts:
        args.append(arr)
        in_specs.append(const_spec(arr))
    for arr in b_list:
        args.append(arr)
        in_specs.append(pl.BlockSpec((tn, K), lambda i, j: (j, 0)) if tb else
                        pl.BlockSpec((arr.shape[0], tn), lambda i, j: (0, j)))
    if add is not None:
        args.append(add)
        in_specs.append(pl.BlockSpec((tm, tn), lambda i, j: (i, j)))
    for arr in post_rows:
        args.append(arr)
        in_specs.append(row_spec(arr))
    for arr in post_consts:
        args.append(arr)
        in_specs.append(const_spec(arr))
    n_main = len(args)
    args += hook_in
    in_specs += [_ANY] * len(hook_in)

    out_shape, out_specs = [], []
    if post:
        for c, dt in post_outs:
            out_shape.append(jax.ShapeDtypeStruct((M, c), dt))
            out_specs.append(pl.BlockSpec((tm, c), lambda i, j: (i, 0)))
        for r, c in post_accs:
            out_shape.append(jax.ShapeDtypeStruct((r, c), F32))
            out_specs.append(pl.BlockSpec((r, c), lambda i, j: (0, 0)))
    else:
        out_shape.append(jax.ShapeDtypeStruct((M, N), out_dtype))
        out_specs.append(pl.BlockSpec((tm, tn), lambda i, j: (i, j)))
    if pre:
        out_shape.append(jax.ShapeDtypeStruct((M, K), BF16))
        out_specs.append(pl.BlockSpec((tm, K), lambda i, j: (i, 0)))
    n_out = len(out_shape)
    out_shape += hook_out
    out_specs += [_ANY] * len(hook_out)
    scratch = ([pltpu.VMEM((tm, K), BF16)] if pre else []) + ([] if hook is None else list(hook.scratch))
    aliases = {} if hook is None else {n_main + hi: n_out + ho for hi, ho in hook.aliases.items()}

    def body(*refs):
        ins, outs, scr = refs[:n_main], refs[len(args):len(args) + n_out], refs[len(args) + len(out_shape):]
        hargs = (refs[n_main:len(args)], refs[len(args) + n_out:len(args) + len(out_shape)], scr[1 if pre else 0:])
        i, j = pl.program_id(0), pl.program_id(1)
        if hook is not None:
            @pl.when((i == 0) & (j == 0))
            def _():
                hook.start(*hargs)

        it = iter(ins)
        if pre:
            rows_ = [next(it) for _ in pre_rows]
            consts_ = [next(it) for _ in pre_consts]

            @pl.when(j == 0)
            def _():
                av = pre_fn(*[_wide(r[...]) for r in rows_], *[_wide(r[...]) for r in consts_]).astype(BF16)
                scr[0][...] = av
                outs[-1][...] = av

            ats = [scr[0][...]]
        else:
            ats = [next(it)[...] for _ in a_list]
        p = None
        for at in ats:
            part = _dg(at, next(it)[...], 1, cb)
            p = part if p is None else p + part
        if add is not None:
            p = p + next(it)[...].astype(F32)
        if post:
            rows_ = [next(it) for _ in post_rows]
            consts_ = [next(it) for _ in post_consts]
            res = post_fn(p, *[_wide(r[...]) for r in rows_], *[_wide(r[...]) for r in consts_])
            for r, v in zip(outs[:len(post_outs)], res[:len(post_outs)]):
                r[...] = v.astype(r.dtype)
            for r, v in zip(outs[len(post_outs):], res[len(post_outs):]):
                @pl.when(i == 0)
                def _(r=r, v=v):
                    r[...] = v

                @pl.when(i > 0)
                def _(r=r, v=v):
                    r[...] += v
        else:
            outs[0][...] = p.astype(outs[0].dtype)
        if hook is not None:
            @pl.when((i == ni - 1) & (j == nj - 1))
            def _():
                hook.finish(*hargs)

    res = pl.pallas_call(
        body, name=name, grid=(ni, nj), in_specs=in_specs, out_specs=out_specs, out_shape=out_shape,
        scratch_shapes=scratch, input_output_aliases=aliases,
        compiler_params=pltpu.CompilerParams(dimension_semantics=("arbitrary", "arbitrary")),
    )(*args)
    if hook is not None:
        hook.done(res[n_out:])
    return res[:n_out]


def _mm_tiles(M, N, K, sa, sb, so, sadd):
    def tiles(d):
        return [t for t in range(LANES, min(d, 2048) + 1, LANES) if d % t == 0] or [d]

    best = None
    for tk in [K] + [t for t in tiles(K) if t < K]:
        for tm in tiles(M):
            for tn in tiles(N):
                vmem = 2 * (tm * tk * sa + tk * tn * sb + tm * tn * (so + sadd)) + (tm * tn * 4 if tk < K else 0)
                if vmem > MM_VMEM_BYTES or tm * tn > MM_MAX_OUT_TILE:
                    continue
                a_reads = 1 if tk == K else N // tn
                traffic = M * K * sa * a_reads + K * N * sb * (M // tm) + M * N * (so + sadd)
                steps = (M // tm) * (N // tn) * (K // tk)
                width = -(-tn // MXU_WIDTH) * MXU_WIDTH
                mxu = 2.0 * M * K * N * (width / tn) / MXU_FLOPS_PER_US
                edge = tm * tk * sa + tk * tn * sb + tm * tn * (so + sadd)
                cost = max(traffic / HBM_BYTES_PER_US, mxu) + steps * STEP_US + edge / HBM_BYTES_PER_US
                if best is None or cost < best[0]:
                    best = (cost, tm, tn, tk)
    assert best is not None, (M, N, K)
    return best[1:]


class _Hook:
    def __init__(self, inputs, out_shapes, aliases, scratch, start, finish, done):
        self.inputs, self.out_shapes, self.aliases, self.scratch = inputs, out_shapes, aliases, scratch
        self.start, self.finish, self.done = start, finish, done


class _Ctx:
    def __init__(self, first, last, row0, rows):
        self.first, self.last, self.row0, self.rows = first, last, row0, rows


def _rows(name, fn, ins, outs, accs=(), *, tm, nrows, ncol=1, chunk=None):
    nt = nrows // tm
    hb = tm // HALO
    nh = nrows // HALO
    ch = chunk or tm
    nch = tm // ch
    ins = [(kind, arr, arr.shape[1] if kind == "row" and cw is None else cw, base) for kind, arr, cw, base in ins]

    def row_of(k):
        return next(q for q, s in enumerate(ins) if s[0] == "row" and s[1] is ins[k][1] and s[2:] == ins[k][2:])
    in_specs, args = [], []
    for kind, arr, cw, base in ins:
        if kind == "row":
            in_specs.append(pl.BlockSpec((tm, cw), lambda j, i, base=base: (i, base + j)))
        elif kind == "prev":
            in_specs.append(pl.BlockSpec((HALO, cw), lambda j, i, base=base: (jnp.maximum(i * hb - 1, 0), base + j)))
        elif kind == "next":
            in_specs.append(pl.BlockSpec((HALO, cw), lambda j, i, base=base: (jnp.minimum((i + 1) * hb, nh - 1), base + j)))
        elif kind in ("const", "raw"):
            in_specs.append(pl.BlockSpec(arr.shape, lambda j, i, nd=arr.ndim: (0,) * nd))
        elif kind == "ccol":
            in_specs.append(pl.BlockSpec((arr.shape[0], cw), lambda j, i, base=base: (0, base + j)))
        else:
            raise ValueError(kind)
        args.append(arr)
    out_specs, out_shape = [], []
    for ctot, cw, base, dt in outs:
        out_specs.append(pl.BlockSpec((tm, cw), lambda j, i, base=base: (i, base + j)))
        out_shape.append(jax.ShapeDtypeStruct((nrows, ctot), dt))
    for r, ctot, cw in accs:
        out_specs.append(pl.BlockSpec((r, cw), lambda j, i: (0, j)))
        out_shape.append(jax.ShapeDtypeStruct((r, ctot), F32))
    n_in, n_out = len(ins), len(outs)

    def body(*refs):
        i = pl.program_id(1)
        in_refs, out_refs, acc_refs = refs[:n_in], refs[n_in:n_in + n_out], refs[n_in + n_out:]
        if acc_refs:
            @pl.when(i == 0)
            def _():
                for r in acc_refs:
                    r[...] = jnp.zeros_like(r)

        whole = {k: (in_refs[k][...] if s[0] == "raw" else _wide(in_refs[k][...]))
                 for k, s in enumerate(ins) if s[0] in ("const", "ccol", "raw")}

        def do_chunk(c, carry):
            r0 = pl.multiple_of(c * ch, ch) if nch > 1 else 0
            rows_ = pl.ds(r0, ch)
            vals = []
            for k, (kind, _, _, _) in enumerate(ins):
                r = in_refs[k]
                if k in whole:
                    vals.append(whole[k])
                    continue
                if kind == "row":
                    v = r[rows_, :]
                elif nch == 1:
                    v = r[...]
                elif kind == "prev":
                    inner = in_refs[row_of(k)][pl.ds(pl.multiple_of(jnp.maximum(r0 - HALO, 0), HALO), HALO), :]
                    v = jnp.where(c == 0, r[...], inner)
                else:
                    inner = in_refs[row_of(k)][pl.ds(pl.multiple_of(jnp.minimum(r0 + ch, tm - HALO), HALO), HALO), :]
                    v = jnp.where(c == nch - 1, r[...], inner)
                vals.append(_wide(v))
            ctx = _Ctx((i == 0) & (c == 0), (i == nt - 1) & (c == nch - 1), i * tm + r0, ch)
            res = fn(ctx, *vals)
            for r, v in zip(out_refs, res[:n_out]):
                r[rows_, :] = v.astype(r.dtype)
            for r, v in zip(acc_refs, res[n_out:]):
                r[...] += v
            return carry

        if nch == 1:
            do_chunk(0, 0)
        else:
            lax.fori_loop(0, nch, do_chunk, 0)

    res = pl.pallas_call(
        body, name=name, grid=(ncol, nt), in_specs=in_specs, out_specs=out_specs, out_shape=out_shape,
        compiler_params=pltpu.CompilerParams(dimension_semantics=("arbitrary", "arbitrary")),
    )(*args)
    return res


def _shift_down(xcat, k):
    return xcat if k == 0 else pltpu.roll(xcat, k, 0)


def _shift_up(xcat, k):
    return xcat if k == 0 else pltpu.roll(xcat, xcat.shape[0] - k, 0)


def _with_prev(ctx, halo, x):
    return jnp.concatenate([jnp.where(ctx.first, 0.0, halo), x], axis=0)


def _with_next(ctx, x, halo):
    return jnp.concatenate([x, jnp.where(ctx.last, 0.0, halo)], axis=0)


def _rms_core(x, g):
    r = lax.rsqrt(jnp.mean(x * x, axis=-1, keepdims=True) + EPS)
    return x * r * g


def _rms_post(du, xv, drv, gv):
    _, vjp = jax.vjp(_rms_core, xv, gv)
    dx, dg = vjp(du)
    return [drv + dx, dg]


def _final_loss(x, target, g):
    S = x.shape[0]

    def fn(ctx, xv, tv, gv):
        def f(xx, gg):
            err = _rms_core(xx, gg) - tv
            return 0.5 * jnp.sum(err * err) / D

        loss, vjp = jax.vjp(f, xv, gv)
        dx, dg = vjp(jnp.ones((), F32))
        return [dx, dg, jnp.zeros((1, LANES), F32) + loss]

    return _rows("final_loss", fn, [("row", x, None, 0), ("row", target, None, 0), ("const", g, None, 0)],
                 [(D, D, 0, F32)], [(1, D, D), (1, LANES, LANES)], tm=256, nrows=S, chunk=CHUNK_WIDE)


def _attn_valid(n):
    qi = _iota((WIN, 2 * WIN), 0)
    kk = _iota((WIN, 2 * WIN), 1)
    rel = qi + WIN - kk
    return (rel >= 0) & (rel <= WIN) & ((kk >= WIN) | (n > 0))


def _attn_block(q, kp, kc, vp, vc, b0, b1, valid):
    k = jnp.concatenate([kp, kc], axis=0)
    v = jnp.concatenate([vp, vc], axis=0)
    lo = _iota((WIN, LANES), 1) < HD
    scale = 1.0 / math.sqrt(HD)
    os_, ls_ = [], []
    for hh, b in ((0, b0), (1, b1)):
        qm = jnp.where(lo if hh == 0 else ~lo, q, 0.0)
        s = _bdot_nt(qm, k) * scale + b
        s = jnp.where(valid, s, NEG)
        m = lax.stop_gradient(jnp.max(s, axis=1, keepdims=True))
        p = jnp.exp(s - m)
        l = jnp.sum(p, axis=1, keepdims=True)
        os_.append(_bdot_nn(p, v) / l)
        ls_.append(m + jnp.log(l))
    return jnp.where(lo, os_[0], os_[1]), jnp.where(lo, ls_[0], ls_[1])


def _residue_rows(r, d):
    return pl.ds(0, WIN) if d == 1 else pl.ds(r, WIN, stride=d)


def _for_residues(d, fn):
    if d == 1:
        fn(0, 0)
    else:
        lax.fori_loop(0, d, fn, 0, unroll=min(d, 8))


def _pairs_per_step(d):
    return 3 if d == 1 else 1


def _bias_table(rel_bias, bucket, gi, name):
    def body(t_ref, b_ref, o_ref):
        h = 6 * gi + pl.program_id(0)
        b = b_ref[...]
        acc = jnp.zeros(b.shape, F32)
        for k in range(REL_BUCKETS):
            acc = jnp.where(b == k, t_ref[k, h], acc)
        o_ref[0] = acc

    return pl.pallas_call(
        body, name=name, grid=(6,),
        in_specs=[pl.BlockSpec(memory_space=pltpu.SMEM), pl.BlockSpec((WIN, 2 * WIN), lambda h: (0, 0))],
        out_specs=pl.BlockSpec((1, WIN, 2 * WIN), lambda h: (h, 0, 0)),
        out_shape=jax.ShapeDtypeStruct((6, WIN, 2 * WIN), F32),
    )(rel_bias, bucket)


def _attn_fwd(pa, bias, gi, name):
    S = pa.shape[0]
    d = DILATIONS[gi]
    bt = WIN * d
    nb = S // bt
    hpw = _pairs_per_step(d)
    bw = hpw * LANES
    cb = 3 * gi // hpw

    def body(q_ref, kp_ref, kc_ref, vp_ref, vc_ref, b_ref, o_ref, l_ref):
        valid = _attn_valid(pl.program_id(1))

        def residue(r, carry):
            sl = _residue_rows(r, d)
            for t in range(hpw):
                ln = pl.ds(t * LANES, LANES)
                o, lse = _attn_block(q_ref[sl, ln], kp_ref[sl, ln], kc_ref[sl, ln], vp_ref[sl, ln], vc_ref[sl, ln],
                                     b_ref[2 * t], b_ref[2 * t + 1], valid)
                o_ref[sl, ln] = o
                l_ref[sl, ln] = lse
            return carry

        _for_residues(d, residue)

    def spec(off, prev):
        if prev:
            return pl.BlockSpec((bt, bw), lambda hp, n: (jnp.maximum(n - 1, 0), off // hpw + cb + hp))
        return pl.BlockSpec((bt, bw), lambda hp, n: (n, off // hpw + cb + hp))

    ospec = pl.BlockSpec((bt, bw), lambda hp, n: (n, hp))
    return pl.pallas_call(
        body, name=name, grid=(3 // hpw, nb),
        in_specs=[spec(0, False), spec(9, True), spec(9, False), spec(18, True), spec(18, False),
                  pl.BlockSpec((2 * hpw, WIN, 2 * WIN), lambda hp, n: (hp, 0, 0))],
        out_specs=[ospec, ospec],
        out_shape=[jax.ShapeDtypeStruct((S, GW), F32)] * 2,
        compiler_params=pltpu.CompilerParams(dimension_semantics=("parallel", "arbitrary")),
    )(pa, pa, pa, pa, pa, bias)


def _attn_bwd(pa, bias, do, dlse, db_in, dqkv, gi, name):
    S = pa.shape[0]
    d = DILATIONS[gi]
    bt = WIN * d
    nb = S // bt
    hpw = _pairs_per_step(d)
    bw = hpw * LANES
    cb = 3 * gi // hpw

    def body(q_ref, kp_ref, kc_ref, vp_ref, vc_ref, b_ref, do_ref, dl_ref, dbi_ref, dqi_ref, dki_ref, dvi_ref,
             dq_ref, dk_ref, dv_ref, db_ref, ck, cv):
        n = pl.program_id(1)

        @pl.when(n == 0)
        def _():
            db_ref[...] = dbi_ref[...]
            ck[...] = jnp.zeros_like(ck)
            cv[...] = jnp.zeros_like(cv)

        @pl.when(n < nb)
        def _():
            f = functools.partial(_attn_block, valid=_attn_valid(n))

            def residue(r, carry):
                sl = _residue_rows(r, d)
                cs = pl.ds(pl.multiple_of(r * WIN, WIN), WIN)
                for t in range(hpw):
                    ln = pl.ds(t * LANES, LANES)
                    _, vjp = jax.vjp(f, q_ref[sl, ln], kp_ref[sl, ln], kc_ref[sl, ln], vp_ref[sl, ln], vc_ref[sl, ln],
                                     b_ref[2 * t], b_ref[2 * t + 1])
                    dq, dkp, dkc, dvp, dvc, db0, db1 = vjp((do_ref[sl, ln], dl_ref[sl, ln]))
                    dq_ref[sl, ln] = dq
                    dk_ref[sl, ln] = ck[cs, ln] + dkp
                    dv_ref[sl, ln] = cv[cs, ln] + dvp
                    ck[cs, ln] = dkc
                    cv[cs, ln] = dvc
                    db_ref[2 * t] += db0
                    db_ref[2 * t + 1] += db1
                return carry

            _for_residues(d, residue)

        @pl.when(n == nb)
        def _():
            def residue(r, carry):
                sl = _residue_rows(r, d)
                cs = pl.ds(pl.multiple_of(r * WIN, WIN), WIN)
                dk_ref[sl, :] = ck[cs, :]
                dv_ref[sl, :] = cv[cs, :]
                return carry

            _for_residues(d, residue)

    def cur(n):
        return jnp.minimum(n, nb - 1)

    def spec(off, prev):
        if prev:
            return pl.BlockSpec((bt, bw), lambda hp, n: (jnp.maximum(cur(n) - 1, 0), off // hpw + cb + hp))
        return pl.BlockSpec((bt, bw), lambda hp, n: (cur(n), off // hpw + cb + hp))

    gspec = pl.BlockSpec((bt, bw), lambda hp, n: (cur(n), hp))
    bspec = pl.BlockSpec((2 * hpw, WIN, 2 * WIN), lambda hp, n: (hp, 0, 0))
    qspec = pl.BlockSpec((bt, bw), lambda hp, n: (cur(n), cb + hp))
    kspec = pl.BlockSpec((bt, bw), lambda hp, n: (jnp.maximum(n - 1, 0), cb + hp))
    dq, dk, dv, db = pl.pallas_call(
        body, name=name, grid=(3 // hpw, nb + 1),
        in_specs=[spec(0, False), spec(9, True), spec(9, False), spec(18, True), spec(18, False),
                  bspec, gspec, gspec, bspec, _ANY, _ANY, _ANY],
        out_specs=[qspec, kspec, kspec, bspec],
        out_shape=[jax.ShapeDtypeStruct((S, AW), F32)] * 3 + [jax.ShapeDtypeStruct((6, WIN, 2 * WIN), F32)],
        scratch_shapes=[pltpu.VMEM((bt, bw), F32), pltpu.VMEM((bt, bw), F32)],
        input_output_aliases={9: 0, 10: 1, 11: 2},
        compiler_params=pltpu.CompilerParams(dimension_semantics=("arbitrary", "arbitrary")),
    )(pa, pa, pa, pa, pa, bias, do, dlse, db_in, *dqkv)
    return (dq, dk, dv), db


def _mix_core(o0, o1, o2, l0, l1, l2):
    m = lax.stop_gradient(jnp.maximum(jnp.maximum(l0, l1), l2))
    e0, e1, e2 = jnp.exp(l0 - m), jnp.exp(l1 - m), jnp.exp(l2 - m)
    return (e0 * o0 + e1 * o1 + e2 * o2) / (e0 + e1 + e2)


def _mix_fwd(os_, ls_, name):
    S = os_[0].shape[0]
    ins = [("row", a, None, 0) for a in (*os_, *ls_)]
    return _rows(name, lambda ctx, *v: [_mix_core(*v)], ins, [(GW, GW, 0, BF16)], tm=256, nrows=S, chunk=CHUNK_NARROW)[0]


def _mix_bwd(os_, ls_, datt, name):
    S = datt.shape[0]

    def fn(ctx, *v):
        _, vjp = jax.vjp(_mix_core, *v[:6])
        return list(vjp(v[6]))

    ins = [("row", a, None, 0) for a in (*os_, *ls_, datt)]
    outs = [(GW, GW, 0, F32)] * 6
    r = _rows(name, fn, ins, outs, tm=256, nrows=S, chunk=CHUNK_NARROW)
    return r[:3], r[3:]


def _t5_bucket(dist):
    max_exact = REL_BUCKETS // 2
    is_small = dist < max_exact
    nf = jnp.maximum(dist, 1).astype(F32)
    large = max_exact + (jnp.log(nf / max_exact) / math.log(REL_MAX_DISTANCE / max_exact)
                         * (REL_BUCKETS - max_exact)).astype(jnp.int32)
    large = jnp.minimum(large, REL_BUCKETS - 1)
    return jnp.where(is_small, dist, large)


def _buckets(d):
    qi = jnp.arange(WIN)[:, None]
    kk = jnp.arange(2 * WIN)[None, :]
    rel = qi + WIN - kk
    return _t5_bucket(jnp.clip(rel, 0, None) * d)


def _pool_cnt(ctx, w):
    pos = ctx.row0 + _iota((ctx.rows, PG), 0) + 1
    return jnp.minimum(pos, w).astype(F32)


def _pool_d(ctx, halo, u):
    ds = []
    for g, w in enumerate(POOL_WINDOWS):
        ug = u[:, g * PG:(g + 1) * PG]
        s = _with_prev(ctx, halo[:, g * PG:(g + 1) * PG], ug)
        step = 1
        while step < w:
            s = s + _shift_down(s, step)
            step *= 2
        ds.append(s[HALO:] / _pool_cnt(ctx, w) - ug)
    return ds


def _pool_fwd(pb, pw, scale, name):
    S = pb.shape[0]

    def fn(ctx, halo, u, w, sc):
        ds = _pool_d(ctx, halo, u)
        return [jnp.concatenate([_dg(ds[k], w[k], 1, 0) for k in range(4)], axis=1) * sc]

    return _rows(name, fn, [("prev", pb, D, 0), ("row", pb, None, 0), ("raw", pw, None, 0), ("const", scale, None, 0)],
                 [(D, D, 0, BF16)], tm=256, nrows=S, chunk=CHUNK_POOL)[0]


def _pool_bwd(pb, pw, scale, dpo, name):
    S = pb.shape[0]

    def fn1(ctx, halo, u, w, sc, dy):
        ds = _pool_d(ctx, halo, u)
        dyp = dy * sc
        y = jnp.concatenate([_dg(ds[k], w[k], 1, 0) for k in range(4)], axis=1)
        es, dws = [], []
        for k, wd in enumerate(POOL_WINDOWS):
            cols = slice(k * PG, (k + 1) * PG)
            es.append(_dg(dyp[:, cols], w[k], 1, 1) / _pool_cnt(ctx, wd))
            dws.append(_dg(ds[k], dyp[:, cols], 0, 0))
        return [jnp.concatenate(es, axis=1), jnp.concatenate(dws, axis=0), jnp.sum(dy * y, axis=0, keepdims=True)]

    e, dpw, dsc = _rows(name + "_a", fn1,
                        [("prev", pb, D, 0), ("row", pb, None, 0), ("raw", pw, None, 0), ("const", scale, None, 0),
                         ("row", dpo, None, 0)],
                        [(D, D, 0, F32)], [(4 * PG, PG, PG), (1, D, D)], tm=256, nrows=S, chunk=CHUNK_POOL)

    def fn2(ctx, ev, halo):
        outs = []
        for g, w in enumerate(POOL_WINDOWS):
            eg = ev[:, g * PG:(g + 1) * PG]
            s = _with_next(ctx, eg, halo[:, g * PG:(g + 1) * PG])
            step = 1
            while step < w:
                s = s + _shift_up(s, step)
                step *= 2
            outs.append(s[:ctx.rows] - eg * _pool_cnt(ctx, w))
        return [jnp.concatenate(outs, axis=1)]

    du = _rows(name + "_b", fn2, [("row", e, None, 0), ("next", e, D, 0)], [(D, D, 0, BF16)], tm=256, nrows=S,
               chunk=CHUNK_POOL)[0]
    return du, dpw, dsc


def _conv_taps(ctx, halo, x, K):
    cat = _with_prev(ctx, halo, x)
    return [_shift_down(cat, K - 1 - k)[HALO:] for k in range(K)]


def _conv_pre(taps, w, b):
    acc = b
    for k, t in enumerate(taps):
        acc = acc + t * _row_pick(w, k)
    return acc


CW = 256
CWS = 512
CONV_TM = 512
CHUNK_NARROW = None
CHUNK_POOL = None
CHUNK_WIDE = None


def _ext_taps(ctx, prev, x, nxt, K):
    cat = jnp.concatenate([jnp.where(ctx.first, 0.0, prev), x, jnp.where(ctx.last, 0.0, nxt)], axis=0)
    return [_shift_down(cat, K - 1 - k)[HALO:] for k in range(K)]


def _conv_t_rows(dp, w, K, tm):
    acc = jnp.zeros((tm, dp.shape[1]), F32)
    for k in range(K):
        acc = acc + _shift_up(dp, K - 1 - k)[:tm] * _row_pick(w, k)
    return acc


def _ssd_conv_fwd(pc, w, b, name):
    S = pc.shape[0]
    base = D // CWS

    def fn(ctx, halo, x, wv, bv):
        return [_silu(_conv_pre(_conv_taps(ctx, halo, x, 4), wv, bv))]

    return _rows(name, fn, [("prev", pc, CWS, base), ("row", pc, CWS, base), ("ccol", w, CWS, 0), ("ccol", b, CWS, 0)],
                 [(XBC, CWS, 0, F32)], tm=CONV_TM, nrows=S, ncol=XBC // CWS, chunk=CHUNK_POOL)[0]


def _ssd_conv_bwd(pc, w, b, dy, name):
    S = pc.shape[0]
    base = D // CWS

    def fn(ctx, prev, x, nxt, wv, bv, dyv, dyn):
        n = ctx.rows
        taps = _ext_taps(ctx, prev, x, nxt, 4)
        pre = _conv_pre(taps, wv, bv)
        sg = _sigmoid(pre)
        dye = jnp.concatenate([dyv, jnp.where(ctx.last, 0.0, dyn)], axis=0)
        dpre = dye * sg * (1.0 + pre * (1.0 - sg))
        dw = _stack_rows([jnp.sum(dpre[:n] * t[:n], axis=0, keepdims=True) for t in taps], 4)
        return [_conv_t_rows(dpre, wv, 4, n), dw, jnp.sum(dpre[:n], axis=0, keepdims=True)]

    return _rows(name, fn,
                 [("prev", pc, CWS, base), ("row", pc, CWS, base), ("next", pc, CWS, base), ("ccol", w, CWS, 0),
                  ("ccol", b, CWS, 0), ("row", dy, CWS, 0), ("next", dy, CWS, 0)],
                 [(XBC, CWS, 0, BF16)], [(4, XBC, CWS), (1, XBC, CWS)], tm=CONV_TM, nrows=S, ncol=XBC // CWS,
                 chunk=CHUNK_POOL)


NFC = D_FF // CW


def _ffn_act_fwd(h, w, b, name):
    S = h.shape[0]

    def fn(ctx, ha, a, hv, v, wa, wv, ba, bv):
        pa = _conv_pre(_conv_taps(ctx, ha, a, 3), wa, ba)
        pv = _conv_pre(_conv_taps(ctx, hv, v, 3), wv, bv)
        return [_silu(pa) * pv]

    return _rows(name, fn,
                 [("prev", h, CW, 0), ("row", h, CW, 0), ("prev", h, CW, NFC), ("row", h, CW, NFC),
                  ("ccol", w, CW, 0), ("ccol", w, CW, NFC), ("ccol", b, CW, 0), ("ccol", b, CW, NFC)],
                 [(D_FF, CW, 0, BF16)], tm=CONV_TM, nrows=S, ncol=NFC, chunk=CHUNK_NARROW)[0]


def _ffn_act_bwd(h, w, b, df, name):
    S = h.shape[0]

    def fn(ctx, pa_, a, na, pv_, v, nv, wa, wv, ba, bv, dfv, dfn):
        n = ctx.rows
        ta = _ext_taps(ctx, pa_, a, na, 3)
        tv = _ext_taps(ctx, pv_, v, nv, 3)
        pa = _conv_pre(ta, wa, ba)
        pv = _conv_pre(tv, wv, bv)
        sg = _sigmoid(pa)
        dfe = jnp.concatenate([dfv, jnp.where(ctx.last, 0.0, dfn)], axis=0)
        dpa = dfe * pv * sg * (1.0 + pa * (1.0 - sg))
        dpv = dfe * pa * sg
        res = [_conv_t_rows(dpa, wa, 3, n), _conv_t_rows(dpv, wv, 3, n)]
        for dp, taps in ((dpa, ta), (dpv, tv)):
            res.append(_stack_rows([jnp.sum(dp[:n] * t[:n], axis=0, keepdims=True) for t in taps], 3))
        for dp in (dpa, dpv):
            res.append(jnp.sum(dp[:n], axis=0, keepdims=True))
        return res

    ins = []
    for base in (0, NFC):
        ins += [("prev", h, CW, base), ("row", h, CW, base), ("next", h, CW, base)]
    ins += [("ccol", w, CW, 0), ("ccol", w, CW, NFC), ("ccol", b, CW, 0), ("ccol", b, CW, NFC),
            ("row", df, CW, 0), ("next", df, CW, 0)]
    dha, dhv, dwa, dwv, dba, dbv = _rows(
        name, fn, ins, [(D_FF, CW, 0, BF16)] * 2, [(3, D_FF, CW)] * 2 + [(1, D_FF, CW)] * 2, tm=CONV_TM, nrows=S, ncol=NFC,
        chunk=CHUNK_NARROW)
    return dha, dhv, jnp.concatenate([dwa, dwv], axis=1), jnp.concatenate([dba, dbv], axis=1)


NSLAB = D // LANES
CPS = 2


def _ssd_chunk(xs, Bs, Cs, dtraw, dtb, alog, prev):
    lsz = SSD_CHUNK
    lane = _iota((lsz, LANES), 1)
    row = _iota((lsz, LANES), 0)
    dt = jnp.where(lane < SSD_HEADS, _softplus(dtraw + dtb), 0.0)
    a = dt * (-jnp.exp(alog))
    tril = row >= lane
    a_cs = _fdot(tril.astype(F32), a)
    a_cst = a_cs.T
    a_last = jnp.sum(a, axis=0, keepdims=True)
    lo = lane < HD
    top = row < HD
    cbs = [_bdot_nt(Cs[g], Bs[g]) for g in range(2)]
    ys, news = [], []
    for s in range(NSLAB):
        g = s // (NSLAB // 2)
        cols, lms, dts, als = [], [], [], []
        for hh in range(2):
            h = 2 * s + hh
            col = _lane_pick(a_cs, h)
            seg = col - _row_pick(a_cst, h)
            lms.append(jnp.exp(jnp.where(tril, seg, NEG)))
            cols.append(col)
            dts.append(_lane_pick(dt, h))
            als.append(_lane_pick(a_last, h))
        col_x = jnp.where(lo, cols[0], cols[1])
        al_x = jnp.where(lo, als[0], als[1])
        xc = xs[s] * jnp.where(lo, dts[0], dts[1])
        yd = jnp.where(lo, _bdot_nn(cbs[g] * lms[0], xc), _bdot_nn(cbs[g] * lms[1], xc))
        yoff = _bdot_nt(Cs[g], prev[s]) * jnp.exp(col_x)
        ys.append(yd + yoff)
        st = _bdot_tn(xc * jnp.exp(al_x - col_x), Bs[g])
        news.append(prev[s] * jnp.exp(jnp.where(top, als[0], als[1])) + st)
    return ys, news


def _ssd_scan_fwd(xbc_c, pd, dtb, alog, name):
    S = xbc_c.shape[0]
    nc = S // SSD_CHUNK
    rows_ = CPS * SSD_CHUNK

    def body(x_ref, b_ref, c_ref, dt_ref, dtb_ref, al_ref, y_ref, st_ref, state):
        c = pl.program_id(0)

        @pl.when(c == 0)
        def _():
            state[...] = jnp.zeros_like(state)

        prev = [state[s * LANES:(s + 1) * LANES, :] for s in range(NSLAB)]
        for u in range(CPS):
            rw = pl.ds(u * SSD_CHUNK, SSD_CHUNK)
            xs = [x_ref[rw, s * LANES:(s + 1) * LANES] for s in range(NSLAB)]
            Bs = [b_ref[rw, g * SSD_N:(g + 1) * SSD_N] for g in range(2)]
            Cs = [c_ref[rw, g * SSD_N:(g + 1) * SSD_N] for g in range(2)]
            for s in range(NSLAB):
                st_ref[u, s * LANES:(s + 1) * LANES, :] = prev[s]
            ys, prev = _ssd_chunk(xs, Bs, Cs, dt_ref[rw, :].astype(F32), dtb_ref[...], al_ref[...], prev)
            for s in range(NSLAB):
                y_ref[rw, s * LANES:(s + 1) * LANES] = ys[s]
        for s in range(NSLAB):
            state[s * LANES:(s + 1) * LANES, :] = prev[s]

    return pl.pallas_call(
        body, name=name, grid=(nc // CPS,),
        in_specs=[pl.BlockSpec((rows_, D), lambda c: (c, 0)),
                  pl.BlockSpec((rows_, 2 * SSD_N), lambda c: (c, D // (2 * SSD_N))),
                  pl.BlockSpec((rows_, 2 * SSD_N), lambda c: (c, D // (2 * SSD_N) + 1)),
                  pl.BlockSpec((rows_, LANES), lambda c: (c, 0)),
                  pl.BlockSpec((1, LANES), lambda c: (0, 0)), pl.BlockSpec((1, LANES), lambda c: (0, 0))],
        out_specs=[pl.BlockSpec((rows_, D), lambda c: (c, 0)), pl.BlockSpec((CPS, D, SSD_N), lambda c: (c, 0, 0))],
        out_shape=[jax.ShapeDtypeStruct((S, D), F32), jax.ShapeDtypeStruct((nc, D, SSD_N), F32)],
        scratch_shapes=[pltpu.VMEM((D, SSD_N), F32)],
        compiler_params=pltpu.CompilerParams(dimension_semantics=("arbitrary",)),
    )(xbc_c, xbc_c, xbc_c, pd, dtb, alog)


def _ssd_scan_bwd(xbc_c, pd, dtb, alog, states, dy, dxs_skip, name):
    S = xbc_c.shape[0]
    nc = S // SSD_CHUNK
    rows_ = CPS * SSD_CHUNK

    def body(x_ref, b_ref, c_ref, dt_ref, dtb_ref, al_ref, st_ref, dy_ref, sk_ref,
             dx_ref, ddt_ref, ddtb_ref, dal_ref, dstate):
        c = pl.program_id(0)

        @pl.when(c == 0)
        def _():
            dstate[...] = jnp.zeros_like(dstate)
            ddtb_ref[...] = jnp.zeros_like(ddtb_ref)
            dal_ref[...] = jnp.zeros_like(dal_ref)

        dnew = [dstate[s * LANES:(s + 1) * LANES, :] for s in range(NSLAB)]
        for u in reversed(range(CPS)):
            rw = pl.ds(u * SSD_CHUNK, SSD_CHUNK)
            xs = [x_ref[rw, s * LANES:(s + 1) * LANES] for s in range(NSLAB)]
            Bs = [b_ref[rw, g * SSD_N:(g + 1) * SSD_N] for g in range(2)]
            Cs = [c_ref[rw, g * SSD_N:(g + 1) * SSD_N] for g in range(2)]
            prev = [st_ref[u, s * LANES:(s + 1) * LANES, :] for s in range(NSLAB)]
            _, vjp = jax.vjp(_ssd_chunk, xs, Bs, Cs, dt_ref[rw, :].astype(F32), dtb_ref[...], al_ref[...], prev)
            dys = [dy_ref[rw, s * LANES:(s + 1) * LANES] for s in range(NSLAB)]
            dxs, dBs, dCs, ddt, ddtb, dal, dnew = vjp((dys, dnew))
            for s in range(NSLAB):
                dx_ref[rw, s * LANES:(s + 1) * LANES] = dxs[s] + sk_ref[rw, s * LANES:(s + 1) * LANES]
            for g in range(2):
                dx_ref[rw, D + g * SSD_N:D + (g + 1) * SSD_N] = dBs[g]
                dx_ref[rw, D + 2 * SSD_N + g * SSD_N:D + 2 * SSD_N + (g + 1) * SSD_N] = dCs[g]
            ddt_ref[rw, :] = ddt
            ddtb_ref[...] += ddtb
            dal_ref[...] += dal
        for s in range(NSLAB):
            dstate[s * LANES:(s + 1) * LANES, :] = dnew[s]

    def rv(c):
        return nc // CPS - 1 - c

    return pl.pallas_call(
        body, name=name, grid=(nc // CPS,),
        in_specs=[pl.BlockSpec((rows_, D), lambda c: (rv(c), 0)),
                  pl.BlockSpec((rows_, 2 * SSD_N), lambda c: (rv(c), D // (2 * SSD_N))),
                  pl.BlockSpec((rows_, 2 * SSD_N), lambda c: (rv(c), D // (2 * SSD_N) + 1)),
                  pl.BlockSpec((rows_, LANES), lambda c: (rv(c), 0)),
                  pl.BlockSpec((1, LANES), lambda c: (0, 0)), pl.BlockSpec((1, LANES), lambda c: (0, 0)),
                  pl.BlockSpec((CPS, D, SSD_N), lambda c: (rv(c), 0, 0)),
                  pl.BlockSpec((rows_, D), lambda c: (rv(c), 0)),
                  pl.BlockSpec((rows_, D), lambda c: (rv(c), 0))],
        out_specs=[pl.BlockSpec((rows_, XBC), lambda c: (rv(c), 0)),
                   pl.BlockSpec((rows_, LANES), lambda c: (rv(c), 0)),
                   pl.BlockSpec((1, LANES), lambda c: (0, 0)), pl.BlockSpec((1, LANES), lambda c: (0, 0))],
        out_shape=[jax.ShapeDtypeStruct((S, XBC), F32), jax.ShapeDtypeStruct((S, LANES), F32),
                   jax.ShapeDtypeStruct((1, LANES), F32), jax.ShapeDtypeStruct((1, LANES), F32)],
        scratch_shapes=[pltpu.VMEM((D, SSD_N), F32)],
        compiler_params=pltpu.CompilerParams(dimension_semantics=("arbitrary",)),
    )(xbc_c, xbc_c, xbc_c, pd, dtb, alog, states, dy, dxs_skip)


def _ssd_post_core(y, xs, z, d128, nw):
    tm = y.shape[0]
    ex = (_iota((LANES, D), 1) // HD == _iota((LANES, D), 0)).astype(F32)
    d_x = jnp.sum(_fdot(jnp.broadcast_to(d128, (8, LANES)), ex), axis=0, keepdims=True) * 0.125
    y2 = (y + d_x * xs) * _silu(z)
    lo = _iota((tm, D), 1) < D // 2
    sq = y2 * y2
    ms0 = jnp.sum(jnp.where(lo, sq, 0.0), axis=-1, keepdims=True) / (D // 2)
    ms1 = jnp.sum(jnp.where(lo, 0.0, sq), axis=-1, keepdims=True) / (D // 2)
    r = jnp.where(lo, lax.rsqrt(ms0 + EPS), lax.rsqrt(ms1 + EPS))
    return y2 * r * nw


def _ssd_post_ins(y, xbc_c, pc, d128, nw):
    return [("row", y, None, 0), ("row", xbc_c, D, 0), ("row", pc, D, 0), ("const", d128, None, 0), ("const", nw, None, 0)]


def _ssd_post_fwd(y, xbc_c, pc, d128, nw, name):
    S = y.shape[0]
    return _rows(name, lambda ctx, *v: [_ssd_post_core(*v)], _ssd_post_ins(y, xbc_c, pc, d128, nw),
                 [(D, D, 0, BF16)], tm=256, nrows=S, chunk=CHUNK_WIDE)[0]


def _ssd_post_bwd(y, xbc_c, pc, d128, nw, dout, name):
    S = y.shape[0]

    def fn(ctx, *v):
        _, vjp = jax.vjp(_ssd_post_core, *v[:5])
        return list(vjp(v[5]))

    return _rows(name, fn, _ssd_post_ins(y, xbc_c, pc, d128, nw) + [("row", dout, None, 0)],
                 [(D, D, 0, F32), (D, D, 0, F32), (D, D, 0, BF16)], [(1, LANES, LANES), (1, D, D)], tm=256, nrows=S,
                 chunk=CHUNK_WIDE)


def _gates_core(g0, g1, g2, b0, b1, b2, ya, yb, yc):
    return _sigmoid(g0 + b0) * ya + _sigmoid(g1 + b1) * yb + _sigmoid(g2 + b2) * yc


def _gate_parts(pdv, bv):
    gp = pltpu.roll(pdv, SEC_D - 16, 1)
    return [gp[:, k * D:(k + 1) * D] for k in range(3)] + [bv[:, k * D:(k + 1) * D] for k in range(3)]


def _gates_fwd(pd, bg, ya, yb, yc, name):
    S = pd.shape[0]

    def fn(ctx, pdv, bv, a, b, c):
        return [_gates_core(*_gate_parts(pdv, bv), a, b, c)]

    return _rows(name, fn, [("row", pd, None, 0), ("const", bg, None, 0), ("row", ya, None, 0), ("row", yb, None, 0),
                            ("row", yc, None, 0)], [(D, D, 0, BF16)], tm=256, nrows=S, chunk=CHUNK_WIDE)[0]


def _gates_post(dm, pdv, a, b, c, bv):
    _, vjp = jax.vjp(_gates_core, *_gate_parts(pdv, bv), a, b, c)
    g = vjp(dm)
    return [g[6], g[7], g[8], jnp.concatenate(g[0:3], axis=1), jnp.concatenate(g[3:6], axis=1)]


def _adam_update(wv, gv, mv, vv):
    m2 = ADAM_B1 * mv + (1.0 - ADAM_B1) * gv
    v2 = ADAM_B2 * vv + (1.0 - ADAM_B2) * jnp.square(gv)
    m_hat = m2 / (1.0 - ADAM_B1 ** ADAM_STEP)
    v_hat = v2 / (1.0 - ADAM_B2 ** ADAM_STEP)
    delta = -ADAM_LR * (m_hat / (jnp.sqrt(v_hat) + ADAM_EPS) + ADAM_WD * wv)
    return [delta, m2, v2]


def _adamw(w, g, m, v, name):
    rows, C = w.shape
    tm = _pick(rows, [t for t in (512, 256, 128, 64, 32, 16, 8) if t * C <= ADAM_TILE])
    return _rows(name, lambda ctx, *a: _adam_update(*a), [("row", a, None, 0) for a in (w, g, m, v)],
                 [(C, C, 0, F32)] * 3, tm=tm, nrows=rows)


def _adamw_part(w, g, m, v, row0, prev, name, hook=None):
    rows, C = w.shape
    n = g.shape[0]
    tm = _pick(math.gcd(n, row0) if row0 else n, [t for t in (512, 256, 128, 64, 32, 16, 8) if t * C <= ADAM_TILE])
    off = row0 // tm
    n_h = 0 if hook is None else len(hook.inputs)
    n_ho = 0 if hook is None else len(hook.out_shapes)
    n_prev = 0 if prev is None else 3
    n_in = 4 + n_prev + n_h

    def body(*refs):
        w_ref, g_ref, m_ref, v_ref = refs[:4]
        outs = refs[n_in:n_in + 3]
        hargs = (refs[4 + n_prev:n_in], refs[n_in + 3:n_in + 3 + n_ho], refs[n_in + 3 + n_ho:])
        i = pl.program_id(0)
        if hook is not None:
            @pl.when(i == 0)
            def _():
                hook.start(*hargs)

        for r, val in zip(outs, _adam_update(w_ref[...], g_ref[...], m_ref[...], v_ref[...])):
            r[...] = val
        if hook is not None:
            @pl.when(i == n // tm - 1)
            def _():
                hook.finish(*hargs)

    part = pl.BlockSpec((tm, C), lambda i: (i + off, 0))
    in_specs = [part, pl.BlockSpec((tm, C), lambda i: (i, 0)), part, part] + [_ANY] * (n_prev + n_h)
    args = [w, g, m, v] + (list(prev) if prev is not None else []) + ([] if hook is None else list(hook.inputs))
    out_shape = [jax.ShapeDtypeStruct((rows, C), F32)] * 3 + ([] if hook is None else list(hook.out_shapes))
    out_specs = [part] * 3 + ([] if hook is None else [_ANY] * len(hook.out_shapes))
    aliases = {4 + k: k for k in range(n_prev)}
    if hook is not None:
        aliases.update({4 + n_prev + hi: 3 + ho for hi, ho in hook.aliases.items()})
    res = pl.pallas_call(
        body, name=name, grid=(n // tm,), in_specs=in_specs, out_specs=out_specs, out_shape=out_shape,
        scratch_shapes=[] if hook is None else list(hook.scratch), input_output_aliases=aliases,
        compiler_params=pltpu.CompilerParams(dimension_semantics=("arbitrary",)),
    )(*args)
    if hook is not None:
        hook.done(res[3:])
    return res[:3]


def _position():
    return lax.axis_index("x"), lax.axis_index("y"), lax.axis_index("c")


def _other_chips(x, y):
    return [(1 - x, y), (x, 1 - y), (1 - x, 1 - y)]


_HBM = pl.BlockSpec(memory_space=pltpu.HBM)


def _gather_parts(half, lo, n):
    def copies(p_ref, out_ref, send_sems, recv_sems):
        x, y, c = _position()
        sibling = (x, y, 1 - c)
        chips = _other_chips(x, y)

        def slab(chip, h):
            return out_ref.at[2 * chip[0] + chip[1], pl.ds(h * half + lo, n), :]

        def copy(k, src, dst, to):
            return pltpu.make_async_remote_copy(src_ref=src, dst_ref=dst, send_sem=send_sems.at[k],
                                                recv_sem=recv_sems.at[k], device_id=to, device_id_type=MESH)

        first = [copy(j, p_ref.at[pl.ds(c * half + lo, n), :], slab((x, y), c), (*chip, c)) for j, chip in enumerate(chips)]
        passed = [copy(3 + j, slab(chip, c), slab(chip, c), sibling) for j, chip in enumerate(chips)]
        from_chips = [copy(j, slab(chip, c), slab(chip, c), (x, y, c)) for j, chip in enumerate(chips)]
        from_sibling = [copy(3 + j, slab(chip, 1 - c), slab(chip, 1 - c), (x, y, c)) for j, chip in enumerate(chips)]
        return first, passed, from_chips, from_sibling

    def start(ins, outs, scr):
        for cp in copies(ins[0], outs[0], *scr)[0]:
            cp.start()

    def finish(ins, outs, scr):
        first, passed, from_chips, from_sibling = copies(ins[0], outs[0], *scr)
        for j in range(3):
            from_chips[j].wait_recv()
            passed[j].start()
        for cp in from_sibling:
            cp.wait_recv()
        for cp in first + passed:
            cp.wait_send()

    return start, finish


def _rs_chip_parts(lo, n):
    def copies(h_ref, out_ref, send_sems, recv_sems):
        x, y, c = _position()
        return [pltpu.make_async_remote_copy(src_ref=h_ref.at[2 * chip[0] + chip[1], pl.ds(lo, n), :],
                                             dst_ref=out_ref.at[j, pl.ds(lo, n), :],
                                             send_sem=send_sems.at[j], recv_sem=recv_sems.at[j],
                                             device_id=(*chip, c), device_id_type=MESH)
                for j, chip in enumerate(_other_chips(x, y))]

    def start(ins, outs, scr):
        for cp in copies(ins[0], outs[0], *scr):
            cp.start()

    def finish(ins, outs, scr):
        for cp in copies(ins[0], outs[0], *scr):
            cp.wait()

    return start, finish


class _Stream:
    def __init__(self, src, buf, parts, nsem, units, name):
        self.src, self.buf, self.parts, self.nsem, self.name = src, buf, parts, nsem, name
        self.next, self.units = 0, units

    def _scratch(self):
        return [pltpu.SemaphoreType.DMA((self.nsem,)), pltpu.SemaphoreType.DMA((self.nsem,))]

    def _take(self, units):
        units = min(units, self.units - self.next)
        lo = self.next * 16
        self.next += units
        return lo, units * 16

    def _set(self, outs):
        self.buf = outs[0]

    def hook(self, units):
        lo, n = self._take(units)
        if n == 0:
            return None
        start, finish = self.parts(lo, n)
        return _Hook([self.src, self.buf], [jax.ShapeDtypeStruct(self.buf.shape, self.buf.dtype)], {1: 0},
                     self._scratch(), start, finish, self._set)

    def drain(self):
        lo, n = self._take(self.units)
        if n:
            start, finish = self.parts(lo, n)

            def body(s_ref, b_ref, o_ref, send_sems, recv_sems):
                args = ((s_ref, b_ref), (o_ref,), (send_sems, recv_sems))
                start(*args)
                finish(*args)

            self.buf = pl.pallas_call(
                body, name=self.name, in_specs=[_ANY, _ANY], out_specs=_ANY,
                out_shape=jax.ShapeDtypeStruct(self.buf.shape, self.buf.dtype),
                scratch_shapes=self._scratch(), input_output_aliases={1: 0},
            )(self.src, self.buf)
        return self.buf


def _rs_pair_parts(half, lo, n):
    def copy(g_ref, out_ref, send_sems, recv_sems):
        x, y, c = _position()
        return pltpu.make_async_remote_copy(
            src_ref=g_ref.at[pl.ds(0, 4), pl.ds((1 - c) * half + lo, n), :], dst_ref=out_ref.at[pl.ds(0, 4), pl.ds(lo, n), :],
            send_sem=send_sems.at[0], recv_sem=recv_sems.at[0], device_id=(x, y, 1 - c), device_id_type=MESH)

    def start(ins, outs, scr):
        copy(ins[0], outs[0], *scr).start()

    def finish(ins, outs, scr):
        copy(ins[0], outs[0], *scr).wait()

    return start, finish


def _rs_swap(r, name):
    Rh, C = r.shape

    def body(r_ref, out_ref, send_sem, recv_sem):
        x, y, c = _position()
        cp = pltpu.make_async_remote_copy(src_ref=r_ref, dst_ref=out_ref, send_sem=send_sem,
                                          recv_sem=recv_sem, device_id=(x, y, 1 - c), device_id_type=MESH)
        cp.start()
        cp.wait()

    return pl.pallas_call(
        body, name=name, in_specs=[_HBM], out_specs=_HBM,
        out_shape=jax.ShapeDtypeStruct((Rh, C), r.dtype),
        scratch_shapes=[pltpu.SemaphoreType.DMA, pltpu.SemaphoreType.DMA],
    )(r)


def _rs_add_pair(g, recv, cidx, name):
    _, R, C = g.shape
    Rh = R // 2
    tm = _pick(Rh, (400, 280, 200, 160, 80, 40, 16, 8))
    nt = Rh // tm

    def body(c_ref, g_ref, r_ref, o_ref):
        o_ref[...] = (g_ref[...].astype(F32) + r_ref[...].astype(F32)).astype(o_ref.dtype)

    return pl.pallas_call(
        body, name=name,
        grid_spec=pltpu.PrefetchScalarGridSpec(
            num_scalar_prefetch=1, grid=(4, nt),
            in_specs=[pl.BlockSpec((1, tm, C), lambda k, i, cr: (k, cr[0] * nt + i, 0)),
                      pl.BlockSpec((1, tm, C), lambda k, i, cr: (k, i, 0))],
            out_specs=pl.BlockSpec((1, tm, C), lambda k, i, cr: (k, i, 0))),
        out_shape=jax.ShapeDtypeStruct((4, Rh, C), BF16),
    )(cidx, g, recv)


def _rs_add_chips(h, recv, chip_idx, name):
    _, Rh, C = h.shape
    tm = _pick(Rh, (400, 280, 200, 160, 80, 40, 16, 8))

    def body(c_ref, h_ref, r_ref, o_ref):
        acc = h_ref[0].astype(F32)
        for j in range(3):
            acc = acc + r_ref[j].astype(F32)
        o_ref[...] = acc

    return pl.pallas_call(
        body, name=name,
        grid_spec=pltpu.PrefetchScalarGridSpec(
            num_scalar_prefetch=1, grid=(Rh // tm,),
            in_specs=[pl.BlockSpec((1, tm, C), lambda i, cr: (cr[0], i, 0)), pl.BlockSpec((3, tm, C), lambda i, cr: (0, i, 0))],
            out_specs=pl.BlockSpec((tm, C), lambda i, cr: (i, 0))),
        out_shape=jax.ShapeDtypeStruct((Rh, C), F32),
    )(chip_idx, h, recv)


def _all_reduce_small(vec, name):
    n, C = vec.shape

    def body(v_ref, out_ref, buf, send_sems, recv_sems):
        x, y, c = _position()

        def flip(k):
            return ((1 - x) if k & 4 else x, (1 - y) if k & 2 else y, (1 - c) if k & 1 else c)

        def idx(p):
            return 4 * p[0] + 2 * p[1] + p[2]

        me = idx((x, y, c))
        buf[me] = v_ref[...]
        cps = [pltpu.make_async_remote_copy(src_ref=v_ref, dst_ref=buf.at[me], send_sem=send_sems.at[k - 1],
                                            recv_sem=recv_sems.at[k - 1], device_id=flip(k), device_id_type=MESH)
               for k in range(1, 8)]
        for cp in cps:
            cp.start()
        for k in range(1, 8):
            pltpu.make_async_remote_copy(src_ref=v_ref, dst_ref=buf.at[idx(flip(k))], send_sem=send_sems.at[k - 1],
                                         recv_sem=recv_sems.at[k - 1], device_id=flip(k), device_id_type=MESH).wait_recv()
        for cp in cps:
            cp.wait_send()
        acc = buf[0]
        for s in range(1, 8):
            acc = acc + buf[s]
        out_ref[...] = acc

    return pl.pallas_call(
        body, name=name,
        in_specs=[pl.BlockSpec(memory_space=pltpu.VMEM)], out_specs=pl.BlockSpec(memory_space=pltpu.VMEM),
        out_shape=jax.ShapeDtypeStruct((n, C), F32),
        scratch_shapes=[pltpu.VMEM((8, n, C), F32), pltpu.SemaphoreType.DMA((7,)), pltpu.SemaphoreType.DMA((7,))],
    )(vec)


BIG = (("w_in", (D, IN_WIDTH // 4), "cols"), ("w_a", (GW, D // 4), "cols"), ("pool_w", (4, PG // 4, PG), "pool"),
       ("w_b", (D // 4, D), "rows"), ("w_c", (D // 4, D), "rows"), ("w_o", (D // 4, D), "rows"),
       ("ffn_w_up", (D, 2 * D_FF // 4), "cols"), ("ffn_w_down", (D_FF // 4, D), "rows"))
def _pack_rows(s):
    k = math.prod(s) // D
    return -(-k // 16) * 16, k


PACK_ROWS = sum(_pack_rows(s)[0] for _, s, _ in BIG)
PACK_PAD = -(-PACK_ROWS // 32) * 32


def _pad_rows(v, rows):
    pad = [(0, 0)] * v.ndim
    pad[-2] = (0, rows - v.shape[-2])
    return jnp.pad(v, pad) if rows > v.shape[-2] else v


def _pack_blocks(blocks, dtype):
    lead = blocks["w_in"].shape[:-2]
    flat = []
    for n, s, how in BIG:
        v = blocks[n].astype(dtype)
        if how == "cols":
            v = jnp.swapaxes(v, -1, -2)
        flat.append(_pad_rows(v.reshape(*lead, -1, D), _pack_rows(s)[0]))
    flat.append(jnp.zeros((*lead, PACK_PAD - PACK_ROWS, D), dtype))
    return jnp.concatenate(flat, axis=-2)


def _unpack_blocks(pack):
    out, r = {}, 0
    for n, s, how in BIG:
        rows, k = _pack_rows(s)
        v = pack[r:r + k, :]
        out[n] = v.reshape(s[1], s[0]).T if how == "cols" else v.reshape(s)
        r += rows
    return out


def _operands(allp):
    out, r = {}, 0
    for n, s, how in BIG:
        rows, k = _pack_rows(s)
        v = allp[:, r:r + k, :]
        if how == "cols":
            out[n] = v.reshape(4 * s[1], s[0])
        elif how == "rows":
            out[n] = v.reshape(4 * s[0], s[1])
        else:
            out[n] = v.reshape(4, *s).transpose(1, 0, 2, 3).reshape(4, PG, PG)
        r += rows
    return out


def _pack_operands(g, dtype):
    flat = []
    for n, s, how in BIG:
        v = g[n].astype(dtype)
        if how == "pool":
            v = v.reshape(4, 4, s[1], s[2]).transpose(1, 0, 2, 3)
        flat.append(_pad_rows(v.reshape(4, -1, D), _pack_rows(s)[0]))
    flat.append(jnp.zeros((4, PACK_PAD - PACK_ROWS, D), dtype))
    return jnp.concatenate(flat, axis=1)


def _layer_fwd(x, w, sm, bias, hk):
    pa, u = _mmf(None, w["in_a"], tb=True, pre=(_rms_core, [x], [sm["ln1_g"]]), name="in_a", tm=1024, hook=hk("in_a"))
    pb = _mm(u, w["in_b"], tb=True, out_dtype=BF16, name="in_b", hook=hk("in_b"))
    pc = _mm(u, w["in_c"], tb=True, out_dtype=BF16, name="in_c", hook=hk("in_c"))
    pd = _mm(u, w["in_d"], tb=True, out_dtype=BF16, name="in_d", hook=hk("in_d"))
    os_, ls_ = [], []
    for gi in range(3):
        o, l = _attn_fwd(pa, bias[gi], gi, "attn_fwd%d" % gi)
        os_.append(o)
        ls_.append(l)
    att = _mix_fwd(os_, ls_, "mix_fwd")
    ya = _mm(att, w["w_a"], tb=True, out_dtype=BF16, name="mm_wa")
    pool_o = _pool_fwd(pb, w["pool_w"], sm["pool_scale"], "pool_fwd")
    yb = _mm(pool_o, w["w_b"], out_dtype=BF16, name="mm_wb")
    xbc_c = _ssd_conv_fwd(pc, sm["ssd_conv_w"], sm["ssd_conv_b"], "ssd_conv_fwd")
    y_scan, states = _ssd_scan_fwd(xbc_c, pd, sm["ssd_dt_bias"], sm["ssd_a_log"], "ssd_scan_fwd")
    ssd_o = _ssd_post_fwd(y_scan, xbc_c, pc, sm["ssd_d"], sm["ssd_norm_w"], "ssd_post_fwd")
    yc = _mm(ssd_o, w["w_c"], out_dtype=BF16, name="mm_wc")
    merged = _gates_fwd(pd, sm["b_gate"], ya, yb, yc, "gates_fwd")
    x1 = _mm(merged, w["w_o"], add=x, name="mm_wo", hook=hk("mm_wo"))
    h, u2 = _mmf(None, w["ffn_w_up"], tb=True, pre=(_rms_core, [x1], [sm["ln2_g"]]), out_dtype=BF16, name="mm_up",
                 tm=1024, hook=hk("mm_up"))
    f = _ffn_act_fwd(h, sm["ffn_conv_w"], sm["ffn_conv_b"], "ffn_act_fwd")
    x2 = _mm(f, w["ffn_w_down"], add=x1, name="mm_down", hook=hk("mm_down"))
    saved = dict(x=x, u=u, pa=pa, pb=pb, pc=pc, pd=pd, os=os_, ls=ls_, att=att, ya=ya, yb=yb, yc=yc, pool_o=pool_o,
                 xbc_c=xbc_c, y_scan=y_scan, states=states, ssd_o=ssd_o, merged=merged, x1=x1, u2=u2, h=h, f=f)
    return x2, saved


def _layer_bwd(dx2, w, sm, bias, dbs, sv, hk):
    gw, gs = {}, {}
    S = dx2.shape[0]

    def gmm(a, b, name):
        return _mm(a, b, ta=True, out_dtype=BF16, name=name, hook=hk(name))

    df = _mm(dx2, w["ffn_w_down"], tb=True, out_dtype=BF16, name="d_f", hook=hk("d_f"))
    gw["ffn_w_down"] = gmm(sv["f"], dx2, "g_down")
    dha, dhv, gs["ffn_conv_w"], gs["ffn_conv_b"] = _ffn_act_bwd(sv["h"], sm["ffn_conv_w"], sm["ffn_conv_b"], df, "ffn_act_bwd")
    dx1, gs["ln2_g"] = _mmf([dha, dhv], [w["up_a"], w["up_v"]], name="d_u2_v", tm=256, hook=hk("d_u2_v"),
                            post=(_rms_post, [sv["x1"], dx2], [sm["ln2_g"]], [(D, F32)], [(1, D)]))
    gw["ffn_w_up"] = jnp.concatenate([gmm(dha, sv["u2"], "g_up_a"), gmm(dhv, sv["u2"], "g_up_v")], axis=0)
    dya, dyb, dyc, dgate, gs["b_gate"] = _mmf(
        dx1, w["w_o"], tb=True, name="d_merged", tm=256, hook=hk("d_merged"),
        post=(_gates_post, [sv["pd"], sv["ya"], sv["yb"], sv["yc"]], [sm["b_gate"]],
              [(D, BF16)] * 3 + [(3 * D, BF16)], [(1, 3 * D)]))
    gw["w_o"] = gmm(sv["merged"], dx1, "g_wo")
    dssd_o = _mm(dyc, w["w_c"], tb=True, name="d_ssd_o")
    gw["w_c"] = gmm(sv["ssd_o"], dyc, "g_wc")
    dy_scan, dxs_skip, dz, gs["ssd_d"], gs["ssd_norm_w"] = _ssd_post_bwd(
        sv["y_scan"], sv["xbc_c"], sv["pc"], sm["ssd_d"], sm["ssd_norm_w"], dssd_o, "ssd_post_bwd")
    dxbc_c, ddt, gs["ssd_dt_bias"], gs["ssd_a_log"] = _ssd_scan_bwd(
        sv["xbc_c"], sv["pd"], sm["ssd_dt_bias"], sm["ssd_a_log"], sv["states"], dy_scan, dxs_skip, "ssd_scan_bwd")
    dxbc, gs["ssd_conv_w"], gs["ssd_conv_b"] = _ssd_conv_bwd(sv["pc"], sm["ssd_conv_w"], sm["ssd_conv_b"], dxbc_c, "ssd_conv_bwd")
    dpool_o = _mm(dyb, w["w_b"], tb=True, name="d_pool_o")
    gw["w_b"] = gmm(sv["pool_o"], dyb, "g_wb")
    dpb, dpw, gs["pool_scale"] = _pool_bwd(sv["pb"], w["pool_w"], sm["pool_scale"], dpool_o, "pool_bwd")
    gw["pool_w"] = dpw.reshape(4, PG, PG)
    datt = _mm(dya, w["w_a"], name="d_att")
    gw["w_a"] = gmm(dya, sv["att"], "g_wa")
    dos, dls = _mix_bwd(sv["os"], sv["ls"], datt, "mix_bwd")
    dqkv = tuple(lax.empty((S, AW), F32) for _ in range(3))
    dbs = list(dbs)
    for gi in range(3):
        dqkv, dbs[gi] = _attn_bwd(sv["pa"], bias[gi], dos[gi], dls[gi], dbs[gi], dqkv, gi, "attn_bwd%d" % gi)
    u = sv["u"]
    pieces = [(dqkv[0], "wq"), (dqkv[1], "wk"), (dqkv[2], "wv"), (dpb, "in_b"), (dz, "wz"), (dxbc, "wxbc"),
              (ddt, "wdt"), (dgate, "wgate")]
    du = _mmf([dp for dp, _ in pieces[:4]], [w[key] for _, key in pieces[:4]], name="d_u_a", tm=256, hook=hk("d_u_a"))[0]
    dx, gs["ln1_g"] = _mmf([dp for dp, _ in pieces[4:]], [w[key] for _, key in pieces[4:]], add=du, name="d_u_wgate",
                           tm=256, hook=hk("d_u_wgate"),
                           post=(_rms_post, [sv["x"], dx1], [sm["ln1_g"]], [(D, F32)], [(1, D)]))
    g_in = []
    for dp, key in pieces:
        g = gmm(dp, u, "g_in_" + key)
        g_in.append(g[:SSD_HEADS] if key == "wdt" else g)
    gw["w_in"] = jnp.concatenate(g_in, axis=0)
    return dx, gw, gs, dbs


SMALL_LAYER = ("ln1_g", "b_gate", "pool_scale", "ssd_conv_w", "ssd_conv_b", "ssd_dt_bias", "ssd_a_log", "ssd_d",
               "ssd_norm_w", "ln2_g", "ffn_conv_w", "ffn_conv_b")


def _pad_lanes(v):
    return jnp.pad(v, (0, LANES - v.shape[0])).reshape(1, LANES)


def _layer_weights(ops):
    wt = ops["w_in"]
    o1, o2, o3 = SEC_A, SEC_A + SEC_B, SEC_A + SEC_B + SEC_C
    w = dict(ops)
    w["in_a"] = jnp.pad(wt[:o1], ((0, SEC_A_PAD - o1), (0, 0)))
    w["in_b"] = wt[o1:o2]
    w["in_c"] = wt[o2:o3]
    w["in_d"] = jnp.pad(wt[o3:], ((0, SEC_D - (IN_WIDTH - o3)), (0, 0)))
    w["wq"], w["wk"], w["wv"] = wt[:AW], wt[AW:2 * AW], wt[2 * AW:o1]
    w["wz"], w["wxbc"] = wt[o2:o2 + D], wt[o2 + D:o3]
    w["wdt"] = jnp.pad(wt[o3:o3 + SSD_HEADS], ((0, LANES - SSD_HEADS), (0, 0)))
    w["wgate"] = wt[o3 + SSD_HEADS:]
    w["up_a"], w["up_v"] = ops["ffn_w_up"][:D_FF], ops["ffn_w_up"][D_FF:]
    return w


def _layer_small(p, i):
    sm = {n: p[n][i] for n in SMALL_LAYER}
    out = {}
    for n, v in sm.items():
        if n in ("ssd_dt_bias", "ssd_a_log", "ssd_d"):
            out[n] = _pad_lanes(v)
        elif v.ndim == 1:
            out[n] = v.reshape(1, -1)
        else:
            out[n] = v
    return out


def _local_step(x, target, rel_bias, final_g, layer_full, small, fwd_hooks=None, bwd_hooks=None, after_bwd=None):
    nl = small["ln1_g"].shape[0]
    buckets = [_buckets(d).astype(jnp.int32) for d in DILATIONS]
    bias = [_bias_table(rel_bias, buckets[gi], gi, "bias_table%d" % gi) for gi in range(3)]
    no_hooks = lambda i: (lambda name: None)
    fwd_hooks = fwd_hooks or no_hooks
    bwd_hooks = bwd_hooks or no_hooks
    saved, ws, sms = [], [], []
    h = x
    for i in range(nl):
        w = _layer_weights(layer_full(i))
        sm = _layer_small(small, i)
        h, sv = _layer_fwd(h, w, sm, bias, fwd_hooks(i))
        saved.append(sv)
        ws.append(w)
        sms.append(sm)
    dh, dfinal, loss = _final_loss(h, target, final_g.reshape(1, D))
    gws, gss = [None] * nl, [None] * nl
    dbs = [jnp.zeros((6, WIN, 2 * WIN), F32)] * 3
    for i in reversed(range(nl)):
        dh, gws[i], gss[i], dbs = _layer_bwd(dh, ws[i], sms[i], bias, dbs, saved[i], bwd_hooks(i))
        if after_bwd is not None:
            after_bwd(i, gws[i])
    drel = []
    for gi in range(3):
        onehot = jnp.pad(jax.nn.one_hot(buckets[gi].reshape(-1), REL_BUCKETS, dtype=BF16), ((0, 0), (0, LANES - REL_BUCKETS)))
        drel.append(_mm(dbs[gi].reshape(6, WIN * 2 * WIN), onehot, name="g_relb"))
    return loss, dh, gws, gss, dfinal, jnp.concatenate(drel, axis=0)


WEIGHTS = ("rel_bias", "ln1_g", "w_in", "b_gate", "w_a", "pool_w", "pool_scale", "w_b", "ssd_conv_w", "ssd_conv_b",
           "ssd_dt_bias", "ssd_a_log", "ssd_d", "ssd_norm_w", "w_c", "w_o", "ln2_g", "ffn_w_up", "ffn_conv_w",
           "ffn_conv_b", "ffn_w_down", "final_g")
BIG_NAMES = tuple(n for n, _, _ in BIG)
SHARDED_SMALL = {"ssd_conv_w": XBC // 4, "ffn_conv_w": 2 * D_FF // 4}


def _to_rows(flat):
    n = flat.shape[0]
    rows = -(-n // LANES)
    rows = -(-rows // 8) * 8
    return jnp.pad(flat, (0, rows * LANES - n)).reshape(rows, LANES)


def _flatten(tree, names):
    return jnp.concatenate([tree[n].reshape(-1) for n in names])


def _unflatten(flat, shapes, names):
    out, o = {}, 0
    for n in names:
        k = math.prod(shapes[n])
        out[n] = flat[o:o + k].reshape(shapes[n])
        o += k
    return out


def kernel(x, rel_bias, ln1_g, w_in, b_gate, w_a, pool_w, pool_scale, w_b, ssd_conv_w, ssd_conv_b, ssd_dt_bias, ssd_a_log, ssd_d, ssd_norm_w, w_c, w_o, ln2_g, ffn_w_up, ffn_conv_w, ffn_conv_b, ffn_w_down, final_g, loss_target, m_rel_bias, m_ln1_g, m_w_in, m_b_gate, m_w_a, m_pool_w, m_pool_scale, m_w_b, m_ssd_conv_w, m_ssd_conv_b, m_ssd_dt_bias, m_ssd_a_log, m_ssd_d, m_ssd_norm_w, m_w_c, m_w_o, m_ln2_g, m_ffn_w_up, m_ffn_conv_w, m_ffn_conv_b, m_ffn_w_down, m_final_g, v_rel_bias, v_ln1_g, v_w_in, v_b_gate, v_w_a, v_pool_w, v_pool_scale, v_w_b, v_ssd_conv_w, v_ssd_conv_b, v_ssd_dt_bias, v_ssd_a_log, v_ssd_d, v_ssd_norm_w, v_w_c, v_w_o, v_ln2_g, v_ffn_w_up, v_ffn_conv_w, v_ffn_conv_b, v_ffn_w_down, v_final_g):
    W = dict(rel_bias=rel_bias, ln1_g=ln1_g, w_in=w_in, b_gate=b_gate, w_a=w_a, pool_w=pool_w, pool_scale=pool_scale,
             w_b=w_b, ssd_conv_w=ssd_conv_w, ssd_conv_b=ssd_conv_b, ssd_dt_bias=ssd_dt_bias, ssd_a_log=ssd_a_log,
             ssd_d=ssd_d, ssd_norm_w=ssd_norm_w, w_c=w_c, w_o=w_o, ln2_g=ln2_g, ffn_w_up=ffn_w_up,
             ffn_conv_w=ffn_conv_w, ffn_conv_b=ffn_conv_b, ffn_w_down=ffn_w_down, final_g=final_g)
    M = dict(rel_bias=m_rel_bias, ln1_g=m_ln1_g, w_in=m_w_in, b_gate=m_b_gate, w_a=m_w_a, pool_w=m_pool_w,
             pool_scale=m_pool_scale, w_b=m_w_b, ssd_conv_w=m_ssd_conv_w, ssd_conv_b=m_ssd_conv_b,
             ssd_dt_bias=m_ssd_dt_bias, ssd_a_log=m_ssd_a_log, ssd_d=m_ssd_d, ssd_norm_w=m_ssd_norm_w, w_c=m_w_c,
             w_o=m_w_o, ln2_g=m_ln2_g, ffn_w_up=m_ffn_w_up, ffn_conv_w=m_ffn_conv_w, ffn_conv_b=m_ffn_conv_b,
             ffn_w_down=m_ffn_w_down, final_g=m_final_g)
    V = dict(rel_bias=v_rel_bias, ln1_g=v_ln1_g, w_in=v_w_in, b_gate=v_b_gate, w_a=v_w_a, pool_w=v_pool_w,
             pool_scale=v_pool_scale, w_b=v_w_b, ssd_conv_w=v_ssd_conv_w, ssd_conv_b=v_ssd_conv_b,
             ssd_dt_bias=v_ssd_dt_bias, ssd_a_log=v_ssd_a_log, ssd_d=v_ssd_d, ssd_norm_w=v_ssd_norm_w, w_c=v_w_c,
             w_o=v_w_o, ln2_g=v_ln2_g, ffn_w_up=v_ffn_w_up, ffn_conv_w=v_ffn_conv_w, ffn_conv_b=v_ffn_conv_b,
             ffn_w_down=v_ffn_w_down, final_g=v_final_g)
    nl = ln1_g.shape[0]
    px, py, pc_ = _position()
    chip = 2 * px + py
    cidx = jnp.reshape(pc_, (1,)).astype(jnp.int32)
    chip_idx = jnp.reshape(chip, (1,)).astype(jnp.int32)

    placed = {}
    for n, cs in SHARDED_SMALL.items():
        full = jnp.zeros(W[n].shape[:-1] + (4 * cs,), F32)
        full = lax.dynamic_update_slice(full, W[n], (0, 0, chip * cs))
        placed[n] = jnp.where(pc_ == 0, full, 0.0)
    names_sh = tuple(SHARDED_SMALL)
    shapes_sh = {n: placed[n].shape for n in names_sh}
    got = _all_reduce_small(_to_rows(_flatten(placed, names_sh)), "gather_small")
    small = {n: W[n] for n in SMALL_LAYER}
    small.update(_unflatten(got.reshape(-1), shapes_sh, names_sh))

    packs = _pack_blocks({n: W[n] for n in BIG_NAMES}, BF16)

    half = PACK_PAD // 2
    units = half // 16

    def share(weights, total):
        tot = sum(weights.values())
        return {n: math.ceil(total * v / tot) for n, v in weights.items()}

    gathers = {}

    def gather(i):
        if i not in gathers:
            buf = lax.dynamic_update_slice(lax.empty((4, PACK_PAD, D), BF16), packs[i][None], (chip, 0, 0))
            gathers[i] = _Stream(packs[i], buf, functools.partial(_gather_parts, half), 6, units, "gather_w")
        return gathers[i]

    def layer_full(i):
        return _operands(gather(i).drain())

    fwd_share = share(dict(in_a=89, in_b=26, in_c=57, in_d=66, mm_wo=28, mm_up=120, mm_down=46), units)

    def fwd_hooks(i):
        if i + 1 >= nl:
            return lambda name: None
        return lambda name: gather(i + 1).hook(fwd_share[name]) if name in fwd_share else None

    exchanges = {}
    bwd_share = share(dict(g_down=46, d_u2_v=80, g_up_a=40, g_up_v=40, d_merged=55, g_wo=20, d_u_a=60,
                           d_u_wgate=90, g_in_wgate=40), units)

    class Exchange:
        def __init__(self, g):
            self.g = g
            self.pair = _Stream(g, lax.empty((4, half, D), BF16), functools.partial(_rs_pair_parts, half), 1, units, "rs_pair")
            self.hsum = self.chips = None

        def to_chips(self):
            if self.chips is None:
                self.hsum = _rs_add_pair(self.g, self.pair.drain(), cidx, "rs_add_pair")
                self.chips = _Stream(self.hsum, lax.empty((3, half, D), BF16), _rs_chip_parts, 3, units, "rs_chips")
            return self.chips

    def after_bwd(i, gw):
        exchanges[i] = Exchange(_pack_operands(gw, BF16))

    def bwd_hooks(i):
        if i + 1 >= nl:
            return lambda name: None

        def hk(name):
            if name == "d_f":
                return exchanges[i + 1].pair.hook(units)
            return exchanges[i + 1].to_chips().hook(bwd_share[name]) if name in bwd_share else None

        return hk

    loss, dx, gws, gss, dfinal, drel = _local_step(x[0], loss_target[0], rel_bias, final_g, layer_full, small,
                                                   fwd_hooks, bwd_hooks, after_bwd)

    def reduced(i):
        recv3 = exchanges[i].to_chips().drain()
        r = _rs_add_chips(exchanges[i].hsum, recv3, chip_idx, "rs_add_chips")
        other = _rs_swap(r, "rs_swap")
        both = jnp.concatenate([jnp.where(pc_ == 0, r, other), jnp.where(pc_ == 0, other, r)], axis=0)
        return _unpack_blocks(both)

    red = [None] + [reduced(i) for i in range(1, nl)]
    delta, new_m, new_v = {}, {}, {}
    later = {}
    adam_share = share({n: math.prod(W[n].shape) for n in BIG_NAMES}, units)
    for n in BIG_NAMES if nl > 1 else ():
        rows_l = math.prod(W[n].shape[1:-1])
        r2 = lambda a: a.reshape(-1, W[n].shape[-1])
        g = jnp.concatenate([r2(red[i][n]) for i in range(1, nl)], axis=0)
        later[n] = _adamw_part(r2(W[n]), g, r2(M[n]), r2(V[n]), rows_l, None, "adamw_" + n,
                               hook=exchanges[0].to_chips().hook(adam_share[n]))
    red[0] = reduced(0)
    grads = {}
    for n in BIG_NAMES:
        shp = W[n].shape
        rows_l = math.prod(shp[1:-1])
        r2 = lambda a: a.reshape(-1, shp[-1])
        res = _adamw_part(r2(W[n]), r2(red[0][n]), r2(M[n]), r2(V[n]), 0, later.get(n), "adamw0_" + n)
        delta[n], new_m[n], new_v[n] = [a.reshape(shp) for a in res]
        grads[n] = jnp.stack([red[i][n] for i in range(nl)], axis=0)

    sg = {}
    for n in SMALL_LAYER:
        sg[n] = jnp.stack([gss[i][n] for i in range(nl)], axis=0)
    for n in ("ssd_dt_bias", "ssd_a_log", "ssd_d"):
        sg[n] = sg[n][:, 0, :SSD_HEADS]
    sg["rel_bias"] = drel[:, :REL_BUCKETS].T
    sg["final_g"] = dfinal.reshape(D)
    sg["loss"] = loss[0, :1]
    names_sg = tuple(sg)
    shapes_sg = {n: ((nl,) + W[n].shape[1:] if n in SMALL_LAYER and n not in SHARDED_SMALL else
                     (placed[n].shape if n in SHARDED_SMALL else sg[n].shape)) for n in names_sg}
    for n in names_sg:
        sg[n] = sg[n].reshape(shapes_sg[n])
    tot = _all_reduce_small(_to_rows(_flatten(sg, names_sg)), "allreduce_small")
    tot = _unflatten(tot.reshape(-1), shapes_sg, names_sg)
    loss_out = tot.pop("loss").reshape(())
    for n, cs in SHARDED_SMALL.items():
        tot[n] = lax.dynamic_slice(tot[n], (0, 0, chip * cs), tot[n].shape[:-1] + (cs,))
    grads.update(tot)

    names_s = tuple(n for n in WEIGHTS if n not in BIG_NAMES)
    shapes_s = {n: W[n].shape for n in names_s}
    pk = lambda t: _to_rows(_flatten(t, names_s))
    dl, m2, v2 = _adamw(pk(W), pk(grads), pk(M), pk(V), "adamw_small")
    delta.update(_unflatten(dl.reshape(-1), shapes_s, names_s))
    new_m.update(_unflatten(m2.reshape(-1), shapes_s, names_s))
    new_v.update(_unflatten(v2.reshape(-1), shapes_s, names_s))

    return (loss_out, dx[None], *[grads[n] for n in WEIGHTS], *[delta[n] for n in WEIGHTS],
            *[new_m[n] for n in WEIGHTS], *[new_v[n] for n in WEIGHTS])
```

```python
import functools
import math

import jax
import jax.numpy as jnp
from jax import lax
from jax.experimental import pallas as pl
from jax.experimental.pallas import tpu as pltpu

F32 = jnp.float32
BF16 = jnp.bfloat16
MESH = pl.DeviceIdType.MESH

D = 1024
HD = 64
GW = 384
AW = 3 * GW
WIN = 128
DILATIONS = (1, 4, 16)
REL_BUCKETS = 32
REL_MAX_DISTANCE = 2048
POOL_WINDOWS = (2, 4, 8, 16)
PG = 256
SSD_HEADS = 16
SSD_N = 128
SSD_CHUNK = 128
XBC = 1536
D_FF = 2816
EPS = 1e-6
NEG = -1e30
HALO = 16
LANES = 128

SEC_A = 3 * AW
SEC_B = D
SEC_C = D + XBC
SEC_D = 3328
SEC_A_PAD = 3584
IN_WIDTH = SEC_A + SEC_B + SEC_C + 16 + 3 * D

ADAM_LR = 0.001
ADAM_B1 = 0.9
ADAM_B2 = 0.999
ADAM_EPS = 1e-08
ADAM_WD = 0.01
ADAM_STEP = 10
ADAM_TILE = 256 * 1024
MM_VMEM_BYTES = 40 * 1024 * 1024
MM_MAX_OUT_TILE = 1024 * 1024
HBM_BYTES_PER_US = 2.0e6
STEP_US = 0.35
MXU_WIDTH = 256
MXU_FLOPS_PER_US = 0.65e6


_ANY = pl.BlockSpec(memory_space=pl.ANY)


def _pick(d, cands):
    for t in cands:
        if d % t == 0:
            return t
    return d


def _iota(shape, dim):
    return lax.broadcasted_iota(jnp.int32, shape, dim)


def _dg(a, b, ca, cb):
    return lax.dot_general(a.astype(BF16), b.astype(BF16), (((ca,), (cb,)), ((), ())),
                           preferred_element_type=F32)


@jax.custom_vjp
def _bdot_nn(a, b):
    return _dg(a, b, 1, 0)


def _nn_fwd(a, b):
    return _dg(a, b, 1, 0), (a, b)


def _nn_bwd(res, g):
    a, b = res
    return _dg(g, b, 1, 1), _dg(a, g, 0, 0)


_bdot_nn.defvjp(_nn_fwd, _nn_bwd)


@jax.custom_vjp
def _bdot_nt(a, b):
    return _dg(a, b, 1, 1)


def _nt_fwd(a, b):
    return _dg(a, b, 1, 1), (a, b)


def _nt_bwd(res, g):
    a, b = res
    return _dg(g, b, 1, 0), _dg(g, a, 0, 0)


_bdot_nt.defvjp(_nt_fwd, _nt_bwd)


@jax.custom_vjp
def _bdot_tn(a, b):
    return _dg(a, b, 0, 0)


def _tn_fwd(a, b):
    return _dg(a, b, 0, 0), (a, b)


def _tn_bwd(res, g):
    a, b = res
    return _dg(b, g, 1, 1), _dg(a, g, 1, 0)


_bdot_tn.defvjp(_tn_fwd, _tn_bwd)


def _fdot(a, b):
    return jnp.dot(a, b, preferred_element_type=F32, precision=lax.Precision.HIGHEST)


def _sigmoid(x):
    return 0.5 * jnp.tanh(0.5 * x) + 0.5


def _silu(x):
    return x * _sigmoid(x)


def _softplus(x):
    return jnp.maximum(x, 0.0) + jnp.log(1.0 + jnp.exp(-jnp.abs(x)))


def _lane_pick(m, h):
    return jnp.sum(jnp.where(_iota(m.shape, 1) == h, m, 0.0), axis=1, keepdims=True)


def _row_pick(m, h):
    return jnp.sum(jnp.where(_iota(m.shape, 0) == h, m, 0.0), axis=0, keepdims=True)


def _stack_rows(rows, n):
    c = rows[0].shape[1]
    r = _iota((n, c), 0)
    out = jnp.zeros((n, c), F32)
    for k, v in enumerate(rows):
        out = out + jnp.where(r == k, v, 0.0)
    return out


def _mm(a, b, *, ta=False, tb=False, add=None, out_dtype=F32, name, hook=None):
    if ta:
        K, M = a.shape
    else:
        M, K = a.shape
    if tb:
        N, Kb = b.shape
    else:
        Kb, N = b.shape
    assert K == Kb, (a.shape, b.shape, ta, tb)
    tm, tn, tk = _mm_tiles(M, N, K, a.dtype.itemsize, b.dtype.itemsize, jnp.dtype(out_dtype).itemsize,
                           0 if add is None else add.dtype.itemsize)
    ni, nj, nk = M // tm, N // tn, K // tk
    ca = 0 if ta else 1
    cb = 1 if tb else 0
    n_in = 2 if add is None else 3
    n_hin = 0 if hook is None else len(hook.inputs)
    n_hout = 0 if hook is None else len(hook.out_shapes)

    def body(*refs):
        a_ref, b_ref = refs[:2]
        add_ref = None if add is None else refs[2]
        o_ref = refs[n_in + n_hin]
        scr = refs[n_in + n_hin + 1 + n_hout:]
        acc_ref = scr[0] if nk > 1 else None
        hargs = (refs[n_in:n_in + n_hin], refs[n_in + n_hin + 1:n_in + n_hin + 1 + n_hout], scr[1 if nk > 1 else 0:])
        i, j, k = pl.program_id(0), pl.program_id(1), pl.program_id(2)
        if hook is not None:
            @pl.when((i == 0) & (j == 0) & (k == 0))
            def _():
                hook.start(*hargs)

        part = _dg(a_ref[...], b_ref[...], ca, cb)

        def finish(r):
            if add_ref is not None:
                r = r + add_ref[...].astype(F32)
            o_ref[...] = r.astype(o_ref.dtype)

        if nk == 1:
            finish(part)
        else:
            @pl.when(k == 0)
            def _():
                acc_ref[...] = part

            @pl.when((k > 0) & (k < nk - 1))
            def _():
                acc_ref[...] += part

            @pl.when(k == nk - 1)
            def _():
                finish(acc_ref[...] + part)

        if hook is not None:
            @pl.when((i == ni - 1) & (j == nj - 1) & (k == nk - 1))
            def _():
                hook.finish(*hargs)

    a_spec = pl.BlockSpec((tk, tm), lambda i, j, k: (k, i)) if ta else pl.BlockSpec((tm, tk), lambda i, j, k: (i, k))
    b_spec = pl.BlockSpec((tn, tk), lambda i, j, k: (j, k)) if tb else pl.BlockSpec((tk, tn), lambda i, j, k: (k, j))
    in_specs = [a_spec, b_spec]
    args = [a, b]
    if add is not None:
        in_specs.append(pl.BlockSpec((tm, tn), lambda i, j, k: (i, j)))
        args.append(add)
    out_specs = [pl.BlockSpec((tm, tn), lambda i, j, k: (i, j))]
    out_shape = [jax.ShapeDtypeStruct((M, N), out_dtype)]
    scratch = [pltpu.VMEM((tm, tn), F32)] if nk > 1 else []
    aliases = {}
    if hook is not None:
        in_specs += [_ANY] * n_hin
        args += list(hook.inputs)
        out_specs += [_ANY] * n_hout
        out_shape += list(hook.out_shapes)
        scratch += list(hook.scratch)
        aliases = {n_in + hi: 1 + ho for hi, ho in hook.aliases.items()}
    sem = ("parallel", "parallel", "arbitrary") if hook is None else ("arbitrary",) * 3
    res = pl.pallas_call(
        body, name=name, grid=(ni, nj, nk), in_specs=in_specs, out_specs=out_specs, out_shape=out_shape,
        scratch_shapes=scratch, input_output_aliases=aliases,
        compiler_params=pltpu.CompilerParams(dimension_semantics=sem),
    )(*args)
    if hook is not None:
        hook.done(res[1:])
    return res[0]


def _wide(v):
    return v.astype(F32) if v.dtype == BF16 else v


def _mmf(a, b, *, tb=False, add=None, pre=None, post=None, out_dtype=F32, name, tm, hook=None):
    a_list = list(a) if isinstance(a, (list, tuple)) else [a]
    b_list = list(b) if isinstance(b, (list, tuple)) else [b]
    assert len(a_list) == len(b_list) and (len(b_list) == 1 or not (tb or pre))
    b = b_list[0]
    if tb:
        N, K = b.shape
    else:
        K, N = b.shape
    M = pre[1][0].shape[0] if pre else a_list[0].shape[0]
    tn = N if post else _pick(N, (512, 256, LANES))
    ni, nj = M // tm, N // tn
    cb = 1 if tb else 0
    pre_fn, pre_rows, pre_consts = pre if pre else (None, [], [])
    post_fn, post_rows, post_consts, post_outs, post_accs = post if post else (None, [], [], [], [])
    hook_in = [] if hook is None else list(hook.inputs)
    hook_out = [] if hook is None else list(hook.out_shapes)

    def row_spec(arr):
        return pl.BlockSpec((tm, arr.shape[1]), lambda i, j: (i, 0))

    def const_spec(arr):
        return pl.BlockSpec(arr.shape, lambda i, j, nd=arr.ndim: (0,) * nd)

    args, in_specs = [], []
    for arr in (a_list if not pre else pre_rows):
        args.append(arr)
        in_specs.append(row_spec(arr))
    for arr in pre_consts:
        args.append(arr)
        in_specs.append(const_spec(arr))
    for arr in b_list:
        args.append(arr)
        in_specs.append(pl.BlockSpec((tn, K), lambda i, j: (j, 0)) if tb else
                        pl.BlockSpec((arr.shape[0], tn), lambda i, j: (0, j)))
    if add is not None:
        args.append(add)
        in_specs.append(pl.BlockSpec((tm, tn), lambda i, j: (i, j)))
    for arr in post_rows:
        args.append(arr)
        in_specs.append(row_spec(arr))
    for arr in post_consts:
        args.append(arr)
        in_specs.append(const_spec(arr))
    n_main = len(args)
    args += hook_in
    in_specs += [_ANY] * len(hook_in)

    out_shape, out_specs = [], []
    if post:
        for c, dt in post_outs:
            out_shape.append(jax.ShapeDtypeStruct((M, c), dt))
            out_specs.append(pl.BlockSpec((tm, c), lambda i, j: (i, 0)))
        for r, c in post_accs:
            out_shape.append(jax.ShapeDtypeStruct((r, c), F32))
            out_specs.append(pl.BlockSpec((r, c), lambda i, j: (0, 0)))
    else:
        out_shape.append(jax.ShapeDtypeStruct((M, N), out_dtype))
        out_specs.append(pl.BlockSpec((tm, tn), lambda i, j: (i, j)))
    if pre:
        out_shape.append(jax.ShapeDtypeStruct((M, K), BF16))
        out_specs.append(pl.BlockSpec((tm, K), lambda i, j: (i, 0)))
    n_out = len(out_shape)
    out_shape += hook_out
    out_specs += [_ANY] * len(hook_out)
    scratch = ([pltpu.VMEM((tm, K), BF16)] if pre else []) + ([] if hook is None else list(hook.scratch))
    aliases = {} if hook is None else {n_main + hi: n_out + ho for hi, ho in hook.aliases.items()}

    def body(*refs):
        ins, outs, scr = refs[:n_main], refs[len(args):len(args) + n_out], refs[len(args) + len(out_shape):]
        hargs = (refs[n_main:len(args)], refs[len(args) + n_out:len(args) + len(out_shape)], scr[1 if pre else 0:])
        i, j = pl.program_id(0), pl.program_id(1)
        if hook is not None:
            @pl.when((i == 0) & (j == 0))
            def _():
                hook.start(*hargs)

        it = iter(ins)
        if pre:
            rows_ = [next(it) for _ in pre_rows]
            consts_ = [next(it) for _ in pre_consts]

            @pl.when(j == 0)
            def _():
                av = pre_fn(*[_wide(r[...]) for r in rows_], *[_wide(r[...]) for r in consts_]).astype(BF16)
                scr[0][...] = av
                outs[-1][...] = av

            ats = [scr[0][...]]
        else:
            ats = [next(it)[...] for _ in a_list]
        p = None
        for at in ats:
            part = _dg(at, next(it)[...], 1, cb)
            p = part if p is None else p + part
        if add is not None:
            p = p + next(it)[...].astype(F32)
        if post:
            rows_ = [next(it) for _ in post_rows]
            consts_ = [next(it) for _ in post_consts]
            res = post_fn(p, *[_wide(r[...]) for r in rows_], *[_wide(r[...]) for r in consts_])
            for r, v in zip(outs[:len(post_outs)], res[:len(post_outs)]):
                r[...] = v.astype(r.dtype)
            for r, v in zip(outs[len(post_outs):], res[len(post_outs):]):
                @pl.when(i == 0)
                def _(r=r, v=v):
                    r[...] = v

                @pl.when(i > 0)
                def _(r=r, v=v):
                    r[...] += v
        else:
            outs[0][...] = p.astype(outs[0].dtype)
        if hook is not None:
            @pl.when((i == ni - 1) & (j == nj - 1))
            def _():
                hook.finish(*hargs)

    res = pl.pallas_call(
        body, name=name, grid=(ni, nj), in_specs=in_specs, out_specs=out_specs, out_shape=out_shape,
        scratch_shapes=scratch, input_output_aliases=aliases,
        compiler_params=pltpu.CompilerParams(dimension_semantics=("arbitrary", "arbitrary")),
    )(*args)
    if hook is not None:
        hook.done(res[n_out:])
    return res[:n_out]


def _mm_tiles(M, N, K, sa, sb, so, sadd):
    def tiles(d):
        return [t for t in range(LANES, min(d, 2048) + 1, LANES) if d % t == 0] or [d]

    best = None
    for tk in [K] + [t for t in tiles(K) if t < K]:
        for tm in tiles(M):
            for tn in tiles(N):
                vmem = 2 * (tm * tk * sa + tk * tn * sb + tm * tn * (so + sadd)) + (tm * tn * 4 if tk < K else 0)
                if vmem > MM_VMEM_BYTES or tm * tn > MM_MAX_OUT_TILE:
                    continue
                a_reads = 1 if tk == K else N // tn
                traffic = M * K * sa * a_reads + K * N * sb * (M // tm) + M * N * (so + sadd)
                steps = (M // tm) * (N // tn) * (K // tk)
                width = -(-tn // MXU_WIDTH) * MXU_WIDTH
                mxu = 2.0 * M * K * N * (width / tn) / MXU_FLOPS_PER_US
                edge = tm * tk * sa + tk * tn * sb + tm * tn * (so + sadd)
                cost = max(traffic / HBM_BYTES_PER_US, mxu) + steps * STEP_US + edge / HBM_BYTES_PER_US
                if best is None or cost < best[0]:
                    best = (cost, tm, tn, tk)
    assert best is not None, (M, N, K)
    return best[1:]


class _Hook:
    def __init__(self, inputs, out_shapes, aliases, scratch, start, finish, done):
        self.inputs, self.out_shapes, self.aliases, self.scratch = inputs, out_shapes, aliases, scratch
        self.start, self.finish, self.done = start, finish, done


class _Ctx:
    def __init__(self, first, last, row0, rows):
        self.first, self.last, self.row0, self.rows = first, last, row0, rows


def _rows(name, fn, ins, outs, accs=(), *, tm, nrows, ncol=1):
    nt = nrows // tm
    hb = tm // HALO
    nh = nrows // HALO
    ins = [(kind, arr, arr.shape[1] if kind == "row" and cw is None else cw, base) for kind, arr, cw, base in ins]
    in_specs, args = [], []
    for kind, arr, cw, base in ins:
        if kind == "row":
            in_specs.append(pl.BlockSpec((tm, cw), lambda j, i, base=base: (i, base + j)))
        elif kind == "prev":
            in_specs.append(pl.BlockSpec((HALO, cw), lambda j, i, base=base: (jnp.maximum(i * hb - 1, 0), base + j)))
        elif kind == "next":
            in_specs.append(pl.BlockSpec((HALO, cw), lambda j, i, base=base: (jnp.minimum((i + 1) * hb, nh - 1), base + j)))
        elif kind in ("const", "raw"):
            in_specs.append(pl.BlockSpec(arr.shape, lambda j, i, nd=arr.ndim: (0,) * nd))
        elif kind == "ccol":
            in_specs.append(pl.BlockSpec((arr.shape[0], cw), lambda j, i, base=base: (0, base + j)))
        else:
            raise ValueError(kind)
        args.append(arr)
    out_specs, out_shape = [], []
    for ctot, cw, base, dt in outs:
        out_specs.append(pl.BlockSpec((tm, cw), lambda j, i, base=base: (i, base + j)))
        out_shape.append(jax.ShapeDtypeStruct((nrows, ctot), dt))
    for r, ctot, cw in accs:
        out_specs.append(pl.BlockSpec((r, cw), lambda j, i: (0, j)))
        out_shape.append(jax.ShapeDtypeStruct((r, ctot), F32))
    n_in, n_out = len(ins), len(outs)

    def body(*refs):
        i = pl.program_id(1)
        in_refs, out_refs, acc_refs = refs[:n_in], refs[n_in:n_in + n_out], refs[n_in + n_out:]
        if acc_refs:
            @pl.when(i == 0)
            def _():
                for r in acc_refs:
                    r[...] = jnp.zeros_like(r)

        vals = [r[...] if s[0] == "raw" else _wide(r[...]) for r, s in zip(in_refs, ins)]
        res = fn(_Ctx(i == 0, i == nt - 1, i * tm, tm), *vals)
        for r, v in zip(out_refs, res[:n_out]):
            r[...] = v.astype(r.dtype)
        for r, v in zip(acc_refs, res[n_out:]):
            r[...] += v

    res = pl.pallas_call(
        body, name=name, grid=(ncol, nt), in_specs=in_specs, out_specs=out_specs, out_shape=out_shape,
        compiler_params=pltpu.CompilerParams(dimension_semantics=("arbitrary", "arbitrary")),
    )(*args)
    return res


def _shift_down(xcat, k):
    return xcat if k == 0 else pltpu.roll(xcat, k, 0)


def _shift_up(xcat, k):
    return xcat if k == 0 else pltpu.roll(xcat, xcat.shape[0] - k, 0)


def _with_prev(ctx, halo, x):
    return jnp.concatenate([jnp.where(ctx.first, 0.0, halo), x], axis=0)


def _with_next(ctx, x, halo):
    return jnp.concatenate([x, jnp.where(ctx.last, 0.0, halo)], axis=0)


def _rms_core(x, g):
    r = lax.rsqrt(jnp.mean(x * x, axis=-1, keepdims=True) + EPS)
    return x * r * g


def _rms_post(du, xv, drv, gv):
    _, vjp = jax.vjp(_rms_core, xv, gv)
    dx, dg = vjp(du)
    return [drv + dx, dg]


def _final_loss(x, target, g):
    S = x.shape[0]

    def fn(ctx, xv, tv, gv):
        def f(xx, gg):
            err = _rms_core(xx, gg) - tv
            return 0.5 * jnp.sum(err * err) / D

        loss, vjp = jax.vjp(f, xv, gv)
        dx, dg = vjp(jnp.ones((), F32))
        return [dx, dg, jnp.zeros((1, LANES), F32) + loss]

    return _rows("final_loss", fn, [("row", x, None, 0), ("row", target, None, 0), ("const", g, None, 0)],
                 [(D, D, 0, F32)], [(1, D, D), (1, LANES, LANES)], tm=256, nrows=S)


def _attn_valid(n):
    qi = _iota((WIN, 2 * WIN), 0)
    kk = _iota((WIN, 2 * WIN), 1)
    rel = qi + WIN - kk
    return (rel >= 0) & (rel <= WIN) & ((kk >= WIN) | (n > 0))


def _attn_block(q, kp, kc, vp, vc, b0, b1, valid):
    k = jnp.concatenate([kp, kc], axis=0)
    v = jnp.concatenate([vp, vc], axis=0)
    lo = _iota((WIN, LANES), 1) < HD
    scale = 1.0 / math.sqrt(HD)
    os_, ls_ = [], []
    for hh, b in ((0, b0), (1, b1)):
        qm = jnp.where(lo if hh == 0 else ~lo, q, 0.0)
        s = _bdot_nt(qm, k) * scale + b
        s = jnp.where(valid, s, NEG)
        m = lax.stop_gradient(jnp.max(s, axis=1, keepdims=True))
        p = jnp.exp(s - m)
        l = jnp.sum(p, axis=1, keepdims=True)
        os_.append(_bdot_nn(p, v) / l)
        ls_.append(m + jnp.log(l))
    return jnp.where(lo, os_[0], os_[1]), jnp.where(lo, ls_[0], ls_[1])


def _residue_rows(r, d):
    return pl.ds(0, WIN) if d == 1 else pl.ds(r, WIN, stride=d)


def _for_residues(d, fn):
    if d == 1:
        fn(0, 0)
    else:
        lax.fori_loop(0, d, fn, 0, unroll=min(d, 8))


def _pairs_per_step(d):
    return 3 if d == 1 else 1


def _bias_table(rel_bias, bucket, gi, name):
    def body(t_ref, b_ref, o_ref):
        h = 6 * gi + pl.program_id(0)
        b = b_ref[...]
        acc = jnp.zeros(b.shape, F32)
        for k in range(REL_BUCKETS):
            acc = jnp.where(b == k, t_ref[k, h], acc)
        o_ref[0] = acc

    return pl.pallas_call(
        body, name=name, grid=(6,),
        in_specs=[pl.BlockSpec(memory_space=pltpu.SMEM), pl.BlockSpec((WIN, 2 * WIN), lambda h: (0, 0))],
        out_specs=pl.BlockSpec((1, WIN, 2 * WIN), lambda h: (h, 0, 0)),
        out_shape=jax.ShapeDtypeStruct((6, WIN, 2 * WIN), F32),
    )(rel_bias, bucket)


def _attn_fwd(pa, bias, gi, name):
    S = pa.shape[0]
    d = DILATIONS[gi]
    bt = WIN * d
    nb = S // bt
    hpw = _pairs_per_step(d)
    bw = hpw * LANES
    cb = 3 * gi // hpw

    def body(q_ref, kp_ref, kc_ref, vp_ref, vc_ref, b_ref, o_ref, l_ref):
        valid = _attn_valid(pl.program_id(1))

        def residue(r, carry):
            sl = _residue_rows(r, d)
            for t in range(hpw):
                ln = pl.ds(t * LANES, LANES)
                o, lse = _attn_block(q_ref[sl, ln], kp_ref[sl, ln], kc_ref[sl, ln], vp_ref[sl, ln], vc_ref[sl, ln],
                                     b_ref[2 * t], b_ref[2 * t + 1], valid)
                o_ref[sl, ln] = o
                l_ref[sl, ln] = lse
            return carry

        _for_residues(d, residue)

    def spec(off, prev):
        if prev:
            return pl.BlockSpec((bt, bw), lambda hp, n: (jnp.maximum(n - 1, 0), off // hpw + cb + hp))
        return pl.BlockSpec((bt, bw), lambda hp, n: (n, off // hpw + cb + hp))

    ospec = pl.BlockSpec((bt, bw), lambda hp, n: (n, hp))
    return pl.pallas_call(
        body, name=name, grid=(3 // hpw, nb),
        in_specs=[spec(0, False), spec(9, True), spec(9, False), spec(18, True), spec(18, False),
                  pl.BlockSpec((2 * hpw, WIN, 2 * WIN), lambda hp, n: (hp, 0, 0))],
        out_specs=[ospec, ospec],
        out_shape=[jax.ShapeDtypeStruct((S, GW), F32)] * 2,
        compiler_params=pltpu.CompilerParams(dimension_semantics=("parallel", "arbitrary")),
    )(pa, pa, pa, pa, pa, bias)


def _attn_bwd(pa, bias, do, dlse, db_in, dqkv, gi, name):
    S = pa.shape[0]
    d = DILATIONS[gi]
    bt = WIN * d
    nb = S // bt
    hpw = _pairs_per_step(d)
    bw = hpw * LANES
    cb = 3 * gi // hpw

    def body(q_ref, kp_ref, kc_ref, vp_ref, vc_ref, b_ref, do_ref, dl_ref, dbi_ref, dqi_ref, dki_ref, dvi_ref,
             dq_ref, dk_ref, dv_ref, db_ref, ck, cv):
        n = pl.program_id(1)

        @pl.when(n == 0)
        def _():
            db_ref[...] = dbi_ref[...]
            ck[...] = jnp.zeros_like(ck)
            cv[...] = jnp.zeros_like(cv)

        @pl.when(n < nb)
        def _():
            f = functools.partial(_attn_block, valid=_attn_valid(n))

            def residue(r, carry):
                sl = _residue_rows(r, d)
                cs = pl.ds(pl.multiple_of(r * WIN, WIN), WIN)
                for t in range(hpw):
                    ln = pl.ds(t * LANES, LANES)
                    _, vjp = jax.vjp(f, q_ref[sl, ln], kp_ref[sl, ln], kc_ref[sl, ln], vp_ref[sl, ln], vc_ref[sl, ln],
                                     b_ref[2 * t], b_ref[2 * t + 1])
                    dq, dkp, dkc, dvp, dvc, db0, db1 = vjp((do_ref[sl, ln], dl_ref[sl, ln]))
                    dq_ref[sl, ln] = dq
                    dk_ref[sl, ln] = ck[cs, ln] + dkp
                    dv_ref[sl, ln] = cv[cs, ln] + dvp
                    ck[cs, ln] = dkc
                    cv[cs, ln] = dvc
                    db_ref[2 * t] += db0
                    db_ref[2 * t + 1] += db1
                return carry

            _for_residues(d, residue)

        @pl.when(n == nb)
        def _():
            def residue(r, carry):
                sl = _residue_rows(r, d)
                cs = pl.ds(pl.multiple_of(r * WIN, WIN), WIN)
                dk_ref[sl, :] = ck[cs, :]
                dv_ref[sl, :] = cv[cs, :]
                return carry

            _for_residues(d, residue)

    def cur(n):
        return jnp.minimum(n, nb - 1)

    def spec(off, prev):
        if prev:
            return pl.BlockSpec((bt, bw), lambda hp, n: (jnp.maximum(cur(n) - 1, 0), off // hpw + cb + hp))
        return pl.BlockSpec((bt, bw), lambda hp, n: (cur(n), off // hpw + cb + hp))

    gspec = pl.BlockSpec((bt, bw), lambda hp, n: (cur(n), hp))
    bspec = pl.BlockSpec((2 * hpw, WIN, 2 * WIN), lambda hp, n: (hp, 0, 0))
    qspec = pl.BlockSpec((bt, bw), lambda hp, n: (cur(n), cb + hp))
    kspec = pl.BlockSpec((bt, bw), lambda hp, n: (jnp.maximum(n - 1, 0), cb + hp))
    dq, dk, dv, db = pl.pallas_call(
        body, name=name, grid=(3 // hpw, nb + 1),
        in_specs=[spec(0, False), spec(9, True), spec(9, False), spec(18, True), spec(18, False),
                  bspec, gspec, gspec, bspec, _ANY, _ANY, _ANY],
        out_specs=[qspec, kspec, kspec, bspec],
        out_shape=[jax.ShapeDtypeStruct((S, AW), F32)] * 3 + [jax.ShapeDtypeStruct((6, WIN, 2 * WIN), F32)],
        scratch_shapes=[pltpu.VMEM((bt, bw), F32), pltpu.VMEM((bt, bw), F32)],
        input_output_aliases={9: 0, 10: 1, 11: 2},
        compiler_params=pltpu.CompilerParams(dimension_semantics=("arbitrary", "arbitrary")),
    )(pa, pa, pa, pa, pa, bias, do, dlse, db_in, *dqkv)
    return (dq, dk, dv), db


def _mix_core(o0, o1, o2, l0, l1, l2):
    m = lax.stop_gradient(jnp.maximum(jnp.maximum(l0, l1), l2))
    e0, e1, e2 = jnp.exp(l0 - m), jnp.exp(l1 - m), jnp.exp(l2 - m)
    return (e0 * o0 + e1 * o1 + e2 * o2) / (e0 + e1 + e2)


def _mix_fwd(os_, ls_, name):
    S = os_[0].shape[0]
    ins = [("row", a, None, 0) for a in (*os_, *ls_)]
    return _rows(name, lambda ctx, *v: [_mix_core(*v)], ins, [(GW, GW, 0, BF16)], tm=256, nrows=S)[0]


def _mix_bwd(os_, ls_, datt, name):
    S = datt.shape[0]

    def fn(ctx, *v):
        _, vjp = jax.vjp(_mix_core, *v[:6])
        return list(vjp(v[6]))

    ins = [("row", a, None, 0) for a in (*os_, *ls_, datt)]
    outs = [(GW, GW, 0, F32)] * 6
    r = _rows(name, fn, ins, outs, tm=256, nrows=S)
    return r[:3], r[3:]


def _t5_bucket(dist):
    max_exact = REL_BUCKETS // 2
    is_small = dist < max_exact
    nf = jnp.maximum(dist, 1).astype(F32)
    large = max_exact + (jnp.log(nf / max_exact) / math.log(REL_MAX_DISTANCE / max_exact)
                         * (REL_BUCKETS - max_exact)).astype(jnp.int32)
    large = jnp.minimum(large, REL_BUCKETS - 1)
    return jnp.where(is_small, dist, large)


def _buckets(d):
    qi = jnp.arange(WIN)[:, None]
    kk = jnp.arange(2 * WIN)[None, :]
    rel = qi + WIN - kk
    return _t5_bucket(jnp.clip(rel, 0, None) * d)


def _pool_cnt(ctx, w):
    pos = ctx.row0 + _iota((ctx.rows, PG), 0) + 1
    return jnp.minimum(pos, w).astype(F32)


def _pool_d(ctx, halo, u):
    ds = []
    for g, w in enumerate(POOL_WINDOWS):
        ug = u[:, g * PG:(g + 1) * PG]
        s = _with_prev(ctx, halo[:, g * PG:(g + 1) * PG], ug)
        step = 1
        while step < w:
            s = s + _shift_down(s, step)
            step *= 2
        ds.append(s[HALO:] / _pool_cnt(ctx, w) - ug)
    return ds


def _pool_fwd(pb, pw, scale, name):
    S = pb.shape[0]

    def fn(ctx, halo, u, w, sc):
        ds = _pool_d(ctx, halo, u)
        return [jnp.concatenate([_dg(ds[k], w[k], 1, 0) for k in range(4)], axis=1) * sc]

    return _rows(name, fn, [("prev", pb, D, 0), ("row", pb, None, 0), ("raw", pw, None, 0), ("const", scale, None, 0)],
                 [(D, D, 0, BF16)], tm=256, nrows=S)[0]


def _pool_bwd(pb, pw, scale, dpo, name):
    S = pb.shape[0]

    def fn1(ctx, halo, u, w, sc, dy):
        ds = _pool_d(ctx, halo, u)
        dyp = dy * sc
        y = jnp.concatenate([_dg(ds[k], w[k], 1, 0) for k in range(4)], axis=1)
        es, dws = [], []
        for k, wd in enumerate(POOL_WINDOWS):
            cols = slice(k * PG, (k + 1) * PG)
            es.append(_dg(dyp[:, cols], w[k], 1, 1) / _pool_cnt(ctx, wd))
            dws.append(_dg(ds[k], dyp[:, cols], 0, 0))
        return [jnp.concatenate(es, axis=1), jnp.concatenate(dws, axis=0), jnp.sum(dy * y, axis=0, keepdims=True)]

    e, dpw, dsc = _rows(name + "_a", fn1,
                        [("prev", pb, D, 0), ("row", pb, None, 0), ("raw", pw, None, 0), ("const", scale, None, 0),
                         ("row", dpo, None, 0)],
                        [(D, D, 0, F32)], [(4 * PG, PG, PG), (1, D, D)], tm=256, nrows=S)

    def fn2(ctx, ev, halo):
        outs = []
        for g, w in enumerate(POOL_WINDOWS):
            eg = ev[:, g * PG:(g + 1) * PG]
            s = _with_next(ctx, eg, halo[:, g * PG:(g + 1) * PG])
            step = 1
            while step < w:
                s = s + _shift_up(s, step)
                step *= 2
            outs.append(s[:ctx.rows] - eg * _pool_cnt(ctx, w))
        return [jnp.concatenate(outs, axis=1)]

    du = _rows(name + "_b", fn2, [("row", e, None, 0), ("next", e, D, 0)], [(D, D, 0, BF16)], tm=256, nrows=S)[0]
    return du, dpw, dsc


def _conv_taps(ctx, halo, x, K):
    cat = _with_prev(ctx, halo, x)
    return [_shift_down(cat, K - 1 - k)[HALO:] for k in range(K)]


def _conv_pre(taps, w, b):
    acc = b
    for k, t in enumerate(taps):
        acc = acc + t * _row_pick(w, k)
    return acc


CW = 256
CWS = 512
CONV_TM = 512


def _ext_taps(ctx, prev, x, nxt, K):
    cat = jnp.concatenate([jnp.where(ctx.first, 0.0, prev), x, jnp.where(ctx.last, 0.0, nxt)], axis=0)
    return [_shift_down(cat, K - 1 - k)[HALO:] for k in range(K)]


def _conv_t_rows(dp, w, K, tm):
    acc = jnp.zeros((tm, dp.shape[1]), F32)
    for k in range(K):
        acc = acc + _shift_up(dp, K - 1 - k)[:tm] * _row_pick(w, k)
    return acc


def _ssd_conv_fwd(pc, w, b, name):
    S = pc.shape[0]
    base = D // CWS

    def fn(ctx, halo, x, wv, bv):
        return [_silu(_conv_pre(_conv_taps(ctx, halo, x, 4), wv, bv))]

    return _rows(name, fn, [("prev", pc, CWS, base), ("row", pc, CWS, base), ("ccol", w, CWS, 0), ("ccol", b, CWS, 0)],
                 [(XBC, CWS, 0, F32)], tm=CONV_TM, nrows=S, ncol=XBC // CWS)[0]


def _ssd_conv_bwd(pc, w, b, dy, name):
    S = pc.shape[0]
    base = D // CWS

    def fn(ctx, prev, x, nxt, wv, bv, dyv, dyn):
        n = ctx.rows
        taps = _ext_taps(ctx, prev, x, nxt, 4)
        pre = _conv_pre(taps, wv, bv)
        sg = _sigmoid(pre)
        dye = jnp.concatenate([dyv, jnp.where(ctx.last, 0.0, dyn)], axis=0)
        dpre = dye * sg * (1.0 + pre * (1.0 - sg))
        dw = _stack_rows([jnp.sum(dpre[:n] * t[:n], axis=0, keepdims=True) for t in taps], 4)
        return [_conv_t_rows(dpre, wv, 4, n), dw, jnp.sum(dpre[:n], axis=0, keepdims=True)]

    return _rows(name, fn,
                 [("prev", pc, CWS, base), ("row", pc, CWS, base), ("next", pc, CWS, base), ("ccol", w, CWS, 0),
                  ("ccol", b, CWS, 0), ("row", dy, CWS, 0), ("next", dy, CWS, 0)],
                 [(XBC, CWS, 0, BF16)], [(4, XBC, CWS), (1, XBC, CWS)], tm=CONV_TM, nrows=S, ncol=XBC // CWS)


NFC = D_FF // CW


def _ffn_act_fwd(h, w, b, name):
    S = h.shape[0]

    def fn(ctx, ha, a, hv, v, wa, wv, ba, bv):
        pa = _conv_pre(_conv_taps(ctx, ha, a, 3), wa, ba)
        pv = _conv_pre(_conv_taps(ctx, hv, v, 3), wv, bv)
        return [_silu(pa) * pv]

    return _rows(name, fn,
                 [("prev", h, CW, 0), ("row", h, CW, 0), ("prev", h, CW, NFC), ("row", h, CW, NFC),
                  ("ccol", w, CW, 0), ("ccol", w, CW, NFC), ("ccol", b, CW, 0), ("ccol", b, CW, NFC)],
                 [(D_FF, CW, 0, BF16)], tm=CONV_TM, nrows=S, ncol=NFC)[0]


def _ffn_act_bwd(h, w, b, df, name):
    S = h.shape[0]

    def fn(ctx, pa_, a, na, pv_, v, nv, wa, wv, ba, bv, dfv, dfn):
        n = ctx.rows
        ta = _ext_taps(ctx, pa_, a, na, 3)
        tv = _ext_taps(ctx, pv_, v, nv, 3)
        pa = _conv_pre(ta, wa, ba)
        pv = _conv_pre(tv, wv, bv)
        sg = _sigmoid(pa)
        dfe = jnp.concatenate([dfv, jnp.where(ctx.last, 0.0, dfn)], axis=0)
        dpa = dfe * pv * sg * (1.0 + pa * (1.0 - sg))
        dpv = dfe * pa * sg
        res = [_conv_t_rows(dpa, wa, 3, n), _conv_t_rows(dpv, wv, 3, n)]
        for dp, taps in ((dpa, ta), (dpv, tv)):
            res.append(_stack_rows([jnp.sum(dp[:n] * t[:n], axis=0, keepdims=True) for t in taps], 3))
        for dp in (dpa, dpv):
            res.append(jnp.sum(dp[:n], axis=0, keepdims=True))
        return res

    ins = []
    for base in (0, NFC):
        ins += [("prev", h, CW, base), ("row", h, CW, base), ("next", h, CW, base)]
    ins += [("ccol", w, CW, 0), ("ccol", w, CW, NFC), ("ccol", b, CW, 0), ("ccol", b, CW, NFC),
            ("row", df, CW, 0), ("next", df, CW, 0)]
    dha, dhv, dwa, dwv, dba, dbv = _rows(
        name, fn, ins, [(D_FF, CW, 0, BF16)] * 2, [(3, D_FF, CW)] * 2 + [(1, D_FF, CW)] * 2, tm=CONV_TM, nrows=S, ncol=NFC)
    return dha, dhv, jnp.concatenate([dwa, dwv], axis=1), jnp.concatenate([dba, dbv], axis=1)


NSLAB = D // LANES
CPS = 2


def _ssd_chunk(xs, Bs, Cs, dtraw, dtb, alog, prev):
    lsz = SSD_CHUNK
    lane = _iota((lsz, LANES), 1)
    row = _iota((lsz, LANES), 0)
    dt = jnp.where(lane < SSD_HEADS, _softplus(dtraw + dtb), 0.0)
    a = dt * (-jnp.exp(alog))
    tril = row >= lane
    a_cs = _fdot(tril.astype(F32), a)
    a_cst = a_cs.T
    a_last = jnp.sum(a, axis=0, keepdims=True)
    lo = lane < HD
    top = row < HD
    cbs = [_bdot_nt(Cs[g], Bs[g]) for g in range(2)]
    ys, news = [], []
    for s in range(NSLAB):
        g = s // (NSLAB // 2)
        cols, lms, dts, als = [], [], [], []
        for hh in range(2):
            h = 2 * s + hh
            col = _lane_pick(a_cs, h)
            seg = col - _row_pick(a_cst, h)
            lms.append(jnp.exp(jnp.where(tril, seg, NEG)))
            cols.append(col)
            dts.append(_lane_pick(dt, h))
            als.append(_lane_pick(a_last, h))
        col_x = jnp.where(lo, cols[0], cols[1])
        al_x = jnp.where(lo, als[0], als[1])
        xc = xs[s] * jnp.where(lo, dts[0], dts[1])
        yd = jnp.where(lo, _bdot_nn(cbs[g] * lms[0], xc), _bdot_nn(cbs[g] * lms[1], xc))
        yoff = _bdot_nt(Cs[g], prev[s]) * jnp.exp(col_x)
        ys.append(yd + yoff)
        st = _bdot_tn(xc * jnp.exp(al_x - col_x), Bs[g])
        news.append(prev[s] * jnp.exp(jnp.where(top, als[0], als[1])) + st)
    return ys, news


def _ssd_scan_fwd(xbc_c, pd, dtb, alog, name):
    S = xbc_c.shape[0]
    nc = S // SSD_CHUNK
    rows_ = CPS * SSD_CHUNK

    def body(x_ref, b_ref, c_ref, dt_ref, dtb_ref, al_ref, y_ref, st_ref, state):
        c = pl.program_id(0)

        @pl.when(c == 0)
        def _():
            state[...] = jnp.zeros_like(state)

        prev = [state[s * LANES:(s + 1) * LANES, :] for s in range(NSLAB)]
        for u in range(CPS):
            rw = pl.ds(u * SSD_CHUNK, SSD_CHUNK)
            xs = [x_ref[rw, s * LANES:(s + 1) * LANES] for s in range(NSLAB)]
            Bs = [b_ref[rw, g * SSD_N:(g + 1) * SSD_N] for g in range(2)]
            Cs = [c_ref[rw, g * SSD_N:(g + 1) * SSD_N] for g in range(2)]
            for s in range(NSLAB):
                st_ref[u, s * LANES:(s + 1) * LANES, :] = prev[s]
            ys, prev = _ssd_chunk(xs, Bs, Cs, dt_ref[rw, :].astype(F32), dtb_ref[...], al_ref[...], prev)
            for s in range(NSLAB):
                y_ref[rw, s * LANES:(s + 1) * LANES] = ys[s]
        for s in range(NSLAB):
            state[s * LANES:(s + 1) * LANES, :] = prev[s]

    return pl.pallas_call(
        body, name=name, grid=(nc // CPS,),
        in_specs=[pl.BlockSpec((rows_, D), lambda c: (c, 0)),
                  pl.BlockSpec((rows_, 2 * SSD_N), lambda c: (c, D // (2 * SSD_N))),
                  pl.BlockSpec((rows_, 2 * SSD_N), lambda c: (c, D // (2 * SSD_N) + 1)),
                  pl.BlockSpec((rows_, LANES), lambda c: (c, 0)),
                  pl.BlockSpec((1, LANES), lambda c: (0, 0)), pl.BlockSpec((1, LANES), lambda c: (0, 0))],
        out_specs=[pl.BlockSpec((rows_, D), lambda c: (c, 0)), pl.BlockSpec((CPS, D, SSD_N), lambda c: (c, 0, 0))],
        out_shape=[jax.ShapeDtypeStruct((S, D), F32), jax.ShapeDtypeStruct((nc, D, SSD_N), F32)],
        scratch_shapes=[pltpu.VMEM((D, SSD_N), F32)],
        compiler_params=pltpu.CompilerParams(dimension_semantics=("arbitrary",)),
    )(xbc_c, xbc_c, xbc_c, pd, dtb, alog)


def _ssd_scan_bwd(xbc_c, pd, dtb, alog, states, dy, dxs_skip, name):
    S = xbc_c.shape[0]
    nc = S // SSD_CHUNK
    rows_ = CPS * SSD_CHUNK

    def body(x_ref, b_ref, c_ref, dt_ref, dtb_ref, al_ref, st_ref, dy_ref, sk_ref,
             dx_ref, ddt_ref, ddtb_ref, dal_ref, dstate):
        c = pl.program_id(0)

        @pl.when(c == 0)
        def _():
            dstate[...] = jnp.zeros_like(dstate)
            ddtb_ref[...] = jnp.zeros_like(ddtb_ref)
            dal_ref[...] = jnp.zeros_like(dal_ref)

        dnew = [dstate[s * LANES:(s + 1) * LANES, :] for s in range(NSLAB)]
        for u in reversed(range(CPS)):
            rw = pl.ds(u * SSD_CHUNK, SSD_CHUNK)
            xs = [x_ref[rw, s * LANES:(s + 1) * LANES] for s in range(NSLAB)]
            Bs = [b_ref[rw, g * SSD_N:(g + 1) * SSD_N] for g in range(2)]
            Cs = [c_ref[rw, g * SSD_N:(g + 1) * SSD_N] for g in range(2)]
            prev = [st_ref[u, s * LANES:(s + 1) * LANES, :] for s in range(NSLAB)]
            _, vjp = jax.vjp(_ssd_chunk, xs, Bs, Cs, dt_ref[rw, :].astype(F32), dtb_ref[...], al_ref[...], prev)
            dys = [dy_ref[rw, s * LANES:(s + 1) * LANES] for s in range(NSLAB)]
            dxs, dBs, dCs, ddt, ddtb, dal, dnew = vjp((dys, dnew))
            for s in range(NSLAB):
                dx_ref[rw, s * LANES:(s + 1) * LANES] = dxs[s] + sk_ref[rw, s * LANES:(s + 1) * LANES]
            for g in range(2):
                dx_ref[rw, D + g * SSD_N:D + (g + 1) * SSD_N] = dBs[g]
                dx_ref[rw, D + 2 * SSD_N + g * SSD_N:D + 2 * SSD_N + (g + 1) * SSD_N] = dCs[g]
            ddt_ref[rw, :] = ddt
            ddtb_ref[...] += ddtb
            dal_ref[...] += dal
        for s in range(NSLAB):
            dstate[s * LANES:(s + 1) * LANES, :] = dnew[s]

    def rv(c):
        return nc // CPS - 1 - c

    return pl.pallas_call(
        body, name=name, grid=(nc // CPS,),
        in_specs=[pl.BlockSpec((rows_, D), lambda c: (rv(c), 0)),
                  pl.BlockSpec((rows_, 2 * SSD_N), lambda c: (rv(c), D // (2 * SSD_N))),
                  pl.BlockSpec((rows_, 2 * SSD_N), lambda c: (rv(c), D // (2 * SSD_N) + 1)),
                  pl.BlockSpec((rows_, LANES), lambda c: (rv(c), 0)),
                  pl.BlockSpec((1, LANES), lambda c: (0, 0)), pl.BlockSpec((1, LANES), lambda c: (0, 0)),
                  pl.BlockSpec((CPS, D, SSD_N), lambda c: (rv(c), 0, 0)),
                  pl.BlockSpec((rows_, D), lambda c: (rv(c), 0)),
                  pl.BlockSpec((rows_, D), lambda c: (rv(c), 0))],
        out_specs=[pl.BlockSpec((rows_, XBC), lambda c: (rv(c), 0)),
                   pl.BlockSpec((rows_, LANES), lambda c: (rv(c), 0)),
                   pl.BlockSpec((1, LANES), lambda c: (0, 0)), pl.BlockSpec((1, LANES), lambda c: (0, 0))],
        out_shape=[jax.ShapeDtypeStruct((S, XBC), F32), jax.ShapeDtypeStruct((S, LANES), F32),
                   jax.ShapeDtypeStruct((1, LANES), F32), jax.ShapeDtypeStruct((1, LANES), F32)],
        scratch_shapes=[pltpu.VMEM((D, SSD_N), F32)],
        compiler_params=pltpu.CompilerParams(dimension_semantics=("arbitrary",)),
    )(xbc_c, xbc_c, xbc_c, pd, dtb, alog, states, dy, dxs_skip)


def _ssd_post_core(y, xs, z, d128, nw):
    tm = y.shape[0]
    ex = (_iota((LANES, D), 1) // HD == _iota((LANES, D), 0)).astype(F32)
    d_x = jnp.sum(_fdot(jnp.broadcast_to(d128, (8, LANES)), ex), axis=0, keepdims=True) * 0.125
    y2 = (y + d_x * xs) * _silu(z)
    lo = _iota((tm, D), 1) < D // 2
    sq = y2 * y2
    ms0 = jnp.sum(jnp.where(lo, sq, 0.0), axis=-1, keepdims=True) / (D // 2)
    ms1 = jnp.sum(jnp.where(lo, 0.0, sq), axis=-1, keepdims=True) / (D // 2)
    r = jnp.where(lo, lax.rsqrt(ms0 + EPS), lax.rsqrt(ms1 + EPS))
    return y2 * r * nw


def _ssd_post_ins(y, xbc_c, pc, d128, nw):
    return [("row", y, None, 0), ("row", xbc_c, D, 0), ("row", pc, D, 0), ("const", d128, None, 0), ("const", nw, None, 0)]


def _ssd_post_fwd(y, xbc_c, pc, d128, nw, name):
    S = y.shape[0]
    return _rows(name, lambda ctx, *v: [_ssd_post_core(*v)], _ssd_post_ins(y, xbc_c, pc, d128, nw),
                 [(D, D, 0, BF16)], tm=256, nrows=S)[0]


def _ssd_post_bwd(y, xbc_c, pc, d128, nw, dout, name):
    S = y.shape[0]

    def fn(ctx, *v):
        _, vjp = jax.vjp(_ssd_post_core, *v[:5])
        return list(vjp(v[5]))

    return _rows(name, fn, _ssd_post_ins(y, xbc_c, pc, d128, nw) + [("row", dout, None, 0)],
                 [(D, D, 0, F32), (D, D, 0, F32), (D, D, 0, BF16)], [(1, LANES, LANES), (1, D, D)], tm=256, nrows=S)


def _gates_core(g0, g1, g2, b0, b1, b2, ya, yb, yc):
    return _sigmoid(g0 + b0) * ya + _sigmoid(g1 + b1) * yb + _sigmoid(g2 + b2) * yc


def _gate_parts(pdv, bv):
    gp = pltpu.roll(pdv, SEC_D - 16, 1)
    return [gp[:, k * D:(k + 1) * D] for k in range(3)] + [bv[:, k * D:(k + 1) * D] for k in range(3)]


def _gates_fwd(pd, bg, ya, yb, yc, name):
    S = pd.shape[0]

    def fn(ctx, pdv, bv, a, b, c):
        return [_gates_core(*_gate_parts(pdv, bv), a, b, c)]

    return _rows(name, fn, [("row", pd, None, 0), ("const", bg, None, 0), ("row", ya, None, 0), ("row", yb, None, 0),
                            ("row", yc, None, 0)], [(D, D, 0, BF16)], tm=256, nrows=S)[0]


def _gates_post(dm, pdv, a, b, c, bv):
    _, vjp = jax.vjp(_gates_core, *_gate_parts(pdv, bv), a, b, c)
    g = vjp(dm)
    return [g[6], g[7], g[8], jnp.concatenate(g[0:3], axis=1), jnp.concatenate(g[3:6], axis=1)]


def _adam_update(wv, gv, mv, vv):
    m2 = ADAM_B1 * mv + (1.0 - ADAM_B1) * gv
    v2 = ADAM_B2 * vv + (1.0 - ADAM_B2) * jnp.square(gv)
    m_hat = m2 / (1.0 - ADAM_B1 ** ADAM_STEP)
    v_hat = v2 / (1.0 - ADAM_B2 ** ADAM_STEP)
    delta = -ADAM_LR * (m_hat / (jnp.sqrt(v_hat) + ADAM_EPS) + ADAM_WD * wv)
    return [delta, m2, v2]


def _adamw(w, g, m, v, name):
    rows, C = w.shape
    tm = _pick(rows, [t for t in (512, 256, 128, 64, 32, 16, 8) if t * C <= ADAM_TILE])
    return _rows(name, lambda ctx, *a: _adam_update(*a), [("row", a, None, 0) for a in (w, g, m, v)],
                 [(C, C, 0, F32)] * 3, tm=tm, nrows=rows)


def _position():
    return lax.axis_index("x"), lax.axis_index("y"), lax.axis_index("c")


def _other_chips(x, y):
    return [(1 - x, y), (x, 1 - y), (1 - x, 1 - y)]


_HBM = pl.BlockSpec(memory_space=pltpu.HBM)


def _gather_parts(half, lo, n):
    def copies(p_ref, out_ref, send_sems, recv_sems):
        x, y, c = _position()
        sibling = (x, y, 1 - c)
        chips = _other_chips(x, y)

        def slab(chip, h):
            return out_ref.at[2 * chip[0] + chip[1], pl.ds(h * half + lo, n), :]

        def copy(k, src, dst, to):
            return pltpu.make_async_remote_copy(src_ref=src, dst_ref=dst, send_sem=send_sems.at[k],
                                                recv_sem=recv_sems.at[k], device_id=to, device_id_type=MESH)

        first = [copy(j, p_ref.at[pl.ds(c * half + lo, n), :], slab((x, y), c), (*chip, c)) for j, chip in enumerate(chips)]
        passed = [copy(3 + j, slab(chip, c), slab(chip, c), sibling) for j, chip in enumerate(chips)]
        from_chips = [copy(j, slab(chip, c), slab(chip, c), (x, y, c)) for j, chip in enumerate(chips)]
        from_sibling = [copy(3 + j, slab(chip, 1 - c), slab(chip, 1 - c), (x, y, c)) for j, chip in enumerate(chips)]
        return first, passed, from_chips, from_sibling

    def start(ins, outs, scr):
        for cp in copies(ins[0], outs[0], *scr)[0]:
            cp.start()

    def finish(ins, outs, scr):
        first, passed, from_chips, from_sibling = copies(ins[0], outs[0], *scr)
        for j in range(3):
            from_chips[j].wait_recv()
            passed[j].start()
        for cp in from_sibling:
            cp.wait_recv()
        for cp in first + passed:
            cp.wait_send()

    return start, finish


def _rs_chip_parts(lo, n):
    def copies(h_ref, out_ref, send_sems, recv_sems):
        x, y, c = _position()
        return [pltpu.make_async_remote_copy(src_ref=h_ref.at[2 * chip[0] + chip[1], pl.ds(lo, n), :],
                                             dst_ref=out_ref.at[j, pl.ds(lo, n), :],
                                             send_sem=send_sems.at[j], recv_sem=recv_sems.at[j],
                                             device_id=(*chip, c), device_id_type=MESH)
                for j, chip in enumerate(_other_chips(x, y))]

    def start(ins, outs, scr):
        for cp in copies(ins[0], outs[0], *scr):
            cp.start()

    def finish(ins, outs, scr):
        for cp in copies(ins[0], outs[0], *scr):
            cp.wait()

    return start, finish


class _Stream:
    def __init__(self, src, buf, parts, nsem, units, name):
        self.src, self.buf, self.parts, self.nsem, self.name = src, buf, parts, nsem, name
        self.next, self.units = 0, units

    def _scratch(self):
        return [pltpu.SemaphoreType.DMA((self.nsem,)), pltpu.SemaphoreType.DMA((self.nsem,))]

    def _take(self, units):
        units = min(units, self.units - self.next)
        lo = self.next * 16
        self.next += units
        return lo, units * 16

    def _set(self, outs):
        self.buf = outs[0]

    def hook(self, units):
        lo, n = self._take(units)
        if n == 0:
            return None
        start, finish = self.parts(lo, n)
        return _Hook([self.src, self.buf], [jax.ShapeDtypeStruct(self.buf.shape, self.buf.dtype)], {1: 0},
                     self._scratch(), start, finish, self._set)

    def drain(self):
        lo, n = self._take(self.units)
        if n:
            start, finish = self.parts(lo, n)

            def body(s_ref, b_ref, o_ref, send_sems, recv_sems):
                args = ((s_ref, b_ref), (o_ref,), (send_sems, recv_sems))
                start(*args)
                finish(*args)

            self.buf = pl.pallas_call(
                body, name=self.name, in_specs=[_ANY, _ANY], out_specs=_ANY,
                out_shape=jax.ShapeDtypeStruct(self.buf.shape, self.buf.dtype),
                scratch_shapes=self._scratch(), input_output_aliases={1: 0},
            )(self.src, self.buf)
        return self.buf


def _rs_pair_parts(half, lo, n):
    def copy(g_ref, out_ref, send_sems, recv_sems):
        x, y, c = _position()
        return pltpu.make_async_remote_copy(
            src_ref=g_ref.at[pl.ds(0, 4), pl.ds((1 - c) * half + lo, n), :], dst_ref=out_ref.at[pl.ds(0, 4), pl.ds(lo, n), :],
            send_sem=send_sems.at[0], recv_sem=recv_sems.at[0], device_id=(x, y, 1 - c), device_id_type=MESH)

    def start(ins, outs, scr):
        copy(ins[0], outs[0], *scr).start()

    def finish(ins, outs, scr):
        copy(ins[0], outs[0], *scr).wait()

    return start, finish


def _rs_swap(r, name):
    Rh, C = r.shape

    def body(r_ref, out_ref, send_sem, recv_sem):
        x, y, c = _position()
        cp = pltpu.make_async_remote_copy(src_ref=r_ref, dst_ref=out_ref, send_sem=send_sem,
                                          recv_sem=recv_sem, device_id=(x, y, 1 - c), device_id_type=MESH)
        cp.start()
        cp.wait()

    return pl.pallas_call(
        body, name=name, in_specs=[_HBM], out_specs=_HBM,
        out_shape=jax.ShapeDtypeStruct((Rh, C), r.dtype),
        scratch_shapes=[pltpu.SemaphoreType.DMA, pltpu.SemaphoreType.DMA],
    )(r)


def _rs_add_pair(g, recv, cidx, name):
    _, R, C = g.shape
    Rh = R // 2
    tm = _pick(Rh, (400, 280, 200, 160, 80, 40, 16, 8))
    nt = Rh // tm

    def body(c_ref, g_ref, r_ref, o_ref):
        o_ref[...] = (g_ref[...].astype(F32) + r_ref[...].astype(F32)).astype(o_ref.dtype)

    return pl.pallas_call(
        body, name=name,
        grid_spec=pltpu.PrefetchScalarGridSpec(
            num_scalar_prefetch=1, grid=(4, nt),
            in_specs=[pl.BlockSpec((1, tm, C), lambda k, i, cr: (k, cr[0] * nt + i, 0)),
                      pl.BlockSpec((1, tm, C), lambda k, i, cr: (k, i, 0))],
            out_specs=pl.BlockSpec((1, tm, C), lambda k, i, cr: (k, i, 0))),
        out_shape=jax.ShapeDtypeStruct((4, Rh, C), BF16),
    )(cidx, g, recv)


def _rs_add_chips(h, recv, chip_idx, name):
    _, Rh, C = h.shape
    tm = _pick(Rh, (400, 280, 200, 160, 80, 40, 16, 8))

    def body(c_ref, h_ref, r_ref, o_ref):
        acc = h_ref[0].astype(F32)
        for j in range(3):
            acc = acc + r_ref[j].astype(F32)
        o_ref[...] = acc

    return pl.pallas_call(
        body, name=name,
        grid_spec=pltpu.PrefetchScalarGridSpec(
            num_scalar_prefetch=1, grid=(Rh // tm,),
            in_specs=[pl.BlockSpec((1, tm, C), lambda i, cr: (cr[0], i, 0)), pl.BlockSpec((3, tm, C), lambda i, cr: (0, i, 0))],
            out_specs=pl.BlockSpec((tm, C), lambda i, cr: (i, 0))),
        out_shape=jax.ShapeDtypeStruct((Rh, C), F32),
    )(chip_idx, h, recv)


def _all_reduce_small(vec, name):
    n, C = vec.shape

    def body(v_ref, out_ref, buf, send_sems, recv_sems):
        x, y, c = _position()

        def flip(k):
            return ((1 - x) if k & 4 else x, (1 - y) if k & 2 else y, (1 - c) if k & 1 else c)

        def idx(p):
            return 4 * p[0] + 2 * p[1] + p[2]

        me = idx((x, y, c))
        buf[me] = v_ref[...]
        cps = [pltpu.make_async_remote_copy(src_ref=v_ref, dst_ref=buf.at[me], send_sem=send_sems.at[k - 1],
                                            recv_sem=recv_sems.at[k - 1], device_id=flip(k), device_id_type=MESH)
               for k in range(1, 8)]
        for cp in cps:
            cp.start()
        for k in range(1, 8):
            pltpu.make_async_remote_copy(src_ref=v_ref, dst_ref=buf.at[idx(flip(k))], send_sem=send_sems.at[k - 1],
                                         recv_sem=recv_sems.at[k - 1], device_id=flip(k), device_id_type=MESH).wait_recv()
        for cp in cps:
            cp.wait_send()
        acc = buf[0]
        for s in range(1, 8):
            acc = acc + buf[s]
        out_ref[...] = acc

    return pl.pallas_call(
        body, name=name,
        in_specs=[pl.BlockSpec(memory_space=pltpu.VMEM)], out_specs=pl.BlockSpec(memory_space=pltpu.VMEM),
        out_shape=jax.ShapeDtypeStruct((n, C), F32),
        scratch_shapes=[pltpu.VMEM((8, n, C), F32), pltpu.SemaphoreType.DMA((7,)), pltpu.SemaphoreType.DMA((7,))],
    )(vec)


BIG = (("w_in", (D, IN_WIDTH // 4), "cols"), ("w_a", (GW, D // 4), "cols"), ("pool_w", (4, PG // 4, PG), "pool"),
       ("w_b", (D // 4, D), "rows"), ("w_c", (D // 4, D), "rows"), ("w_o", (D // 4, D), "rows"),
       ("ffn_w_up", (D, 2 * D_FF // 4), "cols"), ("ffn_w_down", (D_FF // 4, D), "rows"))
def _pack_rows(s):
    k = math.prod(s) // D
    return -(-k // 16) * 16, k


PACK_ROWS = sum(_pack_rows(s)[0] for _, s, _ in BIG)
PACK_PAD = -(-PACK_ROWS // 32) * 32


def _pad_rows(v, rows):
    pad = [(0, 0)] * v.ndim
    pad[-2] = (0, rows - v.shape[-2])
    return jnp.pad(v, pad) if rows > v.shape[-2] else v


def _pack_blocks(blocks, dtype):
    lead = blocks["w_in"].shape[:-2]
    flat = []
    for n, s, how in BIG:
        v = blocks[n].astype(dtype)
        if how == "cols":
            v = jnp.swapaxes(v, -1, -2)
        flat.append(_pad_rows(v.reshape(*lead, -1, D), _pack_rows(s)[0]))
    flat.append(jnp.zeros((*lead, PACK_PAD - PACK_ROWS, D), dtype))
    return jnp.concatenate(flat, axis=-2)


def _unpack_blocks(pack):
    out, r = {}, 0
    for n, s, how in BIG:
        rows, k = _pack_rows(s)
        v = pack[r:r + k, :]
        out[n] = v.reshape(s[1], s[0]).T if how == "cols" else v.reshape(s)
        r += rows
    return out


def _operands(allp):
    out, r = {}, 0
    for n, s, how in BIG:
        rows, k = _pack_rows(s)
        v = allp[:, r:r + k, :]
        if how == "cols":
            out[n] = v.reshape(4 * s[1], s[0])
        elif how == "rows":
            out[n] = v.reshape(4 * s[0], s[1])
        else:
            out[n] = v.reshape(4, *s).transpose(1, 0, 2, 3).reshape(4, PG, PG)
        r += rows
    return out


def _pack_operands(g, dtype):
    flat = []
    for n, s, how in BIG:
        v = g[n].astype(dtype)
        if how == "pool":
            v = v.reshape(4, 4, s[1], s[2]).transpose(1, 0, 2, 3)
        flat.append(_pad_rows(v.reshape(4, -1, D), _pack_rows(s)[0]))
    flat.append(jnp.zeros((4, PACK_PAD - PACK_ROWS, D), dtype))
    return jnp.concatenate(flat, axis=1)


def _layer_fwd(x, w, sm, bias, hk):
    pa, u = _mmf(None, w["in_a"], tb=True, pre=(_rms_core, [x], [sm["ln1_g"]]), name="in_a", tm=1024, hook=hk("in_a"))
    pb = _mm(u, w["in_b"], tb=True, out_dtype=BF16, name="in_b", hook=hk("in_b"))
    pc = _mm(u, w["in_c"], tb=True, out_dtype=BF16, name="in_c", hook=hk("in_c"))
    pd = _mm(u, w["in_d"], tb=True, out_dtype=BF16, name="in_d", hook=hk("in_d"))
    os_, ls_ = [], []
    for gi in range(3):
        o, l = _attn_fwd(pa, bias[gi], gi, "attn_fwd%d" % gi)
        os_.append(o)
        ls_.append(l)
    att = _mix_fwd(os_, ls_, "mix_fwd")
    ya = _mm(att, w["w_a"], tb=True, out_dtype=BF16, name="mm_wa")
    pool_o = _pool_fwd(pb, w["pool_w"], sm["pool_scale"], "pool_fwd")
    yb = _mm(pool_o, w["w_b"], out_dtype=BF16, name="mm_wb")
    xbc_c = _ssd_conv_fwd(pc, sm["ssd_conv_w"], sm["ssd_conv_b"], "ssd_conv_fwd")
    y_scan, states = _ssd_scan_fwd(xbc_c, pd, sm["ssd_dt_bias"], sm["ssd_a_log"], "ssd_scan_fwd")
    ssd_o = _ssd_post_fwd(y_scan, xbc_c, pc, sm["ssd_d"], sm["ssd_norm_w"], "ssd_post_fwd")
    yc = _mm(ssd_o, w["w_c"], out_dtype=BF16, name="mm_wc")
    merged = _gates_fwd(pd, sm["b_gate"], ya, yb, yc, "gates_fwd")
    x1 = _mm(merged, w["w_o"], add=x, name="mm_wo", hook=hk("mm_wo"))
    h, u2 = _mmf(None, w["ffn_w_up"], tb=True, pre=(_rms_core, [x1], [sm["ln2_g"]]), out_dtype=BF16, name="mm_up",
                 tm=1024, hook=hk("mm_up"))
    f = _ffn_act_fwd(h, sm["ffn_conv_w"], sm["ffn_conv_b"], "ffn_act_fwd")
    x2 = _mm(f, w["ffn_w_down"], add=x1, name="mm_down", hook=hk("mm_down"))
    saved = dict(x=x, u=u, pa=pa, pb=pb, pc=pc, pd=pd, os=os_, ls=ls_, att=att, ya=ya, yb=yb, yc=yc, pool_o=pool_o,
                 xbc_c=xbc_c, y_scan=y_scan, states=states, ssd_o=ssd_o, merged=merged, x1=x1, u2=u2, h=h, f=f)
    return x2, saved


def _layer_bwd(dx2, w, sm, bias, dbs, sv, hk):
    gw, gs = {}, {}
    S = dx2.shape[0]

    def gmm(a, b, name):
        return _mm(a, b, ta=True, out_dtype=BF16, name=name, hook=hk(name))

    df = _mm(dx2, w["ffn_w_down"], tb=True, out_dtype=BF16, name="d_f", hook=hk("d_f"))
    gw["ffn_w_down"] = gmm(sv["f"], dx2, "g_down")
    dha, dhv, gs["ffn_conv_w"], gs["ffn_conv_b"] = _ffn_act_bwd(sv["h"], sm["ffn_conv_w"], sm["ffn_conv_b"], df, "ffn_act_bwd")
    dx1, gs["ln2_g"] = _mmf([dha, dhv], [w["up_a"], w["up_v"]], name="d_u2_v", tm=256, hook=hk("d_u2_v"),
                            post=(_rms_post, [sv["x1"], dx2], [sm["ln2_g"]], [(D, F32)], [(1, D)]))
    gw["ffn_w_up"] = jnp.concatenate([gmm(dha, sv["u2"], "g_up_a"), gmm(dhv, sv["u2"], "g_up_v")], axis=0)
    dya, dyb, dyc, dgate, gs["b_gate"] = _mmf(
        dx1, w["w_o"], tb=True, name="d_merged", tm=256, hook=hk("d_merged"),
        post=(_gates_post, [sv["pd"], sv["ya"], sv["yb"], sv["yc"]], [sm["b_gate"]],
              [(D, BF16)] * 3 + [(3 * D, BF16)], [(1, 3 * D)]))
    gw["w_o"] = gmm(sv["merged"], dx1, "g_wo")
    dssd_o = _mm(dyc, w["w_c"], tb=True, name="d_ssd_o")
    gw["w_c"] = gmm(sv["ssd_o"], dyc, "g_wc")
    dy_scan, dxs_skip, dz, gs["ssd_d"], gs["ssd_norm_w"] = _ssd_post_bwd(
        sv["y_scan"], sv["xbc_c"], sv["pc"], sm["ssd_d"], sm["ssd_norm_w"], dssd_o, "ssd_post_bwd")
    dxbc_c, ddt, gs["ssd_dt_bias"], gs["ssd_a_log"] = _ssd_scan_bwd(
        sv["xbc_c"], sv["pd"], sm["ssd_dt_bias"], sm["ssd_a_log"], sv["states"], dy_scan, dxs_skip, "ssd_scan_bwd")
    dxbc, gs["ssd_conv_w"], gs["ssd_conv_b"] = _ssd_conv_bwd(sv["pc"], sm["ssd_conv_w"], sm["ssd_conv_b"], dxbc_c, "ssd_conv_bwd")
    dpool_o = _mm(dyb, w["w_b"], tb=True, name="d_pool_o")
    gw["w_b"] = gmm(sv["pool_o"], dyb, "g_wb")
    dpb, dpw, gs["pool_scale"] = _pool_bwd(sv["pb"], w["pool_w"], sm["pool_scale"], dpool_o, "pool_bwd")
    gw["pool_w"] = dpw.reshape(4, PG, PG)
    datt = _mm(dya, w["w_a"], name="d_att")
    gw["w_a"] = gmm(dya, sv["att"], "g_wa")
    dos, dls = _mix_bwd(sv["os"], sv["ls"], datt, "mix_bwd")
    dqkv = tuple(lax.empty((S, AW), F32) for _ in range(3))
    dbs = list(dbs)
    for gi in range(3):
        dqkv, dbs[gi] = _attn_bwd(sv["pa"], bias[gi], dos[gi], dls[gi], dbs[gi], dqkv, gi, "attn_bwd%d" % gi)
    u = sv["u"]
    pieces = [(dqkv[0], "wq"), (dqkv[1], "wk"), (dqkv[2], "wv"), (dpb, "in_b"), (dz, "wz"), (dxbc, "wxbc"),
              (ddt, "wdt"), (dgate, "wgate")]
    du = _mmf([dp for dp, _ in pieces[:4]], [w[key] for _, key in pieces[:4]], name="d_u_a", tm=256, hook=hk("d_u_a"))[0]
    dx, gs["ln1_g"] = _mmf([dp for dp, _ in pieces[4:]], [w[key] for _, key in pieces[4:]], add=du, name="d_u_wgate",
                           tm=256, hook=hk("d_u_wgate"),
                           post=(_rms_post, [sv["x"], dx1], [sm["ln1_g"]], [(D, F32)], [(1, D)]))
    g_in = []
    for dp, key in pieces:
        g = gmm(dp, u, "g_in_" + key)
        g_in.append(g[:SSD_HEADS] if key == "wdt" else g)
    gw["w_in"] = jnp.concatenate(g_in, axis=0)
    return dx, gw, gs, dbs


SMALL_LAYER = ("ln1_g", "b_gate", "pool_scale", "ssd_conv_w", "ssd_conv_b", "ssd_dt_bias", "ssd_a_log", "ssd_d",
               "ssd_norm_w", "ln2_g", "ffn_conv_w", "ffn_conv_b")


def _pad_lanes(v):
    return jnp.pad(v, (0, LANES - v.shape[0])).reshape(1, LANES)


def _layer_weights(ops):
    wt = ops["w_in"]
    o1, o2, o3 = SEC_A, SEC_A + SEC_B, SEC_A + SEC_B + SEC_C
    w = dict(ops)
    w["in_a"] = jnp.pad(wt[:o1], ((0, SEC_A_PAD - o1), (0, 0)))
    w["in_b"] = wt[o1:o2]
    w["in_c"] = wt[o2:o3]
    w["in_d"] = jnp.pad(wt[o3:], ((0, SEC_D - (IN_WIDTH - o3)), (0, 0)))
    w["wq"], w["wk"], w["wv"] = wt[:AW], wt[AW:2 * AW], wt[2 * AW:o1]
    w["wz"], w["wxbc"] = wt[o2:o2 + D], wt[o2 + D:o3]
    w["wdt"] = jnp.pad(wt[o3:o3 + SSD_HEADS], ((0, LANES - SSD_HEADS), (0, 0)))
    w["wgate"] = wt[o3 + SSD_HEADS:]
    w["up_a"], w["up_v"] = ops["ffn_w_up"][:D_FF], ops["ffn_w_up"][D_FF:]
    return w


def _layer_small(p, i):
    sm = {n: p[n][i] for n in SMALL_LAYER}
    out = {}
    for n, v in sm.items():
        if n in ("ssd_dt_bias", "ssd_a_log", "ssd_d"):
            out[n] = _pad_lanes(v)
        elif v.ndim == 1:
            out[n] = v.reshape(1, -1)
        else:
            out[n] = v
    return out


def _local_step(x, target, rel_bias, final_g, layer_full, small, fwd_hooks=None, bwd_hooks=None, after_bwd=None):
    nl = small["ln1_g"].shape[0]
    buckets = [_buckets(d).astype(jnp.int32) for d in DILATIONS]
    bias = [_bias_table(rel_bias, buckets[gi], gi, "bias_table%d" % gi) for gi in range(3)]
    no_hooks = lambda i: (lambda name: None)
    fwd_hooks = fwd_hooks or no_hooks
    bwd_hooks = bwd_hooks or no_hooks
    saved, ws, sms = [], [], []
    h = x
    for i in range(nl):
        w = _layer_weights(layer_full(i))
        sm = _layer_small(small, i)
        h, sv = _layer_fwd(h, w, sm, bias, fwd_hooks(i))
        saved.append(sv)
        ws.append(w)
        sms.append(sm)
    dh, dfinal, loss = _final_loss(h, target, final_g.reshape(1, D))
    gws, gss = [None] * nl, [None] * nl
    dbs = [jnp.zeros((6, WIN, 2 * WIN), F32)] * 3
    for i in reversed(range(nl)):
        dh, gws[i], gss[i], dbs = _layer_bwd(dh, ws[i], sms[i], bias, dbs, saved[i], bwd_hooks(i))
        if after_bwd is not None:
            after_bwd(i, gws[i])
    drel = []
    for gi in range(3):
        onehot = jnp.pad(jax.nn.one_hot(buckets[gi].reshape(-1), REL_BUCKETS, dtype=BF16), ((0, 0), (0, LANES - REL_BUCKETS)))
        drel.append(_mm(dbs[gi].reshape(6, WIN * 2 * WIN), onehot, name="g_relb"))
    return loss, dh, gws, gss, dfinal, jnp.concatenate(drel, axis=0)


WEIGHTS = ("rel_bias", "ln1_g", "w_in", "b_gate", "w_a", "pool_w", "pool_scale", "w_b", "ssd_conv_w", "ssd_conv_b",
           "ssd_dt_bias", "ssd_a_log", "ssd_d", "ssd_norm_w", "w_c", "w_o", "ln2_g", "ffn_w_up", "ffn_conv_w",
           "ffn_conv_b", "ffn_w_down", "final_g")
BIG_NAMES = tuple(n for n, _, _ in BIG)
SHARDED_SMALL = {"ssd_conv_w": XBC // 4, "ffn_conv_w": 2 * D_FF // 4}


def _to_rows(flat):
    n = flat.shape[0]
    rows = -(-n // LANES)
    rows = -(-rows // 8) * 8
    return jnp.pad(flat, (0, rows * LANES - n)).reshape(rows, LANES)


def _flatten(tree, names):
    return jnp.concatenate([tree[n].reshape(-1) for n in names])


def _unflatten(flat, shapes, names):
    out, o = {}, 0
    for n in names:
        k = math.prod(shapes[n])
        out[n] = flat[o:o + k].reshape(shapes[n])
        o += k
    return out


def kernel(x, rel_bias, ln1_g, w_in, b_gate, w_a, pool_w, pool_scale, w_b, ssd_conv_w, ssd_conv_b, ssd_dt_bias, ssd_a_log, ssd_d, ssd_norm_w, w_c, w_o, ln2_g, ffn_w_up, ffn_conv_w, ffn_conv_b, ffn_w_down, final_g, loss_target, m_rel_bias, m_ln1_g, m_w_in, m_b_gate, m_w_a, m_pool_w, m_pool_scale, m_w_b, m_ssd_conv_w, m_ssd_conv_b, m_ssd_dt_bias, m_ssd_a_log, m_ssd_d, m_ssd_norm_w, m_w_c, m_w_o, m_ln2_g, m_ffn_w_up, m_ffn_conv_w, m_ffn_conv_b, m_ffn_w_down, m_final_g, v_rel_bias, v_ln1_g, v_w_in, v_b_gate, v_w_a, v_pool_w, v_pool_scale, v_w_b, v_ssd_conv_w, v_ssd_conv_b, v_ssd_dt_bias, v_ssd_a_log, v_ssd_d, v_ssd_norm_w, v_w_c, v_w_o, v_ln2_g, v_ffn_w_up, v_ffn_conv_w, v_ffn_conv_b, v_ffn_w_down, v_final_g):
    W = dict(rel_bias=rel_bias, ln1_g=ln1_g, w_in=w_in, b_gate=b_gate, w_a=w_a, pool_w=pool_w, pool_scale=pool_scale,
             w_b=w_b, ssd_conv_w=ssd_conv_w, ssd_conv_b=ssd_conv_b, ssd_dt_bias=ssd_dt_bias, ssd_a_log=ssd_a_log,
             ssd_d=ssd_d, ssd_norm_w=ssd_norm_w, w_c=w_c, w_o=w_o, ln2_g=ln2_g, ffn_w_up=ffn_w_up,
             ffn_conv_w=ffn_conv_w, ffn_conv_b=ffn_conv_b, ffn_w_down=ffn_w_down, final_g=final_g)
    M = dict(rel_bias=m_rel_bias, ln1_g=m_ln1_g, w_in=m_w_in, b_gate=m_b_gate, w_a=m_w_a, pool_w=m_pool_w,
             pool_scale=m_pool_scale, w_b=m_w_b, ssd_conv_w=m_ssd_conv_w, ssd_conv_b=m_ssd_conv_b,
             ssd_dt_bias=m_ssd_dt_bias, ssd_a_log=m_ssd_a_log, ssd_d=m_ssd_d, ssd_norm_w=m_ssd_norm_w, w_c=m_w_c,
             w_o=m_w_o, ln2_g=m_ln2_g, ffn_w_up=m_ffn_w_up, ffn_conv_w=m_ffn_conv_w, ffn_conv_b=m_ffn_conv_b,
             ffn_w_down=m_ffn_w_down, final_g=m_final_g)
    V = dict(rel_bias=v_rel_bias, ln1_g=v_ln1_g, w_in=v_w_in, b_gate=v_b_gate, w_a=v_w_a, pool_w=v_pool_w,
             pool_scale=v_pool_scale, w_b=v_w_b, ssd_conv_w=v_ssd_conv_w, ssd_conv_b=v_ssd_conv_b,
             ssd_dt_bias=v_ssd_dt_bias, ssd_a_log=v_ssd_a_log, ssd_d=v_ssd_d, ssd_norm_w=v_ssd_norm_w, w_c=v_w_c,
             w_o=v_w_o, ln2_g=v_ln2_g, ffn_w_up=v_ffn_w_up, ffn_conv_w=v_ffn_conv_w, ffn_conv_b=v_ffn_conv_b,
             ffn_w_down=v_ffn_w_down, final_g=v_final_g)
    nl = ln1_g.shape[0]
    px, py, pc_ = _position()
    chip = 2 * px + py
    cidx = jnp.reshape(pc_, (1,)).astype(jnp.int32)
    chip_idx = jnp.reshape(chip, (1,)).astype(jnp.int32)

    placed = {}
    for n, cs in SHARDED_SMALL.items():
        full = jnp.zeros(W[n].shape[:-1] + (4 * cs,), F32)
        full = lax.dynamic_update_slice(full, W[n], (0, 0, chip * cs))
        placed[n] = jnp.where(pc_ == 0, full, 0.0)
    names_sh = tuple(SHARDED_SMALL)
    shapes_sh = {n: placed[n].shape for n in names_sh}
    got = _all_reduce_small(_to_rows(_flatten(placed, names_sh)), "gather_small")
    small = {n: W[n] for n in SMALL_LAYER}
    small.update(_unflatten(got.reshape(-1), shapes_sh, names_sh))

    packs = _pack_blocks({n: W[n] for n in BIG_NAMES}, BF16)

    half = PACK_PAD // 2
    units = half // 16

    def share(weights, total):
        tot = sum(weights.values())
        return {n: math.ceil(total * v / tot) for n, v in weights.items()}

    gathers = {}

    def gather(i):
        if i not in gathers:
            buf = lax.dynamic_update_slice(lax.empty((4, PACK_PAD, D), BF16), packs[i][None], (chip, 0, 0))
            gathers[i] = _Stream(packs[i], buf, functools.partial(_gather_parts, half), 6, units, "gather_w")
        return gathers[i]

    def layer_full(i):
        return _operands(gather(i).drain())

    fwd_share = share(dict(in_a=63, in_c=31, in_d=44, mm_up=83, mm_down=34), units)

    def fwd_hooks(i):
        if i + 1 >= nl:
            return lambda name: None
        return lambda name: gather(i + 1).hook(fwd_share[name]) if name in fwd_share else None

    exchanges = {}
    bwd_share = share(dict(g_down=38, d_u2_v=60, g_up_a=35, g_up_v=35, d_merged=50, d_u_a=85, d_u_wgate=70,
                           g_in_wgate=36), units)

    class Exchange:
        def __init__(self, g):
            self.g = g
            self.pair = _Stream(g, lax.empty((4, half, D), BF16), functools.partial(_rs_pair_parts, half), 1, units, "rs_pair")
            self.hsum = self.chips = None

        def to_chips(self):
            if self.chips is None:
                self.hsum = _rs_add_pair(self.g, self.pair.drain(), cidx, "rs_add_pair")
                self.chips = _Stream(self.hsum, lax.empty((3, half, D), BF16), _rs_chip_parts, 3, units, "rs_chips")
            return self.chips

    def after_bwd(i, gw):
        exchanges[i] = Exchange(_pack_operands(gw, BF16))

    def bwd_hooks(i):
        if i + 1 >= nl:
            return lambda name: None

        def hk(name):
            if name == "d_f":
                return exchanges[i + 1].pair.hook(units)
            return exchanges[i + 1].to_chips().hook(bwd_share[name]) if name in bwd_share else None

        return hk

    loss, dx, gws, gss, dfinal, drel = _local_step(x[0], loss_target[0], rel_bias, final_g, layer_full, small,
                                                   fwd_hooks, bwd_hooks, after_bwd)

    def reduced(i):
        recv3 = exchanges[i].to_chips().drain()
        r = _rs_add_chips(exchanges[i].hsum, recv3, chip_idx, "rs_add_chips")
        other = _rs_swap(r, "rs_swap")
        both = jnp.concatenate([jnp.where(pc_ == 0, r, other), jnp.where(pc_ == 0, other, r)], axis=0)
        return _unpack_blocks(both)

    red = [reduced(i) for i in range(nl)]
    delta, new_m, new_v, grads = {}, {}, {}, {}
    for n in BIG_NAMES:
        shp = W[n].shape
        r2 = lambda a: a.reshape(-1, shp[-1])
        grads[n] = jnp.stack([red[i][n] for i in range(nl)], axis=0)
        res = _adamw(r2(W[n]), r2(grads[n]), r2(M[n]), r2(V[n]), "adamw_" + n)
        delta[n], new_m[n], new_v[n] = [a.reshape(shp) for a in res]

    sg = {}
    for n in SMALL_LAYER:
        sg[n] = jnp.stack([gss[i][n] for i in range(nl)], axis=0)
    for n in ("ssd_dt_bias", "ssd_a_log", "ssd_d"):
        sg[n] = sg[n][:, 0, :SSD_HEADS]
    sg["rel_bias"] = drel[:, :REL_BUCKETS].T
    sg["final_g"] = dfinal.reshape(D)
    sg["loss"] = loss[0, :1]
    names_sg = tuple(sg)
    shapes_sg = {n: ((nl,) + W[n].shape[1:] if n in SMALL_LAYER and n not in SHARDED_SMALL else
                     (placed[n].shape if n in SHARDED_SMALL else sg[n].shape)) for n in names_sg}
    for n in names_sg:
        sg[n] = sg[n].reshape(shapes_sg[n])
    tot = _all_reduce_small(_to_rows(_flatten(sg, names_sg)), "allreduce_small")
    tot = _unflatten(tot.reshape(-1), shapes_sg, names_sg)
    loss_out = tot.pop("loss").reshape(())
    for n, cs in SHARDED_SMALL.items():
        tot[n] = lax.dynamic_slice(tot[n], (0, 0, chip * cs), tot[n].shape[:-1] + (cs,))
    grads.update(tot)

    names_s = tuple(n for n in WEIGHTS if n not in BIG_NAMES)
    shapes_s = {n: W[n].shape for n in names_s}
    pk = lambda t: _to_rows(_flatten(t, names_s))
    dl, m2, v2 = _adamw(pk(W), pk(grads), pk(M), pk(V), "adamw_small")
    delta.update(_unflatten(dl.reshape(-1), shapes_s, names_s))
    new_m.update(_unflatten(m2.reshape(-1), shapes_s, names_s))
    new_v.update(_unflatten(v2.reshape(-1), shapes_s, names_s))

    return (loss_out, dx[None], *[grads[n] for n in WEIGHTS], *[delta[n] for n in WEIGHTS],
            *[new_m[n] for n in WEIGHTS], *[new_v[n] for n in WEIGHTS])
```

```python
import functools
import math

import jax
import jax.numpy as jnp
from jax import lax
from jax.experimental import pallas as pl
from jax.experimental.pallas import tpu as pltpu

F32 = jnp.float32
BF16 = jnp.bfloat16
MESH = pl.DeviceIdType.MESH

D = 1024
HD = 64
GW = 384
AW = 3 * GW
WIN = 128
DILATIONS = (1, 4, 16)
REL_BUCKETS = 32
REL_MAX_DISTANCE = 2048
POOL_WINDOWS = (2, 4, 8, 16)
PG = 256
SSD_HEADS = 16
SSD_N = 128
SSD_CHUNK = 128
XBC = 1536
D_FF = 2816
EPS = 1e-6
NEG = -1e30
HALO = 16
LANES = 128

SEC_A = 3 * AW
SEC_B = D
SEC_C = D + XBC
SEC_D = 3328
SEC_A_PAD = 3584
IN_WIDTH = SEC_A + SEC_B + SEC_C + 16 + 3 * D

ADAM_LR = 0.001
ADAM_B1 = 0.9
ADAM_B2 = 0.999
ADAM_EPS = 1e-08
ADAM_WD = 0.01
ADAM_STEP = 10
ADAM_TILE = 256 * 1024
MM_VMEM_BYTES = 40 * 1024 * 1024
MM_MAX_OUT_TILE = 1024 * 1024
HBM_BYTES_PER_US = 2.0e6
STEP_US = 0.35
MXU_WIDTH = 256
MXU_FLOPS_PER_US = 0.65e6


_ANY = pl.BlockSpec(memory_space=pl.ANY)


def _pick(d, cands):
    for t in cands:
        if d % t == 0:
            return t
    return d


def _iota(shape, dim):
    return lax.broadcasted_iota(jnp.int32, shape, dim)


def _dg(a, b, ca, cb):
    return lax.dot_general(a.astype(BF16), b.astype(BF16), (((ca,), (cb,)), ((), ())),
                           preferred_element_type=F32)


@jax.custom_vjp
def _bdot_nn(a, b):
    return _dg(a, b, 1, 0)


def _nn_fwd(a, b):
    return _dg(a, b, 1, 0), (a, b)


def _nn_bwd(res, g):
    a, b = res
    return _dg(g, b, 1, 1), _dg(a, g, 0, 0)


_bdot_nn.defvjp(_nn_fwd, _nn_bwd)


@jax.custom_vjp
def _bdot_nt(a, b):
    return _dg(a, b, 1, 1)


def _nt_fwd(a, b):
    return _dg(a, b, 1, 1), (a, b)


def _nt_bwd(res, g):
    a, b = res
    return _dg(g, b, 1, 0), _dg(g, a, 0, 0)


_bdot_nt.defvjp(_nt_fwd, _nt_bwd)


@jax.custom_vjp
def _bdot_tn(a, b):
    return _dg(a, b, 0, 0)


def _tn_fwd(a, b):
    return _dg(a, b, 0, 0), (a, b)


def _tn_bwd(res, g):
    a, b = res
    return _dg(b, g, 1, 1), _dg(a, g, 1, 0)


_bdot_tn.defvjp(_tn_fwd, _tn_bwd)


def _fdot(a, b):
    return jnp.dot(a, b, preferred_element_type=F32, precision=lax.Precision.HIGHEST)


def _sigmoid(x):
    return 0.5 * jnp.tanh(0.5 * x) + 0.5


def _silu(x):
    return x * _sigmoid(x)


def _softplus(x):
    return jnp.maximum(x, 0.0) + jnp.log(1.0 + jnp.exp(-jnp.abs(x)))


def _lane_pick(m, h):
    return jnp.sum(jnp.where(_iota(m.shape, 1) == h, m, 0.0), axis=1, keepdims=True)


def _row_pick(m, h):
    return jnp.sum(jnp.where(_iota(m.shape, 0) == h, m, 0.0), axis=0, keepdims=True)


def _stack_rows(rows, n):
    c = rows[0].shape[1]
    r = _iota((n, c), 0)
    out = jnp.zeros((n, c), F32)
    for k, v in enumerate(rows):
        out = out + jnp.where(r == k, v, 0.0)
    return out


def _mm(a, b, *, ta=False, tb=False, add=None, out_dtype=F32, name, hook=None):
    if ta:
        K, M = a.shape
    else:
        M, K = a.shape
    if tb:
        N, Kb = b.shape
    else:
        Kb, N = b.shape
    assert K == Kb, (a.shape, b.shape, ta, tb)
    tm, tn, tk = _mm_tiles(M, N, K, a.dtype.itemsize, b.dtype.itemsize, jnp.dtype(out_dtype).itemsize,
                           0 if add is None else add.dtype.itemsize)
    ni, nj, nk = M // tm, N // tn, K // tk
    ca = 0 if ta else 1
    cb = 1 if tb else 0
    n_in = 2 if add is None else 3
    n_hin = 0 if hook is None else len(hook.inputs)
    n_hout = 0 if hook is None else len(hook.out_shapes)

    def body(*refs):
        a_ref, b_ref = refs[:2]
        add_ref = None if add is None else refs[2]
        o_ref = refs[n_in + n_hin]
        scr = refs[n_in + n_hin + 1 + n_hout:]
        acc_ref = scr[0] if nk > 1 else None
        hargs = (refs[n_in:n_in + n_hin], refs[n_in + n_hin + 1:n_in + n_hin + 1 + n_hout], scr[1 if nk > 1 else 0:])
        i, j, k = pl.program_id(0), pl.program_id(1), pl.program_id(2)
        if hook is not None:
            @pl.when((i == 0) & (j == 0) & (k == 0))
            def _():
                hook.start(*hargs)

        part = _dg(a_ref[...], b_ref[...], ca, cb)

        def finish(r):
            if add_ref is not None:
                r = r + add_ref[...].astype(F32)
            o_ref[...] = r.astype(o_ref.dtype)

        if nk == 1:
            finish(part)
        else:
            @pl.when(k == 0)
            def _():
                acc_ref[...] = part

            @pl.when((k > 0) & (k < nk - 1))
            def _():
                acc_ref[...] += part

            @pl.when(k == nk - 1)
            def _():
                finish(acc_ref[...] + part)

        if hook is not None:
            @pl.when((i == ni - 1) & (j == nj - 1) & (k == nk - 1))
            def _():
                hook.finish(*hargs)

    a_spec = pl.BlockSpec((tk, tm), lambda i, j, k: (k, i)) if ta else pl.BlockSpec((tm, tk), lambda i, j, k: (i, k))
    b_spec = pl.BlockSpec((tn, tk), lambda i, j, k: (j, k)) if tb else pl.BlockSpec((tk, tn), lambda i, j, k: (k, j))
    in_specs = [a_spec, b_spec]
    args = [a, b]
    if add is not None:
        in_specs.append(pl.BlockSpec((tm, tn), lambda i, j, k: (i, j)))
        args.append(add)
    out_specs = [pl.BlockSpec((tm, tn), lambda i, j, k: (i, j))]
    out_shape = [jax.ShapeDtypeStruct((M, N), out_dtype)]
    scratch = [pltpu.VMEM((tm, tn), F32)] if nk > 1 else []
    aliases = {}
    if hook is not None:
        in_specs += [_ANY] * n_hin
        args += list(hook.inputs)
        out_specs += [_ANY] * n_hout
        out_shape += list(hook.out_shapes)
        scratch += list(hook.scratch)
        aliases = {n_in + hi: 1 + ho for hi, ho in hook.aliases.items()}
    sem = ("parallel", "parallel", "arbitrary") if hook is None else ("arbitrary",) * 3
    res = pl.pallas_call(
        body, name=name, grid=(ni, nj, nk), in_specs=in_specs, out_specs=out_specs, out_shape=out_shape,
        scratch_shapes=scratch, input_output_aliases=aliases,
        compiler_params=pltpu.CompilerParams(dimension_semantics=sem),
    )(*args)
    if hook is not None:
        hook.done(res[1:])
    return res[0]


def _wide(v):
    return v.astype(F32) if v.dtype == BF16 else v


def _mmf(a, b, *, tb=False, add=None, pre=None, post=None, out_dtype=F32, name, tm, hook=None):
    a_list = list(a) if isinstance(a, (list, tuple)) else [a]
    b_list = list(b) if isinstance(b, (list, tuple)) else [b]
    assert len(a_list) == len(b_list) and (len(b_list) == 1 or not (tb or pre))
    b = b_list[0]
    if tb:
        N, K = b.shape
    else:
        K, N = b.shape
    M = pre[1][0].shape[0] if pre else a_list[0].shape[0]
    tn = N if post or N <= 1024 else _pick(N, (512, 256, LANES))
    ni, nj = M // tm, N // tn
    cb = 1 if tb else 0
    pre_fn, pre_rows, pre_consts = pre if pre else (None, [], [])
    post_fn, post_rows, post_consts, post_outs, post_accs = post if post else (None, [], [], [], [])
    hook_in = [] if hook is None else list(hook.inputs)
    hook_out = [] if hook is None else list(hook.out_shapes)

    def row_spec(arr):
        return pl.BlockSpec((tm, arr.shape[1]), lambda i, j: (i, 0))

    def const_spec(arr):
        return pl.BlockSpec(arr.shape, lambda i, j, nd=arr.ndim: (0,) * nd)

    args, in_specs = [], []
    for arr in (a_list if not pre else pre_rows):
        args.append(arr)
        in_specs.append(row_spec(arr))
    for arr in pre_consts:
        args.append(arr)
        in_specs.append(const_spec(arr))
    for arr in b_list:
        args.append(arr)
        in_specs.append(pl.BlockSpec((tn, K), lambda i, j: (j, 0)) if tb else
                        pl.BlockSpec((arr.shape[0], tn), lambda i, j: (0, j)))
    if add is not None:
        args.append(add)
        in_specs.append(pl.BlockSpec((tm, tn), lambda i, j: (i, j)))
    for arr in post_rows:
        args.append(arr)
        in_specs.append(row_spec(arr))
    for arr in post_consts:
        args.append(arr)
        in_specs.append(const_spec(arr))
    n_main = len(args)
    args += hook_in
    in_specs += [_ANY] * len(hook_in)

    out_shape, out_specs = [], []
    if post:
        for c, dt in post_outs:
            out_shape.append(jax.ShapeDtypeStruct((M, c), dt))
            out_specs.append(pl.BlockSpec((tm, c), lambda i, j: (i, 0)))
        for r, c in post_accs:
            out_shape.append(jax.ShapeDtypeStruct((r, c), F32))
            out_specs.append(pl.BlockSpec((r, c), lambda i, j: (0, 0)))
    else:
        out_shape.append(jax.ShapeDtypeStruct((M, N), out_dtype))
        out_specs.append(pl.BlockSpec((tm, tn), lambda i, j: (i, j)))
    if pre:
        out_shape.append(jax.ShapeDtypeStruct((M, K), BF16))
        out_specs.append(pl.BlockSpec((tm, K), lambda i, j: (i, 0)))
    n_out = len(out_shape)
    out_shape += hook_out
    out_specs += [_ANY] * len(hook_out)
    scratch = ([pltpu.VMEM((tm, K), BF16)] if pre else []) + ([] if hook is None else list(hook.scratch))
    aliases = {} if hook is None else {n_main + hi: n_out + ho for hi, ho in hook.aliases.items()}

    def body(*refs):
        ins, outs, scr = refs[:n_main], refs[len(args):len(args) + n_out], refs[len(args) + len(out_shape):]
        hargs = (refs[n_main:len(args)], refs[len(args) + n_out:len(args) + len(out_shape)], scr[1 if pre else 0:])
        i, j = pl.program_id(0), pl.program_id(1)
        if hook is not None:
            @pl.when((i == 0) & (j == 0))
            def _():
                hook.start(*hargs)

        it = iter(ins)
        if pre:
            rows_ = [next(it) for _ in pre_rows]
            consts_ = [next(it) for _ in pre_consts]

            @pl.when(j == 0)
            def _():
                av = pre_fn(*[_wide(r[...]) for r in rows_], *[_wide(r[...]) for r in consts_]).astype(BF16)
                scr[0][...] = av
                outs[-1][...] = av

            ats = [scr[0][...]]
        else:
            ats = [next(it)[...] for _ in a_list]
        p = None
        for at in ats:
            part = _dg(at, next(it)[...], 1, cb)
            p = part if p is None else p + part
        if add is not None:
            p = p + next(it)[...].astype(F32)
        if post:
            rows_ = [next(it) for _ in post_rows]
            consts_ = [next(it) for _ in post_consts]
            res = post_fn(p, *[_wide(r[...]) for r in rows_], *[_wide(r[...]) for r in consts_])
            for r, v in zip(outs[:len(post_outs)], res[:len(post_outs)]):
                r[...] = v.astype(r.dtype)
            for r, v in zip(outs[len(post_outs):], res[len(post_outs):]):
                @pl.when(i == 0)
                def _(r=r, v=v):
                    r[...] = v

                @pl.when(i > 0)
                def _(r=r, v=v):
                    r[...] += v
        else:
            outs[0][...] = p.astype(outs[0].dtype)
        if hook is not None:
            @pl.when((i == ni - 1) & (j == nj - 1))
            def _():
                hook.finish(*hargs)

    res = pl.pallas_call(
        body, name=name, grid=(ni, nj), in_specs=in_specs, out_specs=out_specs, out_shape=out_shape,
        scratch_shapes=scratch, input_output_aliases=aliases,
        compiler_params=pltpu.CompilerParams(dimension_semantics=("arbitrary", "arbitrary")),
    )(*args)
    if hook is not None:
        hook.done(res[n_out:])
    return res[:n_out]


def _mm_tiles(M, N, K, sa, sb, so, sadd):
    def tiles(d):
        return [t for t in range(LANES, min(d, 2048) + 1, LANES) if d % t == 0] or [d]

    best = None
    for tk in [K] + [t for t in tiles(K) if t < K]:
        for tm in tiles(M):
            for tn in tiles(N):
                vmem = 2 * (tm * tk * sa + tk * tn * sb + tm * tn * (so + sadd)) + (tm * tn * 4 if tk < K else 0)
                if vmem > MM_VMEM_BYTES or tm * tn > MM_MAX_OUT_TILE:
                    continue
                a_reads = 1 if tk == K else N // tn
                traffic = M * K * sa * a_reads + K * N * sb * (M // tm) + M * N * (so + sadd)
                steps = (M // tm) * (N // tn) * (K // tk)
                width = -(-tn // MXU_WIDTH) * MXU_WIDTH
                mxu = 2.0 * M * K * N * (width / tn) / MXU_FLOPS_PER_US
                edge = tm * tk * sa + tk * tn * sb + tm * tn * (so + sadd)
                cost = max(traffic / HBM_BYTES_PER_US, mxu) + steps * STEP_US + edge / HBM_BYTES_PER_US
                if best is None or cost < best[0]:
                    best = (cost, tm, tn, tk)
    assert best is not None, (M, N, K)
    return best[1:]


class _Hook:
    def __init__(self, inputs, out_shapes, aliases, scratch, start, finish, done):
        self.inputs, self.out_shapes, self.aliases, self.scratch = inputs, out_shapes, aliases, scratch
        self.start, self.finish, self.done = start, finish, done


class _Ctx:
    def __init__(self, first, last, row0, rows):
        self.first, self.last, self.row0, self.rows = first, last, row0, rows


def _rows(name, fn, ins, outs, accs=(), *, tm, nrows, ncol=1):
    nt = nrows // tm
    hb = tm // HALO
    nh = nrows // HALO
    ins = [(kind, arr, arr.shape[1] if kind == "row" and cw is None else cw, base) for kind, arr, cw, base in ins]
    in_specs, args = [], []
    for kind, arr, cw, base in ins:
        if kind == "row":
            in_specs.append(pl.BlockSpec((tm, cw), lambda j, i, base=base: (i, base + j)))
        elif kind == "prev":
            in_specs.append(pl.BlockSpec((HALO, cw), lambda j, i, base=base: (jnp.maximum(i * hb - 1, 0), base + j)))
        elif kind == "next":
            in_specs.append(pl.BlockSpec((HALO, cw), lambda j, i, base=base: (jnp.minimum((i + 1) * hb, nh - 1), base + j)))
        elif kind in ("const", "raw"):
            in_specs.append(pl.BlockSpec(arr.shape, lambda j, i, nd=arr.ndim: (0,) * nd))
        elif kind == "ccol":
            in_specs.append(pl.BlockSpec((arr.shape[0], cw), lambda j, i, base=base: (0, base + j)))
        else:
            raise ValueError(kind)
        args.append(arr)
    out_specs, out_shape = [], []
    for ctot, cw, base, dt in outs:
        out_specs.append(pl.BlockSpec((tm, cw), lambda j, i, base=base: (i, base + j)))
        out_shape.append(jax.ShapeDtypeStruct((nrows, ctot), dt))
    for r, ctot, cw in accs:
        out_specs.append(pl.BlockSpec((r, cw), lambda j, i: (0, j)))
        out_shape.append(jax.ShapeDtypeStruct((r, ctot), F32))
    n_in, n_out = len(ins), len(outs)

    def body(*refs):
        i = pl.program_id(1)
        in_refs, out_refs, acc_refs = refs[:n_in], refs[n_in:n_in + n_out], refs[n_in + n_out:]
        if acc_refs:
            @pl.when(i == 0)
            def _():
                for r in acc_refs:
                    r[...] = jnp.zeros_like(r)

        vals = [r[...] if s[0] == "raw" else _wide(r[...]) for r, s in zip(in_refs, ins)]
        res = fn(_Ctx(i == 0, i == nt - 1, i * tm, tm), *vals)
        for r, v in zip(out_refs, res[:n_out]):
            r[...] = v.astype(r.dtype)
        for r, v in zip(acc_refs, res[n_out:]):
            r[...] += v

    res = pl.pallas_call(
        body, name=name, grid=(ncol, nt), in_specs=in_specs, out_specs=out_specs, out_shape=out_shape,
        compiler_params=pltpu.CompilerParams(dimension_semantics=("arbitrary", "arbitrary")),
    )(*args)
    return res


def _shift_down(xcat, k):
    return xcat if k == 0 else pltpu.roll(xcat, k, 0)


def _shift_up(xcat, k):
    return xcat if k == 0 else pltpu.roll(xcat, xcat.shape[0] - k, 0)


def _with_prev(ctx, halo, x):
    return jnp.concatenate([jnp.where(ctx.first, 0.0, halo), x], axis=0)


def _with_next(ctx, x, halo):
    return jnp.concatenate([x, jnp.where(ctx.last, 0.0, halo)], axis=0)


def _rms_core(x, g):
    r = lax.rsqrt(jnp.mean(x * x, axis=-1, keepdims=True) + EPS)
    return x * r * g


def _rms_post(du, xv, drv, gv):
    _, vjp = jax.vjp(_rms_core, xv, gv)
    dx, dg = vjp(du)
    return [drv + dx, dg]


def _final_loss(x, target, g):
    S = x.shape[0]

    def fn(ctx, xv, tv, gv):
        def f(xx, gg):
            err = _rms_core(xx, gg) - tv
            return 0.5 * jnp.sum(err * err) / D

        loss, vjp = jax.vjp(f, xv, gv)
        dx, dg = vjp(jnp.ones((), F32))
        return [dx, dg, jnp.zeros((1, LANES), F32) + loss]

    return _rows("final_loss", fn, [("row", x, None, 0), ("row", target, None, 0), ("const", g, None, 0)],
                 [(D, D, 0, F32)], [(1, D, D), (1, LANES, LANES)], tm=256, nrows=S)


def _attn_valid(n):
    qi = _iota((WIN, 2 * WIN), 0)
    kk = _iota((WIN, 2 * WIN), 1)
    rel = qi + WIN - kk
    return (rel >= 0) & (rel <= WIN) & ((kk >= WIN) | (n > 0))


def _attn_block(q, kp, kc, vp, vc, b0, b1):
    k = jnp.concatenate([kp, kc], axis=0)
    v = jnp.concatenate([vp, vc], axis=0)
    lo = _iota((WIN, LANES), 1) < HD
    scale = 1.0 / math.sqrt(HD)
    os_, ls_ = [], []
    for hh, b in ((0, b0), (1, b1)):
        qm = jnp.where(lo if hh == 0 else ~lo, q, 0.0)
        s = _bdot_nt(qm, k) * scale + b
        m = lax.stop_gradient(jnp.max(s, axis=1, keepdims=True))
        p = jnp.exp(s - m)
        l = jnp.sum(p, axis=1, keepdims=True)
        os_.append(_bdot_nn(p, v) / l)
        ls_.append(m + jnp.log(l))
    return jnp.where(lo, os_[0], os_[1]), jnp.where(lo, ls_[0], ls_[1])


def _residue_rows(r, d):
    return pl.ds(0, WIN) if d == 1 else pl.ds(r, WIN, stride=d)


def _for_residues(d, fn):
    if d == 1:
        fn(0, 0)
    else:
        lax.fori_loop(0, d, fn, 0, unroll=min(d, 8))


def _pairs_per_step(d):
    return 3 if d == 1 else 1


def _bias_table(rel_bias, bucket, gi, name):
    def body(t_ref, b_ref, o_ref):
        h = 6 * gi + pl.program_id(0)
        b = b_ref[...]
        acc = jnp.zeros(b.shape, F32)
        for k in range(REL_BUCKETS):
            acc = jnp.where(b == k, t_ref[k, h], acc)
        o_ref[0] = acc

    return pl.pallas_call(
        body, name=name, grid=(6,),
        in_specs=[pl.BlockSpec(memory_space=pltpu.SMEM), pl.BlockSpec((WIN, 2 * WIN), lambda h: (0, 0))],
        out_specs=pl.BlockSpec((1, WIN, 2 * WIN), lambda h: (h, 0, 0)),
        out_shape=jax.ShapeDtypeStruct((6, WIN, 2 * WIN), F32),
    )(rel_bias, bucket)


def _attn_fwd(pa, bias, gi, name):
    S = pa.shape[0]
    d = DILATIONS[gi]
    bt = WIN * d
    nb = S // bt
    hpw = _pairs_per_step(d)
    bw = hpw * LANES
    cb = 3 * gi // hpw

    def body(q_ref, kp_ref, kc_ref, vp_ref, vc_ref, b_ref, o_ref, l_ref):
        valid = _attn_valid(pl.program_id(1))
        bm = [jnp.where(valid, b_ref[k], NEG) for k in range(2 * hpw)]

        def residue(r, carry):
            sl = _residue_rows(r, d)
            for t in range(hpw):
                ln = pl.ds(t * LANES, LANES)
                o, lse = _attn_block(q_ref[sl, ln], kp_ref[sl, ln], kc_ref[sl, ln], vp_ref[sl, ln], vc_ref[sl, ln],
                                     bm[2 * t], bm[2 * t + 1])
                o_ref[sl, ln] = o
                l_ref[sl, ln] = lse
            return carry

        _for_residues(d, residue)

    def spec(off, prev):
        if prev:
            return pl.BlockSpec((bt, bw), lambda hp, n: (jnp.maximum(n - 1, 0), off // hpw + cb + hp))
        return pl.BlockSpec((bt, bw), lambda hp, n: (n, off // hpw + cb + hp))

    ospec = pl.BlockSpec((bt, bw), lambda hp, n: (n, hp))
    return pl.pallas_call(
        body, name=name, grid=(3 // hpw, nb),
        in_specs=[spec(0, False), spec(9, True), spec(9, False), spec(18, True), spec(18, False),
                  pl.BlockSpec((2 * hpw, WIN, 2 * WIN), lambda hp, n: (hp, 0, 0))],
        out_specs=[ospec, ospec],
        out_shape=[jax.ShapeDtypeStruct((S, GW), F32)] * 2,
        compiler_params=pltpu.CompilerParams(dimension_semantics=("parallel", "arbitrary")),
    )(pa, pa, pa, pa, pa, bias)


def _attn_bwd(pa, bias, do, dlse, db_in, dqkv, gi, name):
    S = pa.shape[0]
    d = DILATIONS[gi]
    bt = WIN * d
    nb = S // bt
    hpw = _pairs_per_step(d)
    bw = hpw * LANES
    cb = 3 * gi // hpw

    def body(q_ref, kp_ref, kc_ref, vp_ref, vc_ref, b_ref, do_ref, dl_ref, dbi_ref, dqi_ref, dki_ref, dvi_ref,
             dq_ref, dk_ref, dv_ref, db_ref, ck, cv):
        n = pl.program_id(1)

        @pl.when(n == 0)
        def _():
            db_ref[...] = dbi_ref[...]
            ck[...] = jnp.zeros_like(ck)
            cv[...] = jnp.zeros_like(cv)

        @pl.when(n < nb)
        def _():
            valid = _attn_valid(n)
            bm = [jnp.where(valid, b_ref[k], NEG) for k in range(2 * hpw)]

            def residue(r, carry):
                sl = _residue_rows(r, d)
                cs = pl.ds(pl.multiple_of(r * WIN, WIN), WIN)
                for t in range(hpw):
                    ln = pl.ds(t * LANES, LANES)
                    _, vjp = jax.vjp(_attn_block, q_ref[sl, ln], kp_ref[sl, ln], kc_ref[sl, ln], vp_ref[sl, ln],
                                     vc_ref[sl, ln], bm[2 * t], bm[2 * t + 1])
                    dq, dkp, dkc, dvp, dvc, db0, db1 = vjp((do_ref[sl, ln], dl_ref[sl, ln]))
                    dq_ref[sl, ln] = dq
                    dk_ref[sl, ln] = ck[cs, ln] + dkp
                    dv_ref[sl, ln] = cv[cs, ln] + dvp
                    ck[cs, ln] = dkc
                    cv[cs, ln] = dvc
                    db_ref[2 * t] += db0
                    db_ref[2 * t + 1] += db1
                return carry

            _for_residues(d, residue)

        @pl.when(n == nb)
        def _():
            def residue(r, carry):
                sl = _residue_rows(r, d)
                cs = pl.ds(pl.multiple_of(r * WIN, WIN), WIN)
                dk_ref[sl, :] = ck[cs, :]
                dv_ref[sl, :] = cv[cs, :]
                return carry

            _for_residues(d, residue)

    def cur(n):
        return jnp.minimum(n, nb - 1)

    def spec(off, prev):
        if prev:
            return pl.BlockSpec((bt, bw), lambda hp, n: (jnp.maximum(cur(n) - 1, 0), off // hpw + cb + hp))
        return pl.BlockSpec((bt, bw), lambda hp, n: (cur(n), off // hpw + cb + hp))

    gspec = pl.BlockSpec((bt, bw), lambda hp, n: (cur(n), hp))
    bspec = pl.BlockSpec((2 * hpw, WIN, 2 * WIN), lambda hp, n: (hp, 0, 0))
    qspec = pl.BlockSpec((bt, bw), lambda hp, n: (cur(n), cb + hp))
    kspec = pl.BlockSpec((bt, bw), lambda hp, n: (jnp.maximum(n - 1, 0), cb + hp))
    dq, dk, dv, db = pl.pallas_call(
        body, name=name, grid=(3 // hpw, nb + 1),
        in_specs=[spec(0, False), spec(9, True), spec(9, False), spec(18, True), spec(18, False),
                  bspec, gspec, gspec, bspec, _ANY, _ANY, _ANY],
        out_specs=[qspec, kspec, kspec, bspec],
        out_shape=[jax.ShapeDtypeStruct((S, AW), F32)] * 3 + [jax.ShapeDtypeStruct((6, WIN, 2 * WIN), F32)],
        scratch_shapes=[pltpu.VMEM((bt, bw), F32), pltpu.VMEM((bt, bw), F32)],
        input_output_aliases={9: 0, 10: 1, 11: 2},
        compiler_params=pltpu.CompilerParams(dimension_semantics=("arbitrary", "arbitrary")),
    )(pa, pa, pa, pa, pa, bias, do, dlse, db_in, *dqkv)
    return (dq, dk, dv), db


def _mix_core(o0, o1, o2, l0, l1, l2):
    m = lax.stop_gradient(jnp.maximum(jnp.maximum(l0, l1), l2))
    e0, e1, e2 = jnp.exp(l0 - m), jnp.exp(l1 - m), jnp.exp(l2 - m)
    return (e0 * o0 + e1 * o1 + e2 * o2) / (e0 + e1 + e2)


def _mix_fwd(os_, ls_, name):
    S = os_[0].shape[0]
    ins = [("row", a, None, 0) for a in (*os_, *ls_)]
    return _rows(name, lambda ctx, *v: [_mix_core(*v)], ins, [(GW, GW, 0, BF16)], tm=256, nrows=S)[0]


def _mix_bwd(os_, ls_, datt, name):
    S = datt.shape[0]

    def fn(ctx, *v):
        _, vjp = jax.vjp(_mix_core, *v[:6])
        return list(vjp(v[6]))

    ins = [("row", a, None, 0) for a in (*os_, *ls_, datt)]
    outs = [(GW, GW, 0, F32)] * 6
    r = _rows(name, fn, ins, outs, tm=256, nrows=S)
    return r[:3], r[3:]


def _t5_bucket(dist):
    max_exact = REL_BUCKETS // 2
    is_small = dist < max_exact
    nf = jnp.maximum(dist, 1).astype(F32)
    large = max_exact + (jnp.log(nf / max_exact) / math.log(REL_MAX_DISTANCE / max_exact)
                         * (REL_BUCKETS - max_exact)).astype(jnp.int32)
    large = jnp.minimum(large, REL_BUCKETS - 1)
    return jnp.where(is_small, dist, large)


def _buckets(d):
    qi = jnp.arange(WIN)[:, None]
    kk = jnp.arange(2 * WIN)[None, :]
    rel = qi + WIN - kk
    return _t5_bucket(jnp.clip(rel, 0, None) * d)


def _pool_cnt(ctx, w):
    pos = ctx.row0 + _iota((ctx.rows, PG), 0) + 1
    return jnp.minimum(pos, w).astype(F32)


def _pool_d(ctx, halo, u):
    ds = []
    for g, w in enumerate(POOL_WINDOWS):
        ug = u[:, g * PG:(g + 1) * PG]
        s = _with_prev(ctx, halo[:, g * PG:(g + 1) * PG], ug)
        step = 1
        while step < w:
            s = s + _shift_down(s, step)
            step *= 2
        ds.append(s[HALO:] / _pool_cnt(ctx, w) - ug)
    return ds


def _pool_fwd(pb, pw, scale, name):
    S = pb.shape[0]

    def fn(ctx, halo, u, w, sc):
        ds = _pool_d(ctx, halo, u)
        return [jnp.concatenate([_dg(ds[k], w[k], 1, 0) for k in range(4)], axis=1) * sc]

    return _rows(name, fn, [("prev", pb, D, 0), ("row", pb, None, 0), ("raw", pw, None, 0), ("const", scale, None, 0)],
                 [(D, D, 0, BF16)], tm=256, nrows=S)[0]


def _pool_bwd(pb, pw, scale, dpo, name):
    S = pb.shape[0]

    def fn1(ctx, halo, u, w, sc, dy):
        ds = _pool_d(ctx, halo, u)
        dyp = dy * sc
        y = jnp.concatenate([_dg(ds[k], w[k], 1, 0) for k in range(4)], axis=1)
        es, dws = [], []
        for k, wd in enumerate(POOL_WINDOWS):
            cols = slice(k * PG, (k + 1) * PG)
            es.append(_dg(dyp[:, cols], w[k], 1, 1) / _pool_cnt(ctx, wd))
            dws.append(_dg(ds[k], dyp[:, cols], 0, 0))
        return [jnp.concatenate(es, axis=1), jnp.concatenate(dws, axis=0), jnp.sum(dy * y, axis=0, keepdims=True)]

    e, dpw, dsc = _rows(name + "_a", fn1,
                        [("prev", pb, D, 0), ("row", pb, None, 0), ("raw", pw, None, 0), ("const", scale, None, 0),
                         ("row", dpo, None, 0)],
                        [(D, D, 0, F32)], [(4 * PG, PG, PG), (1, D, D)], tm=256, nrows=S)

    def fn2(ctx, ev, halo):
        outs = []
        for g, w in enumerate(POOL_WINDOWS):
            eg = ev[:, g * PG:(g + 1) * PG]
            s = _with_next(ctx, eg, halo[:, g * PG:(g + 1) * PG])
            step = 1
            while step < w:
                s = s + _shift_up(s, step)
                step *= 2
            outs.append(s[:ctx.rows] - eg * _pool_cnt(ctx, w))
        return [jnp.concatenate(outs, axis=1)]

    du = _rows(name + "_b", fn2, [("row", e, None, 0), ("next", e, D, 0)], [(D, D, 0, BF16)], tm=256, nrows=S)[0]
    return du, dpw, dsc


def _conv_taps(ctx, halo, x, K):
    cat = _with_prev(ctx, halo, x)
    return [_shift_down(cat, K - 1 - k)[HALO:] for k in range(K)]


def _conv_pre(taps, w, b):
    acc = b
    for k, t in enumerate(taps):
        acc = acc + t * _row_pick(w, k)
    return acc


CW = 256
CWS = 512
CONV_TM = 512


def _ext_taps(ctx, prev, x, nxt, K):
    cat = jnp.concatenate([jnp.where(ctx.first, 0.0, prev), x, jnp.where(ctx.last, 0.0, nxt)], axis=0)
    return [_shift_down(cat, K - 1 - k)[HALO:] for k in range(K)]


def _conv_t_rows(dp, w, K, tm):
    acc = jnp.zeros((tm, dp.shape[1]), F32)
    for k in range(K):
        acc = acc + _shift_up(dp, K - 1 - k)[:tm] * _row_pick(w, k)
    return acc


def _ssd_conv_fwd(pc, w, b, name):
    S = pc.shape[0]
    base = D // CWS

    def fn(ctx, halo, x, wv, bv):
        return [_silu(_conv_pre(_conv_taps(ctx, halo, x, 4), wv, bv))]

    return _rows(name, fn, [("prev", pc, CWS, base), ("row", pc, CWS, base), ("ccol", w, CWS, 0), ("ccol", b, CWS, 0)],
                 [(XBC, CWS, 0, F32)], tm=CONV_TM, nrows=S, ncol=XBC // CWS)[0]


def _ssd_conv_bwd(pc, w, b, dy, name):
    S = pc.shape[0]
    base = D // CWS

    def fn(ctx, prev, x, nxt, wv, bv, dyv, dyn):
        n = ctx.rows
        taps = _ext_taps(ctx, prev, x, nxt, 4)
        pre = _conv_pre(taps, wv, bv)
        sg = _sigmoid(pre)
        dye = jnp.concatenate([dyv, jnp.where(ctx.last, 0.0, dyn)], axis=0)
        dpre = dye * sg * (1.0 + pre * (1.0 - sg))
        dw = _stack_rows([jnp.sum(dpre[:n] * t[:n], axis=0, keepdims=True) for t in taps], 4)
        return [_conv_t_rows(dpre, wv, 4, n), dw, jnp.sum(dpre[:n], axis=0, keepdims=True)]

    return _rows(name, fn,
                 [("prev", pc, CWS, base), ("row", pc, CWS, base), ("next", pc, CWS, base), ("ccol", w, CWS, 0),
                  ("ccol", b, CWS, 0), ("row", dy, CWS, 0), ("next", dy, CWS, 0)],
                 [(XBC, CWS, 0, BF16)], [(4, XBC, CWS), (1, XBC, CWS)], tm=CONV_TM, nrows=S, ncol=XBC // CWS)


NFC = D_FF // CW


def _ffn_act_fwd(h, w, b, name):
    S = h.shape[0]

    def fn(ctx, ha, a, hv, v, wa, wv, ba, bv):
        pa = _conv_pre(_conv_taps(ctx, ha, a, 3), wa, ba)
        pv = _conv_pre(_conv_taps(ctx, hv, v, 3), wv, bv)
        return [_silu(pa) * pv]

    return _rows(name, fn,
                 [("prev", h, CW, 0), ("row", h, CW, 0), ("prev", h, CW, NFC), ("row", h, CW, NFC),
                  ("ccol", w, CW, 0), ("ccol", w, CW, NFC), ("ccol", b, CW, 0), ("ccol", b, CW, NFC)],
                 [(D_FF, CW, 0, BF16)], tm=CONV_TM, nrows=S, ncol=NFC)[0]


def _ffn_act_bwd(h, w, b, df, name):
    S = h.shape[0]

    def fn(ctx, pa_, a, na, pv_, v, nv, wa, wv, ba, bv, dfv, dfn):
        n = ctx.rows
        ta = _ext_taps(ctx, pa_, a, na, 3)
        tv = _ext_taps(ctx, pv_, v, nv, 3)
        pa = _conv_pre(ta, wa, ba)
        pv = _conv_pre(tv, wv, bv)
        sg = _sigmoid(pa)
        dfe = jnp.concatenate([dfv, jnp.where(ctx.last, 0.0, dfn)], axis=0)
        dpa = dfe * pv * sg * (1.0 + pa * (1.0 - sg))
        dpv = dfe * pa * sg
        res = [_conv_t_rows(dpa, wa, 3, n), _conv_t_rows(dpv, wv, 3, n)]
        for dp, taps in ((dpa, ta), (dpv, tv)):
            res.append(_stack_rows([jnp.sum(dp[:n] * t[:n], axis=0, keepdims=True) for t in taps], 3))
        for dp in (dpa, dpv):
            res.append(jnp.sum(dp[:n], axis=0, keepdims=True))
        return res

    ins = []
    for base in (0, NFC):
        ins += [("prev", h, CW, base), ("row", h, CW, base), ("next", h, CW, base)]
    ins += [("ccol", w, CW, 0), ("ccol", w, CW, NFC), ("ccol", b, CW, 0), ("ccol", b, CW, NFC),
            ("row", df, CW, 0), ("next", df, CW, 0)]
    dha, dhv, dwa, dwv, dba, dbv = _rows(
        name, fn, ins, [(D_FF, CW, 0, BF16)] * 2, [(3, D_FF, CW)] * 2 + [(1, D_FF, CW)] * 2, tm=CONV_TM, nrows=S, ncol=NFC)
    return dha, dhv, jnp.concatenate([dwa, dwv], axis=1), jnp.concatenate([dba, dbv], axis=1)


NSLAB = D // LANES
CPS = 2


def _ssd_chunk(xs, Bs, Cs, dtraw, dtb, alog, prev):
    lsz = SSD_CHUNK
    lane = _iota((lsz, LANES), 1)
    row = _iota((lsz, LANES), 0)
    dt = jnp.where(lane < SSD_HEADS, _softplus(dtraw + dtb), 0.0)
    a = dt * (-jnp.exp(alog))
    tril = row >= lane
    a_cs = _fdot(tril.astype(F32), a)
    a_cst = a_cs.T
    a_last = jnp.sum(a, axis=0, keepdims=True)
    lo = lane < HD
    top = row < HD
    cbs = [_bdot_nt(Cs[g], Bs[g]) for g in range(2)]
    ys, news = [], []
    for s in range(NSLAB):
        g = s // (NSLAB // 2)
        cols, lms, dts, als = [], [], [], []
        for hh in range(2):
            h = 2 * s + hh
            col = _lane_pick(a_cs, h)
            seg = col - _row_pick(a_cst, h)
            lms.append(jnp.exp(jnp.where(tril, seg, NEG)))
            cols.append(col)
            dts.append(_lane_pick(dt, h))
            als.append(_lane_pick(a_last, h))
        col_x = jnp.where(lo, cols[0], cols[1])
        al_x = jnp.where(lo, als[0], als[1])
        xc = xs[s] * jnp.where(lo, dts[0], dts[1])
        yd = jnp.where(lo, _bdot_nn(cbs[g] * lms[0], xc), _bdot_nn(cbs[g] * lms[1], xc))
        yoff = _bdot_nt(Cs[g], prev[s]) * jnp.exp(col_x)
        ys.append(yd + yoff)
        st = _bdot_tn(xc * jnp.exp(al_x - col_x), Bs[g])
        news.append(prev[s] * jnp.exp(jnp.where(top, als[0], als[1])) + st)
    return ys, news


def _ssd_scan_fwd(xbc_c, pd, dtb, alog, name):
    S = xbc_c.shape[0]
    nc = S // SSD_CHUNK
    rows_ = CPS * SSD_CHUNK

    def body(x_ref, b_ref, c_ref, dt_ref, dtb_ref, al_ref, y_ref, st_ref, state):
        c = pl.program_id(0)

        @pl.when(c == 0)
        def _():
            state[...] = jnp.zeros_like(state)

        prev = [state[s * LANES:(s + 1) * LANES, :] for s in range(NSLAB)]
        for u in range(CPS):
            rw = pl.ds(u * SSD_CHUNK, SSD_CHUNK)
            xs = [x_ref[rw, s * LANES:(s + 1) * LANES] for s in range(NSLAB)]
            Bs = [b_ref[rw, g * SSD_N:(g + 1) * SSD_N] for g in range(2)]
            Cs = [c_ref[rw, g * SSD_N:(g + 1) * SSD_N] for g in range(2)]
            for s in range(NSLAB):
                st_ref[u, s * LANES:(s + 1) * LANES, :] = prev[s]
            ys, prev = _ssd_chunk(xs, Bs, Cs, dt_ref[rw, :].astype(F32), dtb_ref[...], al_ref[...], prev)
            for s in range(NSLAB):
                y_ref[rw, s * LANES:(s + 1) * LANES] = ys[s]
        for s in range(NSLAB):
            state[s * LANES:(s + 1) * LANES, :] = prev[s]

    return pl.pallas_call(
        body, name=name, grid=(nc // CPS,),
        in_specs=[pl.BlockSpec((rows_, D), lambda c: (c, 0)),
                  pl.BlockSpec((rows_, 2 * SSD_N), lambda c: (c, D // (2 * SSD_N))),
                  pl.BlockSpec((rows_, 2 * SSD_N), lambda c: (c, D // (2 * SSD_N) + 1)),
                  pl.BlockSpec((rows_, LANES), lambda c: (c, 0)),
                  pl.BlockSpec((1, LANES), lambda c: (0, 0)), pl.BlockSpec((1, LANES), lambda c: (0, 0))],
        out_specs=[pl.BlockSpec((rows_, D), lambda c: (c, 0)), pl.BlockSpec((CPS, D, SSD_N), lambda c: (c, 0, 0))],
        out_shape=[jax.ShapeDtypeStruct((S, D), F32), jax.ShapeDtypeStruct((nc, D, SSD_N), F32)],
        scratch_shapes=[pltpu.VMEM((D, SSD_N), F32)],
        compiler_params=pltpu.CompilerParams(dimension_semantics=("arbitrary",)),
    )(xbc_c, xbc_c, xbc_c, pd, dtb, alog)


def _ssd_scan_bwd(xbc_c, pd, dtb, alog, states, dy, dxs_skip, name):
    S = xbc_c.shape[0]
    nc = S // SSD_CHUNK
    rows_ = CPS * SSD_CHUNK

    def body(x_ref, b_ref, c_ref, dt_ref, dtb_ref, al_ref, st_ref, dy_ref, sk_ref,
             dx_ref, ddt_ref, ddtb_ref, dal_ref, dstate):
        c = pl.program_id(0)

        @pl.when(c == 0)
        def _():
            dstate[...] = jnp.zeros_like(dstate)
            ddtb_ref[...] = jnp.zeros_like(ddtb_ref)
            dal_ref[...] = jnp.zeros_like(dal_ref)

        dnew = [dstate[s * LANES:(s + 1) * LANES, :] for s in range(NSLAB)]
        for u in reversed(range(CPS)):
            rw = pl.ds(u * SSD_CHUNK, SSD_CHUNK)
            xs = [x_ref[rw, s * LANES:(s + 1) * LANES] for s in range(NSLAB)]
            Bs = [b_ref[rw, g * SSD_N:(g + 1) * SSD_N] for g in range(2)]
            Cs = [c_ref[rw, g * SSD_N:(g + 1) * SSD_N] for g in range(2)]
            prev = [st_ref[u, s * LANES:(s + 1) * LANES, :] for s in range(NSLAB)]
            _, vjp = jax.vjp(_ssd_chunk, xs, Bs, Cs, dt_ref[rw, :].astype(F32), dtb_ref[...], al_ref[...], prev)
            dys = [dy_ref[rw, s * LANES:(s + 1) * LANES] for s in range(NSLAB)]
            dxs, dBs, dCs, ddt, ddtb, dal, dnew = vjp((dys, dnew))
            for s in range(NSLAB):
                dx_ref[rw, s * LANES:(s + 1) * LANES] = dxs[s] + sk_ref[rw, s * LANES:(s + 1) * LANES]
            for g in range(2):
                dx_ref[rw, D + g * SSD_N:D + (g + 1) * SSD_N] = dBs[g]
                dx_ref[rw, D + 2 * SSD_N + g * SSD_N:D + 2 * SSD_N + (g + 1) * SSD_N] = dCs[g]
            ddt_ref[rw, :] = ddt
            ddtb_ref[...] += ddtb
            dal_ref[...] += dal
        for s in range(NSLAB):
            dstate[s * LANES:(s + 1) * LANES, :] = dnew[s]

    def rv(c):
        return nc // CPS - 1 - c

    return pl.pallas_call(
        body, name=name, grid=(nc // CPS,),
        in_specs=[pl.BlockSpec((rows_, D), lambda c: (rv(c), 0)),
                  pl.BlockSpec((rows_, 2 * SSD_N), lambda c: (rv(c), D // (2 * SSD_N))),
                  pl.BlockSpec((rows_, 2 * SSD_N), lambda c: (rv(c), D // (2 * SSD_N) + 1)),
                  pl.BlockSpec((rows_, LANES), lambda c: (rv(c), 0)),
                  pl.BlockSpec((1, LANES), lambda c: (0, 0)), pl.BlockSpec((1, LANES), lambda c: (0, 0)),
                  pl.BlockSpec((CPS, D, SSD_N), lambda c: (rv(c), 0, 0)),
                  pl.BlockSpec((rows_, D), lambda c: (rv(c), 0)),
                  pl.BlockSpec((rows_, D), lambda c: (rv(c), 0))],
        out_specs=[pl.BlockSpec((rows_, XBC), lambda c: (rv(c), 0)),
                   pl.BlockSpec((rows_, LANES), lambda c: (rv(c), 0)),
                   pl.BlockSpec((1, LANES), lambda c: (0, 0)), pl.BlockSpec((1, LANES), lambda c: (0, 0))],
        out_shape=[jax.ShapeDtypeStruct((S, XBC), F32), jax.ShapeDtypeStruct((S, LANES), F32),
                   jax.ShapeDtypeStruct((1, LANES), F32), jax.ShapeDtypeStruct((1, LANES), F32)],
        scratch_shapes=[pltpu.VMEM((D, SSD_N), F32)],
        compiler_params=pltpu.CompilerParams(dimension_semantics=("arbitrary",)),
    )(xbc_c, xbc_c, xbc_c, pd, dtb, alog, states, dy, dxs_skip)


def _ssd_post_core(y, xs, z, d128, nw):
    tm = y.shape[0]
    ex = (_iota((LANES, D), 1) // HD == _iota((LANES, D), 0)).astype(F32)
    d_x = jnp.sum(_fdot(jnp.broadcast_to(d128, (8, LANES)), ex), axis=0, keepdims=True) * 0.125
    y2 = (y + d_x * xs) * _silu(z)
    lo = _iota((tm, D), 1) < D // 2
    sq = y2 * y2
    ms0 = jnp.sum(jnp.where(lo, sq, 0.0), axis=-1, keepdims=True) / (D // 2)
    ms1 = jnp.sum(jnp.where(lo, 0.0, sq), axis=-1, keepdims=True) / (D // 2)
    r = jnp.where(lo, lax.rsqrt(ms0 + EPS), lax.rsqrt(ms1 + EPS))
    return y2 * r * nw


def _ssd_post_ins(y, xbc_c, pc, d128, nw):
    return [("row", y, None, 0), ("row", xbc_c, D, 0), ("row", pc, D, 0), ("const", d128, None, 0), ("const", nw, None, 0)]


def _ssd_post_fwd(y, xbc_c, pc, d128, nw, name):
    S = y.shape[0]
    return _rows(name, lambda ctx, *v: [_ssd_post_core(*v)], _ssd_post_ins(y, xbc_c, pc, d128, nw),
                 [(D, D, 0, BF16)], tm=256, nrows=S)[0]


def _ssd_post_bwd(y, xbc_c, pc, d128, nw, dout, name):
    S = y.shape[0]

    def fn(ctx, *v):
        _, vjp = jax.vjp(_ssd_post_core, *v[:5])
        return list(vjp(v[5]))

    return _rows(name, fn, _ssd_post_ins(y, xbc_c, pc, d128, nw) + [("row", dout, None, 0)],
                 [(D, D, 0, F32), (D, D, 0, F32), (D, D, 0, BF16)], [(1, LANES, LANES), (1, D, D)], tm=256, nrows=S)


def _gates_core(g0, g1, g2, b0, b1, b2, ya, yb, yc):
    return _sigmoid(g0 + b0) * ya + _sigmoid(g1 + b1) * yb + _sigmoid(g2 + b2) * yc


def _gate_parts(pdv, bv):
    gp = pltpu.roll(pdv, SEC_D - 16, 1)
    return [gp[:, k * D:(k + 1) * D] for k in range(3)] + [bv[:, k * D:(k + 1) * D] for k in range(3)]


def _gates_fwd(pd, bg, ya, yb, yc, name):
    S = pd.shape[0]

    def fn(ctx, pdv, bv, a, b, c):
        return [_gates_core(*_gate_parts(pdv, bv), a, b, c)]

    return _rows(name, fn, [("row", pd, None, 0), ("const", bg, None, 0), ("row", ya, None, 0), ("row", yb, None, 0),
                            ("row", yc, None, 0)], [(D, D, 0, BF16)], tm=256, nrows=S)[0]


def _gates_post(dm, pdv, a, b, c, bv):
    _, vjp = jax.vjp(_gates_core, *_gate_parts(pdv, bv), a, b, c)
    g = vjp(dm)
    return [g[6], g[7], g[8], jnp.concatenate(g[0:3], axis=1), jnp.concatenate(g[3:6], axis=1)]


def _adam_update(wv, gv, mv, vv):
    m2 = ADAM_B1 * mv + (1.0 - ADAM_B1) * gv
    v2 = ADAM_B2 * vv + (1.0 - ADAM_B2) * jnp.square(gv)
    m_hat = m2 / (1.0 - ADAM_B1 ** ADAM_STEP)
    v_hat = v2 / (1.0 - ADAM_B2 ** ADAM_STEP)
    delta = -ADAM_LR * (m_hat / (jnp.sqrt(v_hat) + ADAM_EPS) + ADAM_WD * wv)
    return [delta, m2, v2]


def _adamw(w, g, m, v, name):
    rows, C = w.shape
    tm = _pick(rows, [t for t in (512, 256, 128, 64, 32, 16, 8) if t * C <= ADAM_TILE])
    return _rows(name, lambda ctx, *a: _adam_update(*a), [("row", a, None, 0) for a in (w, g, m, v)],
                 [(C, C, 0, F32)] * 3, tm=tm, nrows=rows)


def _position():
    return lax.axis_index("x"), lax.axis_index("y"), lax.axis_index("c")


def _other_chips(x, y):
    return [(1 - x, y), (x, 1 - y), (1 - x, 1 - y)]


_HBM = pl.BlockSpec(memory_space=pltpu.HBM)


def _gather_parts(half, lo, n):
    def copies(p_ref, out_ref, send_sems, recv_sems):
        x, y, c = _position()
        sibling = (x, y, 1 - c)
        chips = _other_chips(x, y)

        def slab(chip, h):
            return out_ref.at[2 * chip[0] + chip[1], pl.ds(h * half + lo, n), :]

        def copy(k, src, dst, to):
            return pltpu.make_async_remote_copy(src_ref=src, dst_ref=dst, send_sem=send_sems.at[k],
                                                recv_sem=recv_sems.at[k], device_id=to, device_id_type=MESH)

        first = [copy(j, p_ref.at[pl.ds(c * half + lo, n), :], slab((x, y), c), (*chip, c)) for j, chip in enumerate(chips)]
        passed = [copy(3 + j, slab(chip, c), slab(chip, c), sibling) for j, chip in enumerate(chips)]
        from_chips = [copy(j, slab(chip, c), slab(chip, c), (x, y, c)) for j, chip in enumerate(chips)]
        from_sibling = [copy(3 + j, slab(chip, 1 - c), slab(chip, 1 - c), (x, y, c)) for j, chip in enumerate(chips)]
        return first, passed, from_chips, from_sibling

    def start(ins, outs, scr):
        for cp in copies(ins[0], outs[0], *scr)[0]:
            cp.start()

    def finish(ins, outs, scr):
        first, passed, from_chips, from_sibling = copies(ins[0], outs[0], *scr)
        for j in range(3):
            from_chips[j].wait_recv()
            passed[j].start()
        for cp in from_sibling:
            cp.wait_recv()
        for cp in first + passed:
            cp.wait_send()

    return start, finish


def _rs_chip_parts(lo, n):
    def copies(h_ref, out_ref, send_sems, recv_sems):
        x, y, c = _position()
        return [pltpu.make_async_remote_copy(src_ref=h_ref.at[2 * chip[0] + chip[1], pl.ds(lo, n), :],
                                             dst_ref=out_ref.at[j, pl.ds(lo, n), :],
                                             send_sem=send_sems.at[j], recv_sem=recv_sems.at[j],
                                             device_id=(*chip, c), device_id_type=MESH)
                for j, chip in enumerate(_other_chips(x, y))]

    def start(ins, outs, scr):
        for cp in copies(ins[0], outs[0], *scr):
            cp.start()

    def finish(ins, outs, scr):
        for cp in copies(ins[0], outs[0], *scr):
            cp.wait()

    return start, finish


class _Stream:
    def __init__(self, src, buf, parts, nsem, units, name):
        self.src, self.buf, self.parts, self.nsem, self.name = src, buf, parts, nsem, name
        self.next, self.units = 0, units

    def _scratch(self):
        return [pltpu.SemaphoreType.DMA((self.nsem,)), pltpu.SemaphoreType.DMA((self.nsem,))]

    def _take(self, units):
        units = min(units, self.units - self.next)
        lo = self.next * 16
        self.next += units
        return lo, units * 16

    def _set(self, outs):
        self.buf = outs[0]

    def hook(self, units):
        lo, n = self._take(units)
        if n == 0:
            return None
        start, finish = self.parts(lo, n)
        return _Hook([self.src, self.buf], [jax.ShapeDtypeStruct(self.buf.shape, self.buf.dtype)], {1: 0},
                     self._scratch(), start, finish, self._set)

    def drain(self):
        lo, n = self._take(self.units)
        if n:
            start, finish = self.parts(lo, n)

            def body(s_ref, b_ref, o_ref, send_sems, recv_sems):
                args = ((s_ref, b_ref), (o_ref,), (send_sems, recv_sems))
                start(*args)
                finish(*args)

            self.buf = pl.pallas_call(
                body, name=self.name, in_specs=[_ANY, _ANY], out_specs=_ANY,
                out_shape=jax.ShapeDtypeStruct(self.buf.shape, self.buf.dtype),
                scratch_shapes=self._scratch(), input_output_aliases={1: 0},
            )(self.src, self.buf)
        return self.buf


def _rs_pair_parts(half, lo, n):
    def copy(g_ref, out_ref, send_sems, recv_sems):
        x, y, c = _position()
        return pltpu.make_async_remote_copy(
            src_ref=g_ref.at[pl.ds(0, 4), pl.ds((1 - c) * half + lo, n), :], dst_ref=out_ref.at[pl.ds(0, 4), pl.ds(lo, n), :],
            send_sem=send_sems.at[0], recv_sem=recv_sems.at[0], device_id=(x, y, 1 - c), device_id_type=MESH)

    def start(ins, outs, scr):
        copy(ins[0], outs[0], *scr).start()

    def finish(ins, outs, scr):
        copy(ins[0], outs[0], *scr).wait()

    return start, finish


def _rs_swap(r, name):
    Rh, C = r.shape

    def body(r_ref, out_ref, send_sem, recv_sem):
        x, y, c = _position()
        cp = pltpu.make_async_remote_copy(src_ref=r_ref, dst_ref=out_ref, send_sem=send_sem,
                                          recv_sem=recv_sem, device_id=(x, y, 1 - c), device_id_type=MESH)
        cp.start()
        cp.wait()

    return pl.pallas_call(
        body, name=name, in_specs=[_HBM], out_specs=_HBM,
        out_shape=jax.ShapeDtypeStruct((Rh, C), r.dtype),
        scratch_shapes=[pltpu.SemaphoreType.DMA, pltpu.SemaphoreType.DMA],
    )(r)


def _rs_add_pair(g, recv, cidx, name):
    _, R, C = g.shape
    Rh = R // 2
    tm = _pick(Rh, (400, 280, 200, 160, 80, 40, 16, 8))
    nt = Rh // tm

    def body(c_ref, g_ref, r_ref, o_ref):
        o_ref[...] = (g_ref[...].astype(F32) + r_ref[...].astype(F32)).astype(o_ref.dtype)

    return pl.pallas_call(
        body, name=name,
        grid_spec=pltpu.PrefetchScalarGridSpec(
            num_scalar_prefetch=1, grid=(4, nt),
            in_specs=[pl.BlockSpec((1, tm, C), lambda k, i, cr: (k, cr[0] * nt + i, 0)),
                      pl.BlockSpec((1, tm, C), lambda k, i, cr: (k, i, 0))],
            out_specs=pl.BlockSpec((1, tm, C), lambda k, i, cr: (k, i, 0))),
        out_shape=jax.ShapeDtypeStruct((4, Rh, C), BF16),
    )(cidx, g, recv)


def _rs_add_chips(h, recv, chip_idx, name):
    _, Rh, C = h.shape
    tm = _pick(Rh, (400, 280, 200, 160, 80, 40, 16, 8))

    def body(c_ref, h_ref, r_ref, o_ref):
        acc = h_ref[0].astype(F32)
        for j in range(3):
            acc = acc + r_ref[j].astype(F32)
        o_ref[...] = acc

    return pl.pallas_call(
        body, name=name,
        grid_spec=pltpu.PrefetchScalarGridSpec(
            num_scalar_prefetch=1, grid=(Rh // tm,),
            in_specs=[pl.BlockSpec((1, tm, C), lambda i, cr: (cr[0], i, 0)), pl.BlockSpec((3, tm, C), lambda i, cr: (0, i, 0))],
            out_specs=pl.BlockSpec((tm, C), lambda i, cr: (i, 0))),
        out_shape=jax.ShapeDtypeStruct((Rh, C), F32),
    )(chip_idx, h, recv)


def _all_reduce_small(vec, name):
    n, C = vec.shape

    def body(v_ref, out_ref, buf, send_sems, recv_sems):
        x, y, c = _position()

        def flip(k):
            return ((1 - x) if k & 4 else x, (1 - y) if k & 2 else y, (1 - c) if k & 1 else c)

        def idx(p):
            return 4 * p[0] + 2 * p[1] + p[2]

        me = idx((x, y, c))
        buf[me] = v_ref[...]
        cps = [pltpu.make_async_remote_copy(src_ref=v_ref, dst_ref=buf.at[me], send_sem=send_sems.at[k - 1],
                                            recv_sem=recv_sems.at[k - 1], device_id=flip(k), device_id_type=MESH)
               for k in range(1, 8)]
        for cp in cps:
            cp.start()
        for k in range(1, 8):
            pltpu.make_async_remote_copy(src_ref=v_ref, dst_ref=buf.at[idx(flip(k))], send_sem=send_sems.at[k - 1],
                                         recv_sem=recv_sems.at[k - 1], device_id=flip(k), device_id_type=MESH).wait_recv()
        for cp in cps:
            cp.wait_send()
        acc = buf[0]
        for s in range(1, 8):
            acc = acc + buf[s]
        out_ref[...] = acc

    return pl.pallas_call(
        body, name=name,
        in_specs=[pl.BlockSpec(memory_space=pltpu.VMEM)], out_specs=pl.BlockSpec(memory_space=pltpu.VMEM),
        out_shape=jax.ShapeDtypeStruct((n, C), F32),
        scratch_shapes=[pltpu.VMEM((8, n, C), F32), pltpu.SemaphoreType.DMA((7,)), pltpu.SemaphoreType.DMA((7,))],
    )(vec)


BIG = (("w_in", (D, IN_WIDTH // 4), "cols"), ("w_a", (GW, D // 4), "cols"), ("pool_w", (4, PG // 4, PG), "pool"),
       ("w_b", (D // 4, D), "rows"), ("w_c", (D // 4, D), "rows"), ("w_o", (D // 4, D), "rows"),
       ("ffn_w_up", (D, 2 * D_FF // 4), "cols"), ("ffn_w_down", (D_FF // 4, D), "rows"))
def _pack_rows(s):
    k = math.prod(s) // D
    return -(-k // 16) * 16, k


PACK_ROWS = sum(_pack_rows(s)[0] for _, s, _ in BIG)
PACK_PAD = -(-PACK_ROWS // 32) * 32


def _pad_rows(v, rows):
    pad = [(0, 0)] * v.ndim
    pad[-2] = (0, rows - v.shape[-2])
    return jnp.pad(v, pad) if rows > v.shape[-2] else v


def _pack_blocks(blocks, dtype):
    lead = blocks["w_in"].shape[:-2]
    flat = []
    for n, s, how in BIG:
        v = blocks[n].astype(dtype)
        if how == "cols":
            v = jnp.swapaxes(v, -1, -2)
        flat.append(_pad_rows(v.reshape(*lead, -1, D), _pack_rows(s)[0]))
    flat.append(jnp.zeros((*lead, PACK_PAD - PACK_ROWS, D), dtype))
    return jnp.concatenate(flat, axis=-2)


def _unpack_blocks(pack):
    out, r = {}, 0
    for n, s, how in BIG:
        rows, k = _pack_rows(s)
        v = pack[r:r + k, :]
        out[n] = v.reshape(s[1], s[0]).T if how == "cols" else v.reshape(s)
        r += rows
    return out


def _operands(allp):
    out, r = {}, 0
    for n, s, how in BIG:
        rows, k = _pack_rows(s)
        v = allp[:, r:r + k, :]
        if how == "cols":
            out[n] = v.reshape(4 * s[1], s[0])
        elif how == "rows":
            out[n] = v.reshape(4 * s[0], s[1])
        else:
            out[n] = v.reshape(4, *s).transpose(1, 0, 2, 3).reshape(4, PG, PG)
        r += rows
    return out


def _pack_operands(g, dtype):
    flat = []
    for n, s, how in BIG:
        v = g[n].astype(dtype)
        if how == "pool":
            v = v.reshape(4, 4, s[1], s[2]).transpose(1, 0, 2, 3)
        flat.append(_pad_rows(v.reshape(4, -1, D), _pack_rows(s)[0]))
    flat.append(jnp.zeros((4, PACK_PAD - PACK_ROWS, D), dtype))
    return jnp.concatenate(flat, axis=1)


def _layer_fwd(x, w, sm, bias, hk):
    pa, u = _mmf(None, w["in_a"], tb=True, pre=(_rms_core, [x], [sm["ln1_g"]]), name="in_a", tm=1024, hook=hk("in_a"))
    pb = _mm(u, w["in_b"], tb=True, out_dtype=BF16, name="in_b", hook=hk("in_b"))
    pc = _mm(u, w["in_c"], tb=True, out_dtype=BF16, name="in_c", hook=hk("in_c"))
    pd = _mm(u, w["in_d"], tb=True, out_dtype=BF16, name="in_d", hook=hk("in_d"))
    os_, ls_ = [], []
    for gi in range(3):
        o, l = _attn_fwd(pa, bias[gi], gi, "attn_fwd%d" % gi)
        os_.append(o)
        ls_.append(l)
    att = _mix_fwd(os_, ls_, "mix_fwd")
    ya = _mm(att, w["w_a"], tb=True, out_dtype=BF16, name="mm_wa")
    pool_o = _pool_fwd(pb, w["pool_w"], sm["pool_scale"], "pool_fwd")
    yb = _mm(pool_o, w["w_b"], out_dtype=BF16, name="mm_wb")
    xbc_c = _ssd_conv_fwd(pc, sm["ssd_conv_w"], sm["ssd_conv_b"], "ssd_conv_fwd")
    y_scan, states = _ssd_scan_fwd(xbc_c, pd, sm["ssd_dt_bias"], sm["ssd_a_log"], "ssd_scan_fwd")
    ssd_o = _ssd_post_fwd(y_scan, xbc_c, pc, sm["ssd_d"], sm["ssd_norm_w"], "ssd_post_fwd")
    yc = _mm(ssd_o, w["w_c"], out_dtype=BF16, name="mm_wc")
    merged = _gates_fwd(pd, sm["b_gate"], ya, yb, yc, "gates_fwd")
    x1 = _mm(merged, w["w_o"], add=x, name="mm_wo", hook=hk("mm_wo"))
    h, u2 = _mmf(None, w["ffn_w_up"], tb=True, pre=(_rms_core, [x1], [sm["ln2_g"]]), out_dtype=BF16, name="mm_up",
                 tm=1024, hook=hk("mm_up"))
    f = _ffn_act_fwd(h, sm["ffn_conv_w"], sm["ffn_conv_b"], "ffn_act_fwd")
    x2 = _mm(f, w["ffn_w_down"], add=x1, name="mm_down", hook=hk("mm_down"))
    saved = dict(x=x, u=u, pa=pa, pb=pb, pc=pc, pd=pd, os=os_, ls=ls_, att=att, ya=ya, yb=yb, yc=yc, pool_o=pool_o,
                 xbc_c=xbc_c, y_scan=y_scan, states=states, ssd_o=ssd_o, merged=merged, x1=x1, u2=u2, h=h, f=f)
    return x2, saved


def _layer_bwd(dx2, w, sm, bias, dbs, sv, hk):
    gw, gs = {}, {}
    S = dx2.shape[0]

    def gmm(a, b, name):
        return _mm(a, b, ta=True, out_dtype=BF16, name=name, hook=hk(name))

    df = _mm(dx2, w["ffn_w_down"], tb=True, out_dtype=BF16, name="d_f", hook=hk("d_f"))
    gw["ffn_w_down"] = gmm(sv["f"], dx2, "g_down")
    dha, dhv, gs["ffn_conv_w"], gs["ffn_conv_b"] = _ffn_act_bwd(sv["h"], sm["ffn_conv_w"], sm["ffn_conv_b"], df, "ffn_act_bwd")
    dx1, gs["ln2_g"] = _mmf([dha, dhv], [w["up_a"], w["up_v"]], name="d_u2_v", tm=256, hook=hk("d_u2_v"),
                            post=(_rms_post, [sv["x1"], dx2], [sm["ln2_g"]], [(D, F32)], [(1, D)]))
    gw["ffn_w_up"] = jnp.concatenate([gmm(dha, sv["u2"], "g_up_a"), gmm(dhv, sv["u2"], "g_up_v")], axis=0)
    dya, dyb, dyc, dgate, gs["b_gate"] = _mmf(
        dx1, w["w_o"], tb=True, name="d_merged", tm=256, hook=hk("d_merged"),
        post=(_gates_post, [sv["pd"], sv["ya"], sv["yb"], sv["yc"]], [sm["b_gate"]],
              [(D, BF16)] * 3 + [(3 * D, BF16)], [(1, 3 * D)]))
    gw["w_o"] = gmm(sv["merged"], dx1, "g_wo")
    dssd_o = _mm(dyc, w["w_c"], tb=True, name="d_ssd_o")
    gw["w_c"] = gmm(sv["ssd_o"], dyc, "g_wc")
    dy_scan, dxs_skip, dz, gs["ssd_d"], gs["ssd_norm_w"] = _ssd_post_bwd(
        sv["y_scan"], sv["xbc_c"], sv["pc"], sm["ssd_d"], sm["ssd_norm_w"], dssd_o, "ssd_post_bwd")
    dxbc_c, ddt, gs["ssd_dt_bias"], gs["ssd_a_log"] = _ssd_scan_bwd(
        sv["xbc_c"], sv["pd"], sm["ssd_dt_bias"], sm["ssd_a_log"], sv["states"], dy_scan, dxs_skip, "ssd_scan_bwd")
    dxbc, gs["ssd_conv_w"], gs["ssd_conv_b"] = _ssd_conv_bwd(sv["pc"], sm["ssd_conv_w"], sm["ssd_conv_b"], dxbc_c, "ssd_conv_bwd")
    dpool_o = _mm(dyb, w["w_b"], tb=True, name="d_pool_o")
    gw["w_b"] = gmm(sv["pool_o"], dyb, "g_wb")
    dpb, dpw, gs["pool_scale"] = _pool_bwd(sv["pb"], w["pool_w"], sm["pool_scale"], dpool_o, "pool_bwd")
    gw["pool_w"] = dpw.reshape(4, PG, PG)
    datt = _mm(dya, w["w_a"], name="d_att")
    gw["w_a"] = gmm(dya, sv["att"], "g_wa")
    dos, dls = _mix_bwd(sv["os"], sv["ls"], datt, "mix_bwd")
    dqkv = tuple(lax.empty((S, AW), F32) for _ in range(3))
    dbs = list(dbs)
    for gi in range(3):
        dqkv, dbs[gi] = _attn_bwd(sv["pa"], bias[gi], dos[gi], dls[gi], dbs[gi], dqkv, gi, "attn_bwd%d" % gi)
    u = sv["u"]
    pieces = [(dqkv[0], "wq"), (dqkv[1], "wk"), (dqkv[2], "wv"), (dpb, "in_b"), (dz, "wz"), (dxbc, "wxbc"),
              (ddt, "wdt"), (dgate, "wgate")]
    du = _mmf([dp for dp, _ in pieces[:4]], [w[key] for _, key in pieces[:4]], name="d_u_a", tm=256, hook=hk("d_u_a"))[0]
    dx, gs["ln1_g"] = _mmf([dp for dp, _ in pieces[4:]], [w[key] for _, key in pieces[4:]], add=du, name="d_u_wgate",
                           tm=256, hook=hk("d_u_wgate"),
                           post=(_rms_post, [sv["x"], dx1], [sm["ln1_g"]], [(D, F32)], [(1, D)]))
    g_in = []
    for dp, key in pieces:
        g = gmm(dp, u, "g_in_" + key)
        g_in.append(g[:SSD_HEADS] if key == "wdt" else g)
    gw["w_in"] = jnp.concatenate(g_in, axis=0)
    return dx, gw, gs, dbs


SMALL_LAYER = ("ln1_g", "b_gate", "pool_scale", "ssd_conv_w", "ssd_conv_b", "ssd_dt_bias", "ssd_a_log", "ssd_d",
               "ssd_norm_w", "ln2_g", "ffn_conv_w", "ffn_conv_b")


def _pad_lanes(v):
    return jnp.pad(v, (0, LANES - v.shape[0])).reshape(1, LANES)


def _layer_weights(ops):
    wt = ops["w_in"]
    o1, o2, o3 = SEC_A, SEC_A + SEC_B, SEC_A + SEC_B + SEC_C
    w = dict(ops)
    w["in_a"] = jnp.pad(wt[:o1], ((0, SEC_A_PAD - o1), (0, 0)))
    w["in_b"] = wt[o1:o2]
    w["in_c"] = wt[o2:o3]
    w["in_d"] = jnp.pad(wt[o3:], ((0, SEC_D - (IN_WIDTH - o3)), (0, 0)))
    w["wq"], w["wk"], w["wv"] = wt[:AW], wt[AW:2 * AW], wt[2 * AW:o1]
    w["wz"], w["wxbc"] = wt[o2:o2 + D], wt[o2 + D:o3]
    w["wdt"] = jnp.pad(wt[o3:o3 + SSD_HEADS], ((0, LANES - SSD_HEADS), (0, 0)))
    w["wgate"] = wt[o3 + SSD_HEADS:]
    w["up_a"], w["up_v"] = ops["ffn_w_up"][:D_FF], ops["ffn_w_up"][D_FF:]
    return w


def _layer_small(p, i):
    sm = {n: p[n][i] for n in SMALL_LAYER}
    out = {}
    for n, v in sm.items():
        if n in ("ssd_dt_bias", "ssd_a_log", "ssd_d"):
            out[n] = _pad_lanes(v)
        elif v.ndim == 1:
            out[n] = v.reshape(1, -1)
        else:
            out[n] = v
    return out


def _local_step(x, target, rel_bias, final_g, layer_full, small, fwd_hooks=None, bwd_hooks=None, after_bwd=None):
    nl = small["ln1_g"].shape[0]
    buckets = [_buckets(d).astype(jnp.int32) for d in DILATIONS]
    bias = [_bias_table(rel_bias, buckets[gi], gi, "bias_table%d" % gi) for gi in range(3)]
    no_hooks = lambda i: (lambda name: None)
    fwd_hooks = fwd_hooks or no_hooks
    bwd_hooks = bwd_hooks or no_hooks
    saved, ws, sms = [], [], []
    h = x
    for i in range(nl):
        w = _layer_weights(layer_full(i))
        sm = _layer_small(small, i)
        h, sv = _layer_fwd(h, w, sm, bias, fwd_hooks(i))
        saved.append(sv)
        ws.append(w)
        sms.append(sm)
    dh, dfinal, loss = _final_loss(h, target, final_g.reshape(1, D))
    gws, gss = [None] * nl, [None] * nl
    dbs = [jnp.zeros((6, WIN, 2 * WIN), F32)] * 3
    for i in reversed(range(nl)):
        dh, gws[i], gss[i], dbs = _layer_bwd(dh, ws[i], sms[i], bias, dbs, saved[i], bwd_hooks(i))
        if after_bwd is not None:
            after_bwd(i, gws[i])
    drel = []
    for gi in range(3):
        onehot = jnp.pad(jax.nn.one_hot(buckets[gi].reshape(-1), REL_BUCKETS, dtype=BF16), ((0, 0), (0, LANES - REL_BUCKETS)))
        drel.append(_mm(dbs[gi].reshape(6, WIN * 2 * WIN), onehot, name="g_relb"))
    return loss, dh, gws, gss, dfinal, jnp.concatenate(drel, axis=0)


WEIGHTS = ("rel_bias", "ln1_g", "w_in", "b_gate", "w_a", "pool_w", "pool_scale", "w_b", "ssd_conv_w", "ssd_conv_b",
           "ssd_dt_bias", "ssd_a_log", "ssd_d", "ssd_norm_w", "w_c", "w_o", "ln2_g", "ffn_w_up", "ffn_conv_w",
           "ffn_conv_b", "ffn_w_down", "final_g")
BIG_NAMES = tuple(n for n, _, _ in BIG)
SHARDED_SMALL = {"ssd_conv_w": XBC // 4, "ffn_conv_w": 2 * D_FF // 4}


def _to_rows(flat):
    n = flat.shape[0]
    rows = -(-n // LANES)
    rows = -(-rows // 8) * 8
    return jnp.pad(flat, (0, rows * LANES - n)).reshape(rows, LANES)


def _flatten(tree, names):
    return jnp.concatenate([tree[n].reshape(-1) for n in names])


def _unflatten(flat, shapes, names):
    out, o = {}, 0
    for n in names:
        k = math.prod(shapes[n])
        out[n] = flat[o:o + k].reshape(shapes[n])
        o += k
    return out


def kernel(x, rel_bias, ln1_g, w_in, b_gate, w_a, pool_w, pool_scale, w_b, ssd_conv_w, ssd_conv_b, ssd_dt_bias, ssd_a_log, ssd_d, ssd_norm_w, w_c, w_o, ln2_g, ffn_w_up, ffn_conv_w, ffn_conv_b, ffn_w_down, final_g, loss_target, m_rel_bias, m_ln1_g, m_w_in, m_b_gate, m_w_a, m_pool_w, m_pool_scale, m_w_b, m_ssd_conv_w, m_ssd_conv_b, m_ssd_dt_bias, m_ssd_a_log, m_ssd_d, m_ssd_norm_w, m_w_c, m_w_o, m_ln2_g, m_ffn_w_up, m_ffn_conv_w, m_ffn_conv_b, m_ffn_w_down, m_final_g, v_rel_bias, v_ln1_g, v_w_in, v_b_gate, v_w_a, v_pool_w, v_pool_scale, v_w_b, v_ssd_conv_w, v_ssd_conv_b, v_ssd_dt_bias, v_ssd_a_log, v_ssd_d, v_ssd_norm_w, v_w_c, v_w_o, v_ln2_g, v_ffn_w_up, v_ffn_conv_w, v_ffn_conv_b, v_ffn_w_down, v_final_g):
    W = dict(rel_bias=rel_bias, ln1_g=ln1_g, w_in=w_in, b_gate=b_gate, w_a=w_a, pool_w=pool_w, pool_scale=pool_scale,
             w_b=w_b, ssd_conv_w=ssd_conv_w, ssd_conv_b=ssd_conv_b, ssd_dt_bias=ssd_dt_bias, ssd_a_log=ssd_a_log,
             ssd_d=ssd_d, ssd_norm_w=ssd_norm_w, w_c=w_c, w_o=w_o, ln2_g=ln2_g, ffn_w_up=ffn_w_up,
             ffn_conv_w=ffn_conv_w, ffn_conv_b=ffn_conv_b, ffn_w_down=ffn_w_down, final_g=final_g)
    M = dict(rel_bias=m_rel_bias, ln1_g=m_ln1_g, w_in=m_w_in, b_gate=m_b_gate, w_a=m_w_a, pool_w=m_pool_w,
             pool_scale=m_pool_scale, w_b=m_w_b, ssd_conv_w=m_ssd_conv_w, ssd_conv_b=m_ssd_conv_b,
             ssd_dt_bias=m_ssd_dt_bias, ssd_a_log=m_ssd_a_log, ssd_d=m_ssd_d, ssd_norm_w=m_ssd_norm_w, w_c=m_w_c,
             w_o=m_w_o, ln2_g=m_ln2_g, ffn_w_up=m_ffn_w_up, ffn_conv_w=m_ffn_conv_w, ffn_conv_b=m_ffn_conv_b,
             ffn_w_down=m_ffn_w_down, final_g=m_final_g)
    V = dict(rel_bias=v_rel_bias, ln1_g=v_ln1_g, w_in=v_w_in, b_gate=v_b_gate, w_a=v_w_a, pool_w=v_pool_w,
             pool_scale=v_pool_scale, w_b=v_w_b, ssd_conv_w=v_ssd_conv_w, ssd_conv_b=v_ssd_conv_b,
             ssd_dt_bias=v_ssd_dt_bias, ssd_a_log=v_ssd_a_log, ssd_d=v_ssd_d, ssd_norm_w=v_ssd_norm_w, w_c=v_w_c,
             w_o=v_w_o, ln2_g=v_ln2_g, ffn_w_up=v_ffn_w_up, ffn_conv_w=v_ffn_conv_w, ffn_conv_b=v_ffn_conv_b,
             ffn_w_down=v_ffn_w_down, final_g=v_final_g)
    nl = ln1_g.shape[0]
    px, py, pc_ = _position()
    chip = 2 * px + py
    cidx = jnp.reshape(pc_, (1,)).astype(jnp.int32)
    chip_idx = jnp.reshape(chip, (1,)).astype(jnp.int32)

    placed = {}
    for n, cs in SHARDED_SMALL.items():
        full = jnp.zeros(W[n].shape[:-1] + (4 * cs,), F32)
        full = lax.dynamic_update_slice(full, W[n], (0, 0, chip * cs))
        placed[n] = jnp.where(pc_ == 0, full, 0.0)
    names_sh = tuple(SHARDED_SMALL)
    shapes_sh = {n: placed[n].shape for n in names_sh}
    got = _all_reduce_small(_to_rows(_flatten(placed, names_sh)), "gather_small")
    small = {n: W[n] for n in SMALL_LAYER}
    small.update(_unflatten(got.reshape(-1), shapes_sh, names_sh))

    packs = _pack_blocks({n: W[n] for n in BIG_NAMES}, BF16)

    half = PACK_PAD // 2
    units = half // 16

    def share(weights, total):
        tot = sum(weights.values())
        return {n: math.ceil(total * v / tot) for n, v in weights.items()}

    gathers = {}

    def gather(i):
        if i not in gathers:
            buf = lax.dynamic_update_slice(lax.empty((4, PACK_PAD, D), BF16), packs[i][None], (chip, 0, 0))
            gathers[i] = _Stream(packs[i], buf, functools.partial(_gather_parts, half), 6, units, "gather_w")
        return gathers[i]

    def layer_full(i):
        return _operands(gather(i).drain())

    fwd_share = share(dict(in_a=63, in_c=31, in_d=44, mm_up=83, mm_down=34), units)

    def fwd_hooks(i):
        if i + 1 >= nl:
            return lambda name: None
        return lambda name: gather(i + 1).hook(fwd_share[name]) if name in fwd_share else None

    exchanges = {}
    bwd_share = share(dict(g_down=38, d_u2_v=60, g_up_a=35, g_up_v=35, d_merged=50, d_u_a=85, d_u_wgate=70,
                           g_in_wgate=36), units)

    class Exchange:
        def __init__(self, g):
            self.g = g
            self.pair = _Stream(g, lax.empty((4, half, D), BF16), functools.partial(_rs_pair_parts, half), 1, units, "rs_pair")
            self.hsum = self.chips = None

        def to_chips(self):
            if self.chips is None:
                self.hsum = _rs_add_pair(self.g, self.pair.drain(), cidx, "rs_add_pair")
                self.chips = _Stream(self.hsum, lax.empty((3, half, D), BF16), _rs_chip_parts, 3, units, "rs_chips")
            return self.chips

    def after_bwd(i, gw):
        exchanges[i] = Exchange(_pack_operands(gw, BF16))

    def bwd_hooks(i):
        if i + 1 >= nl:
            return lambda name: None

        def hk(name):
            if name == "d_f":
                return exchanges[i + 1].pair.hook(units)
            return exchanges[i + 1].to_chips().hook(bwd_share[name]) if name in bwd_share else None

        return hk

    loss, dx, gws, gss, dfinal, drel = _local_step(x[0], loss_target[0], rel_bias, final_g, layer_full, small,
                                                   fwd_hooks, bwd_hooks, after_bwd)

    def reduced(i):
        recv3 = exchanges[i].to_chips().drain()
        r = _rs_add_chips(exchanges[i].hsum, recv3, chip_idx, "rs_add_chips")
        other = _rs_swap(r, "rs_swap")
        both = jnp.concatenate([jnp.where(pc_ == 0, r, other), jnp.where(pc_ == 0, other, r)], axis=0)
        return _unpack_blocks(both)

    red = [reduced(i) for i in range(nl)]
    delta, new_m, new_v, grads = {}, {}, {}, {}
    for n in BIG_NAMES:
        shp = W[n].shape
        r2 = lambda a: a.reshape(-1, shp[-1])
        grads[n] = jnp.stack([red[i][n] for i in range(nl)], axis=0)
        res = _adamw(r2(W[n]), r2(grads[n]), r2(M[n]), r2(V[n]), "adamw_" + n)
        delta[n], new_m[n], new_v[n] = [a.reshape(shp) for a in res]

    sg = {}
    for n in SMALL_LAYER:
        sg[n] = jnp.stack([gss[i][n] for i in range(nl)], axis=0)
    for n in ("ssd_dt_bias", "ssd_a_log", "ssd_d"):
        sg[n] = sg[n][:, 0, :SSD_HEADS]
    sg["rel_bias"] = drel[:, :REL_BUCKETS].T
    sg["final_g"] = dfinal.reshape(D)
    sg["loss"] = loss[0, :1]
    names_sg = tuple(sg)
    shapes_sg = {n: ((nl,) + W[n].shape[1:] if n in SMALL_LAYER and n not in SHARDED_SMALL else
                     (placed[n].shape if n in SHARDED_SMALL else sg[n].shape)) for n in names_sg}
    for n in names_sg:
        sg[n] = sg[n].reshape(shapes_sg[n])
    tot = _all_reduce_small(_to_rows(_flatten(sg, names_sg)), "allreduce_small")
    tot = _unflatten(tot.reshape(-1), shapes_sg, names_sg)
    loss_out = tot.pop("loss").reshape(())
    for n, cs in SHARDED_SMALL.items():
        tot[n] = lax.dynamic_slice(tot[n], (0, 0, chip * cs), tot[n].shape[:-1] + (cs,))
    grads.update(tot)

    names_s = tuple(n for n in WEIGHTS if n not in BIG_NAMES)
    shapes_s = {n: W[n].shape for n in names_s}
    pk = lambda t: _to_rows(_flatten(t, names_s))
    dl, m2, v2 = _adamw(pk(W), pk(grads), pk(M), pk(V), "adamw_small")
    delta.update(_unflatten(dl.reshape(-1), shapes_s, names_s))
    new_m.update(_unflatten(m2.reshape(-1), shapes_s, names_s))
    new_v.update(_unflatten(v2.reshape(-1), shapes_s, names_s))

    return (loss_out, dx[None], *[grads[n] for n in WEIGHTS], *[delta[n] for n in WEIGHTS],
            *[new_m[n] for n in WEIGHTS], *[new_v[n] for n in WEIGHTS])
```

```python
import functools
import math

import jax
import jax.numpy as jnp
from jax import lax
from jax.experimental import pallas as pl
from jax.experimental.pallas import tpu as pltpu

F32 = jnp.float32
BF16 = jnp.bfloat16
MESH = pl.DeviceIdType.MESH

D = 1024
HD = 64
GW = 384
AW = 3 * GW
WIN = 128
DILATIONS = (1, 4, 16)
REL_BUCKETS = 32
REL_MAX_DISTANCE = 2048
POOL_WINDOWS = (2, 4, 8, 16)
PG = 256
SSD_HEADS = 16
SSD_N = 128
SSD_CHUNK = 128
XBC = 1536
D_FF = 2816
EPS = 1e-6
NEG = -1e30
HALO = 16
LANES = 128

SEC_A = 3 * AW
SEC_B = D
SEC_C = D + XBC
SEC_D = 3328
SEC_A_PAD = 3584
IN_WIDTH = SEC_A + SEC_B + SEC_C + 16 + 3 * D

ADAM_LR = 0.001
ADAM_B1 = 0.9
ADAM_B2 = 0.999
ADAM_EPS = 1e-08
ADAM_WD = 0.01
ADAM_STEP = 10
ADAM_TILE = 256 * 1024
MM_VMEM_BYTES = 40 * 1024 * 1024
MM_MAX_OUT_TILE = 1024 * 1024
HBM_BYTES_PER_US = 2.0e6
STEP_US = 0.35
MXU_WIDTH = 256
MXU_FLOPS_PER_US = 0.65e6


_ANY = pl.BlockSpec(memory_space=pl.ANY)


def _pick(d, cands):
    for t in cands:
        if d % t == 0:
            return t
    return d


def _iota(shape, dim):
    return lax.broadcasted_iota(jnp.int32, shape, dim)


def _dg(a, b, ca, cb):
    return lax.dot_general(a.astype(BF16), b.astype(BF16), (((ca,), (cb,)), ((), ())),
                           preferred_element_type=F32)


@jax.custom_vjp
def _bdot_nn(a, b):
    return _dg(a, b, 1, 0)


def _nn_fwd(a, b):
    return _dg(a, b, 1, 0), (a, b)


def _nn_bwd(res, g):
    a, b = res
    return _dg(g, b, 1, 1), _dg(a, g, 0, 0)


_bdot_nn.defvjp(_nn_fwd, _nn_bwd)


@jax.custom_vjp
def _bdot_nt(a, b):
    return _dg(a, b, 1, 1)


def _nt_fwd(a, b):
    return _dg(a, b, 1, 1), (a, b)


def _nt_bwd(res, g):
    a, b = res
    return _dg(g, b, 1, 0), _dg(g, a, 0, 0)


_bdot_nt.defvjp(_nt_fwd, _nt_bwd)


@jax.custom_vjp
def _bdot_tn(a, b):
    return _dg(a, b, 0, 0)


def _tn_fwd(a, b):
    return _dg(a, b, 0, 0), (a, b)


def _tn_bwd(res, g):
    a, b = res
    return _dg(b, g, 1, 1), _dg(a, g, 1, 0)


_bdot_tn.defvjp(_tn_fwd, _tn_bwd)


def _fdot(a, b):
    return jnp.dot(a, b, preferred_element_type=F32, precision=lax.Precision.HIGHEST)


def _sigmoid(x):
    return 0.5 * jnp.tanh(0.5 * x) + 0.5


def _silu(x):
    return x * _sigmoid(x)


def _softplus(x):
    return jnp.maximum(x, 0.0) + jnp.log(1.0 + jnp.exp(-jnp.abs(x)))


def _lane_pick(m, h):
    return jnp.sum(jnp.where(_iota(m.shape, 1) == h, m, 0.0), axis=1, keepdims=True)


def _row_pick(m, h):
    return jnp.sum(jnp.where(_iota(m.shape, 0) == h, m, 0.0), axis=0, keepdims=True)


def _stack_rows(rows, n):
    c = rows[0].shape[1]
    r = _iota((n, c), 0)
    out = jnp.zeros((n, c), F32)
    for k, v in enumerate(rows):
        out = out + jnp.where(r == k, v, 0.0)
    return out


def _mm(a, b, *, ta=False, tb=False, add=None, out_dtype=F32, name, hook=None):
    if ta:
        K, M = a.shape
    else:
        M, K = a.shape
    if tb:
        N, Kb = b.shape
    else:
        Kb, N = b.shape
    assert K == Kb, (a.shape, b.shape, ta, tb)
    tm, tn, tk = _mm_tiles(M, N, K, a.dtype.itemsize, b.dtype.itemsize, jnp.dtype(out_dtype).itemsize,
                           0 if add is None else add.dtype.itemsize)
    ni, nj, nk = M // tm, N // tn, K // tk
    ca = 0 if ta else 1
    cb = 1 if tb else 0
    n_in = 2 if add is None else 3
    n_hin = 0 if hook is None else len(hook.inputs)
    n_hout = 0 if hook is None else len(hook.out_shapes)

    def body(*refs):
        a_ref, b_ref = refs[:2]
        add_ref = None if add is None else refs[2]
        o_ref = refs[n_in + n_hin]
        scr = refs[n_in + n_hin + 1 + n_hout:]
        acc_ref = scr[0] if nk > 1 else None
        hargs = (refs[n_in:n_in + n_hin], refs[n_in + n_hin + 1:n_in + n_hin + 1 + n_hout], scr[1 if nk > 1 else 0:])
        i, j, k = pl.program_id(0), pl.program_id(1), pl.program_id(2)
        if hook is not None:
            @pl.when((i == 0) & (j == 0) & (k == 0))
            def _():
                hook.start(*hargs)

        part = _dg(a_ref[...], b_ref[...], ca, cb)

        def finish(r):
            if add_ref is not None:
                r = r + add_ref[...].astype(F32)
            o_ref[...] = r.astype(o_ref.dtype)

        if nk == 1:
            finish(part)
        else:
            @pl.when(k == 0)
            def _():
                acc_ref[...] = part

            @pl.when((k > 0) & (k < nk - 1))
            def _():
                acc_ref[...] += part

            @pl.when(k == nk - 1)
            def _():
                finish(acc_ref[...] + part)

        if hook is not None:
            @pl.when((i == ni - 1) & (j == nj - 1) & (k == nk - 1))
            def _():
                hook.finish(*hargs)

    a_spec = pl.BlockSpec((tk, tm), lambda i, j, k: (k, i)) if ta else pl.BlockSpec((tm, tk), lambda i, j, k: (i, k))
    b_spec = pl.BlockSpec((tn, tk), lambda i, j, k: (j, k)) if tb else pl.BlockSpec((tk, tn), lambda i, j, k: (k, j))
    in_specs = [a_spec, b_spec]
    args = [a, b]
    if add is not None:
        in_specs.append(pl.BlockSpec((tm, tn), lambda i, j, k: (i, j)))
        args.append(add)
    out_specs = [pl.BlockSpec((tm, tn), lambda i, j, k: (i, j))]
    out_shape = [jax.ShapeDtypeStruct((M, N), out_dtype)]
    scratch = [pltpu.VMEM((tm, tn), F32)] if nk > 1 else []
    aliases = {}
    if hook is not None:
        in_specs += [_ANY] * n_hin
        args += list(hook.inputs)
        out_specs += [_ANY] * n_hout
        out_shape += list(hook.out_shapes)
        scratch += list(hook.scratch)
        aliases = {n_in + hi: 1 + ho for hi, ho in hook.aliases.items()}
    sem = ("parallel", "parallel", "arbitrary") if hook is None else ("arbitrary",) * 3
    res = pl.pallas_call(
        body, name=name, grid=(ni, nj, nk), in_specs=in_specs, out_specs=out_specs, out_shape=out_shape,
        scratch_shapes=scratch, input_output_aliases=aliases,
        compiler_params=pltpu.CompilerParams(dimension_semantics=sem),
    )(*args)
    if hook is not None:
        hook.done(res[1:])
    return res[0]


def _wide(v):
    return v.astype(F32) if v.dtype == BF16 else v


def _mmf(a, b, *, tb=False, add=None, pre=None, post=None, out_dtype=F32, name, tm, hook=None):
    a_list = list(a) if isinstance(a, (list, tuple)) else [a]
    b_list = list(b) if isinstance(b, (list, tuple)) else [b]
    assert len(a_list) == len(b_list) and (len(b_list) == 1 or not (tb or pre))
    b = b_list[0]
    if tb:
        N, K = b.shape
    else:
        K, N = b.shape
    M = pre[1][0].shape[0] if pre else a_list[0].shape[0]
    tn = N if post or N <= 1024 else _pick(N, (512, 256, LANES))
    ni, nj = M // tm, N // tn
    cb = 1 if tb else 0
    pre_fn, pre_rows, pre_consts = pre if pre else (None, [], [])
    post_fn, post_rows, post_consts, post_outs, post_accs = post if post else (None, [], [], [], [])
    hook_in = [] if hook is None else list(hook.inputs)
    hook_out = [] if hook is None else list(hook.out_shapes)

    def row_spec(arr):
        return pl.BlockSpec((tm, arr.shape[1]), lambda i, j: (i, 0))

    def const_spec(arr):
        return pl.BlockSpec(arr.shape, lambda i, j, nd=arr.ndim: (0,) * nd)

    args, in_specs = [], []
    for arr in (a_list if not pre else pre_rows):
        args.append(arr)
        in_specs.append(row_spec(arr))
    for arr in pre_consts:
        args.append(arr)
        in_specs.append(const_spec(arr))
    for arr in b_list:
        args.append(arr)
        in_specs.append(pl.BlockSpec((tn, K), lambda i, j: (j, 0)) if tb else
                        pl.BlockSpec((arr.shape[0], tn), lambda i, j: (0, j)))
    if add is not None:
        args.append(add)
        in_specs.append(pl.BlockSpec((tm, tn), lambda i, j: (i, j)))
    for arr in post_rows:
        args.append(arr)
        in_specs.append(row_spec(arr))
    for arr in post_consts:
        args.append(arr)
        in_specs.append(const_spec(arr))
    n_main = len(args)
    args += hook_in
    in_specs += [_ANY] * len(hook_in)

    out_shape, out_specs = [], []
    if post:
        for c, dt in post_outs:
            out_shape.append(jax.ShapeDtypeStruct((M, c), dt))
            out_specs.append(pl.BlockSpec((tm, c), lambda i, j: (i, 0)))
        for r, c in post_accs:
            out_shape.append(jax.ShapeDtypeStruct((r, c), F32))
            out_specs.append(pl.BlockSpec((r, c), lambda i, j: (0, 0)))
    else:
        out_shape.append(jax.ShapeDtypeStruct((M, N), out_dtype))
        out_specs.append(pl.BlockSpec((tm, tn), lambda i, j: (i, j)))
    if pre:
        out_shape.append(jax.ShapeDtypeStruct((M, K), BF16))
        out_specs.append(pl.BlockSpec((tm, K), lambda i, j: (i, 0)))
    n_out = len(out_shape)
    out_shape += hook_out
    out_specs += [_ANY] * len(hook_out)
    scratch = ([pltpu.VMEM((tm, K), BF16)] if pre else []) + ([] if hook is None else list(hook.scratch))
    aliases = {} if hook is None else {n_main + hi: n_out + ho for hi, ho in hook.aliases.items()}

    def body(*refs):
        ins, outs, scr = refs[:n_main], refs[len(args):len(args) + n_out], refs[len(args) + len(out_shape):]
        hargs = (refs[n_main:len(args)], refs[len(args) + n_out:len(args) + len(out_shape)], scr[1 if pre else 0:])
        i, j = pl.program_id(0), pl.program_id(1)
        if hook is not None:
            @pl.when((i == 0) & (j == 0))
            def _():
                hook.start(*hargs)

        it = iter(ins)
        if pre:
            rows_ = [next(it) for _ in pre_rows]
            consts_ = [next(it) for _ in pre_consts]

            @pl.when(j == 0)
            def _():
                av = pre_fn(*[_wide(r[...]) for r in rows_], *[_wide(r[...]) for r in consts_]).astype(BF16)
                scr[0][...] = av
                outs[-1][...] = av

            ats = [scr[0][...]]
        else:
            ats = [next(it)[...] for _ in a_list]
        p = None
        for at in ats:
            part = _dg(at, next(it)[...], 1, cb)
            p = part if p is None else p + part
        if add is not None:
            p = p + next(it)[...].astype(F32)
        if post:
            rows_ = [next(it) for _ in post_rows]
            consts_ = [next(it) for _ in post_consts]
            res = post_fn(p, *[_wide(r[...]) for r in rows_], *[_wide(r[...]) for r in consts_])
            for r, v in zip(outs[:len(post_outs)], res[:len(post_outs)]):
                r[...] = v.astype(r.dtype)
            for r, v in zip(outs[len(post_outs):], res[len(post_outs):]):
                @pl.when(i == 0)
                def _(r=r, v=v):
                    r[...] = v

                @pl.when(i > 0)
                def _(r=r, v=v):
                    r[...] += v
        else:
            outs[0][...] = p.astype(outs[0].dtype)
        if hook is not None:
            @pl.when((i == ni - 1) & (j == nj - 1))
            def _():
                hook.finish(*hargs)

    res = pl.pallas_call(
        body, name=name, grid=(ni, nj), in_specs=in_specs, out_specs=out_specs, out_shape=out_shape,
        scratch_shapes=scratch, input_output_aliases=aliases,
        compiler_params=pltpu.CompilerParams(dimension_semantics=("arbitrary", "arbitrary")),
    )(*args)
    if hook is not None:
        hook.done(res[n_out:])
    return res[:n_out]


def _mm_tiles(M, N, K, sa, sb, so, sadd):
    def tiles(d):
        return [t for t in range(LANES, min(d, 2048) + 1, LANES) if d % t == 0] or [d]

    best = None
    for tk in [K] + [t for t in tiles(K) if t < K]:
        for tm in tiles(M):
            for tn in tiles(N):
                vmem = 2 * (tm * tk * sa + tk * tn * sb + tm * tn * (so + sadd)) + (tm * tn * 4 if tk < K else 0)
                if vmem > MM_VMEM_BYTES or tm * tn > MM_MAX_OUT_TILE:
                    continue
                a_reads = 1 if tk == K else N // tn
                traffic = M * K * sa * a_reads + K * N * sb * (M // tm) + M * N * (so + sadd)
                steps = (M // tm) * (N // tn) * (K // tk)
                width = -(-tn // MXU_WIDTH) * MXU_WIDTH
                mxu = 2.0 * M * K * N * (width / tn) / MXU_FLOPS_PER_US
                edge = tm * tk * sa + tk * tn * sb + tm * tn * (so + sadd)
                cost = max(traffic / HBM_BYTES_PER_US, mxu) + steps * STEP_US + edge / HBM_BYTES_PER_US
                if best is None or cost < best[0]:
                    best = (cost, tm, tn, tk)
    assert best is not None, (M, N, K)
    return best[1:]


class _Hook:
    def __init__(self, inputs, out_shapes, aliases, scratch, start, finish, done):
        self.inputs, self.out_shapes, self.aliases, self.scratch = inputs, out_shapes, aliases, scratch
        self.start, self.finish, self.done = start, finish, done


class _Ctx:
    def __init__(self, first, last, row0, rows):
        self.first, self.last, self.row0, self.rows = first, last, row0, rows


def _rows(name, fn, ins, outs, accs=(), *, tm, nrows, ncol=1):
    nt = nrows // tm
    hb = tm // HALO
    nh = nrows // HALO
    ins = [(kind, arr, arr.shape[1] if kind == "row" and cw is None else cw, base) for kind, arr, cw, base in ins]
    in_specs, args = [], []
    for kind, arr, cw, base in ins:
        if kind == "row":
            in_specs.append(pl.BlockSpec((tm, cw), lambda j, i, base=base: (i, base + j)))
        elif kind == "prev":
            in_specs.append(pl.BlockSpec((HALO, cw), lambda j, i, base=base: (jnp.maximum(i * hb - 1, 0), base + j)))
        elif kind == "next":
            in_specs.append(pl.BlockSpec((HALO, cw), lambda j, i, base=base: (jnp.minimum((i + 1) * hb, nh - 1), base + j)))
        elif kind in ("const", "raw"):
            in_specs.append(pl.BlockSpec(arr.shape, lambda j, i, nd=arr.ndim: (0,) * nd))
        elif kind == "ccol":
            in_specs.append(pl.BlockSpec((arr.shape[0], cw), lambda j, i, base=base: (0, base + j)))
        else:
            raise ValueError(kind)
        args.append(arr)
    out_specs, out_shape = [], []
    for ctot, cw, base, dt in outs:
        out_specs.append(pl.BlockSpec((tm, cw), lambda j, i, base=base: (i, base + j)))
        out_shape.append(jax.ShapeDtypeStruct((nrows, ctot), dt))
    for r, ctot, cw in accs:
        out_specs.append(pl.BlockSpec((r, cw), lambda j, i: (0, j)))
        out_shape.append(jax.ShapeDtypeStruct((r, ctot), F32))
    n_in, n_out = len(ins), len(outs)

    def body(*refs):
        i = pl.program_id(1)
        in_refs, out_refs, acc_refs = refs[:n_in], refs[n_in:n_in + n_out], refs[n_in + n_out:]
        if acc_refs:
            @pl.when(i == 0)
            def _():
                for r in acc_refs:
                    r[...] = jnp.zeros_like(r)

        vals = [r[...] if s[0] == "raw" else _wide(r[...]) for r, s in zip(in_refs, ins)]
        res = fn(_Ctx(i == 0, i == nt - 1, i * tm, tm), *vals)
        for r, v in zip(out_refs, res[:n_out]):
            r[...] = v.astype(r.dtype)
        for r, v in zip(acc_refs, res[n_out:]):
            r[...] += v

    res = pl.pallas_call(
        body, name=name, grid=(ncol, nt), in_specs=in_specs, out_specs=out_specs, out_shape=out_shape,
        compiler_params=pltpu.CompilerParams(dimension_semantics=("arbitrary", "arbitrary")),
    )(*args)
    return res


def _shift_down(xcat, k):
    return xcat if k == 0 else pltpu.roll(xcat, k, 0)


def _shift_up(xcat, k):
    return xcat if k == 0 else pltpu.roll(xcat, xcat.shape[0] - k, 0)


def _with_prev(ctx, halo, x):
    return jnp.concatenate([jnp.where(ctx.first, 0.0, halo), x], axis=0)


def _with_next(ctx, x, halo):
    return jnp.concatenate([x, jnp.where(ctx.last, 0.0, halo)], axis=0)


def _rms_core(x, g):
    r = lax.rsqrt(jnp.mean(x * x, axis=-1, keepdims=True) + EPS)
    return x * r * g


def _rms_post(du, xv, drv, gv):
    _, vjp = jax.vjp(_rms_core, xv, gv)
    dx, dg = vjp(du)
    return [drv + dx, drv + dx, dg]


RMS_POST_OUTS = [(D, F32), (D, BF16)]


def _final_loss(x, target, g):
    S = x.shape[0]

    def fn(ctx, xv, tv, gv):
        def f(xx, gg):
            err = _rms_core(xx, gg) - tv
            return 0.5 * jnp.sum(err * err) / D

        loss, vjp = jax.vjp(f, xv, gv)
        dx, dg = vjp(jnp.ones((), F32))
        return [dx, dx, dg, jnp.zeros((1, LANES), F32) + loss]

    return _rows("final_loss", fn, [("row", x, None, 0), ("row", target, None, 0), ("const", g, None, 0)],
                 [(D, D, 0, F32), (D, D, 0, BF16)], [(1, D, D), (1, LANES, LANES)], tm=256, nrows=S)


def _attn_valid(n):
    qi = _iota((WIN, 2 * WIN), 0)
    kk = _iota((WIN, 2 * WIN), 1)
    rel = qi + WIN - kk
    return (rel >= 0) & (rel <= WIN) & ((kk >= WIN) | (n > 0))


def _attn_block(q, kp, kc, vp, vc, b0, b1):
    k = jnp.concatenate([kp, kc], axis=0)
    v = jnp.concatenate([vp, vc], axis=0)
    lo = _iota((WIN, LANES), 1) < HD
    scale = 1.0 / math.sqrt(HD)
    os_, ls_ = [], []
    for hh, b in ((0, b0), (1, b1)):
        qm = jnp.where(lo if hh == 0 else ~lo, q, 0.0)
        s = _bdot_nt(qm, k) * scale + b
        m = lax.stop_gradient(jnp.max(s, axis=1, keepdims=True))
        p = jnp.exp(s - m)
        l = jnp.sum(p, axis=1, keepdims=True)
        os_.append(_bdot_nn(p, v) / l)
        ls_.append(m + jnp.log(l))
    return jnp.where(lo, os_[0], os_[1]), jnp.where(lo, ls_[0], ls_[1])


def _residue_rows(r, d):
    return pl.ds(0, WIN) if d == 1 else pl.ds(r, WIN, stride=d)


def _for_residues(d, fn):
    if d == 1:
        fn(0, 0)
    else:
        lax.fori_loop(0, d, fn, 0, unroll=min(d, 8))


def _pairs_per_step(d):
    return 3 if d == 1 else 1


def _bias_table(rel_bias, bucket, gi, name):
    def body(t_ref, b_ref, o_ref):
        h = 6 * gi + pl.program_id(0)
        b = b_ref[...]
        acc = jnp.zeros(b.shape, F32)
        for k in range(REL_BUCKETS):
            acc = jnp.where(b == k, t_ref[k, h], acc)
        o_ref[0] = acc

    return pl.pallas_call(
        body, name=name, grid=(6,),
        in_specs=[pl.BlockSpec(memory_space=pltpu.SMEM), pl.BlockSpec((WIN, 2 * WIN), lambda h: (0, 0))],
        out_specs=pl.BlockSpec((1, WIN, 2 * WIN), lambda h: (h, 0, 0)),
        out_shape=jax.ShapeDtypeStruct((6, WIN, 2 * WIN), F32),
    )(rel_bias, bucket)


def _attn_fwd(pa, bias, gi, name):
    S = pa.shape[0]
    d = DILATIONS[gi]
    bt = WIN * d
    nb = S // bt
    hpw = _pairs_per_step(d)
    bw = hpw * LANES
    cb = 3 * gi // hpw

    def body(q_ref, kp_ref, kc_ref, vp_ref, vc_ref, b_ref, o_ref, l_ref):
        valid = _attn_valid(pl.program_id(1))
        bm = [jnp.where(valid, b_ref[k], NEG) for k in range(2 * hpw)]

        def residue(r, carry):
            sl = _residue_rows(r, d)
            for t in range(hpw):
                ln = pl.ds(t * LANES, LANES)
                o, lse = _attn_block(q_ref[sl, ln], kp_ref[sl, ln], kc_ref[sl, ln], vp_ref[sl, ln], vc_ref[sl, ln],
                                     bm[2 * t], bm[2 * t + 1])
                o_ref[sl, ln] = o
                l_ref[sl, ln] = lse
            return carry

        _for_residues(d, residue)

    def spec(off, prev):
        if prev:
            return pl.BlockSpec((bt, bw), lambda hp, n: (jnp.maximum(n - 1, 0), off // hpw + cb + hp))
        return pl.BlockSpec((bt, bw), lambda hp, n: (n, off // hpw + cb + hp))

    ospec = pl.BlockSpec((bt, bw), lambda hp, n: (n, hp))
    return pl.pallas_call(
        body, name=name, grid=(3 // hpw, nb),
        in_specs=[spec(0, False), spec(9, True), spec(9, False), spec(18, True), spec(18, False),
                  pl.BlockSpec((2 * hpw, WIN, 2 * WIN), lambda hp, n: (hp, 0, 0))],
        out_specs=[ospec, ospec],
        out_shape=[jax.ShapeDtypeStruct((S, GW), F32)] * 2,
        compiler_params=pltpu.CompilerParams(dimension_semantics=("parallel", "arbitrary")),
    )(pa, pa, pa, pa, pa, bias)


def _attn_bwd(pa, bias, do, dlse, db_in, dqkv, gi, name):
    S = pa.shape[0]
    d = DILATIONS[gi]
    bt = WIN * d
    nb = S // bt
    hpw = _pairs_per_step(d)
    bw = hpw * LANES
    cb = 3 * gi // hpw

    def body(q_ref, kp_ref, kc_ref, vp_ref, vc_ref, b_ref, do_ref, dl_ref, dbi_ref, dqi_ref, dki_ref, dvi_ref,
             dq_ref, dk_ref, dv_ref, db_ref, ck, cv):
        n = pl.program_id(1)

        @pl.when(n == 0)
        def _():
            db_ref[...] = dbi_ref[...]
            ck[...] = jnp.zeros_like(ck)
            cv[...] = jnp.zeros_like(cv)

        @pl.when(n < nb)
        def _():
            valid = _attn_valid(n)
            bm = [jnp.where(valid, b_ref[k], NEG) for k in range(2 * hpw)]

            def residue(r, carry):
                sl = _residue_rows(r, d)
                cs = pl.ds(pl.multiple_of(r * WIN, WIN), WIN)
                for t in range(hpw):
                    ln = pl.ds(t * LANES, LANES)
                    _, vjp = jax.vjp(_attn_block, q_ref[sl, ln], kp_ref[sl, ln], kc_ref[sl, ln], vp_ref[sl, ln],
                                     vc_ref[sl, ln], bm[2 * t], bm[2 * t + 1])
                    dq, dkp, dkc, dvp, dvc, db0, db1 = vjp((do_ref[sl, ln], dl_ref[sl, ln]))
                    dq_ref[sl, ln] = dq
                    dk_ref[sl, ln] = ck[cs, ln] + dkp
                    dv_ref[sl, ln] = cv[cs, ln] + dvp
                    ck[cs, ln] = dkc
                    cv[cs, ln] = dvc
                    db_ref[2 * t] += db0
                    db_ref[2 * t + 1] += db1
                return carry

            _for_residues(d, residue)

        @pl.when(n == nb)
        def _():
            def residue(r, carry):
                sl = _residue_rows(r, d)
                cs = pl.ds(pl.multiple_of(r * WIN, WIN), WIN)
                dk_ref[sl, :] = ck[cs, :]
                dv_ref[sl, :] = cv[cs, :]
                return carry

            _for_residues(d, residue)

    def cur(n):
        return jnp.minimum(n, nb - 1)

    def spec(off, prev):
        if prev:
            return pl.BlockSpec((bt, bw), lambda hp, n: (jnp.maximum(cur(n) - 1, 0), off // hpw + cb + hp))
        return pl.BlockSpec((bt, bw), lambda hp, n: (cur(n), off // hpw + cb + hp))

    gspec = pl.BlockSpec((bt, bw), lambda hp, n: (cur(n), hp))
    bspec = pl.BlockSpec((2 * hpw, WIN, 2 * WIN), lambda hp, n: (hp, 0, 0))
    qspec = pl.BlockSpec((bt, bw), lambda hp, n: (cur(n), cb + hp))
    kspec = pl.BlockSpec((bt, bw), lambda hp, n: (jnp.maximum(n - 1, 0), cb + hp))
    dq, dk, dv, db = pl.pallas_call(
        body, name=name, grid=(3 // hpw, nb + 1),
        in_specs=[spec(0, False), spec(9, True), spec(9, False), spec(18, True), spec(18, False),
                  bspec, gspec, gspec, bspec, _ANY, _ANY, _ANY],
        out_specs=[qspec, kspec, kspec, bspec],
        out_shape=[jax.ShapeDtypeStruct((S, AW), F32)] * 3 + [jax.ShapeDtypeStruct((6, WIN, 2 * WIN), F32)],
        scratch_shapes=[pltpu.VMEM((bt, bw), F32), pltpu.VMEM((bt, bw), F32)],
        input_output_aliases={9: 0, 10: 1, 11: 2},
        compiler_params=pltpu.CompilerParams(dimension_semantics=("arbitrary", "arbitrary")),
    )(pa, pa, pa, pa, pa, bias, do, dlse, db_in, *dqkv)
    return (dq, dk, dv), db


def _mix_core(o0, o1, o2, l0, l1, l2):
    m = lax.stop_gradient(jnp.maximum(jnp.maximum(l0, l1), l2))
    e0, e1, e2 = jnp.exp(l0 - m), jnp.exp(l1 - m), jnp.exp(l2 - m)
    return (e0 * o0 + e1 * o1 + e2 * o2) / (e0 + e1 + e2)


def _mix_fwd(os_, ls_, name):
    S = os_[0].shape[0]
    ins = [("row", a, None, 0) for a in (*os_, *ls_)]
    return _rows(name, lambda ctx, *v: [_mix_core(*v)], ins, [(GW, GW, 0, BF16)], tm=256, nrows=S)[0]


def _mix_bwd(os_, ls_, datt, name):
    S = datt.shape[0]

    def fn(ctx, *v):
        _, vjp = jax.vjp(_mix_core, *v[:6])
        return list(vjp(v[6]))

    ins = [("row", a, None, 0) for a in (*os_, *ls_, datt)]
    outs = [(GW, GW, 0, F32)] * 6
    r = _rows(name, fn, ins, outs, tm=256, nrows=S)
    return r[:3], r[3:]


def _t5_bucket(dist):
    max_exact = REL_BUCKETS // 2
    is_small = dist < max_exact
    nf = jnp.maximum(dist, 1).astype(F32)
    large = max_exact + (jnp.log(nf / max_exact) / math.log(REL_MAX_DISTANCE / max_exact)
                         * (REL_BUCKETS - max_exact)).astype(jnp.int32)
    large = jnp.minimum(large, REL_BUCKETS - 1)
    return jnp.where(is_small, dist, large)


def _buckets(d):
    qi = jnp.arange(WIN)[:, None]
    kk = jnp.arange(2 * WIN)[None, :]
    rel = qi + WIN - kk
    return _t5_bucket(jnp.clip(rel, 0, None) * d)


def _pool_cnt(ctx, w):
    pos = ctx.row0 + _iota((ctx.rows, PG), 0) + 1
    return jnp.minimum(pos, w).astype(F32)


def _pool_d(ctx, halo, u):
    ds = []
    for g, w in enumerate(POOL_WINDOWS):
        ug = u[:, g * PG:(g + 1) * PG]
        s = _with_prev(ctx, halo[:, g * PG:(g + 1) * PG], ug)
        step = 1
        while step < w:
            s = s + _shift_down(s, step)
            step *= 2
        ds.append(s[HALO:] / _pool_cnt(ctx, w) - ug)
    return ds


def _pool_fwd(pb, pw, scale, name):
    S = pb.shape[0]

    def fn(ctx, halo, u, w, sc):
        ds = _pool_d(ctx, halo, u)
        return [jnp.concatenate([_dg(ds[k], w[k], 1, 0) for k in range(4)], axis=1) * sc]

    return _rows(name, fn, [("prev", pb, D, 0), ("row", pb, None, 0), ("raw", pw, None, 0), ("const", scale, None, 0)],
                 [(D, D, 0, BF16)], tm=256, nrows=S)[0]


def _pool_bwd(pb, pw, scale, dpo, name):
    S = pb.shape[0]

    def fn1(ctx, halo, u, w, sc, dy):
        ds = _pool_d(ctx, halo, u)
        dyp = dy * sc
        y = jnp.concatenate([_dg(ds[k], w[k], 1, 0) for k in range(4)], axis=1)
        es, dws = [], []
        for k, wd in enumerate(POOL_WINDOWS):
            cols = slice(k * PG, (k + 1) * PG)
            es.append(_dg(dyp[:, cols], w[k], 1, 1) / _pool_cnt(ctx, wd))
            dws.append(_dg(ds[k], dyp[:, cols], 0, 0))
        return [jnp.concatenate(es, axis=1), jnp.concatenate(dws, axis=0), jnp.sum(dy * y, axis=0, keepdims=True)]

    e, dpw, dsc = _rows(name + "_a", fn1,
                        [("prev", pb, D, 0), ("row", pb, None, 0), ("raw", pw, None, 0), ("const", scale, None, 0),
                         ("row", dpo, None, 0)],
                        [(D, D, 0, F32)], [(4 * PG, PG, PG), (1, D, D)], tm=256, nrows=S)

    def fn2(ctx, ev, halo):
        outs = []
        for g, w in enumerate(POOL_WINDOWS):
            eg = ev[:, g * PG:(g + 1) * PG]
            s = _with_next(ctx, eg, halo[:, g * PG:(g + 1) * PG])
            step = 1
            while step < w:
                s = s + _shift_up(s, step)
                step *= 2
            outs.append(s[:ctx.rows] - eg * _pool_cnt(ctx, w))
        return [jnp.concatenate(outs, axis=1)]

    du = _rows(name + "_b", fn2, [("row", e, None, 0), ("next", e, D, 0)], [(D, D, 0, BF16)], tm=256, nrows=S)[0]
    return du, dpw, dsc


def _conv_taps(ctx, halo, x, K):
    cat = _with_prev(ctx, halo, x)
    return [_shift_down(cat, K - 1 - k)[HALO:] for k in range(K)]


def _conv_pre(taps, w, b):
    acc = b
    for k, t in enumerate(taps):
        acc = acc + t * _row_pick(w, k)
    return acc


CW = 256
CWS = 512
CONV_TM = 512


def _ext_taps(ctx, prev, x, nxt, K):
    cat = jnp.concatenate([jnp.where(ctx.first, 0.0, prev), x, jnp.where(ctx.last, 0.0, nxt)], axis=0)
    return [_shift_down(cat, K - 1 - k)[HALO:] for k in range(K)]


def _conv_t_rows(dp, w, K, tm):
    acc = jnp.zeros((tm, dp.shape[1]), F32)
    for k in range(K):
        acc = acc + _shift_up(dp, K - 1 - k)[:tm] * _row_pick(w, k)
    return acc


def _ssd_conv_fwd(pc, w, b, name):
    S = pc.shape[0]
    base = D // CWS

    def fn(ctx, halo, x, wv, bv):
        return [_silu(_conv_pre(_conv_taps(ctx, halo, x, 4), wv, bv))]

    return _rows(name, fn, [("prev", pc, CWS, base), ("row", pc, CWS, base), ("ccol", w, CWS, 0), ("ccol", b, CWS, 0)],
                 [(XBC, CWS, 0, F32)], tm=CONV_TM, nrows=S, ncol=XBC // CWS)[0]


def _ssd_conv_bwd(pc, w, b, dy, name):
    S = pc.shape[0]
    base = D // CWS

    def fn(ctx, prev, x, nxt, wv, bv, dyv, dyn):
        n = ctx.rows
        taps = _ext_taps(ctx, prev, x, nxt, 4)
        pre = _conv_pre(taps, wv, bv)
        sg = _sigmoid(pre)
        dye = jnp.concatenate([dyv, jnp.where(ctx.last, 0.0, dyn)], axis=0)
        dpre = dye * sg * (1.0 + pre * (1.0 - sg))
        dw = _stack_rows([jnp.sum(dpre[:n] * t[:n], axis=0, keepdims=True) for t in taps], 4)
        return [_conv_t_rows(dpre, wv, 4, n), dw, jnp.sum(dpre[:n], axis=0, keepdims=True)]

    return _rows(name, fn,
                 [("prev", pc, CWS, base), ("row", pc, CWS, base), ("next", pc, CWS, base), ("ccol", w, CWS, 0),
                  ("ccol", b, CWS, 0), ("row", dy, CWS, 0), ("next", dy, CWS, 0)],
                 [(XBC, CWS, 0, BF16)], [(4, XBC, CWS), (1, XBC, CWS)], tm=CONV_TM, nrows=S, ncol=XBC // CWS)


NFC = D_FF // CW


def _ffn_act_fwd(h, w, b, name):
    S = h.shape[0]

    def fn(ctx, ha, a, hv, v, wa, wv, ba, bv):
        pa = _conv_pre(_conv_taps(ctx, ha, a, 3), wa, ba)
        pv = _conv_pre(_conv_taps(ctx, hv, v, 3), wv, bv)
        return [_silu(pa) * pv]

    return _rows(name, fn,
                 [("prev", h, CW, 0), ("row", h, CW, 0), ("prev", h, CW, NFC), ("row", h, CW, NFC),
                  ("ccol", w, CW, 0), ("ccol", w, CW, NFC), ("ccol", b, CW, 0), ("ccol", b, CW, NFC)],
                 [(D_FF, CW, 0, BF16)], tm=CONV_TM, nrows=S, ncol=NFC)[0]


def _ffn_act_bwd(h, w, b, df, name):
    S = h.shape[0]

    def fn(ctx, pa_, a, na, pv_, v, nv, wa, wv, ba, bv, dfv, dfn):
        n = ctx.rows
        ta = _ext_taps(ctx, pa_, a, na, 3)
        tv = _ext_taps(ctx, pv_, v, nv, 3)
        pa = _conv_pre(ta, wa, ba)
        pv = _conv_pre(tv, wv, bv)
        sg = _sigmoid(pa)
        dfe = jnp.concatenate([dfv, jnp.where(ctx.last, 0.0, dfn)], axis=0)
        dpa = dfe * pv * sg * (1.0 + pa * (1.0 - sg))
        dpv = dfe * pa * sg
        res = [_conv_t_rows(dpa, wa, 3, n), _conv_t_rows(dpv, wv, 3, n)]
        for dp, taps in ((dpa, ta), (dpv, tv)):
            res.append(_stack_rows([jnp.sum(dp[:n] * t[:n], axis=0, keepdims=True) for t in taps], 3))
        for dp in (dpa, dpv):
            res.append(jnp.sum(dp[:n], axis=0, keepdims=True))
        return res

    ins = []
    for base in (0, NFC):
        ins += [("prev", h, CW, base), ("row", h, CW, base), ("next", h, CW, base)]
    ins += [("ccol", w, CW, 0), ("ccol", w, CW, NFC), ("ccol", b, CW, 0), ("ccol", b, CW, NFC),
            ("row", df, CW, 0), ("next", df, CW, 0)]
    dha, dhv, dwa, dwv, dba, dbv = _rows(
        name, fn, ins, [(D_FF, CW, 0, BF16)] * 2, [(3, D_FF, CW)] * 2 + [(1, D_FF, CW)] * 2, tm=CONV_TM, nrows=S, ncol=NFC)
    return dha, dhv, jnp.concatenate([dwa, dwv], axis=1), jnp.concatenate([dba, dbv], axis=1)


NSLAB = D // LANES
CPS = 2


def _ssd_chunk(xs, Bs, Cs, dtraw, dtb, alog, prev):
    lsz = SSD_CHUNK
    lane = _iota((lsz, LANES), 1)
    row = _iota((lsz, LANES), 0)
    dt = jnp.where(lane < SSD_HEADS, _softplus(dtraw + dtb), 0.0)
    a = dt * (-jnp.exp(alog))
    tril = row >= lane
    a_cs = _fdot(tril.astype(F32), a)
    a_cst = a_cs.T
    a_last = jnp.sum(a, axis=0, keepdims=True)
    lo = lane < HD
    top = row < HD
    cbs = [_bdot_nt(Cs[g], Bs[g]) for g in range(2)]
    ys, news = [], []
    for s in range(NSLAB):
        g = s // (NSLAB // 2)
        cols, lms, dts, als = [], [], [], []
        for hh in range(2):
            h = 2 * s + hh
            col = _lane_pick(a_cs, h)
            seg = col - _row_pick(a_cst, h)
            lms.append(jnp.exp(jnp.where(tril, seg, NEG)))
            cols.append(col)
            dts.append(_lane_pick(dt, h))
            als.append(_lane_pick(a_last, h))
        col_x = jnp.where(lo, cols[0], cols[1])
        al_x = jnp.where(lo, als[0], als[1])
        xc = xs[s] * jnp.where(lo, dts[0], dts[1])
        yd = jnp.where(lo, _bdot_nn(cbs[g] * lms[0], xc), _bdot_nn(cbs[g] * lms[1], xc))
        yoff = _bdot_nt(Cs[g], prev[s]) * jnp.exp(col_x)
        ys.append(yd + yoff)
        st = _bdot_tn(xc * jnp.exp(al_x - col_x), Bs[g])
        news.append(prev[s] * jnp.exp(jnp.where(top, als[0], als[1])) + st)
    return ys, news


def _ssd_scan_fwd(xbc_c, pd, dtb, alog, name):
    S = xbc_c.shape[0]
    nc = S // SSD_CHUNK
    rows_ = CPS * SSD_CHUNK

    def body(x_ref, b_ref, c_ref, dt_ref, dtb_ref, al_ref, y_ref, st_ref, state):
        c = pl.program_id(0)

        @pl.when(c == 0)
        def _():
            state[...] = jnp.zeros_like(state)

        prev = [state[s * LANES:(s + 1) * LANES, :] for s in range(NSLAB)]
        for u in range(CPS):
            rw = pl.ds(u * SSD_CHUNK, SSD_CHUNK)
            xs = [x_ref[rw, s * LANES:(s + 1) * LANES] for s in range(NSLAB)]
            Bs = [b_ref[rw, g * SSD_N:(g + 1) * SSD_N] for g in range(2)]
            Cs = [c_ref[rw, g * SSD_N:(g + 1) * SSD_N] for g in range(2)]
            for s in range(NSLAB):
                st_ref[u, s * LANES:(s + 1) * LANES, :] = prev[s]
            ys, prev = _ssd_chunk(xs, Bs, Cs, dt_ref[rw, :].astype(F32), dtb_ref[...], al_ref[...], prev)
            for s in range(NSLAB):
                y_ref[rw, s * LANES:(s + 1) * LANES] = ys[s]
        for s in range(NSLAB):
            state[s * LANES:(s + 1) * LANES, :] = prev[s]

    return pl.pallas_call(
        body, name=name, grid=(nc // CPS,),
        in_specs=[pl.BlockSpec((rows_, D), lambda c: (c, 0)),
                  pl.BlockSpec((rows_, 2 * SSD_N), lambda c: (c, D // (2 * SSD_N))),
                  pl.BlockSpec((rows_, 2 * SSD_N), lambda c: (c, D // (2 * SSD_N) + 1)),
                  pl.BlockSpec((rows_, LANES), lambda c: (c, 0)),
                  pl.BlockSpec((1, LANES), lambda c: (0, 0)), pl.BlockSpec((1, LANES), lambda c: (0, 0))],
        out_specs=[pl.BlockSpec((rows_, D), lambda c: (c, 0)), pl.BlockSpec((CPS, D, SSD_N), lambda c: (c, 0, 0))],
        out_shape=[jax.ShapeDtypeStruct((S, D), F32), jax.ShapeDtypeStruct((nc, D, SSD_N), F32)],
        scratch_shapes=[pltpu.VMEM((D, SSD_N), F32)],
        compiler_params=pltpu.CompilerParams(dimension_semantics=("arbitrary",)),
    )(xbc_c, xbc_c, xbc_c, pd, dtb, alog)


def _ssd_scan_bwd(xbc_c, pd, dtb, alog, states, dy, dxs_skip, name):
    S = xbc_c.shape[0]
    nc = S // SSD_CHUNK
    rows_ = CPS * SSD_CHUNK

    def body(x_ref, b_ref, c_ref, dt_ref, dtb_ref, al_ref, st_ref, dy_ref, sk_ref,
             dx_ref, ddt_ref, ddtb_ref, dal_ref, dstate):
        c = pl.program_id(0)

        @pl.when(c == 0)
        def _():
            dstate[...] = jnp.zeros_like(dstate)
            ddtb_ref[...] = jnp.zeros_like(ddtb_ref)
            dal_ref[...] = jnp.zeros_like(dal_ref)

        dnew = [dstate[s * LANES:(s + 1) * LANES, :] for s in range(NSLAB)]
        for u in reversed(range(CPS)):
            rw = pl.ds(u * SSD_CHUNK, SSD_CHUNK)
            xs = [x_ref[rw, s * LANES:(s + 1) * LANES] for s in range(NSLAB)]
            Bs = [b_ref[rw, g * SSD_N:(g + 1) * SSD_N] for g in range(2)]
            Cs = [c_ref[rw, g * SSD_N:(g + 1) * SSD_N] for g in range(2)]
            prev = [st_ref[u, s * LANES:(s + 1) * LANES, :] for s in range(NSLAB)]
            _, vjp = jax.vjp(_ssd_chunk, xs, Bs, Cs, dt_ref[rw, :].astype(F32), dtb_ref[...], al_ref[...], prev)
            dys = [dy_ref[rw, s * LANES:(s + 1) * LANES] for s in range(NSLAB)]
            dxs, dBs, dCs, ddt, ddtb, dal, dnew = vjp((dys, dnew))
            for s in range(NSLAB):
                dx_ref[rw, s * LANES:(s + 1) * LANES] = dxs[s] + sk_ref[rw, s * LANES:(s + 1) * LANES]
            for g in range(2):
                dx_ref[rw, D + g * SSD_N:D + (g + 1) * SSD_N] = dBs[g]
                dx_ref[rw, D + 2 * SSD_N + g * SSD_N:D + 2 * SSD_N + (g + 1) * SSD_N] = dCs[g]
            ddt_ref[rw, :] = ddt
            ddtb_ref[...] += ddtb
            dal_ref[...] += dal
        for s in range(NSLAB):
            dstate[s * LANES:(s + 1) * LANES, :] = dnew[s]

    def rv(c):
        return nc // CPS - 1 - c

    return pl.pallas_call(
        body, name=name, grid=(nc // CPS,),
        in_specs=[pl.BlockSpec((rows_, D), lambda c: (rv(c), 0)),
                  pl.BlockSpec((rows_, 2 * SSD_N), lambda c: (rv(c), D // (2 * SSD_N))),
                  pl.BlockSpec((rows_, 2 * SSD_N), lambda c: (rv(c), D // (2 * SSD_N) + 1)),
                  pl.BlockSpec((rows_, LANES), lambda c: (rv(c), 0)),
                  pl.BlockSpec((1, LANES), lambda c: (0, 0)), pl.BlockSpec((1, LANES), lambda c: (0, 0)),
                  pl.BlockSpec((CPS, D, SSD_N), lambda c: (rv(c), 0, 0)),
                  pl.BlockSpec((rows_, D), lambda c: (rv(c), 0)),
                  pl.BlockSpec((rows_, D), lambda c: (rv(c), 0))],
        out_specs=[pl.BlockSpec((rows_, XBC), lambda c: (rv(c), 0)),
                   pl.BlockSpec((rows_, LANES), lambda c: (rv(c), 0)),
                   pl.BlockSpec((1, LANES), lambda c: (0, 0)), pl.BlockSpec((1, LANES), lambda c: (0, 0))],
        out_shape=[jax.ShapeDtypeStruct((S, XBC), F32), jax.ShapeDtypeStruct((S, LANES), F32),
                   jax.ShapeDtypeStruct((1, LANES), F32), jax.ShapeDtypeStruct((1, LANES), F32)],
        scratch_shapes=[pltpu.VMEM((D, SSD_N), F32)],
        compiler_params=pltpu.CompilerParams(dimension_semantics=("arbitrary",)),
    )(xbc_c, xbc_c, xbc_c, pd, dtb, alog, states, dy, dxs_skip)


def _ssd_post_core(y, xs, z, d128, nw):
    tm = y.shape[0]
    ex = (_iota((LANES, D), 1) // HD == _iota((LANES, D), 0)).astype(F32)
    d_x = jnp.sum(_fdot(jnp.broadcast_to(d128, (8, LANES)), ex), axis=0, keepdims=True) * 0.125
    y2 = (y + d_x * xs) * _silu(z)
    lo = _iota((tm, D), 1) < D // 2
    sq = y2 * y2
    ms0 = jnp.sum(jnp.where(lo, sq, 0.0), axis=-1, keepdims=True) / (D // 2)
    ms1 = jnp.sum(jnp.where(lo, 0.0, sq), axis=-1, keepdims=True) / (D // 2)
    r = jnp.where(lo, lax.rsqrt(ms0 + EPS), lax.rsqrt(ms1 + EPS))
    return y2 * r * nw


def _ssd_post_ins(y, xbc_c, pc, d128, nw):
    return [("row", y, None, 0), ("row", xbc_c, D, 0), ("row", pc, D, 0), ("const", d128, None, 0), ("const", nw, None, 0)]


def _ssd_post_fwd(y, xbc_c, pc, d128, nw, name):
    S = y.shape[0]
    return _rows(name, lambda ctx, *v: [_ssd_post_core(*v)], _ssd_post_ins(y, xbc_c, pc, d128, nw),
                 [(D, D, 0, BF16)], tm=256, nrows=S)[0]


def _ssd_post_bwd(y, xbc_c, pc, d128, nw, dout, name):
    S = y.shape[0]

    def fn(ctx, *v):
        _, vjp = jax.vjp(_ssd_post_core, *v[:5])
        return list(vjp(v[5]))

    return _rows(name, fn, _ssd_post_ins(y, xbc_c, pc, d128, nw) + [("row", dout, None, 0)],
                 [(D, D, 0, F32), (D, D, 0, F32), (D, D, 0, BF16)], [(1, LANES, LANES), (1, D, D)], tm=256, nrows=S)


def _gates_core(g0, g1, g2, b0, b1, b2, ya, yb, yc):
    return _sigmoid(g0 + b0) * ya + _sigmoid(g1 + b1) * yb + _sigmoid(g2 + b2) * yc


def _gate_parts(pdv, bv):
    gp = pltpu.roll(pdv, SEC_D - 16, 1)
    return [gp[:, k * D:(k + 1) * D] for k in range(3)] + [bv[:, k * D:(k + 1) * D] for k in range(3)]


def _gates_fwd(pd, bg, ya, yb, yc, name):
    S = pd.shape[0]

    def fn(ctx, pdv, bv, a, b, c):
        return [_gates_core(*_gate_parts(pdv, bv), a, b, c)]

    return _rows(name, fn, [("row", pd, None, 0), ("const", bg, None, 0), ("row", ya, None, 0), ("row", yb, None, 0),
                            ("row", yc, None, 0)], [(D, D, 0, BF16)], tm=256, nrows=S)[0]


def _gates_post(dm, pdv, a, b, c, bv):
    _, vjp = jax.vjp(_gates_core, *_gate_parts(pdv, bv), a, b, c)
    g = vjp(dm)
    return [g[6], g[7], g[8], jnp.concatenate(g[0:3], axis=1), jnp.concatenate(g[3:6], axis=1)]


def _adam_update(wv, gv, mv, vv):
    m2 = ADAM_B1 * mv + (1.0 - ADAM_B1) * gv
    v2 = ADAM_B2 * vv + (1.0 - ADAM_B2) * jnp.square(gv)
    m_hat = m2 / (1.0 - ADAM_B1 ** ADAM_STEP)
    v_hat = v2 / (1.0 - ADAM_B2 ** ADAM_STEP)
    delta = -ADAM_LR * (m_hat / (jnp.sqrt(v_hat) + ADAM_EPS) + ADAM_WD * wv)
    return [delta, m2, v2]


def _adamw(w, g, m, v, name):
    rows, C = w.shape
    tm = _pick(rows, [t for t in (512, 256, 128, 64, 32, 16, 8) if t * C <= ADAM_TILE])
    return _rows(name, lambda ctx, *a: _adam_update(*a), [("row", a, None, 0) for a in (w, g, m, v)],
                 [(C, C, 0, F32)] * 3, tm=tm, nrows=rows)


def _position():
    return lax.axis_index("x"), lax.axis_index("y"), lax.axis_index("c")


def _other_chips(x, y):
    return [(1 - x, y), (x, 1 - y), (1 - x, 1 - y)]


_HBM = pl.BlockSpec(memory_space=pltpu.HBM)


def _gather_parts(half, lo, n):
    def copies(p_ref, out_ref, send_sems, recv_sems):
        x, y, c = _position()
        sibling = (x, y, 1 - c)
        chips = _other_chips(x, y)

        def slab(chip, h):
            return out_ref.at[2 * chip[0] + chip[1], pl.ds(h * half + lo, n), :]

        def copy(k, src, dst, to):
            return pltpu.make_async_remote_copy(src_ref=src, dst_ref=dst, send_sem=send_sems.at[k],
                                                recv_sem=recv_sems.at[k], device_id=to, device_id_type=MESH)

        first = [copy(j, p_ref.at[pl.ds(c * half + lo, n), :], slab((x, y), c), (*chip, c)) for j, chip in enumerate(chips)]
        passed = [copy(3 + j, slab(chip, c), slab(chip, c), sibling) for j, chip in enumerate(chips)]
        from_chips = [copy(j, slab(chip, c), slab(chip, c), (x, y, c)) for j, chip in enumerate(chips)]
        from_sibling = [copy(3 + j, slab(chip, 1 - c), slab(chip, 1 - c), (x, y, c)) for j, chip in enumerate(chips)]
        return first, passed, from_chips, from_sibling

    def start(ins, outs, scr):
        for cp in copies(ins[0], outs[0], *scr)[0]:
            cp.start()

    def finish(ins, outs, scr):
        first, passed, from_chips, from_sibling = copies(ins[0], outs[0], *scr)
        for j in range(3):
            from_chips[j].wait_recv()
            passed[j].start()
        for cp in from_sibling:
            cp.wait_recv()
        for cp in first + passed:
            cp.wait_send()

    return start, finish


def _rs_chip_parts(lo, n):
    def copies(h_ref, out_ref, send_sems, recv_sems):
        x, y, c = _position()
        return [pltpu.make_async_remote_copy(src_ref=h_ref.at[2 * chip[0] + chip[1], pl.ds(lo, n), :],
                                             dst_ref=out_ref.at[j, pl.ds(lo, n), :],
                                             send_sem=send_sems.at[j], recv_sem=recv_sems.at[j],
                                             device_id=(*chip, c), device_id_type=MESH)
                for j, chip in enumerate(_other_chips(x, y))]

    def start(ins, outs, scr):
        for cp in copies(ins[0], outs[0], *scr):
            cp.start()

    def finish(ins, outs, scr):
        for cp in copies(ins[0], outs[0], *scr):
            cp.wait()

    return start, finish


class _Stream:
    def __init__(self, src, buf, parts, nsem, units, name):
        self.src, self.buf, self.parts, self.nsem, self.name = src, buf, parts, nsem, name
        self.next, self.units = 0, units

    def _scratch(self):
        return [pltpu.SemaphoreType.DMA((self.nsem,)), pltpu.SemaphoreType.DMA((self.nsem,))]

    def _take(self, units):
        units = min(units, self.units - self.next)
        lo = self.next * 16
        self.next += units
        return lo, units * 16

    def _set(self, outs):
        self.buf = outs[0]

    def hook(self, units):
        lo, n = self._take(units)
        if n == 0:
            return None
        start, finish = self.parts(lo, n)
        return _Hook([self.src, self.buf], [jax.ShapeDtypeStruct(self.buf.shape, self.buf.dtype)], {1: 0},
                     self._scratch(), start, finish, self._set)

    def drain(self):
        lo, n = self._take(self.units)
        if n:
            start, finish = self.parts(lo, n)

            def body(s_ref, b_ref, o_ref, send_sems, recv_sems):
                args = ((s_ref, b_ref), (o_ref,), (send_sems, recv_sems))
                start(*args)
                finish(*args)

            self.buf = pl.pallas_call(
                body, name=self.name, in_specs=[_ANY, _ANY], out_specs=_ANY,
                out_shape=jax.ShapeDtypeStruct(self.buf.shape, self.buf.dtype),
                scratch_shapes=self._scratch(), input_output_aliases={1: 0},
            )(self.src, self.buf)
        return self.buf


def _rs_pair_parts(half, lo, n):
    def copy(g_ref, out_ref, send_sems, recv_sems):
        x, y, c = _position()
        return pltpu.make_async_remote_copy(
            src_ref=g_ref.at[pl.ds(0, 4), pl.ds((1 - c) * half + lo, n), :], dst_ref=out_ref.at[pl.ds(0, 4), pl.ds(lo, n), :],
            send_sem=send_sems.at[0], recv_sem=recv_sems.at[0], device_id=(x, y, 1 - c), device_id_type=MESH)

    def start(ins, outs, scr):
        copy(ins[0], outs[0], *scr).start()

    def finish(ins, outs, scr):
        copy(ins[0], outs[0], *scr).wait()

    return start, finish


def _rs_swap(r, name):
    Rh, C = r.shape

    def body(r_ref, out_ref, send_sem, recv_sem):
        x, y, c = _position()
        cp = pltpu.make_async_remote_copy(src_ref=r_ref, dst_ref=out_ref, send_sem=send_sem,
                                          recv_sem=recv_sem, device_id=(x, y, 1 - c), device_id_type=MESH)
        cp.start()
        cp.wait()

    return pl.pallas_call(
        body, name=name, in_specs=[_HBM], out_specs=_HBM,
        out_shape=jax.ShapeDtypeStruct((Rh, C), r.dtype),
        scratch_shapes=[pltpu.SemaphoreType.DMA, pltpu.SemaphoreType.DMA],
    )(r)


def _rs_add_pair(g, recv, cidx, name):
    _, R, C = g.shape
    Rh = R // 2
    tm = _pick(Rh, (400, 280, 200, 160, 80, 40, 16, 8))
    nt = Rh // tm

    def body(c_ref, g_ref, r_ref, o_ref):
        o_ref[...] = (g_ref[...].astype(F32) + r_ref[...].astype(F32)).astype(o_ref.dtype)

    return pl.pallas_call(
        body, name=name,
        grid_spec=pltpu.PrefetchScalarGridSpec(
            num_scalar_prefetch=1, grid=(4, nt),
            in_specs=[pl.BlockSpec((1, tm, C), lambda k, i, cr: (k, cr[0] * nt + i, 0)),
                      pl.BlockSpec((1, tm, C), lambda k, i, cr: (k, i, 0))],
            out_specs=pl.BlockSpec((1, tm, C), lambda k, i, cr: (k, i, 0))),
        out_shape=jax.ShapeDtypeStruct((4, Rh, C), BF16),
    )(cidx, g, recv)


def _rs_add_chips(h, recv, chip_idx, name):
    _, Rh, C = h.shape
    tm = _pick(Rh, (400, 280, 200, 160, 80, 40, 16, 8))

    def body(c_ref, h_ref, r_ref, o_ref):
        acc = h_ref[0].astype(F32)
        for j in range(3):
            acc = acc + r_ref[j].astype(F32)
        o_ref[...] = acc

    return pl.pallas_call(
        body, name=name,
        grid_spec=pltpu.PrefetchScalarGridSpec(
            num_scalar_prefetch=1, grid=(Rh // tm,),
            in_specs=[pl.BlockSpec((1, tm, C), lambda i, cr: (cr[0], i, 0)), pl.BlockSpec((3, tm, C), lambda i, cr: (0, i, 0))],
            out_specs=pl.BlockSpec((tm, C), lambda i, cr: (i, 0))),
        out_shape=jax.ShapeDtypeStruct((Rh, C), F32),
    )(chip_idx, h, recv)


def _all_reduce_small(vec, name):
    n, C = vec.shape

    def body(v_ref, out_ref, buf, send_sems, recv_sems):
        x, y, c = _position()

        def flip(k):
            return ((1 - x) if k & 4 else x, (1 - y) if k & 2 else y, (1 - c) if k & 1 else c)

        def idx(p):
            return 4 * p[0] + 2 * p[1] + p[2]

        me = idx((x, y, c))
        buf[me] = v_ref[...]
        cps = [pltpu.make_async_remote_copy(src_ref=v_ref, dst_ref=buf.at[me], send_sem=send_sems.at[k - 1],
                                            recv_sem=recv_sems.at[k - 1], device_id=flip(k), device_id_type=MESH)
               for k in range(1, 8)]
        for cp in cps:
            cp.start()
        for k in range(1, 8):
            pltpu.make_async_remote_copy(src_ref=v_ref, dst_ref=buf.at[idx(flip(k))], send_sem=send_sems.at[k - 1],
                                         recv_sem=recv_sems.at[k - 1], device_id=flip(k), device_id_type=MESH).wait_recv()
        for cp in cps:
            cp.wait_send()
        acc = buf[0]
        for s in range(1, 8):
            acc = acc + buf[s]
        out_ref[...] = acc

    return pl.pallas_call(
        body, name=name,
        in_specs=[pl.BlockSpec(memory_space=pltpu.VMEM)], out_specs=pl.BlockSpec(memory_space=pltpu.VMEM),
        out_shape=jax.ShapeDtypeStruct((n, C), F32),
        scratch_shapes=[pltpu.VMEM((8, n, C), F32), pltpu.SemaphoreType.DMA((7,)), pltpu.SemaphoreType.DMA((7,))],
    )(vec)


BIG = (("w_in", (D, IN_WIDTH // 4), "cols"), ("w_a", (GW, D // 4), "cols"), ("pool_w", (4, PG // 4, PG), "pool"),
       ("w_b", (D // 4, D), "rows"), ("w_c", (D // 4, D), "rows"), ("w_o", (D // 4, D), "rows"),
       ("ffn_w_up", (D, 2 * D_FF // 4), "cols"), ("ffn_w_down", (D_FF // 4, D), "rows"))
def _pack_rows(s):
    k = math.prod(s) // D
    return -(-k // 16) * 16, k


PACK_ROWS = sum(_pack_rows(s)[0] for _, s, _ in BIG)
PACK_PAD = -(-PACK_ROWS // 32) * 32


def _pad_rows(v, rows):
    pad = [(0, 0)] * v.ndim
    pad[-2] = (0, rows - v.shape[-2])
    return jnp.pad(v, pad) if rows > v.shape[-2] else v


def _pack_blocks(blocks, dtype):
    lead = blocks["w_in"].shape[:-2]
    flat = []
    for n, s, how in BIG:
        v = blocks[n].astype(dtype)
        if how == "cols":
            v = jnp.swapaxes(v, -1, -2)
        flat.append(_pad_rows(v.reshape(*lead, -1, D), _pack_rows(s)[0]))
    flat.append(jnp.zeros((*lead, PACK_PAD - PACK_ROWS, D), dtype))
    return jnp.concatenate(flat, axis=-2)


def _unpack_blocks(pack):
    out, r = {}, 0
    for n, s, how in BIG:
        rows, k = _pack_rows(s)
        v = pack[r:r + k, :]
        out[n] = v.reshape(s[1], s[0]).T if how == "cols" else v.reshape(s)
        r += rows
    return out


def _operands(allp):
    out, r = {}, 0
    for n, s, how in BIG:
        rows, k = _pack_rows(s)
        v = allp[:, r:r + k, :]
        if how == "cols":
            out[n] = v.reshape(4 * s[1], s[0])
        elif how == "rows":
            out[n] = v.reshape(4 * s[0], s[1])
        else:
            out[n] = v.reshape(4, *s).transpose(1, 0, 2, 3).reshape(4, PG, PG)
        r += rows
    return out


def _pack_operands(g, dtype):
    flat = []
    for n, s, how in BIG:
        v = g[n].astype(dtype)
        if how == "pool":
            v = v.reshape(4, 4, s[1], s[2]).transpose(1, 0, 2, 3)
        flat.append(_pad_rows(v.reshape(4, -1, D), _pack_rows(s)[0]))
    flat.append(jnp.zeros((4, PACK_PAD - PACK_ROWS, D), dtype))
    return jnp.concatenate(flat, axis=1)


def _layer_fwd(x, w, sm, bias, hk):
    pa, u = _mmf(None, w["in_a"], tb=True, pre=(_rms_core, [x], [sm["ln1_g"]]), name="in_a", tm=1024, hook=hk("in_a"))
    pb = _mm(u, w["in_b"], tb=True, out_dtype=BF16, name="in_b", hook=hk("in_b"))
    pc = _mm(u, w["in_c"], tb=True, out_dtype=BF16, name="in_c", hook=hk("in_c"))
    pd = _mm(u, w["in_d"], tb=True, out_dtype=BF16, name="in_d", hook=hk("in_d"))
    os_, ls_ = [], []
    for gi in range(3):
        o, l = _attn_fwd(pa, bias[gi], gi, "attn_fwd%d" % gi)
        os_.append(o)
        ls_.append(l)
    att = _mix_fwd(os_, ls_, "mix_fwd")
    ya = _mm(att, w["w_a"], tb=True, out_dtype=BF16, name="mm_wa")
    pool_o = _pool_fwd(pb, w["pool_w"], sm["pool_scale"], "pool_fwd")
    yb = _mm(pool_o, w["w_b"], out_dtype=BF16, name="mm_wb")
    xbc_c = _ssd_conv_fwd(pc, sm["ssd_conv_w"], sm["ssd_conv_b"], "ssd_conv_fwd")
    y_scan, states = _ssd_scan_fwd(xbc_c, pd, sm["ssd_dt_bias"], sm["ssd_a_log"], "ssd_scan_fwd")
    ssd_o = _ssd_post_fwd(y_scan, xbc_c, pc, sm["ssd_d"], sm["ssd_norm_w"], "ssd_post_fwd")
    yc = _mm(ssd_o, w["w_c"], out_dtype=BF16, name="mm_wc")
    merged = _gates_fwd(pd, sm["b_gate"], ya, yb, yc, "gates_fwd")
    x1 = _mm(merged, w["w_o"], add=x, name="mm_wo", hook=hk("mm_wo"))
    h, u2 = _mmf(None, w["ffn_w_up"], tb=True, pre=(_rms_core, [x1], [sm["ln2_g"]]), out_dtype=BF16, name="mm_up",
                 tm=1024, hook=hk("mm_up"))
    f = _ffn_act_fwd(h, sm["ffn_conv_w"], sm["ffn_conv_b"], "ffn_act_fwd")
    x2 = _mm(f, w["ffn_w_down"], add=x1, name="mm_down", hook=hk("mm_down"))
    saved = dict(x=x, u=u, pa=pa, pb=pb, pc=pc, pd=pd, os=os_, ls=ls_, att=att, ya=ya, yb=yb, yc=yc, pool_o=pool_o,
                 xbc_c=xbc_c, y_scan=y_scan, states=states, ssd_o=ssd_o, merged=merged, x1=x1, u2=u2, h=h, f=f)
    return x2, saved


def _layer_bwd(dx2, dx2b, w, sm, bias, dbs, sv, hk):
    gw, gs = {}, {}
    S = dx2.shape[0]

    def gmm(a, b, name):
        return _mm(a, b, ta=True, out_dtype=BF16, name=name, hook=hk(name))

    df = _mm(dx2b, w["ffn_w_down"], tb=True, out_dtype=BF16, name="d_f", hook=hk("d_f"))
    gw["ffn_w_down"] = gmm(sv["f"], dx2b, "g_down")
    dha, dhv, gs["ffn_conv_w"], gs["ffn_conv_b"] = _ffn_act_bwd(sv["h"], sm["ffn_conv_w"], sm["ffn_conv_b"], df, "ffn_act_bwd")
    dx1, dx1b, gs["ln2_g"] = _mmf([dha, dhv], [w["up_a"], w["up_v"]], name="d_u2_v", tm=256, hook=hk("d_u2_v"),
                                  post=(_rms_post, [sv["x1"], dx2], [sm["ln2_g"]], RMS_POST_OUTS, [(1, D)]))
    gw["ffn_w_up"] = jnp.concatenate([gmm(dha, sv["u2"], "g_up_a"), gmm(dhv, sv["u2"], "g_up_v")], axis=0)
    dya, dyb, dyc, dgate, gs["b_gate"] = _mmf(
        dx1b, w["w_o"], tb=True, name="d_merged", tm=256, hook=hk("d_merged"),
        post=(_gates_post, [sv["pd"], sv["ya"], sv["yb"], sv["yc"]], [sm["b_gate"]],
              [(D, BF16)] * 3 + [(3 * D, BF16)], [(1, 3 * D)]))
    gw["w_o"] = gmm(sv["merged"], dx1b, "g_wo")
    dssd_o = _mm(dyc, w["w_c"], tb=True, name="d_ssd_o")
    gw["w_c"] = gmm(sv["ssd_o"], dyc, "g_wc")
    dy_scan, dxs_skip, dz, gs["ssd_d"], gs["ssd_norm_w"] = _ssd_post_bwd(
        sv["y_scan"], sv["xbc_c"], sv["pc"], sm["ssd_d"], sm["ssd_norm_w"], dssd_o, "ssd_post_bwd")
    dxbc_c, ddt, gs["ssd_dt_bias"], gs["ssd_a_log"] = _ssd_scan_bwd(
        sv["xbc_c"], sv["pd"], sm["ssd_dt_bias"], sm["ssd_a_log"], sv["states"], dy_scan, dxs_skip, "ssd_scan_bwd")
    dxbc, gs["ssd_conv_w"], gs["ssd_conv_b"] = _ssd_conv_bwd(sv["pc"], sm["ssd_conv_w"], sm["ssd_conv_b"], dxbc_c, "ssd_conv_bwd")
    dpool_o = _mm(dyb, w["w_b"], tb=True, name="d_pool_o")
    gw["w_b"] = gmm(sv["pool_o"], dyb, "g_wb")
    dpb, dpw, gs["pool_scale"] = _pool_bwd(sv["pb"], w["pool_w"], sm["pool_scale"], dpool_o, "pool_bwd")
    gw["pool_w"] = dpw.reshape(4, PG, PG)
    datt = _mm(dya, w["w_a"], name="d_att")
    gw["w_a"] = gmm(dya, sv["att"], "g_wa")
    dos, dls = _mix_bwd(sv["os"], sv["ls"], datt, "mix_bwd")
    dqkv = tuple(lax.empty((S, AW), F32) for _ in range(3))
    dbs = list(dbs)
    for gi in range(3):
        dqkv, dbs[gi] = _attn_bwd(sv["pa"], bias[gi], dos[gi], dls[gi], dbs[gi], dqkv, gi, "attn_bwd%d" % gi)
    u = sv["u"]
    pieces = [(dqkv[0], "wq"), (dqkv[1], "wk"), (dqkv[2], "wv"), (dpb, "in_b"), (dz, "wz"), (dxbc, "wxbc"),
              (ddt, "wdt"), (dgate, "wgate")]
    du = _mmf([dp for dp, _ in pieces[:4]], [w[key] for _, key in pieces[:4]], name="d_u_a", tm=256, hook=hk("d_u_a"))[0]
    dx, dxb, gs["ln1_g"] = _mmf([dp for dp, _ in pieces[4:]], [w[key] for _, key in pieces[4:]], add=du,
                                name="d_u_wgate", tm=256, hook=hk("d_u_wgate"),
                                post=(_rms_post, [sv["x"], dx1], [sm["ln1_g"]], RMS_POST_OUTS, [(1, D)]))
    g_in = []
    for dp, key in pieces:
        g = gmm(dp, u, "g_in_" + key)
        g_in.append(g[:SSD_HEADS] if key == "wdt" else g)
    gw["w_in"] = jnp.concatenate(g_in, axis=0)
    return dx, dxb, gw, gs, dbs


SMALL_LAYER = ("ln1_g", "b_gate", "pool_scale", "ssd_conv_w", "ssd_conv_b", "ssd_dt_bias", "ssd_a_log", "ssd_d",
               "ssd_norm_w", "ln2_g", "ffn_conv_w", "ffn_conv_b")


def _pad_lanes(v):
    return jnp.pad(v, (0, LANES - v.shape[0])).reshape(1, LANES)


def _layer_weights(ops):
    wt = ops["w_in"]
    o1, o2, o3 = SEC_A, SEC_A + SEC_B, SEC_A + SEC_B + SEC_C
    w = dict(ops)
    w["in_a"] = jnp.pad(wt[:o1], ((0, SEC_A_PAD - o1), (0, 0)))
    w["in_b"] = wt[o1:o2]
    w["in_c"] = wt[o2:o3]
    w["in_d"] = jnp.pad(wt[o3:], ((0, SEC_D - (IN_WIDTH - o3)), (0, 0)))
    w["wq"], w["wk"], w["wv"] = wt[:AW], wt[AW:2 * AW], wt[2 * AW:o1]
    w["wz"], w["wxbc"] = wt[o2:o2 + D], wt[o2 + D:o3]
    w["wdt"] = jnp.pad(wt[o3:o3 + SSD_HEADS], ((0, LANES - SSD_HEADS), (0, 0)))
    w["wgate"] = wt[o3 + SSD_HEADS:]
    w["up_a"], w["up_v"] = ops["ffn_w_up"][:D_FF], ops["ffn_w_up"][D_FF:]
    return w


def _layer_small(p, i):
    sm = {n: p[n][i] for n in SMALL_LAYER}
    out = {}
    for n, v in sm.items():
        if n in ("ssd_dt_bias", "ssd_a_log", "ssd_d"):
            out[n] = _pad_lanes(v)
        elif v.ndim == 1:
            out[n] = v.reshape(1, -1)
        else:
            out[n] = v
    return out


def _local_step(x, target, rel_bias, final_g, layer_full, small, fwd_hooks=None, bwd_hooks=None, after_bwd=None):
    nl = small["ln1_g"].shape[0]
    buckets = [_buckets(d).astype(jnp.int32) for d in DILATIONS]
    bias = [_bias_table(rel_bias, buckets[gi], gi, "bias_table%d" % gi) for gi in range(3)]
    no_hooks = lambda i: (lambda name: None)
    fwd_hooks = fwd_hooks or no_hooks
    bwd_hooks = bwd_hooks or no_hooks
    saved, ws, sms = [], [], []
    h = x
    for i in range(nl):
        w = _layer_weights(layer_full(i))
        sm = _layer_small(small, i)
        h, sv = _layer_fwd(h, w, sm, bias, fwd_hooks(i))
        saved.append(sv)
        ws.append(w)
        sms.append(sm)
    dh, dhb, dfinal, loss = _final_loss(h, target, final_g.reshape(1, D))
    gws, gss = [None] * nl, [None] * nl
    dbs = [jnp.zeros((6, WIN, 2 * WIN), F32)] * 3
    for i in reversed(range(nl)):
        dh, dhb, gws[i], gss[i], dbs = _layer_bwd(dh, dhb, ws[i], sms[i], bias, dbs, saved[i], bwd_hooks(i))
        if after_bwd is not None:
            after_bwd(i, gws[i])
    drel = []
    for gi in range(3):
        onehot = jnp.pad(jax.nn.one_hot(buckets[gi].reshape(-1), REL_BUCKETS, dtype=BF16), ((0, 0), (0, LANES - REL_BUCKETS)))
        drel.append(_mm(dbs[gi].reshape(6, WIN * 2 * WIN), onehot, name="g_relb"))
    return loss, dh, gws, gss, dfinal, jnp.concatenate(drel, axis=0)


WEIGHTS = ("rel_bias", "ln1_g", "w_in", "b_gate", "w_a", "pool_w", "pool_scale", "w_b", "ssd_conv_w", "ssd_conv_b",
           "ssd_dt_bias", "ssd_a_log", "ssd_d", "ssd_norm_w", "w_c", "w_o", "ln2_g", "ffn_w_up", "ffn_conv_w",
           "ffn_conv_b", "ffn_w_down", "final_g")
BIG_NAMES = tuple(n for n, _, _ in BIG)
SHARDED_SMALL = {"ssd_conv_w": XBC // 4, "ffn_conv_w": 2 * D_FF // 4}


def _to_rows(flat):
    n = flat.shape[0]
    rows = -(-n // LANES)
    rows = -(-rows // 8) * 8
    return jnp.pad(flat, (0, rows * LANES - n)).reshape(rows, LANES)


def _flatten(tree, names):
    return jnp.concatenate([tree[n].reshape(-1) for n in names])


def _unflatten(flat, shapes, names):
    out, o = {}, 0
    for n in names:
        k = math.prod(shapes[n])
        out[n] = flat[o:o + k].reshape(shapes[n])
        o += k
    return out


def kernel(x, rel_bias, ln1_g, w_in, b_gate, w_a, pool_w, pool_scale, w_b, ssd_conv_w, ssd_conv_b, ssd_dt_bias, ssd_a_log, ssd_d, ssd_norm_w, w_c, w_o, ln2_g, ffn_w_up, ffn_conv_w, ffn_conv_b, ffn_w_down, final_g, loss_target, m_rel_bias, m_ln1_g, m_w_in, m_b_gate, m_w_a, m_pool_w, m_pool_scale, m_w_b, m_ssd_conv_w, m_ssd_conv_b, m_ssd_dt_bias, m_ssd_a_log, m_ssd_d, m_ssd_norm_w, m_w_c, m_w_o, m_ln2_g, m_ffn_w_up, m_ffn_conv_w, m_ffn_conv_b, m_ffn_w_down, m_final_g, v_rel_bias, v_ln1_g, v_w_in, v_b_gate, v_w_a, v_pool_w, v_pool_scale, v_w_b, v_ssd_conv_w, v_ssd_conv_b, v_ssd_dt_bias, v_ssd_a_log, v_ssd_d, v_ssd_norm_w, v_w_c, v_w_o, v_ln2_g, v_ffn_w_up, v_ffn_conv_w, v_ffn_conv_b, v_ffn_w_down, v_final_g):
    W = dict(rel_bias=rel_bias, ln1_g=ln1_g, w_in=w_in, b_gate=b_gate, w_a=w_a, pool_w=pool_w, pool_scale=pool_scale,
             w_b=w_b, ssd_conv_w=ssd_conv_w, ssd_conv_b=ssd_conv_b, ssd_dt_bias=ssd_dt_bias, ssd_a_log=ssd_a_log,
             ssd_d=ssd_d, ssd_norm_w=ssd_norm_w, w_c=w_c, w_o=w_o, ln2_g=ln2_g, ffn_w_up=ffn_w_up,
             ffn_conv_w=ffn_conv_w, ffn_conv_b=ffn_conv_b, ffn_w_down=ffn_w_down, final_g=final_g)
    M = dict(rel_bias=m_rel_bias, ln1_g=m_ln1_g, w_in=m_w_in, b_gate=m_b_gate, w_a=m_w_a, pool_w=m_pool_w,
             pool_scale=m_pool_scale, w_b=m_w_b, ssd_conv_w=m_ssd_conv_w, ssd_conv_b=m_ssd_conv_b,
             ssd_dt_bias=m_ssd_dt_bias, ssd_a_log=m_ssd_a_log, ssd_d=m_ssd_d, ssd_norm_w=m_ssd_norm_w, w_c=m_w_c,
             w_o=m_w_o, ln2_g=m_ln2_g, ffn_w_up=m_ffn_w_up, ffn_conv_w=m_ffn_conv_w, ffn_conv_b=m_ffn_conv_b,
             ffn_w_down=m_ffn_w_down, final_g=m_final_g)
    V = dict(rel_bias=v_rel_bias, ln1_g=v_ln1_g, w_in=v_w_in, b_gate=v_b_gate, w_a=v_w_a, pool_w=v_pool_w,
             pool_scale=v_pool_scale, w_b=v_w_b, ssd_conv_w=v_ssd_conv_w, ssd_conv_b=v_ssd_conv_b,
             ssd_dt_bias=v_ssd_dt_bias, ssd_a_log=v_ssd_a_log, ssd_d=v_ssd_d, ssd_norm_w=v_ssd_norm_w, w_c=v_w_c,
             w_o=v_w_o, ln2_g=v_ln2_g, ffn_w_up=v_ffn_w_up, ffn_conv_w=v_ffn_conv_w, ffn_conv_b=v_ffn_conv_b,
             ffn_w_down=v_ffn_w_down, final_g=v_final_g)
    nl = ln1_g.shape[0]
    px, py, pc_ = _position()
    chip = 2 * px + py
    cidx = jnp.reshape(pc_, (1,)).astype(jnp.int32)
    chip_idx = jnp.reshape(chip, (1,)).astype(jnp.int32)

    placed = {}
    for n, cs in SHARDED_SMALL.items():
        full = jnp.zeros(W[n].shape[:-1] + (4 * cs,), F32)
        full = lax.dynamic_update_slice(full, W[n], (0, 0, chip * cs))
        placed[n] = jnp.where(pc_ == 0, full, 0.0)
    names_sh = tuple(SHARDED_SMALL)
    shapes_sh = {n: placed[n].shape for n in names_sh}
    got = _all_reduce_small(_to_rows(_flatten(placed, names_sh)), "gather_small")
    small = {n: W[n] for n in SMALL_LAYER}
    small.update(_unflatten(got.reshape(-1), shapes_sh, names_sh))

    packs = _pack_blocks({n: W[n] for n in BIG_NAMES}, BF16)

    half = PACK_PAD // 2
    units = half // 16

    def share(weights, total):
        tot = sum(weights.values())
        return {n: math.ceil(total * v / tot) for n, v in weights.items()}

    gathers = {}

    def gather(i):
        if i not in gathers:
            buf = lax.dynamic_update_slice(lax.empty((4, PACK_PAD, D), BF16), packs[i][None], (chip, 0, 0))
            gathers[i] = _Stream(packs[i], buf, functools.partial(_gather_parts, half), 6, units, "gather_w")
        return gathers[i]

    def layer_full(i):
        return _operands(gather(i).drain())

    fwd_share = share(dict(in_a=63, in_c=31, in_d=44, mm_up=83, mm_down=34), units)

    def fwd_hooks(i):
        if i + 1 >= nl:
            return lambda name: None
        return lambda name: gather(i + 1).hook(fwd_share[name]) if name in fwd_share else None

    exchanges = {}
    bwd_share = share(dict(g_down=35, d_u2_v=60, g_up_a=35, g_up_v=35, d_merged=50, d_u_a=60, d_u_wgate=70,
                           g_in_wgate=36), units)

    class Exchange:
        def __init__(self, g):
            self.g = g
            self.pair = _Stream(g, lax.empty((4, half, D), BF16), functools.partial(_rs_pair_parts, half), 1, units, "rs_pair")
            self.hsum = self.chips = None

        def to_chips(self):
            if self.chips is None:
                self.hsum = _rs_add_pair(self.g, self.pair.drain(), cidx, "rs_add_pair")
                self.chips = _Stream(self.hsum, lax.empty((3, half, D), BF16), _rs_chip_parts, 3, units, "rs_chips")
            return self.chips

    def after_bwd(i, gw):
        exchanges[i] = Exchange(_pack_operands(gw, BF16))

    def bwd_hooks(i):
        if i + 1 >= nl:
            return lambda name: None

        def hk(name):
            if name == "d_f":
                return exchanges[i + 1].pair.hook(units)
            return exchanges[i + 1].to_chips().hook(bwd_share[name]) if name in bwd_share else None

        return hk

    loss, dx, gws, gss, dfinal, drel = _local_step(x[0], loss_target[0], rel_bias, final_g, layer_full, small,
                                                   fwd_hooks, bwd_hooks, after_bwd)

    def reduced(i):
        recv3 = exchanges[i].to_chips().drain()
        r = _rs_add_chips(exchanges[i].hsum, recv3, chip_idx, "rs_add_chips")
        other = _rs_swap(r, "rs_swap")
        both = jnp.concatenate([jnp.where(pc_ == 0, r, other), jnp.where(pc_ == 0, other, r)], axis=0)
        return _unpack_blocks(both)

    red = [reduced(i) for i in range(nl)]
    delta, new_m, new_v, grads = {}, {}, {}, {}
    for n in BIG_NAMES:
        shp = W[n].shape
        r2 = lambda a: a.reshape(-1, shp[-1])
        grads[n] = jnp.stack([red[i][n] for i in range(nl)], axis=0)
        res = _adamw(r2(W[n]), r2(grads[n]), r2(M[n]), r2(V[n]), "adamw_" + n)
        delta[n], new_m[n], new_v[n] = [a.reshape(shp) for a in res]

    sg = {}
    for n in SMALL_LAYER:
        sg[n] = jnp.stack([gss[i][n] for i in range(nl)], axis=0)
    for n in ("ssd_dt_bias", "ssd_a_log", "ssd_d"):
        sg[n] = sg[n][:, 0, :SSD_HEADS]
    sg["rel_bias"] = drel[:, :REL_BUCKETS].T
    sg["final_g"] = dfinal.reshape(D)
    sg["loss"] = loss[0, :1]
    names_sg = tuple(sg)
    shapes_sg = {n: ((nl,) + W[n].shape[1:] if n in SMALL_LAYER and n not in SHARDED_SMALL else
                     (placed[n].shape if n in SHARDED_SMALL else sg[n].shape)) for n in names_sg}
    for n in names_sg:
        sg[n] = sg[n].reshape(shapes_sg[n])
    tot = _all_reduce_small(_to_rows(_flatten(sg, names_sg)), "allreduce_small")
    tot = _unflatten(tot.reshape(-1), shapes_sg, names_sg)
    loss_out = tot.pop("loss").reshape(())
    for n, cs in SHARDED_SMALL.items():
        tot[n] = lax.dynamic_slice(tot[n], (0, 0, chip * cs), tot[n].shape[:-1] + (cs,))
    grads.update(tot)

    names_s = tuple(n for n in WEIGHTS if n not in BIG_NAMES)
    shapes_s = {n: W[n].shape for n in names_s}
    pk = lambda t: _to_rows(_flatten(t, names_s))
    dl, m2, v2 = _adamw(pk(W), pk(grads), pk(M), pk(V), "adamw_small")
    delta.update(_unflatten(dl.reshape(-1), shapes_s, names_s))
    new_m.update(_unflatten(m2.reshape(-1), shapes_s, names_s))
    new_v.update(_unflatten(v2.reshape(-1), shapes_s, names_s))

    return (loss_out, dx[None], *[grads[n] for n in WEIGHTS], *[delta[n] for n in WEIGHTS],
            *[new_m[n] for n in WEIGHTS], *[new_v[n] for n in WEIGHTS])
```

```python
import functools
import math

import jax
import jax.numpy as jnp
from jax import lax
from jax.experimental import pallas as pl
from jax.experimental.pallas import tpu as pltpu

F32 = jnp.float32
BF16 = jnp.bfloat16
MESH = pl.DeviceIdType.MESH

D = 1024
HD = 64
GW = 384
AW = 3 * GW
WIN = 128
DILATIONS = (1, 4, 16)
REL_BUCKETS = 32
REL_MAX_DISTANCE = 2048
POOL_WINDOWS = (2, 4, 8, 16)
PG = 256
SSD_HEADS = 16
SSD_N = 128
SSD_CHUNK = 128
XBC = 1536
D_FF = 2816
EPS = 1e-6
NEG = -1e30
HALO = 16
LANES = 128

SEC_A = 3 * AW
SEC_B = D
SEC_C = D + XBC
SEC_D = 3328
SEC_A_PAD = 3584
IN_WIDTH = SEC_A + SEC_B + SEC_C + 16 + 3 * D

ADAM_LR = 0.001
ADAM_B1 = 0.9
ADAM_B2 = 0.999
ADAM_EPS = 1e-08
ADAM_WD = 0.01
ADAM_STEP = 10
ADAM_TILE = 256 * 1024
MM_VMEM_BYTES = 40 * 1024 * 1024
MM_MAX_OUT_TILE = 1024 * 1024
HBM_BYTES_PER_US = 2.0e6
STEP_US = 0.35
MXU_WIDTH = 256
MXU_FLOPS_PER_US = 0.65e6


_ANY = pl.BlockSpec(memory_space=pl.ANY)


def _pick(d, cands):
    for t in cands:
        if d % t == 0:
            return t
    return d


def _iota(shape, dim):
    return lax.broadcasted_iota(jnp.int32, shape, dim)


def _dg(a, b, ca, cb):
    return lax.dot_general(a.astype(BF16), b.astype(BF16), (((ca,), (cb,)), ((), ())),
                           preferred_element_type=F32)


@jax.custom_vjp
def _bdot_nn(a, b):
    return _dg(a, b, 1, 0)


def _nn_fwd(a, b):
    return _dg(a, b, 1, 0), (a, b)


def _nn_bwd(res, g):
    a, b = res
    return _dg(g, b, 1, 1), _dg(a, g, 0, 0)


_bdot_nn.defvjp(_nn_fwd, _nn_bwd)


@jax.custom_vjp
def _bdot_nt(a, b):
    return _dg(a, b, 1, 1)


def _nt_fwd(a, b):
    return _dg(a, b, 1, 1), (a, b)


def _nt_bwd(res, g):
    a, b = res
    return _dg(g, b, 1, 0), _dg(g, a, 0, 0)


_bdot_nt.defvjp(_nt_fwd, _nt_bwd)


@jax.custom_vjp
def _bdot_tn(a, b):
    return _dg(a, b, 0, 0)


def _tn_fwd(a, b):
    return _dg(a, b, 0, 0), (a, b)


def _tn_bwd(res, g):
    a, b = res
    return _dg(b, g, 1, 1), _dg(a, g, 1, 0)


_bdot_tn.defvjp(_tn_fwd, _tn_bwd)


def _fdot(a, b):
    return jnp.dot(a, b, preferred_element_type=F32, precision=lax.Precision.HIGHEST)


def _sigmoid(x):
    return 0.5 * jnp.tanh(0.5 * x) + 0.5


def _silu(x):
    return x * _sigmoid(x)


def _softplus(x):
    return jnp.maximum(x, 0.0) + jnp.log(1.0 + jnp.exp(-jnp.abs(x)))


def _lane_pick(m, h):
    return jnp.sum(jnp.where(_iota(m.shape, 1) == h, m, 0.0), axis=1, keepdims=True)


def _row_pick(m, h):
    return jnp.sum(jnp.where(_iota(m.shape, 0) == h, m, 0.0), axis=0, keepdims=True)


def _stack_rows(rows, n):
    c = rows[0].shape[1]
    r = _iota((n, c), 0)
    out = jnp.zeros((n, c), F32)
    for k, v in enumerate(rows):
        out = out + jnp.where(r == k, v, 0.0)
    return out


def _mm(a, b, *, ta=False, tb=False, add=None, out_dtype=F32, name, hook=None):
    if ta:
        K, M = a.shape
    else:
        M, K = a.shape
    if tb:
        N, Kb = b.shape
    else:
        Kb, N = b.shape
    assert K == Kb, (a.shape, b.shape, ta, tb)
    tm, tn, tk = _mm_tiles(M, N, K, a.dtype.itemsize, b.dtype.itemsize, jnp.dtype(out_dtype).itemsize,
                           0 if add is None else add.dtype.itemsize)
    ni, nj, nk = M // tm, N // tn, K // tk
    ca = 0 if ta else 1
    cb = 1 if tb else 0
    n_in = 2 if add is None else 3
    n_hin = 0 if hook is None else len(hook.inputs)
    n_hout = 0 if hook is None else len(hook.out_shapes)

    def body(*refs):
        a_ref, b_ref = refs[:2]
        add_ref = None if add is None else refs[2]
        o_ref = refs[n_in + n_hin]
        scr = refs[n_in + n_hin + 1 + n_hout:]
        acc_ref = scr[0] if nk > 1 else None
        hargs = (refs[n_in:n_in + n_hin], refs[n_in + n_hin + 1:n_in + n_hin + 1 + n_hout], scr[1 if nk > 1 else 0:])
        i, j, k = pl.program_id(0), pl.program_id(1), pl.program_id(2)
        if hook is not None:
            @pl.when((i == 0) & (j == 0) & (k == 0))
            def _():
                hook.start(*hargs)

        part = _dg(a_ref[...], b_ref[...], ca, cb)

        def finish(r):
            if add_ref is not None:
                r = r + add_ref[...].astype(F32)
            o_ref[...] = r.astype(o_ref.dtype)

        if nk == 1:
            finish(part)
        else:
            @pl.when(k == 0)
            def _():
                acc_ref[...] = part

            @pl.when((k > 0) & (k < nk - 1))
            def _():
                acc_ref[...] += part

            @pl.when(k == nk - 1)
            def _():
                finish(acc_ref[...] + part)

        if hook is not None:
            @pl.when((i == ni - 1) & (j == nj - 1) & (k == nk - 1))
            def _():
                hook.finish(*hargs)

    a_spec = pl.BlockSpec((tk, tm), lambda i, j, k: (k, i)) if ta else pl.BlockSpec((tm, tk), lambda i, j, k: (i, k))
    b_spec = pl.BlockSpec((tn, tk), lambda i, j, k: (j, k)) if tb else pl.BlockSpec((tk, tn), lambda i, j, k: (k, j))
    in_specs = [a_spec, b_spec]
    args = [a, b]
    if add is not None:
        in_specs.append(pl.BlockSpec((tm, tn), lambda i, j, k: (i, j)))
        args.append(add)
    out_specs = [pl.BlockSpec((tm, tn), lambda i, j, k: (i, j))]
    out_shape = [jax.ShapeDtypeStruct((M, N), out_dtype)]
    scratch = [pltpu.VMEM((tm, tn), F32)] if nk > 1 else []
    aliases = {}
    if hook is not None:
        in_specs += [_ANY] * n_hin
        args += list(hook.inputs)
        out_specs += [_ANY] * n_hout
        out_shape += list(hook.out_shapes)
        scratch += list(hook.scratch)
        aliases = {n_in + hi: 1 + ho for hi, ho in hook.aliases.items()}
    sem = ("parallel", "parallel", "arbitrary") if hook is None else ("arbitrary",) * 3
    res = pl.pallas_call(
        body, name=name, grid=(ni, nj, nk), in_specs=in_specs, out_specs=out_specs, out_shape=out_shape,
        scratch_shapes=scratch, input_output_aliases=aliases,
        compiler_params=pltpu.CompilerParams(dimension_semantics=sem),
    )(*args)
    if hook is not None:
        hook.done(res[1:])
    return res[0]


def _wide(v):
    return v.astype(F32) if v.dtype == BF16 else v


def _mmf(a, b, *, tb=False, add=None, pre=None, post=None, out_dtype=F32, name, tm, hook=None):
    a_list = list(a) if isinstance(a, (list, tuple)) else [a]
    b_list = list(b) if isinstance(b, (list, tuple)) else [b]
    assert len(a_list) == len(b_list) and (len(b_list) == 1 or not (tb or pre))
    b = b_list[0]
    if tb:
        N, K = b.shape
    else:
        K, N = b.shape
    M = pre[1][0].shape[0] if pre else a_list[0].shape[0]
    tn = N if post or N <= 1024 else _pick(N, (512, 256, LANES))
    ni, nj = M // tm, N // tn
    cb = 1 if tb else 0
    pre_fn, pre_rows, pre_consts = pre if pre else (None, [], [])
    post_fn, post_rows, post_consts, post_outs, post_accs = post if post else (None, [], [], [], [])
    hook_in = [] if hook is None else list(hook.inputs)
    hook_out = [] if hook is None else list(hook.out_shapes)

    def row_spec(arr):
        return pl.BlockSpec((tm, arr.shape[1]), lambda i, j: (i, 0))

    def const_spec(arr):
        return pl.BlockSpec(arr.shape, lambda i, j, nd=arr.ndim: (0,) * nd)

    args, in_specs = [], []
    for arr in (a_list if not pre else pre_rows):
        args.append(arr)
        in_specs.append(row_spec(arr))
    for arr in pre_consts:
        args.append(arr)
        in_specs.append(const_spec(arr))
    for arr in b_list:
        args.append(arr)
        in_specs.append(pl.BlockSpec((tn, K), lambda i, j: (j, 0)) if tb else
                        pl.BlockSpec((arr.shape[0], tn), lambda i, j: (0, j)))
    if add is not None:
        args.append(add)
        in_specs.append(pl.BlockSpec((tm, tn), lambda i, j: (i, j)))
    for arr in post_rows:
        args.append(arr)
        in_specs.append(row_spec(arr))
    for arr in post_consts:
        args.append(arr)
        in_specs.append(const_spec(arr))
    n_main = len(args)
    args += hook_in
    in_specs += [_ANY] * len(hook_in)

    out_shape, out_specs = [], []
    if post:
        for c, dt in post_outs:
            out_shape.append(jax.ShapeDtypeStruct((M, c), dt))
            out_specs.append(pl.BlockSpec((tm, c), lambda i, j: (i, 0)))
        for r, c in post_accs:
            out_shape.append(jax.ShapeDtypeStruct((r, c), F32))
            out_specs.append(pl.BlockSpec((r, c), lambda i, j: (0, 0)))
    else:
        out_shape.append(jax.ShapeDtypeStruct((M, N), out_dtype))
        out_specs.append(pl.BlockSpec((tm, tn), lambda i, j: (i, j)))
    if pre:
        out_shape.append(jax.ShapeDtypeStruct((M, K), BF16))
        out_specs.append(pl.BlockSpec((tm, K), lambda i, j: (i, 0)))
    n_out = len(out_shape)
    out_shape += hook_out
    out_specs += [_ANY] * len(hook_out)
    scratch = ([pltpu.VMEM((tm, K), BF16)] if pre else []) + ([] if hook is None else list(hook.scratch))
    aliases = {} if hook is None else {n_main + hi: n_out + ho for hi, ho in hook.aliases.items()}

    def body(*refs):
        ins, outs, scr = refs[:n_main], refs[len(args):len(args) + n_out], refs[len(args) + len(out_shape):]
        hargs = (refs[n_main:len(args)], refs[len(args) + n_out:len(args) + len(out_shape)], scr[1 if pre else 0:])
        i, j = pl.program_id(0), pl.program_id(1)
        if hook is not None:
            @pl.when((i == 0) & (j == 0))
            def _():
                hook.start(*hargs)

        it = iter(ins)
        if pre:
            rows_ = [next(it) for _ in pre_rows]
            consts_ = [next(it) for _ in pre_consts]

            @pl.when(j == 0)
            def _():
                av = pre_fn(*[_wide(r[...]) for r in rows_], *[_wide(r[...]) for r in consts_]).astype(BF16)
                scr[0][...] = av
                outs[-1][...] = av

            ats = [scr[0][...]]
        else:
            ats = [next(it)[...] for _ in a_list]
        p = None
        for at in ats:
            part = _dg(at, next(it)[...], 1, cb)
            p = part if p is None else p + part
        if add is not None:
            p = p + next(it)[...].astype(F32)
        if post:
            rows_ = [next(it) for _ in post_rows]
            consts_ = [next(it) for _ in post_consts]
            res = post_fn(p, *[_wide(r[...]) for r in rows_], *[_wide(r[...]) for r in consts_])
            for r, v in zip(outs[:len(post_outs)], res[:len(post_outs)]):
                r[...] = v.astype(r.dtype)
            for r, v in zip(outs[len(post_outs):], res[len(post_outs):]):
                @pl.when(i == 0)
                def _(r=r, v=v):
                    r[...] = v

                @pl.when(i > 0)
                def _(r=r, v=v):
                    r[...] += v
        else:
            outs[0][...] = p.astype(outs[0].dtype)
        if hook is not None:
            @pl.when((i == ni - 1) & (j == nj - 1))
            def _():
                hook.finish(*hargs)

    res = pl.pallas_call(
        body, name=name, grid=(ni, nj), in_specs=in_specs, out_specs=out_specs, out_shape=out_shape,
        scratch_shapes=scratch, input_output_aliases=aliases,
        compiler_params=pltpu.CompilerParams(dimension_semantics=("arbitrary", "arbitrary")),
    )(*args)
    if hook is not None:
        hook.done(res[n_out:])
    return res[:n_out]


def _mm_tiles(M, N, K, sa, sb, so, sadd):
    def tiles(d):
        return [t for t in range(LANES, min(d, 2048) + 1, LANES) if d % t == 0] or [d]

    best = None
    for tk in [K] + [t for t in tiles(K) if t < K]:
        for tm in tiles(M):
            for tn in tiles(N):
                vmem = 2 * (tm * tk * sa + tk * tn * sb + tm * tn * (so + sadd)) + (tm * tn * 4 if tk < K else 0)
                if vmem > MM_VMEM_BYTES or tm * tn > MM_MAX_OUT_TILE:
                    continue
                a_reads = 1 if tk == K else N // tn
                traffic = M * K * sa * a_reads + K * N * sb * (M // tm) + M * N * (so + sadd)
                steps = (M // tm) * (N // tn) * (K // tk)
                width = -(-tn // MXU_WIDTH) * MXU_WIDTH
                mxu = 2.0 * M * K * N * (width / tn) / MXU_FLOPS_PER_US
                edge = tm * tk * sa + tk * tn * sb + tm * tn * (so + sadd)
                cost = max(traffic / HBM_BYTES_PER_US, mxu) + steps * STEP_US + edge / HBM_BYTES_PER_US
                if best is None or cost < best[0]:
                    best = (cost, tm, tn, tk)
    assert best is not None, (M, N, K)
    return best[1:]


class _Hook:
    def __init__(self, inputs, out_shapes, aliases, scratch, start, finish, done):
        self.inputs, self.out_shapes, self.aliases, self.scratch = inputs, out_shapes, aliases, scratch
        self.start, self.finish, self.done = start, finish, done


class _Ctx:
    def __init__(self, first, last, row0, rows):
        self.first, self.last, self.row0, self.rows = first, last, row0, rows


def _rows(name, fn, ins, outs, accs=(), *, tm, nrows, ncol=1):
    nt = nrows // tm
    hb = tm // HALO
    nh = nrows // HALO
    ins = [(kind, arr, arr.shape[1] if kind == "row" and cw is None else cw, base) for kind, arr, cw, base in ins]
    in_specs, args = [], []
    for kind, arr, cw, base in ins:
        if kind == "row":
            in_specs.append(pl.BlockSpec((tm, cw), lambda j, i, base=base: (i, base + j)))
        elif kind == "prev":
            in_specs.append(pl.BlockSpec((HALO, cw), lambda j, i, base=base: (jnp.maximum(i * hb - 1, 0), base + j)))
        elif kind == "next":
            in_specs.append(pl.BlockSpec((HALO, cw), lambda j, i, base=base: (jnp.minimum((i + 1) * hb, nh - 1), base + j)))
        elif kind in ("const", "raw"):
            in_specs.append(pl.BlockSpec(arr.shape, lambda j, i, nd=arr.ndim: (0,) * nd))
        elif kind == "ccol":
            in_specs.append(pl.BlockSpec((arr.shape[0], cw), lambda j, i, base=base: (0, base + j)))
        else:
            raise ValueError(kind)
        args.append(arr)
    out_specs, out_shape = [], []
    for ctot, cw, base, dt in outs:
        out_specs.append(pl.BlockSpec((tm, cw), lambda j, i, base=base: (i, base + j)))
        out_shape.append(jax.ShapeDtypeStruct((nrows, ctot), dt))
    for r, ctot, cw in accs:
        out_specs.append(pl.BlockSpec((r, cw), lambda j, i: (0, j)))
        out_shape.append(jax.ShapeDtypeStruct((r, ctot), F32))
    n_in, n_out = len(ins), len(outs)

    def body(*refs):
        i = pl.program_id(1)
        in_refs, out_refs, acc_refs = refs[:n_in], refs[n_in:n_in + n_out], refs[n_in + n_out:]
        if acc_refs:
            @pl.when(i == 0)
            def _():
                for r in acc_refs:
                    r[...] = jnp.zeros_like(r)

        vals = [r[...] if s[0] == "raw" else _wide(r[...]) for r, s in zip(in_refs, ins)]
        res = fn(_Ctx(i == 0, i == nt - 1, i * tm, tm), *vals)
        for r, v in zip(out_refs, res[:n_out]):
            r[...] = v.astype(r.dtype)
        for r, v in zip(acc_refs, res[n_out:]):
            r[...] += v

    res = pl.pallas_call(
        body, name=name, grid=(ncol, nt), in_specs=in_specs, out_specs=out_specs, out_shape=out_shape,
        compiler_params=pltpu.CompilerParams(dimension_semantics=("arbitrary", "arbitrary")),
    )(*args)
    return res


def _shift_down(xcat, k):
    return xcat if k == 0 else pltpu.roll(xcat, k, 0)


def _shift_up(xcat, k):
    return xcat if k == 0 else pltpu.roll(xcat, xcat.shape[0] - k, 0)


def _with_prev(ctx, halo, x):
    return jnp.concatenate([jnp.where(ctx.first, 0.0, halo), x], axis=0)


def _with_next(ctx, x, halo):
    return jnp.concatenate([x, jnp.where(ctx.last, 0.0, halo)], axis=0)


def _rms_core(x, g):
    r = lax.rsqrt(jnp.mean(x * x, axis=-1, keepdims=True) + EPS)
    return x * r * g


def _rms_post(du, xv, drv, gv):
    _, vjp = jax.vjp(_rms_core, xv, gv)
    dx, dg = vjp(du)
    return [drv + dx, drv + dx, dg]


RMS_POST_OUTS = [(D, F32), (D, BF16)]


def _final_loss(x, target, g):
    S = x.shape[0]

    def fn(ctx, xv, tv, gv):
        def f(xx, gg):
            err = _rms_core(xx, gg) - tv
            return 0.5 * jnp.sum(err * err) / D

        loss, vjp = jax.vjp(f, xv, gv)
        dx, dg = vjp(jnp.ones((), F32))
        return [dx, dx, dg, jnp.zeros((1, LANES), F32) + loss]

    return _rows("final_loss", fn, [("row", x, None, 0), ("row", target, None, 0), ("const", g, None, 0)],
                 [(D, D, 0, F32), (D, D, 0, BF16)], [(1, D, D), (1, LANES, LANES)], tm=256, nrows=S)


def _attn_valid(n):
    qi = _iota((WIN, 2 * WIN), 0)
    kk = _iota((WIN, 2 * WIN), 1)
    rel = qi + WIN - kk
    return (rel >= 0) & (rel <= WIN) & ((kk >= WIN) | (n > 0))


def _attn_block(q, kp, kc, vp, vc, b0, b1):
    k = jnp.concatenate([kp, kc], axis=0)
    v = jnp.concatenate([vp, vc], axis=0)
    lo = _iota((WIN, LANES), 1) < HD
    scale = 1.0 / math.sqrt(HD)
    os_, ls_ = [], []
    for hh, b in ((0, b0), (1, b1)):
        qm = jnp.where(lo if hh == 0 else ~lo, q, 0.0)
        s = _bdot_nt(qm, k) * scale + b
        m = lax.stop_gradient(jnp.max(s, axis=1, keepdims=True))
        p = jnp.exp(s - m)
        l = jnp.sum(p, axis=1, keepdims=True)
        os_.append(_bdot_nn(p, v) / l)
        ls_.append(m + jnp.log(l))
    return jnp.where(lo, os_[0], os_[1]), jnp.where(lo, ls_[0], ls_[1])


def _residue_rows(r, d):
    return pl.ds(0, WIN) if d == 1 else pl.ds(r, WIN, stride=d)


def _for_residues(d, fn):
    if d == 1:
        fn(0, 0)
    else:
        lax.fori_loop(0, d, fn, 0, unroll=min(d, 8))


def _pairs_per_step(d):
    return 3 if d == 1 else 1


def _bias_table(rel_bias, bucket, gi, name):
    def body(t_ref, b_ref, o_ref):
        h = 6 * gi + pl.program_id(0)
        b = b_ref[...]
        acc = jnp.zeros(b.shape, F32)
        for k in range(REL_BUCKETS):
            acc = jnp.where(b == k, t_ref[k, h], acc)
        o_ref[0] = acc

    return pl.pallas_call(
        body, name=name, grid=(6,),
        in_specs=[pl.BlockSpec(memory_space=pltpu.SMEM), pl.BlockSpec((WIN, 2 * WIN), lambda h: (0, 0))],
        out_specs=pl.BlockSpec((1, WIN, 2 * WIN), lambda h: (h, 0, 0)),
        out_shape=jax.ShapeDtypeStruct((6, WIN, 2 * WIN), F32),
    )(rel_bias, bucket)


def _attn_fwd(pa, bias, gi, name):
    S = pa.shape[0]
    d = DILATIONS[gi]
    bt = WIN * d
    nb = S // bt
    hpw = _pairs_per_step(d)
    bw = hpw * LANES
    cb = 3 * gi // hpw

    def body(q_ref, kp_ref, kc_ref, vp_ref, vc_ref, b_ref, o_ref, l_ref):
        valid = _attn_valid(pl.program_id(1))
        bm = [jnp.where(valid, b_ref[k], NEG) for k in range(2 * hpw)]

        def residue(r, carry):
            sl = _residue_rows(r, d)
            for t in range(hpw):
                ln = pl.ds(t * LANES, LANES)
                o, lse = _attn_block(q_ref[sl, ln], kp_ref[sl, ln], kc_ref[sl, ln], vp_ref[sl, ln], vc_ref[sl, ln],
                                     bm[2 * t], bm[2 * t + 1])
                o_ref[sl, ln] = o
                l_ref[sl, ln] = lse
            return carry

        _for_residues(d, residue)

    def spec(off, prev):
        if prev:
            return pl.BlockSpec((bt, bw), lambda hp, n: (jnp.maximum(n - 1, 0), off // hpw + cb + hp))
        return pl.BlockSpec((bt, bw), lambda hp, n: (n, off // hpw + cb + hp))

    ospec = pl.BlockSpec((bt, bw), lambda hp, n: (n, hp))
    return pl.pallas_call(
        body, name=name, grid=(3 // hpw, nb),
        in_specs=[spec(0, False), spec(9, True), spec(9, False), spec(18, True), spec(18, False),
                  pl.BlockSpec((2 * hpw, WIN, 2 * WIN), lambda hp, n: (hp, 0, 0))],
        out_specs=[ospec, ospec],
        out_shape=[jax.ShapeDtypeStruct((S, GW), F32)] * 2,
        compiler_params=pltpu.CompilerParams(dimension_semantics=("parallel", "arbitrary")),
    )(pa, pa, pa, pa, pa, bias)


def _attn_bwd(pa, bias, do, dlse, db_in, dqkv, gi, name):
    S = pa.shape[0]
    d = DILATIONS[gi]
    bt = WIN * d
    nb = S // bt
    hpw = _pairs_per_step(d)
    bw = hpw * LANES
    cb = 3 * gi // hpw

    def body(q_ref, kp_ref, kc_ref, vp_ref, vc_ref, b_ref, do_ref, dl_ref, dbi_ref, dqi_ref, dki_ref, dvi_ref,
             dq_ref, dk_ref, dv_ref, db_ref, ck, cv):
        n = pl.program_id(1)

        @pl.when(n == 0)
        def _():
            db_ref[...] = dbi_ref[...]
            ck[...] = jnp.zeros_like(ck)
            cv[...] = jnp.zeros_like(cv)

        @pl.when(n < nb)
        def _():
            valid = _attn_valid(n)
            bm = [jnp.where(valid, b_ref[k], NEG) for k in range(2 * hpw)]

            def residue(r, carry):
                sl = _residue_rows(r, d)
                cs = pl.ds(pl.multiple_of(r * WIN, WIN), WIN)
                for t in range(hpw):
                    ln = pl.ds(t * LANES, LANES)
                    _, vjp = jax.vjp(_attn_block, q_ref[sl, ln], kp_ref[sl, ln], kc_ref[sl, ln], vp_ref[sl, ln],
                                     vc_ref[sl, ln], bm[2 * t], bm[2 * t + 1])
                    dq, dkp, dkc, dvp, dvc, db0, db1 = vjp((do_ref[sl, ln], dl_ref[sl, ln]))
                    dq_ref[sl, ln] = dq
                    dk_ref[sl, ln] = ck[cs, ln] + dkp
                    dv_ref[sl, ln] = cv[cs, ln] + dvp
                    ck[cs, ln] = dkc
                    cv[cs, ln] = dvc
                    db_ref[2 * t] += db0
                    db_ref[2 * t + 1] += db1
                return carry

            _for_residues(d, residue)

        @pl.when(n == nb)
        def _():
            def residue(r, carry):
                sl = _residue_rows(r, d)
                cs = pl.ds(pl.multiple_of(r * WIN, WIN), WIN)
                dk_ref[sl, :] = ck[cs, :]
                dv_ref[sl, :] = cv[cs, :]
                return carry

            _for_residues(d, residue)

    def cur(n):
        return jnp.minimum(n, nb - 1)

    def spec(off, prev):
        if prev:
            return pl.BlockSpec((bt, bw), lambda hp, n: (jnp.maximum(cur(n) - 1, 0), off // hpw + cb + hp))
        return pl.BlockSpec((bt, bw), lambda hp, n: (cur(n), off // hpw + cb + hp))

    gspec = pl.BlockSpec((bt, bw), lambda hp, n: (cur(n), hp))
    bspec = pl.BlockSpec((2 * hpw, WIN, 2 * WIN), lambda hp, n: (hp, 0, 0))
    qspec = pl.BlockSpec((bt, bw), lambda hp, n: (cur(n), cb + hp))
    kspec = pl.BlockSpec((bt, bw), lambda hp, n: (jnp.maximum(n - 1, 0), cb + hp))
    dq, dk, dv, db = pl.pallas_call(
        body, name=name, grid=(3 // hpw, nb + 1),
        in_specs=[spec(0, False), spec(9, True), spec(9, False), spec(18, True), spec(18, False),
                  bspec, gspec, gspec, bspec, _ANY, _ANY, _ANY],
        out_specs=[qspec, kspec, kspec, bspec],
        out_shape=[jax.ShapeDtypeStruct((S, AW), F32)] * 3 + [jax.ShapeDtypeStruct((6, WIN, 2 * WIN), F32)],
        scratch_shapes=[pltpu.VMEM((bt, bw), F32), pltpu.VMEM((bt, bw), F32)],
        input_output_aliases={9: 0, 10: 1, 11: 2},
        compiler_params=pltpu.CompilerParams(dimension_semantics=("arbitrary", "arbitrary")),
    )(pa, pa, pa, pa, pa, bias, do, dlse, db_in, *dqkv)
    return (dq, dk, dv), db


def _mix_core(o0, o1, o2, l0, l1, l2):
    m = lax.stop_gradient(jnp.maximum(jnp.maximum(l0, l1), l2))
    e0, e1, e2 = jnp.exp(l0 - m), jnp.exp(l1 - m), jnp.exp(l2 - m)
    return (e0 * o0 + e1 * o1 + e2 * o2) / (e0 + e1 + e2)


def _mix_fwd(os_, ls_, name):
    S = os_[0].shape[0]
    ins = [("row", a, None, 0) for a in (*os_, *ls_)]
    return _rows(name, lambda ctx, *v: [_mix_core(*v)], ins, [(GW, GW, 0, BF16)], tm=256, nrows=S)[0]


def _mix_bwd(os_, ls_, datt, name):
    S = datt.shape[0]

    def fn(ctx, *v):
        _, vjp = jax.vjp(_mix_core, *v[:6])
        return list(vjp(v[6]))

    ins = [("row", a, None, 0) for a in (*os_, *ls_, datt)]
    outs = [(GW, GW, 0, F32)] * 6
    r = _rows(name, fn, ins, outs, tm=256, nrows=S)
    return r[:3], r[3:]


def _t5_bucket(dist):
    max_exact = REL_BUCKETS // 2
    is_small = dist < max_exact
    nf = jnp.maximum(dist, 1).astype(F32)
    large = max_exact + (jnp.log(nf / max_exact) / math.log(REL_MAX_DISTANCE / max_exact)
                         * (REL_BUCKETS - max_exact)).astype(jnp.int32)
    large = jnp.minimum(large, REL_BUCKETS - 1)
    return jnp.where(is_small, dist, large)


def _buckets(d):
    qi = jnp.arange(WIN)[:, None]
    kk = jnp.arange(2 * WIN)[None, :]
    rel = qi + WIN - kk
    return _t5_bucket(jnp.clip(rel, 0, None) * d)


def _pool_cnt(ctx, w):
    pos = ctx.row0 + _iota((ctx.rows, PG), 0) + 1
    return jnp.minimum(pos, w).astype(F32)


def _pool_d(ctx, halo, u):
    ds = []
    for g, w in enumerate(POOL_WINDOWS):
        ug = u[:, g * PG:(g + 1) * PG]
        s = _with_prev(ctx, halo[:, g * PG:(g + 1) * PG], ug)
        step = 1
        while step < w:
            s = s + _shift_down(s, step)
            step *= 2
        ds.append(s[HALO:] / _pool_cnt(ctx, w) - ug)
    return ds


def _pool_fwd(pb, pw, scale, name):
    S = pb.shape[0]

    def fn(ctx, halo, u, w, sc):
        ds = _pool_d(ctx, halo, u)
        return [jnp.concatenate([_dg(ds[k], w[k], 1, 0) for k in range(4)], axis=1) * sc]

    return _rows(name, fn, [("prev", pb, D, 0), ("row", pb, None, 0), ("raw", pw, None, 0), ("const", scale, None, 0)],
                 [(D, D, 0, BF16)], tm=256, nrows=S)[0]


def _pool_bwd(pb, pw, scale, dpo, name):
    S = pb.shape[0]

    def fn1(ctx, halo, u, w, sc, dy):
        ds = _pool_d(ctx, halo, u)
        dyp = dy * sc
        y = jnp.concatenate([_dg(ds[k], w[k], 1, 0) for k in range(4)], axis=1)
        es, dws = [], []
        for k, wd in enumerate(POOL_WINDOWS):
            cols = slice(k * PG, (k + 1) * PG)
            es.append(_dg(dyp[:, cols], w[k], 1, 1) / _pool_cnt(ctx, wd))
            dws.append(_dg(ds[k], dyp[:, cols], 0, 0))
        return [jnp.concatenate(es, axis=1), jnp.concatenate(dws, axis=0), jnp.sum(dy * y, axis=0, keepdims=True)]

    e, dpw, dsc = _rows(name + "_a", fn1,
                        [("prev", pb, D, 0), ("row", pb, None, 0), ("raw", pw, None, 0), ("const", scale, None, 0),
                         ("row", dpo, None, 0)],
                        [(D, D, 0, F32)], [(4 * PG, PG, PG), (1, D, D)], tm=256, nrows=S)

    def fn2(ctx, ev, halo):
        outs = []
        for g, w in enumerate(POOL_WINDOWS):
            eg = ev[:, g * PG:(g + 1) * PG]
            s = _with_next(ctx, eg, halo[:, g * PG:(g + 1) * PG])
            step = 1
            while step < w:
                s = s + _shift_up(s, step)
                step *= 2
            outs.append(s[:ctx.rows] - eg * _pool_cnt(ctx, w))
        return [jnp.concatenate(outs, axis=1)]

    du = _rows(name + "_b", fn2, [("row", e, None, 0), ("next", e, D, 0)], [(D, D, 0, BF16)], tm=256, nrows=S)[0]
    return du, dpw, dsc


def _conv_taps(ctx, halo, x, K):
    cat = _with_prev(ctx, halo, x)
    return [_shift_down(cat, K - 1 - k)[HALO:] for k in range(K)]


def _conv_pre(taps, w, b):
    acc = b
    for k, t in enumerate(taps):
        acc = acc + t * _row_pick(w, k)
    return acc


CW = 256
CWS = 512
CONV_TM = 512
CONV_FWD_TM = 1024


def _ext_taps(ctx, prev, x, nxt, K):
    cat = jnp.concatenate([jnp.where(ctx.first, 0.0, prev), x, jnp.where(ctx.last, 0.0, nxt)], axis=0)
    return [_shift_down(cat, K - 1 - k)[HALO:] for k in range(K)]


def _conv_t_rows(dp, w, K, tm):
    acc = jnp.zeros((tm, dp.shape[1]), F32)
    for k in range(K):
        acc = acc + _shift_up(dp, K - 1 - k)[:tm] * _row_pick(w, k)
    return acc


def _ssd_conv_fwd(pc, w, b, name):
    S = pc.shape[0]
    base = D // CWS

    def fn(ctx, halo, x, wv, bv):
        return [_silu(_conv_pre(_conv_taps(ctx, halo, x, 4), wv, bv))]

    return _rows(name, fn, [("prev", pc, CWS, base), ("row", pc, CWS, base), ("ccol", w, CWS, 0), ("ccol", b, CWS, 0)],
                 [(XBC, CWS, 0, F32)], tm=CONV_FWD_TM, nrows=S, ncol=XBC // CWS)[0]


def _ssd_conv_bwd(pc, w, b, dy, name):
    S = pc.shape[0]
    base = D // CWS

    def fn(ctx, prev, x, nxt, wv, bv, dyv, dyn):
        n = ctx.rows
        taps = _ext_taps(ctx, prev, x, nxt, 4)
        pre = _conv_pre(taps, wv, bv)
        sg = _sigmoid(pre)
        dye = jnp.concatenate([dyv, jnp.where(ctx.last, 0.0, dyn)], axis=0)
        dpre = dye * sg * (1.0 + pre * (1.0 - sg))
        dw = _stack_rows([jnp.sum(dpre[:n] * t[:n], axis=0, keepdims=True) for t in taps], 4)
        return [_conv_t_rows(dpre, wv, 4, n), dw, jnp.sum(dpre[:n], axis=0, keepdims=True)]

    return _rows(name, fn,
                 [("prev", pc, CWS, base), ("row", pc, CWS, base), ("next", pc, CWS, base), ("ccol", w, CWS, 0),
                  ("ccol", b, CWS, 0), ("row", dy, CWS, 0), ("next", dy, CWS, 0)],
                 [(XBC, CWS, 0, BF16)], [(4, XBC, CWS), (1, XBC, CWS)], tm=CONV_TM, nrows=S, ncol=XBC // CWS)


NFC = D_FF // CW


def _ffn_act_fwd(h, w, b, name):
    S = h.shape[0]

    def fn(ctx, ha, a, hv, v, wa, wv, ba, bv):
        pa = _conv_pre(_conv_taps(ctx, ha, a, 3), wa, ba)
        pv = _conv_pre(_conv_taps(ctx, hv, v, 3), wv, bv)
        return [_silu(pa) * pv]

    return _rows(name, fn,
                 [("prev", h, CW, 0), ("row", h, CW, 0), ("prev", h, CW, NFC), ("row", h, CW, NFC),
                  ("ccol", w, CW, 0), ("ccol", w, CW, NFC), ("ccol", b, CW, 0), ("ccol", b, CW, NFC)],
                 [(D_FF, CW, 0, BF16)], tm=CONV_FWD_TM, nrows=S, ncol=NFC)[0]


def _ffn_act_bwd(h, w, b, df, name):
    S = h.shape[0]

    def fn(ctx, pa_, a, na, pv_, v, nv, wa, wv, ba, bv, dfv, dfn):
        n = ctx.rows
        ta = _ext_taps(ctx, pa_, a, na, 3)
        tv = _ext_taps(ctx, pv_, v, nv, 3)
        pa = _conv_pre(ta, wa, ba)
        pv = _conv_pre(tv, wv, bv)
        sg = _sigmoid(pa)
        dfe = jnp.concatenate([dfv, jnp.where(ctx.last, 0.0, dfn)], axis=0)
        dpa = dfe * pv * sg * (1.0 + pa * (1.0 - sg))
        dpv = dfe * pa * sg
        res = [_conv_t_rows(dpa, wa, 3, n), _conv_t_rows(dpv, wv, 3, n)]
        for dp, taps in ((dpa, ta), (dpv, tv)):
            res.append(_stack_rows([jnp.sum(dp[:n] * t[:n], axis=0, keepdims=True) for t in taps], 3))
        for dp in (dpa, dpv):
            res.append(jnp.sum(dp[:n], axis=0, keepdims=True))
        return res

    ins = []
    for base in (0, NFC):
        ins += [("prev", h, CW, base), ("row", h, CW, base), ("next", h, CW, base)]
    ins += [("ccol", w, CW, 0), ("ccol", w, CW, NFC), ("ccol", b, CW, 0), ("ccol", b, CW, NFC),
            ("row", df, CW, 0), ("next", df, CW, 0)]
    dha, dhv, dwa, dwv, dba, dbv = _rows(
        name, fn, ins, [(D_FF, CW, 0, BF16)] * 2, [(3, D_FF, CW)] * 2 + [(1, D_FF, CW)] * 2, tm=CONV_TM, nrows=S, ncol=NFC)
    return dha, dhv, jnp.concatenate([dwa, dwv], axis=1), jnp.concatenate([dba, dbv], axis=1)


NSLAB = D // LANES
CPS = 2


def _ssd_chunk(xs, Bs, Cs, dtraw, dtb, alog, prev):
    lsz = SSD_CHUNK
    lane = _iota((lsz, LANES), 1)
    row = _iota((lsz, LANES), 0)
    dt = jnp.where(lane < SSD_HEADS, _softplus(dtraw + dtb), 0.0)
    a = dt * (-jnp.exp(alog))
    tril = row >= lane
    a_cs = _fdot(tril.astype(F32), a)
    a_cst = a_cs.T
    a_last = jnp.sum(a, axis=0, keepdims=True)
    lo = lane < HD
    top = row < HD
    cbs = [_bdot_nt(Cs[g], Bs[g]) for g in range(2)]
    ys, news = [], []
    for s in range(NSLAB):
        g = s // (NSLAB // 2)
        cols, lms, dts, als = [], [], [], []
        for hh in range(2):
            h = 2 * s + hh
            col = _lane_pick(a_cs, h)
            seg = col - _row_pick(a_cst, h)
            lms.append(jnp.exp(jnp.where(tril, seg, NEG)))
            cols.append(col)
            dts.append(_lane_pick(dt, h))
            als.append(_lane_pick(a_last, h))
        col_x = jnp.where(lo, cols[0], cols[1])
        al_x = jnp.where(lo, als[0], als[1])
        xc = xs[s] * jnp.where(lo, dts[0], dts[1])
        yd = jnp.where(lo, _bdot_nn(cbs[g] * lms[0], xc), _bdot_nn(cbs[g] * lms[1], xc))
        yoff = _bdot_nt(Cs[g], prev[s]) * jnp.exp(col_x)
        ys.append(yd + yoff)
        st = _bdot_tn(xc * jnp.exp(al_x - col_x), Bs[g])
        news.append(prev[s] * jnp.exp(jnp.where(top, als[0], als[1])) + st)
    return ys, news


def _ssd_scan_fwd(xbc_c, pd, dtb, alog, name):
    S = xbc_c.shape[0]
    nc = S // SSD_CHUNK
    rows_ = CPS * SSD_CHUNK

    def body(x_ref, b_ref, c_ref, dt_ref, dtb_ref, al_ref, y_ref, st_ref, state):
        c = pl.program_id(0)

        @pl.when(c == 0)
        def _():
            state[...] = jnp.zeros_like(state)

        prev = [state[s * LANES:(s + 1) * LANES, :] for s in range(NSLAB)]
        for u in range(CPS):
            rw = pl.ds(u * SSD_CHUNK, SSD_CHUNK)
            xs = [x_ref[rw, s * LANES:(s + 1) * LANES] for s in range(NSLAB)]
            Bs = [b_ref[rw, g * SSD_N:(g + 1) * SSD_N] for g in range(2)]
            Cs = [c_ref[rw, g * SSD_N:(g + 1) * SSD_N] for g in range(2)]
            for s in range(NSLAB):
                st_ref[u, s * LANES:(s + 1) * LANES, :] = prev[s]
            ys, prev = _ssd_chunk(xs, Bs, Cs, dt_ref[rw, :].astype(F32), dtb_ref[...], al_ref[...], prev)
            for s in range(NSLAB):
                y_ref[rw, s * LANES:(s + 1) * LANES] = ys[s]
        for s in range(NSLAB):
            state[s * LANES:(s + 1) * LANES, :] = prev[s]

    return pl.pallas_call(
        body, name=name, grid=(nc // CPS,),
        in_specs=[pl.BlockSpec((rows_, D), lambda c: (c, 0)),
                  pl.BlockSpec((rows_, 2 * SSD_N), lambda c: (c, D // (2 * SSD_N))),
                  pl.BlockSpec((rows_, 2 * SSD_N), lambda c: (c, D // (2 * SSD_N) + 1)),
                  pl.BlockSpec((rows_, LANES), lambda c: (c, 0)),
                  pl.BlockSpec((1, LANES), lambda c: (0, 0)), pl.BlockSpec((1, LANES), lambda c: (0, 0))],
        out_specs=[pl.BlockSpec((rows_, D), lambda c: (c, 0)), pl.BlockSpec((CPS, D, SSD_N), lambda c: (c, 0, 0))],
        out_shape=[jax.ShapeDtypeStruct((S, D), F32), jax.ShapeDtypeStruct((nc, D, SSD_N), F32)],
        scratch_shapes=[pltpu.VMEM((D, SSD_N), F32)],
        compiler_params=pltpu.CompilerParams(dimension_semantics=("arbitrary",)),
    )(xbc_c, xbc_c, xbc_c, pd, dtb, alog)


def _ssd_scan_bwd(xbc_c, pd, dtb, alog, states, dy, dxs_skip, name):
    S = xbc_c.shape[0]
    nc = S // SSD_CHUNK
    rows_ = CPS * SSD_CHUNK

    def body(x_ref, b_ref, c_ref, dt_ref, dtb_ref, al_ref, st_ref, dy_ref, sk_ref,
             dx_ref, ddt_ref, ddtb_ref, dal_ref, dstate):
        c = pl.program_id(0)

        @pl.when(c == 0)
        def _():
            dstate[...] = jnp.zeros_like(dstate)
            ddtb_ref[...] = jnp.zeros_like(ddtb_ref)
            dal_ref[...] = jnp.zeros_like(dal_ref)

        dnew = [dstate[s * LANES:(s + 1) * LANES, :] for s in range(NSLAB)]
        for u in reversed(range(CPS)):
            rw = pl.ds(u * SSD_CHUNK, SSD_CHUNK)
            xs = [x_ref[rw, s * LANES:(s + 1) * LANES] for s in range(NSLAB)]
            Bs = [b_ref[rw, g * SSD_N:(g + 1) * SSD_N] for g in range(2)]
            Cs = [c_ref[rw, g * SSD_N:(g + 1) * SSD_N] for g in range(2)]
            prev = [st_ref[u, s * LANES:(s + 1) * LANES, :] for s in range(NSLAB)]
            _, vjp = jax.vjp(_ssd_chunk, xs, Bs, Cs, dt_ref[rw, :].astype(F32), dtb_ref[...], al_ref[...], prev)
            dys = [dy_ref[rw, s * LANES:(s + 1) * LANES] for s in range(NSLAB)]
            dxs, dBs, dCs, ddt, ddtb, dal, dnew = vjp((dys, dnew))
            for s in range(NSLAB):
                dx_ref[rw, s * LANES:(s + 1) * LANES] = dxs[s] + sk_ref[rw, s * LANES:(s + 1) * LANES]
            for g in range(2):
                dx_ref[rw, D + g * SSD_N:D + (g + 1) * SSD_N] = dBs[g]
                dx_ref[rw, D + 2 * SSD_N + g * SSD_N:D + 2 * SSD_N + (g + 1) * SSD_N] = dCs[g]
            ddt_ref[rw, :] = ddt
            ddtb_ref[...] += ddtb
            dal_ref[...] += dal
        for s in range(NSLAB):
            dstate[s * LANES:(s + 1) * LANES, :] = dnew[s]

    def rv(c):
        return nc // CPS - 1 - c

    return pl.pallas_call(
        body, name=name, grid=(nc // CPS,),
        in_specs=[pl.BlockSpec((rows_, D), lambda c: (rv(c), 0)),
                  pl.BlockSpec((rows_, 2 * SSD_N), lambda c: (rv(c), D // (2 * SSD_N))),
                  pl.BlockSpec((rows_, 2 * SSD_N), lambda c: (rv(c), D // (2 * SSD_N) + 1)),
                  pl.BlockSpec((rows_, LANES), lambda c: (rv(c), 0)),
                  pl.BlockSpec((1, LANES), lambda c: (0, 0)), pl.BlockSpec((1, LANES), lambda c: (0, 0)),
                  pl.BlockSpec((CPS, D, SSD_N), lambda c: (rv(c), 0, 0)),
                  pl.BlockSpec((rows_, D), lambda c: (rv(c), 0)),
                  pl.BlockSpec((rows_, D), lambda c: (rv(c), 0))],
        out_specs=[pl.BlockSpec((rows_, XBC), lambda c: (rv(c), 0)),
                   pl.BlockSpec((rows_, LANES), lambda c: (rv(c), 0)),
                   pl.BlockSpec((1, LANES), lambda c: (0, 0)), pl.BlockSpec((1, LANES), lambda c: (0, 0))],
        out_shape=[jax.ShapeDtypeStruct((S, XBC), F32), jax.ShapeDtypeStruct((S, LANES), F32),
                   jax.ShapeDtypeStruct((1, LANES), F32), jax.ShapeDtypeStruct((1, LANES), F32)],
        scratch_shapes=[pltpu.VMEM((D, SSD_N), F32)],
        compiler_params=pltpu.CompilerParams(dimension_semantics=("arbitrary",)),
    )(xbc_c, xbc_c, xbc_c, pd, dtb, alog, states, dy, dxs_skip)


def _ssd_post_core(y, xs, z, d128, nw):
    tm = y.shape[0]
    ex = (_iota((LANES, D), 1) // HD == _iota((LANES, D), 0)).astype(F32)
    d_x = jnp.sum(_fdot(jnp.broadcast_to(d128, (8, LANES)), ex), axis=0, keepdims=True) * 0.125
    y2 = (y + d_x * xs) * _silu(z)
    lo = _iota((tm, D), 1) < D // 2
    sq = y2 * y2
    ms0 = jnp.sum(jnp.where(lo, sq, 0.0), axis=-1, keepdims=True) / (D // 2)
    ms1 = jnp.sum(jnp.where(lo, 0.0, sq), axis=-1, keepdims=True) / (D // 2)
    r = jnp.where(lo, lax.rsqrt(ms0 + EPS), lax.rsqrt(ms1 + EPS))
    return y2 * r * nw


def _ssd_post_ins(y, xbc_c, pc, d128, nw):
    return [("row", y, None, 0), ("row", xbc_c, D, 0), ("row", pc, D, 0), ("const", d128, None, 0), ("const", nw, None, 0)]


def _ssd_post_fwd(y, xbc_c, pc, d128, nw, name):
    S = y.shape[0]
    return _rows(name, lambda ctx, *v: [_ssd_post_core(*v)], _ssd_post_ins(y, xbc_c, pc, d128, nw),
                 [(D, D, 0, BF16)], tm=256, nrows=S)[0]


def _ssd_post_bwd(y, xbc_c, pc, d128, nw, dout, name):
    S = y.shape[0]

    def fn(ctx, *v):
        _, vjp = jax.vjp(_ssd_post_core, *v[:5])
        return list(vjp(v[5]))

    return _rows(name, fn, _ssd_post_ins(y, xbc_c, pc, d128, nw) + [("row", dout, None, 0)],
                 [(D, D, 0, F32), (D, D, 0, F32), (D, D, 0, BF16)], [(1, LANES, LANES), (1, D, D)], tm=256, nrows=S)


def _gates_core(g0, g1, g2, b0, b1, b2, ya, yb, yc):
    return _sigmoid(g0 + b0) * ya + _sigmoid(g1 + b1) * yb + _sigmoid(g2 + b2) * yc


def _gate_parts(pdv, bv):
    gp = pltpu.roll(pdv, SEC_D - 16, 1)
    return [gp[:, k * D:(k + 1) * D] for k in range(3)] + [bv[:, k * D:(k + 1) * D] for k in range(3)]


def _gates_fwd(pd, bg, ya, yb, yc, name):
    S = pd.shape[0]

    def fn(ctx, pdv, bv, a, b, c):
        return [_gates_core(*_gate_parts(pdv, bv), a, b, c)]

    return _rows(name, fn, [("row", pd, None, 0), ("const", bg, None, 0), ("row", ya, None, 0), ("row", yb, None, 0),
                            ("row", yc, None, 0)], [(D, D, 0, BF16)], tm=256, nrows=S)[0]


def _gates_post(dm, pdv, a, b, c, bv):
    _, vjp = jax.vjp(_gates_core, *_gate_parts(pdv, bv), a, b, c)
    g = vjp(dm)
    return [g[6], g[7], g[8], jnp.concatenate(g[0:3], axis=1), jnp.concatenate(g[3:6], axis=1)]


def _adam_update(wv, gv, mv, vv):
    m2 = ADAM_B1 * mv + (1.0 - ADAM_B1) * gv
    v2 = ADAM_B2 * vv + (1.0 - ADAM_B2) * jnp.square(gv)
    m_hat = m2 / (1.0 - ADAM_B1 ** ADAM_STEP)
    v_hat = v2 / (1.0 - ADAM_B2 ** ADAM_STEP)
    delta = -ADAM_LR * (m_hat / (jnp.sqrt(v_hat) + ADAM_EPS) + ADAM_WD * wv)
    return [delta, m2, v2]


def _adamw(w, g, m, v, name):
    rows, C = w.shape
    tm = _pick(rows, [t for t in (512, 256, 128, 64, 32, 16, 8) if t * C <= ADAM_TILE])
    return _rows(name, lambda ctx, *a: _adam_update(*a), [("row", a, None, 0) for a in (w, g, m, v)],
                 [(C, C, 0, F32)] * 3, tm=tm, nrows=rows)


def _position():
    return lax.axis_index("x"), lax.axis_index("y"), lax.axis_index("c")


def _other_chips(x, y):
    return [(1 - x, y), (x, 1 - y), (1 - x, 1 - y)]


_HBM = pl.BlockSpec(memory_space=pltpu.HBM)


def _gather_parts(half, lo, n):
    def copies(p_ref, out_ref, send_sems, recv_sems):
        x, y, c = _position()
        sibling = (x, y, 1 - c)
        chips = _other_chips(x, y)

        def slab(chip, h):
            return out_ref.at[2 * chip[0] + chip[1], pl.ds(h * half + lo, n), :]

        def copy(k, src, dst, to):
            return pltpu.make_async_remote_copy(src_ref=src, dst_ref=dst, send_sem=send_sems.at[k],
                                                recv_sem=recv_sems.at[k], device_id=to, device_id_type=MESH)

        first = [copy(j, p_ref.at[pl.ds(c * half + lo, n), :], slab((x, y), c), (*chip, c)) for j, chip in enumerate(chips)]
        passed = [copy(3 + j, slab(chip, c), slab(chip, c), sibling) for j, chip in enumerate(chips)]
        from_chips = [copy(j, slab(chip, c), slab(chip, c), (x, y, c)) for j, chip in enumerate(chips)]
        from_sibling = [copy(3 + j, slab(chip, 1 - c), slab(chip, 1 - c), (x, y, c)) for j, chip in enumerate(chips)]
        return first, passed, from_chips, from_sibling

    def start(ins, outs, scr):
        for cp in copies(ins[0], outs[0], *scr)[0]:
            cp.start()

    def finish(ins, outs, scr):
        first, passed, from_chips, from_sibling = copies(ins[0], outs[0], *scr)
        for j in range(3):
            from_chips[j].wait_recv()
            passed[j].start()
        for cp in from_sibling:
            cp.wait_recv()
        for cp in first + passed:
            cp.wait_send()

    return start, finish


def _rs_chip_parts(lo, n):
    def copies(h_ref, out_ref, send_sems, recv_sems):
        x, y, c = _position()
        return [pltpu.make_async_remote_copy(src_ref=h_ref.at[2 * chip[0] + chip[1], pl.ds(lo, n), :],
                                             dst_ref=out_ref.at[j, pl.ds(lo, n), :],
                                             send_sem=send_sems.at[j], recv_sem=recv_sems.at[j],
                                             device_id=(*chip, c), device_id_type=MESH)
                for j, chip in enumerate(_other_chips(x, y))]

    def start(ins, outs, scr):
        for cp in copies(ins[0], outs[0], *scr):
            cp.start()

    def finish(ins, outs, scr):
        for cp in copies(ins[0], outs[0], *scr):
            cp.wait()

    return start, finish


class _Stream:
    def __init__(self, src, buf, parts, nsem, units, name):
        self.src, self.buf, self.parts, self.nsem, self.name = src, buf, parts, nsem, name
        self.next, self.units = 0, units

    def _scratch(self):
        return [pltpu.SemaphoreType.DMA((self.nsem,)), pltpu.SemaphoreType.DMA((self.nsem,))]

    def _take(self, units):
        units = min(units, self.units - self.next)
        lo = self.next * 16
        self.next += units
        return lo, units * 16

    def _set(self, outs):
        self.buf = outs[0]

    def hook(self, units):
        lo, n = self._take(units)
        if n == 0:
            return None
        start, finish = self.parts(lo, n)
        return _Hook([self.src, self.buf], [jax.ShapeDtypeStruct(self.buf.shape, self.buf.dtype)], {1: 0},
                     self._scratch(), start, finish, self._set)

    def drain(self):
        lo, n = self._take(self.units)
        if n:
            start, finish = self.parts(lo, n)

            def body(s_ref, b_ref, o_ref, send_sems, recv_sems):
                args = ((s_ref, b_ref), (o_ref,), (send_sems, recv_sems))
                start(*args)
                finish(*args)

            self.buf = pl.pallas_call(
                body, name=self.name, in_specs=[_ANY, _ANY], out_specs=_ANY,
                out_shape=jax.ShapeDtypeStruct(self.buf.shape, self.buf.dtype),
                scratch_shapes=self._scratch(), input_output_aliases={1: 0},
            )(self.src, self.buf)
        return self.buf


def _rs_pair_parts(half, lo, n):
    def copy(g_ref, out_ref, send_sems, recv_sems):
        x, y, c = _position()
        return pltpu.make_async_remote_copy(
            src_ref=g_ref.at[pl.ds(0, 4), pl.ds((1 - c) * half + lo, n), :], dst_ref=out_ref.at[pl.ds(0, 4), pl.ds(lo, n), :],
            send_sem=send_sems.at[0], recv_sem=recv_sems.at[0], device_id=(x, y, 1 - c), device_id_type=MESH)

    def start(ins, outs, scr):
        copy(ins[0], outs[0], *scr).start()

    def finish(ins, outs, scr):
        copy(ins[0], outs[0], *scr).wait()

    return start, finish


def _rs_swap(r, name):
    Rh, C = r.shape

    def body(r_ref, out_ref, send_sem, recv_sem):
        x, y, c = _position()
        cp = pltpu.make_async_remote_copy(src_ref=r_ref, dst_ref=out_ref, send_sem=send_sem,
                                          recv_sem=recv_sem, device_id=(x, y, 1 - c), device_id_type=MESH)
        cp.start()
        cp.wait()

    return pl.pallas_call(
        body, name=name, in_specs=[_HBM], out_specs=_HBM,
        out_shape=jax.ShapeDtypeStruct((Rh, C), r.dtype),
        scratch_shapes=[pltpu.SemaphoreType.DMA, pltpu.SemaphoreType.DMA],
    )(r)


def _rs_add_pair(g, recv, cidx, name):
    _, R, C = g.shape
    Rh = R // 2
    tm = _pick(Rh, (400, 280, 200, 160, 80, 40, 16, 8))
    nt = Rh // tm

    def body(c_ref, g_ref, r_ref, o_ref):
        o_ref[...] = (g_ref[...].astype(F32) + r_ref[...].astype(F32)).astype(o_ref.dtype)

    return pl.pallas_call(
        body, name=name,
        grid_spec=pltpu.PrefetchScalarGridSpec(
            num_scalar_prefetch=1, grid=(4, nt),
            in_specs=[pl.BlockSpec((1, tm, C), lambda k, i, cr: (k, cr[0] * nt + i, 0)),
                      pl.BlockSpec((1, tm, C), lambda k, i, cr: (k, i, 0))],
            out_specs=pl.BlockSpec((1, tm, C), lambda k, i, cr: (k, i, 0))),
        out_shape=jax.ShapeDtypeStruct((4, Rh, C), BF16),
    )(cidx, g, recv)


def _rs_add_chips(h, recv, chip_idx, name):
    _, Rh, C = h.shape
    tm = _pick(Rh, (400, 280, 200, 160, 80, 40, 16, 8))

    def body(c_ref, h_ref, r_ref, o_ref):
        acc = h_ref[0].astype(F32)
        for j in range(3):
            acc = acc + r_ref[j].astype(F32)
        o_ref[...] = acc

    return pl.pallas_call(
        body, name=name,
        grid_spec=pltpu.PrefetchScalarGridSpec(
            num_scalar_prefetch=1, grid=(Rh // tm,),
            in_specs=[pl.BlockSpec((1, tm, C), lambda i, cr: (cr[0], i, 0)), pl.BlockSpec((3, tm, C), lambda i, cr: (0, i, 0))],
            out_specs=pl.BlockSpec((tm, C), lambda i, cr: (i, 0))),
        out_shape=jax.ShapeDtypeStruct((Rh, C), F32),
    )(chip_idx, h, recv)


def _all_reduce_small(vec, name):
    n, C = vec.shape

    def body(v_ref, out_ref, buf, send_sems, recv_sems):
        x, y, c = _position()

        def flip(k):
            return ((1 - x) if k & 4 else x, (1 - y) if k & 2 else y, (1 - c) if k & 1 else c)

        def idx(p):
            return 4 * p[0] + 2 * p[1] + p[2]

        me = idx((x, y, c))
        buf[me] = v_ref[...]
        cps = [pltpu.make_async_remote_copy(src_ref=v_ref, dst_ref=buf.at[me], send_sem=send_sems.at[k - 1],
                                            recv_sem=recv_sems.at[k - 1], device_id=flip(k), device_id_type=MESH)
               for k in range(1, 8)]
        for cp in cps:
            cp.start()
        for k in range(1, 8):
            pltpu.make_async_remote_copy(src_ref=v_ref, dst_ref=buf.at[idx(flip(k))], send_sem=send_sems.at[k - 1],
                                         recv_sem=recv_sems.at[k - 1], device_id=flip(k), device_id_type=MESH).wait_recv()
        for cp in cps:
            cp.wait_send()
        acc = buf[0]
        for s in range(1, 8):
            acc = acc + buf[s]
        out_ref[...] = acc

    return pl.pallas_call(
        body, name=name,
        in_specs=[pl.BlockSpec(memory_space=pltpu.VMEM)], out_specs=pl.BlockSpec(memory_space=pltpu.VMEM),
        out_shape=jax.ShapeDtypeStruct((n, C), F32),
        scratch_shapes=[pltpu.VMEM((8, n, C), F32), pltpu.SemaphoreType.DMA((7,)), pltpu.SemaphoreType.DMA((7,))],
    )(vec)


BIG = (("w_in", (D, IN_WIDTH // 4), "cols"), ("w_a", (GW, D // 4), "cols"), ("pool_w", (4, PG // 4, PG), "pool"),
       ("w_b", (D // 4, D), "rows"), ("w_c", (D // 4, D), "rows"), ("w_o", (D // 4, D), "rows"),
       ("ffn_w_up", (D, 2 * D_FF // 4), "cols"), ("ffn_w_down", (D_FF // 4, D), "rows"))
def _pack_rows(s):
    k = math.prod(s) // D
    return -(-k // 16) * 16, k


PACK_ROWS = sum(_pack_rows(s)[0] for _, s, _ in BIG)
PACK_PAD = -(-PACK_ROWS // 32) * 32


def _pad_rows(v, rows):
    pad = [(0, 0)] * v.ndim
    pad[-2] = (0, rows - v.shape[-2])
    return jnp.pad(v, pad) if rows > v.shape[-2] else v


def _pack_blocks(blocks, dtype):
    lead = blocks["w_in"].shape[:-2]
    flat = []
    for n, s, how in BIG:
        v = blocks[n].astype(dtype)
        if how == "cols":
            v = jnp.swapaxes(v, -1, -2)
        flat.append(_pad_rows(v.reshape(*lead, -1, D), _pack_rows(s)[0]))
    flat.append(jnp.zeros((*lead, PACK_PAD - PACK_ROWS, D), dtype))
    return jnp.concatenate(flat, axis=-2)


def _unpack_blocks(pack):
    out, r = {}, 0
    for n, s, how in BIG:
        rows, k = _pack_rows(s)
        v = pack[r:r + k, :]
        out[n] = v.reshape(s[1], s[0]).T if how == "cols" else v.reshape(s)
        r += rows
    return out


def _operands(allp):
    out, r = {}, 0
    for n, s, how in BIG:
        rows, k = _pack_rows(s)
        v = allp[:, r:r + k, :]
        if how == "cols":
            out[n] = v.reshape(4 * s[1], s[0])
        elif how == "rows":
            out[n] = v.reshape(4 * s[0], s[1])
        else:
            out[n] = v.reshape(4, *s).transpose(1, 0, 2, 3).reshape(4, PG, PG)
        r += rows
    return out


def _pack_operands(g, dtype):
    flat = []
    for n, s, how in BIG:
        v = g[n].astype(dtype)
        if how == "pool":
            v = v.reshape(4, 4, s[1], s[2]).transpose(1, 0, 2, 3)
        flat.append(_pad_rows(v.reshape(4, -1, D), _pack_rows(s)[0]))
    flat.append(jnp.zeros((4, PACK_PAD - PACK_ROWS, D), dtype))
    return jnp.concatenate(flat, axis=1)


def _layer_fwd(x, w, sm, bias, hk):
    pa, u = _mmf(None, w["in_a"], tb=True, pre=(_rms_core, [x], [sm["ln1_g"]]), name="in_a", tm=1024, hook=hk("in_a"))
    pb = _mm(u, w["in_b"], tb=True, out_dtype=BF16, name="in_b", hook=hk("in_b"))
    pc = _mm(u, w["in_c"], tb=True, out_dtype=BF16, name="in_c", hook=hk("in_c"))
    pd = _mm(u, w["in_d"], tb=True, out_dtype=BF16, name="in_d", hook=hk("in_d"))
    os_, ls_ = [], []
    for gi in range(3):
        o, l = _attn_fwd(pa, bias[gi], gi, "attn_fwd%d" % gi)
        os_.append(o)
        ls_.append(l)
    att = _mix_fwd(os_, ls_, "mix_fwd")
    ya = _mm(att, w["w_a"], tb=True, out_dtype=BF16, name="mm_wa")
    pool_o = _pool_fwd(pb, w["pool_w"], sm["pool_scale"], "pool_fwd")
    yb = _mm(pool_o, w["w_b"], out_dtype=BF16, name="mm_wb")
    xbc_c = _ssd_conv_fwd(pc, sm["ssd_conv_w"], sm["ssd_conv_b"], "ssd_conv_fwd")
    y_scan, states = _ssd_scan_fwd(xbc_c, pd, sm["ssd_dt_bias"], sm["ssd_a_log"], "ssd_scan_fwd")
    ssd_o = _ssd_post_fwd(y_scan, xbc_c, pc, sm["ssd_d"], sm["ssd_norm_w"], "ssd_post_fwd")
    yc = _mm(ssd_o, w["w_c"], out_dtype=BF16, name="mm_wc")
    merged = _gates_fwd(pd, sm["b_gate"], ya, yb, yc, "gates_fwd")
    x1 = _mm(merged, w["w_o"], add=x, name="mm_wo", hook=hk("mm_wo"))
    h, u2 = _mmf(None, w["ffn_w_up"], tb=True, pre=(_rms_core, [x1], [sm["ln2_g"]]), out_dtype=BF16, name="mm_up",
                 tm=1024, hook=hk("mm_up"))
    f = _ffn_act_fwd(h, sm["ffn_conv_w"], sm["ffn_conv_b"], "ffn_act_fwd")
    x2 = _mm(f, w["ffn_w_down"], add=x1, name="mm_down", hook=hk("mm_down"))
    saved = dict(x=x, u=u, pa=pa, pb=pb, pc=pc, pd=pd, os=os_, ls=ls_, att=att, ya=ya, yb=yb, yc=yc, pool_o=pool_o,
                 xbc_c=xbc_c, y_scan=y_scan, states=states, ssd_o=ssd_o, merged=merged, x1=x1, u2=u2, h=h, f=f)
    return x2, saved


def _layer_bwd(dx2, dx2b, w, sm, bias, dbs, sv, hk):
    gw, gs = {}, {}
    S = dx2.shape[0]

    def gmm(a, b, name):
        return _mm(a, b, ta=True, out_dtype=BF16, name=name, hook=hk(name))

    df = _mm(dx2b, w["ffn_w_down"], tb=True, out_dtype=BF16, name="d_f", hook=hk("d_f"))
    gw["ffn_w_down"] = gmm(sv["f"], dx2b, "g_down")
    dha, dhv, gs["ffn_conv_w"], gs["ffn_conv_b"] = _ffn_act_bwd(sv["h"], sm["ffn_conv_w"], sm["ffn_conv_b"], df, "ffn_act_bwd")
    dx1, dx1b, gs["ln2_g"] = _mmf([dha, dhv], [w["up_a"], w["up_v"]], name="d_u2_v", tm=256, hook=hk("d_u2_v"),
                                  post=(_rms_post, [sv["x1"], dx2], [sm["ln2_g"]], RMS_POST_OUTS, [(1, D)]))
    gw["ffn_w_up"] = jnp.concatenate([gmm(dha, sv["u2"], "g_up_a"), gmm(dhv, sv["u2"], "g_up_v")], axis=0)
    dya, dyb, dyc, dgate, gs["b_gate"] = _mmf(
        dx1b, w["w_o"], tb=True, name="d_merged", tm=256, hook=hk("d_merged"),
        post=(_gates_post, [sv["pd"], sv["ya"], sv["yb"], sv["yc"]], [sm["b_gate"]],
              [(D, BF16)] * 3 + [(3 * D, BF16)], [(1, 3 * D)]))
    gw["w_o"] = gmm(sv["merged"], dx1b, "g_wo")
    dssd_o = _mm(dyc, w["w_c"], tb=True, name="d_ssd_o")
    gw["w_c"] = gmm(sv["ssd_o"], dyc, "g_wc")
    dy_scan, dxs_skip, dz, gs["ssd_d"], gs["ssd_norm_w"] = _ssd_post_bwd(
        sv["y_scan"], sv["xbc_c"], sv["pc"], sm["ssd_d"], sm["ssd_norm_w"], dssd_o, "ssd_post_bwd")
    dxbc_c, ddt, gs["ssd_dt_bias"], gs["ssd_a_log"] = _ssd_scan_bwd(
        sv["xbc_c"], sv["pd"], sm["ssd_dt_bias"], sm["ssd_a_log"], sv["states"], dy_scan, dxs_skip, "ssd_scan_bwd")
    dxbc, gs["ssd_conv_w"], gs["ssd_conv_b"] = _ssd_conv_bwd(sv["pc"], sm["ssd_conv_w"], sm["ssd_conv_b"], dxbc_c, "ssd_conv_bwd")
    dpool_o = _mm(dyb, w["w_b"], tb=True, name="d_pool_o")
    gw["w_b"] = gmm(sv["pool_o"], dyb, "g_wb")
    dpb, dpw, gs["pool_scale"] = _pool_bwd(sv["pb"], w["pool_w"], sm["pool_scale"], dpool_o, "pool_bwd")
    gw["pool_w"] = dpw.reshape(4, PG, PG)
    datt = _mm(dya, w["w_a"], name="d_att")
    gw["w_a"] = gmm(dya, sv["att"], "g_wa")
    dos, dls = _mix_bwd(sv["os"], sv["ls"], datt, "mix_bwd")
    dqkv = tuple(lax.empty((S, AW), F32) for _ in range(3))
    dbs = list(dbs)
    for gi in range(3):
        dqkv, dbs[gi] = _attn_bwd(sv["pa"], bias[gi], dos[gi], dls[gi], dbs[gi], dqkv, gi, "attn_bwd%d" % gi)
    u = sv["u"]
    pieces = [(dqkv[0], "wq"), (dqkv[1], "wk"), (dqkv[2], "wv"), (dpb, "in_b"), (dz, "wz"), (dxbc, "wxbc"),
              (ddt, "wdt"), (dgate, "wgate")]
    du = _mmf([dp for dp, _ in pieces[:4]], [w[key] for _, key in pieces[:4]], name="d_u_a", tm=256, hook=hk("d_u_a"))[0]
    dx, dxb, gs["ln1_g"] = _mmf([dp for dp, _ in pieces[4:]], [w[key] for _, key in pieces[4:]], add=du,
                                name="d_u_wgate", tm=256, hook=hk("d_u_wgate"),
                                post=(_rms_post, [sv["x"], dx1], [sm["ln1_g"]], RMS_POST_OUTS, [(1, D)]))
    g_in = []
    for dp, key in pieces:
        g = gmm(dp, u, "g_in_" + key)
        g_in.append(g[:SSD_HEADS] if key == "wdt" else g)
    gw["w_in"] = jnp.concatenate(g_in, axis=0)
    return dx, dxb, gw, gs, dbs


SMALL_LAYER = ("ln1_g", "b_gate", "pool_scale", "ssd_conv_w", "ssd_conv_b", "ssd_dt_bias", "ssd_a_log", "ssd_d",
               "ssd_norm_w", "ln2_g", "ffn_conv_w", "ffn_conv_b")


def _pad_lanes(v):
    return jnp.pad(v, (0, LANES - v.shape[0])).reshape(1, LANES)


def _layer_weights(ops):
    wt = ops["w_in"]
    o1, o2, o3 = SEC_A, SEC_A + SEC_B, SEC_A + SEC_B + SEC_C
    w = dict(ops)
    w["in_a"] = jnp.pad(wt[:o1], ((0, SEC_A_PAD - o1), (0, 0)))
    w["in_b"] = wt[o1:o2]
    w["in_c"] = wt[o2:o3]
    w["in_d"] = jnp.pad(wt[o3:], ((0, SEC_D - (IN_WIDTH - o3)), (0, 0)))
    w["wq"], w["wk"], w["wv"] = wt[:AW], wt[AW:2 * AW], wt[2 * AW:o1]
    w["wz"], w["wxbc"] = wt[o2:o2 + D], wt[o2 + D:o3]
    w["wdt"] = jnp.pad(wt[o3:o3 + SSD_HEADS], ((0, LANES - SSD_HEADS), (0, 0)))
    w["wgate"] = wt[o3 + SSD_HEADS:]
    w["up_a"], w["up_v"] = ops["ffn_w_up"][:D_FF], ops["ffn_w_up"][D_FF:]
    return w


def _layer_small(p, i):
    sm = {n: p[n][i] for n in SMALL_LAYER}
    out = {}
    for n, v in sm.items():
        if n in ("ssd_dt_bias", "ssd_a_log", "ssd_d"):
            out[n] = _pad_lanes(v)
        elif v.ndim == 1:
            out[n] = v.reshape(1, -1)
        else:
            out[n] = v
    return out


def _local_step(x, target, rel_bias, final_g, layer_full, small, fwd_hooks=None, bwd_hooks=None, after_bwd=None):
    nl = small["ln1_g"].shape[0]
    buckets = [_buckets(d).astype(jnp.int32) for d in DILATIONS]
    bias = [_bias_table(rel_bias, buckets[gi], gi, "bias_table%d" % gi) for gi in range(3)]
    no_hooks = lambda i: (lambda name: None)
    fwd_hooks = fwd_hooks or no_hooks
    bwd_hooks = bwd_hooks or no_hooks
    saved, ws, sms = [], [], []
    h = x
    for i in range(nl):
        w = _layer_weights(layer_full(i))
        sm = _layer_small(small, i)
        h, sv = _layer_fwd(h, w, sm, bias, fwd_hooks(i))
        saved.append(sv)
        ws.append(w)
        sms.append(sm)
    dh, dhb, dfinal, loss = _final_loss(h, target, final_g.reshape(1, D))
    gws, gss = [None] * nl, [None] * nl
    dbs = [jnp.zeros((6, WIN, 2 * WIN), F32)] * 3
    for i in reversed(range(nl)):
        dh, dhb, gws[i], gss[i], dbs = _layer_bwd(dh, dhb, ws[i], sms[i], bias, dbs, saved[i], bwd_hooks(i))
        if after_bwd is not None:
            after_bwd(i, gws[i])
    drel = []
    for gi in range(3):
        onehot = jnp.pad(jax.nn.one_hot(buckets[gi].reshape(-1), REL_BUCKETS, dtype=BF16), ((0, 0), (0, LANES - REL_BUCKETS)))
        drel.append(_mm(dbs[gi].reshape(6, WIN * 2 * WIN), onehot, name="g_relb"))
    return loss, dh, gws, gss, dfinal, jnp.concatenate(drel, axis=0)


WEIGHTS = ("rel_bias", "ln1_g", "w_in", "b_gate", "w_a", "pool_w", "pool_scale", "w_b", "ssd_conv_w", "ssd_conv_b",
           "ssd_dt_bias", "ssd_a_log", "ssd_d", "ssd_norm_w", "w_c", "w_o", "ln2_g", "ffn_w_up", "ffn_conv_w",
           "ffn_conv_b", "ffn_w_down", "final_g")
BIG_NAMES = tuple(n for n, _, _ in BIG)
SHARDED_SMALL = {"ssd_conv_w": XBC // 4, "ffn_conv_w": 2 * D_FF // 4}


def _to_rows(flat):
    n = flat.shape[0]
    rows = -(-n // LANES)
    rows = -(-rows // 8) * 8
    return jnp.pad(flat, (0, rows * LANES - n)).reshape(rows, LANES)


def _flatten(tree, names):
    return jnp.concatenate([tree[n].reshape(-1) for n in names])


def _unflatten(flat, shapes, names):
    out, o = {}, 0
    for n in names:
        k = math.prod(shapes[n])
        out[n] = flat[o:o + k].reshape(shapes[n])
        o += k
    return out


def kernel(x, rel_bias, ln1_g, w_in, b_gate, w_a, pool_w, pool_scale, w_b, ssd_conv_w, ssd_conv_b, ssd_dt_bias, ssd_a_log, ssd_d, ssd_norm_w, w_c, w_o, ln2_g, ffn_w_up, ffn_conv_w, ffn_conv_b, ffn_w_down, final_g, loss_target, m_rel_bias, m_ln1_g, m_w_in, m_b_gate, m_w_a, m_pool_w, m_pool_scale, m_w_b, m_ssd_conv_w, m_ssd_conv_b, m_ssd_dt_bias, m_ssd_a_log, m_ssd_d, m_ssd_norm_w, m_w_c, m_w_o, m_ln2_g, m_ffn_w_up, m_ffn_conv_w, m_ffn_conv_b, m_ffn_w_down, m_final_g, v_rel_bias, v_ln1_g, v_w_in, v_b_gate, v_w_a, v_pool_w, v_pool_scale, v_w_b, v_ssd_conv_w, v_ssd_conv_b, v_ssd_dt_bias, v_ssd_a_log, v_ssd_d, v_ssd_norm_w, v_w_c, v_w_o, v_ln2_g, v_ffn_w_up, v_ffn_conv_w, v_ffn_conv_b, v_ffn_w_down, v_final_g):
    W = dict(rel_bias=rel_bias, ln1_g=ln1_g, w_in=w_in, b_gate=b_gate, w_a=w_a, pool_w=pool_w, pool_scale=pool_scale,
             w_b=w_b, ssd_conv_w=ssd_conv_w, ssd_conv_b=ssd_conv_b, ssd_dt_bias=ssd_dt_bias, ssd_a_log=ssd_a_log,
             ssd_d=ssd_d, ssd_norm_w=ssd_norm_w, w_c=w_c, w_o=w_o, ln2_g=ln2_g, ffn_w_up=ffn_w_up,
             ffn_conv_w=ffn_conv_w, ffn_conv_b=ffn_conv_b, ffn_w_down=ffn_w_down, final_g=final_g)
    M = dict(rel_bias=m_rel_bias, ln1_g=m_ln1_g, w_in=m_w_in, b_gate=m_b_gate, w_a=m_w_a, pool_w=m_pool_w,
             pool_scale=m_pool_scale, w_b=m_w_b, ssd_conv_w=m_ssd_conv_w, ssd_conv_b=m_ssd_conv_b,
             ssd_dt_bias=m_ssd_dt_bias, ssd_a_log=m_ssd_a_log, ssd_d=m_ssd_d, ssd_norm_w=m_ssd_norm_w, w_c=m_w_c,
             w_o=m_w_o, ln2_g=m_ln2_g, ffn_w_up=m_ffn_w_up, ffn_conv_w=m_ffn_conv_w, ffn_conv_b=m_ffn_conv_b,
             ffn_w_down=m_ffn_w_down, final_g=m_final_g)
    V = dict(rel_bias=v_rel_bias, ln1_g=v_ln1_g, w_in=v_w_in, b_gate=v_b_gate, w_a=v_w_a, pool_w=v_pool_w,
             pool_scale=v_pool_scale, w_b=v_w_b, ssd_conv_w=v_ssd_conv_w, ssd_conv_b=v_ssd_conv_b,
             ssd_dt_bias=v_ssd_dt_bias, ssd_a_log=v_ssd_a_log, ssd_d=v_ssd_d, ssd_norm_w=v_ssd_norm_w, w_c=v_w_c,
             w_o=v_w_o, ln2_g=v_ln2_g, ffn_w_up=v_ffn_w_up, ffn_conv_w=v_ffn_conv_w, ffn_conv_b=v_ffn_conv_b,
             ffn_w_down=v_ffn_w_down, final_g=v_final_g)
    nl = ln1_g.shape[0]
    px, py, pc_ = _position()
    chip = 2 * px + py
    cidx = jnp.reshape(pc_, (1,)).astype(jnp.int32)
    chip_idx = jnp.reshape(chip, (1,)).astype(jnp.int32)

    placed = {}
    for n, cs in SHARDED_SMALL.items():
        full = jnp.zeros(W[n].shape[:-1] + (4 * cs,), F32)
        full = lax.dynamic_update_slice(full, W[n], (0, 0, chip * cs))
        placed[n] = jnp.where(pc_ == 0, full, 0.0)
    names_sh = tuple(SHARDED_SMALL)
    shapes_sh = {n: placed[n].shape for n in names_sh}
    got = _all_reduce_small(_to_rows(_flatten(placed, names_sh)), "gather_small")
    small = {n: W[n] for n in SMALL_LAYER}
    small.update(_unflatten(got.reshape(-1), shapes_sh, names_sh))

    packs = _pack_blocks({n: W[n] for n in BIG_NAMES}, BF16)

    half = PACK_PAD // 2
    units = half // 16

    def share(weights, total):
        tot = sum(weights.values())
        return {n: math.ceil(total * v / tot) for n, v in weights.items()}

    gathers = {}

    def gather(i):
        if i not in gathers:
            buf = lax.dynamic_update_slice(lax.empty((4, PACK_PAD, D), BF16), packs[i][None], (chip, 0, 0))
            gathers[i] = _Stream(packs[i], buf, functools.partial(_gather_parts, half), 6, units, "gather_w")
        return gathers[i]

    def layer_full(i):
        return _operands(gather(i).drain())

    fwd_share = share(dict(in_a=63, in_c=31, in_d=44, mm_up=83, mm_down=34), units)

    def fwd_hooks(i):
        if i + 1 >= nl:
            return lambda name: None
        return lambda name: gather(i + 1).hook(fwd_share[name]) if name in fwd_share else None

    exchanges = {}
    bwd_share = share(dict(g_down=35, d_u2_v=60, g_up_a=35, g_up_v=35, d_merged=50, d_u_a=60, d_u_wgate=70,
                           g_in_wgate=36), units)

    class Exchange:
        def __init__(self, g):
            self.g = g
            self.pair = _Stream(g, lax.empty((4, half, D), BF16), functools.partial(_rs_pair_parts, half), 1, units, "rs_pair")
            self.hsum = self.chips = None

        def to_chips(self):
            if self.chips is None:
                self.hsum = _rs_add_pair(self.g, self.pair.drain(), cidx, "rs_add_pair")
                self.chips = _Stream(self.hsum, lax.empty((3, half, D), BF16), _rs_chip_parts, 3, units, "rs_chips")
            return self.chips

    def after_bwd(i, gw):
        exchanges[i] = Exchange(_pack_operands(gw, BF16))

    def bwd_hooks(i):
        if i + 1 >= nl:
            return lambda name: None

        def hk(name):
            if name == "d_f":
                return exchanges[i + 1].pair.hook(units)
            return exchanges[i + 1].to_chips().hook(bwd_share[name]) if name in bwd_share else None

        return hk

    loss, dx, gws, gss, dfinal, drel = _local_step(x[0], loss_target[0], rel_bias, final_g, layer_full, small,
                                                   fwd_hooks, bwd_hooks, after_bwd)

    def reduced(i):
        recv3 = exchanges[i].to_chips().drain()
        r = _rs_add_chips(exchanges[i].hsum, recv3, chip_idx, "rs_add_chips")
        other = _rs_swap(r, "rs_swap")
        both = jnp.concatenate([jnp.where(pc_ == 0, r, other), jnp.where(pc_ == 0, other, r)], axis=0)
        return _unpack_blocks(both)

    red = [reduced(i) for i in range(nl)]
    delta, new_m, new_v, grads = {}, {}, {}, {}
    for n in BIG_NAMES:
        shp = W[n].shape
        r2 = lambda a: a.reshape(-1, shp[-1])
        grads[n] = jnp.stack([red[i][n] for i in range(nl)], axis=0)
        res = _adamw(r2(W[n]), r2(grads[n]), r2(M[n]), r2(V[n]), "adamw_" + n)
        delta[n], new_m[n], new_v[n] = [a.reshape(shp) for a in res]

    sg = {}
    for n in SMALL_LAYER:
        sg[n] = jnp.stack([gss[i][n] for i in range(nl)], axis=0)
    for n in ("ssd_dt_bias", "ssd_a_log", "ssd_d"):
        sg[n] = sg[n][:, 0, :SSD_HEADS]
    sg["rel_bias"] = drel[:, :REL_BUCKETS].T
    sg["final_g"] = dfinal.reshape(D)
    sg["loss"] = loss[0, :1]
    names_sg = tuple(sg)
    shapes_sg = {n: ((nl,) + W[n].shape[1:] if n in SMALL_LAYER and n not in SHARDED_SMALL else
                     (placed[n].shape if n in SHARDED_SMALL else sg[n].shape)) for n in names_sg}
    for n in names_sg:
        sg[n] = sg[n].reshape(shapes_sg[n])
    tot = _all_reduce_small(_to_rows(_flatten(sg, names_sg)), "allreduce_small")
    tot = _unflatten(tot.reshape(-1), shapes_sg, names_sg)
    loss_out = tot.pop("loss").reshape(())
    for n, cs in SHARDED_SMALL.items():
        tot[n] = lax.dynamic_slice(tot[n], (0, 0, chip * cs), tot[n].shape[:-1] + (cs,))
    grads.update(tot)

    names_s = tuple(n for n in WEIGHTS if n not in BIG_NAMES)
    shapes_s = {n: W[n].shape for n in names_s}
    pk = lambda t: _to_rows(_flatten(t, names_s))
    dl, m2, v2 = _adamw(pk(W), pk(grads), pk(M), pk(V), "adamw_small")
    delta.update(_unflatten(dl.reshape(-1), shapes_s, names_s))
    new_m.update(_unflatten(m2.reshape(-1), shapes_s, names_s))
    new_v.update(_unflatten(v2.reshape(-1), shapes_s, names_s))

    return (loss_out, dx[None], *[grads[n] for n in WEIGHTS], *[delta[n] for n in WEIGHTS],
            *[new_m[n] for n in WEIGHTS], *[new_v[n] for n in WEIGHTS])
```

```python
import functools
import math

import jax
import jax.numpy as jnp
from jax import lax
from jax.experimental import pallas as pl
from jax.experimental.pallas import tpu as pltpu

F32 = jnp.float32
BF16 = jnp.bfloat16
MESH = pl.DeviceIdType.MESH

D = 1024
HD = 64
GW = 384
AW = 3 * GW
WIN = 128
DILATIONS = (1, 4, 16)
REL_BUCKETS = 32
REL_MAX_DISTANCE = 2048
POOL_WINDOWS = (2, 4, 8, 16)
PG = 256
SSD_HEADS = 16
SSD_N = 128
SSD_CHUNK = 128
XBC = 1536
D_FF = 2816
EPS = 1e-6
NEG = -1e30
HALO = 16
LANES = 128

SEC_A = 3 * AW
SEC_B = D
SEC_C = D + XBC
SEC_D = 3328
SEC_A_PAD = 3584
IN_WIDTH = SEC_A + SEC_B + SEC_C + 16 + 3 * D

ADAM_LR = 0.001
ADAM_B1 = 0.9
ADAM_B2 = 0.999
ADAM_EPS = 1e-08
ADAM_WD = 0.01
ADAM_STEP = 10
ADAM_TILE = 256 * 1024
MM_VMEM_BYTES = 40 * 1024 * 1024
MM_MAX_OUT_TILE = 1024 * 1024
HBM_BYTES_PER_US = 2.0e6
STEP_US = 0.35
MXU_WIDTH = 256
MXU_FLOPS_PER_US = 0.65e6


_ANY = pl.BlockSpec(memory_space=pl.ANY)


def _pick(d, cands):
    for t in cands:
        if d % t == 0:
            return t
    return d


def _iota(shape, dim):
    return lax.broadcasted_iota(jnp.int32, shape, dim)


def _dg(a, b, ca, cb):
    return lax.dot_general(a.astype(BF16), b.astype(BF16), (((ca,), (cb,)), ((), ())),
                           preferred_element_type=F32)


@jax.custom_vjp
def _bdot_nn(a, b):
    return _dg(a, b, 1, 0)


def _nn_fwd(a, b):
    return _dg(a, b, 1, 0), (a, b)


def _nn_bwd(res, g):
    a, b = res
    return _dg(g, b, 1, 1), _dg(a, g, 0, 0)


_bdot_nn.defvjp(_nn_fwd, _nn_bwd)


@jax.custom_vjp
def _bdot_nt(a, b):
    return _dg(a, b, 1, 1)


def _nt_fwd(a, b):
    return _dg(a, b, 1, 1), (a, b)


def _nt_bwd(res, g):
    a, b = res
    return _dg(g, b, 1, 0), _dg(g, a, 0, 0)


_bdot_nt.defvjp(_nt_fwd, _nt_bwd)


@jax.custom_vjp
def _bdot_tn(a, b):
    return _dg(a, b, 0, 0)


def _tn_fwd(a, b):
    return _dg(a, b, 0, 0), (a, b)


def _tn_bwd(res, g):
    a, b = res
    return _dg(b, g, 1, 1), _dg(a, g, 1, 0)


_bdot_tn.defvjp(_tn_fwd, _tn_bwd)


def _fdot(a, b):
    return jnp.dot(a, b, preferred_element_type=F32, precision=lax.Precision.HIGHEST)


def _sigmoid(x):
    return 0.5 * jnp.tanh(0.5 * x) + 0.5


def _silu(x):
    return x * _sigmoid(x)


def _softplus(x):
    return jnp.maximum(x, 0.0) + jnp.log(1.0 + jnp.exp(-jnp.abs(x)))


def _lane_pick(m, h):
    return jnp.sum(jnp.where(_iota(m.shape, 1) == h, m, 0.0), axis=1, keepdims=True)


def _row_pick(m, h):
    return jnp.sum(jnp.where(_iota(m.shape, 0) == h, m, 0.0), axis=0, keepdims=True)


def _stack_rows(rows, n):
    c = rows[0].shape[1]
    r = _iota((n, c), 0)
    out = jnp.zeros((n, c), F32)
    for k, v in enumerate(rows):
        out = out + jnp.where(r == k, v, 0.0)
    return out


def _mm(a, b, *, ta=False, tb=False, add=None, out_dtype=F32, name, hook=None):
    if ta:
        K, M = a.shape
    else:
        M, K = a.shape
    if tb:
        N, Kb = b.shape
    else:
        Kb, N = b.shape
    assert K == Kb, (a.shape, b.shape, ta, tb)
    tm, tn, tk = _mm_tiles(M, N, K, a.dtype.itemsize, b.dtype.itemsize, jnp.dtype(out_dtype).itemsize,
                           0 if add is None else add.dtype.itemsize)
    ni, nj, nk = M // tm, N // tn, K // tk
    ca = 0 if ta else 1
    cb = 1 if tb else 0
    n_in = 2 if add is None else 3
    n_hin = 0 if hook is None else len(hook.inputs)
    n_hout = 0 if hook is None else len(hook.out_shapes)

    def body(*refs):
        a_ref, b_ref = refs[:2]
        add_ref = None if add is None else refs[2]
        o_ref = refs[n_in + n_hin]
        scr = refs[n_in + n_hin + 1 + n_hout:]
        acc_ref = scr[0] if nk > 1 else None
        hargs = (refs[n_in:n_in + n_hin], refs[n_in + n_hin + 1:n_in + n_hin + 1 + n_hout], scr[1 if nk > 1 else 0:])
        i, j, k = pl.program_id(0), pl.program_id(1), pl.program_id(2)
        if hook is not None:
            @pl.when((i == 0) & (j == 0) & (k == 0))
            def _():
                hook.start(*hargs)

        part = _dg(a_ref[...], b_ref[...], ca, cb)

        def finish(r):
            if add_ref is not None:
                r = r + add_ref[...].astype(F32)
            o_ref[...] = r.astype(o_ref.dtype)

        if nk == 1:
            finish(part)
        else:
            @pl.when(k == 0)
            def _():
                acc_ref[...] = part

            @pl.when((k > 0) & (k < nk - 1))
            def _():
                acc_ref[...] += part

            @pl.when(k == nk - 1)
            def _():
                finish(acc_ref[...] + part)

        if hook is not None:
            @pl.when((i == ni - 1) & (j == nj - 1) & (k == nk - 1))
            def _():
                hook.finish(*hargs)

    a_spec = pl.BlockSpec((tk, tm), lambda i, j, k: (k, i)) if ta else pl.BlockSpec((tm, tk), lambda i, j, k: (i, k))
    b_spec = pl.BlockSpec((tn, tk), lambda i, j, k: (j, k)) if tb else pl.BlockSpec((tk, tn), lambda i, j, k: (k, j))
    in_specs = [a_spec, b_spec]
    args = [a, b]
    if add is not None:
        in_specs.append(pl.BlockSpec((tm, tn), lambda i, j, k: (i, j)))
        args.append(add)
    out_specs = [pl.BlockSpec((tm, tn), lambda i, j, k: (i, j))]
    out_shape = [jax.ShapeDtypeStruct((M, N), out_dtype)]
    scratch = [pltpu.VMEM((tm, tn), F32)] if nk > 1 else []
    aliases = {}
    if hook is not None:
        in_specs += [_ANY] * n_hin
        args += list(hook.inputs)
        out_specs += [_ANY] * n_hout
        out_shape += list(hook.out_shapes)
        scratch += list(hook.scratch)
        aliases = {n_in + hi: 1 + ho for hi, ho in hook.aliases.items()}
    sem = ("parallel", "parallel", "arbitrary") if hook is None else ("arbitrary",) * 3
    res = pl.pallas_call(
        body, name=name, grid=(ni, nj, nk), in_specs=in_specs, out_specs=out_specs, out_shape=out_shape,
        scratch_shapes=scratch, input_output_aliases=aliases,
        compiler_params=pltpu.CompilerParams(dimension_semantics=sem),
    )(*args)
    if hook is not None:
        hook.done(res[1:])
    return res[0]


def _wide(v):
    return v.astype(F32) if v.dtype == BF16 else v


def _mmf(a, b, *, tb=False, add=None, pre=None, post=None, out_dtype=F32, name, tm, hook=None):
    a_list = list(a) if isinstance(a, (list, tuple)) else [a]
    b_list = list(b) if isinstance(b, (list, tuple)) else [b]
    assert len(a_list) == len(b_list) and (len(b_list) == 1 or not (tb or pre))
    b = b_list[0]
    if tb:
        N, K = b.shape
    else:
        K, N = b.shape
    M = pre[1][0].shape[0] if pre else a_list[0].shape[0]
    tn = N if post or N <= 1024 else _pick(N, (512, 256, LANES))
    ni, nj = M // tm, N // tn
    cb = 1 if tb else 0
    pre_fn, pre_rows, pre_consts = pre if pre else (None, [], [])
    post_fn, post_rows, post_consts, post_outs, post_accs = post if post else (None, [], [], [], [])
    hook_in = [] if hook is None else list(hook.inputs)
    hook_out = [] if hook is None else list(hook.out_shapes)

    def row_spec(arr):
        return pl.BlockSpec((tm, arr.shape[1]), lambda i, j: (i, 0))

    def const_spec(arr):
        return pl.BlockSpec(arr.shape, lambda i, j, nd=arr.ndim: (0,) * nd)

    args, in_specs = [], []
    for arr in (a_list if not pre else pre_rows):
        args.append(arr)
        in_specs.append(row_spec(arr))
    for arr in pre_consts:
        args.append(arr)
        in_specs.append(const_spec(arr))
    for arr in b_list:
        args.append(arr)
        in_specs.append(pl.BlockSpec((tn, K), lambda i, j: (j, 0)) if tb else
                        pl.BlockSpec((arr.shape[0], tn), lambda i, j: (0, j)))
    if add is not None:
        args.append(add)
        in_specs.append(pl.BlockSpec((tm, tn), lambda i, j: (i, j)))
    for arr in post_rows:
        args.append(arr)
        in_specs.append(row_spec(arr))
    for arr in post_consts:
        args.append(arr)
        in_specs.append(const_spec(arr))
    n_main = len(args)
    args += hook_in
    in_specs += [_ANY] * len(hook_in)

    out_shape, out_specs = [], []
    if post:
        for c, dt in post_outs:
            out_shape.append(jax.ShapeDtypeStruct((M, c), dt))
            out_specs.append(pl.BlockSpec((tm, c), lambda i, j: (i, 0)))
        for r, c in post_accs:
            out_shape.append(jax.ShapeDtypeStruct((r, c), F32))
            out_specs.append(pl.BlockSpec((r, c), lambda i, j: (0, 0)))
    else:
        out_shape.append(jax.ShapeDtypeStruct((M, N), out_dtype))
        out_specs.append(pl.BlockSpec((tm, tn), lambda i, j: (i, j)))
    if pre:
        out_shape.append(jax.ShapeDtypeStruct((M, K), BF16))
        out_specs.append(pl.BlockSpec((tm, K), lambda i, j: (i, 0)))
    n_out = len(out_shape)
    out_shape += hook_out
    out_specs += [_ANY] * len(hook_out)
    scratch = ([pltpu.VMEM((tm, K), BF16)] if pre else []) + ([] if hook is None else list(hook.scratch))
    aliases = {} if hook is None else {n_main + hi: n_out + ho for hi, ho in hook.aliases.items()}

    def body(*refs):
        ins, outs, scr = refs[:n_main], refs[len(args):len(args) + n_out], refs[len(args) + len(out_shape):]
        hargs = (refs[n_main:len(args)], refs[len(args) + n_out:len(args) + len(out_shape)], scr[1 if pre else 0:])
        i, j = pl.program_id(0), pl.program_id(1)
        if hook is not None:
            @pl.when((i == 0) & (j == 0))
            def _():
                hook.start(*hargs)

        it = iter(ins)
        if pre:
            rows_ = [next(it) for _ in pre_rows]
            consts_ = [next(it) for _ in pre_consts]

            @pl.when(j == 0)
            def _():
                av = pre_fn(*[_wide(r[...]) for r in rows_], *[_wide(r[...]) for r in consts_]).astype(BF16)
                scr[0][...] = av
                outs[-1][...] = av

            ats = [scr[0][...]]
        else:
            ats = [next(it)[...] for _ in a_list]
        p = None
        for at in ats:
            part = _dg(at, next(it)[...], 1, cb)
            p = part if p is None else p + part
        if add is not None:
            p = p + next(it)[...].astype(F32)
        if post:
            rows_ = [next(it) for _ in post_rows]
            consts_ = [next(it) for _ in post_consts]
            res = post_fn(p, *[_wide(r[...]) for r in rows_], *[_wide(r[...]) for r in consts_])
            for r, v in zip(outs[:len(post_outs)], res[:len(post_outs)]):
                r[...] = v.astype(r.dtype)
            for r, v in zip(outs[len(post_outs):], res[len(post_outs):]):
                @pl.when(i == 0)
                def _(r=r, v=v):
                    r[...] = v

                @pl.when(i > 0)
                def _(r=r, v=v):
                    r[...] += v
        else:
            outs[0][...] = p.astype(outs[0].dtype)
        if hook is not None:
            @pl.when((i == ni - 1) & (j == nj - 1))
            def _():
                hook.finish(*hargs)

    res = pl.pallas_call(
        body, name=name, grid=(ni, nj), in_specs=in_specs, out_specs=out_specs, out_shape=out_shape,
        scratch_shapes=scratch, input_output_aliases=aliases,
        compiler_params=pltpu.CompilerParams(dimension_semantics=("arbitrary", "arbitrary")),
    )(*args)
    if hook is not None:
        hook.done(res[n_out:])
    return res[:n_out]


def _mm_tiles(M, N, K, sa, sb, so, sadd):
    def tiles(d):
        return [t for t in range(LANES, min(d, 2048) + 1, LANES) if d % t == 0] or [d]

    best = None
    for tk in [K] + [t for t in tiles(K) if t < K]:
        for tm in tiles(M):
            for tn in tiles(N):
                vmem = 2 * (tm * tk * sa + tk * tn * sb + tm * tn * (so + sadd)) + (tm * tn * 4 if tk < K else 0)
                if vmem > MM_VMEM_BYTES or tm * tn > MM_MAX_OUT_TILE:
                    continue
                a_reads = 1 if tk == K else N // tn
                traffic = M * K * sa * a_reads + K * N * sb * (M // tm) + M * N * (so + sadd)
                steps = (M // tm) * (N // tn) * (K // tk)
                width = -(-tn // MXU_WIDTH) * MXU_WIDTH
                mxu = 2.0 * M * K * N * (width / tn) / MXU_FLOPS_PER_US
                edge = tm * tk * sa + tk * tn * sb + tm * tn * (so + sadd)
                cost = max(traffic / HBM_BYTES_PER_US, mxu) + steps * STEP_US + edge / HBM_BYTES_PER_US
                if best is None or cost < best[0]:
                    best = (cost, tm, tn, tk)
    assert best is not None, (M, N, K)
    return best[1:]


class _Hook:
    def __init__(self, inputs, out_shapes, aliases, scratch, start, finish, done):
        self.inputs, self.out_shapes, self.aliases, self.scratch = inputs, out_shapes, aliases, scratch
        self.start, self.finish, self.done = start, finish, done


class _Ctx:
    def __init__(self, first, last, row0, rows):
        self.first, self.last, self.row0, self.rows = first, last, row0, rows


def _rows(name, fn, ins, outs, accs=(), *, tm, nrows, ncol=1):
    nt = nrows // tm
    hb = tm // HALO
    nh = nrows // HALO
    ins = [(kind, arr, arr.shape[1] if kind == "row" and cw is None else cw, base) for kind, arr, cw, base in ins]
    in_specs, args = [], []
    for kind, arr, cw, base in ins:
        if kind == "row":
            in_specs.append(pl.BlockSpec((tm, cw), lambda j, i, base=base: (i, base + j)))
        elif kind == "prev":
            in_specs.append(pl.BlockSpec((HALO, cw), lambda j, i, base=base: (jnp.maximum(i * hb - 1, 0), base + j)))
        elif kind == "next":
            in_specs.append(pl.BlockSpec((HALO, cw), lambda j, i, base=base: (jnp.minimum((i + 1) * hb, nh - 1), base + j)))
        elif kind in ("const", "raw"):
            in_specs.append(pl.BlockSpec(arr.shape, lambda j, i, nd=arr.ndim: (0,) * nd))
        elif kind == "ccol":
            in_specs.append(pl.BlockSpec((arr.shape[0], cw), lambda j, i, base=base: (0, base + j)))
        else:
            raise ValueError(kind)
        args.append(arr)
    out_specs, out_shape = [], []
    for ctot, cw, base, dt in outs:
        out_specs.append(pl.BlockSpec((tm, cw), lambda j, i, base=base: (i, base + j)))
        out_shape.append(jax.ShapeDtypeStruct((nrows, ctot), dt))
    for r, ctot, cw in accs:
        out_specs.append(pl.BlockSpec((r, cw), lambda j, i: (0, j)))
        out_shape.append(jax.ShapeDtypeStruct((r, ctot), F32))
    n_in, n_out = len(ins), len(outs)

    def body(*refs):
        i = pl.program_id(1)
        in_refs, out_refs, acc_refs = refs[:n_in], refs[n_in:n_in + n_out], refs[n_in + n_out:]
        if acc_refs:
            @pl.when(i == 0)
            def _():
                for r in acc_refs:
                    r[...] = jnp.zeros_like(r)

        vals = [r[...] if s[0] == "raw" else _wide(r[...]) for r, s in zip(in_refs, ins)]
        res = fn(_Ctx(i == 0, i == nt - 1, i * tm, tm), *vals)
        for r, v in zip(out_refs, res[:n_out]):
            r[...] = v.astype(r.dtype)
        for r, v in zip(acc_refs, res[n_out:]):
            r[...] += v

    res = pl.pallas_call(
        body, name=name, grid=(ncol, nt), in_specs=in_specs, out_specs=out_specs, out_shape=out_shape,
        compiler_params=pltpu.CompilerParams(dimension_semantics=("arbitrary", "arbitrary")),
    )(*args)
    return res


def _shift_down(xcat, k):
    return xcat if k == 0 else pltpu.roll(xcat, k, 0)


def _shift_up(xcat, k):
    return xcat if k == 0 else pltpu.roll(xcat, xcat.shape[0] - k, 0)


def _with_prev(ctx, halo, x):
    return jnp.concatenate([jnp.where(ctx.first, 0.0, halo), x], axis=0)


def _with_next(ctx, x, halo):
    return jnp.concatenate([x, jnp.where(ctx.last, 0.0, halo)], axis=0)


def _rms_core(x, g):
    r = lax.rsqrt(jnp.mean(x * x, axis=-1, keepdims=True) + EPS)
    return x * r * g


def _rms_post(du, xv, drv, gv):
    _, vjp = jax.vjp(_rms_core, xv, gv)
    dx, dg = vjp(du)
    return [drv + dx, drv + dx, dg]


RMS_POST_OUTS = [(D, F32), (D, BF16)]


def _final_loss(x, target, g):
    S = x.shape[0]

    def fn(ctx, xv, tv, gv):
        def f(xx, gg):
            err = _rms_core(xx, gg) - tv
            return 0.5 * jnp.sum(err * err) / D

        loss, vjp = jax.vjp(f, xv, gv)
        dx, dg = vjp(jnp.ones((), F32))
        return [dx, dx, dg, jnp.zeros((1, LANES), F32) + loss]

    return _rows("final_loss", fn, [("row", x, None, 0), ("row", target, None, 0), ("const", g, None, 0)],
                 [(D, D, 0, F32), (D, D, 0, BF16)], [(1, D, D), (1, LANES, LANES)], tm=256, nrows=S)


def _attn_valid(n):
    qi = _iota((WIN, 2 * WIN), 0)
    kk = _iota((WIN, 2 * WIN), 1)
    rel = qi + WIN - kk
    return (rel >= 0) & (rel <= WIN) & ((kk >= WIN) | (n > 0))


def _attn_block(q, kp, kc, vp, vc, b0, b1):
    k = jnp.concatenate([kp, kc], axis=0)
    v = jnp.concatenate([vp, vc], axis=0)
    lo = _iota((WIN, LANES), 1) < HD
    scale = 1.0 / math.sqrt(HD)
    os_, ls_ = [], []
    for hh, b in ((0, b0), (1, b1)):
        qm = jnp.where(lo if hh == 0 else ~lo, q, 0.0)
        s = _bdot_nt(qm, k) * scale + b
        m = lax.stop_gradient(jnp.max(s, axis=1, keepdims=True))
        p = jnp.exp(s - m)
        l = jnp.sum(p, axis=1, keepdims=True)
        os_.append(_bdot_nn(p, v) / l)
        ls_.append(m + jnp.log(l))
    return jnp.where(lo, os_[0], os_[1]), jnp.where(lo, ls_[0], ls_[1])


def _residue_rows(r, d):
    return pl.ds(0, WIN) if d == 1 else pl.ds(r, WIN, stride=d)


def _for_residues(d, fn):
    if d == 1:
        fn(0, 0)
    else:
        lax.fori_loop(0, d, fn, 0, unroll=min(d, 8))


def _pairs_per_step(d):
    return 3 if d == 1 else 1


def _bias_table(rel_bias, bucket, gi, name):
    def body(t_ref, b_ref, o_ref):
        h = 6 * gi + pl.program_id(0)
        b = b_ref[...]
        acc = jnp.zeros(b.shape, F32)
        for k in range(REL_BUCKETS):
            acc = jnp.where(b == k, t_ref[k, h], acc)
        o_ref[0] = acc

    return pl.pallas_call(
        body, name=name, grid=(6,),
        in_specs=[pl.BlockSpec(memory_space=pltpu.SMEM), pl.BlockSpec((WIN, 2 * WIN), lambda h: (0, 0))],
        out_specs=pl.BlockSpec((1, WIN, 2 * WIN), lambda h: (h, 0, 0)),
        out_shape=jax.ShapeDtypeStruct((6, WIN, 2 * WIN), F32),
    )(rel_bias, bucket)


def _attn_fwd(pa, bias, gi, name):
    S = pa.shape[0]
    d = DILATIONS[gi]
    bt = WIN * d
    nb = S // bt
    hpw = _pairs_per_step(d)
    bw = hpw * LANES
    cb = 3 * gi // hpw

    def body(q_ref, kp_ref, kc_ref, vp_ref, vc_ref, b_ref, o_ref, l_ref):
        valid = _attn_valid(pl.program_id(1))
        bm = [jnp.where(valid, b_ref[k], NEG) for k in range(2 * hpw)]

        def residue(r, carry):
            sl = _residue_rows(r, d)
            for t in range(hpw):
                ln = pl.ds(t * LANES, LANES)
                o, lse = _attn_block(q_ref[sl, ln], kp_ref[sl, ln], kc_ref[sl, ln], vp_ref[sl, ln], vc_ref[sl, ln],
                                     bm[2 * t], bm[2 * t + 1])
                o_ref[sl, ln] = o
                l_ref[sl, ln] = lse
            return carry

        _for_residues(d, residue)

    def spec(off, prev):
        if prev:
            return pl.BlockSpec((bt, bw), lambda hp, n: (jnp.maximum(n - 1, 0), off // hpw + cb + hp))
        return pl.BlockSpec((bt, bw), lambda hp, n: (n, off // hpw + cb + hp))

    ospec = pl.BlockSpec((bt, bw), lambda hp, n: (n, hp))
    return pl.pallas_call(
        body, name=name, grid=(3 // hpw, nb),
        in_specs=[spec(0, False), spec(9, True), spec(9, False), spec(18, True), spec(18, False),
                  pl.BlockSpec((2 * hpw, WIN, 2 * WIN), lambda hp, n: (hp, 0, 0))],
        out_specs=[ospec, ospec],
        out_shape=[jax.ShapeDtypeStruct((S, GW), F32)] * 2,
        compiler_params=pltpu.CompilerParams(dimension_semantics=("parallel", "arbitrary")),
    )(pa, pa, pa, pa, pa, bias)


def _attn_bwd(pa, bias, do, dlse, db_in, dqkv, gi, name):
    S = pa.shape[0]
    d = DILATIONS[gi]
    bt = WIN * d
    nb = S // bt
    hpw = _pairs_per_step(d)
    bw = hpw * LANES
    cb = 3 * gi // hpw

    def body(q_ref, kp_ref, kc_ref, vp_ref, vc_ref, b_ref, do_ref, dl_ref, dbi_ref, dqi_ref, dki_ref, dvi_ref,
             dq_ref, dk_ref, dv_ref, db_ref, ck, cv):
        n = pl.program_id(1)

        @pl.when(n == 0)
        def _():
            db_ref[...] = dbi_ref[...]
            ck[...] = jnp.zeros_like(ck)
            cv[...] = jnp.zeros_like(cv)

        @pl.when(n < nb)
        def _():
            valid = _attn_valid(n)
            bm = [jnp.where(valid, b_ref[k], NEG) for k in range(2 * hpw)]

            def residue(r, carry):
                sl = _residue_rows(r, d)
                cs = pl.ds(pl.multiple_of(r * WIN, WIN), WIN)
                for t in range(hpw):
                    ln = pl.ds(t * LANES, LANES)
                    _, vjp = jax.vjp(_attn_block, q_ref[sl, ln], kp_ref[sl, ln], kc_ref[sl, ln], vp_ref[sl, ln],
                                     vc_ref[sl, ln], bm[2 * t], bm[2 * t + 1])
                    dq, dkp, dkc, dvp, dvc, db0, db1 = vjp((do_ref[sl, ln], dl_ref[sl, ln]))
                    dq_ref[sl, ln] = dq
                    dk_ref[sl, ln] = ck[cs, ln] + dkp
                    dv_ref[sl, ln] = cv[cs, ln] + dvp
                    ck[cs, ln] = dkc
                    cv[cs, ln] = dvc
                    db_ref[2 * t] += db0
                    db_ref[2 * t + 1] += db1
                return carry

            _for_residues(d, residue)

        @pl.when(n == nb)
        def _():
            def residue(r, carry):
                sl = _residue_rows(r, d)
                cs = pl.ds(pl.multiple_of(r * WIN, WIN), WIN)
                dk_ref[sl, :] = ck[cs, :]
                dv_ref[sl, :] = cv[cs, :]
                return carry

            _for_residues(d, residue)

    def cur(n):
        return jnp.minimum(n, nb - 1)

    def spec(off, prev):
        if prev:
            return pl.BlockSpec((bt, bw), lambda hp, n: (jnp.maximum(cur(n) - 1, 0), off // hpw + cb + hp))
        return pl.BlockSpec((bt, bw), lambda hp, n: (cur(n), off // hpw + cb + hp))

    gspec = pl.BlockSpec((bt, bw), lambda hp, n: (cur(n), hp))
    bspec = pl.BlockSpec((2 * hpw, WIN, 2 * WIN), lambda hp, n: (hp, 0, 0))
    qspec = pl.BlockSpec((bt, bw), lambda hp, n: (cur(n), cb + hp))
    kspec = pl.BlockSpec((bt, bw), lambda hp, n: (jnp.maximum(n - 1, 0), cb + hp))
    dq, dk, dv, db = pl.pallas_call(
        body, name=name, grid=(3 // hpw, nb + 1),
        in_specs=[spec(0, False), spec(9, True), spec(9, False), spec(18, True), spec(18, False),
                  bspec, gspec, gspec, bspec, _ANY, _ANY, _ANY],
        out_specs=[qspec, kspec, kspec, bspec],
        out_shape=[jax.ShapeDtypeStruct((S, AW), F32)] * 3 + [jax.ShapeDtypeStruct((6, WIN, 2 * WIN), F32)],
        scratch_shapes=[pltpu.VMEM((bt, bw), F32), pltpu.VMEM((bt, bw), F32)],
        input_output_aliases={9: 0, 10: 1, 11: 2},
        compiler_params=pltpu.CompilerParams(dimension_semantics=("arbitrary", "arbitrary")),
    )(pa, pa, pa, pa, pa, bias, do, dlse, db_in, *dqkv)
    return (dq, dk, dv), db


def _mix_core(o0, o1, o2, l0, l1, l2):
    m = lax.stop_gradient(jnp.maximum(jnp.maximum(l0, l1), l2))
    e0, e1, e2 = jnp.exp(l0 - m), jnp.exp(l1 - m), jnp.exp(l2 - m)
    return (e0 * o0 + e1 * o1 + e2 * o2) / (e0 + e1 + e2)


def _mix_fwd(os_, ls_, name):
    S = os_[0].shape[0]
    ins = [("row", a, None, 0) for a in (*os_, *ls_)]
    return _rows(name, lambda ctx, *v: [_mix_core(*v)], ins, [(GW, GW, 0, BF16)], tm=512, nrows=S)[0]


def _mix_bwd(os_, ls_, datt, name):
    S = datt.shape[0]

    def fn(ctx, *v):
        _, vjp = jax.vjp(_mix_core, *v[:6])
        return list(vjp(v[6]))

    ins = [("row", a, None, 0) for a in (*os_, *ls_, datt)]
    outs = [(GW, GW, 0, F32)] * 6
    r = _rows(name, fn, ins, outs, tm=512, nrows=S)
    return r[:3], r[3:]


def _t5_bucket(dist):
    max_exact = REL_BUCKETS // 2
    is_small = dist < max_exact
    nf = jnp.maximum(dist, 1).astype(F32)
    large = max_exact + (jnp.log(nf / max_exact) / math.log(REL_MAX_DISTANCE / max_exact)
                         * (REL_BUCKETS - max_exact)).astype(jnp.int32)
    large = jnp.minimum(large, REL_BUCKETS - 1)
    return jnp.where(is_small, dist, large)


def _buckets(d):
    qi = jnp.arange(WIN)[:, None]
    kk = jnp.arange(2 * WIN)[None, :]
    rel = qi + WIN - kk
    return _t5_bucket(jnp.clip(rel, 0, None) * d)


def _pool_cnt(ctx, w):
    pos = ctx.row0 + _iota((ctx.rows, PG), 0) + 1
    return jnp.minimum(pos, w).astype(F32)


def _pool_d(ctx, halo, u):
    ds = []
    for g, w in enumerate(POOL_WINDOWS):
        ug = u[:, g * PG:(g + 1) * PG]
        s = _with_prev(ctx, halo[:, g * PG:(g + 1) * PG], ug)
        step = 1
        while step < w:
            s = s + _shift_down(s, step)
            step *= 2
        ds.append(s[HALO:] / _pool_cnt(ctx, w) - ug)
    return ds


def _pool_fwd(pb, pw, scale, name):
    S = pb.shape[0]

    def fn(ctx, halo, u, w, sc):
        ds = _pool_d(ctx, halo, u)
        return [jnp.concatenate([_dg(ds[k], w[k], 1, 0) for k in range(4)], axis=1) * sc]

    return _rows(name, fn, [("prev", pb, D, 0), ("row", pb, None, 0), ("raw", pw, None, 0), ("const", scale, None, 0)],
                 [(D, D, 0, BF16)], tm=256, nrows=S)[0]


def _pool_bwd(pb, pw, scale, dpo, name):
    S = pb.shape[0]

    def fn1(ctx, halo, u, w, sc, dy):
        ds = _pool_d(ctx, halo, u)
        dyp = dy * sc
        y = jnp.concatenate([_dg(ds[k], w[k], 1, 0) for k in range(4)], axis=1)
        es, dws = [], []
        for k, wd in enumerate(POOL_WINDOWS):
            cols = slice(k * PG, (k + 1) * PG)
            es.append(_dg(dyp[:, cols], w[k], 1, 1) / _pool_cnt(ctx, wd))
            dws.append(_dg(ds[k], dyp[:, cols], 0, 0))
        return [jnp.concatenate(es, axis=1), jnp.concatenate(dws, axis=0), jnp.sum(dy * y, axis=0, keepdims=True)]

    e, dpw, dsc = _rows(name + "_a", fn1,
                        [("prev", pb, D, 0), ("row", pb, None, 0), ("raw", pw, None, 0), ("const", scale, None, 0),
                         ("row", dpo, None, 0)],
                        [(D, D, 0, F32)], [(4 * PG, PG, PG), (1, D, D)], tm=256, nrows=S)

    def fn2(ctx, ev, halo):
        outs = []
        for g, w in enumerate(POOL_WINDOWS):
            eg = ev[:, g * PG:(g + 1) * PG]
            s = _with_next(ctx, eg, halo[:, g * PG:(g + 1) * PG])
            step = 1
            while step < w:
                s = s + _shift_up(s, step)
                step *= 2
            outs.append(s[:ctx.rows] - eg * _pool_cnt(ctx, w))
        return [jnp.concatenate(outs, axis=1)]

    du = _rows(name + "_b", fn2, [("row", e, None, 0), ("next", e, D, 0)], [(D, D, 0, BF16)], tm=256, nrows=S)[0]
    return du, dpw, dsc


def _conv_taps(ctx, halo, x, K):
    cat = _with_prev(ctx, halo, x)
    return [_shift_down(cat, K - 1 - k)[HALO:] for k in range(K)]


def _conv_pre(taps, w, b):
    acc = b
    for k, t in enumerate(taps):
        acc = acc + t * _row_pick(w, k)
    return acc


CW = 256
CWS = 512
CONV_BWD_TILE = 256 * 1024
CONV_FWD_TILE = 512 * 1024


def _ext_taps(ctx, prev, x, nxt, K):
    cat = jnp.concatenate([jnp.where(ctx.first, 0.0, prev), x, jnp.where(ctx.last, 0.0, nxt)], axis=0)
    return [_shift_down(cat, K - 1 - k)[HALO:] for k in range(K)]


def _conv_t_rows(dp, w, K, tm):
    acc = jnp.zeros((tm, dp.shape[1]), F32)
    for k in range(K):
        acc = acc + _shift_up(dp, K - 1 - k)[:tm] * _row_pick(w, k)
    return acc


def _ssd_conv_fwd(pc, w, b, name):
    S = pc.shape[0]
    base = D // CWS

    def fn(ctx, halo, x, wv, bv):
        return [_silu(_conv_pre(_conv_taps(ctx, halo, x, 4), wv, bv))]

    return _rows(name, fn, [("prev", pc, CWS, base), ("row", pc, CWS, base), ("ccol", w, CWS, 0), ("ccol", b, CWS, 0)],
                 [(XBC, CWS, 0, F32)], tm=min(S, CONV_FWD_TILE // CWS), nrows=S, ncol=XBC // CWS)[0]


def _ssd_conv_bwd(pc, w, b, dy, name):
    S = pc.shape[0]
    base = D // CWS

    def fn(ctx, prev, x, nxt, wv, bv, dyv, dyn):
        n = ctx.rows
        taps = _ext_taps(ctx, prev, x, nxt, 4)
        pre = _conv_pre(taps, wv, bv)
        sg = _sigmoid(pre)
        dye = jnp.concatenate([dyv, jnp.where(ctx.last, 0.0, dyn)], axis=0)
        dpre = dye * sg * (1.0 + pre * (1.0 - sg))
        dw = _stack_rows([jnp.sum(dpre[:n] * t[:n], axis=0, keepdims=True) for t in taps], 4)
        return [_conv_t_rows(dpre, wv, 4, n), dw, jnp.sum(dpre[:n], axis=0, keepdims=True)]

    return _rows(name, fn,
                 [("prev", pc, CWS, base), ("row", pc, CWS, base), ("next", pc, CWS, base), ("ccol", w, CWS, 0),
                  ("ccol", b, CWS, 0), ("row", dy, CWS, 0), ("next", dy, CWS, 0)],
                 [(XBC, CWS, 0, BF16)], [(4, XBC, CWS), (1, XBC, CWS)], tm=min(S, CONV_BWD_TILE // CWS), nrows=S,
                 ncol=XBC // CWS)


NFC = D_FF // CW


def _ffn_act_fwd(h, w, b, name):
    S = h.shape[0]

    def fn(ctx, ha, a, hv, v, wa, wv, ba, bv):
        pa = _conv_pre(_conv_taps(ctx, ha, a, 3), wa, ba)
        pv = _conv_pre(_conv_taps(ctx, hv, v, 3), wv, bv)
        return [_silu(pa) * pv]

    return _rows(name, fn,
                 [("prev", h, CW, 0), ("row", h, CW, 0), ("prev", h, CW, NFC), ("row", h, CW, NFC),
                  ("ccol", w, CW, 0), ("ccol", w, CW, NFC), ("ccol", b, CW, 0), ("ccol", b, CW, NFC)],
                 [(D_FF, CW, 0, BF16)], tm=min(S, CONV_FWD_TILE // CW), nrows=S, ncol=NFC)[0]


def _ffn_act_bwd(h, w, b, df, name):
    S = h.shape[0]

    def fn(ctx, pa_, a, na, pv_, v, nv, wa, wv, ba, bv, dfv, dfn):
        n = ctx.rows
        ta = _ext_taps(ctx, pa_, a, na, 3)
        tv = _ext_taps(ctx, pv_, v, nv, 3)
        pa = _conv_pre(ta, wa, ba)
        pv = _conv_pre(tv, wv, bv)
        sg = _sigmoid(pa)
        dfe = jnp.concatenate([dfv, jnp.where(ctx.last, 0.0, dfn)], axis=0)
        dpa = dfe * pv * sg * (1.0 + pa * (1.0 - sg))
        dpv = dfe * pa * sg
        res = [_conv_t_rows(dpa, wa, 3, n), _conv_t_rows(dpv, wv, 3, n)]
        for dp, taps in ((dpa, ta), (dpv, tv)):
            res.append(_stack_rows([jnp.sum(dp[:n] * t[:n], axis=0, keepdims=True) for t in taps], 3))
        for dp in (dpa, dpv):
            res.append(jnp.sum(dp[:n], axis=0, keepdims=True))
        return res

    ins = []
    for base in (0, NFC):
        ins += [("prev", h, CW, base), ("row", h, CW, base), ("next", h, CW, base)]
    ins += [("ccol", w, CW, 0), ("ccol", w, CW, NFC), ("ccol", b, CW, 0), ("ccol", b, CW, NFC),
            ("row", df, CW, 0), ("next", df, CW, 0)]
    dha, dhv, dwa, dwv, dba, dbv = _rows(
        name, fn, ins, [(D_FF, CW, 0, BF16)] * 2, [(3, D_FF, CW)] * 2 + [(1, D_FF, CW)] * 2,
        tm=min(S, CONV_BWD_TILE // CW), nrows=S, ncol=NFC)
    return dha, dhv, jnp.concatenate([dwa, dwv], axis=1), jnp.concatenate([dba, dbv], axis=1)


NSLAB = D // LANES
CPS = 2


def _ssd_chunk(xs, Bs, Cs, dtraw, dtb, alog, prev):
    lsz = SSD_CHUNK
    lane = _iota((lsz, LANES), 1)
    row = _iota((lsz, LANES), 0)
    dt = jnp.where(lane < SSD_HEADS, _softplus(dtraw + dtb), 0.0)
    a = dt * (-jnp.exp(alog))
    tril = row >= lane
    a_cs = _fdot(tril.astype(F32), a)
    a_cst = a_cs.T
    a_last = jnp.sum(a, axis=0, keepdims=True)
    lo = lane < HD
    top = row < HD
    cbs = [_bdot_nt(Cs[g], Bs[g]) for g in range(2)]
    ys, news = [], []
    for s in range(NSLAB):
        g = s // (NSLAB // 2)
        cols, lms, dts, als = [], [], [], []
        for hh in range(2):
            h = 2 * s + hh
            col = _lane_pick(a_cs, h)
            seg = col - _row_pick(a_cst, h)
            lms.append(jnp.exp(jnp.where(tril, seg, NEG)))
            cols.append(col)
            dts.append(_lane_pick(dt, h))
            als.append(_lane_pick(a_last, h))
        col_x = jnp.where(lo, cols[0], cols[1])
        al_x = jnp.where(lo, als[0], als[1])
        xc = xs[s] * jnp.where(lo, dts[0], dts[1])
        yd = jnp.where(lo, _bdot_nn(cbs[g] * lms[0], xc), _bdot_nn(cbs[g] * lms[1], xc))
        yoff = _bdot_nt(Cs[g], prev[s]) * jnp.exp(col_x)
        ys.append(yd + yoff)
        st = _bdot_tn(xc * jnp.exp(al_x - col_x), Bs[g])
        news.append(prev[s] * jnp.exp(jnp.where(top, als[0], als[1])) + st)
    return ys, news


def _ssd_scan_fwd(xbc_c, pd, dtb, alog, name):
    S = xbc_c.shape[0]
    nc = S // SSD_CHUNK
    rows_ = CPS * SSD_CHUNK

    def body(x_ref, b_ref, c_ref, dt_ref, dtb_ref, al_ref, y_ref, st_ref, state):
        c = pl.program_id(0)

        @pl.when(c == 0)
        def _():
            state[...] = jnp.zeros_like(state)

        prev = [state[s * LANES:(s + 1) * LANES, :] for s in range(NSLAB)]
        for u in range(CPS):
            rw = pl.ds(u * SSD_CHUNK, SSD_CHUNK)
            xs = [x_ref[rw, s * LANES:(s + 1) * LANES] for s in range(NSLAB)]
            Bs = [b_ref[rw, g * SSD_N:(g + 1) * SSD_N] for g in range(2)]
            Cs = [c_ref[rw, g * SSD_N:(g + 1) * SSD_N] for g in range(2)]
            for s in range(NSLAB):
                st_ref[u, s * LANES:(s + 1) * LANES, :] = prev[s]
            ys, prev = _ssd_chunk(xs, Bs, Cs, dt_ref[rw, :].astype(F32), dtb_ref[...], al_ref[...], prev)
            for s in range(NSLAB):
                y_ref[rw, s * LANES:(s + 1) * LANES] = ys[s]
        for s in range(NSLAB):
            state[s * LANES:(s + 1) * LANES, :] = prev[s]

    return pl.pallas_call(
        body, name=name, grid=(nc // CPS,),
        in_specs=[pl.BlockSpec((rows_, D), lambda c: (c, 0)),
                  pl.BlockSpec((rows_, 2 * SSD_N), lambda c: (c, D // (2 * SSD_N))),
                  pl.BlockSpec((rows_, 2 * SSD_N), lambda c: (c, D // (2 * SSD_N) + 1)),
                  pl.BlockSpec((rows_, LANES), lambda c: (c, 0)),
                  pl.BlockSpec((1, LANES), lambda c: (0, 0)), pl.BlockSpec((1, LANES), lambda c: (0, 0))],
        out_specs=[pl.BlockSpec((rows_, D), lambda c: (c, 0)), pl.BlockSpec((CPS, D, SSD_N), lambda c: (c, 0, 0))],
        out_shape=[jax.ShapeDtypeStruct((S, D), F32), jax.ShapeDtypeStruct((nc, D, SSD_N), F32)],
        scratch_shapes=[pltpu.VMEM((D, SSD_N), F32)],
        compiler_params=pltpu.CompilerParams(dimension_semantics=("arbitrary",)),
    )(xbc_c, xbc_c, xbc_c, pd, dtb, alog)


def _ssd_scan_bwd(xbc_c, pd, dtb, alog, states, dy, dxs_skip, name):
    S = xbc_c.shape[0]
    nc = S // SSD_CHUNK
    rows_ = CPS * SSD_CHUNK

    def body(x_ref, b_ref, c_ref, dt_ref, dtb_ref, al_ref, st_ref, dy_ref, sk_ref,
             dx_ref, ddt_ref, ddtb_ref, dal_ref, dstate):
        c = pl.program_id(0)

        @pl.when(c == 0)
        def _():
            dstate[...] = jnp.zeros_like(dstate)
            ddtb_ref[...] = jnp.zeros_like(ddtb_ref)
            dal_ref[...] = jnp.zeros_like(dal_ref)

        dnew = [dstate[s * LANES:(s + 1) * LANES, :] for s in range(NSLAB)]
        for u in reversed(range(CPS)):
            rw = pl.ds(u * SSD_CHUNK, SSD_CHUNK)
            xs = [x_ref[rw, s * LANES:(s + 1) * LANES] for s in range(NSLAB)]
            Bs = [b_ref[rw, g * SSD_N:(g + 1) * SSD_N] for g in range(2)]
            Cs = [c_ref[rw, g * SSD_N:(g + 1) * SSD_N] for g in range(2)]
            prev = [st_ref[u, s * LANES:(s + 1) * LANES, :] for s in range(NSLAB)]
            _, vjp = jax.vjp(_ssd_chunk, xs, Bs, Cs, dt_ref[rw, :].astype(F32), dtb_ref[...], al_ref[...], prev)
            dys = [dy_ref[rw, s * LANES:(s + 1) * LANES] for s in range(NSLAB)]
            dxs, dBs, dCs, ddt, ddtb, dal, dnew = vjp((dys, dnew))
            for s in range(NSLAB):
                dx_ref[rw, s * LANES:(s + 1) * LANES] = dxs[s] + sk_ref[rw, s * LANES:(s + 1) * LANES]
            for g in range(2):
                dx_ref[rw, D + g * SSD_N:D + (g + 1) * SSD_N] = dBs[g]
                dx_ref[rw, D + 2 * SSD_N + g * SSD_N:D + 2 * SSD_N + (g + 1) * SSD_N] = dCs[g]
            ddt_ref[rw, :] = ddt
            ddtb_ref[...] += ddtb
            dal_ref[...] += dal
        for s in range(NSLAB):
            dstate[s * LANES:(s + 1) * LANES, :] = dnew[s]

    def rv(c):
        return nc // CPS - 1 - c

    return pl.pallas_call(
        body, name=name, grid=(nc // CPS,),
        in_specs=[pl.BlockSpec((rows_, D), lambda c: (rv(c), 0)),
                  pl.BlockSpec((rows_, 2 * SSD_N), lambda c: (rv(c), D // (2 * SSD_N))),
                  pl.BlockSpec((rows_, 2 * SSD_N), lambda c: (rv(c), D // (2 * SSD_N) + 1)),
                  pl.BlockSpec((rows_, LANES), lambda c: (rv(c), 0)),
                  pl.BlockSpec((1, LANES), lambda c: (0, 0)), pl.BlockSpec((1, LANES), lambda c: (0, 0)),
                  pl.BlockSpec((CPS, D, SSD_N), lambda c: (rv(c), 0, 0)),
                  pl.BlockSpec((rows_, D), lambda c: (rv(c), 0)),
                  pl.BlockSpec((rows_, D), lambda c: (rv(c), 0))],
        out_specs=[pl.BlockSpec((rows_, XBC), lambda c: (rv(c), 0)),
                   pl.BlockSpec((rows_, LANES), lambda c: (rv(c), 0)),
                   pl.BlockSpec((1, LANES), lambda c: (0, 0)), pl.BlockSpec((1, LANES), lambda c: (0, 0))],
        out_shape=[jax.ShapeDtypeStruct((S, XBC), F32), jax.ShapeDtypeStruct((S, LANES), F32),
                   jax.ShapeDtypeStruct((1, LANES), F32), jax.ShapeDtypeStruct((1, LANES), F32)],
        scratch_shapes=[pltpu.VMEM((D, SSD_N), F32)],
        compiler_params=pltpu.CompilerParams(dimension_semantics=("arbitrary",)),
    )(xbc_c, xbc_c, xbc_c, pd, dtb, alog, states, dy, dxs_skip)


def _ssd_post_core(y, xs, z, d128, nw):
    tm = y.shape[0]
    ex = (_iota((LANES, D), 1) // HD == _iota((LANES, D), 0)).astype(F32)
    d_x = jnp.sum(_fdot(jnp.broadcast_to(d128, (8, LANES)), ex), axis=0, keepdims=True) * 0.125
    y2 = (y + d_x * xs) * _silu(z)
    lo = _iota((tm, D), 1) < D // 2
    sq = y2 * y2
    ms0 = jnp.sum(jnp.where(lo, sq, 0.0), axis=-1, keepdims=True) / (D // 2)
    ms1 = jnp.sum(jnp.where(lo, 0.0, sq), axis=-1, keepdims=True) / (D // 2)
    r = jnp.where(lo, lax.rsqrt(ms0 + EPS), lax.rsqrt(ms1 + EPS))
    return y2 * r * nw


def _ssd_post_ins(y, xbc_c, pc, d128, nw):
    return [("row", y, None, 0), ("row", xbc_c, D, 0), ("row", pc, D, 0), ("const", d128, None, 0), ("const", nw, None, 0)]


def _ssd_post_fwd(y, xbc_c, pc, d128, nw, name):
    S = y.shape[0]
    return _rows(name, lambda ctx, *v: [_ssd_post_core(*v)], _ssd_post_ins(y, xbc_c, pc, d128, nw),
                 [(D, D, 0, BF16)], tm=256, nrows=S)[0]


def _ssd_post_bwd(y, xbc_c, pc, d128, nw, dout, name):
    S = y.shape[0]

    def fn(ctx, *v):
        _, vjp = jax.vjp(_ssd_post_core, *v[:5])
        return list(vjp(v[5]))

    return _rows(name, fn, _ssd_post_ins(y, xbc_c, pc, d128, nw) + [("row", dout, None, 0)],
                 [(D, D, 0, F32), (D, D, 0, F32), (D, D, 0, BF16)], [(1, LANES, LANES), (1, D, D)], tm=256, nrows=S)


def _gates_core(g0, g1, g2, b0, b1, b2, ya, yb, yc):
    return _sigmoid(g0 + b0) * ya + _sigmoid(g1 + b1) * yb + _sigmoid(g2 + b2) * yc


def _gate_parts(pdv, bv):
    gp = pltpu.roll(pdv, SEC_D - 16, 1)
    return [gp[:, k * D:(k + 1) * D] for k in range(3)] + [bv[:, k * D:(k + 1) * D] for k in range(3)]


def _gates_fwd(pd, bg, ya, yb, yc, name):
    S = pd.shape[0]

    def fn(ctx, pdv, bv, a, b, c):
        return [_gates_core(*_gate_parts(pdv, bv), a, b, c)]

    return _rows(name, fn, [("row", pd, None, 0), ("const", bg, None, 0), ("row", ya, None, 0), ("row", yb, None, 0),
                            ("row", yc, None, 0)], [(D, D, 0, BF16)], tm=256, nrows=S)[0]


def _gates_post(dm, pdv, a, b, c, bv):
    _, vjp = jax.vjp(_gates_core, *_gate_parts(pdv, bv), a, b, c)
    g = vjp(dm)
    return [g[6], g[7], g[8], jnp.concatenate(g[0:3], axis=1), jnp.concatenate(g[3:6], axis=1)]


def _adam_update(wv, gv, mv, vv):
    m2 = ADAM_B1 * mv + (1.0 - ADAM_B1) * gv
    v2 = ADAM_B2 * vv + (1.0 - ADAM_B2) * jnp.square(gv)
    m_hat = m2 / (1.0 - ADAM_B1 ** ADAM_STEP)
    v_hat = v2 / (1.0 - ADAM_B2 ** ADAM_STEP)
    delta = -ADAM_LR * (m_hat / (jnp.sqrt(v_hat) + ADAM_EPS) + ADAM_WD * wv)
    return [delta, m2, v2]


def _adamw(w, g, m, v, name):
    rows, C = w.shape
    tm = _pick(rows, [t for t in (512, 256, 128, 64, 32, 16, 8) if t * C <= ADAM_TILE])
    return _rows(name, lambda ctx, *a: _adam_update(*a), [("row", a, None, 0) for a in (w, g, m, v)],
                 [(C, C, 0, F32)] * 3, tm=tm, nrows=rows)


def _position():
    return lax.axis_index("x"), lax.axis_index("y"), lax.axis_index("c")


def _other_chips(x, y):
    return [(1 - x, y), (x, 1 - y), (1 - x, 1 - y)]


_HBM = pl.BlockSpec(memory_space=pltpu.HBM)


def _gather_parts(half, lo, n):
    def copies(p_ref, out_ref, send_sems, recv_sems):
        x, y, c = _position()
        sibling = (x, y, 1 - c)
        chips = _other_chips(x, y)

        def slab(chip, h):
            return out_ref.at[2 * chip[0] + chip[1], pl.ds(h * half + lo, n), :]

        def copy(k, src, dst, to):
            return pltpu.make_async_remote_copy(src_ref=src, dst_ref=dst, send_sem=send_sems.at[k],
                                                recv_sem=recv_sems.at[k], device_id=to, device_id_type=MESH)

        first = [copy(j, p_ref.at[pl.ds(c * half + lo, n), :], slab((x, y), c), (*chip, c)) for j, chip in enumerate(chips)]
        passed = [copy(3 + j, slab(chip, c), slab(chip, c), sibling) for j, chip in enumerate(chips)]
        from_chips = [copy(j, slab(chip, c), slab(chip, c), (x, y, c)) for j, chip in enumerate(chips)]
        from_sibling = [copy(3 + j, slab(chip, 1 - c), slab(chip, 1 - c), (x, y, c)) for j, chip in enumerate(chips)]
        return first, passed, from_chips, from_sibling

    def start(ins, outs, scr):
        for cp in copies(ins[0], outs[0], *scr)[0]:
            cp.start()

    def finish(ins, outs, scr):
        first, passed, from_chips, from_sibling = copies(ins[0], outs[0], *scr)
        for j in range(3):
            from_chips[j].wait_recv()
            passed[j].start()
        for cp in from_sibling:
            cp.wait_recv()
        for cp in first + passed:
            cp.wait_send()

    return start, finish


def _rs_chip_parts(lo, n):
    def copies(h_ref, out_ref, send_sems, recv_sems):
        x, y, c = _position()
        return [pltpu.make_async_remote_copy(src_ref=h_ref.at[2 * chip[0] + chip[1], pl.ds(lo, n), :],
                                             dst_ref=out_ref.at[j, pl.ds(lo, n), :],
                                             send_sem=send_sems.at[j], recv_sem=recv_sems.at[j],
                                             device_id=(*chip, c), device_id_type=MESH)
                for j, chip in enumerate(_other_chips(x, y))]

    def start(ins, outs, scr):
        for cp in copies(ins[0], outs[0], *scr):
            cp.start()

    def finish(ins, outs, scr):
        for cp in copies(ins[0], outs[0], *scr):
            cp.wait()

    return start, finish


class _Stream:
    def __init__(self, src, buf, parts, nsem, units, name):
        self.src, self.buf, self.parts, self.nsem, self.name = src, buf, parts, nsem, name
        self.next, self.units = 0, units

    def _scratch(self):
        return [pltpu.SemaphoreType.DMA((self.nsem,)), pltpu.SemaphoreType.DMA((self.nsem,))]

    def _take(self, units):
        units = min(units, self.units - self.next)
        lo = self.next * 16
        self.next += units
        return lo, units * 16

    def _set(self, outs):
        self.buf = outs[0]

    def hook(self, units):
        lo, n = self._take(units)
        if n == 0:
            return None
        start, finish = self.parts(lo, n)
        return _Hook([self.src, self.buf], [jax.ShapeDtypeStruct(self.buf.shape, self.buf.dtype)], {1: 0},
                     self._scratch(), start, finish, self._set)

    def drain(self):
        lo, n = self._take(self.units)
        if n:
            start, finish = self.parts(lo, n)

            def body(s_ref, b_ref, o_ref, send_sems, recv_sems):
                args = ((s_ref, b_ref), (o_ref,), (send_sems, recv_sems))
                start(*args)
                finish(*args)

            self.buf = pl.pallas_call(
                body, name=self.name, in_specs=[_ANY, _ANY], out_specs=_ANY,
                out_shape=jax.ShapeDtypeStruct(self.buf.shape, self.buf.dtype),
                scratch_shapes=self._scratch(), input_output_aliases={1: 0},
            )(self.src, self.buf)
        return self.buf


def _rs_pair_parts(half, lo, n):
    def copy(g_ref, out_ref, send_sems, recv_sems):
        x, y, c = _position()
        return pltpu.make_async_remote_copy(
            src_ref=g_ref.at[pl.ds(0, 4), pl.ds((1 - c) * half + lo, n), :], dst_ref=out_ref.at[pl.ds(0, 4), pl.ds(lo, n), :],
            send_sem=send_sems.at[0], recv_sem=recv_sems.at[0], device_id=(x, y, 1 - c), device_id_type=MESH)

    def start(ins, outs, scr):
        copy(ins[0], outs[0], *scr).start()

    def finish(ins, outs, scr):
        copy(ins[0], outs[0], *scr).wait()

    return start, finish


def _rs_swap(r, name):
    Rh, C = r.shape

    def body(r_ref, out_ref, send_sem, recv_sem):
        x, y, c = _position()
        cp = pltpu.make_async_remote_copy(src_ref=r_ref, dst_ref=out_ref, send_sem=send_sem,
                                          recv_sem=recv_sem, device_id=(x, y, 1 - c), device_id_type=MESH)
        cp.start()
        cp.wait()

    return pl.pallas_call(
        body, name=name, in_specs=[_HBM], out_specs=_HBM,
        out_shape=jax.ShapeDtypeStruct((Rh, C), r.dtype),
        scratch_shapes=[pltpu.SemaphoreType.DMA, pltpu.SemaphoreType.DMA],
    )(r)


def _rs_add_pair(g, recv, cidx, name):
    _, R, C = g.shape
    Rh = R // 2
    tm = _pick(Rh, (400, 280, 200, 160, 80, 40, 16, 8))
    nt = Rh // tm

    def body(c_ref, g_ref, r_ref, o_ref):
        o_ref[...] = (g_ref[...].astype(F32) + r_ref[...].astype(F32)).astype(o_ref.dtype)

    return pl.pallas_call(
        body, name=name,
        grid_spec=pltpu.PrefetchScalarGridSpec(
            num_scalar_prefetch=1, grid=(4, nt),
            in_specs=[pl.BlockSpec((1, tm, C), lambda k, i, cr: (k, cr[0] * nt + i, 0)),
                      pl.BlockSpec((1, tm, C), lambda k, i, cr: (k, i, 0))],
            out_specs=pl.BlockSpec((1, tm, C), lambda k, i, cr: (k, i, 0))),
        out_shape=jax.ShapeDtypeStruct((4, Rh, C), BF16),
    )(cidx, g, recv)


def _rs_add_chips(h, recv, chip_idx, name):
    _, Rh, C = h.shape
    tm = _pick(Rh, (400, 280, 200, 160, 80, 40, 16, 8))

    def body(c_ref, h_ref, r_ref, o_ref):
        acc = h_ref[0].astype(F32)
        for j in range(3):
            acc = acc + r_ref[j].astype(F32)
        o_ref[...] = acc

    return pl.pallas_call(
        body, name=name,
        grid_spec=pltpu.PrefetchScalarGridSpec(
            num_scalar_prefetch=1, grid=(Rh // tm,),
            in_specs=[pl.BlockSpec((1, tm, C), lambda i, cr: (cr[0], i, 0)), pl.BlockSpec((3, tm, C), lambda i, cr: (0, i, 0))],
            out_specs=pl.BlockSpec((tm, C), lambda i, cr: (i, 0))),
        out_shape=jax.ShapeDtypeStruct((Rh, C), F32),
    )(chip_idx, h, recv)


def _all_reduce_small(vec, name):
    n, C = vec.shape

    def body(v_ref, out_ref, buf, send_sems, recv_sems):
        x, y, c = _position()

        def flip(k):
            return ((1 - x) if k & 4 else x, (1 - y) if k & 2 else y, (1 - c) if k & 1 else c)

        def idx(p):
            return 4 * p[0] + 2 * p[1] + p[2]

        me = idx((x, y, c))
        buf[me] = v_ref[...]
        cps = [pltpu.make_async_remote_copy(src_ref=v_ref, dst_ref=buf.at[me], send_sem=send_sems.at[k - 1],
                                            recv_sem=recv_sems.at[k - 1], device_id=flip(k), device_id_type=MESH)
               for k in range(1, 8)]
        for cp in cps:
            cp.start()
        for k in range(1, 8):
            pltpu.make_async_remote_copy(src_ref=v_ref, dst_ref=buf.at[idx(flip(k))], send_sem=send_sems.at[k - 1],
                                         recv_sem=recv_sems.at[k - 1], device_id=flip(k), device_id_type=MESH).wait_recv()
        for cp in cps:
            cp.wait_send()
        acc = buf[0]
        for s in range(1, 8):
            acc = acc + buf[s]
        out_ref[...] = acc

    return pl.pallas_call(
        body, name=name,
        in_specs=[pl.BlockSpec(memory_space=pltpu.VMEM)], out_specs=pl.BlockSpec(memory_space=pltpu.VMEM),
        out_shape=jax.ShapeDtypeStruct((n, C), F32),
        scratch_shapes=[pltpu.VMEM((8, n, C), F32), pltpu.SemaphoreType.DMA((7,)), pltpu.SemaphoreType.DMA((7,))],
    )(vec)


BIG = (("w_in", (D, IN_WIDTH // 4), "cols"), ("w_a", (GW, D // 4), "cols"), ("pool_w", (4, PG // 4, PG), "pool"),
       ("w_b", (D // 4, D), "rows"), ("w_c", (D // 4, D), "rows"), ("w_o", (D // 4, D), "rows"),
       ("ffn_w_up", (D, 2 * D_FF // 4), "cols"), ("ffn_w_down", (D_FF // 4, D), "rows"))
def _pack_rows(s):
    k = math.prod(s) // D
    return -(-k // 16) * 16, k


PACK_ROWS = sum(_pack_rows(s)[0] for _, s, _ in BIG)
PACK_PAD = -(-PACK_ROWS // 32) * 32


def _pad_rows(v, rows):
    pad = [(0, 0)] * v.ndim
    pad[-2] = (0, rows - v.shape[-2])
    return jnp.pad(v, pad) if rows > v.shape[-2] else v


def _pack_blocks(blocks, dtype):
    lead = blocks["w_in"].shape[:-2]
    flat = []
    for n, s, how in BIG:
        v = blocks[n].astype(dtype)
        if how == "cols":
            v = jnp.swapaxes(v, -1, -2)
        flat.append(_pad_rows(v.reshape(*lead, -1, D), _pack_rows(s)[0]))
    flat.append(jnp.zeros((*lead, PACK_PAD - PACK_ROWS, D), dtype))
    return jnp.concatenate(flat, axis=-2)


def _unpack_blocks(pack):
    out, r = {}, 0
    for n, s, how in BIG:
        rows, k = _pack_rows(s)
        v = pack[r:r + k, :]
        out[n] = v.reshape(s[1], s[0]).T if how == "cols" else v.reshape(s)
        r += rows
    return out


def _operands(allp):
    out, r = {}, 0
    for n, s, how in BIG:
        rows, k = _pack_rows(s)
        v = allp[:, r:r + k, :]
        if how == "cols":
            out[n] = v.reshape(4 * s[1], s[0])
        elif how == "rows":
            out[n] = v.reshape(4 * s[0], s[1])
        else:
            out[n] = v.reshape(4, *s).transpose(1, 0, 2, 3).reshape(4, PG, PG)
        r += rows
    return out


def _pack_operands(g, dtype):
    flat = []
    for n, s, how in BIG:
        v = g[n].astype(dtype)
        if how == "pool":
            v = v.reshape(4, 4, s[1], s[2]).transpose(1, 0, 2, 3)
        flat.append(_pad_rows(v.reshape(4, -1, D), _pack_rows(s)[0]))
    flat.append(jnp.zeros((4, PACK_PAD - PACK_ROWS, D), dtype))
    return jnp.concatenate(flat, axis=1)


def _layer_fwd(x, w, sm, bias, hk):
    pa, u = _mmf(None, w["in_a"], tb=True, pre=(_rms_core, [x], [sm["ln1_g"]]), name="in_a", tm=1024, hook=hk("in_a"))
    pb = _mm(u, w["in_b"], tb=True, out_dtype=BF16, name="in_b", hook=hk("in_b"))
    pc = _mm(u, w["in_c"], tb=True, out_dtype=BF16, name="in_c", hook=hk("in_c"))
    pd = _mm(u, w["in_d"], tb=True, out_dtype=BF16, name="in_d", hook=hk("in_d"))
    os_, ls_ = [], []
    for gi in range(3):
        o, l = _attn_fwd(pa, bias[gi], gi, "attn_fwd%d" % gi)
        os_.append(o)
        ls_.append(l)
    att = _mix_fwd(os_, ls_, "mix_fwd")
    ya = _mm(att, w["w_a"], tb=True, out_dtype=BF16, name="mm_wa")
    pool_o = _pool_fwd(pb, w["pool_w"], sm["pool_scale"], "pool_fwd")
    yb = _mm(pool_o, w["w_b"], out_dtype=BF16, name="mm_wb")
    xbc_c = _ssd_conv_fwd(pc, sm["ssd_conv_w"], sm["ssd_conv_b"], "ssd_conv_fwd")
    y_scan, states = _ssd_scan_fwd(xbc_c, pd, sm["ssd_dt_bias"], sm["ssd_a_log"], "ssd_scan_fwd")
    ssd_o = _ssd_post_fwd(y_scan, xbc_c, pc, sm["ssd_d"], sm["ssd_norm_w"], "ssd_post_fwd")
    yc = _mm(ssd_o, w["w_c"], out_dtype=BF16, name="mm_wc")
    merged = _gates_fwd(pd, sm["b_gate"], ya, yb, yc, "gates_fwd")
    x1 = _mm(merged, w["w_o"], add=x, name="mm_wo", hook=hk("mm_wo"))
    h, u2 = _mmf(None, w["ffn_w_up"], tb=True, pre=(_rms_core, [x1], [sm["ln2_g"]]), out_dtype=BF16, name="mm_up",
                 tm=1024, hook=hk("mm_up"))
    f = _ffn_act_fwd(h, sm["ffn_conv_w"], sm["ffn_conv_b"], "ffn_act_fwd")
    x2 = _mm(f, w["ffn_w_down"], add=x1, name="mm_down", hook=hk("mm_down"))
    saved = dict(x=x, u=u, pa=pa, pb=pb, pc=pc, pd=pd, os=os_, ls=ls_, att=att, ya=ya, yb=yb, yc=yc, pool_o=pool_o,
                 xbc_c=xbc_c, y_scan=y_scan, states=states, ssd_o=ssd_o, merged=merged, x1=x1, u2=u2, h=h, f=f)
    return x2, saved


def _layer_bwd(dx2, dx2b, w, sm, bias, dbs, sv, hk):
    gw, gs = {}, {}
    S = dx2.shape[0]

    def gmm(a, b, name):
        return _mm(a, b, ta=True, out_dtype=BF16, name=name, hook=hk(name))

    df = _mm(dx2b, w["ffn_w_down"], tb=True, out_dtype=BF16, name="d_f", hook=hk("d_f"))
    gw["ffn_w_down"] = gmm(sv["f"], dx2b, "g_down")
    dha, dhv, gs["ffn_conv_w"], gs["ffn_conv_b"] = _ffn_act_bwd(sv["h"], sm["ffn_conv_w"], sm["ffn_conv_b"], df, "ffn_act_bwd")
    dx1, dx1b, gs["ln2_g"] = _mmf([dha, dhv], [w["up_a"], w["up_v"]], name="d_u2_v", tm=256, hook=hk("d_u2_v"),
                                  post=(_rms_post, [sv["x1"], dx2], [sm["ln2_g"]], RMS_POST_OUTS, [(1, D)]))
    gw["ffn_w_up"] = jnp.concatenate([gmm(dha, sv["u2"], "g_up_a"), gmm(dhv, sv["u2"], "g_up_v")], axis=0)
    dya, dyb, dyc, dgate, gs["b_gate"] = _mmf(
        dx1b, w["w_o"], tb=True, name="d_merged", tm=256, hook=hk("d_merged"),
        post=(_gates_post, [sv["pd"], sv["ya"], sv["yb"], sv["yc"]], [sm["b_gate"]],
              [(D, BF16)] * 3 + [(3 * D, BF16)], [(1, 3 * D)]))
    gw["w_o"] = gmm(sv["merged"], dx1b, "g_wo")
    dssd_o = _mm(dyc, w["w_c"], tb=True, name="d_ssd_o")
    gw["w_c"] = gmm(sv["ssd_o"], dyc, "g_wc")
    dy_scan, dxs_skip, dz, gs["ssd_d"], gs["ssd_norm_w"] = _ssd_post_bwd(
        sv["y_scan"], sv["xbc_c"], sv["pc"], sm["ssd_d"], sm["ssd_norm_w"], dssd_o, "ssd_post_bwd")
    dxbc_c, ddt, gs["ssd_dt_bias"], gs["ssd_a_log"] = _ssd_scan_bwd(
        sv["xbc_c"], sv["pd"], sm["ssd_dt_bias"], sm["ssd_a_log"], sv["states"], dy_scan, dxs_skip, "ssd_scan_bwd")
    dxbc, gs["ssd_conv_w"], gs["ssd_conv_b"] = _ssd_conv_bwd(sv["pc"], sm["ssd_conv_w"], sm["ssd_conv_b"], dxbc_c, "ssd_conv_bwd")
    dpool_o = _mm(dyb, w["w_b"], tb=True, name="d_pool_o")
    gw["w_b"] = gmm(sv["pool_o"], dyb, "g_wb")
    dpb, dpw, gs["pool_scale"] = _pool_bwd(sv["pb"], w["pool_w"], sm["pool_scale"], dpool_o, "pool_bwd")
    gw["pool_w"] = dpw.reshape(4, PG, PG)
    datt = _mm(dya, w["w_a"], name="d_att")
    gw["w_a"] = gmm(dya, sv["att"], "g_wa")
    dos, dls = _mix_bwd(sv["os"], sv["ls"], datt, "mix_bwd")
    dqkv = tuple(lax.empty((S, AW), F32) for _ in range(3))
    dbs = list(dbs)
    for gi in range(3):
        dqkv, dbs[gi] = _attn_bwd(sv["pa"], bias[gi], dos[gi], dls[gi], dbs[gi], dqkv, gi, "attn_bwd%d" % gi)
    u = sv["u"]
    pieces = [(dqkv[0], "wq"), (dqkv[1], "wk"), (dqkv[2], "wv"), (dpb, "in_b"), (dz, "wz"), (dxbc, "wxbc"),
              (ddt, "wdt"), (dgate, "wgate")]
    du = _mmf([dp for dp, _ in pieces[:4]], [w[key] for _, key in pieces[:4]], name="d_u_a", tm=256, hook=hk("d_u_a"))[0]
    dx, dxb, gs["ln1_g"] = _mmf([dp for dp, _ in pieces[4:]], [w[key] for _, key in pieces[4:]], add=du,
                                name="d_u_wgate", tm=256, hook=hk("d_u_wgate"),
                                post=(_rms_post, [sv["x"], dx1], [sm["ln1_g"]], RMS_POST_OUTS, [(1, D)]))
    g_in = []
    for dp, key in pieces:
        g = gmm(dp, u, "g_in_" + key)
        g_in.append(g[:SSD_HEADS] if key == "wdt" else g)
    gw["w_in"] = jnp.concatenate(g_in, axis=0)
    return dx, dxb, gw, gs, dbs


SMALL_LAYER = ("ln1_g", "b_gate", "pool_scale", "ssd_conv_w", "ssd_conv_b", "ssd_dt_bias", "ssd_a_log", "ssd_d",
               "ssd_norm_w", "ln2_g", "ffn_conv_w", "ffn_conv_b")


def _pad_lanes(v):
    return jnp.pad(v, (0, LANES - v.shape[0])).reshape(1, LANES)


def _layer_weights(ops):
    wt = ops["w_in"]
    o1, o2, o3 = SEC_A, SEC_A + SEC_B, SEC_A + SEC_B + SEC_C
    w = dict(ops)
    w["in_a"] = jnp.pad(wt[:o1], ((0, SEC_A_PAD - o1), (0, 0)))
    w["in_b"] = wt[o1:o2]
    w["in_c"] = wt[o2:o3]
    w["in_d"] = jnp.pad(wt[o3:], ((0, SEC_D - (IN_WIDTH - o3)), (0, 0)))
    w["wq"], w["wk"], w["wv"] = wt[:AW], wt[AW:2 * AW], wt[2 * AW:o1]
    w["wz"], w["wxbc"] = wt[o2:o2 + D], wt[o2 + D:o3]
    w["wdt"] = jnp.pad(wt[o3:o3 + SSD_HEADS], ((0, LANES - SSD_HEADS), (0, 0)))
    w["wgate"] = wt[o3 + SSD_HEADS:]
    w["up_a"], w["up_v"] = ops["ffn_w_up"][:D_FF], ops["ffn_w_up"][D_FF:]
    return w


def _layer_small(p, i):
    sm = {n: p[n][i] for n in SMALL_LAYER}
    out = {}
    for n, v in sm.items():
        if n in ("ssd_dt_bias", "ssd_a_log", "ssd_d"):
            out[n] = _pad_lanes(v)
        elif v.ndim == 1:
            out[n] = v.reshape(1, -1)
        else:
            out[n] = v
    return out


def _local_step(x, target, rel_bias, final_g, layer_full, small, fwd_hooks=None, bwd_hooks=None, after_bwd=None):
    nl = small["ln1_g"].shape[0]
    buckets = [_buckets(d).astype(jnp.int32) for d in DILATIONS]
    bias = [_bias_table(rel_bias, buckets[gi], gi, "bias_table%d" % gi) for gi in range(3)]
    no_hooks = lambda i: (lambda name: None)
    fwd_hooks = fwd_hooks or no_hooks
    bwd_hooks = bwd_hooks or no_hooks
    saved, ws, sms = [], [], []
    h = x
    for i in range(nl):
        w = _layer_weights(layer_full(i))
        sm = _layer_small(small, i)
        h, sv = _layer_fwd(h, w, sm, bias, fwd_hooks(i))
        saved.append(sv)
        ws.append(w)
        sms.append(sm)
    dh, dhb, dfinal, loss = _final_loss(h, target, final_g.reshape(1, D))
    gws, gss = [None] * nl, [None] * nl
    dbs = [jnp.zeros((6, WIN, 2 * WIN), F32)] * 3
    for i in reversed(range(nl)):
        dh, dhb, gws[i], gss[i], dbs = _layer_bwd(dh, dhb, ws[i], sms[i], bias, dbs, saved[i], bwd_hooks(i))
        if after_bwd is not None:
            after_bwd(i, gws[i])
    drel = []
    for gi in range(3):
        onehot = jnp.pad(jax.nn.one_hot(buckets[gi].reshape(-1), REL_BUCKETS, dtype=BF16), ((0, 0), (0, LANES - REL_BUCKETS)))
        drel.append(_mm(dbs[gi].reshape(6, WIN * 2 * WIN), onehot, name="g_relb"))
    return loss, dh, gws, gss, dfinal, jnp.concatenate(drel, axis=0)


WEIGHTS = ("rel_bias", "ln1_g", "w_in", "b_gate", "w_a", "pool_w", "pool_scale", "w_b", "ssd_conv_w", "ssd_conv_b",
           "ssd_dt_bias", "ssd_a_log", "ssd_d", "ssd_norm_w", "w_c", "w_o", "ln2_g", "ffn_w_up", "ffn_conv_w",
           "ffn_conv_b", "ffn_w_down", "final_g")
BIG_NAMES = tuple(n for n, _, _ in BIG)
SHARDED_SMALL = {"ssd_conv_w": XBC // 4, "ffn_conv_w": 2 * D_FF // 4}


def _to_rows(flat):
    n = flat.shape[0]
    rows = -(-n // LANES)
    rows = -(-rows // 8) * 8
    return jnp.pad(flat, (0, rows * LANES - n)).reshape(rows, LANES)


def _flatten(tree, names):
    return jnp.concatenate([tree[n].reshape(-1) for n in names])


def _unflatten(flat, shapes, names):
    out, o = {}, 0
    for n in names:
        k = math.prod(shapes[n])
        out[n] = flat[o:o + k].reshape(shapes[n])
        o += k
    return out


def kernel(x, rel_bias, ln1_g, w_in, b_gate, w_a, pool_w, pool_scale, w_b, ssd_conv_w, ssd_conv_b, ssd_dt_bias, ssd_a_log, ssd_d, ssd_norm_w, w_c, w_o, ln2_g, ffn_w_up, ffn_conv_w, ffn_conv_b, ffn_w_down, final_g, loss_target, m_rel_bias, m_ln1_g, m_w_in, m_b_gate, m_w_a, m_pool_w, m_pool_scale, m_w_b, m_ssd_conv_w, m_ssd_conv_b, m_ssd_dt_bias, m_ssd_a_log, m_ssd_d, m_ssd_norm_w, m_w_c, m_w_o, m_ln2_g, m_ffn_w_up, m_ffn_conv_w, m_ffn_conv_b, m_ffn_w_down, m_final_g, v_rel_bias, v_ln1_g, v_w_in, v_b_gate, v_w_a, v_pool_w, v_pool_scale, v_w_b, v_ssd_conv_w, v_ssd_conv_b, v_ssd_dt_bias, v_ssd_a_log, v_ssd_d, v_ssd_norm_w, v_w_c, v_w_o, v_ln2_g, v_ffn_w_up, v_ffn_conv_w, v_ffn_conv_b, v_ffn_w_down, v_final_g):
    W = dict(rel_bias=rel_bias, ln1_g=ln1_g, w_in=w_in, b_gate=b_gate, w_a=w_a, pool_w=pool_w, pool_scale=pool_scale,
             w_b=w_b, ssd_conv_w=ssd_conv_w, ssd_conv_b=ssd_conv_b, ssd_dt_bias=ssd_dt_bias, ssd_a_log=ssd_a_log,
             ssd_d=ssd_d, ssd_norm_w=ssd_norm_w, w_c=w_c, w_o=w_o, ln2_g=ln2_g, ffn_w_up=ffn_w_up,
             ffn_conv_w=ffn_conv_w, ffn_conv_b=ffn_conv_b, ffn_w_down=ffn_w_down, final_g=final_g)
    M = dict(rel_bias=m_rel_bias, ln1_g=m_ln1_g, w_in=m_w_in, b_gate=m_b_gate, w_a=m_w_a, pool_w=m_pool_w,
             pool_scale=m_pool_scale, w_b=m_w_b, ssd_conv_w=m_ssd_conv_w, ssd_conv_b=m_ssd_conv_b,
             ssd_dt_bias=m_ssd_dt_bias, ssd_a_log=m_ssd_a_log, ssd_d=m_ssd_d, ssd_norm_w=m_ssd_norm_w, w_c=m_w_c,
             w_o=m_w_o, ln2_g=m_ln2_g, ffn_w_up=m_ffn_w_up, ffn_conv_w=m_ffn_conv_w, ffn_conv_b=m_ffn_conv_b,
             ffn_w_down=m_ffn_w_down, final_g=m_final_g)
    V = dict(rel_bias=v_rel_bias, ln1_g=v_ln1_g, w_in=v_w_in, b_gate=v_b_gate, w_a=v_w_a, pool_w=v_pool_w,
             pool_scale=v_pool_scale, w_b=v_w_b, ssd_conv_w=v_ssd_conv_w, ssd_conv_b=v_ssd_conv_b,
             ssd_dt_bias=v_ssd_dt_bias, ssd_a_log=v_ssd_a_log, ssd_d=v_ssd_d, ssd_norm_w=v_ssd_norm_w, w_c=v_w_c,
             w_o=v_w_o, ln2_g=v_ln2_g, ffn_w_up=v_ffn_w_up, ffn_conv_w=v_ffn_conv_w, ffn_conv_b=v_ffn_conv_b,
             ffn_w_down=v_ffn_w_down, final_g=v_final_g)
    nl = ln1_g.shape[0]
    px, py, pc_ = _position()
    chip = 2 * px + py
    cidx = jnp.reshape(pc_, (1,)).astype(jnp.int32)
    chip_idx = jnp.reshape(chip, (1,)).astype(jnp.int32)

    placed = {}
    for n, cs in SHARDED_SMALL.items():
        full = jnp.zeros(W[n].shape[:-1] + (4 * cs,), F32)
        full = lax.dynamic_update_slice(full, W[n], (0, 0, chip * cs))
        placed[n] = jnp.where(pc_ == 0, full, 0.0)
    names_sh = tuple(SHARDED_SMALL)
    shapes_sh = {n: placed[n].shape for n in names_sh}
    got = _all_reduce_small(_to_rows(_flatten(placed, names_sh)), "gather_small")
    small = {n: W[n] for n in SMALL_LAYER}
    small.update(_unflatten(got.reshape(-1), shapes_sh, names_sh))

    packs = _pack_blocks({n: W[n] for n in BIG_NAMES}, BF16)

    half = PACK_PAD // 2
    units = half // 16

    def share(weights, total):
        tot = sum(weights.values())
        return {n: math.ceil(total * v / tot) for n, v in weights.items()}

    gathers = {}

    def gather(i):
        if i not in gathers:
            buf = lax.dynamic_update_slice(lax.empty((4, PACK_PAD, D), BF16), packs[i][None], (chip, 0, 0))
            gathers[i] = _Stream(packs[i], buf, functools.partial(_gather_parts, half), 6, units, "gather_w")
        return gathers[i]

    def layer_full(i):
        return _operands(gather(i).drain())

    fwd_share = share(dict(in_a=63, in_c=31, in_d=44, mm_up=83, mm_down=34), units)

    def fwd_hooks(i):
        if i + 1 >= nl:
            return lambda name: None
        return lambda name: gather(i + 1).hook(fwd_share[name]) if name in fwd_share else None

    exchanges = {}
    bwd_share = share(dict(g_down=35, d_u2_v=60, g_up_a=35, g_up_v=35, d_merged=50, d_u_a=60, d_u_wgate=70,
                           g_in_wgate=36), units)

    class Exchange:
        def __init__(self, g):
            self.g = g
            self.pair = _Stream(g, lax.empty((4, half, D), BF16), functools.partial(_rs_pair_parts, half), 1, units, "rs_pair")
            self.hsum = self.chips = None

        def to_chips(self):
            if self.chips is None:
                self.hsum = _rs_add_pair(self.g, self.pair.drain(), cidx, "rs_add_pair")
                self.chips = _Stream(self.hsum, lax.empty((3, half, D), BF16), _rs_chip_parts, 3, units, "rs_chips")
            return self.chips

    def after_bwd(i, gw):
        exchanges[i] = Exchange(_pack_operands(gw, BF16))

    def bwd_hooks(i):
        if i + 1 >= nl:
            return lambda name: None

        def hk(name):
            if name == "d_f":
                return exchanges[i + 1].pair.hook(units)
            return exchanges[i + 1].to_chips().hook(bwd_share[name]) if name in bwd_share else None

        return hk

    loss, dx, gws, gss, dfinal, drel = _local_step(x[0], loss_target[0], rel_bias, final_g, layer_full, small,
                                                   fwd_hooks, bwd_hooks, after_bwd)

    def reduced(i):
        recv3 = exchanges[i].to_chips().drain()
        r = _rs_add_chips(exchanges[i].hsum, recv3, chip_idx, "rs_add_chips")
        other = _rs_swap(r, "rs_swap")
        both = jnp.concatenate([jnp.where(pc_ == 0, r, other), jnp.where(pc_ == 0, other, r)], axis=0)
        return _unpack_blocks(both)

    red = [reduced(i) for i in range(nl)]
    delta, new_m, new_v, grads = {}, {}, {}, {}
    for n in BIG_NAMES:
        shp = W[n].shape
        r2 = lambda a: a.reshape(-1, shp[-1])
        grads[n] = jnp.stack([red[i][n] for i in range(nl)], axis=0)
        res = _adamw(r2(W[n]), r2(grads[n]), r2(M[n]), r2(V[n]), "adamw_" + n)
        delta[n], new_m[n], new_v[n] = [a.reshape(shp) for a in res]

    sg = {}
    for n in SMALL_LAYER:
        sg[n] = jnp.stack([gss[i][n] for i in range(nl)], axis=0)
    for n in ("ssd_dt_bias", "ssd_a_log", "ssd_d"):
        sg[n] = sg[n][:, 0, :SSD_HEADS]
    sg["rel_bias"] = drel[:, :REL_BUCKETS].T
    sg["final_g"] = dfinal.reshape(D)
    sg["loss"] = loss[0, :1]
    names_sg = tuple(sg)
    shapes_sg = {n: ((nl,) + W[n].shape[1:] if n in SMALL_LAYER and n not in SHARDED_SMALL else
                     (placed[n].shape if n in SHARDED_SMALL else sg[n].shape)) for n in names_sg}
    for n in names_sg:
        sg[n] = sg[n].reshape(shapes_sg[n])
    tot = _all_reduce_small(_to_rows(_flatten(sg, names_sg)), "allreduce_small")
    tot = _unflatten(tot.reshape(-1), shapes_sg, names_sg)
    loss_out = tot.pop("loss").reshape(())
    for n, cs in SHARDED_SMALL.items():
        tot[n] = lax.dynamic_slice(tot[n], (0, 0, chip * cs), tot[n].shape[:-1] + (cs,))
    grads.update(tot)

    names_s = tuple(n for n in WEIGHTS if n not in BIG_NAMES)
    shapes_s = {n: W[n].shape for n in names_s}
    pk = lambda t: _to_rows(_flatten(t, names_s))
    dl, m2, v2 = _adamw(pk(W), pk(grads), pk(M), pk(V), "adamw_small")
    delta.update(_unflatten(dl.reshape(-1), shapes_s, names_s))
    new_m.update(_unflatten(m2.reshape(-1), shapes_s, names_s))
    new_v.update(_unflatten(v2.reshape(-1), shapes_s, names_s))

    return (loss_out, dx[None], *[grads[n] for n in WEIGHTS], *[delta[n] for n in WEIGHTS],
            *[new_m[n] for n in WEIGHTS], *[new_v[n] for n in WEIGHTS])
```

```python
import functools
import math

import jax
import jax.numpy as jnp
from jax import lax
from jax.experimental import pallas as pl
from jax.experimental.pallas import tpu as pltpu

F32 = jnp.float32
BF16 = jnp.bfloat16
MESH = pl.DeviceIdType.MESH

D = 1024
HD = 64
GW = 384
AW = 3 * GW
WIN = 128
DILATIONS = (1, 4, 16)
REL_BUCKETS = 32
REL_MAX_DISTANCE = 2048
POOL_WINDOWS = (2, 4, 8, 16)
PG = 256
SSD_HEADS = 16
SSD_N = 128
SSD_CHUNK = 128
XBC = 1536
D_FF = 2816
EPS = 1e-6
NEG = -1e30
HALO = 16
LANES = 128

SEC_A = 3 * AW
SEC_B = D
SEC_C = D + XBC
SEC_D = 3328
SEC_A_PAD = 3584
IN_WIDTH = SEC_A + SEC_B + SEC_C + 16 + 3 * D

ADAM_LR = 0.001
ADAM_B1 = 0.9
ADAM_B2 = 0.999
ADAM_EPS = 1e-08
ADAM_WD = 0.01
ADAM_STEP = 10
ADAM_TILE = 256 * 1024
MM_VMEM_BYTES = 40 * 1024 * 1024
MM_MAX_OUT_TILE = 1024 * 1024
HBM_BYTES_PER_US = 2.0e6
STEP_US = 0.35
MXU_WIDTH = 256
MXU_FLOPS_PER_US = 0.65e6


_ANY = pl.BlockSpec(memory_space=pl.ANY)


def _pick(d, cands):
    for t in cands:
        if d % t == 0:
            return t
    return d


def _iota(shape, dim):
    return lax.broadcasted_iota(jnp.int32, shape, dim)


def _dg(a, b, ca, cb):
    return lax.dot_general(a.astype(BF16), b.astype(BF16), (((ca,), (cb,)), ((), ())),
                           preferred_element_type=F32)


@jax.custom_vjp
def _bdot_nn(a, b):
    return _dg(a, b, 1, 0)


def _nn_fwd(a, b):
    return _dg(a, b, 1, 0), (a, b)


def _nn_bwd(res, g):
    a, b = res
    return _dg(g, b, 1, 1), _dg(a, g, 0, 0)


_bdot_nn.defvjp(_nn_fwd, _nn_bwd)


@jax.custom_vjp
def _bdot_nt(a, b):
    return _dg(a, b, 1, 1)


def _nt_fwd(a, b):
    return _dg(a, b, 1, 1), (a, b)


def _nt_bwd(res, g):
    a, b = res
    return _dg(g, b, 1, 0), _dg(g, a, 0, 0)


_bdot_nt.defvjp(_nt_fwd, _nt_bwd)


@jax.custom_vjp
def _bdot_tn(a, b):
    return _dg(a, b, 0, 0)


def _tn_fwd(a, b):
    return _dg(a, b, 0, 0), (a, b)


def _tn_bwd(res, g):
    a, b = res
    return _dg(b, g, 1, 1), _dg(a, g, 1, 0)


_bdot_tn.defvjp(_tn_fwd, _tn_bwd)


def _fdot(a, b):
    return jnp.dot(a, b, preferred_element_type=F32, precision=lax.Precision.HIGHEST)


def _sigmoid(x):
    return 0.5 * jnp.tanh(0.5 * x) + 0.5


def _silu(x):
    return x * _sigmoid(x)


def _softplus(x):
    return jnp.maximum(x, 0.0) + jnp.log(1.0 + jnp.exp(-jnp.abs(x)))


def _lane_pick(m, h):
    return jnp.sum(jnp.where(_iota(m.shape, 1) == h, m, 0.0), axis=1, keepdims=True)


def _row_pick(m, h):
    return jnp.sum(jnp.where(_iota(m.shape, 0) == h, m, 0.0), axis=0, keepdims=True)


def _stack_rows(rows, n):
    c = rows[0].shape[1]
    r = _iota((n, c), 0)
    out = jnp.zeros((n, c), F32)
    for k, v in enumerate(rows):
        out = out + jnp.where(r == k, v, 0.0)
    return out


def _mm(a, b, *, ta=False, tb=False, add=None, out_dtype=F32, name, hook=None):
    if ta:
        K, M = a.shape
    else:
        M, K = a.shape
    if tb:
        N, Kb = b.shape
    else:
        Kb, N = b.shape
    assert K == Kb, (a.shape, b.shape, ta, tb)
    tm, tn, tk = _mm_tiles(M, N, K, a.dtype.itemsize, b.dtype.itemsize, jnp.dtype(out_dtype).itemsize,
                           0 if add is None else add.dtype.itemsize)
    ni, nj, nk = M // tm, N // tn, K // tk
    ca = 0 if ta else 1
    cb = 1 if tb else 0
    n_in = 2 if add is None else 3
    n_hin = 0 if hook is None else len(hook.inputs)
    n_hout = 0 if hook is None else len(hook.out_shapes)

    def body(*refs):
        a_ref, b_ref = refs[:2]
        add_ref = None if add is None else refs[2]
        o_ref = refs[n_in + n_hin]
        scr = refs[n_in + n_hin + 1 + n_hout:]
        acc_ref = scr[0] if nk > 1 else None
        hargs = (refs[n_in:n_in + n_hin], refs[n_in + n_hin + 1:n_in + n_hin + 1 + n_hout], scr[1 if nk > 1 else 0:])
        i, j, k = pl.program_id(0), pl.program_id(1), pl.program_id(2)
        if hook is not None:
            @pl.when((i == 0) & (j == 0) & (k == 0))
            def _():
                hook.start(*hargs)

        part = _dg(a_ref[...], b_ref[...], ca, cb)

        def finish(r):
            if add_ref is not None:
                r = r + add_ref[...].astype(F32)
            o_ref[...] = r.astype(o_ref.dtype)

        if nk == 1:
            finish(part)
        else:
            @pl.when(k == 0)
            def _():
                acc_ref[...] = part

            @pl.when((k > 0) & (k < nk - 1))
            def _():
                acc_ref[...] += part

            @pl.when(k == nk - 1)
            def _():
                finish(acc_ref[...] + part)

        if hook is not None:
            @pl.when((i == ni - 1) & (j == nj - 1) & (k == nk - 1))
            def _():
                hook.finish(*hargs)

    a_spec = pl.BlockSpec((tk, tm), lambda i, j, k: (k, i)) if ta else pl.BlockSpec((tm, tk), lambda i, j, k: (i, k))
    b_spec = pl.BlockSpec((tn, tk), lambda i, j, k: (j, k)) if tb else pl.BlockSpec((tk, tn), lambda i, j, k: (k, j))
    in_specs = [a_spec, b_spec]
    args = [a, b]
    if add is not None:
        in_specs.append(pl.BlockSpec((tm, tn), lambda i, j, k: (i, j)))
        args.append(add)
    out_specs = [pl.BlockSpec((tm, tn), lambda i, j, k: (i, j))]
    out_shape = [jax.ShapeDtypeStruct((M, N), out_dtype)]
    scratch = [pltpu.VMEM((tm, tn), F32)] if nk > 1 else []
    aliases = {}
    if hook is not None:
        in_specs += [_ANY] * n_hin
        args += list(hook.inputs)
        out_specs += [_ANY] * n_hout
        out_shape += list(hook.out_shapes)
        scratch += list(hook.scratch)
        aliases = {n_in + hi: 1 + ho for hi, ho in hook.aliases.items()}
    sem = ("parallel", "parallel", "arbitrary") if hook is None else ("arbitrary",) * 3
    res = pl.pallas_call(
        body, name=name, grid=(ni, nj, nk), in_specs=in_specs, out_specs=out_specs, out_shape=out_shape,
        scratch_shapes=scratch, input_output_aliases=aliases,
        compiler_params=pltpu.CompilerParams(dimension_semantics=sem),
    )(*args)
    if hook is not None:
        hook.done(res[1:])
    return res[0]


def _wide(v):
    return v.astype(F32) if v.dtype == BF16 else v


def _mmf(a, b, *, tb=False, add=None, pre=None, post=None, out_dtype=F32, name, tm, hook=None):
    a_list = list(a) if isinstance(a, (list, tuple)) else [a]
    b_list = list(b) if isinstance(b, (list, tuple)) else [b]
    assert len(a_list) == len(b_list) and (len(b_list) == 1 or not (tb or pre))
    b = b_list[0]
    if tb:
        N, K = b.shape
    else:
        K, N = b.shape
    M = pre[1][0].shape[0] if pre else a_list[0].shape[0]
    tn = N if post or N <= 1024 else _pick(N, (512, 256, LANES))
    ni, nj = M // tm, N // tn
    cb = 1 if tb else 0
    pre_fn, pre_rows, pre_consts = pre if pre else (None, [], [])
    post_fn, post_rows, post_consts, post_outs, post_accs = post if post else (None, [], [], [], [])
    hook_in = [] if hook is None else list(hook.inputs)
    hook_out = [] if hook is None else list(hook.out_shapes)

    def row_spec(arr):
        return pl.BlockSpec((tm, arr.shape[1]), lambda i, j: (i, 0))

    def const_spec(arr):
        return pl.BlockSpec(arr.shape, lambda i, j, nd=arr.ndim: (0,) * nd)

    args, in_specs = [], []
    for arr in (a_list if not pre else pre_rows):
        args.append(arr)
        in_specs.append(row_spec(arr))
    for arr in pre_consts:
        args.append(arr)
        in_specs.append(const_spec(arr))
    for arr in b_list:
        args.append(arr)
        in_specs.append(pl.BlockSpec((tn, K), lambda i, j: (j, 0)) if tb else
                        pl.BlockSpec((arr.shape[0], tn), lambda i, j: (0, j)))
    if add is not None:
        args.append(add)
        in_specs.append(pl.BlockSpec((tm, tn), lambda i, j: (i, j)))
    for arr in post_rows:
        args.append(arr)
        in_specs.append(row_spec(arr))
    for arr in post_consts:
        args.append(arr)
        in_specs.append(const_spec(arr))
    n_main = len(args)
    args += hook_in
    in_specs += [_ANY] * len(hook_in)

    out_shape, out_specs = [], []
    if post:
        for c, dt in post_outs:
            out_shape.append(jax.ShapeDtypeStruct((M, c), dt))
            out_specs.append(pl.BlockSpec((tm, c), lambda i, j: (i, 0)))
        for r, c in post_accs:
            out_shape.append(jax.ShapeDtypeStruct((r, c), F32))
            out_specs.append(pl.BlockSpec((r, c), lambda i, j: (0, 0)))
    else:
        out_shape.append(jax.ShapeDtypeStruct((M, N), out_dtype))
        out_specs.append(pl.BlockSpec((tm, tn), lambda i, j: (i, j)))
    if pre:
        out_shape.append(jax.ShapeDtypeStruct((M, K), BF16))
        out_specs.append(pl.BlockSpec((tm, K), lambda i, j: (i, 0)))
    n_out = len(out_shape)
    out_shape += hook_out
    out_specs += [_ANY] * len(hook_out)
    scratch = ([pltpu.VMEM((tm, K), BF16)] if pre else []) + ([] if hook is None else list(hook.scratch))
    aliases = {} if hook is None else {n_main + hi: n_out + ho for hi, ho in hook.aliases.items()}

    def body(*refs):
        ins, outs, scr = refs[:n_main], refs[len(args):len(args) + n_out], refs[len(args) + len(out_shape):]
        hargs = (refs[n_main:len(args)], refs[len(args) + n_out:len(args) + len(out_shape)], scr[1 if pre else 0:])
        i, j = pl.program_id(0), pl.program_id(1)
        if hook is not None:
            @pl.when((i == 0) & (j == 0))
            def _():
                hook.start(*hargs)

        it = iter(ins)
        if pre:
            rows_ = [next(it) for _ in pre_rows]
            consts_ = [next(it) for _ in pre_consts]

            @pl.when(j == 0)
            def _():
                av = pre_fn(*[_wide(r[...]) for r in rows_], *[_wide(r[...]) for r in consts_]).astype(BF16)
                scr[0][...] = av
                outs[-1][...] = av

            ats = [scr[0][...]]
        else:
            ats = [next(it)[...] for _ in a_list]
        p = None
        for at in ats:
            part = _dg(at, next(it)[...], 1, cb)
            p = part if p is None else p + part
        if add is not None:
            p = p + next(it)[...].astype(F32)
        if post:
            rows_ = [next(it) for _ in post_rows]
            consts_ = [next(it) for _ in post_consts]
            res = post_fn(p, *[_wide(r[...]) for r in rows_], *[_wide(r[...]) for r in consts_])
            for r, v in zip(outs[:len(post_outs)], res[:len(post_outs)]):
                r[...] = v.astype(r.dtype)
            for r, v in zip(outs[len(post_outs):], res[len(post_outs):]):
                @pl.when(i == 0)
                def _(r=r, v=v):
                    r[...] = v

                @pl.when(i > 0)
                def _(r=r, v=v):
                    r[...] += v
        else:
            outs[0][...] = p.astype(outs[0].dtype)
        if hook is not None:
            @pl.when((i == ni - 1) & (j == nj - 1))
            def _():
                hook.finish(*hargs)

    res = pl.pallas_call(
        body, name=name, grid=(ni, nj), in_specs=in_specs, out_specs=out_specs, out_shape=out_shape,
        scratch_shapes=scratch, input_output_aliases=aliases,
        compiler_params=pltpu.CompilerParams(dimension_semantics=("arbitrary", "arbitrary")),
    )(*args)
    if hook is not None:
        hook.done(res[n_out:])
    return res[:n_out]


def _mm_tiles(M, N, K, sa, sb, so, sadd):
    def tiles(d):
        return [t for t in range(LANES, min(d, 2048) + 1, LANES) if d % t == 0] or [d]

    best = None
    for tk in [K] + [t for t in tiles(K) if t < K]:
        for tm in tiles(M):
            for tn in tiles(N):
                vmem = 2 * (tm * tk * sa + tk * tn * sb + tm * tn * (so + sadd)) + (tm * tn * 4 if tk < K else 0)
                if vmem > MM_VMEM_BYTES or tm * tn > MM_MAX_OUT_TILE:
                    continue
                a_reads = 1 if tk == K else N // tn
                traffic = M * K * sa * a_reads + K * N * sb * (M // tm) + M * N * (so + sadd)
                steps = (M // tm) * (N // tn) * (K // tk)
                width = -(-tn // MXU_WIDTH) * MXU_WIDTH
                mxu = 2.0 * M * K * N * (width / tn) / MXU_FLOPS_PER_US
                edge = tm * tk * sa + tk * tn * sb + tm * tn * (so + sadd)
                cost = max(traffic / HBM_BYTES_PER_US, mxu) + steps * STEP_US + edge / HBM_BYTES_PER_US
                if best is None or cost < best[0]:
                    best = (cost, tm, tn, tk)
    assert best is not None, (M, N, K)
    return best[1:]


class _Hook:
    def __init__(self, inputs, out_shapes, aliases, scratch, start, finish, done):
        self.inputs, self.out_shapes, self.aliases, self.scratch = inputs, out_shapes, aliases, scratch
        self.start, self.finish, self.done = start, finish, done


class _Ctx:
    def __init__(self, first, last, row0, rows):
        self.first, self.last, self.row0, self.rows = first, last, row0, rows


def _rows(name, fn, ins, outs, accs=(), *, tm, nrows, ncol=1):
    nt = nrows // tm
    hb = tm // HALO
    nh = nrows // HALO
    ins = [(kind, arr, arr.shape[1] if kind == "row" and cw is None else cw, base) for kind, arr, cw, base in ins]
    in_specs, args = [], []
    for kind, arr, cw, base in ins:
        if kind == "row":
            in_specs.append(pl.BlockSpec((tm, cw), lambda j, i, base=base: (i, base + j)))
        elif kind == "prev":
            in_specs.append(pl.BlockSpec((HALO, cw), lambda j, i, base=base: (jnp.maximum(i * hb - 1, 0), base + j)))
        elif kind == "next":
            in_specs.append(pl.BlockSpec((HALO, cw), lambda j, i, base=base: (jnp.minimum((i + 1) * hb, nh - 1), base + j)))
        elif kind in ("const", "raw"):
            in_specs.append(pl.BlockSpec(arr.shape, lambda j, i, nd=arr.ndim: (0,) * nd))
        elif kind == "ccol":
            in_specs.append(pl.BlockSpec((arr.shape[0], cw), lambda j, i, base=base: (0, base + j)))
        else:
            raise ValueError(kind)
        args.append(arr)
    out_specs, out_shape = [], []
    for ctot, cw, base, dt in outs:
        out_specs.append(pl.BlockSpec((tm, cw), lambda j, i, base=base: (i, base + j)))
        out_shape.append(jax.ShapeDtypeStruct((nrows, ctot), dt))
    for r, ctot, cw in accs:
        out_specs.append(pl.BlockSpec((r, cw), lambda j, i: (0, j)))
        out_shape.append(jax.ShapeDtypeStruct((r, ctot), F32))
    n_in, n_out = len(ins), len(outs)

    def body(*refs):
        i = pl.program_id(1)
        in_refs, out_refs, acc_refs = refs[:n_in], refs[n_in:n_in + n_out], refs[n_in + n_out:]
        if acc_refs:
            @pl.when(i == 0)
            def _():
                for r in acc_refs:
                    r[...] = jnp.zeros_like(r)

        vals = [r[...] if s[0] == "raw" else _wide(r[...]) for r, s in zip(in_refs, ins)]
        res = fn(_Ctx(i == 0, i == nt - 1, i * tm, tm), *vals)
        for r, v in zip(out_refs, res[:n_out]):
            r[...] = v.astype(r.dtype)
        for r, v in zip(acc_refs, res[n_out:]):
            r[...] += v

    res = pl.pallas_call(
        body, name=name, grid=(ncol, nt), in_specs=in_specs, out_specs=out_specs, out_shape=out_shape,
        compiler_params=pltpu.CompilerParams(dimension_semantics=("arbitrary", "arbitrary")),
    )(*args)
    return res


def _shift_down(xcat, k):
    return xcat if k == 0 else pltpu.roll(xcat, k, 0)


def _shift_up(xcat, k):
    return xcat if k == 0 else pltpu.roll(xcat, xcat.shape[0] - k, 0)


def _with_prev(ctx, halo, x):
    return jnp.concatenate([jnp.where(ctx.first, 0.0, halo), x], axis=0)


def _with_next(ctx, x, halo):
    return jnp.concatenate([x, jnp.where(ctx.last, 0.0, halo)], axis=0)


def _rms_core(x, g):
    r = lax.rsqrt(jnp.mean(x * x, axis=-1, keepdims=True) + EPS)
    return x * r * g


def _rms_post(du, xv, drv, gv):
    _, vjp = jax.vjp(_rms_core, xv, gv)
    dx, dg = vjp(du)
    return [drv + dx, drv + dx, dg]


RMS_POST_OUTS = [(D, F32), (D, BF16)]


def _final_loss(x, target, g):
    S = x.shape[0]

    def fn(ctx, xv, tv, gv):
        def f(xx, gg):
            err = _rms_core(xx, gg) - tv
            return 0.5 * jnp.sum(err * err) / D

        loss, vjp = jax.vjp(f, xv, gv)
        dx, dg = vjp(jnp.ones((), F32))
        return [dx, dx, dg, jnp.zeros((1, LANES), F32) + loss]

    return _rows("final_loss", fn, [("row", x, None, 0), ("row", target, None, 0), ("const", g, None, 0)],
                 [(D, D, 0, F32), (D, D, 0, BF16)], [(1, D, D), (1, LANES, LANES)], tm=256, nrows=S)


def _attn_valid(n):
    qi = _iota((WIN, 2 * WIN), 0)
    kk = _iota((WIN, 2 * WIN), 1)
    rel = qi + WIN - kk
    return (rel >= 0) & (rel <= WIN) & ((kk >= WIN) | (n > 0))


def _attn_block(q, kp, kc, vp, vc, b0, b1):
    k = jnp.concatenate([kp, kc], axis=0)
    v = jnp.concatenate([vp, vc], axis=0)
    lo = _iota((WIN, LANES), 1) < HD
    scale = 1.0 / math.sqrt(HD)
    os_, ls_ = [], []
    for hh, b in ((0, b0), (1, b1)):
        qm = jnp.where(lo if hh == 0 else ~lo, q, 0.0)
        s = _bdot_nt(qm, k) * scale + b
        m = lax.stop_gradient(jnp.max(s, axis=1, keepdims=True))
        p = jnp.exp(s - m)
        l = jnp.sum(p, axis=1, keepdims=True)
        os_.append(_bdot_nn(p, v) / l)
        ls_.append(m + jnp.log(l))
    return jnp.where(lo, os_[0], os_[1]), jnp.where(lo, ls_[0], ls_[1])


def _residue_rows(r, d):
    return pl.ds(0, WIN) if d == 1 else pl.ds(r, WIN, stride=d)


def _for_residues(d, fn):
    if d == 1:
        fn(0, 0)
    else:
        lax.fori_loop(0, d, fn, 0, unroll=min(d, 8))


def _pairs_per_step(d):
    return 3 if d == 1 else 1


def _bias_table(rel_bias, bucket, gi, name):
    def body(t_ref, b_ref, o_ref):
        h = 6 * gi + pl.program_id(0)
        b = b_ref[...]
        acc = jnp.zeros(b.shape, F32)
        for k in range(REL_BUCKETS):
            acc = jnp.where(b == k, t_ref[k, h], acc)
        o_ref[0] = acc

    return pl.pallas_call(
        body, name=name, grid=(6,),
        in_specs=[pl.BlockSpec(memory_space=pltpu.SMEM), pl.BlockSpec((WIN, 2 * WIN), lambda h: (0, 0))],
        out_specs=pl.BlockSpec((1, WIN, 2 * WIN), lambda h: (h, 0, 0)),
        out_shape=jax.ShapeDtypeStruct((6, WIN, 2 * WIN), F32),
    )(rel_bias, bucket)


def _attn_fwd(pa, bias, gi, name):
    S = pa.shape[0]
    d = DILATIONS[gi]
    bt = WIN * d
    nb = S // bt
    hpw = _pairs_per_step(d)
    bw = hpw * LANES
    cb = 3 * gi // hpw

    def body(q_ref, kp_ref, kc_ref, vp_ref, vc_ref, b_ref, o_ref, l_ref):
        valid = _attn_valid(pl.program_id(1))
        bm = [jnp.where(valid, b_ref[k], NEG) for k in range(2 * hpw)]

        def residue(r, carry):
            sl = _residue_rows(r, d)
            for t in range(hpw):
                ln = pl.ds(t * LANES, LANES)
                o, lse = _attn_block(q_ref[sl, ln], kp_ref[sl, ln], kc_ref[sl, ln], vp_ref[sl, ln], vc_ref[sl, ln],
                                     bm[2 * t], bm[2 * t + 1])
                o_ref[sl, ln] = o
                l_ref[sl, ln] = lse
            return carry

        _for_residues(d, residue)

    def spec(off, prev):
        if prev:
            return pl.BlockSpec((bt, bw), lambda hp, n: (jnp.maximum(n - 1, 0), off // hpw + cb + hp))
        return pl.BlockSpec((bt, bw), lambda hp, n: (n, off // hpw + cb + hp))

    ospec = pl.BlockSpec((bt, bw), lambda hp, n: (n, hp))
    return pl.pallas_call(
        body, name=name, grid=(3 // hpw, nb),
        in_specs=[spec(0, False), spec(9, True), spec(9, False), spec(18, True), spec(18, False),
                  pl.BlockSpec((2 * hpw, WIN, 2 * WIN), lambda hp, n: (hp, 0, 0))],
        out_specs=[ospec, ospec],
        out_shape=[jax.ShapeDtypeStruct((S, GW), F32)] * 2,
        compiler_params=pltpu.CompilerParams(dimension_semantics=("parallel", "arbitrary")),
    )(pa, pa, pa, pa, pa, bias)


def _attn_bwd(pa, bias, do, dlse, db_in, dqkv, gi, name):
    S = pa.shape[0]
    d = DILATIONS[gi]
    bt = WIN * d
    nb = S // bt
    hpw = _pairs_per_step(d)
    bw = hpw * LANES
    cb = 3 * gi // hpw

    def body(q_ref, kp_ref, kc_ref, vp_ref, vc_ref, b_ref, do_ref, dl_ref, dbi_ref, dqi_ref, dki_ref, dvi_ref,
             dq_ref, dk_ref, dv_ref, db_ref, ck, cv):
        n = pl.program_id(1)

        @pl.when(n == 0)
        def _():
            db_ref[...] = dbi_ref[...]
            ck[...] = jnp.zeros_like(ck)
            cv[...] = jnp.zeros_like(cv)

        @pl.when(n < nb)
        def _():
            valid = _attn_valid(n)
            bm = [jnp.where(valid, b_ref[k], NEG) for k in range(2 * hpw)]

            def residue(r, carry):
                sl = _residue_rows(r, d)
                cs = pl.ds(pl.multiple_of(r * WIN, WIN), WIN)
                for t in range(hpw):
                    ln = pl.ds(t * LANES, LANES)
                    _, vjp = jax.vjp(_attn_block, q_ref[sl, ln], kp_ref[sl, ln], kc_ref[sl, ln], vp_ref[sl, ln],
                                     vc_ref[sl, ln], bm[2 * t], bm[2 * t + 1])
                    dq, dkp, dkc, dvp, dvc, db0, db1 = vjp((do_ref[sl, ln], dl_ref[sl, ln]))
                    dq_ref[sl, ln] = dq
                    dk_ref[sl, ln] = ck[cs, ln] + dkp
                    dv_ref[sl, ln] = cv[cs, ln] + dvp
                    ck[cs, ln] = dkc
                    cv[cs, ln] = dvc
                    db_ref[2 * t] += db0
                    db_ref[2 * t + 1] += db1
                return carry

            _for_residues(d, residue)

        @pl.when(n == nb)
        def _():
            def residue(r, carry):
                sl = _residue_rows(r, d)
                cs = pl.ds(pl.multiple_of(r * WIN, WIN), WIN)
                dk_ref[sl, :] = ck[cs, :]
                dv_ref[sl, :] = cv[cs, :]
                return carry

            _for_residues(d, residue)

    def cur(n):
        return jnp.minimum(n, nb - 1)

    def spec(off, prev):
        if prev:
            return pl.BlockSpec((bt, bw), lambda hp, n: (jnp.maximum(cur(n) - 1, 0), off // hpw + cb + hp))
        return pl.BlockSpec((bt, bw), lambda hp, n: (cur(n), off // hpw + cb + hp))

    gspec = pl.BlockSpec((bt, bw), lambda hp, n: (cur(n), hp))
    bspec = pl.BlockSpec((2 * hpw, WIN, 2 * WIN), lambda hp, n: (hp, 0, 0))
    qspec = pl.BlockSpec((bt, bw), lambda hp, n: (cur(n), cb + hp))
    kspec = pl.BlockSpec((bt, bw), lambda hp, n: (jnp.maximum(n - 1, 0), cb + hp))
    dq, dk, dv, db = pl.pallas_call(
        body, name=name, grid=(3 // hpw, nb + 1),
        in_specs=[spec(0, False), spec(9, True), spec(9, False), spec(18, True), spec(18, False),
                  bspec, gspec, gspec, bspec, _ANY, _ANY, _ANY],
        out_specs=[qspec, kspec, kspec, bspec],
        out_shape=[jax.ShapeDtypeStruct((S, AW), F32)] * 3 + [jax.ShapeDtypeStruct((6, WIN, 2 * WIN), F32)],
        scratch_shapes=[pltpu.VMEM((bt, bw), F32), pltpu.VMEM((bt, bw), F32)],
        input_output_aliases={9: 0, 10: 1, 11: 2},
        compiler_params=pltpu.CompilerParams(dimension_semantics=("arbitrary", "arbitrary")),
    )(pa, pa, pa, pa, pa, bias, do, dlse, db_in, *dqkv)
    return (dq, dk, dv), db


def _mix_core(o0, o1, o2, l0, l1, l2):
    m = lax.stop_gradient(jnp.maximum(jnp.maximum(l0, l1), l2))
    e0, e1, e2 = jnp.exp(l0 - m), jnp.exp(l1 - m), jnp.exp(l2 - m)
    return (e0 * o0 + e1 * o1 + e2 * o2) / (e0 + e1 + e2)


def _mix_fwd(os_, ls_, name):
    S = os_[0].shape[0]
    ins = [("row", a, None, 0) for a in (*os_, *ls_)]
    return _rows(name, lambda ctx, *v: [_mix_core(*v)], ins, [(GW, GW, 0, BF16)], tm=512, nrows=S)[0]


def _mix_bwd(os_, ls_, datt, name):
    S = datt.shape[0]

    def fn(ctx, *v):
        _, vjp = jax.vjp(_mix_core, *v[:6])
        return list(vjp(v[6]))

    ins = [("row", a, None, 0) for a in (*os_, *ls_, datt)]
    outs = [(GW, GW, 0, F32)] * 6
    r = _rows(name, fn, ins, outs, tm=512, nrows=S)
    return r[:3], r[3:]


def _t5_bucket(dist):
    max_exact = REL_BUCKETS // 2
    is_small = dist < max_exact
    nf = jnp.maximum(dist, 1).astype(F32)
    large = max_exact + (jnp.log(nf / max_exact) / math.log(REL_MAX_DISTANCE / max_exact)
                         * (REL_BUCKETS - max_exact)).astype(jnp.int32)
    large = jnp.minimum(large, REL_BUCKETS - 1)
    return jnp.where(is_small, dist, large)


def _buckets(d):
    qi = jnp.arange(WIN)[:, None]
    kk = jnp.arange(2 * WIN)[None, :]
    rel = qi + WIN - kk
    return _t5_bucket(jnp.clip(rel, 0, None) * d)


def _pool_cnt(ctx, w):
    pos = ctx.row0 + _iota((ctx.rows, PG), 0) + 1
    return jnp.minimum(pos, w).astype(F32)


def _pool_d(ctx, halo, u):
    ds = []
    for g, w in enumerate(POOL_WINDOWS):
        ug = u[:, g * PG:(g + 1) * PG]
        s = _with_prev(ctx, halo[:, g * PG:(g + 1) * PG], ug)
        step = 1
        while step < w:
            s = s + _shift_down(s, step)
            step *= 2
        ds.append(s[HALO:] / _pool_cnt(ctx, w) - ug)
    return ds


def _pool_fwd(pb, pw, scale, name):
    S = pb.shape[0]

    def fn(ctx, halo, u, w, sc):
        ds = _pool_d(ctx, halo, u)
        return [jnp.concatenate([_dg(ds[k], w[k], 1, 0) for k in range(4)], axis=1) * sc]

    return _rows(name, fn, [("prev", pb, D, 0), ("row", pb, None, 0), ("raw", pw, None, 0), ("const", scale, None, 0)],
                 [(D, D, 0, BF16)], tm=512, nrows=S)[0]


def _pool_bwd(pb, pw, scale, dpo, name):
    S = pb.shape[0]

    def fn1(ctx, halo, u, w, sc, dy):
        ds = _pool_d(ctx, halo, u)
        dyp = dy * sc
        y = jnp.concatenate([_dg(ds[k], w[k], 1, 0) for k in range(4)], axis=1)
        es, dws = [], []
        for k, wd in enumerate(POOL_WINDOWS):
            cols = slice(k * PG, (k + 1) * PG)
            es.append(_dg(dyp[:, cols], w[k], 1, 1) / _pool_cnt(ctx, wd))
            dws.append(_dg(ds[k], dyp[:, cols], 0, 0))
        return [jnp.concatenate(es, axis=1), jnp.concatenate(dws, axis=0), jnp.sum(dy * y, axis=0, keepdims=True)]

    e, dpw, dsc = _rows(name + "_a", fn1,
                        [("prev", pb, D, 0), ("row", pb, None, 0), ("raw", pw, None, 0), ("const", scale, None, 0),
                         ("row", dpo, None, 0)],
                        [(D, D, 0, F32)], [(4 * PG, PG, PG), (1, D, D)], tm=256, nrows=S)

    def fn2(ctx, ev, halo):
        outs = []
        for g, w in enumerate(POOL_WINDOWS):
            eg = ev[:, g * PG:(g + 1) * PG]
            s = _with_next(ctx, eg, halo[:, g * PG:(g + 1) * PG])
            step = 1
            while step < w:
                s = s + _shift_up(s, step)
                step *= 2
            outs.append(s[:ctx.rows] - eg * _pool_cnt(ctx, w))
        return [jnp.concatenate(outs, axis=1)]

    du = _rows(name + "_b", fn2, [("row", e, None, 0), ("next", e, D, 0)], [(D, D, 0, BF16)], tm=512, nrows=S)[0]
    return du, dpw, dsc


def _conv_taps(ctx, halo, x, K):
    cat = _with_prev(ctx, halo, x)
    return [_shift_down(cat, K - 1 - k)[HALO:] for k in range(K)]


def _conv_pre(taps, w, b):
    acc = b
    for k, t in enumerate(taps):
        acc = acc + t * _row_pick(w, k)
    return acc


CW = 256
CWS = 512
CONV_BWD_TILE = 256 * 1024
CONV_FWD_TILE = 1024 * 1024


def _ext_taps(ctx, prev, x, nxt, K):
    cat = jnp.concatenate([jnp.where(ctx.first, 0.0, prev), x, jnp.where(ctx.last, 0.0, nxt)], axis=0)
    return [_shift_down(cat, K - 1 - k)[HALO:] for k in range(K)]


def _conv_t_rows(dp, w, K, tm):
    acc = jnp.zeros((tm, dp.shape[1]), F32)
    for k in range(K):
        acc = acc + _shift_up(dp, K - 1 - k)[:tm] * _row_pick(w, k)
    return acc


def _ssd_conv_fwd(pc, w, b, name):
    S = pc.shape[0]
    base = D // CWS

    def fn(ctx, halo, x, wv, bv):
        return [_silu(_conv_pre(_conv_taps(ctx, halo, x, 4), wv, bv))]

    return _rows(name, fn, [("prev", pc, CWS, base), ("row", pc, CWS, base), ("ccol", w, CWS, 0), ("ccol", b, CWS, 0)],
                 [(XBC, CWS, 0, F32)], tm=min(S, CONV_FWD_TILE // CWS), nrows=S, ncol=XBC // CWS)[0]


def _ssd_conv_bwd(pc, w, b, dy, name):
    S = pc.shape[0]
    base = D // CWS

    def fn(ctx, prev, x, nxt, wv, bv, dyv, dyn):
        n = ctx.rows
        taps = _ext_taps(ctx, prev, x, nxt, 4)
        pre = _conv_pre(taps, wv, bv)
        sg = _sigmoid(pre)
        dye = jnp.concatenate([dyv, jnp.where(ctx.last, 0.0, dyn)], axis=0)
        dpre = dye * sg * (1.0 + pre * (1.0 - sg))
        dw = _stack_rows([jnp.sum(dpre[:n] * t[:n], axis=0, keepdims=True) for t in taps], 4)
        return [_conv_t_rows(dpre, wv, 4, n), dw, jnp.sum(dpre[:n], axis=0, keepdims=True)]

    return _rows(name, fn,
                 [("prev", pc, CWS, base), ("row", pc, CWS, base), ("next", pc, CWS, base), ("ccol", w, CWS, 0),
                  ("ccol", b, CWS, 0), ("row", dy, CWS, 0), ("next", dy, CWS, 0)],
                 [(XBC, CWS, 0, BF16)], [(4, XBC, CWS), (1, XBC, CWS)], tm=min(S, CONV_BWD_TILE // CWS), nrows=S,
                 ncol=XBC // CWS)


NFC = D_FF // CW


def _ffn_act_fwd(h, w, b, name):
    S = h.shape[0]

    def fn(ctx, ha, a, hv, v, wa, wv, ba, bv):
        pa = _conv_pre(_conv_taps(ctx, ha, a, 3), wa, ba)
        pv = _conv_pre(_conv_taps(ctx, hv, v, 3), wv, bv)
        return [_silu(pa) * pv]

    return _rows(name, fn,
                 [("prev", h, CW, 0), ("row", h, CW, 0), ("prev", h, CW, NFC), ("row", h, CW, NFC),
                  ("ccol", w, CW, 0), ("ccol", w, CW, NFC), ("ccol", b, CW, 0), ("ccol", b, CW, NFC)],
                 [(D_FF, CW, 0, BF16)], tm=min(S, CONV_FWD_TILE // CW), nrows=S, ncol=NFC)[0]


def _ffn_act_bwd(h, w, b, df, name):
    S = h.shape[0]

    def fn(ctx, pa_, a, na, pv_, v, nv, wa, wv, ba, bv, dfv, dfn):
        n = ctx.rows
        ta = _ext_taps(ctx, pa_, a, na, 3)
        tv = _ext_taps(ctx, pv_, v, nv, 3)
        pa = _conv_pre(ta, wa, ba)
        pv = _conv_pre(tv, wv, bv)
        sg = _sigmoid(pa)
        dfe = jnp.concatenate([dfv, jnp.where(ctx.last, 0.0, dfn)], axis=0)
        dpa = dfe * pv * sg * (1.0 + pa * (1.0 - sg))
        dpv = dfe * pa * sg
        res = [_conv_t_rows(dpa, wa, 3, n), _conv_t_rows(dpv, wv, 3, n)]
        for dp, taps in ((dpa, ta), (dpv, tv)):
            res.append(_stack_rows([jnp.sum(dp[:n] * t[:n], axis=0, keepdims=True) for t in taps], 3))
        for dp in (dpa, dpv):
            res.append(jnp.sum(dp[:n], axis=0, keepdims=True))
        return res

    ins = []
    for base in (0, NFC):
        ins += [("prev", h, CW, base), ("row", h, CW, base), ("next", h, CW, base)]
    ins += [("ccol", w, CW, 0), ("ccol", w, CW, NFC), ("ccol", b, CW, 0), ("ccol", b, CW, NFC),
            ("row", df, CW, 0), ("next", df, CW, 0)]
    dha, dhv, dwa, dwv, dba, dbv = _rows(
        name, fn, ins, [(D_FF, CW, 0, BF16)] * 2, [(3, D_FF, CW)] * 2 + [(1, D_FF, CW)] * 2,
        tm=min(S, CONV_BWD_TILE // CW), nrows=S, ncol=NFC)
    return dha, dhv, jnp.concatenate([dwa, dwv], axis=1), jnp.concatenate([dba, dbv], axis=1)


NSLAB = D // LANES
CPS = 2


def _ssd_chunk(xs, Bs, Cs, dtraw, dtb, alog, prev):
    lsz = SSD_CHUNK
    lane = _iota((lsz, LANES), 1)
    row = _iota((lsz, LANES), 0)
    dt = jnp.where(lane < SSD_HEADS, _softplus(dtraw + dtb), 0.0)
    a = dt * (-jnp.exp(alog))
    tril = row >= lane
    a_cs = _fdot(tril.astype(F32), a)
    a_cst = a_cs.T
    a_last = jnp.sum(a, axis=0, keepdims=True)
    lo = lane < HD
    top = row < HD
    cbs = [_bdot_nt(Cs[g], Bs[g]) for g in range(2)]
    ys, news = [], []
    for s in range(NSLAB):
        g = s // (NSLAB // 2)
        cols, lms, dts, als = [], [], [], []
        for hh in range(2):
            h = 2 * s + hh
            col = _lane_pick(a_cs, h)
            seg = col - _row_pick(a_cst, h)
            lms.append(jnp.exp(jnp.where(tril, seg, NEG)))
            cols.append(col)
            dts.append(_lane_pick(dt, h))
            als.append(_lane_pick(a_last, h))
        col_x = jnp.where(lo, cols[0], cols[1])
        al_x = jnp.where(lo, als[0], als[1])
        xc = xs[s] * jnp.where(lo, dts[0], dts[1])
        yd = jnp.where(lo, _bdot_nn(cbs[g] * lms[0], xc), _bdot_nn(cbs[g] * lms[1], xc))
        yoff = _bdot_nt(Cs[g], prev[s]) * jnp.exp(col_x)
        ys.append(yd + yoff)
        st = _bdot_tn(xc * jnp.exp(al_x - col_x), Bs[g])
        news.append(prev[s] * jnp.exp(jnp.where(top, als[0], als[1])) + st)
    return ys, news


def _ssd_scan_fwd(xbc_c, pd, dtb, alog, name):
    S = xbc_c.shape[0]
    nc = S // SSD_CHUNK
    rows_ = CPS * SSD_CHUNK

    def body(x_ref, b_ref, c_ref, dt_ref, dtb_ref, al_ref, y_ref, st_ref, state):
        c = pl.program_id(0)

        @pl.when(c == 0)
        def _():
            state[...] = jnp.zeros_like(state)

        prev = [state[s * LANES:(s + 1) * LANES, :] for s in range(NSLAB)]
        for u in range(CPS):
            rw = pl.ds(u * SSD_CHUNK, SSD_CHUNK)
            xs = [x_ref[rw, s * LANES:(s + 1) * LANES] for s in range(NSLAB)]
            Bs = [b_ref[rw, g * SSD_N:(g + 1) * SSD_N] for g in range(2)]
            Cs = [c_ref[rw, g * SSD_N:(g + 1) * SSD_N] for g in range(2)]
            for s in range(NSLAB):
                st_ref[u, s * LANES:(s + 1) * LANES, :] = prev[s]
            ys, prev = _ssd_chunk(xs, Bs, Cs, dt_ref[rw, :].astype(F32), dtb_ref[...], al_ref[...], prev)
            for s in range(NSLAB):
                y_ref[rw, s * LANES:(s + 1) * LANES] = ys[s]
        for s in range(NSLAB):
            state[s * LANES:(s + 1) * LANES, :] = prev[s]

    return pl.pallas_call(
        body, name=name, grid=(nc // CPS,),
        in_specs=[pl.BlockSpec((rows_, D), lambda c: (c, 0)),
                  pl.BlockSpec((rows_, 2 * SSD_N), lambda c: (c, D // (2 * SSD_N))),
                  pl.BlockSpec((rows_, 2 * SSD_N), lambda c: (c, D // (2 * SSD_N) + 1)),
                  pl.BlockSpec((rows_, LANES), lambda c: (c, 0)),
                  pl.BlockSpec((1, LANES), lambda c: (0, 0)), pl.BlockSpec((1, LANES), lambda c: (0, 0))],
        out_specs=[pl.BlockSpec((rows_, D), lambda c: (c, 0)), pl.BlockSpec((CPS, D, SSD_N), lambda c: (c, 0, 0))],
        out_shape=[jax.ShapeDtypeStruct((S, D), F32), jax.ShapeDtypeStruct((nc, D, SSD_N), F32)],
        scratch_shapes=[pltpu.VMEM((D, SSD_N), F32)],
        compiler_params=pltpu.CompilerParams(dimension_semantics=("arbitrary",)),
    )(xbc_c, xbc_c, xbc_c, pd, dtb, alog)


def _ssd_scan_bwd(xbc_c, pd, dtb, alog, states, dy, dxs_skip, name):
    S = xbc_c.shape[0]
    nc = S // SSD_CHUNK
    rows_ = CPS * SSD_CHUNK

    def body(x_ref, b_ref, c_ref, dt_ref, dtb_ref, al_ref, st_ref, dy_ref, sk_ref,
             dx_ref, ddt_ref, ddtb_ref, dal_ref, dstate):
        c = pl.program_id(0)

        @pl.when(c == 0)
        def _():
            dstate[...] = jnp.zeros_like(dstate)
            ddtb_ref[...] = jnp.zeros_like(ddtb_ref)
            dal_ref[...] = jnp.zeros_like(dal_ref)

        dnew = [dstate[s * LANES:(s + 1) * LANES, :] for s in range(NSLAB)]
        for u in reversed(range(CPS)):
            rw = pl.ds(u * SSD_CHUNK, SSD_CHUNK)
            xs = [x_ref[rw, s * LANES:(s + 1) * LANES] for s in range(NSLAB)]
            Bs = [b_ref[rw, g * SSD_N:(g + 1) * SSD_N] for g in range(2)]
            Cs = [c_ref[rw, g * SSD_N:(g + 1) * SSD_N] for g in range(2)]
            prev = [st_ref[u, s * LANES:(s + 1) * LANES, :] for s in range(NSLAB)]
            _, vjp = jax.vjp(_ssd_chunk, xs, Bs, Cs, dt_ref[rw, :].astype(F32), dtb_ref[...], al_ref[...], prev)
            dys = [dy_ref[rw, s * LANES:(s + 1) * LANES] for s in range(NSLAB)]
            dxs, dBs, dCs, ddt, ddtb, dal, dnew = vjp((dys, dnew))
            for s in range(NSLAB):
                dx_ref[rw, s * LANES:(s + 1) * LANES] = dxs[s] + sk_ref[rw, s * LANES:(s + 1) * LANES]
            for g in range(2):
                dx_ref[rw, D + g * SSD_N:D + (g + 1) * SSD_N] = dBs[g]
                dx_ref[rw, D + 2 * SSD_N + g * SSD_N:D + 2 * SSD_N + (g + 1) * SSD_N] = dCs[g]
            ddt_ref[rw, :] = ddt
            ddtb_ref[...] += ddtb
            dal_ref[...] += dal
        for s in range(NSLAB):
            dstate[s * LANES:(s + 1) * LANES, :] = dnew[s]

    def rv(c):
        return nc // CPS - 1 - c

    return pl.pallas_call(
        body, name=name, grid=(nc // CPS,),
        in_specs=[pl.BlockSpec((rows_, D), lambda c: (rv(c), 0)),
                  pl.BlockSpec((rows_, 2 * SSD_N), lambda c: (rv(c), D // (2 * SSD_N))),
                  pl.BlockSpec((rows_, 2 * SSD_N), lambda c: (rv(c), D // (2 * SSD_N) + 1)),
                  pl.BlockSpec((rows_, LANES), lambda c: (rv(c), 0)),
                  pl.BlockSpec((1, LANES), lambda c: (0, 0)), pl.BlockSpec((1, LANES), lambda c: (0, 0)),
                  pl.BlockSpec((CPS, D, SSD_N), lambda c: (rv(c), 0, 0)),
                  pl.BlockSpec((rows_, D), lambda c: (rv(c), 0)),
                  pl.BlockSpec((rows_, D), lambda c: (rv(c), 0))],
        out_specs=[pl.BlockSpec((rows_, XBC), lambda c: (rv(c), 0)),
                   pl.BlockSpec((rows_, LANES), lambda c: (rv(c), 0)),
                   pl.BlockSpec((1, LANES), lambda c: (0, 0)), pl.BlockSpec((1, LANES), lambda c: (0, 0))],
        out_shape=[jax.ShapeDtypeStruct((S, XBC), F32), jax.ShapeDtypeStruct((S, LANES), F32),
                   jax.ShapeDtypeStruct((1, LANES), F32), jax.ShapeDtypeStruct((1, LANES), F32)],
        scratch_shapes=[pltpu.VMEM((D, SSD_N), F32)],
        compiler_params=pltpu.CompilerParams(dimension_semantics=("arbitrary",)),
    )(xbc_c, xbc_c, xbc_c, pd, dtb, alog, states, dy, dxs_skip)


def _ssd_post_core(y, xs, z, d128, nw):
    tm = y.shape[0]
    ex = (_iota((LANES, D), 1) // HD == _iota((LANES, D), 0)).astype(F32)
    d_x = jnp.sum(_fdot(jnp.broadcast_to(d128, (8, LANES)), ex), axis=0, keepdims=True) * 0.125
    y2 = (y + d_x * xs) * _silu(z)
    lo = _iota((tm, D), 1) < D // 2
    sq = y2 * y2
    ms0 = jnp.sum(jnp.where(lo, sq, 0.0), axis=-1, keepdims=True) / (D // 2)
    ms1 = jnp.sum(jnp.where(lo, 0.0, sq), axis=-1, keepdims=True) / (D // 2)
    r = jnp.where(lo, lax.rsqrt(ms0 + EPS), lax.rsqrt(ms1 + EPS))
    return y2 * r * nw


def _ssd_post_ins(y, xbc_c, pc, d128, nw):
    return [("row", y, None, 0), ("row", xbc_c, D, 0), ("row", pc, D, 0), ("const", d128, None, 0), ("const", nw, None, 0)]


def _ssd_post_fwd(y, xbc_c, pc, d128, nw, name):
    S = y.shape[0]
    return _rows(name, lambda ctx, *v: [_ssd_post_core(*v)], _ssd_post_ins(y, xbc_c, pc, d128, nw),
                 [(D, D, 0, BF16)], tm=256, nrows=S)[0]


def _ssd_post_bwd(y, xbc_c, pc, d128, nw, dout, name):
    S = y.shape[0]

    def fn(ctx, *v):
        _, vjp = jax.vjp(_ssd_post_core, *v[:5])
        return list(vjp(v[5]))

    return _rows(name, fn, _ssd_post_ins(y, xbc_c, pc, d128, nw) + [("row", dout, None, 0)],
                 [(D, D, 0, F32), (D, D, 0, F32), (D, D, 0, BF16)], [(1, LANES, LANES), (1, D, D)], tm=256, nrows=S)


def _gates_core(g0, g1, g2, b0, b1, b2, ya, yb, yc):
    return _sigmoid(g0 + b0) * ya + _sigmoid(g1 + b1) * yb + _sigmoid(g2 + b2) * yc


def _gate_parts(pdv, bv):
    gp = pltpu.roll(pdv, SEC_D - 16, 1)
    return [gp[:, k * D:(k + 1) * D] for k in range(3)] + [bv[:, k * D:(k + 1) * D] for k in range(3)]


def _gates_fwd(pd, bg, ya, yb, yc, name):
    S = pd.shape[0]

    def fn(ctx, pdv, bv, a, b, c):
        return [_gates_core(*_gate_parts(pdv, bv), a, b, c)]

    return _rows(name, fn, [("row", pd, None, 0), ("const", bg, None, 0), ("row", ya, None, 0), ("row", yb, None, 0),
                            ("row", yc, None, 0)], [(D, D, 0, BF16)], tm=512, nrows=S)[0]


def _gates_post(dm, pdv, a, b, c, bv):
    _, vjp = jax.vjp(_gates_core, *_gate_parts(pdv, bv), a, b, c)
    g = vjp(dm)
    return [g[6], g[7], g[8], jnp.concatenate(g[0:3], axis=1), jnp.concatenate(g[3:6], axis=1)]


def _adam_update(wv, gv, mv, vv):
    m2 = ADAM_B1 * mv + (1.0 - ADAM_B1) * gv
    v2 = ADAM_B2 * vv + (1.0 - ADAM_B2) * jnp.square(gv)
    m_hat = m2 / (1.0 - ADAM_B1 ** ADAM_STEP)
    v_hat = v2 / (1.0 - ADAM_B2 ** ADAM_STEP)
    delta = -ADAM_LR * (m_hat / (jnp.sqrt(v_hat) + ADAM_EPS) + ADAM_WD * wv)
    return [delta, m2, v2]


def _adamw(w, g, m, v, name):
    rows, C = w.shape
    tm = _pick(rows, [t for t in (512, 256, 128, 64, 32, 16, 8) if t * C <= ADAM_TILE])
    return _rows(name, lambda ctx, *a: _adam_update(*a), [("row", a, None, 0) for a in (w, g, m, v)],
                 [(C, C, 0, F32)] * 3, tm=tm, nrows=rows)


def _position():
    return lax.axis_index("x"), lax.axis_index("y"), lax.axis_index("c")


def _other_chips(x, y):
    return [(1 - x, y), (x, 1 - y), (1 - x, 1 - y)]


_HBM = pl.BlockSpec(memory_space=pltpu.HBM)


def _gather_parts(half, lo, n):
    def copies(p_ref, out_ref, send_sems, recv_sems):
        x, y, c = _position()
        sibling = (x, y, 1 - c)
        chips = _other_chips(x, y)

        def slab(chip, h):
            return out_ref.at[2 * chip[0] + chip[1], pl.ds(h * half + lo, n), :]

        def copy(k, src, dst, to):
            return pltpu.make_async_remote_copy(src_ref=src, dst_ref=dst, send_sem=send_sems.at[k],
                                                recv_sem=recv_sems.at[k], device_id=to, device_id_type=MESH)

        first = [copy(j, p_ref.at[pl.ds(c * half + lo, n), :], slab((x, y), c), (*chip, c)) for j, chip in enumerate(chips)]
        passed = [copy(3 + j, slab(chip, c), slab(chip, c), sibling) for j, chip in enumerate(chips)]
        from_chips = [copy(j, slab(chip, c), slab(chip, c), (x, y, c)) for j, chip in enumerate(chips)]
        from_sibling = [copy(3 + j, slab(chip, 1 - c), slab(chip, 1 - c), (x, y, c)) for j, chip in enumerate(chips)]
        return first, passed, from_chips, from_sibling

    def start(ins, outs, scr):
        for cp in copies(ins[0], outs[0], *scr)[0]:
            cp.start()

    def finish(ins, outs, scr):
        first, passed, from_chips, from_sibling = copies(ins[0], outs[0], *scr)
        for j in range(3):
            from_chips[j].wait_recv()
            passed[j].start()
        for cp in from_sibling:
            cp.wait_recv()
        for cp in first + passed:
            cp.wait_send()

    return start, finish


def _rs_chip_parts(lo, n):
    def copies(h_ref, out_ref, send_sems, recv_sems):
        x, y, c = _position()
        return [pltpu.make_async_remote_copy(src_ref=h_ref.at[2 * chip[0] + chip[1], pl.ds(lo, n), :],
                                             dst_ref=out_ref.at[j, pl.ds(lo, n), :],
                                             send_sem=send_sems.at[j], recv_sem=recv_sems.at[j],
                                             device_id=(*chip, c), device_id_type=MESH)
                for j, chip in enumerate(_other_chips(x, y))]

    def start(ins, outs, scr):
        for cp in copies(ins[0], outs[0], *scr):
            cp.start()

    def finish(ins, outs, scr):
        for cp in copies(ins[0], outs[0], *scr):
            cp.wait()

    return start, finish


class _Stream:
    def __init__(self, src, buf, parts, nsem, units, name):
        self.src, self.buf, self.parts, self.nsem, self.name = src, buf, parts, nsem, name
        self.next, self.units = 0, units

    def _scratch(self):
        return [pltpu.SemaphoreType.DMA((self.nsem,)), pltpu.SemaphoreType.DMA((self.nsem,))]

    def _take(self, units):
        units = min(units, self.units - self.next)
        lo = self.next * 16
        self.next += units
        return lo, units * 16

    def _set(self, outs):
        self.buf = outs[0]

    def hook(self, units):
        lo, n = self._take(units)
        if n == 0:
            return None
        start, finish = self.parts(lo, n)
        return _Hook([self.src, self.buf], [jax.ShapeDtypeStruct(self.buf.shape, self.buf.dtype)], {1: 0},
                     self._scratch(), start, finish, self._set)

    def drain(self):
        lo, n = self._take(self.units)
        if n:
            start, finish = self.parts(lo, n)

            def body(s_ref, b_ref, o_ref, send_sems, recv_sems):
                args = ((s_ref, b_ref), (o_ref,), (send_sems, recv_sems))
                start(*args)
                finish(*args)

            self.buf = pl.pallas_call(
                body, name=self.name, in_specs=[_ANY, _ANY], out_specs=_ANY,
                out_shape=jax.ShapeDtypeStruct(self.buf.shape, self.buf.dtype),
                scratch_shapes=self._scratch(), input_output_aliases={1: 0},
            )(self.src, self.buf)
        return self.buf


def _rs_pair_parts(half, lo, n):
    def copy(g_ref, out_ref, send_sems, recv_sems):
        x, y, c = _position()
        return pltpu.make_async_remote_copy(
            src_ref=g_ref.at[pl.ds(0, 4), pl.ds((1 - c) * half + lo, n), :], dst_ref=out_ref.at[pl.ds(0, 4), pl.ds(lo, n), :],
            send_sem=send_sems.at[0], recv_sem=recv_sems.at[0], device_id=(x, y, 1 - c), device_id_type=MESH)

    def start(ins, outs, scr):
        copy(ins[0], outs[0], *scr).start()

    def finish(ins, outs, scr):
        copy(ins[0], outs[0], *scr).wait()

    return start, finish


def _rs_swap(r, name):
    Rh, C = r.shape

    def body(r_ref, out_ref, send_sem, recv_sem):
        x, y, c = _position()
        cp = pltpu.make_async_remote_copy(src_ref=r_ref, dst_ref=out_ref, send_sem=send_sem,
                                          recv_sem=recv_sem, device_id=(x, y, 1 - c), device_id_type=MESH)
        cp.start()
        cp.wait()

    return pl.pallas_call(
        body, name=name, in_specs=[_HBM], out_specs=_HBM,
        out_shape=jax.ShapeDtypeStruct((Rh, C), r.dtype),
        scratch_shapes=[pltpu.SemaphoreType.DMA, pltpu.SemaphoreType.DMA],
    )(r)


def _rs_add_pair(g, recv, cidx, name):
    _, R, C = g.shape
    Rh = R // 2
    tm = _pick(Rh, (400, 280, 200, 160, 80, 40, 16, 8))
    nt = Rh // tm

    def body(c_ref, g_ref, r_ref, o_ref):
        o_ref[...] = (g_ref[...].astype(F32) + r_ref[...].astype(F32)).astype(o_ref.dtype)

    return pl.pallas_call(
        body, name=name,
        grid_spec=pltpu.PrefetchScalarGridSpec(
            num_scalar_prefetch=1, grid=(4, nt),
            in_specs=[pl.BlockSpec((1, tm, C), lambda k, i, cr: (k, cr[0] * nt + i, 0)),
                      pl.BlockSpec((1, tm, C), lambda k, i, cr: (k, i, 0))],
            out_specs=pl.BlockSpec((1, tm, C), lambda k, i, cr: (k, i, 0))),
        out_shape=jax.ShapeDtypeStruct((4, Rh, C), BF16),
    )(cidx, g, recv)


def _rs_add_chips(h, recv, chip_idx, name):
    _, Rh, C = h.shape
    tm = _pick(Rh, (400, 280, 200, 160, 80, 40, 16, 8))

    def body(c_ref, h_ref, r_ref, o_ref):
        acc = h_ref[0].astype(F32)
        for j in range(3):
            acc = acc + r_ref[j].astype(F32)
        o_ref[...] = acc

    return pl.pallas_call(
        body, name=name,
        grid_spec=pltpu.PrefetchScalarGridSpec(
            num_scalar_prefetch=1, grid=(Rh // tm,),
            in_specs=[pl.BlockSpec((1, tm, C), lambda i, cr: (cr[0], i, 0)), pl.BlockSpec((3, tm, C), lambda i, cr: (0, i, 0))],
            out_specs=pl.BlockSpec((tm, C), lambda i, cr: (i, 0))),
        out_shape=jax.ShapeDtypeStruct((Rh, C), F32),
    )(chip_idx, h, recv)


def _all_reduce_small(vec, name):
    n, C = vec.shape

    def body(v_ref, out_ref, buf, send_sems, recv_sems):
        x, y, c = _position()

        def flip(k):
            return ((1 - x) if k & 4 else x, (1 - y) if k & 2 else y, (1 - c) if k & 1 else c)

        def idx(p):
            return 4 * p[0] + 2 * p[1] + p[2]

        me = idx((x, y, c))
        buf[me] = v_ref[...]
        cps = [pltpu.make_async_remote_copy(src_ref=v_ref, dst_ref=buf.at[me], send_sem=send_sems.at[k - 1],
                                            recv_sem=recv_sems.at[k - 1], device_id=flip(k), device_id_type=MESH)
               for k in range(1, 8)]
        for cp in cps:
            cp.start()
        for k in range(1, 8):
            pltpu.make_async_remote_copy(src_ref=v_ref, dst_ref=buf.at[idx(flip(k))], send_sem=send_sems.at[k - 1],
                                         recv_sem=recv_sems.at[k - 1], device_id=flip(k), device_id_type=MESH).wait_recv()
        for cp in cps:
            cp.wait_send()
        acc = buf[0]
        for s in range(1, 8):
            acc = acc + buf[s]
        out_ref[...] = acc

    return pl.pallas_call(
        body, name=name,
        in_specs=[pl.BlockSpec(memory_space=pltpu.VMEM)], out_specs=pl.BlockSpec(memory_space=pltpu.VMEM),
        out_shape=jax.ShapeDtypeStruct((n, C), F32),
        scratch_shapes=[pltpu.VMEM((8, n, C), F32), pltpu.SemaphoreType.DMA((7,)), pltpu.SemaphoreType.DMA((7,))],
    )(vec)


BIG = (("w_in", (D, IN_WIDTH // 4), "cols"), ("w_a", (GW, D // 4), "cols"), ("pool_w", (4, PG // 4, PG), "pool"),
       ("w_b", (D // 4, D), "rows"), ("w_c", (D // 4, D), "rows"), ("w_o", (D // 4, D), "rows"),
       ("ffn_w_up", (D, 2 * D_FF // 4), "cols"), ("ffn_w_down", (D_FF // 4, D), "rows"))
def _pack_rows(s):
    k = math.prod(s) // D
    return -(-k // 16) * 16, k


PACK_ROWS = sum(_pack_rows(s)[0] for _, s, _ in BIG)
PACK_PAD = -(-PACK_ROWS // 32) * 32


def _pad_rows(v, rows):
    pad = [(0, 0)] * v.ndim
    pad[-2] = (0, rows - v.shape[-2])
    return jnp.pad(v, pad) if rows > v.shape[-2] else v


def _pack_blocks(blocks, dtype):
    lead = blocks["w_in"].shape[:-2]
    flat = []
    for n, s, how in BIG:
        v = blocks[n].astype(dtype)
        if how == "cols":
            v = jnp.swapaxes(v, -1, -2)
        flat.append(_pad_rows(v.reshape(*lead, -1, D), _pack_rows(s)[0]))
    flat.append(jnp.zeros((*lead, PACK_PAD - PACK_ROWS, D), dtype))
    return jnp.concatenate(flat, axis=-2)


def _unpack_blocks(pack):
    out, r = {}, 0
    for n, s, how in BIG:
        rows, k = _pack_rows(s)
        v = pack[r:r + k, :]
        out[n] = v.reshape(s[1], s[0]).T if how == "cols" else v.reshape(s)
        r += rows
    return out


def _operands(allp):
    out, r = {}, 0
    for n, s, how in BIG:
        rows, k = _pack_rows(s)
        v = allp[:, r:r + k, :]
        if how == "cols":
            out[n] = v.reshape(4 * s[1], s[0])
        elif how == "rows":
            out[n] = v.reshape(4 * s[0], s[1])
        else:
            out[n] = v.reshape(4, *s).transpose(1, 0, 2, 3).reshape(4, PG, PG)
        r += rows
    return out


def _pack_operands(g, dtype):
    flat = []
    for n, s, how in BIG:
        v = g[n].astype(dtype)
        if how == "pool":
            v = v.reshape(4, 4, s[1], s[2]).transpose(1, 0, 2, 3)
        flat.append(_pad_rows(v.reshape(4, -1, D), _pack_rows(s)[0]))
    flat.append(jnp.zeros((4, PACK_PAD - PACK_ROWS, D), dtype))
    return jnp.concatenate(flat, axis=1)


def _layer_fwd(x, w, sm, bias, hk):
    pa, u = _mmf(None, w["in_a"], tb=True, pre=(_rms_core, [x], [sm["ln1_g"]]), name="in_a", tm=1024, hook=hk("in_a"))
    pb = _mm(u, w["in_b"], tb=True, out_dtype=BF16, name="in_b", hook=hk("in_b"))
    pc = _mm(u, w["in_c"], tb=True, out_dtype=BF16, name="in_c", hook=hk("in_c"))
    pd = _mm(u, w["in_d"], tb=True, out_dtype=BF16, name="in_d", hook=hk("in_d"))
    os_, ls_ = [], []
    for gi in range(3):
        o, l = _attn_fwd(pa, bias[gi], gi, "attn_fwd%d" % gi)
        os_.append(o)
        ls_.append(l)
    att = _mix_fwd(os_, ls_, "mix_fwd")
    ya = _mm(att, w["w_a"], tb=True, out_dtype=BF16, name="mm_wa")
    pool_o = _pool_fwd(pb, w["pool_w"], sm["pool_scale"], "pool_fwd")
    yb = _mm(pool_o, w["w_b"], out_dtype=BF16, name="mm_wb")
    xbc_c = _ssd_conv_fwd(pc, sm["ssd_conv_w"], sm["ssd_conv_b"], "ssd_conv_fwd")
    y_scan, states = _ssd_scan_fwd(xbc_c, pd, sm["ssd_dt_bias"], sm["ssd_a_log"], "ssd_scan_fwd")
    ssd_o = _ssd_post_fwd(y_scan, xbc_c, pc, sm["ssd_d"], sm["ssd_norm_w"], "ssd_post_fwd")
    yc = _mm(ssd_o, w["w_c"], out_dtype=BF16, name="mm_wc")
    merged = _gates_fwd(pd, sm["b_gate"], ya, yb, yc, "gates_fwd")
    x1 = _mm(merged, w["w_o"], add=x, name="mm_wo", hook=hk("mm_wo"))
    h, u2 = _mmf(None, w["ffn_w_up"], tb=True, pre=(_rms_core, [x1], [sm["ln2_g"]]), out_dtype=BF16, name="mm_up",
                 tm=1024, hook=hk("mm_up"))
    f = _ffn_act_fwd(h, sm["ffn_conv_w"], sm["ffn_conv_b"], "ffn_act_fwd")
    x2 = _mm(f, w["ffn_w_down"], add=x1, name="mm_down", hook=hk("mm_down"))
    saved = dict(x=x, u=u, pa=pa, pb=pb, pc=pc, pd=pd, os=os_, ls=ls_, att=att, ya=ya, yb=yb, yc=yc, pool_o=pool_o,
                 xbc_c=xbc_c, y_scan=y_scan, states=states, ssd_o=ssd_o, merged=merged, x1=x1, u2=u2, h=h, f=f)
    return x2, saved


def _layer_bwd(dx2, dx2b, w, sm, bias, dbs, sv, hk):
    gw, gs = {}, {}
    S = dx2.shape[0]

    def gmm(a, b, name):
        return _mm(a, b, ta=True, out_dtype=BF16, name=name, hook=hk(name))

    df = _mm(dx2b, w["ffn_w_down"], tb=True, out_dtype=BF16, name="d_f", hook=hk("d_f"))
    gw["ffn_w_down"] = gmm(sv["f"], dx2b, "g_down")
    dha, dhv, gs["ffn_conv_w"], gs["ffn_conv_b"] = _ffn_act_bwd(sv["h"], sm["ffn_conv_w"], sm["ffn_conv_b"], df, "ffn_act_bwd")
    dx1, dx1b, gs["ln2_g"] = _mmf([dha, dhv], [w["up_a"], w["up_v"]], name="d_u2_v", tm=256, hook=hk("d_u2_v"),
                                  post=(_rms_post, [sv["x1"], dx2], [sm["ln2_g"]], RMS_POST_OUTS, [(1, D)]))
    gw["ffn_w_up"] = jnp.concatenate([gmm(dha, sv["u2"], "g_up_a"), gmm(dhv, sv["u2"], "g_up_v")], axis=0)
    dya, dyb, dyc, dgate, gs["b_gate"] = _mmf(
        dx1b, w["w_o"], tb=True, name="d_merged", tm=256, hook=hk("d_merged"),
        post=(_gates_post, [sv["pd"], sv["ya"], sv["yb"], sv["yc"]], [sm["b_gate"]],
              [(D, BF16)] * 3 + [(3 * D, BF16)], [(1, 3 * D)]))
    gw["w_o"] = gmm(sv["merged"], dx1b, "g_wo")
    dssd_o = _mm(dyc, w["w_c"], tb=True, name="d_ssd_o")
    gw["w_c"] = gmm(sv["ssd_o"], dyc, "g_wc")
    dy_scan, dxs_skip, dz, gs["ssd_d"], gs["ssd_norm_w"] = _ssd_post_bwd(
        sv["y_scan"], sv["xbc_c"], sv["pc"], sm["ssd_d"], sm["ssd_norm_w"], dssd_o, "ssd_post_bwd")
    dxbc_c, ddt, gs["ssd_dt_bias"], gs["ssd_a_log"] = _ssd_scan_bwd(
        sv["xbc_c"], sv["pd"], sm["ssd_dt_bias"], sm["ssd_a_log"], sv["states"], dy_scan, dxs_skip, "ssd_scan_bwd")
    dxbc, gs["ssd_conv_w"], gs["ssd_conv_b"] = _ssd_conv_bwd(sv["pc"], sm["ssd_conv_w"], sm["ssd_conv_b"], dxbc_c, "ssd_conv_bwd")
    dpool_o = _mm(dyb, w["w_b"], tb=True, name="d_pool_o")
    gw["w_b"] = gmm(sv["pool_o"], dyb, "g_wb")
    dpb, dpw, gs["pool_scale"] = _pool_bwd(sv["pb"], w["pool_w"], sm["pool_scale"], dpool_o, "pool_bwd")
    gw["pool_w"] = dpw.reshape(4, PG, PG)
    datt = _mm(dya, w["w_a"], name="d_att")
    gw["w_a"] = gmm(dya, sv["att"], "g_wa")
    dos, dls = _mix_bwd(sv["os"], sv["ls"], datt, "mix_bwd")
    dqkv = tuple(lax.empty((S, AW), F32) for _ in range(3))
    dbs = list(dbs)
    for gi in range(3):
        dqkv, dbs[gi] = _attn_bwd(sv["pa"], bias[gi], dos[gi], dls[gi], dbs[gi], dqkv, gi, "attn_bwd%d" % gi)
    u = sv["u"]
    pieces = [(dqkv[0], "wq"), (dqkv[1], "wk"), (dqkv[2], "wv"), (dpb, "in_b"), (dz, "wz"), (dxbc, "wxbc"),
              (ddt, "wdt"), (dgate, "wgate")]
    du = _mmf([dp for dp, _ in pieces[:4]], [w[key] for _, key in pieces[:4]], name="d_u_a", tm=256, hook=hk("d_u_a"))[0]
    dx, dxb, gs["ln1_g"] = _mmf([dp for dp, _ in pieces[4:]], [w[key] for _, key in pieces[4:]], add=du,
                                name="d_u_wgate", tm=256, hook=hk("d_u_wgate"),
                                post=(_rms_post, [sv["x"], dx1], [sm["ln1_g"]], RMS_POST_OUTS, [(1, D)]))
    g_in = []
    for dp, key in pieces:
        g = gmm(dp, u, "g_in_" + key)
        g_in.append(g[:SSD_HEADS] if key == "wdt" else g)
    gw["w_in"] = jnp.concatenate(g_in, axis=0)
    return dx, dxb, gw, gs, dbs


SMALL_LAYER = ("ln1_g", "b_gate", "pool_scale", "ssd_conv_w", "ssd_conv_b", "ssd_dt_bias", "ssd_a_log", "ssd_d",
               "ssd_norm_w", "ln2_g", "ffn_conv_w", "ffn_conv_b")


def _pad_lanes(v):
    return jnp.pad(v, (0, LANES - v.shape[0])).reshape(1, LANES)


def _layer_weights(ops):
    wt = ops["w_in"]
    o1, o2, o3 = SEC_A, SEC_A + SEC_B, SEC_A + SEC_B + SEC_C
    w = dict(ops)
    w["in_a"] = jnp.pad(wt[:o1], ((0, SEC_A_PAD - o1), (0, 0)))
    w["in_b"] = wt[o1:o2]
    w["in_c"] = wt[o2:o3]
    w["in_d"] = jnp.pad(wt[o3:], ((0, SEC_D - (IN_WIDTH - o3)), (0, 0)))
    w["wq"], w["wk"], w["wv"] = wt[:AW], wt[AW:2 * AW], wt[2 * AW:o1]
    w["wz"], w["wxbc"] = wt[o2:o2 + D], wt[o2 + D:o3]
    w["wdt"] = jnp.pad(wt[o3:o3 + SSD_HEADS], ((0, LANES - SSD_HEADS), (0, 0)))
    w["wgate"] = wt[o3 + SSD_HEADS:]
    w["up_a"], w["up_v"] = ops["ffn_w_up"][:D_FF], ops["ffn_w_up"][D_FF:]
    return w


def _layer_small(p, i):
    sm = {n: p[n][i] for n in SMALL_LAYER}
    out = {}
    for n, v in sm.items():
        if n in ("ssd_dt_bias", "ssd_a_log", "ssd_d"):
            out[n] = _pad_lanes(v)
        elif v.ndim == 1:
            out[n] = v.reshape(1, -1)
        else:
            out[n] = v
    return out


def _local_step(x, target, rel_bias, final_g, layer_full, small, fwd_hooks=None, bwd_hooks=None, after_bwd=None):
    nl = small["ln1_g"].shape[0]
    buckets = [_buckets(d).astype(jnp.int32) for d in DILATIONS]
    bias = [_bias_table(rel_bias, buckets[gi], gi, "bias_table%d" % gi) for gi in range(3)]
    no_hooks = lambda i: (lambda name: None)
    fwd_hooks = fwd_hooks or no_hooks
    bwd_hooks = bwd_hooks or no_hooks
    saved, ws, sms = [], [], []
    h = x
    for i in range(nl):
        w = _layer_weights(layer_full(i))
        sm = _layer_small(small, i)
        h, sv = _layer_fwd(h, w, sm, bias, fwd_hooks(i))
        saved.append(sv)
        ws.append(w)
        sms.append(sm)
    dh, dhb, dfinal, loss = _final_loss(h, target, final_g.reshape(1, D))
    gws, gss = [None] * nl, [None] * nl
    dbs = [jnp.zeros((6, WIN, 2 * WIN), F32)] * 3
    for i in reversed(range(nl)):
        dh, dhb, gws[i], gss[i], dbs = _layer_bwd(dh, dhb, ws[i], sms[i], bias, dbs, saved[i], bwd_hooks(i))
        if after_bwd is not None:
            after_bwd(i, gws[i])
    drel = []
    for gi in range(3):
        onehot = jnp.pad(jax.nn.one_hot(buckets[gi].reshape(-1), REL_BUCKETS, dtype=BF16), ((0, 0), (0, LANES - REL_BUCKETS)))
        drel.append(_mm(dbs[gi].reshape(6, WIN * 2 * WIN), onehot, name="g_relb"))
    return loss, dh, gws, gss, dfinal, jnp.concatenate(drel, axis=0)


WEIGHTS = ("rel_bias", "ln1_g", "w_in", "b_gate", "w_a", "pool_w", "pool_scale", "w_b", "ssd_conv_w", "ssd_conv_b",
           "ssd_dt_bias", "ssd_a_log", "ssd_d", "ssd_norm_w", "w_c", "w_o", "ln2_g", "ffn_w_up", "ffn_conv_w",
           "ffn_conv_b", "ffn_w_down", "final_g")
BIG_NAMES = tuple(n for n, _, _ in BIG)
SHARDED_SMALL = {"ssd_conv_w": XBC // 4, "ffn_conv_w": 2 * D_FF // 4}


def _to_rows(flat):
    n = flat.shape[0]
    rows = -(-n // LANES)
    rows = -(-rows // 8) * 8
    return jnp.pad(flat, (0, rows * LANES - n)).reshape(rows, LANES)


def _flatten(tree, names):
    return jnp.concatenate([tree[n].reshape(-1) for n in names])


def _unflatten(flat, shapes, names):
    out, o = {}, 0
    for n in names:
        k = math.prod(shapes[n])
        out[n] = flat[o:o + k].reshape(shapes[n])
        o += k
    return out


def kernel(x, rel_bias, ln1_g, w_in, b_gate, w_a, pool_w, pool_scale, w_b, ssd_conv_w, ssd_conv_b, ssd_dt_bias, ssd_a_log, ssd_d, ssd_norm_w, w_c, w_o, ln2_g, ffn_w_up, ffn_conv_w, ffn_conv_b, ffn_w_down, final_g, loss_target, m_rel_bias, m_ln1_g, m_w_in, m_b_gate, m_w_a, m_pool_w, m_pool_scale, m_w_b, m_ssd_conv_w, m_ssd_conv_b, m_ssd_dt_bias, m_ssd_a_log, m_ssd_d, m_ssd_norm_w, m_w_c, m_w_o, m_ln2_g, m_ffn_w_up, m_ffn_conv_w, m_ffn_conv_b, m_ffn_w_down, m_final_g, v_rel_bias, v_ln1_g, v_w_in, v_b_gate, v_w_a, v_pool_w, v_pool_scale, v_w_b, v_ssd_conv_w, v_ssd_conv_b, v_ssd_dt_bias, v_ssd_a_log, v_ssd_d, v_ssd_norm_w, v_w_c, v_w_o, v_ln2_g, v_ffn_w_up, v_ffn_conv_w, v_ffn_conv_b, v_ffn_w_down, v_final_g):
    W = dict(rel_bias=rel_bias, ln1_g=ln1_g, w_in=w_in, b_gate=b_gate, w_a=w_a, pool_w=pool_w, pool_scale=pool_scale,
             w_b=w_b, ssd_conv_w=ssd_conv_w, ssd_conv_b=ssd_conv_b, ssd_dt_bias=ssd_dt_bias, ssd_a_log=ssd_a_log,
             ssd_d=ssd_d, ssd_norm_w=ssd_norm_w, w_c=w_c, w_o=w_o, ln2_g=ln2_g, ffn_w_up=ffn_w_up,
             ffn_conv_w=ffn_conv_w, ffn_conv_b=ffn_conv_b, ffn_w_down=ffn_w_down, final_g=final_g)
    M = dict(rel_bias=m_rel_bias, ln1_g=m_ln1_g, w_in=m_w_in, b_gate=m_b_gate, w_a=m_w_a, pool_w=m_pool_w,
             pool_scale=m_pool_scale, w_b=m_w_b, ssd_conv_w=m_ssd_conv_w, ssd_conv_b=m_ssd_conv_b,
             ssd_dt_bias=m_ssd_dt_bias, ssd_a_log=m_ssd_a_log, ssd_d=m_ssd_d, ssd_norm_w=m_ssd_norm_w, w_c=m_w_c,
             w_o=m_w_o, ln2_g=m_ln2_g, ffn_w_up=m_ffn_w_up, ffn_conv_w=m_ffn_conv_w, ffn_conv_b=m_ffn_conv_b,
             ffn_w_down=m_ffn_w_down, final_g=m_final_g)
    V = dict(rel_bias=v_rel_bias, ln1_g=v_ln1_g, w_in=v_w_in, b_gate=v_b_gate, w_a=v_w_a, pool_w=v_pool_w,
             pool_scale=v_pool_scale, w_b=v_w_b, ssd_conv_w=v_ssd_conv_w, ssd_conv_b=v_ssd_conv_b,
             ssd_dt_bias=v_ssd_dt_bias, ssd_a_log=v_ssd_a_log, ssd_d=v_ssd_d, ssd_norm_w=v_ssd_norm_w, w_c=v_w_c,
             w_o=v_w_o, ln2_g=v_ln2_g, ffn_w_up=v_ffn_w_up, ffn_conv_w=v_ffn_conv_w, ffn_conv_b=v_ffn_conv_b,
             ffn_w_down=v_ffn_w_down, final_g=v_final_g)
    nl = ln1_g.shape[0]
    px, py, pc_ = _position()
    chip = 2 * px + py
    cidx = jnp.reshape(pc_, (1,)).astype(jnp.int32)
    chip_idx = jnp.reshape(chip, (1,)).astype(jnp.int32)

    placed = {}
    for n, cs in SHARDED_SMALL.items():
        full = jnp.zeros(W[n].shape[:-1] + (4 * cs,), F32)
        full = lax.dynamic_update_slice(full, W[n], (0, 0, chip * cs))
        placed[n] = jnp.where(pc_ == 0, full, 0.0)
    names_sh = tuple(SHARDED_SMALL)
    shapes_sh = {n: placed[n].shape for n in names_sh}
    got = _all_reduce_small(_to_rows(_flatten(placed, names_sh)), "gather_small")
    small = {n: W[n] for n in SMALL_LAYER}
    small.update(_unflatten(got.reshape(-1), shapes_sh, names_sh))

    packs = _pack_blocks({n: W[n] for n in BIG_NAMES}, BF16)

    half = PACK_PAD // 2
    units = half // 16

    def share(weights, total):
        tot = sum(weights.values())
        return {n: math.ceil(total * v / tot) for n, v in weights.items()}

    gathers = {}

    def gather(i):
        if i not in gathers:
            buf = lax.dynamic_update_slice(lax.empty((4, PACK_PAD, D), BF16), packs[i][None], (chip, 0, 0))
            gathers[i] = _Stream(packs[i], buf, functools.partial(_gather_parts, half), 6, units, "gather_w")
        return gathers[i]

    def layer_full(i):
        return _operands(gather(i).drain())

    fwd_share = share(dict(in_a=63, in_c=31, in_d=44, mm_up=83, mm_down=34), units)

    def fwd_hooks(i):
        if i + 1 >= nl:
            return lambda name: None
        return lambda name: gather(i + 1).hook(fwd_share[name]) if name in fwd_share else None

    exchanges = {}
    bwd_share = share(dict(g_down=35, d_u2_v=60, g_up_a=35, g_up_v=35, d_merged=50, d_u_a=60, d_u_wgate=70,
                           g_in_wgate=36), units)

    class Exchange:
        def __init__(self, g):
            self.g = g
            self.pair = _Stream(g, lax.empty((4, half, D), BF16), functools.partial(_rs_pair_parts, half), 1, units, "rs_pair")
            self.hsum = self.chips = None

        def to_chips(self):
            if self.chips is None:
                self.hsum = _rs_add_pair(self.g, self.pair.drain(), cidx, "rs_add_pair")
                self.chips = _Stream(self.hsum, lax.empty((3, half, D), BF16), _rs_chip_parts, 3, units, "rs_chips")
            return self.chips

    def after_bwd(i, gw):
        exchanges[i] = Exchange(_pack_operands(gw, BF16))

    def bwd_hooks(i):
        if i + 1 >= nl:
            return lambda name: None

        def hk(name):
            if name == "d_f":
                return exchanges[i + 1].pair.hook(units)
            return exchanges[i + 1].to_chips().hook(bwd_share[name]) if name in bwd_share else None

        return hk

    loss, dx, gws, gss, dfinal, drel = _local_step(x[0], loss_target[0], rel_bias, final_g, layer_full, small,
                                                   fwd_hooks, bwd_hooks, after_bwd)

    def reduced(i):
        recv3 = exchanges[i].to_chips().drain()
        r = _rs_add_chips(exchanges[i].hsum, recv3, chip_idx, "rs_add_chips")
        other = _rs_swap(r, "rs_swap")
        both = jnp.concatenate([jnp.where(pc_ == 0, r, other), jnp.where(pc_ == 0, other, r)], axis=0)
        return _unpack_blocks(both)

    red = [reduced(i) for i in range(nl)]
    delta, new_m, new_v, grads = {}, {}, {}, {}
    for n in BIG_NAMES:
        shp = W[n].shape
        r2 = lambda a: a.reshape(-1, shp[-1])
        grads[n] = jnp.stack([red[i][n] for i in range(nl)], axis=0)
        res = _adamw(r2(W[n]), r2(grads[n]), r2(M[n]), r2(V[n]), "adamw_" + n)
        delta[n], new_m[n], new_v[n] = [a.reshape(shp) for a in res]

    sg = {}
    for n in SMALL_LAYER:
        sg[n] = jnp.stack([gss[i][n] for i in range(nl)], axis=0)
    for n in ("ssd_dt_bias", "ssd_a_log", "ssd_d"):
        sg[n] = sg[n][:, 0, :SSD_HEADS]
    sg["rel_bias"] = drel[:, :REL_BUCKETS].T
    sg["final_g"] = dfinal.reshape(D)
    sg["loss"] = loss[0, :1]
    names_sg = tuple(sg)
    shapes_sg = {n: ((nl,) + W[n].shape[1:] if n in SMALL_LAYER and n not in SHARDED_SMALL else
                     (placed[n].shape if n in SHARDED_SMALL else sg[n].shape)) for n in names_sg}
    for n in names_sg:
        sg[n] = sg[n].reshape(shapes_sg[n])
    tot = _all_reduce_small(_to_rows(_flatten(sg, names_sg)), "allreduce_small")
    tot = _unflatten(tot.reshape(-1), shapes_sg, names_sg)
    loss_out = tot.pop("loss").reshape(())
    for n, cs in SHARDED_SMALL.items():
        tot[n] = lax.dynamic_slice(tot[n], (0, 0, chip * cs), tot[n].shape[:-1] + (cs,))
    grads.update(tot)

    names_s = tuple(n for n in WEIGHTS if n not in BIG_NAMES)
    shapes_s = {n: W[n].shape for n in names_s}
    pk = lambda t: _to_rows(_flatten(t, names_s))
    dl, m2, v2 = _adamw(pk(W), pk(grads), pk(M), pk(V), "adamw_small")
    delta.update(_unflatten(dl.reshape(-1), shapes_s, names_s))
    new_m.update(_unflatten(m2.reshape(-1), shapes_s, names_s))
    new_v.update(_unflatten(v2.reshape(-1), shapes_s, names_s))

    return (loss_out, dx[None], *[grads[n] for n in WEIGHTS], *[delta[n] for n in WEIGHTS],
            *[new_m[n] for n in WEIGHTS], *[new_v[n] for n in WEIGHTS])
```

```python
import functools
import math

import jax
import jax.numpy as jnp
from jax import lax
from jax.experimental import pallas as pl
from jax.experimental.pallas import tpu as pltpu

F32 = jnp.float32
BF16 = jnp.bfloat16
MESH = pl.DeviceIdType.MESH

D = 1024
HD = 64
GW = 384
AW = 3 * GW
WIN = 128
DILATIONS = (1, 4, 16)
REL_BUCKETS = 32
REL_MAX_DISTANCE = 2048
POOL_WINDOWS = (2, 4, 8, 16)
PG = 256
SSD_HEADS = 16
SSD_N = 128
SSD_CHUNK = 128
XBC = 1536
D_FF = 2816
EPS = 1e-6
NEG = -1e30
HALO = 16
LANES = 128

SEC_A = 3 * AW
SEC_B = D
SEC_C = D + XBC
SEC_D = 3328
SEC_A_PAD = 3584
IN_WIDTH = SEC_A + SEC_B + SEC_C + 16 + 3 * D

ADAM_LR = 0.001
ADAM_B1 = 0.9
ADAM_B2 = 0.999
ADAM_EPS = 1e-08
ADAM_WD = 0.01
ADAM_STEP = 10
ADAM_TILE = 256 * 1024
MM_VMEM_BYTES = 40 * 1024 * 1024
MM_MAX_OUT_TILE = 1024 * 1024
HBM_BYTES_PER_US = 2.0e6
STEP_US = 0.35
MXU_WIDTH = 256
MXU_FLOPS_PER_US = 0.65e6


_ANY = pl.BlockSpec(memory_space=pl.ANY)


def _pick(d, cands):
    for t in cands:
        if d % t == 0:
            return t
    return d


def _iota(shape, dim):
    return lax.broadcasted_iota(jnp.int32, shape, dim)


def _dg(a, b, ca, cb):
    return lax.dot_general(a.astype(BF16), b.astype(BF16), (((ca,), (cb,)), ((), ())),
                           preferred_element_type=F32)


@jax.custom_vjp
def _bdot_nn(a, b):
    return _dg(a, b, 1, 0)


def _nn_fwd(a, b):
    return _dg(a, b, 1, 0), (a, b)


def _nn_bwd(res, g):
    a, b = res
    return _dg(g, b, 1, 1), _dg(a, g, 0, 0)


_bdot_nn.defvjp(_nn_fwd, _nn_bwd)


@jax.custom_vjp
def _bdot_nt(a, b):
    return _dg(a, b, 1, 1)


def _nt_fwd(a, b):
    return _dg(a, b, 1, 1), (a, b)


def _nt_bwd(res, g):
    a, b = res
    return _dg(g, b, 1, 0), _dg(g, a, 0, 0)


_bdot_nt.defvjp(_nt_fwd, _nt_bwd)


@jax.custom_vjp
def _bdot_tn(a, b):
    return _dg(a, b, 0, 0)


def _tn_fwd(a, b):
    return _dg(a, b, 0, 0), (a, b)


def _tn_bwd(res, g):
    a, b = res
    return _dg(b, g, 1, 1), _dg(a, g, 1, 0)


_bdot_tn.defvjp(_tn_fwd, _tn_bwd)


def _fdot(a, b):
    return jnp.dot(a, b, preferred_element_type=F32, precision=lax.Precision.HIGHEST)


def _sigmoid(x):
    return 0.5 * jnp.tanh(0.5 * x) + 0.5


def _silu(x):
    return x * _sigmoid(x)


def _softplus(x):
    return jnp.maximum(x, 0.0) + jnp.log(1.0 + jnp.exp(-jnp.abs(x)))


def _lane_pick(m, h):
    return jnp.sum(jnp.where(_iota(m.shape, 1) == h, m, 0.0), axis=1, keepdims=True)


def _row_pick(m, h):
    return jnp.sum(jnp.where(_iota(m.shape, 0) == h, m, 0.0), axis=0, keepdims=True)


def _stack_rows(rows, n):
    c = rows[0].shape[1]
    r = _iota((n, c), 0)
    out = jnp.zeros((n, c), F32)
    for k, v in enumerate(rows):
        out = out + jnp.where(r == k, v, 0.0)
    return out


def _mm(a, b, *, ta=False, tb=False, add=None, out_dtype=F32, name, hook=None):
    if ta:
        K, M = a.shape
    else:
        M, K = a.shape
    if tb:
        N, Kb = b.shape
    else:
        Kb, N = b.shape
    assert K == Kb, (a.shape, b.shape, ta, tb)
    tm, tn, tk = _mm_tiles(M, N, K, a.dtype.itemsize, b.dtype.itemsize, jnp.dtype(out_dtype).itemsize,
                           0 if add is None else add.dtype.itemsize)
    ni, nj, nk = M // tm, N // tn, K // tk
    ca = 0 if ta else 1
    cb = 1 if tb else 0
    n_in = 2 if add is None else 3
    n_hin = 0 if hook is None else len(hook.inputs)
    n_hout = 0 if hook is None else len(hook.out_shapes)

    def body(*refs):
        a_ref, b_ref = refs[:2]
        add_ref = None if add is None else refs[2]
        o_ref = refs[n_in + n_hin]
        scr = refs[n_in + n_hin + 1 + n_hout:]
        acc_ref = scr[0] if nk > 1 else None
        hargs = (refs[n_in:n_in + n_hin], refs[n_in + n_hin + 1:n_in + n_hin + 1 + n_hout], scr[1 if nk > 1 else 0:])
        i, j, k = pl.program_id(0), pl.program_id(1), pl.program_id(2)
        if hook is not None:
            @pl.when((i == 0) & (j == 0) & (k == 0))
            def _():
                hook.start(*hargs)

        part = _dg(a_ref[...], b_ref[...], ca, cb)

        def finish(r):
            if add_ref is not None:
                r = r + add_ref[...].astype(F32)
            o_ref[...] = r.astype(o_ref.dtype)

        if nk == 1:
            finish(part)
        else:
            @pl.when(k == 0)
            def _():
                acc_ref[...] = part

            @pl.when((k > 0) & (k < nk - 1))
            def _():
                acc_ref[...] += part

            @pl.when(k == nk - 1)
            def _():
                finish(acc_ref[...] + part)

        if hook is not None:
            @pl.when((i == ni - 1) & (j == nj - 1) & (k == nk - 1))
            def _():
                hook.finish(*hargs)

    a_spec = pl.BlockSpec((tk, tm), lambda i, j, k: (k, i)) if ta else pl.BlockSpec((tm, tk), lambda i, j, k: (i, k))
    b_spec = pl.BlockSpec((tn, tk), lambda i, j, k: (j, k)) if tb else pl.BlockSpec((tk, tn), lambda i, j, k: (k, j))
    in_specs = [a_spec, b_spec]
    args = [a, b]
    if add is not None:
        in_specs.append(pl.BlockSpec((tm, tn), lambda i, j, k: (i, j)))
        args.append(add)
    out_specs = [pl.BlockSpec((tm, tn), lambda i, j, k: (i, j))]
    out_shape = [jax.ShapeDtypeStruct((M, N), out_dtype)]
    scratch = [pltpu.VMEM((tm, tn), F32)] if nk > 1 else []
    aliases = {}
    if hook is not None:
        in_specs += [_ANY] * n_hin
        args += list(hook.inputs)
        out_specs += [_ANY] * n_hout
        out_shape += list(hook.out_shapes)
        scratch += list(hook.scratch)
        aliases = {n_in + hi: 1 + ho for hi, ho in hook.aliases.items()}
    sem = ("parallel", "parallel", "arbitrary") if hook is None else ("arbitrary",) * 3
    res = pl.pallas_call(
        body, name=name, grid=(ni, nj, nk), in_specs=in_specs, out_specs=out_specs, out_shape=out_shape,
        scratch_shapes=scratch, input_output_aliases=aliases,
        compiler_params=pltpu.CompilerParams(dimension_semantics=sem),
    )(*args)
    if hook is not None:
        hook.done(res[1:])
    return res[0]


def _wide(v):
    return v.astype(F32) if v.dtype == BF16 else v


def _mmf(a, b, *, tb=False, add=None, pre=None, post=None, out_dtype=F32, name, tm, hook=None):
    a_list = list(a) if isinstance(a, (list, tuple)) else [a]
    b_list = list(b) if isinstance(b, (list, tuple)) else [b]
    assert len(a_list) == len(b_list) and (len(b_list) == 1 or not (tb or pre))
    b = b_list[0]
    if tb:
        N, K = b.shape
    else:
        K, N = b.shape
    M = pre[1][0].shape[0] if pre else a_list[0].shape[0]
    tn = N if post or N <= 1024 else _pick(N, (512, 256, LANES))
    ni, nj = M // tm, N // tn
    cb = 1 if tb else 0
    pre_fn, pre_rows, pre_consts = pre if pre else (None, [], [])
    post_fn, post_rows, post_consts, post_outs, post_accs = post if post else (None, [], [], [], [])
    hook_in = [] if hook is None else list(hook.inputs)
    hook_out = [] if hook is None else list(hook.out_shapes)

    def row_spec(arr):
        return pl.BlockSpec((tm, arr.shape[1]), lambda i, j: (i, 0))

    def const_spec(arr):
        return pl.BlockSpec(arr.shape, lambda i, j, nd=arr.ndim: (0,) * nd)

    args, in_specs = [], []
    for arr in (a_list if not pre else pre_rows):
        args.append(arr)
        in_specs.append(row_spec(arr))
    for arr in pre_consts:
        args.append(arr)
        in_specs.append(const_spec(arr))
    for arr in b_list:
        args.append(arr)
        in_specs.append(pl.BlockSpec((tn, K), lambda i, j: (j, 0)) if tb else
                        pl.BlockSpec((arr.shape[0], tn), lambda i, j: (0, j)))
    if add is not None:
        args.append(add)
        in_specs.append(pl.BlockSpec((tm, tn), lambda i, j: (i, j)))
    for arr in post_rows:
        args.append(arr)
        in_specs.append(row_spec(arr))
    for arr in post_consts:
        args.append(arr)
        in_specs.append(const_spec(arr))
    n_main = len(args)
    args += hook_in
    in_specs += [_ANY] * len(hook_in)

    out_shape, out_specs = [], []
    if post:
        for c, dt in post_outs:
            out_shape.append(jax.ShapeDtypeStruct((M, c), dt))
            out_specs.append(pl.BlockSpec((tm, c), lambda i, j: (i, 0)))
        for r, c in post_accs:
            out_shape.append(jax.ShapeDtypeStruct((r, c), F32))
            out_specs.append(pl.BlockSpec((r, c), lambda i, j: (0, 0)))
    else:
        out_shape.append(jax.ShapeDtypeStruct((M, N), out_dtype))
        out_specs.append(pl.BlockSpec((tm, tn), lambda i, j: (i, j)))
    if pre:
        out_shape.append(jax.ShapeDtypeStruct((M, K), BF16))
        out_specs.append(pl.BlockSpec((tm, K), lambda i, j: (i, 0)))
    n_out = len(out_shape)
    out_shape += hook_out
    out_specs += [_ANY] * len(hook_out)
    scratch = ([pltpu.VMEM((tm, K), BF16)] if pre else []) + ([] if hook is None else list(hook.scratch))
    aliases = {} if hook is None else {n_main + hi: n_out + ho for hi, ho in hook.aliases.items()}

    def body(*refs):
        ins, outs, scr = refs[:n_main], refs[len(args):len(args) + n_out], refs[len(args) + len(out_shape):]
        hargs = (refs[n_main:len(args)], refs[len(args) + n_out:len(args) + len(out_shape)], scr[1 if pre else 0:])
        i, j = pl.program_id(0), pl.program_id(1)
        if hook is not None:
            @pl.when((i == 0) & (j == 0))
            def _():
                hook.start(*hargs)

        it = iter(ins)
        if pre:
            rows_ = [next(it) for _ in pre_rows]
            consts_ = [next(it) for _ in pre_consts]

            @pl.when(j == 0)
            def _():
                av = pre_fn(*[_wide(r[...]) for r in rows_], *[_wide(r[...]) for r in consts_]).astype(BF16)
                scr[0][...] = av
                outs[-1][...] = av

            ats = [scr[0][...]]
        else:
            ats = [next(it)[...] for _ in a_list]
        p = None
        for at in ats:
            part = _dg(at, next(it)[...], 1, cb)
            p = part if p is None else p + part
        if add is not None:
            p = p + next(it)[...].astype(F32)
        if post:
            rows_ = [next(it) for _ in post_rows]
            consts_ = [next(it) for _ in post_consts]
            res = post_fn(p, *[_wide(r[...]) for r in rows_], *[_wide(r[...]) for r in consts_])
            for r, v in zip(outs[:len(post_outs)], res[:len(post_outs)]):
                r[...] = v.astype(r.dtype)
            for r, v in zip(outs[len(post_outs):], res[len(post_outs):]):
                @pl.when(i == 0)
                def _(r=r, v=v):
                    r[...] = v

                @pl.when(i > 0)
                def _(r=r, v=v):
                    r[...] += v
        else:
            outs[0][...] = p.astype(outs[0].dtype)
        if hook is not None:
            @pl.when((i == ni - 1) & (j == nj - 1))
            def _():
                hook.finish(*hargs)

    res = pl.pallas_call(
        body, name=name, grid=(ni, nj), in_specs=in_specs, out_specs=out_specs, out_shape=out_shape,
        scratch_shapes=scratch, input_output_aliases=aliases,
        compiler_params=pltpu.CompilerParams(dimension_semantics=("arbitrary", "arbitrary")),
    )(*args)
    if hook is not None:
        hook.done(res[n_out:])
    return res[:n_out]


def _mm_tiles(M, N, K, sa, sb, so, sadd):
    def tiles(d):
        return [t for t in range(LANES, min(d, 2048) + 1, LANES) if d % t == 0] or [d]

    best = None
    for tk in [K] + [t for t in tiles(K) if t < K]:
        for tm in tiles(M):
            for tn in tiles(N):
                vmem = 2 * (tm * tk * sa + tk * tn * sb + tm * tn * (so + sadd)) + (tm * tn * 4 if tk < K else 0)
                if vmem > MM_VMEM_BYTES or tm * tn > MM_MAX_OUT_TILE:
                    continue
                a_reads = 1 if tk == K else N // tn
                traffic = M * K * sa * a_reads + K * N * sb * (M // tm) + M * N * (so + sadd)
                steps = (M // tm) * (N // tn) * (K // tk)
                width = -(-tn // MXU_WIDTH) * MXU_WIDTH
                mxu = 2.0 * M * K * N * (width / tn) / MXU_FLOPS_PER_US
                edge = tm * tk * sa + tk * tn * sb + tm * tn * (so + sadd)
                cost = max(traffic / HBM_BYTES_PER_US, mxu) + steps * STEP_US + edge / HBM_BYTES_PER_US
                if best is None or cost < best[0]:
                    best = (cost, tm, tn, tk)
    assert best is not None, (M, N, K)
    return best[1:]


class _Hook:
    def __init__(self, inputs, out_shapes, aliases, scratch, start, finish, done):
        self.inputs, self.out_shapes, self.aliases, self.scratch = inputs, out_shapes, aliases, scratch
        self.start, self.finish, self.done = start, finish, done


class _Ctx:
    def __init__(self, first, last, row0, rows):
        self.first, self.last, self.row0, self.rows = first, last, row0, rows


def _rows(name, fn, ins, outs, accs=(), *, tm, nrows, ncol=1):
    nt = nrows // tm
    hb = tm // HALO
    nh = nrows // HALO
    ins = [(kind, arr, arr.shape[1] if kind == "row" and cw is None else cw, base) for kind, arr, cw, base in ins]
    in_specs, args = [], []
    for kind, arr, cw, base in ins:
        if kind == "row":
            in_specs.append(pl.BlockSpec((tm, cw), lambda j, i, base=base: (i, base + j)))
        elif kind == "prev":
            in_specs.append(pl.BlockSpec((HALO, cw), lambda j, i, base=base: (jnp.maximum(i * hb - 1, 0), base + j)))
        elif kind == "next":
            in_specs.append(pl.BlockSpec((HALO, cw), lambda j, i, base=base: (jnp.minimum((i + 1) * hb, nh - 1), base + j)))
        elif kind in ("const", "raw"):
            in_specs.append(pl.BlockSpec(arr.shape, lambda j, i, nd=arr.ndim: (0,) * nd))
        elif kind == "ccol":
            in_specs.append(pl.BlockSpec((arr.shape[0], cw), lambda j, i, base=base: (0, base + j)))
        else:
            raise ValueError(kind)
        args.append(arr)
    out_specs, out_shape = [], []
    for ctot, cw, base, dt in outs:
        out_specs.append(pl.BlockSpec((tm, cw), lambda j, i, base=base: (i, base + j)))
        out_shape.append(jax.ShapeDtypeStruct((nrows, ctot), dt))
    for r, ctot, cw in accs:
        out_specs.append(pl.BlockSpec((r, cw), lambda j, i: (0, j)))
        out_shape.append(jax.ShapeDtypeStruct((r, ctot), F32))
    n_in, n_out = len(ins), len(outs)

    def body(*refs):
        i = pl.program_id(1)
        in_refs, out_refs, acc_refs = refs[:n_in], refs[n_in:n_in + n_out], refs[n_in + n_out:]
        if acc_refs:
            @pl.when(i == 0)
            def _():
                for r in acc_refs:
                    r[...] = jnp.zeros_like(r)

        vals = [r[...] if s[0] == "raw" else _wide(r[...]) for r, s in zip(in_refs, ins)]
        res = fn(_Ctx(i == 0, i == nt - 1, i * tm, tm), *vals)
        for r, v in zip(out_refs, res[:n_out]):
            r[...] = v.astype(r.dtype)
        for r, v in zip(acc_refs, res[n_out:]):
            r[...] += v

    res = pl.pallas_call(
        body, name=name, grid=(ncol, nt), in_specs=in_specs, out_specs=out_specs, out_shape=out_shape,
        compiler_params=pltpu.CompilerParams(dimension_semantics=("arbitrary", "arbitrary")),
    )(*args)
    return res


def _shift_down(xcat, k):
    return xcat if k == 0 else pltpu.roll(xcat, k, 0)


def _shift_up(xcat, k):
    return xcat if k == 0 else pltpu.roll(xcat, xcat.shape[0] - k, 0)


def _with_prev(ctx, halo, x):
    return jnp.concatenate([jnp.where(ctx.first, 0.0, halo), x], axis=0)


def _with_next(ctx, x, halo):
    return jnp.concatenate([x, jnp.where(ctx.last, 0.0, halo)], axis=0)


def _rms_core(x, g):
    r = lax.rsqrt(jnp.mean(x * x, axis=-1, keepdims=True) + EPS)
    return x * r * g


def _rms_post(du, xv, drv, gv):
    _, vjp = jax.vjp(_rms_core, xv, gv)
    dx, dg = vjp(du)
    return [drv + dx, drv + dx, dg]


RMS_POST_OUTS = [(D, F32), (D, BF16)]


def _final_loss(x, target, g):
    S = x.shape[0]

    def fn(ctx, xv, tv, gv):
        def f(xx, gg):
            err = _rms_core(xx, gg) - tv
            return 0.5 * jnp.sum(err * err) / D

        loss, vjp = jax.vjp(f, xv, gv)
        dx, dg = vjp(jnp.ones((), F32))
        return [dx, dx, dg, jnp.zeros((1, LANES), F32) + loss]

    return _rows("final_loss", fn, [("row", x, None, 0), ("row", target, None, 0), ("const", g, None, 0)],
                 [(D, D, 0, F32), (D, D, 0, BF16)], [(1, D, D), (1, LANES, LANES)], tm=256, nrows=S)


def _attn_valid(n):
    qi = _iota((WIN, 2 * WIN), 0)
    kk = _iota((WIN, 2 * WIN), 1)
    rel = qi + WIN - kk
    return (rel >= 0) & (rel <= WIN) & ((kk >= WIN) | (n > 0))


def _attn_block(q, kp, kc, vp, vc, b0, b1):
    k = jnp.concatenate([kp, kc], axis=0)
    v = jnp.concatenate([vp, vc], axis=0)
    lo = _iota((WIN, LANES), 1) < HD
    scale = 1.0 / math.sqrt(HD)
    os_, ls_ = [], []
    for hh, b in ((0, b0), (1, b1)):
        qm = jnp.where(lo if hh == 0 else ~lo, q, 0.0)
        s = _bdot_nt(qm, k) * scale + b
        m = lax.stop_gradient(jnp.max(s, axis=1, keepdims=True))
        p = jnp.exp(s - m)
        l = jnp.sum(p, axis=1, keepdims=True)
        os_.append(_bdot_nn(p, v) / l)
        ls_.append(m + jnp.log(l))
    return jnp.where(lo, os_[0], os_[1]), jnp.where(lo, ls_[0], ls_[1])


def _residue_rows(r, d):
    return pl.ds(0, WIN) if d == 1 else pl.ds(r, WIN, stride=d)


def _for_residues(d, fn):
    if d == 1:
        fn(0, 0)
    else:
        lax.fori_loop(0, d, fn, 0, unroll=min(d, 8))


def _pairs_per_step(d):
    return 3 if d == 1 else 1


def _bias_table(rel_bias, bucket, gi, name):
    def body(t_ref, b_ref, o_ref):
        h = 6 * gi + pl.program_id(0)
        b = b_ref[...]
        acc = jnp.zeros(b.shape, F32)
        for k in range(REL_BUCKETS):
            acc = jnp.where(b == k, t_ref[k, h], acc)
        o_ref[0] = acc

    return pl.pallas_call(
        body, name=name, grid=(6,),
        in_specs=[pl.BlockSpec(memory_space=pltpu.SMEM), pl.BlockSpec((WIN, 2 * WIN), lambda h: (0, 0))],
        out_specs=pl.BlockSpec((1, WIN, 2 * WIN), lambda h: (h, 0, 0)),
        out_shape=jax.ShapeDtypeStruct((6, WIN, 2 * WIN), F32),
    )(rel_bias, bucket)


def _attn_fwd(pa, bias, gi, name):
    S = pa.shape[0]
    d = DILATIONS[gi]
    bt = WIN * d
    nb = S // bt
    hpw = _pairs_per_step(d)
    bw = hpw * LANES
    cb = 3 * gi // hpw

    def body(q_ref, kp_ref, kc_ref, vp_ref, vc_ref, b_ref, o_ref, l_ref):
        valid = _attn_valid(pl.program_id(1))
        bm = [jnp.where(valid, b_ref[k], NEG) for k in range(2 * hpw)]

        def residue(r, carry):
            sl = _residue_rows(r, d)
            for t in range(hpw):
                ln = pl.ds(t * LANES, LANES)
                o, lse = _attn_block(q_ref[sl, ln], kp_ref[sl, ln], kc_ref[sl, ln], vp_ref[sl, ln], vc_ref[sl, ln],
                                     bm[2 * t], bm[2 * t + 1])
                o_ref[sl, ln] = o
                l_ref[sl, ln] = lse
            return carry

        _for_residues(d, residue)

    def spec(off, prev):
        if prev:
            return pl.BlockSpec((bt, bw), lambda hp, n: (jnp.maximum(n - 1, 0), off // hpw + cb + hp))
        return pl.BlockSpec((bt, bw), lambda hp, n: (n, off // hpw + cb + hp))

    ospec = pl.BlockSpec((bt, bw), lambda hp, n: (n, hp))
    return pl.pallas_call(
        body, name=name, grid=(3 // hpw, nb),
        in_specs=[spec(0, False), spec(9, True), spec(9, False), spec(18, True), spec(18, False),
                  pl.BlockSpec((2 * hpw, WIN, 2 * WIN), lambda hp, n: (hp, 0, 0))],
        out_specs=[ospec, ospec],
        out_shape=[jax.ShapeDtypeStruct((S, GW), F32)] * 2,
        compiler_params=pltpu.CompilerParams(dimension_semantics=("parallel", "arbitrary")),
    )(pa, pa, pa, pa, pa, bias)


def _attn_bwd(pa, bias, do, dlse, db_in, dqkv, gi, name):
    S = pa.shape[0]
    d = DILATIONS[gi]
    bt = WIN * d
    nb = S // bt
    hpw = _pairs_per_step(d)
    bw = hpw * LANES
    cb = 3 * gi // hpw

    def body(q_ref, kp_ref, kc_ref, vp_ref, vc_ref, b_ref, do_ref, dl_ref, dbi_ref, dqi_ref, dki_ref, dvi_ref,
             dq_ref, dk_ref, dv_ref, db_ref, ck, cv):
        n = pl.program_id(1)

        @pl.when(n == 0)
        def _():
            db_ref[...] = dbi_ref[...]
            ck[...] = jnp.zeros_like(ck)
            cv[...] = jnp.zeros_like(cv)

        @pl.when(n < nb)
        def _():
            valid = _attn_valid(n)
            bm = [jnp.where(valid, b_ref[k], NEG) for k in range(2 * hpw)]

            def residue(r, carry):
                sl = _residue_rows(r, d)
                cs = pl.ds(pl.multiple_of(r * WIN, WIN), WIN)
                for t in range(hpw):
                    ln = pl.ds(t * LANES, LANES)
                    _, vjp = jax.vjp(_attn_block, q_ref[sl, ln], kp_ref[sl, ln], kc_ref[sl, ln], vp_ref[sl, ln],
                                     vc_ref[sl, ln], bm[2 * t], bm[2 * t + 1])
                    dq, dkp, dkc, dvp, dvc, db0, db1 = vjp((do_ref[sl, ln], dl_ref[sl, ln]))
                    dq_ref[sl, ln] = dq
                    dk_ref[sl, ln] = ck[cs, ln] + dkp
                    dv_ref[sl, ln] = cv[cs, ln] + dvp
                    ck[cs, ln] = dkc
                    cv[cs, ln] = dvc
                    db_ref[2 * t] += db0
                    db_ref[2 * t + 1] += db1
                return carry

            _for_residues(d, residue)

        @pl.when(n == nb)
        def _():
            def residue(r, carry):
                sl = _residue_rows(r, d)
                cs = pl.ds(pl.multiple_of(r * WIN, WIN), WIN)
                dk_ref[sl, :] = ck[cs, :]
                dv_ref[sl, :] = cv[cs, :]
                return carry

            _for_residues(d, residue)

    def cur(n):
        return jnp.minimum(n, nb - 1)

    def spec(off, prev):
        if prev:
            return pl.BlockSpec((bt, bw), lambda hp, n: (jnp.maximum(cur(n) - 1, 0), off // hpw + cb + hp))
        return pl.BlockSpec((bt, bw), lambda hp, n: (cur(n), off // hpw + cb + hp))

    gspec = pl.BlockSpec((bt, bw), lambda hp, n: (cur(n), hp))
    bspec = pl.BlockSpec((2 * hpw, WIN, 2 * WIN), lambda hp, n: (hp, 0, 0))
    qspec = pl.BlockSpec((bt, bw), lambda hp, n: (cur(n), cb + hp))
    kspec = pl.BlockSpec((bt, bw), lambda hp, n: (jnp.maximum(n - 1, 0), cb + hp))
    dq, dk, dv, db = pl.pallas_call(
        body, name=name, grid=(3 // hpw, nb + 1),
        in_specs=[spec(0, False), spec(9, True), spec(9, False), spec(18, True), spec(18, False),
                  bspec, gspec, gspec, bspec, _ANY, _ANY, _ANY],
        out_specs=[qspec, kspec, kspec, bspec],
        out_shape=[jax.ShapeDtypeStruct((S, AW), F32)] * 3 + [jax.ShapeDtypeStruct((6, WIN, 2 * WIN), F32)],
        scratch_shapes=[pltpu.VMEM((bt, bw), F32), pltpu.VMEM((bt, bw), F32)],
        input_output_aliases={9: 0, 10: 1, 11: 2},
        compiler_params=pltpu.CompilerParams(dimension_semantics=("arbitrary", "arbitrary")),
    )(pa, pa, pa, pa, pa, bias, do, dlse, db_in, *dqkv)
    return (dq, dk, dv), db


def _mix_core(o0, o1, o2, l0, l1, l2):
    m = lax.stop_gradient(jnp.maximum(jnp.maximum(l0, l1), l2))
    e0, e1, e2 = jnp.exp(l0 - m), jnp.exp(l1 - m), jnp.exp(l2 - m)
    return (e0 * o0 + e1 * o1 + e2 * o2) / (e0 + e1 + e2)


def _mix_fwd(os_, ls_, name):
    S = os_[0].shape[0]
    ins = [("row", a, None, 0) for a in (*os_, *ls_)]
    return _rows(name, lambda ctx, *v: [_mix_core(*v)], ins, [(GW, GW, 0, BF16)], tm=512, nrows=S)[0]


def _mix_bwd(os_, ls_, datt, name):
    S = datt.shape[0]

    def fn(ctx, *v):
        _, vjp = jax.vjp(_mix_core, *v[:6])
        return list(vjp(v[6]))

    ins = [("row", a, None, 0) for a in (*os_, *ls_, datt)]
    outs = [(GW, GW, 0, F32)] * 6
    r = _rows(name, fn, ins, outs, tm=512, nrows=S)
    return r[:3], r[3:]


def _t5_bucket(dist):
    max_exact = REL_BUCKETS // 2
    is_small = dist < max_exact
    nf = jnp.maximum(dist, 1).astype(F32)
    large = max_exact + (jnp.log(nf / max_exact) / math.log(REL_MAX_DISTANCE / max_exact)
                         * (REL_BUCKETS - max_exact)).astype(jnp.int32)
    large = jnp.minimum(large, REL_BUCKETS - 1)
    return jnp.where(is_small, dist, large)


def _buckets(d):
    qi = jnp.arange(WIN)[:, None]
    kk = jnp.arange(2 * WIN)[None, :]
    rel = qi + WIN - kk
    return _t5_bucket(jnp.clip(rel, 0, None) * d)


def _pool_cnt(ctx, w):
    pos = ctx.row0 + _iota((ctx.rows, PG), 0) + 1
    return jnp.minimum(pos, w).astype(F32)


def _pool_d(ctx, halo, u):
    ds = []
    for g, w in enumerate(POOL_WINDOWS):
        ug = u[:, g * PG:(g + 1) * PG]
        s = _with_prev(ctx, halo[:, g * PG:(g + 1) * PG], ug)
        step = 1
        while step < w:
            s = s + _shift_down(s, step)
            step *= 2
        ds.append(s[HALO:] / _pool_cnt(ctx, w) - ug)
    return ds


def _pool_fwd(pb, pw, scale, name):
    S = pb.shape[0]

    def fn(ctx, halo, u, w, sc):
        ds = _pool_d(ctx, halo, u)
        return [jnp.concatenate([_dg(ds[k], w[k], 1, 0) for k in range(4)], axis=1) * sc]

    return _rows(name, fn, [("prev", pb, D, 0), ("row", pb, None, 0), ("raw", pw, None, 0), ("const", scale, None, 0)],
                 [(D, D, 0, BF16)], tm=512, nrows=S)[0]


def _pool_bwd(pb, pw, scale, dpo, name):
    S = pb.shape[0]

    def fn1(ctx, halo, u, w, sc, dy):
        ds = _pool_d(ctx, halo, u)
        dyp = dy * sc
        y = jnp.concatenate([_dg(ds[k], w[k], 1, 0) for k in range(4)], axis=1)
        es, dws = [], []
        for k, wd in enumerate(POOL_WINDOWS):
            cols = slice(k * PG, (k + 1) * PG)
            es.append(_dg(dyp[:, cols], w[k], 1, 1) / _pool_cnt(ctx, wd))
            dws.append(_dg(ds[k], dyp[:, cols], 0, 0))
        return [jnp.concatenate(es, axis=1), jnp.concatenate(dws, axis=0), jnp.sum(dy * y, axis=0, keepdims=True)]

    e, dpw, dsc = _rows(name + "_a", fn1,
                        [("prev", pb, D, 0), ("row", pb, None, 0), ("raw", pw, None, 0), ("const", scale, None, 0),
                         ("row", dpo, None, 0)],
                        [(D, D, 0, F32)], [(4 * PG, PG, PG), (1, D, D)], tm=512, nrows=S)

    def fn2(ctx, ev, halo):
        outs = []
        for g, w in enumerate(POOL_WINDOWS):
            eg = ev[:, g * PG:(g + 1) * PG]
            s = _with_next(ctx, eg, halo[:, g * PG:(g + 1) * PG])
            step = 1
            while step < w:
                s = s + _shift_up(s, step)
                step *= 2
            outs.append(s[:ctx.rows] - eg * _pool_cnt(ctx, w))
        return [jnp.concatenate(outs, axis=1)]

    du = _rows(name + "_b", fn2, [("row", e, None, 0), ("next", e, D, 0)], [(D, D, 0, BF16)], tm=512, nrows=S)[0]
    return du, dpw, dsc


def _conv_taps(ctx, halo, x, K):
    cat = _with_prev(ctx, halo, x)
    return [_shift_down(cat, K - 1 - k)[HALO:] for k in range(K)]


def _conv_pre(taps, w, b):
    acc = b
    for k, t in enumerate(taps):
        acc = acc + t * _row_pick(w, k)
    return acc


CW = 256
CWS = 512
CONV_BWD_TILE = 256 * 1024
CONV_FWD_TILE = 1024 * 1024


def _ext_taps(ctx, prev, x, nxt, K):
    cat = jnp.concatenate([jnp.where(ctx.first, 0.0, prev), x, jnp.where(ctx.last, 0.0, nxt)], axis=0)
    return [_shift_down(cat, K - 1 - k)[HALO:] for k in range(K)]


def _conv_t_rows(dp, w, K, tm):
    acc = jnp.zeros((tm, dp.shape[1]), F32)
    for k in range(K):
        acc = acc + _shift_up(dp, K - 1 - k)[:tm] * _row_pick(w, k)
    return acc


def _ssd_conv_fwd(pc, w, b, name):
    S = pc.shape[0]
    base = D // CWS

    def fn(ctx, halo, x, wv, bv):
        return [_silu(_conv_pre(_conv_taps(ctx, halo, x, 4), wv, bv))]

    return _rows(name, fn, [("prev", pc, CWS, base), ("row", pc, CWS, base), ("ccol", w, CWS, 0), ("ccol", b, CWS, 0)],
                 [(XBC, CWS, 0, F32)], tm=min(S, CONV_FWD_TILE // CWS), nrows=S, ncol=XBC // CWS)[0]


def _ssd_conv_bwd(pc, w, b, dy, name):
    S = pc.shape[0]
    base = D // CWS

    def fn(ctx, prev, x, nxt, wv, bv, dyv, dyn):
        n = ctx.rows
        taps = _ext_taps(ctx, prev, x, nxt, 4)
        pre = _conv_pre(taps, wv, bv)
        sg = _sigmoid(pre)
        dye = jnp.concatenate([dyv, jnp.where(ctx.last, 0.0, dyn)], axis=0)
        dpre = dye * sg * (1.0 + pre * (1.0 - sg))
        dw = _stack_rows([jnp.sum(dpre[:n] * t[:n], axis=0, keepdims=True) for t in taps], 4)
        return [_conv_t_rows(dpre, wv, 4, n), dw, jnp.sum(dpre[:n], axis=0, keepdims=True)]

    return _rows(name, fn,
                 [("prev", pc, CWS, base), ("row", pc, CWS, base), ("next", pc, CWS, base), ("ccol", w, CWS, 0),
                  ("ccol", b, CWS, 0), ("row", dy, CWS, 0), ("next", dy, CWS, 0)],
                 [(XBC, CWS, 0, BF16)], [(4, XBC, CWS), (1, XBC, CWS)], tm=min(S, CONV_BWD_TILE // CWS), nrows=S,
                 ncol=XBC // CWS)


NFC = D_FF // CW


def _ffn_act_fwd(h, w, b, name):
    S = h.shape[0]

    def fn(ctx, ha, a, hv, v, wa, wv, ba, bv):
        pa = _conv_pre(_conv_taps(ctx, ha, a, 3), wa, ba)
        pv = _conv_pre(_conv_taps(ctx, hv, v, 3), wv, bv)
        return [_silu(pa) * pv]

    return _rows(name, fn,
                 [("prev", h, CW, 0), ("row", h, CW, 0), ("prev", h, CW, NFC), ("row", h, CW, NFC),
                  ("ccol", w, CW, 0), ("ccol", w, CW, NFC), ("ccol", b, CW, 0), ("ccol", b, CW, NFC)],
                 [(D_FF, CW, 0, BF16)], tm=min(S, CONV_FWD_TILE // CW), nrows=S, ncol=NFC)[0]


def _ffn_act_bwd(h, w, b, df, name):
    S = h.shape[0]

    def fn(ctx, pa_, a, na, pv_, v, nv, wa, wv, ba, bv, dfv, dfn):
        n = ctx.rows
        ta = _ext_taps(ctx, pa_, a, na, 3)
        tv = _ext_taps(ctx, pv_, v, nv, 3)
        pa = _conv_pre(ta, wa, ba)
        pv = _conv_pre(tv, wv, bv)
        sg = _sigmoid(pa)
        dfe = jnp.concatenate([dfv, jnp.where(ctx.last, 0.0, dfn)], axis=0)
        dpa = dfe * pv * sg * (1.0 + pa * (1.0 - sg))
        dpv = dfe * pa * sg
        res = [_conv_t_rows(dpa, wa, 3, n), _conv_t_rows(dpv, wv, 3, n)]
        for dp, taps in ((dpa, ta), (dpv, tv)):
            res.append(_stack_rows([jnp.sum(dp[:n] * t[:n], axis=0, keepdims=True) for t in taps], 3))
        for dp in (dpa, dpv):
            res.append(jnp.sum(dp[:n], axis=0, keepdims=True))
        return res

    ins = []
    for base in (0, NFC):
        ins += [("prev", h, CW, base), ("row", h, CW, base), ("next", h, CW, base)]
    ins += [("ccol", w, CW, 0), ("ccol", w, CW, NFC), ("ccol", b, CW, 0), ("ccol", b, CW, NFC),
            ("row", df, CW, 0), ("next", df, CW, 0)]
    dha, dhv, dwa, dwv, dba, dbv = _rows(
        name, fn, ins, [(D_FF, CW, 0, BF16)] * 2, [(3, D_FF, CW)] * 2 + [(1, D_FF, CW)] * 2,
        tm=min(S, CONV_BWD_TILE // CW), nrows=S, ncol=NFC)
    return dha, dhv, jnp.concatenate([dwa, dwv], axis=1), jnp.concatenate([dba, dbv], axis=1)


NSLAB = D // LANES
CPS = 2


def _ssd_chunk(xs, Bs, Cs, dtraw, dtb, alog, prev):
    lsz = SSD_CHUNK
    lane = _iota((lsz, LANES), 1)
    row = _iota((lsz, LANES), 0)
    dt = jnp.where(lane < SSD_HEADS, _softplus(dtraw + dtb), 0.0)
    a = dt * (-jnp.exp(alog))
    tril = row >= lane
    a_cs = _fdot(tril.astype(F32), a)
    a_cst = a_cs.T
    a_last = jnp.sum(a, axis=0, keepdims=True)
    lo = lane < HD
    top = row < HD
    cbs = [_bdot_nt(Cs[g], Bs[g]) for g in range(2)]
    ys, news = [], []
    for s in range(NSLAB):
        g = s // (NSLAB // 2)
        cols, lms, dts, als = [], [], [], []
        for hh in range(2):
            h = 2 * s + hh
            col = _lane_pick(a_cs, h)
            seg = col - _row_pick(a_cst, h)
            lms.append(jnp.exp(jnp.where(tril, seg, NEG)))
            cols.append(col)
            dts.append(_lane_pick(dt, h))
            als.append(_lane_pick(a_last, h))
        col_x = jnp.where(lo, cols[0], cols[1])
        al_x = jnp.where(lo, als[0], als[1])
        xc = xs[s] * jnp.where(lo, dts[0], dts[1])
        yd = jnp.where(lo, _bdot_nn(cbs[g] * lms[0], xc), _bdot_nn(cbs[g] * lms[1], xc))
        yoff = _bdot_nt(Cs[g], prev[s]) * jnp.exp(col_x)
        ys.append(yd + yoff)
        st = _bdot_tn(xc * jnp.exp(al_x - col_x), Bs[g])
        news.append(prev[s] * jnp.exp(jnp.where(top, als[0], als[1])) + st)
    return ys, news


def _ssd_scan_fwd(xbc_c, pd, dtb, alog, name):
    S = xbc_c.shape[0]
    nc = S // SSD_CHUNK
    rows_ = CPS * SSD_CHUNK

    def body(x_ref, b_ref, c_ref, dt_ref, dtb_ref, al_ref, y_ref, st_ref, state):
        c = pl.program_id(0)

        @pl.when(c == 0)
        def _():
            state[...] = jnp.zeros_like(state)

        prev = [state[s * LANES:(s + 1) * LANES, :] for s in range(NSLAB)]
        for u in range(CPS):
            rw = pl.ds(u * SSD_CHUNK, SSD_CHUNK)
            xs = [x_ref[rw, s * LANES:(s + 1) * LANES] for s in range(NSLAB)]
            Bs = [b_ref[rw, g * SSD_N:(g + 1) * SSD_N] for g in range(2)]
            Cs = [c_ref[rw, g * SSD_N:(g + 1) * SSD_N] for g in range(2)]
            for s in range(NSLAB):
                st_ref[u, s * LANES:(s + 1) * LANES, :] = prev[s]
            ys, prev = _ssd_chunk(xs, Bs, Cs, dt_ref[rw, :].astype(F32), dtb_ref[...], al_ref[...], prev)
            for s in range(NSLAB):
                y_ref[rw, s * LANES:(s + 1) * LANES] = ys[s]
        for s in range(NSLAB):
            state[s * LANES:(s + 1) * LANES, :] = prev[s]

    return pl.pallas_call(
        body, name=name, grid=(nc // CPS,),
        in_specs=[pl.BlockSpec((rows_, D), lambda c: (c, 0)),
                  pl.BlockSpec((rows_, 2 * SSD_N), lambda c: (c, D // (2 * SSD_N))),
                  pl.BlockSpec((rows_, 2 * SSD_N), lambda c: (c, D // (2 * SSD_N) + 1)),
                  pl.BlockSpec((rows_, LANES), lambda c: (c, 0)),
                  pl.BlockSpec((1, LANES), lambda c: (0, 0)), pl.BlockSpec((1, LANES), lambda c: (0, 0))],
        out_specs=[pl.BlockSpec((rows_, D), lambda c: (c, 0)), pl.BlockSpec((CPS, D, SSD_N), lambda c: (c, 0, 0))],
        out_shape=[jax.ShapeDtypeStruct((S, D), F32), jax.ShapeDtypeStruct((nc, D, SSD_N), F32)],
        scratch_shapes=[pltpu.VMEM((D, SSD_N), F32)],
        compiler_params=pltpu.CompilerParams(dimension_semantics=("arbitrary",)),
    )(xbc_c, xbc_c, xbc_c, pd, dtb, alog)


def _ssd_scan_bwd(xbc_c, pd, dtb, alog, states, dy, dxs_skip, name):
    S = xbc_c.shape[0]
    nc = S // SSD_CHUNK
    rows_ = CPS * SSD_CHUNK

    def body(x_ref, b_ref, c_ref, dt_ref, dtb_ref, al_ref, st_ref, dy_ref, sk_ref,
             dx_ref, ddt_ref, ddtb_ref, dal_ref, dstate):
        c = pl.program_id(0)

        @pl.when(c == 0)
        def _():
            dstate[...] = jnp.zeros_like(dstate)
            ddtb_ref[...] = jnp.zeros_like(ddtb_ref)
            dal_ref[...] = jnp.zeros_like(dal_ref)

        dnew = [dstate[s * LANES:(s + 1) * LANES, :] for s in range(NSLAB)]
        for u in reversed(range(CPS)):
            rw = pl.ds(u * SSD_CHUNK, SSD_CHUNK)
            xs = [x_ref[rw, s * LANES:(s + 1) * LANES] for s in range(NSLAB)]
            Bs = [b_ref[rw, g * SSD_N:(g + 1) * SSD_N] for g in range(2)]
            Cs = [c_ref[rw, g * SSD_N:(g + 1) * SSD_N] for g in range(2)]
            prev = [st_ref[u, s * LANES:(s + 1) * LANES, :] for s in range(NSLAB)]
            _, vjp = jax.vjp(_ssd_chunk, xs, Bs, Cs, dt_ref[rw, :].astype(F32), dtb_ref[...], al_ref[...], prev)
            dys = [dy_ref[rw, s * LANES:(s + 1) * LANES] for s in range(NSLAB)]
            dxs, dBs, dCs, ddt, ddtb, dal, dnew = vjp((dys, dnew))
            for s in range(NSLAB):
                dx_ref[rw, s * LANES:(s + 1) * LANES] = dxs[s] + sk_ref[rw, s * LANES:(s + 1) * LANES]
            for g in range(2):
                dx_ref[rw, D + g * SSD_N:D + (g + 1) * SSD_N] = dBs[g]
                dx_ref[rw, D + 2 * SSD_N + g * SSD_N:D + 2 * SSD_N + (g + 1) * SSD_N] = dCs[g]
            ddt_ref[rw, :] = ddt
            ddtb_ref[...] += ddtb
            dal_ref[...] += dal
        for s in range(NSLAB):
            dstate[s * LANES:(s + 1) * LANES, :] = dnew[s]

    def rv(c):
        return nc // CPS - 1 - c

    return pl.pallas_call(
        body, name=name, grid=(nc // CPS,),
        in_specs=[pl.BlockSpec((rows_, D), lambda c: (rv(c), 0)),
                  pl.BlockSpec((rows_, 2 * SSD_N), lambda c: (rv(c), D // (2 * SSD_N))),
                  pl.BlockSpec((rows_, 2 * SSD_N), lambda c: (rv(c), D // (2 * SSD_N) + 1)),
                  pl.BlockSpec((rows_, LANES), lambda c: (rv(c), 0)),
                  pl.BlockSpec((1, LANES), lambda c: (0, 0)), pl.BlockSpec((1, LANES), lambda c: (0, 0)),
                  pl.BlockSpec((CPS, D, SSD_N), lambda c: (rv(c), 0, 0)),
                  pl.BlockSpec((rows_, D), lambda c: (rv(c), 0)),
                  pl.BlockSpec((rows_, D), lambda c: (rv(c), 0))],
        out_specs=[pl.BlockSpec((rows_, XBC), lambda c: (rv(c), 0)),
                   pl.BlockSpec((rows_, LANES), lambda c: (rv(c), 0)),
                   pl.BlockSpec((1, LANES), lambda c: (0, 0)), pl.BlockSpec((1, LANES), lambda c: (0, 0))],
        out_shape=[jax.ShapeDtypeStruct((S, XBC), F32), jax.ShapeDtypeStruct((S, LANES), F32),
                   jax.ShapeDtypeStruct((1, LANES), F32), jax.ShapeDtypeStruct((1, LANES), F32)],
        scratch_shapes=[pltpu.VMEM((D, SSD_N), F32)],
        compiler_params=pltpu.CompilerParams(dimension_semantics=("arbitrary",)),
    )(xbc_c, xbc_c, xbc_c, pd, dtb, alog, states, dy, dxs_skip)


def _ssd_post_core(y, xs, z, d128, nw):
    tm = y.shape[0]
    ex = (_iota((LANES, D), 1) // HD == _iota((LANES, D), 0)).astype(F32)
    d_x = jnp.sum(_fdot(jnp.broadcast_to(d128, (8, LANES)), ex), axis=0, keepdims=True) * 0.125
    y2 = (y + d_x * xs) * _silu(z)
    lo = _iota((tm, D), 1) < D // 2
    sq = y2 * y2
    ms0 = jnp.sum(jnp.where(lo, sq, 0.0), axis=-1, keepdims=True) / (D // 2)
    ms1 = jnp.sum(jnp.where(lo, 0.0, sq), axis=-1, keepdims=True) / (D // 2)
    r = jnp.where(lo, lax.rsqrt(ms0 + EPS), lax.rsqrt(ms1 + EPS))
    return y2 * r * nw


def _ssd_post_ins(y, xbc_c, pc, d128, nw):
    return [("row", y, None, 0), ("row", xbc_c, D, 0), ("row", pc, D, 0), ("const", d128, None, 0), ("const", nw, None, 0)]


def _ssd_post_fwd(y, xbc_c, pc, d128, nw, name):
    S = y.shape[0]
    return _rows(name, lambda ctx, *v: [_ssd_post_core(*v)], _ssd_post_ins(y, xbc_c, pc, d128, nw),
                 [(D, D, 0, BF16)], tm=512, nrows=S)[0]


def _ssd_post_bwd(y, xbc_c, pc, d128, nw, dout, name):
    S = y.shape[0]

    def fn(ctx, *v):
        _, vjp = jax.vjp(_ssd_post_core, *v[:5])
        return list(vjp(v[5]))

    return _rows(name, fn, _ssd_post_ins(y, xbc_c, pc, d128, nw) + [("row", dout, None, 0)],
                 [(D, D, 0, F32), (D, D, 0, F32), (D, D, 0, BF16)], [(1, LANES, LANES), (1, D, D)], tm=512, nrows=S)


def _gates_core(g0, g1, g2, b0, b1, b2, ya, yb, yc):
    return _sigmoid(g0 + b0) * ya + _sigmoid(g1 + b1) * yb + _sigmoid(g2 + b2) * yc


def _gate_parts(pdv, bv):
    gp = pltpu.roll(pdv, SEC_D - 16, 1)
    return [gp[:, k * D:(k + 1) * D] for k in range(3)] + [bv[:, k * D:(k + 1) * D] for k in range(3)]


def _gates_fwd(pd, bg, ya, yb, yc, name):
    S = pd.shape[0]

    def fn(ctx, pdv, bv, a, b, c):
        return [_gates_core(*_gate_parts(pdv, bv), a, b, c)]

    return _rows(name, fn, [("row", pd, None, 0), ("const", bg, None, 0), ("row", ya, None, 0), ("row", yb, None, 0),
                            ("row", yc, None, 0)], [(D, D, 0, BF16)], tm=512, nrows=S)[0]


def _gates_post(dm, pdv, a, b, c, bv):
    _, vjp = jax.vjp(_gates_core, *_gate_parts(pdv, bv), a, b, c)
    g = vjp(dm)
    return [g[6], g[7], g[8], jnp.concatenate(g[0:3], axis=1), jnp.concatenate(g[3:6], axis=1)]


def _adam_update(wv, gv, mv, vv):
    m2 = ADAM_B1 * mv + (1.0 - ADAM_B1) * gv
    v2 = ADAM_B2 * vv + (1.0 - ADAM_B2) * jnp.square(gv)
    m_hat = m2 / (1.0 - ADAM_B1 ** ADAM_STEP)
    v_hat = v2 / (1.0 - ADAM_B2 ** ADAM_STEP)
    delta = -ADAM_LR * (m_hat / (jnp.sqrt(v_hat) + ADAM_EPS) + ADAM_WD * wv)
    return [delta, m2, v2]


def _adamw(w, g, m, v, name):
    rows, C = w.shape
    tm = _pick(rows, [t for t in (512, 256, 128, 64, 32, 16, 8) if t * C <= ADAM_TILE])
    return _rows(name, lambda ctx, *a: _adam_update(*a), [("row", a, None, 0) for a in (w, g, m, v)],
                 [(C, C, 0, F32)] * 3, tm=tm, nrows=rows)


def _position():
    return lax.axis_index("x"), lax.axis_index("y"), lax.axis_index("c")


def _other_chips(x, y):
    return [(1 - x, y), (x, 1 - y), (1 - x, 1 - y)]


_HBM = pl.BlockSpec(memory_space=pltpu.HBM)


def _gather_parts(half, lo, n):
    def copies(p_ref, out_ref, send_sems, recv_sems):
        x, y, c = _position()
        sibling = (x, y, 1 - c)
        chips = _other_chips(x, y)

        def slab(chip, h):
            return out_ref.at[2 * chip[0] + chip[1], pl.ds(h * half + lo, n), :]

        def copy(k, src, dst, to):
            return pltpu.make_async_remote_copy(src_ref=src, dst_ref=dst, send_sem=send_sems.at[k],
                                                recv_sem=recv_sems.at[k], device_id=to, device_id_type=MESH)

        first = [copy(j, p_ref.at[pl.ds(c * half + lo, n), :], slab((x, y), c), (*chip, c)) for j, chip in enumerate(chips)]
        passed = [copy(3 + j, slab(chip, c), slab(chip, c), sibling) for j, chip in enumerate(chips)]
        from_chips = [copy(j, slab(chip, c), slab(chip, c), (x, y, c)) for j, chip in enumerate(chips)]
        from_sibling = [copy(3 + j, slab(chip, 1 - c), slab(chip, 1 - c), (x, y, c)) for j, chip in enumerate(chips)]
        return first, passed, from_chips, from_sibling

    def start(ins, outs, scr):
        for cp in copies(ins[0], outs[0], *scr)[0]:
            cp.start()

    def finish(ins, outs, scr):
        first, passed, from_chips, from_sibling = copies(ins[0], outs[0], *scr)
        for j in range(3):
            from_chips[j].wait_recv()
            passed[j].start()
        for cp in from_sibling:
            cp.wait_recv()
        for cp in first + passed:
            cp.wait_send()

    return start, finish


def _rs_chip_parts(lo, n):
    def copies(h_ref, out_ref, send_sems, recv_sems):
        x, y, c = _position()
        return [pltpu.make_async_remote_copy(src_ref=h_ref.at[2 * chip[0] + chip[1], pl.ds(lo, n), :],
                                             dst_ref=out_ref.at[j, pl.ds(lo, n), :],
                                             send_sem=send_sems.at[j], recv_sem=recv_sems.at[j],
                                             device_id=(*chip, c), device_id_type=MESH)
                for j, chip in enumerate(_other_chips(x, y))]

    def start(ins, outs, scr):
        for cp in copies(ins[0], outs[0], *scr):
            cp.start()

    def finish(ins, outs, scr):
        for cp in copies(ins[0], outs[0], *scr):
            cp.wait()

    return start, finish


class _Stream:
    def __init__(self, src, buf, parts, nsem, units, name):
        self.src, self.buf, self.parts, self.nsem, self.name = src, buf, parts, nsem, name
        self.next, self.units = 0, units

    def _scratch(self):
        return [pltpu.SemaphoreType.DMA((self.nsem,)), pltpu.SemaphoreType.DMA((self.nsem,))]

    def _take(self, units):
        units = min(units, self.units - self.next)
        lo = self.next * 16
        self.next += units
        return lo, units * 16

    def _set(self, outs):
        self.buf = outs[0]

    def hook(self, units):
        lo, n = self._take(units)
        if n == 0:
            return None
        start, finish = self.parts(lo, n)
        return _Hook([self.src, self.buf], [jax.ShapeDtypeStruct(self.buf.shape, self.buf.dtype)], {1: 0},
                     self._scratch(), start, finish, self._set)

    def drain(self):
        lo, n = self._take(self.units)
        if n:
            start, finish = self.parts(lo, n)

            def body(s_ref, b_ref, o_ref, send_sems, recv_sems):
                args = ((s_ref, b_ref), (o_ref,), (send_sems, recv_sems))
                start(*args)
                finish(*args)

            self.buf = pl.pallas_call(
                body, name=self.name, in_specs=[_ANY, _ANY], out_specs=_ANY,
                out_shape=jax.ShapeDtypeStruct(self.buf.shape, self.buf.dtype),
                scratch_shapes=self._scratch(), input_output_aliases={1: 0},
            )(self.src, self.buf)
        return self.buf


def _rs_pair_parts(half, lo, n):
    def copy(g_ref, out_ref, send_sems, recv_sems):
        x, y, c = _position()
        return pltpu.make_async_remote_copy(
            src_ref=g_ref.at[pl.ds(0, 4), pl.ds((1 - c) * half + lo, n), :], dst_ref=out_ref.at[pl.ds(0, 4), pl.ds(lo, n), :],
            send_sem=send_sems.at[0], recv_sem=recv_sems.at[0], device_id=(x, y, 1 - c), device_id_type=MESH)

    def start(ins, outs, scr):
        copy(ins[0], outs[0], *scr).start()

    def finish(ins, outs, scr):
        copy(ins[0], outs[0], *scr).wait()

    return start, finish


def _rs_swap(r, name):
    Rh, C = r.shape

    def body(r_ref, out_ref, send_sem, recv_sem):
        x, y, c = _position()
        cp = pltpu.make_async_remote_copy(src_ref=r_ref, dst_ref=out_ref, send_sem=send_sem,
                                          recv_sem=recv_sem, device_id=(x, y, 1 - c), device_id_type=MESH)
        cp.start()
        cp.wait()

    return pl.pallas_call(
        body, name=name, in_specs=[_HBM], out_specs=_HBM,
        out_shape=jax.ShapeDtypeStruct((Rh, C), r.dtype),
        scratch_shapes=[pltpu.SemaphoreType.DMA, pltpu.SemaphoreType.DMA],
    )(r)


def _rs_add_pair(g, recv, cidx, name):
    _, R, C = g.shape
    Rh = R // 2
    tm = _pick(Rh, (400, 280, 200, 160, 80, 40, 16, 8))
    nt = Rh // tm

    def body(c_ref, g_ref, r_ref, o_ref):
        o_ref[...] = (g_ref[...].astype(F32) + r_ref[...].astype(F32)).astype(o_ref.dtype)

    return pl.pallas_call(
        body, name=name,
        grid_spec=pltpu.PrefetchScalarGridSpec(
            num_scalar_prefetch=1, grid=(4, nt),
            in_specs=[pl.BlockSpec((1, tm, C), lambda k, i, cr: (k, cr[0] * nt + i, 0)),
                      pl.BlockSpec((1, tm, C), lambda k, i, cr: (k, i, 0))],
            out_specs=pl.BlockSpec((1, tm, C), lambda k, i, cr: (k, i, 0))),
        out_shape=jax.ShapeDtypeStruct((4, Rh, C), BF16),
    )(cidx, g, recv)


def _rs_add_chips(h, recv, chip_idx, name):
    _, Rh, C = h.shape
    tm = _pick(Rh, (400, 280, 200, 160, 80, 40, 16, 8))

    def body(c_ref, h_ref, r_ref, o_ref):
        acc = h_ref[0].astype(F32)
        for j in range(3):
            acc = acc + r_ref[j].astype(F32)
        o_ref[...] = acc

    return pl.pallas_call(
        body, name=name,
        grid_spec=pltpu.PrefetchScalarGridSpec(
            num_scalar_prefetch=1, grid=(Rh // tm,),
            in_specs=[pl.BlockSpec((1, tm, C), lambda i, cr: (cr[0], i, 0)), pl.BlockSpec((3, tm, C), lambda i, cr: (0, i, 0))],
            out_specs=pl.BlockSpec((tm, C), lambda i, cr: (i, 0))),
        out_shape=jax.ShapeDtypeStruct((Rh, C), F32),
    )(chip_idx, h, recv)


def _all_reduce_small(vec, name):
    n, C = vec.shape

    def body(v_ref, out_ref, buf, send_sems, recv_sems):
        x, y, c = _position()

        def flip(k):
            return ((1 - x) if k & 4 else x, (1 - y) if k & 2 else y, (1 - c) if k & 1 else c)

        def idx(p):
            return 4 * p[0] + 2 * p[1] + p[2]

        me = idx((x, y, c))
        buf[me] = v_ref[...]
        cps = [pltpu.make_async_remote_copy(src_ref=v_ref, dst_ref=buf.at[me], send_sem=send_sems.at[k - 1],
                                            recv_sem=recv_sems.at[k - 1], device_id=flip(k), device_id_type=MESH)
               for k in range(1, 8)]
        for cp in cps:
            cp.start()
        for k in range(1, 8):
            pltpu.make_async_remote_copy(src_ref=v_ref, dst_ref=buf.at[idx(flip(k))], send_sem=send_sems.at[k - 1],
                                         recv_sem=recv_sems.at[k - 1], device_id=flip(k), device_id_type=MESH).wait_recv()
        for cp in cps:
            cp.wait_send()
        acc = buf[0]
        for s in range(1, 8):
            acc = acc + buf[s]
        out_ref[...] = acc

    return pl.pallas_call(
        body, name=name,
        in_specs=[pl.BlockSpec(memory_space=pltpu.VMEM)], out_specs=pl.BlockSpec(memory_space=pltpu.VMEM),
        out_shape=jax.ShapeDtypeStruct((n, C), F32),
        scratch_shapes=[pltpu.VMEM((8, n, C), F32), pltpu.SemaphoreType.DMA((7,)), pltpu.SemaphoreType.DMA((7,))],
    )(vec)


BIG = (("w_in", (D, IN_WIDTH // 4), "cols"), ("w_a", (GW, D // 4), "cols"), ("pool_w", (4, PG // 4, PG), "pool"),
       ("w_b", (D // 4, D), "rows"), ("w_c", (D // 4, D), "rows"), ("w_o", (D // 4, D), "rows"),
       ("ffn_w_up", (D, 2 * D_FF // 4), "cols"), ("ffn_w_down", (D_FF // 4, D), "rows"))
def _pack_rows(s):
    k = math.prod(s) // D
    return -(-k // 16) * 16, k


PACK_ROWS = sum(_pack_rows(s)[0] for _, s, _ in BIG)
PACK_PAD = -(-PACK_ROWS // 32) * 32


def _pad_rows(v, rows):
    pad = [(0, 0)] * v.ndim
    pad[-2] = (0, rows - v.shape[-2])
    return jnp.pad(v, pad) if rows > v.shape[-2] else v


def _pack_blocks(blocks, dtype):
    lead = blocks["w_in"].shape[:-2]
    flat = []
    for n, s, how in BIG:
        v = blocks[n].astype(dtype)
        if how == "cols":
            v = jnp.swapaxes(v, -1, -2)
        flat.append(_pad_rows(v.reshape(*lead, -1, D), _pack_rows(s)[0]))
    flat.append(jnp.zeros((*lead, PACK_PAD - PACK_ROWS, D), dtype))
    return jnp.concatenate(flat, axis=-2)


def _unpack_blocks(pack):
    out, r = {}, 0
    for n, s, how in BIG:
        rows, k = _pack_rows(s)
        v = pack[r:r + k, :]
        out[n] = v.reshape(s[1], s[0]).T if how == "cols" else v.reshape(s)
        r += rows
    return out


def _operands(allp):
    out, r = {}, 0
    for n, s, how in BIG:
        rows, k = _pack_rows(s)
        v = allp[:, r:r + k, :]
        if how == "cols":
            out[n] = v.reshape(4 * s[1], s[0])
        elif how == "rows":
            out[n] = v.reshape(4 * s[0], s[1])
        else:
            out[n] = v.reshape(4, *s).transpose(1, 0, 2, 3).reshape(4, PG, PG)
        r += rows
    return out


def _pack_operands(g, dtype):
    flat = []
    for n, s, how in BIG:
        v = g[n].astype(dtype)
        if how == "pool":
            v = v.reshape(4, 4, s[1], s[2]).transpose(1, 0, 2, 3)
        flat.append(_pad_rows(v.reshape(4, -1, D), _pack_rows(s)[0]))
    flat.append(jnp.zeros((4, PACK_PAD - PACK_ROWS, D), dtype))
    return jnp.concatenate(flat, axis=1)


def _layer_fwd(x, w, sm, bias, hk):
    pa, u = _mmf(None, w["in_a"], tb=True, pre=(_rms_core, [x], [sm["ln1_g"]]), name="in_a", tm=1024, hook=hk("in_a"))
    pb = _mm(u, w["in_b"], tb=True, out_dtype=BF16, name="in_b", hook=hk("in_b"))
    pc = _mm(u, w["in_c"], tb=True, out_dtype=BF16, name="in_c", hook=hk("in_c"))
    pd = _mm(u, w["in_d"], tb=True, out_dtype=BF16, name="in_d", hook=hk("in_d"))
    os_, ls_ = [], []
    for gi in range(3):
        o, l = _attn_fwd(pa, bias[gi], gi, "attn_fwd%d" % gi)
        os_.append(o)
        ls_.append(l)
    att = _mix_fwd(os_, ls_, "mix_fwd")
    ya = _mm(att, w["w_a"], tb=True, out_dtype=BF16, name="mm_wa")
    pool_o = _pool_fwd(pb, w["pool_w"], sm["pool_scale"], "pool_fwd")
    yb = _mm(pool_o, w["w_b"], out_dtype=BF16, name="mm_wb")
    xbc_c = _ssd_conv_fwd(pc, sm["ssd_conv_w"], sm["ssd_conv_b"], "ssd_conv_fwd")
    y_scan, states = _ssd_scan_fwd(xbc_c, pd, sm["ssd_dt_bias"], sm["ssd_a_log"], "ssd_scan_fwd")
    ssd_o = _ssd_post_fwd(y_scan, xbc_c, pc, sm["ssd_d"], sm["ssd_norm_w"], "ssd_post_fwd")
    yc = _mm(ssd_o, w["w_c"], out_dtype=BF16, name="mm_wc")
    merged = _gates_fwd(pd, sm["b_gate"], ya, yb, yc, "gates_fwd")
    x1 = _mm(merged, w["w_o"], add=x, name="mm_wo", hook=hk("mm_wo"))
    h, u2 = _mmf(None, w["ffn_w_up"], tb=True, pre=(_rms_core, [x1], [sm["ln2_g"]]), out_dtype=BF16, name="mm_up",
                 tm=1024, hook=hk("mm_up"))
    f = _ffn_act_fwd(h, sm["ffn_conv_w"], sm["ffn_conv_b"], "ffn_act_fwd")
    x2 = _mm(f, w["ffn_w_down"], add=x1, name="mm_down", hook=hk("mm_down"))
    saved = dict(x=x, u=u, pa=pa, pb=pb, pc=pc, pd=pd, os=os_, ls=ls_, att=att, ya=ya, yb=yb, yc=yc, pool_o=pool_o,
                 xbc_c=xbc_c, y_scan=y_scan, states=states, ssd_o=ssd_o, merged=merged, x1=x1, u2=u2, h=h, f=f)
    return x2, saved


def _layer_bwd(dx2, dx2b, w, sm, bias, dbs, sv, hk):
    gw, gs = {}, {}
    S = dx2.shape[0]

    def gmm(a, b, name):
        return _mm(a, b, ta=True, out_dtype=BF16, name=name, hook=hk(name))

    df = _mm(dx2b, w["ffn_w_down"], tb=True, out_dtype=BF16, name="d_f", hook=hk("d_f"))
    gw["ffn_w_down"] = gmm(sv["f"], dx2b, "g_down")
    dha, dhv, gs["ffn_conv_w"], gs["ffn_conv_b"] = _ffn_act_bwd(sv["h"], sm["ffn_conv_w"], sm["ffn_conv_b"], df, "ffn_act_bwd")
    dx1, dx1b, gs["ln2_g"] = _mmf([dha, dhv], [w["up_a"], w["up_v"]], name="d_u2_v", tm=256, hook=hk("d_u2_v"),
                                  post=(_rms_post, [sv["x1"], dx2], [sm["ln2_g"]], RMS_POST_OUTS, [(1, D)]))
    gw["ffn_w_up"] = jnp.concatenate([gmm(dha, sv["u2"], "g_up_a"), gmm(dhv, sv["u2"], "g_up_v")], axis=0)
    dya, dyb, dyc, dgate, gs["b_gate"] = _mmf(
        dx1b, w["w_o"], tb=True, name="d_merged", tm=256, hook=hk("d_merged"),
        post=(_gates_post, [sv["pd"], sv["ya"], sv["yb"], sv["yc"]], [sm["b_gate"]],
              [(D, BF16)] * 3 + [(3 * D, BF16)], [(1, 3 * D)]))
    gw["w_o"] = gmm(sv["merged"], dx1b, "g_wo")
    dssd_o = _mm(dyc, w["w_c"], tb=True, name="d_ssd_o")
    gw["w_c"] = gmm(sv["ssd_o"], dyc, "g_wc")
    dy_scan, dxs_skip, dz, gs["ssd_d"], gs["ssd_norm_w"] = _ssd_post_bwd(
        sv["y_scan"], sv["xbc_c"], sv["pc"], sm["ssd_d"], sm["ssd_norm_w"], dssd_o, "ssd_post_bwd")
    dxbc_c, ddt, gs["ssd_dt_bias"], gs["ssd_a_log"] = _ssd_scan_bwd(
        sv["xbc_c"], sv["pd"], sm["ssd_dt_bias"], sm["ssd_a_log"], sv["states"], dy_scan, dxs_skip, "ssd_scan_bwd")
    dxbc, gs["ssd_conv_w"], gs["ssd_conv_b"] = _ssd_conv_bwd(sv["pc"], sm["ssd_conv_w"], sm["ssd_conv_b"], dxbc_c, "ssd_conv_bwd")
    dpool_o = _mm(dyb, w["w_b"], tb=True, name="d_pool_o")
    gw["w_b"] = gmm(sv["pool_o"], dyb, "g_wb")
    dpb, dpw, gs["pool_scale"] = _pool_bwd(sv["pb"], w["pool_w"], sm["pool_scale"], dpool_o, "pool_bwd")
    gw["pool_w"] = dpw.reshape(4, PG, PG)
    datt = _mm(dya, w["w_a"], name="d_att")
    gw["w_a"] = gmm(dya, sv["att"], "g_wa")
    dos, dls = _mix_bwd(sv["os"], sv["ls"], datt, "mix_bwd")
    dqkv = tuple(lax.empty((S, AW), F32) for _ in range(3))
    dbs = list(dbs)
    for gi in range(3):
        dqkv, dbs[gi] = _attn_bwd(sv["pa"], bias[gi], dos[gi], dls[gi], dbs[gi], dqkv, gi, "attn_bwd%d" % gi)
    u = sv["u"]
    pieces = [(dqkv[0], "wq"), (dqkv[1], "wk"), (dqkv[2], "wv"), (dpb, "in_b"), (dz, "wz"), (dxbc, "wxbc"),
              (ddt, "wdt"), (dgate, "wgate")]
    du = _mmf([dp for dp, _ in pieces[:4]], [w[key] for _, key in pieces[:4]], name="d_u_a", tm=256, hook=hk("d_u_a"))[0]
    dx, dxb, gs["ln1_g"] = _mmf([dp for dp, _ in pieces[4:]], [w[key] for _, key in pieces[4:]], add=du,
                                name="d_u_wgate", tm=256, hook=hk("d_u_wgate"),
                                post=(_rms_post, [sv["x"], dx1], [sm["ln1_g"]], RMS_POST_OUTS, [(1, D)]))
    g_in = []
    for dp, key in pieces:
        g = gmm(dp, u, "g_in_" + key)
        g_in.append(g[:SSD_HEADS] if key == "wdt" else g)
    gw["w_in"] = jnp.concatenate(g_in, axis=0)
    return dx, dxb, gw, gs, dbs


SMALL_LAYER = ("ln1_g", "b_gate", "pool_scale", "ssd_conv_w", "ssd_conv_b", "ssd_dt_bias", "ssd_a_log", "ssd_d",
               "ssd_norm_w", "ln2_g", "ffn_conv_w", "ffn_conv_b")


def _pad_lanes(v):
    return jnp.pad(v, (0, LANES - v.shape[0])).reshape(1, LANES)


def _layer_weights(ops):
    wt = ops["w_in"]
    o1, o2, o3 = SEC_A, SEC_A + SEC_B, SEC_A + SEC_B + SEC_C
    w = dict(ops)
    w["in_a"] = jnp.pad(wt[:o1], ((0, SEC_A_PAD - o1), (0, 0)))
    w["in_b"] = wt[o1:o2]
    w["in_c"] = wt[o2:o3]
    w["in_d"] = jnp.pad(wt[o3:], ((0, SEC_D - (IN_WIDTH - o3)), (0, 0)))
    w["wq"], w["wk"], w["wv"] = wt[:AW], wt[AW:2 * AW], wt[2 * AW:o1]
    w["wz"], w["wxbc"] = wt[o2:o2 + D], wt[o2 + D:o3]
    w["wdt"] = jnp.pad(wt[o3:o3 + SSD_HEADS], ((0, LANES - SSD_HEADS), (0, 0)))
    w["wgate"] = wt[o3 + SSD_HEADS:]
    w["up_a"], w["up_v"] = ops["ffn_w_up"][:D_FF], ops["ffn_w_up"][D_FF:]
    return w


def _layer_small(p, i):
    sm = {n: p[n][i] for n in SMALL_LAYER}
    out = {}
    for n, v in sm.items():
        if n in ("ssd_dt_bias", "ssd_a_log", "ssd_d"):
            out[n] = _pad_lanes(v)
        elif v.ndim == 1:
            out[n] = v.reshape(1, -1)
        else:
            out[n] = v
    return out


def _local_step(x, target, rel_bias, final_g, layer_full, small, fwd_hooks=None, bwd_hooks=None, after_bwd=None):
    nl = small["ln1_g"].shape[0]
    buckets = [_buckets(d).astype(jnp.int32) for d in DILATIONS]
    bias = [_bias_table(rel_bias, buckets[gi], gi, "bias_table%d" % gi) for gi in range(3)]
    no_hooks = lambda i: (lambda name: None)
    fwd_hooks = fwd_hooks or no_hooks
    bwd_hooks = bwd_hooks or no_hooks
    saved, ws, sms = [], [], []
    h = x
    for i in range(nl):
        w = _layer_weights(layer_full(i))
        sm = _layer_small(small, i)
        h, sv = _layer_fwd(h, w, sm, bias, fwd_hooks(i))
        saved.append(sv)
        ws.append(w)
        sms.append(sm)
    dh, dhb, dfinal, loss = _final_loss(h, target, final_g.reshape(1, D))
    gws, gss = [None] * nl, [None] * nl
    dbs = [jnp.zeros((6, WIN, 2 * WIN), F32)] * 3
    for i in reversed(range(nl)):
        dh, dhb, gws[i], gss[i], dbs = _layer_bwd(dh, dhb, ws[i], sms[i], bias, dbs, saved[i], bwd_hooks(i))
        if after_bwd is not None:
            after_bwd(i, gws[i])
    drel = []
    for gi in range(3):
        onehot = jnp.pad(jax.nn.one_hot(buckets[gi].reshape(-1), REL_BUCKETS, dtype=BF16), ((0, 0), (0, LANES - REL_BUCKETS)))
        drel.append(_mm(dbs[gi].reshape(6, WIN * 2 * WIN), onehot, name="g_relb"))
    return loss, dh, gws, gss, dfinal, jnp.concatenate(drel, axis=0)


WEIGHTS = ("rel_bias", "ln1_g", "w_in", "b_gate", "w_a", "pool_w", "pool_scale", "w_b", "ssd_conv_w", "ssd_conv_b",
           "ssd_dt_bias", "ssd_a_log", "ssd_d", "ssd_norm_w", "w_c", "w_o", "ln2_g", "ffn_w_up", "ffn_conv_w",
           "ffn_conv_b", "ffn_w_down", "final_g")
BIG_NAMES = tuple(n for n, _, _ in BIG)
SHARDED_SMALL = {"ssd_conv_w": XBC // 4, "ffn_conv_w": 2 * D_FF // 4}


def _to_rows(flat):
    n = flat.shape[0]
    rows = -(-n // LANES)
    rows = -(-rows // 8) * 8
    return jnp.pad(flat, (0, rows * LANES - n)).reshape(rows, LANES)


def _flatten(tree, names):
    return jnp.concatenate([tree[n].reshape(-1) for n in names])


def _unflatten(flat, shapes, names):
    out, o = {}, 0
    for n in names:
        k = math.prod(shapes[n])
        out[n] = flat[o:o + k].reshape(shapes[n])
        o += k
    return out


def kernel(x, rel_bias, ln1_g, w_in, b_gate, w_a, pool_w, pool_scale, w_b, ssd_conv_w, ssd_conv_b, ssd_dt_bias, ssd_a_log, ssd_d, ssd_norm_w, w_c, w_o, ln2_g, ffn_w_up, ffn_conv_w, ffn_conv_b, ffn_w_down, final_g, loss_target, m_rel_bias, m_ln1_g, m_w_in, m_b_gate, m_w_a, m_pool_w, m_pool_scale, m_w_b, m_ssd_conv_w, m_ssd_conv_b, m_ssd_dt_bias, m_ssd_a_log, m_ssd_d, m_ssd_norm_w, m_w_c, m_w_o, m_ln2_g, m_ffn_w_up, m_ffn_conv_w, m_ffn_conv_b, m_ffn_w_down, m_final_g, v_rel_bias, v_ln1_g, v_w_in, v_b_gate, v_w_a, v_pool_w, v_pool_scale, v_w_b, v_ssd_conv_w, v_ssd_conv_b, v_ssd_dt_bias, v_ssd_a_log, v_ssd_d, v_ssd_norm_w, v_w_c, v_w_o, v_ln2_g, v_ffn_w_up, v_ffn_conv_w, v_ffn_conv_b, v_ffn_w_down, v_final_g):
    W = dict(rel_bias=rel_bias, ln1_g=ln1_g, w_in=w_in, b_gate=b_gate, w_a=w_a, pool_w=pool_w, pool_scale=pool_scale,
             w_b=w_b, ssd_conv_w=ssd_conv_w, ssd_conv_b=ssd_conv_b, ssd_dt_bias=ssd_dt_bias, ssd_a_log=ssd_a_log,
             ssd_d=ssd_d, ssd_norm_w=ssd_norm_w, w_c=w_c, w_o=w_o, ln2_g=ln2_g, ffn_w_up=ffn_w_up,
             ffn_conv_w=ffn_conv_w, ffn_conv_b=ffn_conv_b, ffn_w_down=ffn_w_down, final_g=final_g)
    M = dict(rel_bias=m_rel_bias, ln1_g=m_ln1_g, w_in=m_w_in, b_gate=m_b_gate, w_a=m_w_a, pool_w=m_pool_w,
             pool_scale=m_pool_scale, w_b=m_w_b, ssd_conv_w=m_ssd_conv_w, ssd_conv_b=m_ssd_conv_b,
             ssd_dt_bias=m_ssd_dt_bias, ssd_a_log=m_ssd_a_log, ssd_d=m_ssd_d, ssd_norm_w=m_ssd_norm_w, w_c=m_w_c,
             w_o=m_w_o, ln2_g=m_ln2_g, ffn_w_up=m_ffn_w_up, ffn_conv_w=m_ffn_conv_w, ffn_conv_b=m_ffn_conv_b,
             ffn_w_down=m_ffn_w_down, final_g=m_final_g)
    V = dict(rel_bias=v_rel_bias, ln1_g=v_ln1_g, w_in=v_w_in, b_gate=v_b_gate, w_a=v_w_a, pool_w=v_pool_w,
             pool_scale=v_pool_scale, w_b=v_w_b, ssd_conv_w=v_ssd_conv_w, ssd_conv_b=v_ssd_conv_b,
             ssd_dt_bias=v_ssd_dt_bias, ssd_a_log=v_ssd_a_log, ssd_d=v_ssd_d, ssd_norm_w=v_ssd_norm_w, w_c=v_w_c,
             w_o=v_w_o, ln2_g=v_ln2_g, ffn_w_up=v_ffn_w_up, ffn_conv_w=v_ffn_conv_w, ffn_conv_b=v_ffn_conv_b,
             ffn_w_down=v_ffn_w_down, final_g=v_final_g)
    nl = ln1_g.shape[0]
    px, py, pc_ = _position()
    chip = 2 * px + py
    cidx = jnp.reshape(pc_, (1,)).astype(jnp.int32)
    chip_idx = jnp.reshape(chip, (1,)).astype(jnp.int32)

    placed = {}
    for n, cs in SHARDED_SMALL.items():
        full = jnp.zeros(W[n].shape[:-1] + (4 * cs,), F32)
        full = lax.dynamic_update_slice(full, W[n], (0, 0, chip * cs))
        placed[n] = jnp.where(pc_ == 0, full, 0.0)
    names_sh = tuple(SHARDED_SMALL)
    shapes_sh = {n: placed[n].shape for n in names_sh}
    got = _all_reduce_small(_to_rows(_flatten(placed, names_sh)), "gather_small")
    small = {n: W[n] for n in SMALL_LAYER}
    small.update(_unflatten(got.reshape(-1), shapes_sh, names_sh))

    packs = _pack_blocks({n: W[n] for n in BIG_NAMES}, BF16)

    half = PACK_PAD // 2
    units = half // 16

    def share(weights, total):
        tot = sum(weights.values())
        return {n: math.ceil(total * v / tot) for n, v in weights.items()}

    gathers = {}

    def gather(i):
        if i not in gathers:
            buf = lax.dynamic_update_slice(lax.empty((4, PACK_PAD, D), BF16), packs[i][None], (chip, 0, 0))
            gathers[i] = _Stream(packs[i], buf, functools.partial(_gather_parts, half), 6, units, "gather_w")
        return gathers[i]

    def layer_full(i):
        return _operands(gather(i).drain())

    fwd_share = share(dict(in_a=63, in_c=31, in_d=44, mm_up=83, mm_down=34), units)

    def fwd_hooks(i):
        if i + 1 >= nl:
            return lambda name: None
        return lambda name: gather(i + 1).hook(fwd_share[name]) if name in fwd_share else None

    exchanges = {}
    bwd_share = share(dict(g_down=35, d_u2_v=60, g_up_a=35, g_up_v=35, d_merged=50, d_u_a=60, d_u_wgate=70,
                           g_in_wgate=36), units)

    class Exchange:
        def __init__(self, g):
            self.g = g
            self.pair = _Stream(g, lax.empty((4, half, D), BF16), functools.partial(_rs_pair_parts, half), 1, units, "rs_pair")
            self.hsum = self.chips = None

        def to_chips(self):
            if self.chips is None:
                self.hsum = _rs_add_pair(self.g, self.pair.drain(), cidx, "rs_add_pair")
                self.chips = _Stream(self.hsum, lax.empty((3, half, D), BF16), _rs_chip_parts, 3, units, "rs_chips")
            return self.chips

    def after_bwd(i, gw):
        exchanges[i] = Exchange(_pack_operands(gw, BF16))

    def bwd_hooks(i):
        if i + 1 >= nl:
            return lambda name: None

        def hk(name):
            if name == "d_f":
                return exchanges[i + 1].pair.hook(units)
            return exchanges[i + 1].to_chips().hook(bwd_share[name]) if name in bwd_share else None

        return hk

    loss, dx, gws, gss, dfinal, drel = _local_step(x[0], loss_target[0], rel_bias, final_g, layer_full, small,
                                                   fwd_hooks, bwd_hooks, after_bwd)

    def reduced(i):
        recv3 = exchanges[i].to_chips().drain()
        r = _rs_add_chips(exchanges[i].hsum, recv3, chip_idx, "rs_add_chips")
        other = _rs_swap(r, "rs_swap")
        both = jnp.concatenate([jnp.where(pc_ == 0, r, other), jnp.where(pc_ == 0, other, r)], axis=0)
        return _unpack_blocks(both)

    red = [reduced(i) for i in range(nl)]
    delta, new_m, new_v, grads = {}, {}, {}, {}
    for n in BIG_NAMES:
        shp = W[n].shape
        r2 = lambda a: a.reshape(-1, shp[-1])
        grads[n] = jnp.stack([red[i][n] for i in range(nl)], axis=0)
        res = _adamw(r2(W[n]), r2(grads[n]), r2(M[n]), r2(V[n]), "adamw_" + n)
        delta[n], new_m[n], new_v[n] = [a.reshape(shp) for a in res]

    sg = {}
    for n in SMALL_LAYER:
        sg[n] = jnp.stack([gss[i][n] for i in range(nl)], axis=0)
    for n in ("ssd_dt_bias", "ssd_a_log", "ssd_d"):
        sg[n] = sg[n][:, 0, :SSD_HEADS]
    sg["rel_bias"] = drel[:, :REL_BUCKETS].T
    sg["final_g"] = dfinal.reshape(D)
    sg["loss"] = loss[0, :1]
    names_sg = tuple(sg)
    shapes_sg = {n: ((nl,) + W[n].shape[1:] if n in SMALL_LAYER and n not in SHARDED_SMALL else
                     (placed[n].shape if n in SHARDED_SMALL else sg[n].shape)) for n in names_sg}
    for n in names_sg:
        sg[n] = sg[n].reshape(shapes_sg[n])
    tot = _all_reduce_small(_to_rows(_flatten(sg, names_sg)), "allreduce_small")
    tot = _unflatten(tot.reshape(-1), shapes_sg, names_sg)
    loss_out = tot.pop("loss").reshape(())
    for n, cs in SHARDED_SMALL.items():
        tot[n] = lax.dynamic_slice(tot[n], (0, 0, chip * cs), tot[n].shape[:-1] + (cs,))
    grads.update(tot)

    names_s = tuple(n for n in WEIGHTS if n not in BIG_NAMES)
    shapes_s = {n: W[n].shape for n in names_s}
    pk = lambda t: _to_rows(_flatten(t, names_s))
    dl, m2, v2 = _adamw(pk(W), pk(grads), pk(M), pk(V), "adamw_small")
    delta.update(_unflatten(dl.reshape(-1), shapes_s, names_s))
    new_m.update(_unflatten(m2.reshape(-1), shapes_s, names_s))
    new_v.update(_unflatten(v2.reshape(-1), shapes_s, names_s))

    return (loss_out, dx[None], *[grads[n] for n in WEIGHTS], *[delta[n] for n in WEIGHTS],
            *[new_m[n] for n in WEIGHTS], *[new_v[n] for n in WEIGHTS])
```

```python
import functools
import math

import jax
import jax.numpy as jnp
from jax import lax
from jax.experimental import pallas as pl
from jax.experimental.pallas import tpu as pltpu

F32 = jnp.float32
BF16 = jnp.bfloat16
MESH = pl.DeviceIdType.MESH

D = 1024
HD = 64
GW = 384
AW = 3 * GW
WIN = 128
DILATIONS = (1, 4, 16)
REL_BUCKETS = 32
REL_MAX_DISTANCE = 2048
POOL_WINDOWS = (2, 4, 8, 16)
PG = 256
SSD_HEADS = 16
SSD_N = 128
SSD_CHUNK = 128
XBC = 1536
D_FF = 2816
EPS = 1e-6
NEG = -1e30
HALO = 16
LANES = 128

SEC_A = 3 * AW
SEC_B = D
SEC_C = D + XBC
SEC_D = 3328
SEC_A_PAD = 3584
IN_WIDTH = SEC_A + SEC_B + SEC_C + 16 + 3 * D

ADAM_LR = 0.001
ADAM_B1 = 0.9
ADAM_B2 = 0.999
ADAM_EPS = 1e-08
ADAM_WD = 0.01
ADAM_STEP = 10
ADAM_TILE = 256 * 1024
MM_VMEM_BYTES = 40 * 1024 * 1024
MM_MAX_OUT_TILE = 1024 * 1024
HBM_BYTES_PER_US = 2.0e6
STEP_US = 0.35
MXU_WIDTH = 256
MXU_FLOPS_PER_US = 0.65e6


_ANY = pl.BlockSpec(memory_space=pl.ANY)


def _pick(d, cands):
    for t in cands:
        if d % t == 0:
            return t
    return d


def _iota(shape, dim):
    return lax.broadcasted_iota(jnp.int32, shape, dim)


def _dg(a, b, ca, cb):
    return lax.dot_general(a.astype(BF16), b.astype(BF16), (((ca,), (cb,)), ((), ())),
                           preferred_element_type=F32)


@jax.custom_vjp
def _bdot_nn(a, b):
    return _dg(a, b, 1, 0)


def _nn_fwd(a, b):
    return _dg(a, b, 1, 0), (a, b)


def _nn_bwd(res, g):
    a, b = res
    return _dg(g, b, 1, 1), _dg(a, g, 0, 0)


_bdot_nn.defvjp(_nn_fwd, _nn_bwd)


@jax.custom_vjp
def _bdot_nt(a, b):
    return _dg(a, b, 1, 1)


def _nt_fwd(a, b):
    return _dg(a, b, 1, 1), (a, b)


def _nt_bwd(res, g):
    a, b = res
    return _dg(g, b, 1, 0), _dg(g, a, 0, 0)


_bdot_nt.defvjp(_nt_fwd, _nt_bwd)


@jax.custom_vjp
def _bdot_tn(a, b):
    return _dg(a, b, 0, 0)


def _tn_fwd(a, b):
    return _dg(a, b, 0, 0), (a, b)


def _tn_bwd(res, g):
    a, b = res
    return _dg(b, g, 1, 1), _dg(a, g, 1, 0)


_bdot_tn.defvjp(_tn_fwd, _tn_bwd)


def _fdot(a, b):
    return jnp.dot(a, b, preferred_element_type=F32, precision=lax.Precision.HIGHEST)


def _sigmoid(x):
    return 0.5 * jnp.tanh(0.5 * x) + 0.5


def _silu(x):
    return x * _sigmoid(x)


def _softplus(x):
    return jnp.maximum(x, 0.0) + jnp.log(1.0 + jnp.exp(-jnp.abs(x)))


def _lane_pick(m, h):
    return jnp.sum(jnp.where(_iota(m.shape, 1) == h, m, 0.0), axis=1, keepdims=True)


def _row_pick(m, h):
    return jnp.sum(jnp.where(_iota(m.shape, 0) == h, m, 0.0), axis=0, keepdims=True)


def _stack_rows(rows, n):
    c = rows[0].shape[1]
    r = _iota((n, c), 0)
    out = jnp.zeros((n, c), F32)
    for k, v in enumerate(rows):
        out = out + jnp.where(r == k, v, 0.0)
    return out


def _mm(a, b, *, ta=False, tb=False, add=None, out_dtype=F32, name, hook=None):
    if ta:
        K, M = a.shape
    else:
        M, K = a.shape
    if tb:
        N, Kb = b.shape
    else:
        Kb, N = b.shape
    assert K == Kb, (a.shape, b.shape, ta, tb)
    tm, tn, tk = _mm_tiles(M, N, K, a.dtype.itemsize, b.dtype.itemsize, jnp.dtype(out_dtype).itemsize,
                           0 if add is None else add.dtype.itemsize)
    ni, nj, nk = M // tm, N // tn, K // tk
    ca = 0 if ta else 1
    cb = 1 if tb else 0
    n_in = 2 if add is None else 3
    n_hin = 0 if hook is None else len(hook.inputs)
    n_hout = 0 if hook is None else len(hook.out_shapes)

    def body(*refs):
        a_ref, b_ref = refs[:2]
        add_ref = None if add is None else refs[2]
        o_ref = refs[n_in + n_hin]
        scr = refs[n_in + n_hin + 1 + n_hout:]
        acc_ref = scr[0] if nk > 1 else None
        hargs = (refs[n_in:n_in + n_hin], refs[n_in + n_hin + 1:n_in + n_hin + 1 + n_hout], scr[1 if nk > 1 else 0:])
        i, j, k = pl.program_id(0), pl.program_id(1), pl.program_id(2)
        if hook is not None:
            @pl.when((i == 0) & (j == 0) & (k == 0))
            def _():
                hook.start(*hargs)

        part = _dg(a_ref[...], b_ref[...], ca, cb)

        def finish(r):
            if add_ref is not None:
                r = r + add_ref[...].astype(F32)
            o_ref[...] = r.astype(o_ref.dtype)

        if nk == 1:
            finish(part)
        else:
            @pl.when(k == 0)
            def _():
                acc_ref[...] = part

            @pl.when((k > 0) & (k < nk - 1))
            def _():
                acc_ref[...] += part

            @pl.when(k == nk - 1)
            def _():
                finish(acc_ref[...] + part)

        if hook is not None:
            @pl.when((i == ni - 1) & (j == nj - 1) & (k == nk - 1))
            def _():
                hook.finish(*hargs)

    a_spec = pl.BlockSpec((tk, tm), lambda i, j, k: (k, i)) if ta else pl.BlockSpec((tm, tk), lambda i, j, k: (i, k))
    b_spec = pl.BlockSpec((tn, tk), lambda i, j, k: (j, k)) if tb else pl.BlockSpec((tk, tn), lambda i, j, k: (k, j))
    in_specs = [a_spec, b_spec]
    args = [a, b]
    if add is not None:
        in_specs.append(pl.BlockSpec((tm, tn), lambda i, j, k: (i, j)))
        args.append(add)
    out_specs = [pl.BlockSpec((tm, tn), lambda i, j, k: (i, j))]
    out_shape = [jax.ShapeDtypeStruct((M, N), out_dtype)]
    scratch = [pltpu.VMEM((tm, tn), F32)] if nk > 1 else []
    aliases = {}
    if hook is not None:
        in_specs += [_ANY] * n_hin
        args += list(hook.inputs)
        out_specs += [_ANY] * n_hout
        out_shape += list(hook.out_shapes)
        scratch += list(hook.scratch)
        aliases = {n_in + hi: 1 + ho for hi, ho in hook.aliases.items()}
    sem = ("parallel", "parallel", "arbitrary") if hook is None else ("arbitrary",) * 3
    res = pl.pallas_call(
        body, name=name, grid=(ni, nj, nk), in_specs=in_specs, out_specs=out_specs, out_shape=out_shape,
        scratch_shapes=scratch, input_output_aliases=aliases,
        compiler_params=pltpu.CompilerParams(dimension_semantics=sem),
    )(*args)
    if hook is not None:
        hook.done(res[1:])
    return res[0]


def _wide(v):
    return v.astype(F32) if v.dtype == BF16 else v


def _mmf(a, b, *, tb=False, add=None, pre=None, post=None, out_dtype=F32, name, tm, hook=None):
    a_list = list(a) if isinstance(a, (list, tuple)) else [a]
    b_list = list(b) if isinstance(b, (list, tuple)) else [b]
    assert len(a_list) == len(b_list) and (len(b_list) == 1 or not (tb or pre))
    b = b_list[0]
    if tb:
        N, K = b.shape
    else:
        K, N = b.shape
    M = pre[1][0].shape[0] if pre else a_list[0].shape[0]
    tn = N if post or N <= 1024 else _pick(N, (512, 256, LANES))
    ni, nj = M // tm, N // tn
    cb = 1 if tb else 0
    pre_fn, pre_rows, pre_consts = pre if pre else (None, [], [])
    post_fn, post_rows, post_consts, post_outs, post_accs = post if post else (None, [], [], [], [])
    hook_in = [] if hook is None else list(hook.inputs)
    hook_out = [] if hook is None else list(hook.out_shapes)

    def row_spec(arr):
        return pl.BlockSpec((tm, arr.shape[1]), lambda i, j: (i, 0))

    def const_spec(arr):
        return pl.BlockSpec(arr.shape, lambda i, j, nd=arr.ndim: (0,) * nd)

    args, in_specs = [], []
    for arr in (a_list if not pre else pre_rows):
        args.append(arr)
        in_specs.append(row_spec(arr))
    for arr in pre_consts:
        args.append(arr)
        in_specs.append(const_spec(arr))
    for arr in b_list:
        args.append(arr)
        in_specs.append(pl.BlockSpec((tn, K), lambda i, j: (j, 0)) if tb else
                        pl.BlockSpec((arr.shape[0], tn), lambda i, j: (0, j)))
    if add is not None:
        args.append(add)
        in_specs.append(pl.BlockSpec((tm, tn), lambda i, j: (i, j)))
    for arr in post_rows:
        args.append(arr)
        in_specs.append(row_spec(arr))
    for arr in post_consts:
        args.append(arr)
        in_specs.append(const_spec(arr))
    n_main = len(args)
    args += hook_in
    in_specs += [_ANY] * len(hook_in)

    out_shape, out_specs = [], []
    if post:
        for c, dt in post_outs:
            out_shape.append(jax.ShapeDtypeStruct((M, c), dt))
            out_specs.append(pl.BlockSpec((tm, c), lambda i, j: (i, 0)))
        for r, c in post_accs:
            out_shape.append(jax.ShapeDtypeStruct((r, c), F32))
            out_specs.append(pl.BlockSpec((r, c), lambda i, j: (0, 0)))
    else:
        out_shape.append(jax.ShapeDtypeStruct((M, N), out_dtype))
        out_specs.append(pl.BlockSpec((tm, tn), lambda i, j: (i, j)))
    if pre:
        out_shape.append(jax.ShapeDtypeStruct((M, K), BF16))
        out_specs.append(pl.BlockSpec((tm, K), lambda i, j: (i, 0)))
    n_out = len(out_shape)
    out_shape += hook_out
    out_specs += [_ANY] * len(hook_out)
    scratch = ([pltpu.VMEM((tm, K), BF16)] if pre else []) + ([] if hook is None else list(hook.scratch))
    aliases = {} if hook is None else {n_main + hi: n_out + ho for hi, ho in hook.aliases.items()}

    def body(*refs):
        ins, outs, scr = refs[:n_main], refs[len(args):len(args) + n_out], refs[len(args) + len(out_shape):]
        hargs = (refs[n_main:len(args)], refs[len(args) + n_out:len(args) + len(out_shape)], scr[1 if pre else 0:])
        i, j = pl.program_id(0), pl.program_id(1)
        if hook is not None:
            @pl.when((i == 0) & (j == 0))
            def _():
                hook.start(*hargs)

        it = iter(ins)
        if pre:
            rows_ = [next(it) for _ in pre_rows]
            consts_ = [next(it) for _ in pre_consts]

            @pl.when(j == 0)
            def _():
                av = pre_fn(*[_wide(r[...]) for r in rows_], *[_wide(r[...]) for r in consts_]).astype(BF16)
                scr[0][...] = av
                outs[-1][...] = av

            ats = [scr[0][...]]
        else:
            ats = [next(it)[...] for _ in a_list]
        p = None
        for at in ats:
            part = _dg(at, next(it)[...], 1, cb)
            p = part if p is None else p + part
        if add is not None:
            p = p + next(it)[...].astype(F32)
        if post:
            rows_ = [next(it) for _ in post_rows]
            consts_ = [next(it) for _ in post_consts]
            res = post_fn(p, *[_wide(r[...]) for r in rows_], *[_wide(r[...]) for r in consts_])
            for r, v in zip(outs[:len(post_outs)], res[:len(post_outs)]):
                r[...] = v.astype(r.dtype)
            for r, v in zip(outs[len(post_outs):], res[len(post_outs):]):
                @pl.when(i == 0)
                def _(r=r, v=v):
                    r[...] = v

                @pl.when(i > 0)
                def _(r=r, v=v):
                    r[...] += v
        else:
            outs[0][...] = p.astype(outs[0].dtype)
        if hook is not None:
            @pl.when((i == ni - 1) & (j == nj - 1))
            def _():
                hook.finish(*hargs)

    res = pl.pallas_call(
        body, name=name, grid=(ni, nj), in_specs=in_specs, out_specs=out_specs, out_shape=out_shape,
        scratch_shapes=scratch, input_output_aliases=aliases,
        compiler_params=pltpu.CompilerParams(dimension_semantics=("arbitrary", "arbitrary")),
    )(*args)
    if hook is not None:
        hook.done(res[n_out:])
    return res[:n_out]


def _mm_tiles(M, N, K, sa, sb, so, sadd):
    def tiles(d):
        return [t for t in range(LANES, min(d, 2048) + 1, LANES) if d % t == 0] or [d]

    best = None
    for tk in [K] + [t for t in tiles(K) if t < K]:
        for tm in tiles(M):
            for tn in tiles(N):
                vmem = 2 * (tm * tk * sa + tk * tn * sb + tm * tn * (so + sadd)) + (tm * tn * 4 if tk < K else 0)
                if vmem > MM_VMEM_BYTES or tm * tn > MM_MAX_OUT_TILE:
                    continue
                a_reads = 1 if tk == K else N // tn
                traffic = M * K * sa * a_reads + K * N * sb * (M // tm) + M * N * (so + sadd)
                steps = (M // tm) * (N // tn) * (K // tk)
                width = -(-tn // MXU_WIDTH) * MXU_WIDTH
                mxu = 2.0 * M * K * N * (width / tn) / MXU_FLOPS_PER_US
                edge = tm * tk * sa + tk * tn * sb + tm * tn * (so + sadd)
                cost = max(traffic / HBM_BYTES_PER_US, mxu) + steps * STEP_US + edge / HBM_BYTES_PER_US
                if best is None or cost < best[0]:
                    best = (cost, tm, tn, tk)
    assert best is not None, (M, N, K)
    return best[1:]


class _Hook:
    def __init__(self, inputs, out_shapes, aliases, scratch, start, finish, done):
        self.inputs, self.out_shapes, self.aliases, self.scratch = inputs, out_shapes, aliases, scratch
        self.start, self.finish, self.done = start, finish, done


class _Ctx:
    def __init__(self, first, last, row0, rows):
        self.first, self.last, self.row0, self.rows = first, last, row0, rows


def _rows(name, fn, ins, outs, accs=(), *, tm, nrows, ncol=1):
    nt = nrows // tm
    hb = tm // HALO
    nh = nrows // HALO
    ins = [(kind, arr, arr.shape[1] if kind == "row" and cw is None else cw, base) for kind, arr, cw, base in ins]
    in_specs, args = [], []
    for kind, arr, cw, base in ins:
        if kind == "row":
            in_specs.append(pl.BlockSpec((tm, cw), lambda j, i, base=base: (i, base + j)))
        elif kind == "prev":
            in_specs.append(pl.BlockSpec((HALO, cw), lambda j, i, base=base: (jnp.maximum(i * hb - 1, 0), base + j)))
        elif kind == "next":
            in_specs.append(pl.BlockSpec((HALO, cw), lambda j, i, base=base: (jnp.minimum((i + 1) * hb, nh - 1), base + j)))
        elif kind in ("const", "raw"):
            in_specs.append(pl.BlockSpec(arr.shape, lambda j, i, nd=arr.ndim: (0,) * nd))
        elif kind == "ccol":
            in_specs.append(pl.BlockSpec((arr.shape[0], cw), lambda j, i, base=base: (0, base + j)))
        else:
            raise ValueError(kind)
        args.append(arr)
    out_specs, out_shape = [], []
    for ctot, cw, base, dt in outs:
        out_specs.append(pl.BlockSpec((tm, cw), lambda j, i, base=base: (i, base + j)))
        out_shape.append(jax.ShapeDtypeStruct((nrows, ctot), dt))
    for r, ctot, cw in accs:
        out_specs.append(pl.BlockSpec((r, cw), lambda j, i: (0, j)))
        out_shape.append(jax.ShapeDtypeStruct((r, ctot), F32))
    n_in, n_out = len(ins), len(outs)

    def body(*refs):
        i = pl.program_id(1)
        in_refs, out_refs, acc_refs = refs[:n_in], refs[n_in:n_in + n_out], refs[n_in + n_out:]
        if acc_refs:
            @pl.when(i == 0)
            def _():
                for r in acc_refs:
                    r[...] = jnp.zeros_like(r)

        vals = [r[...] if s[0] == "raw" else _wide(r[...]) for r, s in zip(in_refs, ins)]
        res = fn(_Ctx(i == 0, i == nt - 1, i * tm, tm), *vals)
        for r, v in zip(out_refs, res[:n_out]):
            r[...] = v.astype(r.dtype)
        for r, v in zip(acc_refs, res[n_out:]):
            r[...] += v

    res = pl.pallas_call(
        body, name=name, grid=(ncol, nt), in_specs=in_specs, out_specs=out_specs, out_shape=out_shape,
        compiler_params=pltpu.CompilerParams(dimension_semantics=("arbitrary", "arbitrary")),
    )(*args)
    return res


def _shift_down(xcat, k):
    return xcat if k == 0 else pltpu.roll(xcat, k, 0)


def _shift_up(xcat, k):
    return xcat if k == 0 else pltpu.roll(xcat, xcat.shape[0] - k, 0)


def _with_prev(ctx, halo, x):
    return jnp.concatenate([jnp.where(ctx.first, 0.0, halo), x], axis=0)


def _with_next(ctx, x, halo):
    return jnp.concatenate([x, jnp.where(ctx.last, 0.0, halo)], axis=0)


def _rms_core(x, g):
    r = lax.rsqrt(jnp.mean(x * x, axis=-1, keepdims=True) + EPS)
    return x * r * g


def _rms_post(du, xv, drv, gv):
    _, vjp = jax.vjp(_rms_core, xv, gv)
    dx, dg = vjp(du)
    return [drv + dx, drv + dx, dg]


RMS_POST_OUTS = [(D, F32), (D, BF16)]


def _final_loss(x, target, g):
    S = x.shape[0]

    def fn(ctx, xv, tv, gv):
        def f(xx, gg):
            err = _rms_core(xx, gg) - tv
            return 0.5 * jnp.sum(err * err) / D

        loss, vjp = jax.vjp(f, xv, gv)
        dx, dg = vjp(jnp.ones((), F32))
        return [dx, dx, dg, jnp.zeros((1, LANES), F32) + loss]

    return _rows("final_loss", fn, [("row", x, None, 0), ("row", target, None, 0), ("const", g, None, 0)],
                 [(D, D, 0, F32), (D, D, 0, BF16)], [(1, D, D), (1, LANES, LANES)], tm=256, nrows=S)


def _attn_valid(n):
    qi = _iota((WIN, 2 * WIN), 0)
    kk = _iota((WIN, 2 * WIN), 1)
    rel = qi + WIN - kk
    return (rel >= 0) & (rel <= WIN) & ((kk >= WIN) | (n > 0))


def _attn_block(q, kp, kc, vp, vc, b0, b1):
    k = jnp.concatenate([kp, kc], axis=0)
    v = jnp.concatenate([vp, vc], axis=0)
    lo = _iota((WIN, LANES), 1) < HD
    scale = 1.0 / math.sqrt(HD)
    os_, ls_ = [], []
    for hh, b in ((0, b0), (1, b1)):
        qm = jnp.where(lo if hh == 0 else ~lo, q, 0.0)
        s = _bdot_nt(qm, k) * scale + b
        m = lax.stop_gradient(jnp.max(s, axis=1, keepdims=True))
        p = jnp.exp(s - m)
        l = jnp.sum(p, axis=1, keepdims=True)
        os_.append(_bdot_nn(p, v) / l)
        ls_.append(m + jnp.log(l))
    return jnp.where(lo, os_[0], os_[1]), jnp.where(lo, ls_[0], ls_[1])


def _residue_rows(r, d):
    return pl.ds(0, WIN) if d == 1 else pl.ds(r, WIN, stride=d)


def _for_residues(d, fn):
    if d == 1:
        fn(0, 0)
    else:
        lax.fori_loop(0, d, fn, 0, unroll=min(d, 8))


def _pairs_per_step(d):
    return 3 if d == 1 else 1


def _bias_table(rel_bias, bucket, gi, name):
    def body(t_ref, b_ref, o_ref):
        h = 6 * gi + pl.program_id(0)
        b = b_ref[...]
        acc = jnp.zeros(b.shape, F32)
        for k in range(REL_BUCKETS):
            acc = jnp.where(b == k, t_ref[k, h], acc)
        o_ref[0] = acc

    return pl.pallas_call(
        body, name=name, grid=(6,),
        in_specs=[pl.BlockSpec(memory_space=pltpu.SMEM), pl.BlockSpec((WIN, 2 * WIN), lambda h: (0, 0))],
        out_specs=pl.BlockSpec((1, WIN, 2 * WIN), lambda h: (h, 0, 0)),
        out_shape=jax.ShapeDtypeStruct((6, WIN, 2 * WIN), F32),
    )(rel_bias, bucket)


def _attn_fwd(pa, bias, gi, name):
    S = pa.shape[0]
    d = DILATIONS[gi]
    bt = WIN * d
    nb = S // bt
    hpw = _pairs_per_step(d)
    bw = hpw * LANES
    cb = 3 * gi // hpw

    def body(q_ref, kp_ref, kc_ref, vp_ref, vc_ref, b_ref, o_ref, l_ref):
        valid = _attn_valid(pl.program_id(1))
        bm = [jnp.where(valid, b_ref[k], NEG) for k in range(2 * hpw)]

        def residue(r, carry):
            sl = _residue_rows(r, d)
            for t in range(hpw):
                ln = pl.ds(t * LANES, LANES)
                o, lse = _attn_block(q_ref[sl, ln], kp_ref[sl, ln], kc_ref[sl, ln], vp_ref[sl, ln], vc_ref[sl, ln],
                                     bm[2 * t], bm[2 * t + 1])
                o_ref[sl, ln] = o
                l_ref[sl, ln] = lse
            return carry

        _for_residues(d, residue)

    def spec(off, prev):
        if prev:
            return pl.BlockSpec((bt, bw), lambda hp, n: (jnp.maximum(n - 1, 0), off // hpw + cb + hp))
        return pl.BlockSpec((bt, bw), lambda hp, n: (n, off // hpw + cb + hp))

    ospec = pl.BlockSpec((bt, bw), lambda hp, n: (n, hp))
    return pl.pallas_call(
        body, name=name, grid=(3 // hpw, nb),
        in_specs=[spec(0, False), spec(9, True), spec(9, False), spec(18, True), spec(18, False),
                  pl.BlockSpec((2 * hpw, WIN, 2 * WIN), lambda hp, n: (hp, 0, 0))],
        out_specs=[ospec, ospec],
        out_shape=[jax.ShapeDtypeStruct((S, GW), F32)] * 2,
        compiler_params=pltpu.CompilerParams(dimension_semantics=("parallel", "arbitrary")),
    )(pa, pa, pa, pa, pa, bias)


def _attn_bwd(pa, bias, do, dlse, db_in, dqkv, gi, name):
    S = pa.shape[0]
    d = DILATIONS[gi]
    bt = WIN * d
    nb = S // bt
    hpw = _pairs_per_step(d)
    bw = hpw * LANES
    cb = 3 * gi // hpw

    def body(q_ref, kp_ref, kc_ref, vp_ref, vc_ref, b_ref, do_ref, dl_ref, dbi_ref, dqi_ref, dki_ref, dvi_ref,
             dq_ref, dk_ref, dv_ref, db_ref, ck, cv):
        n = pl.program_id(1)

        @pl.when(n == 0)
        def _():
            db_ref[...] = dbi_ref[...]
            ck[...] = jnp.zeros_like(ck)
            cv[...] = jnp.zeros_like(cv)

        @pl.when(n < nb)
        def _():
            valid = _attn_valid(n)
            bm = [jnp.where(valid, b_ref[k], NEG) for k in range(2 * hpw)]

            def residue(r, carry):
                sl = _residue_rows(r, d)
                cs = pl.ds(pl.multiple_of(r * WIN, WIN), WIN)
                for t in range(hpw):
                    ln = pl.ds(t * LANES, LANES)
                    _, vjp = jax.vjp(_attn_block, q_ref[sl, ln], kp_ref[sl, ln], kc_ref[sl, ln], vp_ref[sl, ln],
                                     vc_ref[sl, ln], bm[2 * t], bm[2 * t + 1])
                    dq, dkp, dkc, dvp, dvc, db0, db1 = vjp((do_ref[sl, ln], dl_ref[sl, ln]))
                    dq_ref[sl, ln] = dq
                    dk_ref[sl, ln] = ck[cs, ln] + dkp
                    dv_ref[sl, ln] = cv[cs, ln] + dvp
                    ck[cs, ln] = dkc
                    cv[cs, ln] = dvc
                    db_ref[2 * t] += db0
                    db_ref[2 * t + 1] += db1
                return carry

            _for_residues(d, residue)

        @pl.when(n == nb)
        def _():
            def residue(r, carry):
                sl = _residue_rows(r, d)
                cs = pl.ds(pl.multiple_of(r * WIN, WIN), WIN)
                dk_ref[sl, :] = ck[cs, :]
                dv_ref[sl, :] = cv[cs, :]
                return carry

            _for_residues(d, residue)

    def cur(n):
        return jnp.minimum(n, nb - 1)

    def spec(off, prev):
        if prev:
            return pl.BlockSpec((bt, bw), lambda hp, n: (jnp.maximum(cur(n) - 1, 0), off // hpw + cb + hp))
        return pl.BlockSpec((bt, bw), lambda hp, n: (cur(n), off // hpw + cb + hp))

    gspec = pl.BlockSpec((bt, bw), lambda hp, n: (cur(n), hp))
    bspec = pl.BlockSpec((2 * hpw, WIN, 2 * WIN), lambda hp, n: (hp, 0, 0))
    qspec = pl.BlockSpec((bt, bw), lambda hp, n: (cur(n), cb + hp))
    kspec = pl.BlockSpec((bt, bw), lambda hp, n: (jnp.maximum(n - 1, 0), cb + hp))
    dq, dk, dv, db = pl.pallas_call(
        body, name=name, grid=(3 // hpw, nb + 1),
        in_specs=[spec(0, False), spec(9, True), spec(9, False), spec(18, True), spec(18, False),
                  bspec, gspec, gspec, bspec, _ANY, _ANY, _ANY],
        out_specs=[qspec, kspec, kspec, bspec],
        out_shape=[jax.ShapeDtypeStruct((S, AW), F32)] * 3 + [jax.ShapeDtypeStruct((6, WIN, 2 * WIN), F32)],
        scratch_shapes=[pltpu.VMEM((bt, bw), F32), pltpu.VMEM((bt, bw), F32)],
        input_output_aliases={9: 0, 10: 1, 11: 2},
        compiler_params=pltpu.CompilerParams(dimension_semantics=("arbitrary", "arbitrary")),
    )(pa, pa, pa, pa, pa, bias, do, dlse, db_in, *dqkv)
    return (dq, dk, dv), db


def _mix_core(o0, o1, o2, l0, l1, l2):
    m = lax.stop_gradient(jnp.maximum(jnp.maximum(l0, l1), l2))
    e0, e1, e2 = jnp.exp(l0 - m), jnp.exp(l1 - m), jnp.exp(l2 - m)
    return (e0 * o0 + e1 * o1 + e2 * o2) / (e0 + e1 + e2)


def _mix_fwd(os_, ls_, name):
    S = os_[0].shape[0]
    ins = [("row", a, None, 0) for a in (*os_, *ls_)]
    return _rows(name, lambda ctx, *v: [_mix_core(*v)], ins, [(GW, GW, 0, BF16)], tm=512, nrows=S)[0]


def _mix_bwd(os_, ls_, datt, name):
    S = datt.shape[0]

    def fn(ctx, *v):
        _, vjp = jax.vjp(_mix_core, *v[:6])
        return list(vjp(v[6]))

    ins = [("row", a, None, 0) for a in (*os_, *ls_, datt)]
    outs = [(GW, GW, 0, F32)] * 6
    r = _rows(name, fn, ins, outs, tm=512, nrows=S)
    return r[:3], r[3:]


def _t5_bucket(dist):
    max_exact = REL_BUCKETS // 2
    is_small = dist < max_exact
    nf = jnp.maximum(dist, 1).astype(F32)
    large = max_exact + (jnp.log(nf / max_exact) / math.log(REL_MAX_DISTANCE / max_exact)
                         * (REL_BUCKETS - max_exact)).astype(jnp.int32)
    large = jnp.minimum(large, REL_BUCKETS - 1)
    return jnp.where(is_small, dist, large)


def _buckets(d):
    qi = jnp.arange(WIN)[:, None]
    kk = jnp.arange(2 * WIN)[None, :]
    rel = qi + WIN - kk
    return _t5_bucket(jnp.clip(rel, 0, None) * d)


def _pool_cnt(ctx, w):
    pos = ctx.row0 + _iota((ctx.rows, PG), 0) + 1
    return jnp.minimum(pos, w).astype(F32)


def _pool_d(ctx, halo, u):
    ds = []
    for g, w in enumerate(POOL_WINDOWS):
        ug = u[:, g * PG:(g + 1) * PG]
        s = _with_prev(ctx, halo[:, g * PG:(g + 1) * PG], ug)
        step = 1
        while step < w:
            s = s + _shift_down(s, step)
            step *= 2
        ds.append(s[HALO:] / _pool_cnt(ctx, w) - ug)
    return ds


def _pool_fwd(pb, pw, scale, name):
    S = pb.shape[0]

    def fn(ctx, halo, u, w, sc):
        ds = _pool_d(ctx, halo, u)
        return [jnp.concatenate([_dg(ds[k], w[k], 1, 0) for k in range(4)], axis=1) * sc]

    return _rows(name, fn, [("prev", pb, D, 0), ("row", pb, None, 0), ("raw", pw, None, 0), ("const", scale, None, 0)],
                 [(D, D, 0, BF16)], tm=512, nrows=S)[0]


def _pool_bwd(pb, pw, scale, dpo, name):
    S = pb.shape[0]

    def fn1(ctx, halo, u, w, sc, dy):
        ds = _pool_d(ctx, halo, u)
        dyp = dy * sc
        y = jnp.concatenate([_dg(ds[k], w[k], 1, 0) for k in range(4)], axis=1)
        es, dws = [], []
        for k, wd in enumerate(POOL_WINDOWS):
            cols = slice(k * PG, (k + 1) * PG)
            es.append(_dg(dyp[:, cols], w[k], 1, 1) / _pool_cnt(ctx, wd))
            dws.append(_dg(ds[k], dyp[:, cols], 0, 0))
        return [jnp.concatenate(es, axis=1), jnp.concatenate(dws, axis=0), jnp.sum(dy * y, axis=0, keepdims=True)]

    e, dpw, dsc = _rows(name + "_a", fn1,
                        [("prev", pb, D, 0), ("row", pb, None, 0), ("raw", pw, None, 0), ("const", scale, None, 0),
                         ("row", dpo, None, 0)],
                        [(D, D, 0, F32)], [(4 * PG, PG, PG), (1, D, D)], tm=512, nrows=S)

    def fn2(ctx, ev, halo):
        outs = []
        for g, w in enumerate(POOL_WINDOWS):
            eg = ev[:, g * PG:(g + 1) * PG]
            s = _with_next(ctx, eg, halo[:, g * PG:(g + 1) * PG])
            step = 1
            while step < w:
                s = s + _shift_up(s, step)
                step *= 2
            outs.append(s[:ctx.rows] - eg * _pool_cnt(ctx, w))
        return [jnp.concatenate(outs, axis=1)]

    du = _rows(name + "_b", fn2, [("row", e, None, 0), ("next", e, D, 0)], [(D, D, 0, BF16)], tm=512, nrows=S)[0]
    return du, dpw, dsc


def _conv_taps(ctx, halo, x, K):
    cat = _with_prev(ctx, halo, x)
    return [_shift_down(cat, K - 1 - k)[HALO:] for k in range(K)]


def _conv_pre(taps, w, b):
    acc = b
    for k, t in enumerate(taps):
        acc = acc + t * _row_pick(w, k)
    return acc


CW = 128
CWS = 512
CONV_BWD_TILE = 256 * 1024
CONV_FWD_TILE = 1024 * 1024


def _ext_taps(ctx, prev, x, nxt, K):
    cat = jnp.concatenate([jnp.where(ctx.first, 0.0, prev), x, jnp.where(ctx.last, 0.0, nxt)], axis=0)
    return [_shift_down(cat, K - 1 - k)[HALO:] for k in range(K)]


def _conv_t_rows(dp, w, K, tm):
    acc = jnp.zeros((tm, dp.shape[1]), F32)
    for k in range(K):
        acc = acc + _shift_up(dp, K - 1 - k)[:tm] * _row_pick(w, k)
    return acc


def _ssd_conv_fwd(pc, w, b, name):
    S = pc.shape[0]
    base = D // CWS

    def fn(ctx, halo, x, wv, bv):
        return [_silu(_conv_pre(_conv_taps(ctx, halo, x, 4), wv, bv))]

    return _rows(name, fn, [("prev", pc, CWS, base), ("row", pc, CWS, base), ("ccol", w, CWS, 0), ("ccol", b, CWS, 0)],
                 [(XBC, CWS, 0, F32)], tm=min(S, CONV_FWD_TILE // CWS), nrows=S, ncol=XBC // CWS)[0]


def _ssd_conv_bwd(pc, w, b, dy, name):
    S = pc.shape[0]
    base = D // CWS

    def fn(ctx, prev, x, nxt, wv, bv, dyv, dyn):
        n = ctx.rows
        taps = _ext_taps(ctx, prev, x, nxt, 4)
        pre = _conv_pre(taps, wv, bv)
        sg = _sigmoid(pre)
        dye = jnp.concatenate([dyv, jnp.where(ctx.last, 0.0, dyn)], axis=0)
        dpre = dye * sg * (1.0 + pre * (1.0 - sg))
        dw = _stack_rows([jnp.sum(dpre[:n] * t[:n], axis=0, keepdims=True) for t in taps], 4)
        return [_conv_t_rows(dpre, wv, 4, n), dw, jnp.sum(dpre[:n], axis=0, keepdims=True)]

    return _rows(name, fn,
                 [("prev", pc, CWS, base), ("row", pc, CWS, base), ("next", pc, CWS, base), ("ccol", w, CWS, 0),
                  ("ccol", b, CWS, 0), ("row", dy, CWS, 0), ("next", dy, CWS, 0)],
                 [(XBC, CWS, 0, BF16)], [(4, XBC, CWS), (1, XBC, CWS)], tm=min(S, CONV_BWD_TILE // CWS), nrows=S,
                 ncol=XBC // CWS)


NFC = D_FF // CW


def _ffn_act_fwd(h, w, b, name):
    S = h.shape[0]

    def fn(ctx, ha, a, hv, v, wa, wv, ba, bv):
        pa = _conv_pre(_conv_taps(ctx, ha, a, 3), wa, ba)
        pv = _conv_pre(_conv_taps(ctx, hv, v, 3), wv, bv)
        return [_silu(pa) * pv]

    return _rows(name, fn,
                 [("prev", h, CW, 0), ("row", h, CW, 0), ("prev", h, CW, NFC), ("row", h, CW, NFC),
                  ("ccol", w, CW, 0), ("ccol", w, CW, NFC), ("ccol", b, CW, 0), ("ccol", b, CW, NFC)],
                 [(D_FF, CW, 0, BF16)], tm=min(S, CONV_FWD_TILE // CW), nrows=S, ncol=NFC)[0]


def _ffn_act_bwd(h, w, b, df, name):
    S = h.shape[0]

    def fn(ctx, pa_, a, na, pv_, v, nv, wa, wv, ba, bv, dfv, dfn):
        n = ctx.rows
        ta = _ext_taps(ctx, pa_, a, na, 3)
        tv = _ext_taps(ctx, pv_, v, nv, 3)
        pa = _conv_pre(ta, wa, ba)
        pv = _conv_pre(tv, wv, bv)
        sg = _sigmoid(pa)
        dfe = jnp.concatenate([dfv, jnp.where(ctx.last, 0.0, dfn)], axis=0)
        dpa = dfe * pv * sg * (1.0 + pa * (1.0 - sg))
        dpv = dfe * pa * sg
        res = [_conv_t_rows(dpa, wa, 3, n), _conv_t_rows(dpv, wv, 3, n)]
        for dp, taps in ((dpa, ta), (dpv, tv)):
            res.append(_stack_rows([jnp.sum(dp[:n] * t[:n], axis=0, keepdims=True) for t in taps], 3))
        for dp in (dpa, dpv):
            res.append(jnp.sum(dp[:n], axis=0, keepdims=True))
        return res

    ins = []
    for base in (0, NFC):
        ins += [("prev", h, CW, base), ("row", h, CW, base), ("next", h, CW, base)]
    ins += [("ccol", w, CW, 0), ("ccol", w, CW, NFC), ("ccol", b, CW, 0), ("ccol", b, CW, NFC),
            ("row", df, CW, 0), ("next", df, CW, 0)]
    dha, dhv, dwa, dwv, dba, dbv = _rows(
        name, fn, ins, [(D_FF, CW, 0, BF16)] * 2, [(3, D_FF, CW)] * 2 + [(1, D_FF, CW)] * 2,
        tm=min(S, CONV_BWD_TILE // CW), nrows=S, ncol=NFC)
    return dha, dhv, jnp.concatenate([dwa, dwv], axis=1), jnp.concatenate([dba, dbv], axis=1)


NSLAB = D // LANES
CPS = 2


def _ssd_chunk(xs, Bs, Cs, dtraw, dtb, alog, prev):
    lsz = SSD_CHUNK
    lane = _iota((lsz, LANES), 1)
    row = _iota((lsz, LANES), 0)
    dt = jnp.where(lane < SSD_HEADS, _softplus(dtraw + dtb), 0.0)
    a = dt * (-jnp.exp(alog))
    tril = row >= lane
    a_cs = _fdot(tril.astype(F32), a)
    a_cst = a_cs.T
    a_last = jnp.sum(a, axis=0, keepdims=True)
    lo = lane < HD
    top = row < HD
    cbs = [_bdot_nt(Cs[g], Bs[g]) for g in range(2)]
    ys, news = [], []
    for s in range(NSLAB):
        g = s // (NSLAB // 2)
        cols, lms, dts, als = [], [], [], []
        for hh in range(2):
            h = 2 * s + hh
            col = _lane_pick(a_cs, h)
            seg = col - _row_pick(a_cst, h)
            lms.append(jnp.exp(jnp.where(tril, seg, NEG)))
            cols.append(col)
            dts.append(_lane_pick(dt, h))
            als.append(_lane_pick(a_last, h))
        col_x = jnp.where(lo, cols[0], cols[1])
        al_x = jnp.where(lo, als[0], als[1])
        xc = xs[s] * jnp.where(lo, dts[0], dts[1])
        yd = jnp.where(lo, _bdot_nn(cbs[g] * lms[0], xc), _bdot_nn(cbs[g] * lms[1], xc))
        yoff = _bdot_nt(Cs[g], prev[s]) * jnp.exp(col_x)
        ys.append(yd + yoff)
        st = _bdot_tn(xc * jnp.exp(al_x - col_x), Bs[g])
        news.append(prev[s] * jnp.exp(jnp.where(top, als[0], als[1])) + st)
    return ys, news


def _ssd_scan_fwd(xbc_c, pd, dtb, alog, name):
    S = xbc_c.shape[0]
    nc = S // SSD_CHUNK
    rows_ = CPS * SSD_CHUNK

    def body(x_ref, b_ref, c_ref, dt_ref, dtb_ref, al_ref, y_ref, st_ref, state):
        c = pl.program_id(0)

        @pl.when(c == 0)
        def _():
            state[...] = jnp.zeros_like(state)

        prev = [state[s * LANES:(s + 1) * LANES, :] for s in range(NSLAB)]
        for u in range(CPS):
            rw = pl.ds(u * SSD_CHUNK, SSD_CHUNK)
            xs = [x_ref[rw, s * LANES:(s + 1) * LANES] for s in range(NSLAB)]
            Bs = [b_ref[rw, g * SSD_N:(g + 1) * SSD_N] for g in range(2)]
            Cs = [c_ref[rw, g * SSD_N:(g + 1) * SSD_N] for g in range(2)]
            for s in range(NSLAB):
                st_ref[u, s * LANES:(s + 1) * LANES, :] = prev[s]
            ys, prev = _ssd_chunk(xs, Bs, Cs, dt_ref[rw, :].astype(F32), dtb_ref[...], al_ref[...], prev)
            for s in range(NSLAB):
                y_ref[rw, s * LANES:(s + 1) * LANES] = ys[s]
        for s in range(NSLAB):
            state[s * LANES:(s + 1) * LANES, :] = prev[s]

    return pl.pallas_call(
        body, name=name, grid=(nc // CPS,),
        in_specs=[pl.BlockSpec((rows_, D), lambda c: (c, 0)),
                  pl.BlockSpec((rows_, 2 * SSD_N), lambda c: (c, D // (2 * SSD_N))),
                  pl.BlockSpec((rows_, 2 * SSD_N), lambda c: (c, D // (2 * SSD_N) + 1)),
                  pl.BlockSpec((rows_, LANES), lambda c: (c, 0)),
                  pl.BlockSpec((1, LANES), lambda c: (0, 0)), pl.BlockSpec((1, LANES), lambda c: (0, 0))],
        out_specs=[pl.BlockSpec((rows_, D), lambda c: (c, 0)), pl.BlockSpec((CPS, D, SSD_N), lambda c: (c, 0, 0))],
        out_shape=[jax.ShapeDtypeStruct((S, D), F32), jax.ShapeDtypeStruct((nc, D, SSD_N), F32)],
        scratch_shapes=[pltpu.VMEM((D, SSD_N), F32)],
        compiler_params=pltpu.CompilerParams(dimension_semantics=("arbitrary",)),
    )(xbc_c, xbc_c, xbc_c, pd, dtb, alog)


def _ssd_scan_bwd(xbc_c, pd, dtb, alog, states, dy, dxs_skip, name):
    S = xbc_c.shape[0]
    nc = S // SSD_CHUNK
    rows_ = CPS * SSD_CHUNK

    def body(x_ref, b_ref, c_ref, dt_ref, dtb_ref, al_ref, st_ref, dy_ref, sk_ref,
             dx_ref, ddt_ref, ddtb_ref, dal_ref, dstate):
        c = pl.program_id(0)

        @pl.when(c == 0)
        def _():
            dstate[...] = jnp.zeros_like(dstate)
            ddtb_ref[...] = jnp.zeros_like(ddtb_ref)
            dal_ref[...] = jnp.zeros_like(dal_ref)

        dnew = [dstate[s * LANES:(s + 1) * LANES, :] for s in range(NSLAB)]
        for u in reversed(range(CPS)):
            rw = pl.ds(u * SSD_CHUNK, SSD_CHUNK)
            xs = [x_ref[rw, s * LANES:(s + 1) * LANES] for s in range(NSLAB)]
            Bs = [b_ref[rw, g * SSD_N:(g + 1) * SSD_N] for g in range(2)]
            Cs = [c_ref[rw, g * SSD_N:(g + 1) * SSD_N] for g in range(2)]
            prev = [st_ref[u, s * LANES:(s + 1) * LANES, :] for s in range(NSLAB)]
            _, vjp = jax.vjp(_ssd_chunk, xs, Bs, Cs, dt_ref[rw, :].astype(F32), dtb_ref[...], al_ref[...], prev)
            dys = [dy_ref[rw, s * LANES:(s + 1) * LANES] for s in range(NSLAB)]
            dxs, dBs, dCs, ddt, ddtb, dal, dnew = vjp((dys, dnew))
            for s in range(NSLAB):
                dx_ref[rw, s * LANES:(s + 1) * LANES] = dxs[s] + sk_ref[rw, s * LANES:(s + 1) * LANES]
            for g in range(2):
                dx_ref[rw, D + g * SSD_N:D + (g + 1) * SSD_N] = dBs[g]
                dx_ref[rw, D + 2 * SSD_N + g * SSD_N:D + 2 * SSD_N + (g + 1) * SSD_N] = dCs[g]
            ddt_ref[rw, :] = ddt
            ddtb_ref[...] += ddtb
            dal_ref[...] += dal
        for s in range(NSLAB):
            dstate[s * LANES:(s + 1) * LANES, :] = dnew[s]

    def rv(c):
        return nc // CPS - 1 - c

    return pl.pallas_call(
        body, name=name, grid=(nc // CPS,),
        in_specs=[pl.BlockSpec((rows_, D), lambda c: (rv(c), 0)),
                  pl.BlockSpec((rows_, 2 * SSD_N), lambda c: (rv(c), D // (2 * SSD_N))),
                  pl.BlockSpec((rows_, 2 * SSD_N), lambda c: (rv(c), D // (2 * SSD_N) + 1)),
                  pl.BlockSpec((rows_, LANES), lambda c: (rv(c), 0)),
                  pl.BlockSpec((1, LANES), lambda c: (0, 0)), pl.BlockSpec((1, LANES), lambda c: (0, 0)),
                  pl.BlockSpec((CPS, D, SSD_N), lambda c: (rv(c), 0, 0)),
                  pl.BlockSpec((rows_, D), lambda c: (rv(c), 0)),
                  pl.BlockSpec((rows_, D), lambda c: (rv(c), 0))],
        out_specs=[pl.BlockSpec((rows_, XBC), lambda c: (rv(c), 0)),
                   pl.BlockSpec((rows_, LANES), lambda c: (rv(c), 0)),
                   pl.BlockSpec((1, LANES), lambda c: (0, 0)), pl.BlockSpec((1, LANES), lambda c: (0, 0))],
        out_shape=[jax.ShapeDtypeStruct((S, XBC), F32), jax.ShapeDtypeStruct((S, LANES), F32),
                   jax.ShapeDtypeStruct((1, LANES), F32), jax.ShapeDtypeStruct((1, LANES), F32)],
        scratch_shapes=[pltpu.VMEM((D, SSD_N), F32)],
        compiler_params=pltpu.CompilerParams(dimension_semantics=("arbitrary",)),
    )(xbc_c, xbc_c, xbc_c, pd, dtb, alog, states, dy, dxs_skip)


def _ssd_post_core(y, xs, z, d128, nw):
    tm = y.shape[0]
    ex = (_iota((LANES, D), 1) // HD == _iota((LANES, D), 0)).astype(F32)
    d_x = jnp.sum(_fdot(jnp.broadcast_to(d128, (8, LANES)), ex), axis=0, keepdims=True) * 0.125
    y2 = (y + d_x * xs) * _silu(z)
    lo = _iota((tm, D), 1) < D // 2
    sq = y2 * y2
    ms0 = jnp.sum(jnp.where(lo, sq, 0.0), axis=-1, keepdims=True) / (D // 2)
    ms1 = jnp.sum(jnp.where(lo, 0.0, sq), axis=-1, keepdims=True) / (D // 2)
    r = jnp.where(lo, lax.rsqrt(ms0 + EPS), lax.rsqrt(ms1 + EPS))
    return y2 * r * nw


def _ssd_post_ins(y, xbc_c, pc, d128, nw):
    return [("row", y, None, 0), ("row", xbc_c, D, 0), ("row", pc, D, 0), ("const", d128, None, 0), ("const", nw, None, 0)]


def _ssd_post_fwd(y, xbc_c, pc, d128, nw, name):
    S = y.shape[0]
    return _rows(name, lambda ctx, *v: [_ssd_post_core(*v)], _ssd_post_ins(y, xbc_c, pc, d128, nw),
                 [(D, D, 0, BF16)], tm=512, nrows=S)[0]


def _ssd_post_bwd(y, xbc_c, pc, d128, nw, dout, name):
    S = y.shape[0]

    def fn(ctx, *v):
        _, vjp = jax.vjp(_ssd_post_core, *v[:5])
        return list(vjp(v[5]))

    return _rows(name, fn, _ssd_post_ins(y, xbc_c, pc, d128, nw) + [("row", dout, None, 0)],
                 [(D, D, 0, F32), (D, D, 0, F32), (D, D, 0, BF16)], [(1, LANES, LANES), (1, D, D)], tm=512, nrows=S)


def _gates_core(g0, g1, g2, b0, b1, b2, ya, yb, yc):
    return _sigmoid(g0 + b0) * ya + _sigmoid(g1 + b1) * yb + _sigmoid(g2 + b2) * yc


def _gate_parts(pdv, bv):
    gp = pltpu.roll(pdv, SEC_D - 16, 1)
    return [gp[:, k * D:(k + 1) * D] for k in range(3)] + [bv[:, k * D:(k + 1) * D] for k in range(3)]


def _gates_fwd(pd, bg, ya, yb, yc, name):
    S = pd.shape[0]

    def fn(ctx, pdv, bv, a, b, c):
        return [_gates_core(*_gate_parts(pdv, bv), a, b, c)]

    return _rows(name, fn, [("row", pd, None, 0), ("const", bg, None, 0), ("row", ya, None, 0), ("row", yb, None, 0),
                            ("row", yc, None, 0)], [(D, D, 0, BF16)], tm=512, nrows=S)[0]


def _gates_post(dm, pdv, a, b, c, bv):
    _, vjp = jax.vjp(_gates_core, *_gate_parts(pdv, bv), a, b, c)
    g = vjp(dm)
    return [g[6], g[7], g[8], jnp.concatenate(g[0:3], axis=1), jnp.concatenate(g[3:6], axis=1)]


def _adam_update(wv, gv, mv, vv):
    m2 = ADAM_B1 * mv + (1.0 - ADAM_B1) * gv
    v2 = ADAM_B2 * vv + (1.0 - ADAM_B2) * jnp.square(gv)
    m_hat = m2 / (1.0 - ADAM_B1 ** ADAM_STEP)
    v_hat = v2 / (1.0 - ADAM_B2 ** ADAM_STEP)
    delta = -ADAM_LR * (m_hat / (jnp.sqrt(v_hat) + ADAM_EPS) + ADAM_WD * wv)
    return [delta, m2, v2]


def _adamw(w, g, m, v, name):
    rows, C = w.shape
    tm = _pick(rows, [t for t in (512, 256, 128, 64, 32, 16, 8) if t * C <= ADAM_TILE])
    return _rows(name, lambda ctx, *a: _adam_update(*a), [("row", a, None, 0) for a in (w, g, m, v)],
                 [(C, C, 0, F32)] * 3, tm=tm, nrows=rows)


def _position():
    return lax.axis_index("x"), lax.axis_index("y"), lax.axis_index("c")


def _other_chips(x, y):
    return [(1 - x, y), (x, 1 - y), (1 - x, 1 - y)]


_HBM = pl.BlockSpec(memory_space=pltpu.HBM)


def _gather_parts(half, lo, n):
    def copies(p_ref, out_ref, send_sems, recv_sems):
        x, y, c = _position()
        sibling = (x, y, 1 - c)
        chips = _other_chips(x, y)

        def slab(chip, h):
            return out_ref.at[2 * chip[0] + chip[1], pl.ds(h * half + lo, n), :]

        def copy(k, src, dst, to):
            return pltpu.make_async_remote_copy(src_ref=src, dst_ref=dst, send_sem=send_sems.at[k],
                                                recv_sem=recv_sems.at[k], device_id=to, device_id_type=MESH)

        first = [copy(j, p_ref.at[pl.ds(c * half + lo, n), :], slab((x, y), c), (*chip, c)) for j, chip in enumerate(chips)]
        passed = [copy(3 + j, slab(chip, c), slab(chip, c), sibling) for j, chip in enumerate(chips)]
        from_chips = [copy(j, slab(chip, c), slab(chip, c), (x, y, c)) for j, chip in enumerate(chips)]
        from_sibling = [copy(3 + j, slab(chip, 1 - c), slab(chip, 1 - c), (x, y, c)) for j, chip in enumerate(chips)]
        return first, passed, from_chips, from_sibling

    def start(ins, outs, scr):
        for cp in copies(ins[0], outs[0], *scr)[0]:
            cp.start()

    def finish(ins, outs, scr):
        first, passed, from_chips, from_sibling = copies(ins[0], outs[0], *scr)
        for j in range(3):
            from_chips[j].wait_recv()
            passed[j].start()
        for cp in from_sibling:
            cp.wait_recv()
        for cp in first + passed:
            cp.wait_send()

    return start, finish


def _rs_chip_parts(lo, n):
    def copies(h_ref, out_ref, send_sems, recv_sems):
        x, y, c = _position()
        return [pltpu.make_async_remote_copy(src_ref=h_ref.at[2 * chip[0] + chip[1], pl.ds(lo, n), :],
                                             dst_ref=out_ref.at[j, pl.ds(lo, n), :],
                                             send_sem=send_sems.at[j], recv_sem=recv_sems.at[j],
                                             device_id=(*chip, c), device_id_type=MESH)
                for j, chip in enumerate(_other_chips(x, y))]

    def start(ins, outs, scr):
        for cp in copies(ins[0], outs[0], *scr):
            cp.start()

    def finish(ins, outs, scr):
        for cp in copies(ins[0], outs[0], *scr):
            cp.wait()

    return start, finish


class _Stream:
    def __init__(self, src, buf, parts, nsem, units, name):
        self.src, self.buf, self.parts, self.nsem, self.name = src, buf, parts, nsem, name
        self.next, self.units = 0, units

    def _scratch(self):
        return [pltpu.SemaphoreType.DMA((self.nsem,)), pltpu.SemaphoreType.DMA((self.nsem,))]

    def _take(self, units):
        units = min(units, self.units - self.next)
        lo = self.next * 16
        self.next += units
        return lo, units * 16

    def _set(self, outs):
        self.buf = outs[0]

    def hook(self, units):
        lo, n = self._take(units)
        if n == 0:
            return None
        start, finish = self.parts(lo, n)
        return _Hook([self.src, self.buf], [jax.ShapeDtypeStruct(self.buf.shape, self.buf.dtype)], {1: 0},
                     self._scratch(), start, finish, self._set)

    def drain(self):
        lo, n = self._take(self.units)
        if n:
            start, finish = self.parts(lo, n)

            def body(s_ref, b_ref, o_ref, send_sems, recv_sems):
                args = ((s_ref, b_ref), (o_ref,), (send_sems, recv_sems))
                start(*args)
                finish(*args)

            self.buf = pl.pallas_call(
                body, name=self.name, in_specs=[_ANY, _ANY], out_specs=_ANY,
                out_shape=jax.ShapeDtypeStruct(self.buf.shape, self.buf.dtype),
                scratch_shapes=self._scratch(), input_output_aliases={1: 0},
            )(self.src, self.buf)
        return self.buf


def _rs_pair_parts(half, lo, n):
    def copy(g_ref, out_ref, send_sems, recv_sems):
        x, y, c = _position()
        return pltpu.make_async_remote_copy(
            src_ref=g_ref.at[pl.ds(0, 4), pl.ds((1 - c) * half + lo, n), :], dst_ref=out_ref.at[pl.ds(0, 4), pl.ds(lo, n), :],
            send_sem=send_sems.at[0], recv_sem=recv_sems.at[0], device_id=(x, y, 1 - c), device_id_type=MESH)

    def start(ins, outs, scr):
        copy(ins[0], outs[0], *scr).start()

    def finish(ins, outs, scr):
        copy(ins[0], outs[0], *scr).wait()

    return start, finish


def _rs_swap(r, name):
    Rh, C = r.shape

    def body(r_ref, out_ref, send_sem, recv_sem):
        x, y, c = _position()
        cp = pltpu.make_async_remote_copy(src_ref=r_ref, dst_ref=out_ref, send_sem=send_sem,
                                          recv_sem=recv_sem, device_id=(x, y, 1 - c), device_id_type=MESH)
        cp.start()
        cp.wait()

    return pl.pallas_call(
        body, name=name, in_specs=[_HBM], out_specs=_HBM,
        out_shape=jax.ShapeDtypeStruct((Rh, C), r.dtype),
        scratch_shapes=[pltpu.SemaphoreType.DMA, pltpu.SemaphoreType.DMA],
    )(r)


def _rs_add_pair(g, recv, cidx, name):
    _, R, C = g.shape
    Rh = R // 2
    tm = _pick(Rh, (400, 280, 200, 160, 80, 40, 16, 8))
    nt = Rh // tm

    def body(c_ref, g_ref, r_ref, o_ref):
        o_ref[...] = (g_ref[...].astype(F32) + r_ref[...].astype(F32)).astype(o_ref.dtype)

    return pl.pallas_call(
        body, name=name,
        grid_spec=pltpu.PrefetchScalarGridSpec(
            num_scalar_prefetch=1, grid=(4, nt),
            in_specs=[pl.BlockSpec((1, tm, C), lambda k, i, cr: (k, cr[0] * nt + i, 0)),
                      pl.BlockSpec((1, tm, C), lambda k, i, cr: (k, i, 0))],
            out_specs=pl.BlockSpec((1, tm, C), lambda k, i, cr: (k, i, 0))),
        out_shape=jax.ShapeDtypeStruct((4, Rh, C), BF16),
    )(cidx, g, recv)


def _rs_add_chips(h, recv, chip_idx, name):
    _, Rh, C = h.shape
    tm = _pick(Rh, (400, 280, 200, 160, 80, 40, 16, 8))

    def body(c_ref, h_ref, r_ref, o_ref):
        acc = h_ref[0].astype(F32)
        for j in range(3):
            acc = acc + r_ref[j].astype(F32)
        o_ref[...] = acc

    return pl.pallas_call(
        body, name=name,
        grid_spec=pltpu.PrefetchScalarGridSpec(
            num_scalar_prefetch=1, grid=(Rh // tm,),
            in_specs=[pl.BlockSpec((1, tm, C), lambda i, cr: (cr[0], i, 0)), pl.BlockSpec((3, tm, C), lambda i, cr: (0, i, 0))],
            out_specs=pl.BlockSpec((tm, C), lambda i, cr: (i, 0))),
        out_shape=jax.ShapeDtypeStruct((Rh, C), F32),
    )(chip_idx, h, recv)


def _all_reduce_small(vec, name):
    n, C = vec.shape

    def body(v_ref, out_ref, buf, send_sems, recv_sems):
        x, y, c = _position()

        def flip(k):
            return ((1 - x) if k & 4 else x, (1 - y) if k & 2 else y, (1 - c) if k & 1 else c)

        def idx(p):
            return 4 * p[0] + 2 * p[1] + p[2]

        me = idx((x, y, c))
        buf[me] = v_ref[...]
        cps = [pltpu.make_async_remote_copy(src_ref=v_ref, dst_ref=buf.at[me], send_sem=send_sems.at[k - 1],
                                            recv_sem=recv_sems.at[k - 1], device_id=flip(k), device_id_type=MESH)
               for k in range(1, 8)]
        for cp in cps:
            cp.start()
        for k in range(1, 8):
            pltpu.make_async_remote_copy(src_ref=v_ref, dst_ref=buf.at[idx(flip(k))], send_sem=send_sems.at[k - 1],
                                         recv_sem=recv_sems.at[k - 1], device_id=flip(k), device_id_type=MESH).wait_recv()
        for cp in cps:
            cp.wait_send()
        acc = buf[0]
        for s in range(1, 8):
            acc = acc + buf[s]
        out_ref[...] = acc

    return pl.pallas_call(
        body, name=name,
        in_specs=[pl.BlockSpec(memory_space=pltpu.VMEM)], out_specs=pl.BlockSpec(memory_space=pltpu.VMEM),
        out_shape=jax.ShapeDtypeStruct((n, C), F32),
        scratch_shapes=[pltpu.VMEM((8, n, C), F32), pltpu.SemaphoreType.DMA((7,)), pltpu.SemaphoreType.DMA((7,))],
    )(vec)


BIG = (("w_in", (D, IN_WIDTH // 4), "cols"), ("w_a", (GW, D // 4), "cols"), ("pool_w", (4, PG // 4, PG), "pool"),
       ("w_b", (D // 4, D), "rows"), ("w_c", (D // 4, D), "rows"), ("w_o", (D // 4, D), "rows"),
       ("ffn_w_up", (D, 2 * D_FF // 4), "cols"), ("ffn_w_down", (D_FF // 4, D), "rows"))
def _pack_rows(s):
    k = math.prod(s) // D
    return -(-k // 16) * 16, k


PACK_ROWS = sum(_pack_rows(s)[0] for _, s, _ in BIG)
PACK_PAD = -(-PACK_ROWS // 32) * 32


def _pad_rows(v, rows):
    pad = [(0, 0)] * v.ndim
    pad[-2] = (0, rows - v.shape[-2])
    return jnp.pad(v, pad) if rows > v.shape[-2] else v


def _pack_blocks(blocks, dtype):
    lead = blocks["w_in"].shape[:-2]
    flat = []
    for n, s, how in BIG:
        v = blocks[n].astype(dtype)
        if how == "cols":
            v = jnp.swapaxes(v, -1, -2)
        flat.append(_pad_rows(v.reshape(*lead, -1, D), _pack_rows(s)[0]))
    flat.append(jnp.zeros((*lead, PACK_PAD - PACK_ROWS, D), dtype))
    return jnp.concatenate(flat, axis=-2)


def _unpack_blocks(pack):
    out, r = {}, 0
    for n, s, how in BIG:
        rows, k = _pack_rows(s)
        v = pack[r:r + k, :]
        out[n] = v.reshape(s[1], s[0]).T if how == "cols" else v.reshape(s)
        r += rows
    return out


def _operands(allp):
    out, r = {}, 0
    for n, s, how in BIG:
        rows, k = _pack_rows(s)
        v = allp[:, r:r + k, :]
        if how == "cols":
            out[n] = v.reshape(4 * s[1], s[0])
        elif how == "rows":
            out[n] = v.reshape(4 * s[0], s[1])
        else:
            out[n] = v.reshape(4, *s).transpose(1, 0, 2, 3).reshape(4, PG, PG)
        r += rows
    return out


def _pack_operands(g, dtype):
    flat = []
    for n, s, how in BIG:
        v = g[n].astype(dtype)
        if how == "pool":
            v = v.reshape(4, 4, s[1], s[2]).transpose(1, 0, 2, 3)
        flat.append(_pad_rows(v.reshape(4, -1, D), _pack_rows(s)[0]))
    flat.append(jnp.zeros((4, PACK_PAD - PACK_ROWS, D), dtype))
    return jnp.concatenate(flat, axis=1)


def _layer_fwd(x, w, sm, bias, hk):
    pa, u = _mmf(None, w["in_a"], tb=True, pre=(_rms_core, [x], [sm["ln1_g"]]), name="in_a", tm=1024, hook=hk("in_a"))
    pb = _mm(u, w["in_b"], tb=True, out_dtype=BF16, name="in_b", hook=hk("in_b"))
    pc = _mm(u, w["in_c"], tb=True, out_dtype=BF16, name="in_c", hook=hk("in_c"))
    pd = _mm(u, w["in_d"], tb=True, out_dtype=BF16, name="in_d", hook=hk("in_d"))
    os_, ls_ = [], []
    for gi in range(3):
        o, l = _attn_fwd(pa, bias[gi], gi, "attn_fwd%d" % gi)
        os_.append(o)
        ls_.append(l)
    att = _mix_fwd(os_, ls_, "mix_fwd")
    ya = _mm(att, w["w_a"], tb=True, out_dtype=BF16, name="mm_wa")
    pool_o = _pool_fwd(pb, w["pool_w"], sm["pool_scale"], "pool_fwd")
    yb = _mm(pool_o, w["w_b"], out_dtype=BF16, name="mm_wb")
    xbc_c = _ssd_conv_fwd(pc, sm["ssd_conv_w"], sm["ssd_conv_b"], "ssd_conv_fwd")
    y_scan, states = _ssd_scan_fwd(xbc_c, pd, sm["ssd_dt_bias"], sm["ssd_a_log"], "ssd_scan_fwd")
    ssd_o = _ssd_post_fwd(y_scan, xbc_c, pc, sm["ssd_d"], sm["ssd_norm_w"], "ssd_post_fwd")
    yc = _mm(ssd_o, w["w_c"], out_dtype=BF16, name="mm_wc")
    merged = _gates_fwd(pd, sm["b_gate"], ya, yb, yc, "gates_fwd")
    x1 = _mm(merged, w["w_o"], add=x, name="mm_wo", hook=hk("mm_wo"))
    h, u2 = _mmf(None, w["ffn_w_up"], tb=True, pre=(_rms_core, [x1], [sm["ln2_g"]]), out_dtype=BF16, name="mm_up",
                 tm=1024, hook=hk("mm_up"))
    f = _ffn_act_fwd(h, sm["ffn_conv_w"], sm["ffn_conv_b"], "ffn_act_fwd")
    x2 = _mm(f, w["ffn_w_down"], add=x1, name="mm_down", hook=hk("mm_down"))
    saved = dict(x=x, u=u, pa=pa, pb=pb, pc=pc, pd=pd, os=os_, ls=ls_, att=att, ya=ya, yb=yb, yc=yc, pool_o=pool_o,
                 xbc_c=xbc_c, y_scan=y_scan, states=states, ssd_o=ssd_o, merged=merged, x1=x1, u2=u2, h=h, f=f)
    return x2, saved


def _layer_bwd(dx2, dx2b, w, sm, bias, dbs, sv, hk):
    gw, gs = {}, {}
    S = dx2.shape[0]

    def gmm(a, b, name):
        return _mm(a, b, ta=True, out_dtype=BF16, name=name, hook=hk(name))

    df = _mm(dx2b, w["ffn_w_down"], tb=True, out_dtype=BF16, name="d_f", hook=hk("d_f"))
    gw["ffn_w_down"] = gmm(sv["f"], dx2b, "g_down")
    dha, dhv, gs["ffn_conv_w"], gs["ffn_conv_b"] = _ffn_act_bwd(sv["h"], sm["ffn_conv_w"], sm["ffn_conv_b"], df, "ffn_act_bwd")
    dx1, dx1b, gs["ln2_g"] = _mmf([dha, dhv], [w["up_a"], w["up_v"]], name="d_u2_v", tm=256, hook=hk("d_u2_v"),
                                  post=(_rms_post, [sv["x1"], dx2], [sm["ln2_g"]], RMS_POST_OUTS, [(1, D)]))
    gw["ffn_w_up"] = jnp.concatenate([gmm(dha, sv["u2"], "g_up_a"), gmm(dhv, sv["u2"], "g_up_v")], axis=0)
    dya, dyb, dyc, dgate, gs["b_gate"] = _mmf(
        dx1b, w["w_o"], tb=True, name="d_merged", tm=256, hook=hk("d_merged"),
        post=(_gates_post, [sv["pd"], sv["ya"], sv["yb"], sv["yc"]], [sm["b_gate"]],
              [(D, BF16)] * 3 + [(3 * D, BF16)], [(1, 3 * D)]))
    gw["w_o"] = gmm(sv["merged"], dx1b, "g_wo")
    dssd_o = _mm(dyc, w["w_c"], tb=True, name="d_ssd_o")
    gw["w_c"] = gmm(sv["ssd_o"], dyc, "g_wc")
    dy_scan, dxs_skip, dz, gs["ssd_d"], gs["ssd_norm_w"] = _ssd_post_bwd(
        sv["y_scan"], sv["xbc_c"], sv["pc"], sm["ssd_d"], sm["ssd_norm_w"], dssd_o, "ssd_post_bwd")
    dxbc_c, ddt, gs["ssd_dt_bias"], gs["ssd_a_log"] = _ssd_scan_bwd(
        sv["xbc_c"], sv["pd"], sm["ssd_dt_bias"], sm["ssd_a_log"], sv["states"], dy_scan, dxs_skip, "ssd_scan_bwd")
    dxbc, gs["ssd_conv_w"], gs["ssd_conv_b"] = _ssd_conv_bwd(sv["pc"], sm["ssd_conv_w"], sm["ssd_conv_b"], dxbc_c, "ssd_conv_bwd")
    dpool_o = _mm(dyb, w["w_b"], tb=True, name="d_pool_o")
    gw["w_b"] = gmm(sv["pool_o"], dyb, "g_wb")
    dpb, dpw, gs["pool_scale"] = _pool_bwd(sv["pb"], w["pool_w"], sm["pool_scale"], dpool_o, "pool_bwd")
    gw["pool_w"] = dpw.reshape(4, PG, PG)
    datt = _mm(dya, w["w_a"], name="d_att")
    gw["w_a"] = gmm(dya, sv["att"], "g_wa")
    dos, dls = _mix_bwd(sv["os"], sv["ls"], datt, "mix_bwd")
    dqkv = tuple(lax.empty((S, AW), F32) for _ in range(3))
    dbs = list(dbs)
    for gi in range(3):
        dqkv, dbs[gi] = _attn_bwd(sv["pa"], bias[gi], dos[gi], dls[gi], dbs[gi], dqkv, gi, "attn_bwd%d" % gi)
    u = sv["u"]
    pieces = [(dqkv[0], "wq"), (dqkv[1], "wk"), (dqkv[2], "wv"), (dpb, "in_b"), (dz, "wz"), (dxbc, "wxbc"),
              (ddt, "wdt"), (dgate, "wgate")]
    du = _mmf([dp for dp, _ in pieces[:4]], [w[key] for _, key in pieces[:4]], name="d_u_a", tm=256, hook=hk("d_u_a"))[0]
    dx, dxb, gs["ln1_g"] = _mmf([dp for dp, _ in pieces[4:]], [w[key] for _, key in pieces[4:]], add=du,
                                name="d_u_wgate", tm=256, hook=hk("d_u_wgate"),
                                post=(_rms_post, [sv["x"], dx1], [sm["ln1_g"]], RMS_POST_OUTS, [(1, D)]))
    g_in = []
    for dp, key in pieces:
        g = gmm(dp, u, "g_in_" + key)
        g_in.append(g[:SSD_HEADS] if key == "wdt" else g)
    gw["w_in"] = jnp.concatenate(g_in, axis=0)
    return dx, dxb, gw, gs, dbs


SMALL_LAYER = ("ln1_g", "b_gate", "pool_scale", "ssd_conv_w", "ssd_conv_b", "ssd_dt_bias", "ssd_a_log", "ssd_d",
               "ssd_norm_w", "ln2_g", "ffn_conv_w", "ffn_conv_b")


def _pad_lanes(v):
    return jnp.pad(v, (0, LANES - v.shape[0])).reshape(1, LANES)


def _layer_weights(ops):
    wt = ops["w_in"]
    o1, o2, o3 = SEC_A, SEC_A + SEC_B, SEC_A + SEC_B + SEC_C
    w = dict(ops)
    w["in_a"] = jnp.pad(wt[:o1], ((0, SEC_A_PAD - o1), (0, 0)))
    w["in_b"] = wt[o1:o2]
    w["in_c"] = wt[o2:o3]
    w["in_d"] = jnp.pad(wt[o3:], ((0, SEC_D - (IN_WIDTH - o3)), (0, 0)))
    w["wq"], w["wk"], w["wv"] = wt[:AW], wt[AW:2 * AW], wt[2 * AW:o1]
    w["wz"], w["wxbc"] = wt[o2:o2 + D], wt[o2 + D:o3]
    w["wdt"] = jnp.pad(wt[o3:o3 + SSD_HEADS], ((0, LANES - SSD_HEADS), (0, 0)))
    w["wgate"] = wt[o3 + SSD_HEADS:]
    w["up_a"], w["up_v"] = ops["ffn_w_up"][:D_FF], ops["ffn_w_up"][D_FF:]
    return w


def _layer_small(p, i):
    sm = {n: p[n][i] for n in SMALL_LAYER}
    out = {}
    for n, v in sm.items():
        if n in ("ssd_dt_bias", "ssd_a_log", "ssd_d"):
            out[n] = _pad_lanes(v)
        elif v.ndim == 1:
            out[n] = v.reshape(1, -1)
        else:
            out[n] = v
    return out


def _local_step(x, target, rel_bias, final_g, layer_full, small, fwd_hooks=None, bwd_hooks=None, after_bwd=None):
    nl = small["ln1_g"].shape[0]
    buckets = [_buckets(d).astype(jnp.int32) for d in DILATIONS]
    bias = [_bias_table(rel_bias, buckets[gi], gi, "bias_table%d" % gi) for gi in range(3)]
    no_hooks = lambda i: (lambda name: None)
    fwd_hooks = fwd_hooks or no_hooks
    bwd_hooks = bwd_hooks or no_hooks
    saved, ws, sms = [], [], []
    h = x
    for i in range(nl):
        w = _layer_weights(layer_full(i))
        sm = _layer_small(small, i)
        h, sv = _layer_fwd(h, w, sm, bias, fwd_hooks(i))
        saved.append(sv)
        ws.append(w)
        sms.append(sm)
    dh, dhb, dfinal, loss = _final_loss(h, target, final_g.reshape(1, D))
    gws, gss = [None] * nl, [None] * nl
    dbs = [jnp.zeros((6, WIN, 2 * WIN), F32)] * 3
    for i in reversed(range(nl)):
        dh, dhb, gws[i], gss[i], dbs = _layer_bwd(dh, dhb, ws[i], sms[i], bias, dbs, saved[i], bwd_hooks(i))
        if after_bwd is not None:
            after_bwd(i, gws[i])
    drel = []
    for gi in range(3):
        onehot = jnp.pad(jax.nn.one_hot(buckets[gi].reshape(-1), REL_BUCKETS, dtype=BF16), ((0, 0), (0, LANES - REL_BUCKETS)))
        drel.append(_mm(dbs[gi].reshape(6, WIN * 2 * WIN), onehot, name="g_relb"))
    return loss, dh, gws, gss, dfinal, jnp.concatenate(drel, axis=0)


WEIGHTS = ("rel_bias", "ln1_g", "w_in", "b_gate", "w_a", "pool_w", "pool_scale", "w_b", "ssd_conv_w", "ssd_conv_b",
           "ssd_dt_bias", "ssd_a_log", "ssd_d", "ssd_norm_w", "w_c", "w_o", "ln2_g", "ffn_w_up", "ffn_conv_w",
           "ffn_conv_b", "ffn_w_down", "final_g")
BIG_NAMES = tuple(n for n, _, _ in BIG)
SHARDED_SMALL = {"ssd_conv_w": XBC // 4, "ffn_conv_w": 2 * D_FF // 4}


def _to_rows(flat):
    n = flat.shape[0]
    rows = -(-n // LANES)
    rows = -(-rows // 8) * 8
    return jnp.pad(flat, (0, rows * LANES - n)).reshape(rows, LANES)


def _flatten(tree, names):
    return jnp.concatenate([tree[n].reshape(-1) for n in names])


def _unflatten(flat, shapes, names):
    out, o = {}, 0
    for n in names:
        k = math.prod(shapes[n])
        out[n] = flat[o:o + k].reshape(shapes[n])
        o += k
    return out


def kernel(x, rel_bias, ln1_g, w_in, b_gate, w_a, pool_w, pool_scale, w_b, ssd_conv_w, ssd_conv_b, ssd_dt_bias, ssd_a_log, ssd_d, ssd_norm_w, w_c, w_o, ln2_g, ffn_w_up, ffn_conv_w, ffn_conv_b, ffn_w_down, final_g, loss_target, m_rel_bias, m_ln1_g, m_w_in, m_b_gate, m_w_a, m_pool_w, m_pool_scale, m_w_b, m_ssd_conv_w, m_ssd_conv_b, m_ssd_dt_bias, m_ssd_a_log, m_ssd_d, m_ssd_norm_w, m_w_c, m_w_o, m_ln2_g, m_ffn_w_up, m_ffn_conv_w, m_ffn_conv_b, m_ffn_w_down, m_final_g, v_rel_bias, v_ln1_g, v_w_in, v_b_gate, v_w_a, v_pool_w, v_pool_scale, v_w_b, v_ssd_conv_w, v_ssd_conv_b, v_ssd_dt_bias, v_ssd_a_log, v_ssd_d, v_ssd_norm_w, v_w_c, v_w_o, v_ln2_g, v_ffn_w_up, v_ffn_conv_w, v_ffn_conv_b, v_ffn_w_down, v_final_g):
    W = dict(rel_bias=rel_bias, ln1_g=ln1_g, w_in=w_in, b_gate=b_gate, w_a=w_a, pool_w=pool_w, pool_scale=pool_scale,
             w_b=w_b, ssd_conv_w=ssd_conv_w, ssd_conv_b=ssd_conv_b, ssd_dt_bias=ssd_dt_bias, ssd_a_log=ssd_a_log,
             ssd_d=ssd_d, ssd_norm_w=ssd_norm_w, w_c=w_c, w_o=w_o, ln2_g=ln2_g, ffn_w_up=ffn_w_up,
             ffn_conv_w=ffn_conv_w, ffn_conv_b=ffn_conv_b, ffn_w_down=ffn_w_down, final_g=final_g)
    M = dict(rel_bias=m_rel_bias, ln1_g=m_ln1_g, w_in=m_w_in, b_gate=m_b_gate, w_a=m_w_a, pool_w=m_pool_w,
             pool_scale=m_pool_scale, w_b=m_w_b, ssd_conv_w=m_ssd_conv_w, ssd_conv_b=m_ssd_conv_b,
             ssd_dt_bias=m_ssd_dt_bias, ssd_a_log=m_ssd_a_log, ssd_d=m_ssd_d, ssd_norm_w=m_ssd_norm_w, w_c=m_w_c,
             w_o=m_w_o, ln2_g=m_ln2_g, ffn_w_up=m_ffn_w_up, ffn_conv_w=m_ffn_conv_w, ffn_conv_b=m_ffn_conv_b,
             ffn_w_down=m_ffn_w_down, final_g=m_final_g)
    V = dict(rel_bias=v_rel_bias, ln1_g=v_ln1_g, w_in=v_w_in, b_gate=v_b_gate, w_a=v_w_a, pool_w=v_pool_w,
             pool_scale=v_pool_scale, w_b=v_w_b, ssd_conv_w=v_ssd_conv_w, ssd_conv_b=v_ssd_conv_b,
             ssd_dt_bias=v_ssd_dt_bias, ssd_a_log=v_ssd_a_log, ssd_d=v_ssd_d, ssd_norm_w=v_ssd_norm_w, w_c=v_w_c,
             w_o=v_w_o, ln2_g=v_ln2_g, ffn_w_up=v_ffn_w_up, ffn_conv_w=v_ffn_conv_w, ffn_conv_b=v_ffn_conv_b,
             ffn_w_down=v_ffn_w_down, final_g=v_final_g)
    nl = ln1_g.shape[0]
    px, py, pc_ = _position()
    chip = 2 * px + py
    cidx = jnp.reshape(pc_, (1,)).astype(jnp.int32)
    chip_idx = jnp.reshape(chip, (1,)).astype(jnp.int32)

    placed = {}
    for n, cs in SHARDED_SMALL.items():
        full = jnp.zeros(W[n].shape[:-1] + (4 * cs,), F32)
        full = lax.dynamic_update_slice(full, W[n], (0, 0, chip * cs))
        placed[n] = jnp.where(pc_ == 0, full, 0.0)
    names_sh = tuple(SHARDED_SMALL)
    shapes_sh = {n: placed[n].shape for n in names_sh}
    got = _all_reduce_small(_to_rows(_flatten(placed, names_sh)), "gather_small")
    small = {n: W[n] for n in SMALL_LAYER}
    small.update(_unflatten(got.reshape(-1), shapes_sh, names_sh))

    packs = _pack_blocks({n: W[n] for n in BIG_NAMES}, BF16)

    half = PACK_PAD // 2
    units = half // 16

    def share(weights, total):
        tot = sum(weights.values())
        return {n: math.ceil(total * v / tot) for n, v in weights.items()}

    gathers = {}

    def gather(i):
        if i not in gathers:
            buf = lax.dynamic_update_slice(lax.empty((4, PACK_PAD, D), BF16), packs[i][None], (chip, 0, 0))
            gathers[i] = _Stream(packs[i], buf, functools.partial(_gather_parts, half), 6, units, "gather_w")
        return gathers[i]

    def layer_full(i):
        return _operands(gather(i).drain())

    fwd_share = share(dict(in_a=63, in_c=31, in_d=44, mm_up=83, mm_down=34), units)

    def fwd_hooks(i):
        if i + 1 >= nl:
            return lambda name: None
        return lambda name: gather(i + 1).hook(fwd_share[name]) if name in fwd_share else None

    exchanges = {}
    bwd_share = share(dict(g_down=35, d_u2_v=60, g_up_a=35, g_up_v=35, d_merged=50, d_u_a=60, d_u_wgate=70,
                           g_in_wgate=36), units)

    class Exchange:
        def __init__(self, g):
            self.g = g
            self.pair = _Stream(g, lax.empty((4, half, D), BF16), functools.partial(_rs_pair_parts, half), 1, units, "rs_pair")
            self.hsum = self.chips = None

        def to_chips(self):
            if self.chips is None:
                self.hsum = _rs_add_pair(self.g, self.pair.drain(), cidx, "rs_add_pair")
                self.chips = _Stream(self.hsum, lax.empty((3, half, D), BF16), _rs_chip_parts, 3, units, "rs_chips")
            return self.chips

    def after_bwd(i, gw):
        exchanges[i] = Exchange(_pack_operands(gw, BF16))

    def bwd_hooks(i):
        if i + 1 >= nl:
            return lambda name: None

        def hk(name):
            if name == "d_f":
                return exchanges[i + 1].pair.hook(units)
            return exchanges[i + 1].to_chips().hook(bwd_share[name]) if name in bwd_share else None

        return hk

    loss, dx, gws, gss, dfinal, drel = _local_step(x[0], loss_target[0], rel_bias, final_g, layer_full, small,
                                                   fwd_hooks, bwd_hooks, after_bwd)

    def reduced(i):
        recv3 = exchanges[i].to_chips().drain()
        r = _rs_add_chips(exchanges[i].hsum, recv3, chip_idx, "rs_add_chips")
        other = _rs_swap(r, "rs_swap")
        both = jnp.concatenate([jnp.where(pc_ == 0, r, other), jnp.where(pc_ == 0, other, r)], axis=0)
        return _unpack_blocks(both)

    red = [reduced(i) for i in range(nl)]
    delta, new_m, new_v, grads = {}, {}, {}, {}
    for n in BIG_NAMES:
        shp = W[n].shape
        r2 = lambda a: a.reshape(-1, shp[-1])
        grads[n] = jnp.stack([red[i][n] for i in range(nl)], axis=0)
        res = _adamw(r2(W[n]), r2(grads[n]), r2(M[n]), r2(V[n]), "adamw_" + n)
        delta[n], new_m[n], new_v[n] = [a.reshape(shp) for a in res]

    sg = {}
    for n in SMALL_LAYER:
        sg[n] = jnp.stack([gss[i][n] for i in range(nl)], axis=0)
    for n in ("ssd_dt_bias", "ssd_a_log", "ssd_d"):
        sg[n] = sg[n][:, 0, :SSD_HEADS]
    sg["rel_bias"] = drel[:, :REL_BUCKETS].T
    sg["final_g"] = dfinal.reshape(D)
    sg["loss"] = loss[0, :1]
    names_sg = tuple(sg)
    shapes_sg = {n: ((nl,) + W[n].shape[1:] if n in SMALL_LAYER and n not in SHARDED_SMALL else
                     (placed[n].shape if n in SHARDED_SMALL else sg[n].shape)) for n in names_sg}
    for n in names_sg:
        sg[n] = sg[n].reshape(shapes_sg[n])
    tot = _all_reduce_small(_to_rows(_flatten(sg, names_sg)), "allreduce_small")
    tot = _unflatten(tot.reshape(-1), shapes_sg, names_sg)
    loss_out = tot.pop("loss").reshape(())
    for n, cs in SHARDED_SMALL.items():
        tot[n] = lax.dynamic_slice(tot[n], (0, 0, chip * cs), tot[n].shape[:-1] + (cs,))
    grads.update(tot)

    names_s = tuple(n for n in WEIGHTS if n not in BIG_NAMES)
    shapes_s = {n: W[n].shape for n in names_s}
    pk = lambda t: _to_rows(_flatten(t, names_s))
    dl, m2, v2 = _adamw(pk(W), pk(grads), pk(M), pk(V), "adamw_small")
    delta.update(_unflatten(dl.reshape(-1), shapes_s, names_s))
    new_m.update(_unflatten(m2.reshape(-1), shapes_s, names_s))
    new_v.update(_unflatten(v2.reshape(-1), shapes_s, names_s))

    return (loss_out, dx[None], *[grads[n] for n in WEIGHTS], *[delta[n] for n in WEIGHTS],
            *[new_m[n] for n in WEIGHTS], *[new_v[n] for n in WEIGHTS])
```
